```python
import math
import jax
import jax.numpy as jnp
from jax import lax
import numpy as np

D_MODEL = 1024
BATCH = 16
SEQ = 256
DEPTH = 2
DEC_BATCH = 8
DEC_SEQ = 1024
PAST_LEN = 512

GRID_W = 64
MIX_WIDTH = D_MODEL
A_HEADS = 4
A_DK = MIX_WIDTH // 16
A_DV = MIX_WIDTH // 16
A_WIDTH = A_HEADS * A_DV
HGRN_CHUNK = 64
B_HEADS = 4
B_DK = MIX_WIDTH // 16
B_DV = 2 * B_DK
B_WIDTH = B_HEADS * B_DV
C_GROUPS = 4
C_CHUNK = 128
C_WIDTH = MIX_WIDTH // 4
C_DG = C_WIDTH // C_GROUPS
IN_WIDTH = 5 * A_WIDTH + 3 * B_WIDTH + 2 * C_WIDTH
N_EXPERTS = 32
TOP_K = 4
D_FF = D_MODEL
SWIGLU_LIMIT = 7.0
SWIGLU_ALPHA = 1.702
MOE_BLOCK = 128
ROPE_BASE = 10000.0
ATTN_Q_BLOCK = 128
EPS = 1e-6

kernel_name = 'hybrid_hgrn2_diffattn_chunkmlp_moe_dit_step'

F32 = jnp.float32


def rms_norm(x, g):
    xf = x.astype(F32)
    y = xf * lax.rsqrt(jnp.mean(xf * xf, axis=-1, keepdims=True) + EPS)
    return (y * g.astype(F32)).astype(x.dtype)


def layer_norm(x, g, b):
    xf = x.astype(F32)
    mu = jnp.mean(xf, axis=-1, keepdims=True)
    var = jnp.mean(jnp.square(xf - mu), axis=-1, keepdims=True)
    return ((xf - mu) * lax.rsqrt(var + EPS) * g.astype(F32) + b.astype(F32)).astype(x.dtype)


def axial_angles(n_tokens):
    n_rows = n_tokens // GRID_W
    row = jnp.repeat(jnp.arange(n_rows), GRID_W).astype(F32)
    col = jnp.tile(jnp.arange(GRID_W), n_rows).astype(F32)
    half = B_DK // 2
    inv_freq = ROPE_BASE ** (-jnp.arange(0, half, 2, dtype=F32) / half)
    return row[:, None] * inv_freq, col[:, None] * inv_freq


def rope_1d(x, ang):
    m = x.shape[-1] // 2
    x1, x2 = x[..., :m], x[..., m:]
    cos = jnp.cos(ang).astype(x.dtype)
    sin = jnp.sin(ang).astype(x.dtype)
    return jnp.concatenate([x1 * cos - x2 * sin, x1 * sin + x2 * cos], axis=-1)


def rope_2d(x, row_ang, col_ang):
    half = x.shape[-1] // 2
    return jnp.concatenate([rope_1d(x[..., :half], row_ang), rope_1d(x[..., half:], col_ang)], axis=-1)


def hgrn_lower_bound(lower_bounds, l):
    sm = jax.nn.softmax(lower_bounds.astype(F32), axis=0)
    lb = jnp.cumsum(sm, axis=0) - sm[0]
    return lb[l]


def hgrn_chunk_scan(q, k, v, log_f, s0):
    bsz, hh, n, _ = q.shape
    nc = n // HGRN_CHUNK

    def to_chunks(t):
        return jnp.moveaxis(t.reshape(bsz, hh, nc, HGRN_CHUNK, t.shape[-1]), 2, 0)

    lower = jnp.tril(jnp.ones((HGRN_CHUNK, HGRN_CHUNK), bool))[:, :, None]

    def step(state, inp):
        qc, kc, vc, gc = inp
        cum = jnp.cumsum(gc, axis=2)
        rel = cum[:, :, :, None, :] - cum[:, :, None, :, :]
        decay = jnp.where(lower, jnp.exp(jnp.where(lower, rel, 0.0)), 0.0)
        scores = jnp.einsum('bhtk,bhsk,bhtsk->bhts', qc, kc, decay)
        o = (jnp.einsum('bhts,bhsv->bhtv', scores, vc)
             + jnp.einsum('bhtk,bhkv->bhtv', qc * jnp.exp(cum), state))
        last = cum[:, :, -1:, :]
        new_state = (jnp.exp(last[:, :, 0, :])[..., None] * state
                     + jnp.einsum('bhsk,bhsv->bhkv', kc * jnp.exp(last - cum), vc))
        return new_state, o

    final, o = lax.scan(step, s0, (to_chunks(q), to_chunks(k), to_chunks(v), to_chunks(log_f)))
    o = jnp.moveaxis(o, 0, 2).reshape(bsz, hh, n, -1)
    return o, final


def hgrn2_mixer(h_q, h_ff, h_fb, h_i, h_g, lb, norm_g, layer, s0):
    bsz, n, _ = h_q.shape

    def heads(t):
        return t.astype(F32).reshape(bsz, n, A_HEADS, -1).transpose(0, 2, 1, 3)

    q = heads(jax.nn.silu(h_q)) * (A_DK ** -0.5)
    v = heads(h_i)
    out = jnp.zeros((bsz, A_HEADS, n, A_DV), F32)
    finals = []
    for d, logits in enumerate((h_ff, h_fb)):
        z = heads(logits)
        if layer == 0:
            log_f = jax.nn.log_sigmoid(z)
            k = jax.nn.sigmoid(-z)
        else:
            lbd = lb[d].reshape(A_HEADS, 1, A_DK)
            log_f = jnp.log(lbd + (1.0 - lbd) * jax.nn.sigmoid(z))
            k = (1.0 - lbd) * jax.nn.sigmoid(-z)
        if d == 0:
            o, fin = hgrn_chunk_scan(q, k, v, log_f, s0[:, d])
        else:
            flip = lambda t: jnp.flip(t, axis=2)
            o, fin = hgrn_chunk_scan(flip(q), flip(k), flip(v), flip(log_f), s0[:, d])
            o = flip(o)
        out = out + o
        finals.append(fin)
    o = rms_norm(out.transpose(0, 2, 1, 3), norm_g)
    o = o * jax.nn.silu(h_g.astype(F32).reshape(bsz, n, A_HEADS, A_DV))
    return o.reshape(bsz, n, A_WIDTH).astype(h_q.dtype), jnp.stack(finals, axis=1)


def diff_attention(q, k, v, lam):
    bsz, _, hh, nq, dk = q.shape
    nb = nq // ATTN_Q_BLOCK
    qb = jnp.moveaxis(q.reshape(bsz, 2, hh, nb, ATTN_Q_BLOCK, dk), 3, 0)
    vf = v.astype(F32)
    scale = dk ** -0.5

    def block(qi):
        s = jnp.einsum('bmhqd,bmhkd->bmhqk', qi, k).astype(F32) * scale
        p = jax.nn.softmax(s, axis=-1)
        a = p[:, 0] - lam * p[:, 1]
        return jnp.einsum('bhqk,bhkv->bhqv', a, vf)

    o = lax.map(block, qb)
    return jnp.moveaxis(o, 0, 2).reshape(bsz, hh, nq, -1)


def chunk_mlp(h_u, h_v, ln_g, ln_b, w_s, b_s):
    bsz, n, _ = h_u.shape
    u = jax.nn.gelu(h_u, approximate=False)
    v = layer_norm(jax.nn.gelu(h_v, approximate=False), ln_g, ln_b)
    vc = v.reshape(bsz, n // C_CHUNK, C_CHUNK, C_GROUPS, C_DG)
    mixed = jnp.einsum('gts,bcsgd->bctgd', w_s, vc) + b_s.T[:, :, None]
    return u * mixed.reshape(bsz, n, C_WIDTH)


def moe_ffn(h, w_r, b_r, w_gu, b_gu, w_dn, b_dn):
    shape = h.shape
    xt = h.reshape(-1, D_MODEL)
    n_tok = xt.shape[0]
    logits = (xt @ w_r + b_r).astype(F32)
    top_val, top_idx = lax.top_k(logits, TOP_K)
    gate_w = jax.nn.softmax(top_val, axis=-1)
    flat_e = top_idx.reshape(-1)
    n_assign = n_tok * TOP_K
    order = jnp.argsort(flat_e)
    sorted_e = flat_e[order]
    sorted_tok = order // TOP_K
    counts = jnp.bincount(flat_e, length=N_EXPERTS)
    padded = (counts + MOE_BLOCK - 1) // MOE_BLOCK * MOE_BLOCK
    pad_end = jnp.cumsum(padded)
    pad_start = pad_end - padded
    raw_start = jnp.cumsum(counts) - counts
    dest = pad_start[sorted_e] + jnp.arange(n_assign) - raw_start[sorted_e]
    n_rows = (n_assign + MOE_BLOCK - 1) // MOE_BLOCK * MOE_BLOCK + N_EXPERTS * MOE_BLOCK
    n_blocks = n_rows // MOE_BLOCK
    row_tok = jnp.zeros((n_rows,), jnp.int32).at[dest].set(sorted_tok.astype(jnp.int32))
    block_e = jnp.minimum(
        jnp.searchsorted(pad_end, jnp.arange(n_blocks) * MOE_BLOCK, side='right'), N_EXPERTS - 1)
    xb = xt[row_tok].reshape(n_blocks, MOE_BLOCK, D_MODEL)

    def expert_block(args):
        xi, e = args
        gu = xi @ w_gu[e] + b_gu[e]
        glu, lin = jnp.split(gu, 2, axis=-1)
        glu = jnp.minimum(glu, SWIGLU_LIMIT)
        lin = jnp.clip(lin, -SWIGLU_LIMIT, SWIGLU_LIMIT)
        act = glu * jax.nn.sigmoid(SWIGLU_ALPHA * glu) * (lin + 1.0)
        return act @ w_dn[e] + b_dn[e]

    yb = lax.map(expert_block, (xb, block_e)).reshape(n_rows, D_MODEL)
    y_assign = yb[dest] * gate_w.reshape(-1)[order][:, None].astype(yb.dtype)
    y = jax.ops.segment_sum(y_assign, sorted_tok, num_segments=n_tok)
    return y.reshape(shape).astype(h.dtype)


def trunk_layer(x, mod, l, W, ctx=None, grid=None):
    bsz, n, _ = x.shape
    shift1, scale1, gate1, shift2, scale2, gate2 = jnp.split(mod, 6, axis=-1)
    h = rms_norm(x, W['norm_mix_g'][l]) * (1.0 + scale1) + shift1
    proj = h @ W['w_in'][l]
    widths = [A_WIDTH] * 5 + [B_WIDTH] * 3 + [C_WIDTH] * 2
    a_q, a_ff, a_fb, a_i, a_g, b_q, b_k, b_v, c_u, c_v = jnp.split(
        proj, np.cumsum(widths)[:-1].tolist(), axis=-1)

    if ctx is None:
        s0 = jnp.zeros((bsz, 2, A_HEADS, A_DK, A_DV), F32)
    else:
        s0 = ctx[2].astype(F32)
    a_out, a_state = hgrn2_mixer(a_q, a_ff, a_fb, a_i, a_g,
                                 hgrn_lower_bound(W['hgrn_lower_bounds'], l),
                                 W['hgrn_norm_g'][l], l, s0)

    q = rms_norm(b_q.reshape(bsz, n, B_HEADS, 2, B_DK).transpose(0, 3, 2, 1, 4), W['diff_q_norm_g'][l])
    k = rms_norm(b_k.reshape(bsz, n, B_HEADS, 2, B_DK).transpose(0, 3, 2, 1, 4), W['diff_k_norm_g'][l])
    v = b_v.reshape(bsz, n, B_HEADS, B_DV).transpose(0, 2, 1, 3)
    lam_init = 0.8 - 0.6 * math.exp(-0.3 * l)
    lam = (jnp.exp(jnp.sum(W['lq1'][l].astype(F32) * W['lk1'][l].astype(F32)))
           - jnp.exp(jnp.sum(W['lq2'][l].astype(F32) * W['lk2'][l].astype(F32))) + lam_init)
    if ctx is None:
        keys, vals = k, v
    else:
        row_ang, col_ang = grid
        q = rope_2d(q, row_ang, col_ang)
        k = rope_2d(k, row_ang, col_ang)
        keys = jnp.concatenate([k, ctx[0].astype(k.dtype)], axis=3)
        vals = jnp.concatenate([v, ctx[1].astype(v.dtype)], axis=2)
    attn = diff_attention(q, keys, vals, lam)
    attn = rms_norm(attn.transpose(0, 2, 1, 3), W['diff_subln_g'][l]) * (1.0 - lam_init)
    b_out = attn.reshape(bsz, n, B_WIDTH).astype(x.dtype)

    c_out = chunk_mlp(c_u, c_v, W['cmlp_ln_g'][l], W['cmlp_ln_b'][l],
                      W['cmlp_w_s'][l], W['cmlp_b_s'][l]).astype(x.dtype)

    mixed = jnp.concatenate([a_out, b_out, c_out], axis=-1) @ W['w_out'][l]
    x = x + gate1 * mixed
    h2 = rms_norm(x, W['norm_ffn_g'][l]) * (1.0 + scale2) + shift2
    x = x + gate2 * moe_ffn(h2, W['router_w'][l], W['router_b'][l], W['moe_w_gate_up'][l],
                            W['moe_b_gate_up'][l], W['moe_w_down'][l], W['moe_b_down'][l])
    if ctx is None:
        return x, (k, v, a_state)
    return x, None


def setup_inputs(seed: int = 0) -> dict:
    key = jax.random.key(seed)
    ks = iter(jax.random.split(key, 40))

    def nrm(shape, scale=1.0):
        return scale * jax.random.normal(next(ks), shape, F32)

    return {
        'x_prompt': nrm((BATCH, SEQ, D_MODEL)),
        'x_sample': nrm((DEC_BATCH, DEC_SEQ, D_MODEL)),
        'c': nrm((DEC_BATCH, D_MODEL)),
        'cache_diff_k': nrm((DEC_BATCH, DEPTH, 2, B_HEADS, PAST_LEN, B_DK)),
        'cache_diff_v': nrm((DEC_BATCH, DEPTH, B_HEADS, PAST_LEN, B_DV)),
        'state_hgrn': nrm((DEC_BATCH, DEPTH, 2, A_HEADS, A_DK, A_DV), 0.5),
        'c_ctx': nrm((D_MODEL,)),
        'norm_mix_g': 1.0 + nrm((DEPTH, D_MODEL), 0.1),
        'norm_ffn_g': 1.0 + nrm((DEPTH, D_MODEL), 0.1),
        'w_mod': nrm((DEPTH, D_MODEL, 6 * D_MODEL), 0.5 * D_MODEL ** -0.5),
        'b_mod': nrm((DEPTH, 6 * D_MODEL), 0.1),
        'w_in': nrm((DEPTH, D_MODEL, IN_WIDTH), D_MODEL ** -0.5),
        'w_out': nrm((DEPTH, MIX_WIDTH, D_MODEL), MIX_WIDTH ** -0.5),
        'hgrn_lower_bounds': nrm((DEPTH, 2, A_WIDTH)),
        'hgrn_norm_g': 1.0 + nrm((DEPTH, A_DV), 0.1),
        'diff_q_norm_g': 1.0 + nrm((DEPTH, B_DK), 0.1),
        'diff_k_norm_g': 1.0 + nrm((DEPTH, B_DK), 0.1),
        'diff_lambda_q1': nrm((DEPTH, B_DK), 0.1),
        'diff_lambda_k1': nrm((DEPTH, B_DK), 0.1),
        'diff_lambda_q2': nrm((DEPTH, B_DK), 0.1),
        'diff_lambda_k2': nrm((DEPTH, B_DK), 0.1),
        'diff_subln_g': 1.0 + nrm((DEPTH, B_DV), 0.1),
        'cmlp_ln_g': 1.0 + nrm((DEPTH, C_WIDTH), 0.1),
        'cmlp_ln_b': nrm((DEPTH, C_WIDTH), 0.1),
        'cmlp_w_s': nrm((DEPTH, C_GROUPS, C_CHUNK, C_CHUNK), C_CHUNK ** -0.5),
        'cmlp_b_s': 1.0 + nrm((DEPTH, C_GROUPS, C_CHUNK), 0.1),
        'router_w': nrm((DEPTH, D_MODEL, N_EXPERTS), D_MODEL ** -0.5),
        'router_b': nrm((DEPTH, N_EXPERTS), 0.01),
        'moe_w_gate_up': nrm((DEPTH, N_EXPERTS, D_MODEL, 2 * D_FF), D_MODEL ** -0.5),
        'moe_b_gate_up': nrm((DEPTH, N_EXPERTS, 2 * D_FF), 0.01),
        'moe_w_down': nrm((DEPTH, N_EXPERTS, D_FF, D_MODEL), D_FF ** -0.5),
        'moe_b_down': nrm((DEPTH, N_EXPERTS, D_MODEL), 0.01),
    }


def reference(x_prompt, x_sample, c, cache_diff_k, cache_diff_v, state_hgrn, c_ctx,
              norm_mix_g, norm_ffn_g, w_mod, b_mod, w_in, w_out,
              hgrn_lower_bounds, hgrn_norm_g,
              diff_q_norm_g, diff_k_norm_g, diff_lambda_q1, diff_lambda_k1,
              diff_lambda_q2, diff_lambda_k2, diff_subln_g,
              cmlp_ln_g, cmlp_ln_b, cmlp_w_s, cmlp_b_s,
              router_w, router_b, moe_w_gate_up, moe_b_gate_up, moe_w_down, moe_b_down):
    W = {
        'norm_mix_g': norm_mix_g, 'norm_ffn_g': norm_ffn_g, 'w_in': w_in, 'w_out': w_out,
        'hgrn_lower_bounds': hgrn_lower_bounds, 'hgrn_norm_g': hgrn_norm_g,
        'diff_q_norm_g': diff_q_norm_g, 'diff_k_norm_g': diff_k_norm_g,
        'lq1': diff_lambda_q1, 'lk1': diff_lambda_k1, 'lq2': diff_lambda_q2, 'lk2': diff_lambda_k2,
        'diff_subln_g': diff_subln_g,
        'cmlp_ln_g': cmlp_ln_g, 'cmlp_ln_b': cmlp_ln_b, 'cmlp_w_s': cmlp_w_s, 'cmlp_b_s': cmlp_b_s,
        'router_w': router_w, 'router_b': router_b, 'moe_w_gate_up': moe_w_gate_up,
        'moe_b_gate_up': moe_b_gate_up, 'moe_w_down': moe_w_down, 'moe_b_down': moe_b_down,
    }

    y_prompt = x_prompt
    ks, vs, ss = [], [], []
    for l in range(DEPTH):
        mod = (jax.nn.silu(c_ctx) @ w_mod[l] + b_mod[l])[None, None, :]
        y_prompt, (k_l, v_l, s_l) = trunk_layer(y_prompt, mod, l, W)
        ks.append(k_l)
        vs.append(v_l)
        ss.append(s_l)
    new_cache_diff_k = jnp.stack(ks, axis=1).astype(x_prompt.dtype)
    new_cache_diff_v = jnp.stack(vs, axis=1).astype(x_prompt.dtype)
    new_state_hgrn = jnp.stack(ss, axis=1).astype(x_prompt.dtype)

    grid = axial_angles(x_sample.shape[1])
    y_sample = x_sample
    for l in range(DEPTH):
        mod = (jax.nn.silu(c) @ w_mod[l] + b_mod[l])[:, None, :]
        ctx = (cache_diff_k[:, l], cache_diff_v[:, l], state_hgrn[:, l])
        y_sample, _ = trunk_layer(y_sample, mod, l, W, ctx=ctx, grid=grid)

    return (y_prompt, y_sample, new_cache_diff_k, new_cache_diff_v, new_state_hgrn)
```

```python
import functools
import math

import numpy as np
import jax
import jax.numpy as jnp
from jax import lax
from jax.experimental import pallas as pl
from jax.experimental.pallas import tpu as pltpu

F32 = jnp.float32
BF16 = jnp.bfloat16

D = 1024
DEPTH = 2
BATCH, SEQ = 16, 256
DEC_BATCH, DEC_SEQ = 8, 1024
PAST = 512
GRID_W = 64
A_HEADS, A_DK = 4, 64
A_WIDTH = 256
B_HEADS, B_DK, B_DV = 4, 64, 128
B_WIDTH = 512
C_GROUPS, C_CHUNK, C_WIDTH, C_DG = 4, 128, 256, 64
IN_WIDTH = 5 * A_WIDTH + 3 * B_WIDTH + 2 * C_WIDTH
N_EXPERTS, TOP_K = 32, 4
SWIGLU_LIMIT, SWIGLU_ALPHA = 7.0, 1.702
ROPE_BASE = 10000.0
EPS = 1e-6

T_CTX = BATCH * SEQ
T_SMP = DEC_BATCH * DEC_SEQ
T = T_CTX + T_SMP
N_SEQ = BATCH + DEC_BATCH
MOD_ROWS = 16

TM = 256
N_TILES = T // TM
CTX_TILES = T_CTX // TM
SMP_TILES_PER_SEQ = DEC_SEQ // TM
LANES = 128
MOE_BM = 256
MOE_ROWS = T * TOP_K + N_EXPERTS * MOE_BM
MOE_BLOCKS = MOE_ROWS // MOE_BM
VMEM_LIMIT = 56 * 1024 * 1024


def _cparams(sem):
    return pltpu.CompilerParams(dimension_semantics=sem, vmem_limit_bytes=VMEM_LIMIT)


def _mod_row(i):
    return jnp.where(i < CTX_TILES, 0, 1 + (i - CTX_TILES) // SMP_TILES_PER_SEQ)


def _split3(x):
    hi = x.astype(BF16)
    r = x - hi.astype(F32)
    mid = r.astype(BF16)
    lo = (r - mid.astype(F32)).astype(BF16)
    return hi, mid, lo


def _sel_dot(sel, x):
    hi, mid, lo = _split3(x)
    acc = jnp.dot(sel, lo, preferred_element_type=F32)
    acc = acc + jnp.dot(sel, mid, preferred_element_type=F32)
    return acc + jnp.dot(sel, hi, preferred_element_type=F32)


def _dot_sel(x, sel):
    hi, mid, lo = _split3(x)
    acc = jnp.dot(lo, sel, preferred_element_type=F32)
    acc = acc + jnp.dot(mid, sel, preferred_element_type=F32)
    return acc + jnp.dot(hi, sel, preferred_element_type=F32)


def _dot_nt(a, b):
    return lax.dot_general(a, b, (((1,), (1,)), ((), ())), preferred_element_type=F32)


def _dot_tn(a, b):
    return lax.dot_general(a, b, (((0,), (0,)), ((), ())), preferred_element_type=F32)


def _lane(shape):
    return lax.broadcasted_iota(jnp.int32, shape, len(shape) - 1)


def _mod_kernel(c_ref, w_ref, b_ref, o_ref):
    c = c_ref[...]
    s = c * jax.nn.sigmoid(c)
    o_ref[0] = jnp.dot(s.astype(BF16), w_ref[0].astype(BF16), preferred_element_type=F32) + b_ref[0]


def _modulation(cvec, w_mod, b_mod):
    tn = 1536
    return pl.pallas_call(
        _mod_kernel,
        grid=(DEPTH, 6 * D // tn),
        in_specs=[
            pl.BlockSpec((MOD_ROWS, D), lambda l, j: (0, 0)),
            pl.BlockSpec((1, D, tn), lambda l, j: (l, 0, j)),
            pl.BlockSpec((1, 1, tn), lambda l, j: (l, 0, j)),
        ],
        out_specs=pl.BlockSpec((1, MOD_ROWS, tn), lambda l, j: (l, 0, j)),
        out_shape=jax.ShapeDtypeStruct((DEPTH, MOD_ROWS, 6 * D), F32),
        compiler_params=_cparams(("arbitrary", "arbitrary")),
        name="modulation",
    )(cvec, w_mod, b_mod.reshape(DEPTH, 1, 6 * D))


def _inproj_kernel(x_ref, g_ref, shift_ref, scale_ref, w_ref, o_ref):
    x = x_ref[...]
    y = x * lax.rsqrt(jnp.mean(x * x, axis=-1, keepdims=True) + EPS) * g_ref[...]
    h = y * (1.0 + scale_ref[0]) + shift_ref[0]
    o_ref[...] = jnp.dot(h.astype(BF16), w_ref[...], preferred_element_type=F32)


def _in_projection(x, g, mod3, w_in_bf):
    return pl.pallas_call(
        _inproj_kernel,
        grid=(N_TILES,),
        in_specs=[
            pl.BlockSpec((TM, D), lambda i: (i, 0)),
            pl.BlockSpec((1, D), lambda i: (0, 0)),
            pl.BlockSpec((1, 1, D), lambda i: (_mod_row(i), 0, 0)),
            pl.BlockSpec((1, 1, D), lambda i: (_mod_row(i), 0, 1)),
            pl.BlockSpec((D, IN_WIDTH), lambda i: (0, 0)),
        ],
        out_specs=pl.BlockSpec((TM, IN_WIDTH), lambda i: (i, 0)),
        out_shape=jax.ShapeDtypeStruct((T, IN_WIDTH), F32),
        compiler_params=_cparams(("arbitrary",)),
        name="in_projection",
    )(x, g.reshape(1, D), mod3, mod3, w_in_bf)


HG_C = TM
HG_LEVELS = tuple(2 ** j for j in range(1, int(math.log2(HG_C)) + 1))


def _hgrn_tables():
    t = np.arange(HG_C)[:, None]
    s = np.arange(HG_C)[None, :]
    x = t ^ s
    lvl = np.zeros((HG_C, HG_C), np.int32)
    nz = x > 0
    lvl[nz] = np.floor(np.log2(x[nz])).astype(np.int32) + 1
    fwd = np.where(t >= s, lvl, -1).astype(np.int32)
    bwd = np.where(t <= s, lvl, -1).astype(np.int32)
    tri_f = (t >= s).astype(np.float32)
    tri_b = (t <= s).astype(np.float32)
    return np.stack([fwd, bwd]), np.stack([tri_f, tri_b])


def _block_ref(cum, m, idx):
    c, l = cum.shape
    if m >= 16:
        c3 = cum.reshape(c // m, m, l)
        r = c3[:, idx:idx + 1, :]
        return jnp.broadcast_to(r, (c // m, m, l)).reshape(c, l)
    c3 = cum.reshape(c // 8, 8, l)
    sub = lax.broadcasted_iota(jnp.int32, c3.shape, 1)
    out = None
    for j in range(8 // m - 1, -1, -1):
        cand = jnp.broadcast_to(c3[:, j * m + idx:j * m + idx + 1, :], c3.shape)
        out = cand if out is None else jnp.where(sub < (j + 1) * m, cand, out)
    return out.reshape(c, l)


def _hgrn_kernel(q_ref, z_ref, v_ref, lb_ref, s0_ref, lvl_ref, tri_ref, o_ref, fin_ref, st_ref, *, layer, rev):
    g = pl.program_id(0)
    first = jnp.logical_or(g < CTX_TILES, (g - CTX_TILES) % SMP_TILES_PER_SEQ == 0)

    @pl.when(first)
    def _():
        st_ref[...] = s0_ref[0]

    qr = q_ref[...]
    q = qr * jax.nn.sigmoid(qr) * (A_DK ** -0.5)
    z = z_ref[...]
    if layer == 0:
        lf = jnp.minimum(z, 0.0) - jnp.log(1.0 + jnp.exp(-jnp.abs(z)))
        k = jax.nn.sigmoid(-z)
    else:
        lbd = lb_ref[...]
        lf = jnp.log(lbd + (1.0 - lbd) * jax.nn.sigmoid(z))
        k = (1.0 - lbd) * jax.nn.sigmoid(-z)
    v = v_ref[...]
    cum = _sel_dot(tri_ref[0], lf)
    lvl = lvl_ref[0]
    last_row = 0 if rev else HG_C - 1

    for p in range(2):
        sl = slice(p * LANES, (p + 1) * LANES)
        q_p, k_p, v_p, cum_p = q[:, sl], k[:, sl], v[:, sl], cum[:, sl]
        lane = _lane((HG_C, LANES))
        head_masks = (lane < A_DK, lane >= A_DK)
        v_bf = v_p.astype(BF16)
        k_bf = k_p.astype(BF16)
        scores = []
        for hm in head_masks:
            qm = jnp.where(hm, q_p, 0.0).astype(BF16)
            scores.append(jnp.where(lvl == 0, _dot_nt(qm, k_bf), 0.0))
        for li, m in enumerate(HG_LEVELS):
            ref = _block_ref(cum_p, m, m // 2 if rev else m // 2 - 1)
            qd = q_p * jnp.exp(jnp.minimum(cum_p - ref, 0.0))
            kd = (k_p * jnp.exp(jnp.minimum(ref - cum_p, 0.0))).astype(BF16)
            for hi, hm in enumerate(head_masks):
                qm = jnp.where(hm, qd, 0.0).astype(BF16)
                scores[hi] = jnp.where(lvl == li + 1, _dot_nt(qm, kd), scores[hi])
        st = st_ref[p]
        o_intra = jnp.where(head_masks[0],
                            jnp.dot(scores[0].astype(BF16), v_bf, preferred_element_type=F32),
                            jnp.dot(scores[1].astype(BF16), v_bf, preferred_element_type=F32))
        q0 = (q_p * jnp.exp(cum_p)).astype(BF16)
        o_ref[:, sl] = o_intra + _dot_nt(q0, st.astype(BF16))
        last = cum_p[last_row:last_row + 1, :]
        ks = (k_p * jnp.exp(last - cum_p)).astype(BF16)
        upd = _dot_tn(v_bf, ks)
        r = lax.broadcasted_iota(jnp.int32, (LANES, LANES), 0)
        cl = lax.broadcasted_iota(jnp.int32, (LANES, LANES), 1)
        same_head = (r < A_DK) == (cl < A_DK)
        st_new = st * jnp.exp(last) + jnp.where(same_head, upd, 0.0)
        st_ref[p] = st_new
        fin_ref[0, p] = st_new


def _hgrn_seq(g):
    return jnp.where(g < CTX_TILES, g, CTX_TILES + (g - CTX_TILES) // SMP_TILES_PER_SEQ)


def _hgrn_blk(g, rev):
    if not rev:
        return g
    j = g - CTX_TILES
    return jnp.where(g < CTX_TILES, g,
                     CTX_TILES + (j // SMP_TILES_PER_SEQ) * SMP_TILES_PER_SEQ
                     + (SMP_TILES_PER_SEQ - 1 - j % SMP_TILES_PER_SEQ))


def _hgrn_scan(proj, lb_dir, s0_dir, lvl, tri, layer, rev):
    d = 1 if rev else 0
    blk = functools.partial(_hgrn_blk, rev=rev)
    return pl.pallas_call(
        functools.partial(_hgrn_kernel, layer=layer, rev=rev),
        grid=(N_TILES,),
        in_specs=[
            pl.BlockSpec((HG_C, A_WIDTH), lambda g: (blk(g), 0)),
            pl.BlockSpec((HG_C, A_WIDTH), lambda g: (blk(g), 1 + d)),
            pl.BlockSpec((HG_C, A_WIDTH), lambda g: (blk(g), 3)),
            pl.BlockSpec((1, A_WIDTH), lambda g: (0, 0)),
            pl.BlockSpec((1, 2, LANES, LANES), lambda g: (_hgrn_seq(g), 0, 0, 0)),
            pl.BlockSpec((1, HG_C, HG_C), lambda g: (d, 0, 0)),
            pl.BlockSpec((1, HG_C, HG_C), lambda g: (d, 0, 0)),
        ],
        out_specs=[
            pl.BlockSpec((HG_C, A_WIDTH), lambda g: (blk(g), 0)),
            pl.BlockSpec((1, 2, LANES, LANES), lambda g: (_hgrn_seq(g), 0, 0, 0)),
        ],
        out_shape=[
            jax.ShapeDtypeStruct((T, A_WIDTH), F32),
            jax.ShapeDtypeStruct((N_SEQ, 2, LANES, LANES), F32),
        ],
        scratch_shapes=[pltpu.VMEM((2, LANES, LANES), F32)],
        compiler_params=_cparams(("arbitrary",)),
        name=f"hgrn_scan_{'bwd' if rev else 'fwd'}",
    )(proj, proj, proj, lb_dir, s0_dir, lvl, tri)


def _pack_state(s):
    n = s.shape[0]
    st = jnp.swapaxes(s, -1, -2).reshape(n, 2, 2, A_DK, A_DK)
    z = jnp.zeros_like(st[:, :, 0])
    top = jnp.concatenate([st[:, :, 0], z], axis=-1)
    bot = jnp.concatenate([z, st[:, :, 1]], axis=-1)
    return jnp.concatenate([top, bot], axis=-2)


def _unpack_state(sp):
    n = sp.shape[0]
    h0 = sp[:, :, :A_DK, :A_DK]
    h1 = sp[:, :, A_DK:, A_DK:]
    st = jnp.stack([h0, h1], axis=2).reshape(n, A_HEADS, A_DK, A_DK)
    return jnp.swapaxes(st, -1, -2)


def _half_rms(x, g):
    lane = _lane(x.shape)
    lo = lane < B_DK
    xx = x * x
    ms0 = jnp.sum(jnp.where(lo, xx, 0.0), axis=-1, keepdims=True) * (1.0 / B_DK)
    ms1 = jnp.sum(jnp.where(lo, 0.0, xx), axis=-1, keepdims=True) * (1.0 / B_DK)
    inv = jnp.where(lo, lax.rsqrt(ms0 + EPS), lax.rsqrt(ms1 + EPS))
    return x * inv * g


def _rope(x, cos, sin_signed):
    lane = _lane(x.shape)
    first = (lane % 32) < 16
    rot = jnp.where(first, pltpu.roll(x, LANES - 16, 1), pltpu.roll(x, 16, 1))
    return x * cos + rot * sin_signed


def _diff_softmax_pv(q_bf, keys_bf, vals_bf, lam):
    lane = _lane(q_bf.shape)
    zero = jnp.zeros_like(q_bf)
    acc = None
    parts = []
    for mp in range(2):
        qm = jnp.where((lane < B_DK) == (mp == 0), q_bf, zero)
        s = [_dot_nt(qm, kk) for kk in keys_bf]
        mx = functools.reduce(jnp.maximum, [jnp.max(si, axis=-1, keepdims=True) for si in s])
        e = [jnp.exp(si - mx) for si in s]
        den = functools.reduce(lambda a, b: a + b, [jnp.sum(ei, axis=-1, keepdims=True) for ei in e])
        parts.append((e, 1.0 / den))
    (e0, r0), (e1, r1) = parts
    r1 = r1 * lam
    for i in range(len(keys_bf)):
        a = (e0[i] * r0 - e1[i] * r1).astype(BF16)
        pv = jnp.dot(a, vals_bf[i], preferred_element_type=F32)
        acc = pv if acc is None else acc + pv
    return acc


def _subln(o, g, lam_init):
    return o * lax.rsqrt(jnp.mean(o * o, axis=-1, keepdims=True) + EPS) * g * (1.0 - lam_init)


def _attn_ctx_kernel(lam_ref, q_ref, k_ref, v_ref, gq_ref, gk_ref, gs_ref, o_ref, nk_ref, nv_ref, *, lam_init):
    lam = lam_ref[0, 0]
    qn = _half_rms(q_ref[...], gq_ref[...]) * (B_DK ** -0.5)
    kn = _half_rms(k_ref[...], gk_ref[...])
    v = v_ref[...]
    nk_ref[0, 0, 0] = kn[:, :B_DK]
    nk_ref[0, 1, 0] = kn[:, B_DK:]
    nv_ref[0, 0] = v
    o = _diff_softmax_pv(qn.astype(BF16), [kn.astype(BF16)], [v.astype(BF16)], lam)
    o_ref[...] = _subln(o, gs_ref[...], lam_init)


def _attn_ctx(proj, lam, gq2, gk2, gs, lam_init):
    qcol, kcol, vcol = 5 * A_WIDTH // LANES, 5 * A_WIDTH // LANES + 4, 5 * A_WIDTH // LANES + 8
    return pl.pallas_call(
        functools.partial(_attn_ctx_kernel, lam_init=lam_init),
        grid=(BATCH, B_HEADS),
        in_specs=[
            pl.BlockSpec(memory_space=pltpu.SMEM),
            pl.BlockSpec((SEQ, LANES), lambda b, h: (b, qcol + h)),
            pl.BlockSpec((SEQ, LANES), lambda b, h: (b, kcol + h)),
            pl.BlockSpec((SEQ, LANES), lambda b, h: (b, vcol + h)),
            pl.BlockSpec((1, LANES), lambda b, h: (0, 0)),
            pl.BlockSpec((1, LANES), lambda b, h: (0, 0)),
            pl.BlockSpec((1, LANES), lambda b, h: (0, 0)),
        ],
        out_specs=[
            pl.BlockSpec((SEQ, LANES), lambda b, h: (b, h)),
            pl.BlockSpec((1, 2, 1, SEQ, B_DK), lambda b, h: (b, 0, h, 0, 0)),
            pl.BlockSpec((1, 1, SEQ, B_DV), lambda b, h: (b, h, 0, 0)),
        ],
        out_shape=[
            jax.ShapeDtypeStruct((T_CTX, B_WIDTH), F32),
            jax.ShapeDtypeStruct((BATCH, 2, B_HEADS, SEQ, B_DK), F32),
            jax.ShapeDtypeStruct((BATCH, B_HEADS, SEQ, B_DV), F32),
        ],
        compiler_params=_cparams(("arbitrary", "arbitrary")),
        name="diff_attention_ctx",
    )(lam, proj, proj, proj, gq2, gk2, gs)


ATT_TQ = 256


def _attn_smp_kernel(lam_ref, q_ref, k_ref, v_ref, ck_ref, cv_ref, cos_ref, sin_ref, gq_ref, gk_ref, gs_ref,
                     o_ref, qs_ref, ks_ref, *, lam_init):
    lam = lam_ref[0, 0]
    cos = cos_ref[...]
    sin = sin_ref[...]
    qn = _rope(_half_rms(q_ref[...], gq_ref[...]), cos, sin) * (B_DK ** -0.5)
    qs_ref[...] = qn.astype(BF16)
    ks_ref[...] = _rope(_half_rms(k_ref[...], gk_ref[...]), cos, sin).astype(BF16)
    ck = jnp.concatenate([ck_ref[0, 0, 0, 0], ck_ref[0, 0, 1, 0]], axis=-1).astype(BF16)
    cv = cv_ref[0, 0, 0].astype(BF16)
    v_bf = v_ref[...].astype(BF16)
    k_bf = ks_ref[...]
    g = gs_ref[...]

    def body(i, carry):
        r0 = pl.multiple_of(i * ATT_TQ, ATT_TQ)
        q_bf = qs_ref[pl.ds(r0, ATT_TQ), :]
        o = _diff_softmax_pv(q_bf, [k_bf, ck], [v_bf, cv], lam)
        o_ref[pl.ds(r0, ATT_TQ), :] = _subln(o, g, lam_init)
        return carry

    lax.fori_loop(0, DEC_SEQ // ATT_TQ, body, 0)


def _attn_smp(proj, lam, cache_k, cache_v, cos, sin, gq2, gk2, gs, layer, lam_init):
    qcol, kcol, vcol = 5 * A_WIDTH // LANES, 5 * A_WIDTH // LANES + 4, 5 * A_WIDTH // LANES + 8
    r0 = T_CTX // DEC_SEQ
    return pl.pallas_call(
        functools.partial(_attn_smp_kernel, lam_init=lam_init),
        grid=(DEC_BATCH, B_HEADS),
        in_specs=[
            pl.BlockSpec(memory_space=pltpu.SMEM),
            pl.BlockSpec((DEC_SEQ, LANES), lambda b, h: (r0 + b, qcol + h)),
            pl.BlockSpec((DEC_SEQ, LANES), lambda b, h: (r0 + b, kcol + h)),
            pl.BlockSpec((DEC_SEQ, LANES), lambda b, h: (r0 + b, vcol + h)),
            pl.BlockSpec((1, 1, 2, 1, PAST, B_DK), lambda b, h: (b, layer, 0, h, 0, 0)),
            pl.BlockSpec((1, 1, 1, PAST, B_DV), lambda b, h: (b, layer, h, 0, 0)),
            pl.BlockSpec((DEC_SEQ, LANES), lambda b, h: (0, 0)),
            pl.BlockSpec((DEC_SEQ, LANES), lambda b, h: (0, 0)),
            pl.BlockSpec((1, LANES), lambda b, h: (0, 0)),
            pl.BlockSpec((1, LANES), lambda b, h: (0, 0)),
            pl.BlockSpec((1, LANES), lambda b, h: (0, 0)),
        ],
        out_specs=pl.BlockSpec((DEC_SEQ, LANES), lambda b, h: (b, h)),
        out_shape=jax.ShapeDtypeStruct((T_SMP, B_WIDTH), F32),
        scratch_shapes=[pltpu.VMEM((DEC_SEQ, LANES), BF16), pltpu.VMEM((DEC_SEQ, LANES), BF16)],
        compiler_params=_cparams(("arbitrary", "arbitrary")),
        name="diff_attention_smp",
    )(lam, proj, proj, proj, cache_k, cache_v, cos, sin, gq2, gk2, gs)


def _rope_tables():
    n_rows = DEC_SEQ // GRID_W
    row = np.repeat(np.arange(n_rows), GRID_W).astype(np.float32)
    col = np.tile(np.arange(GRID_W), n_rows).astype(np.float32)
    half = B_DK // 2
    inv_freq = (ROPE_BASE ** (-jnp.arange(0, half, 2, dtype=F32) / half))
    row_ang = jnp.asarray(row)[:, None] * inv_freq
    col_ang = jnp.asarray(col)[:, None] * inv_freq
    ang = jnp.concatenate([row_ang, row_ang, col_ang, col_ang], axis=-1)
    ang = jnp.concatenate([ang, ang], axis=-1)
    sign = np.where((np.arange(LANES) % 32) < 16, -1.0, 1.0).astype(np.float32)
    return jnp.cos(ang), jnp.sin(ang) * sign


CM_ROWS = 512


def _gelu(x):
    return 0.5 * x * (1.0 + lax.erf(x * (2.0 ** -0.5)))


def _cmlp_kernel(u_ref, v_ref, g_ref, b_ref, ws_ref, bs_ref, o_ref):
    u = _gelu(u_ref[...])
    gv = _gelu(v_ref[...])
    mu = jnp.mean(gv, axis=-1, keepdims=True)
    dv = gv - mu
    var = jnp.mean(dv * dv, axis=-1, keepdims=True)
    vn = (dv * lax.rsqrt(var + EPS) * g_ref[...] + b_ref[...]).astype(BF16)
    lane = _lane((C_CHUNK, LANES))
    for c in range(CM_ROWS // C_CHUNK):
        rs = slice(c * C_CHUNK, (c + 1) * C_CHUNK)
        for p in range(2):
            cs = slice(p * LANES, (p + 1) * LANES)
            vp = vn[rs, cs]
            m0 = jnp.dot(ws_ref[2 * p].astype(BF16), vp, preferred_element_type=F32)
            m1 = jnp.dot(ws_ref[2 * p + 1].astype(BF16), vp, preferred_element_type=F32)
            mixed = jnp.where(lane < C_DG, m0, m1) + bs_ref[:, cs]
            o_ref[rs, cs] = u[rs, cs] * mixed


def _chunk_mlp(proj, ln_g, ln_b, w_s, bias_full):
    ucol = (5 * A_WIDTH + 3 * B_WIDTH) // C_WIDTH
    return pl.pallas_call(
        _cmlp_kernel,
        grid=(T // CM_ROWS,),
        in_specs=[
            pl.BlockSpec((CM_ROWS, C_WIDTH), lambda i: (i, ucol)),
            pl.BlockSpec((CM_ROWS, C_WIDTH), lambda i: (i, ucol + 1)),
            pl.BlockSpec((1, C_WIDTH), lambda i: (0, 0)),
            pl.BlockSpec((1, C_WIDTH), lambda i: (0, 0)),
            pl.BlockSpec((C_GROUPS, C_CHUNK, C_CHUNK), lambda i: (0, 0, 0)),
            pl.BlockSpec((C_CHUNK, C_WIDTH), lambda i: (0, 0)),
        ],
        out_specs=pl.BlockSpec((CM_ROWS, C_WIDTH), lambda i: (i, 0)),
        out_shape=jax.ShapeDtypeStruct((T, C_WIDTH), F32),
        compiler_params=_cparams(("arbitrary",)),
        name="chunk_mlp",
    )(proj, proj, ln_g.reshape(1, C_WIDTH), ln_b.reshape(1, C_WIDTH), w_s, bias_full)


def _postmix_kernel(of_ref, ob_ref, ag_ref, hg_ref, hsel_ref, b_ref, c_ref, w_ref, x_ref, gate1_ref, shift2_ref,
                    scale2_ref, g2_ref, wr_ref, br_ref, x1_ref, h2_ref, idx_ref, gw_ref):
    o = of_ref[...] + ob_ref[...]
    ms = _dot_sel(o * o, hsel_ref[...]) * (1.0 / A_DK)
    ag = ag_ref[...]
    a = o * lax.rsqrt(ms + EPS) * hg_ref[...] * (ag * jax.nn.sigmoid(ag))
    mixed = jnp.dot(a.astype(BF16), w_ref[0:A_WIDTH, :], preferred_element_type=F32)
    mixed = mixed + jnp.dot(b_ref[...].astype(BF16), w_ref[A_WIDTH:A_WIDTH + B_WIDTH, :], preferred_element_type=F32)
    mixed = mixed + jnp.dot(c_ref[...].astype(BF16), w_ref[A_WIDTH + B_WIDTH:, :], preferred_element_type=F32)
    x1 = x_ref[...] + gate1_ref[0] * mixed
    x1_ref[...] = x1
    y = x1 * lax.rsqrt(jnp.mean(x1 * x1, axis=-1, keepdims=True) + EPS) * g2_ref[...]
    h2 = y * (1.0 + scale2_ref[0]) + shift2_ref[0]
    h2_ref[...] = h2.astype(BF16)
    logits = jnp.dot(h2, wr_ref[...], precision=lax.Precision.HIGHEST, preferred_element_type=F32) + br_ref[...]
    lane = _lane(logits.shape)
    idx_out = jnp.zeros(logits.shape, jnp.int32)
    val_out = jnp.zeros(logits.shape, F32)
    top0 = None
    den = None
    for kk in range(TOP_K):
        mx = jnp.max(logits, axis=-1, keepdims=True)
        am = jnp.min(jnp.where(logits == mx, lane, LANES), axis=-1, keepdims=True)
        if kk == 0:
            top0 = mx
        e = jnp.exp(mx - top0)
        den = e if den is None else den + e
        idx_out = jnp.where(lane == kk, am, idx_out)
        val_out = jnp.where(lane == kk, e, val_out)
        logits = jnp.where(lane == am, -jnp.inf, logits)
    idx_ref[...] = idx_out
    gw_ref[...] = val_out / den


def _post_mix(o_f, o_b, proj, hg, hsel, b_out, c_out, w_out_bf, x, mod3, g2, wr_pad, br_pad):
    tile = lambda w: pl.BlockSpec((TM, w), lambda i: (i, 0))
    const = lambda shape: pl.BlockSpec(shape, lambda i: tuple(0 for _ in shape))
    modspec = lambda j: pl.BlockSpec((1, 1, D), lambda i: (_mod_row(i), 0, j))
    return pl.pallas_call(
        _postmix_kernel,
        grid=(N_TILES,),
        in_specs=[
            tile(A_WIDTH), tile(A_WIDTH),
            pl.BlockSpec((TM, A_WIDTH), lambda i: (i, 4)),
            const((1, A_WIDTH)), const((A_WIDTH, A_WIDTH)),
            tile(B_WIDTH), tile(C_WIDTH),
            const((D, D)),
            tile(D),
            modspec(2), modspec(3), modspec(4),
            const((1, D)), const((D, LANES)), const((1, LANES)),
        ],
        out_specs=[tile(D), tile(D), tile(LANES), tile(LANES)],
        out_shape=[
            jax.ShapeDtypeStruct((T, D), F32),
            jax.ShapeDtypeStruct((T, D), BF16),
            jax.ShapeDtypeStruct((T, LANES), jnp.int32),
            jax.ShapeDtypeStruct((T, LANES), F32),
        ],
        compiler_params=_cparams(("arbitrary",)),
        name="post_mix_router",
    )(o_f, o_b, proj, hg, hsel, b_out, c_out, w_out_bf, x, mod3, mod3, mod3, g2.reshape(1, D), wr_pad, br_pad)


def _moe_kernel(be_ref, nb_ref, x_ref, wgu_ref, bgu_ref, wdn_ref, bdn_ref, o_ref, wgu_bf, wdn_bf):
    i = pl.program_id(0)
    prev = be_ref[jnp.maximum(i - 1, 0)]
    new_expert = jnp.logical_or(i == 0, be_ref[i] != prev)

    @pl.when(new_expert)
    def _():
        wgu_bf[...] = wgu_ref[0].astype(BF16)
        wdn_bf[...] = wdn_ref[0].astype(BF16)

    @pl.when(i < nb_ref[0])
    def _():
        gu = jnp.dot(x_ref[...], wgu_bf[...], preferred_element_type=F32) + bgu_ref[0]
        glu = jnp.minimum(gu[:, :D], SWIGLU_LIMIT)
        lin = jnp.clip(gu[:, D:], -SWIGLU_LIMIT, SWIGLU_LIMIT)
        act = glu * jax.nn.sigmoid(SWIGLU_ALPHA * glu) * (lin + 1.0)
        o_ref[...] = jnp.dot(act.astype(BF16), wdn_bf[...], preferred_element_type=F32) + bdn_ref[0]

    @pl.when(i >= nb_ref[0])
    def _():
        o_ref[...] = jnp.zeros_like(o_ref)


def _moe_ffn(block_e, n_blocks, xs, w_gu, b_gu, w_dn, b_dn):
    return pl.pallas_call(
        _moe_kernel,
        grid_spec=pltpu.PrefetchScalarGridSpec(
            num_scalar_prefetch=2,
            grid=(MOE_BLOCKS,),
            in_specs=[
                pl.BlockSpec((MOE_BM, D), lambda i, be, nb: (i, 0)),
                pl.BlockSpec((1, D, 2 * D), lambda i, be, nb: (be[i], 0, 0)),
                pl.BlockSpec((1, 1, 2 * D), lambda i, be, nb: (be[i], 0, 0)),
                pl.BlockSpec((1, D, D), lambda i, be, nb: (be[i], 0, 0)),
                pl.BlockSpec((1, 1, D), lambda i, be, nb: (be[i], 0, 0)),
            ],
            out_specs=pl.BlockSpec((MOE_BM, D), lambda i, be, nb: (i, 0)),
            scratch_shapes=[pltpu.VMEM((D, 2 * D), BF16), pltpu.VMEM((D, D), BF16)],
        ),
        out_shape=jax.ShapeDtypeStruct((MOE_ROWS, D), F32),
        compiler_params=_cparams(("arbitrary",)),
        name="moe_expert_ffn",
    )(block_e, n_blocks, xs, w_gu, b_gu.reshape(N_EXPERTS, 1, 2 * D), w_dn, b_dn.reshape(N_EXPERTS, 1, D))


def _combine_kernel(x1_ref, y_ref, gw_ref, gate2_ref, o_ref):
    gw = gw_ref[...]
    acc = None
    for kk in range(TOP_K):
        term = y_ref[:, kk * D:(kk + 1) * D] * gw[:, kk:kk + 1]
        acc = term if acc is None else acc + term
    o_ref[...] = x1_ref[...] + gate2_ref[0] * acc


def _combine(x1, yg, gw, mod3):
    return pl.pallas_call(
        _combine_kernel,
        grid=(N_TILES,),
        in_specs=[
            pl.BlockSpec((TM, D), lambda i: (i, 0)),
            pl.BlockSpec((TM, TOP_K * D), lambda i: (i, 0)),
            pl.BlockSpec((TM, LANES), lambda i: (i, 0)),
            pl.BlockSpec((1, 1, D), lambda i: (_mod_row(i), 0, 5)),
        ],
        out_specs=pl.BlockSpec((TM, D), lambda i: (i, 0)),
        out_shape=jax.ShapeDtypeStruct((T, D), F32),
        compiler_params=_cparams(("arbitrary",)),
        name="moe_combine",
    )(x1, yg, gw, mod3)


def _route(top_idx):
    flat_e = top_idx.reshape(-1)
    onehot = (flat_e[:, None] == jnp.arange(N_EXPERTS, dtype=jnp.int32)[None, :]).astype(jnp.int32)
    csum = jnp.cumsum(onehot, axis=0)
    counts = csum[-1]
    pos = jnp.take_along_axis(csum, flat_e[:, None], axis=1)[:, 0] - 1
    padded = (counts + MOE_BM - 1) // MOE_BM * MOE_BM
    pad_end = jnp.cumsum(padded)
    pad_start = pad_end - padded
    dest = pad_start[flat_e] + pos
    tok = jnp.arange(T * TOP_K, dtype=jnp.int32) // TOP_K
    row_tok = jnp.zeros((MOE_ROWS,), jnp.int32).at[dest].set(tok)
    n_blocks = (pad_end[-1] // MOE_BM).astype(jnp.int32)
    blk_start = jnp.arange(MOE_BLOCKS, dtype=jnp.int32) * MOE_BM
    block_e = jnp.searchsorted(pad_end, blk_start, side='right').astype(jnp.int32)
    last_e = jnp.searchsorted(pad_end, (n_blocks - 1) * MOE_BM, side='right').astype(jnp.int32)
    block_e = jnp.minimum(block_e, last_e)
    return dest.reshape(T, TOP_K), row_tok, block_e, n_blocks.reshape(1)


def kernel(x_prompt, x_sample, c, cache_diff_k, cache_diff_v, state_hgrn, c_ctx, norm_mix_g, norm_ffn_g, w_mod, b_mod, w_in, w_out, hgrn_lower_bounds, hgrn_norm_g, diff_q_norm_g, diff_k_norm_g, diff_lambda_q1, diff_lambda_k1, diff_lambda_q2, diff_lambda_k2, diff_subln_g, cmlp_ln_g, cmlp_ln_b, cmlp_w_s, cmlp_b_s, router_w, router_b, moe_w_gate_up, moe_b_gate_up, moe_w_down, moe_b_down):
    x = jnp.concatenate([x_prompt.reshape(T_CTX, D), x_sample.reshape(T_SMP, D)], axis=0)
    cvec = jnp.concatenate([c_ctx[None, :], c, jnp.zeros((MOD_ROWS - 1 - DEC_BATCH, D), F32)], axis=0)
    mod = _modulation(cvec, w_mod, b_mod)

    lvl_np, tri_np = _hgrn_tables()
    lvl = jnp.asarray(lvl_np)
    tri = jnp.asarray(tri_np, dtype=BF16)
    cos, sin = _rope_tables()
    hsel = jnp.asarray(np.kron(np.eye(A_HEADS), np.ones((A_DK, A_DK))), dtype=BF16)
    sm = jax.nn.softmax(hgrn_lower_bounds.astype(F32), axis=0)
    lb_all = jnp.cumsum(sm, axis=0) - sm[0]

    new_k, new_v, new_s = [], [], []
    for l in range(DEPTH):
        mod3 = mod[l].reshape(MOD_ROWS, 1, 6 * D)
        proj = _in_projection(x, norm_mix_g[l], mod3, w_in[l].astype(BF16))

        s0 = jnp.concatenate([jnp.zeros((BATCH, 2, A_HEADS, A_DK, A_DK), F32), state_hgrn[:, l]], axis=0)
        o_dir, fin_dir = [], []
        for d in range(2):
            o_d, fin_d = _hgrn_scan(proj, lb_all[l, d].reshape(1, A_WIDTH), _pack_state(s0[:, d]),
                                    lvl, tri, l, d == 1)
            o_dir.append(o_d)
            fin_dir.append(_unpack_state(fin_d[:BATCH]))
        new_s.append(jnp.stack(fin_dir, axis=1))

        lam_init = 0.8 - 0.6 * math.exp(-0.3 * l)
        lam = (jnp.exp(jnp.sum(diff_lambda_q1[l] * diff_lambda_k1[l]))
               - jnp.exp(jnp.sum(diff_lambda_q2[l] * diff_lambda_k2[l])) + lam_init).reshape(1, 1)
        gq2 = jnp.tile(diff_q_norm_g[l], 2).reshape(1, LANES)
        gk2 = jnp.tile(diff_k_norm_g[l], 2).reshape(1, LANES)
        gs = diff_subln_g[l].reshape(1, LANES)
        b_ctx, k_l, v_l = _attn_ctx(proj, lam, gq2, gk2, gs, lam_init)
        b_smp = _attn_smp(proj, lam, cache_diff_k, cache_diff_v, cos, sin, gq2, gk2, gs, l, lam_init)
        b_out = jnp.concatenate([b_ctx, b_smp], axis=0)
        new_k.append(k_l)
        new_v.append(v_l)

        bias_full = jnp.repeat(cmlp_b_s[l].T, C_DG, axis=1)
        c_out = _chunk_mlp(proj, cmlp_ln_g[l], cmlp_ln_b[l], cmlp_w_s[l], bias_full)

        hg = jnp.tile(hgrn_norm_g[l], A_HEADS).reshape(1, A_WIDTH)
        wr_pad = jnp.pad(router_w[l], ((0, 0), (0, LANES - N_EXPERTS)))
        br_pad = jnp.pad(router_b[l], (0, LANES - N_EXPERTS), constant_values=-jnp.inf).reshape(1, LANES)
        x1, h2, idx_pad, gw_pad = _post_mix(o_dir[0], o_dir[1], proj, hg, hsel, b_out, c_out,
                                            w_out[l].astype(BF16), x, mod3, norm_ffn_g[l], wr_pad, br_pad)

        dest, row_tok, block_e, n_blocks = _route(idx_pad[:, :TOP_K])
        xs = jnp.take(h2, row_tok, axis=0)
        yb = _moe_ffn(block_e, n_blocks, xs, moe_w_gate_up[l], moe_b_gate_up[l], moe_w_down[l], moe_b_down[l])
        yg = jnp.take(yb, dest.reshape(-1), axis=0).reshape(T, TOP_K * D)
        x = _combine(x1, yg, gw_pad, mod3)

    y_prompt = x[:T_CTX].reshape(BATCH, SEQ, D)
    y_sample = x[T_CTX:].reshape(DEC_BATCH, DEC_SEQ, D)
    return (y_prompt, y_sample, jnp.stack(new_k, axis=1), jnp.stack(new_v, axis=1), jnp.stack(new_s, axis=1))
```

```python
import functools
import math

import numpy as np
import jax
import jax.numpy as jnp
from jax import lax
from jax.experimental import pallas as pl
from jax.experimental.pallas import tpu as pltpu

F32 = jnp.float32
BF16 = jnp.bfloat16

D = 1024
DEPTH = 2
BATCH, SEQ = 16, 256
DEC_BATCH, DEC_SEQ = 8, 1024
PAST = 512
GRID_W = 64
A_HEADS, A_DK = 4, 64
A_WIDTH = 256
B_HEADS, B_DK, B_DV = 4, 64, 128
B_WIDTH = 512
C_GROUPS, C_CHUNK, C_WIDTH, C_DG = 4, 128, 256, 64
IN_WIDTH = 5 * A_WIDTH + 3 * B_WIDTH + 2 * C_WIDTH
N_EXPERTS, TOP_K = 32, 4
SWIGLU_LIMIT, SWIGLU_ALPHA = 7.0, 1.702
ROPE_BASE = 10000.0
EPS = 1e-6

T_CTX = BATCH * SEQ
T_SMP = DEC_BATCH * DEC_SEQ
T = T_CTX + T_SMP
N_SEQ = BATCH + DEC_BATCH
MOD_ROWS = 16

TM = 256
N_TILES = T // TM
CTX_TILES = T_CTX // TM
SMP_TILES_PER_SEQ = DEC_SEQ // TM
LANES = 128
MOE_BM = 256
MOE_ROWS = T * TOP_K + N_EXPERTS * MOE_BM
MOE_BLOCKS = MOE_ROWS // MOE_BM
VMEM_LIMIT = 56 * 1024 * 1024


def _cparams(sem):
    return pltpu.CompilerParams(dimension_semantics=sem, vmem_limit_bytes=VMEM_LIMIT)


def _mod_row(i):
    return jnp.where(i < CTX_TILES, 0, 1 + (i - CTX_TILES) // SMP_TILES_PER_SEQ)


def _split3(x):
    hi = x.astype(BF16)
    r = x - hi.astype(F32)
    mid = r.astype(BF16)
    lo = (r - mid.astype(F32)).astype(BF16)
    return hi, mid, lo


def _sel_dot(sel, x):
    hi, mid, lo = _split3(x)
    acc = jnp.dot(sel, lo, preferred_element_type=F32)
    acc = acc + jnp.dot(sel, mid, preferred_element_type=F32)
    return acc + jnp.dot(sel, hi, preferred_element_type=F32)


def _dot_sel(x, sel):
    hi, mid, lo = _split3(x)
    acc = jnp.dot(lo, sel, preferred_element_type=F32)
    acc = acc + jnp.dot(mid, sel, preferred_element_type=F32)
    return acc + jnp.dot(hi, sel, preferred_element_type=F32)


def _dot_nt(a, b):
    return lax.dot_general(a, b, (((1,), (1,)), ((), ())), preferred_element_type=F32)


def _dot_tn(a, b):
    return lax.dot_general(a, b, (((0,), (0,)), ((), ())), preferred_element_type=F32)


def _lane(shape):
    return lax.broadcasted_iota(jnp.int32, shape, len(shape) - 1)


def _mod_kernel(c_ref, w_ref, b_ref, o_ref):
    c = c_ref[...]
    s = c * jax.nn.sigmoid(c)
    o_ref[0] = jnp.dot(s.astype(BF16), w_ref[0].astype(BF16), preferred_element_type=F32) + b_ref[0]


def _modulation(cvec, w_mod, b_mod):
    tn = 1536
    return pl.pallas_call(
        _mod_kernel,
        grid=(DEPTH, 6 * D // tn),
        in_specs=[
            pl.BlockSpec((MOD_ROWS, D), lambda l, j: (0, 0)),
            pl.BlockSpec((1, D, tn), lambda l, j: (l, 0, j)),
            pl.BlockSpec((1, 1, tn), lambda l, j: (l, 0, j)),
        ],
        out_specs=pl.BlockSpec((1, MOD_ROWS, tn), lambda l, j: (l, 0, j)),
        out_shape=jax.ShapeDtypeStruct((DEPTH, MOD_ROWS, 6 * D), F32),
        compiler_params=_cparams(("arbitrary", "arbitrary")),
        name="modulation",
    )(cvec, w_mod, b_mod.reshape(DEPTH, 1, 6 * D))


def _inproj_kernel(x_ref, g_ref, shift_ref, scale_ref, w_ref, o_ref):
    x = x_ref[...]
    y = x * lax.rsqrt(jnp.mean(x * x, axis=-1, keepdims=True) + EPS) * g_ref[...]
    h = y * (1.0 + scale_ref[0]) + shift_ref[0]
    o_ref[...] = jnp.dot(h.astype(BF16), w_ref[...], preferred_element_type=F32)


def _in_projection(x, g, mod3, w_in_bf):
    return pl.pallas_call(
        _inproj_kernel,
        grid=(N_TILES,),
        in_specs=[
            pl.BlockSpec((TM, D), lambda i: (i, 0)),
            pl.BlockSpec((1, D), lambda i: (0, 0)),
            pl.BlockSpec((1, 1, D), lambda i: (_mod_row(i), 0, 0)),
            pl.BlockSpec((1, 1, D), lambda i: (_mod_row(i), 0, 1)),
            pl.BlockSpec((D, IN_WIDTH), lambda i: (0, 0)),
        ],
        out_specs=pl.BlockSpec((TM, IN_WIDTH), lambda i: (i, 0)),
        out_shape=jax.ShapeDtypeStruct((T, IN_WIDTH), F32),
        compiler_params=_cparams(("arbitrary",)),
        name="in_projection",
    )(x, g.reshape(1, D), mod3, mod3, w_in_bf)


HG_C = TM
HG_LEVELS = tuple(2 ** j for j in range(1, int(math.log2(HG_C)) + 1))


def _hgrn_tables():
    t = np.arange(HG_C)[:, None]
    s = np.arange(HG_C)[None, :]
    x = t ^ s
    lvl = np.zeros((HG_C, HG_C), np.int32)
    nz = x > 0
    lvl[nz] = np.floor(np.log2(x[nz])).astype(np.int32) + 1
    fwd = np.where(t >= s, lvl, -1).astype(np.int32)
    bwd = np.where(t <= s, lvl, -1).astype(np.int32)
    tri_f = (t >= s).astype(np.float32)
    tri_b = (t <= s).astype(np.float32)
    return np.stack([fwd, bwd]), np.stack([tri_f, tri_b])


def _block_ref(cum, m, idx):
    c, l = cum.shape
    if m >= 16:
        c3 = cum.reshape(c // m, m, l)
        r = c3[:, idx:idx + 1, :]
        return jnp.broadcast_to(r, (c // m, m, l)).reshape(c, l)
    c3 = cum.reshape(c // 8, 8, l)
    sub = lax.broadcasted_iota(jnp.int32, c3.shape, 1)
    out = None
    for j in range(8 // m - 1, -1, -1):
        cand = jnp.broadcast_to(c3[:, j * m + idx:j * m + idx + 1, :], c3.shape)
        out = cand if out is None else jnp.where(sub < (j + 1) * m, cand, out)
    return out.reshape(c, l)


def _hgrn_kernel(q_ref, z_ref, v_ref, lb_ref, s0_ref, lvl_ref, tri_ref, o_ref, fin_ref, st_ref, *, layer, rev):
    g = pl.program_id(0)
    first = jnp.logical_or(g < CTX_TILES, (g - CTX_TILES) % SMP_TILES_PER_SEQ == 0)

    @pl.when(first)
    def _():
        st_ref[...] = s0_ref[0]

    qr = q_ref[...]
    q = qr * jax.nn.sigmoid(qr) * (A_DK ** -0.5)
    z = z_ref[...]
    if layer == 0:
        lf = jnp.minimum(z, 0.0) - jnp.log(1.0 + jnp.exp(-jnp.abs(z)))
        k = jax.nn.sigmoid(-z)
    else:
        lbd = lb_ref[...]
        lf = jnp.log(lbd + (1.0 - lbd) * jax.nn.sigmoid(z))
        k = (1.0 - lbd) * jax.nn.sigmoid(-z)
    v = v_ref[...]
    cum = _sel_dot(tri_ref[0], lf)
    lvl = lvl_ref[0]
    last_row = 0 if rev else HG_C - 1

    for p in range(2):
        sl = slice(p * LANES, (p + 1) * LANES)
        q_p, k_p, v_p, cum_p = q[:, sl], k[:, sl], v[:, sl], cum[:, sl]
        lane = _lane((HG_C, LANES))
        head_masks = (lane < A_DK, lane >= A_DK)
        v_bf = v_p.astype(BF16)
        k_bf = k_p.astype(BF16)
        scores = []
        for hm in head_masks:
            qm = jnp.where(hm, q_p, 0.0).astype(BF16)
            scores.append(jnp.where(lvl == 0, _dot_nt(qm, k_bf), 0.0))
        for li, m in enumerate(HG_LEVELS):
            ref = _block_ref(cum_p, m, m // 2 if rev else m // 2 - 1)
            qd = q_p * jnp.exp(jnp.minimum(cum_p - ref, 0.0))
            kd = (k_p * jnp.exp(jnp.minimum(ref - cum_p, 0.0))).astype(BF16)
            for hi, hm in enumerate(head_masks):
                qm = jnp.where(hm, qd, 0.0).astype(BF16)
                scores[hi] = jnp.where(lvl == li + 1, _dot_nt(qm, kd), scores[hi])
        st = st_ref[p]
        o_intra = jnp.where(head_masks[0],
                            jnp.dot(scores[0].astype(BF16), v_bf, preferred_element_type=F32),
                            jnp.dot(scores[1].astype(BF16), v_bf, preferred_element_type=F32))
        q0 = (q_p * jnp.exp(cum_p)).astype(BF16)
        o_ref[:, sl] = o_intra + _dot_nt(q0, st.astype(BF16))
        last = cum_p[last_row:last_row + 1, :]
        ks = (k_p * jnp.exp(last - cum_p)).astype(BF16)
        upd = _dot_tn(v_bf, ks)
        r = lax.broadcasted_iota(jnp.int32, (LANES, LANES), 0)
        cl = lax.broadcasted_iota(jnp.int32, (LANES, LANES), 1)
        same_head = (r < A_DK) == (cl < A_DK)
        st_new = st * jnp.exp(last) + jnp.where(same_head, upd, 0.0)
        st_ref[p] = st_new
        fin_ref[0, p] = st_new


def _hgrn_seq(g):
    return jnp.where(g < CTX_TILES, g, CTX_TILES + (g - CTX_TILES) // SMP_TILES_PER_SEQ)


def _hgrn_blk(g, rev):
    if not rev:
        return g
    j = g - CTX_TILES
    return jnp.where(g < CTX_TILES, g,
                     CTX_TILES + (j // SMP_TILES_PER_SEQ) * SMP_TILES_PER_SEQ
                     + (SMP_TILES_PER_SEQ - 1 - j % SMP_TILES_PER_SEQ))


def _hgrn_scan(proj, lb_dir, s0_dir, lvl, tri, layer, rev):
    d = 1 if rev else 0
    blk = functools.partial(_hgrn_blk, rev=rev)
    return pl.pallas_call(
        functools.partial(_hgrn_kernel, layer=layer, rev=rev),
        grid=(N_TILES,),
        in_specs=[
            pl.BlockSpec((HG_C, A_WIDTH), lambda g: (blk(g), 0)),
            pl.BlockSpec((HG_C, A_WIDTH), lambda g: (blk(g), 1 + d)),
            pl.BlockSpec((HG_C, A_WIDTH), lambda g: (blk(g), 3)),
            pl.BlockSpec((1, A_WIDTH), lambda g: (0, 0)),
            pl.BlockSpec((1, 2, LANES, LANES), lambda g: (_hgrn_seq(g), 0, 0, 0)),
            pl.BlockSpec((1, HG_C, HG_C), lambda g: (d, 0, 0)),
            pl.BlockSpec((1, HG_C, HG_C), lambda g: (d, 0, 0)),
        ],
        out_specs=[
            pl.BlockSpec((HG_C, A_WIDTH), lambda g: (blk(g), 0)),
            pl.BlockSpec((1, 2, LANES, LANES), lambda g: (_hgrn_seq(g), 0, 0, 0)),
        ],
        out_shape=[
            jax.ShapeDtypeStruct((T, A_WIDTH), F32),
            jax.ShapeDtypeStruct((N_SEQ, 2, LANES, LANES), F32),
        ],
        scratch_shapes=[pltpu.VMEM((2, LANES, LANES), F32)],
        compiler_params=_cparams(("arbitrary",)),
        name=f"hgrn_scan_{'bwd' if rev else 'fwd'}",
    )(proj, proj, proj, lb_dir, s0_dir, lvl, tri)


def _pack_state(s):
    n = s.shape[0]
    st = jnp.swapaxes(s, -1, -2).reshape(n, 2, 2, A_DK, A_DK)
    z = jnp.zeros_like(st[:, :, 0])
    top = jnp.concatenate([st[:, :, 0], z], axis=-1)
    bot = jnp.concatenate([z, st[:, :, 1]], axis=-1)
    return jnp.concatenate([top, bot], axis=-2)


def _unpack_state(sp):
    n = sp.shape[0]
    h0 = sp[:, :, :A_DK, :A_DK]
    h1 = sp[:, :, A_DK:, A_DK:]
    st = jnp.stack([h0, h1], axis=2).reshape(n, A_HEADS, A_DK, A_DK)
    return jnp.swapaxes(st, -1, -2)


def _half_rms(x, g):
    lane = _lane(x.shape)
    lo = lane < B_DK
    xx = x * x
    ms0 = jnp.sum(jnp.where(lo, xx, 0.0), axis=-1, keepdims=True) * (1.0 / B_DK)
    ms1 = jnp.sum(jnp.where(lo, 0.0, xx), axis=-1, keepdims=True) * (1.0 / B_DK)
    inv = jnp.where(lo, lax.rsqrt(ms0 + EPS), lax.rsqrt(ms1 + EPS))
    return x * inv * g


def _rope(x, cos, sin_signed):
    lane = _lane(x.shape)
    first = (lane % 32) < 16
    rot = jnp.where(first, pltpu.roll(x, LANES - 16, 1), pltpu.roll(x, 16, 1))
    return x * cos + rot * sin_signed


def _diff_softmax_pv(q_bf, keys_bf, vals_bf, lam):
    lane = _lane(q_bf.shape)
    zero = jnp.zeros_like(q_bf)
    acc = None
    parts = []
    for mp in range(2):
        qm = jnp.where((lane < B_DK) == (mp == 0), q_bf, zero)
        s = [_dot_nt(qm, kk) for kk in keys_bf]
        mx = functools.reduce(jnp.maximum, [jnp.max(si, axis=-1, keepdims=True) for si in s])
        e = [jnp.exp(si - mx) for si in s]
        den = functools.reduce(lambda a, b: a + b, [jnp.sum(ei, axis=-1, keepdims=True) for ei in e])
        parts.append((e, 1.0 / den))
    (e0, r0), (e1, r1) = parts
    r1 = r1 * lam
    for i in range(len(keys_bf)):
        a = (e0[i] * r0 - e1[i] * r1).astype(BF16)
        pv = jnp.dot(a, vals_bf[i], preferred_element_type=F32)
        acc = pv if acc is None else acc + pv
    return acc


def _subln(o, g, lam_init):
    return o * lax.rsqrt(jnp.mean(o * o, axis=-1, keepdims=True) + EPS) * g * (1.0 - lam_init)


def _attn_ctx_kernel(lam_ref, q_ref, k_ref, v_ref, gq_ref, gk_ref, gs_ref, o_ref, nk_ref, nv_ref, *, lam_init):
    lam = lam_ref[0, 0]
    qn = _half_rms(q_ref[...], gq_ref[...]) * (B_DK ** -0.5)
    kn = _half_rms(k_ref[...], gk_ref[...])
    v = v_ref[...]
    nk_ref[0, 0, 0] = kn[:, :B_DK]
    nk_ref[0, 1, 0] = kn[:, B_DK:]
    nv_ref[0, 0] = v
    o = _diff_softmax_pv(qn.astype(BF16), [kn.astype(BF16)], [v.astype(BF16)], lam)
    o_ref[...] = _subln(o, gs_ref[...], lam_init)


def _attn_ctx(proj, lam, gq2, gk2, gs, lam_init):
    qcol, kcol, vcol = 5 * A_WIDTH // LANES, 5 * A_WIDTH // LANES + 4, 5 * A_WIDTH // LANES + 8
    return pl.pallas_call(
        functools.partial(_attn_ctx_kernel, lam_init=lam_init),
        grid=(BATCH, B_HEADS),
        in_specs=[
            pl.BlockSpec(memory_space=pltpu.SMEM),
            pl.BlockSpec((SEQ, LANES), lambda b, h: (b, qcol + h)),
            pl.BlockSpec((SEQ, LANES), lambda b, h: (b, kcol + h)),
            pl.BlockSpec((SEQ, LANES), lambda b, h: (b, vcol + h)),
            pl.BlockSpec((1, LANES), lambda b, h: (0, 0)),
            pl.BlockSpec((1, LANES), lambda b, h: (0, 0)),
            pl.BlockSpec((1, LANES), lambda b, h: (0, 0)),
        ],
        out_specs=[
            pl.BlockSpec((SEQ, LANES), lambda b, h: (b, h)),
            pl.BlockSpec((1, 2, 1, SEQ, B_DK), lambda b, h: (b, 0, h, 0, 0)),
            pl.BlockSpec((1, 1, SEQ, B_DV), lambda b, h: (b, h, 0, 0)),
        ],
        out_shape=[
            jax.ShapeDtypeStruct((T_CTX, B_WIDTH), F32),
            jax.ShapeDtypeStruct((BATCH, 2, B_HEADS, SEQ, B_DK), F32),
            jax.ShapeDtypeStruct((BATCH, B_HEADS, SEQ, B_DV), F32),
        ],
        compiler_params=_cparams(("arbitrary", "arbitrary")),
        name="diff_attention_ctx",
    )(lam, proj, proj, proj, gq2, gk2, gs)


ATT_TQ = 256


def _attn_smp_kernel(lam_ref, q_ref, k_ref, v_ref, ck_ref, cv_ref, cos_ref, sin_ref, gq_ref, gk_ref, gs_ref,
                     o_ref, qs_ref, ks_ref, *, lam_init):
    lam = lam_ref[0, 0]
    cos = cos_ref[...]
    sin = sin_ref[...]
    qn = _rope(_half_rms(q_ref[...], gq_ref[...]), cos, sin) * (B_DK ** -0.5)
    qs_ref[...] = qn.astype(BF16)
    ks_ref[...] = _rope(_half_rms(k_ref[...], gk_ref[...]), cos, sin).astype(BF16)
    ck = jnp.concatenate([ck_ref[0, 0, 0, 0], ck_ref[0, 0, 1, 0]], axis=-1).astype(BF16)
    cv = cv_ref[0, 0, 0].astype(BF16)
    v_bf = v_ref[...].astype(BF16)
    k_bf = ks_ref[...]
    g = gs_ref[...]

    def body(i, carry):
        r0 = pl.multiple_of(i * ATT_TQ, ATT_TQ)
        q_bf = qs_ref[pl.ds(r0, ATT_TQ), :]
        o = _diff_softmax_pv(q_bf, [k_bf, ck], [v_bf, cv], lam)
        o_ref[pl.ds(r0, ATT_TQ), :] = _subln(o, g, lam_init)
        return carry

    lax.fori_loop(0, DEC_SEQ // ATT_TQ, body, 0)


def _attn_smp(proj, lam, cache_k, cache_v, cos, sin, gq2, gk2, gs, layer, lam_init):
    qcol, kcol, vcol = 5 * A_WIDTH // LANES, 5 * A_WIDTH // LANES + 4, 5 * A_WIDTH // LANES + 8
    r0 = T_CTX // DEC_SEQ
    return pl.pallas_call(
        functools.partial(_attn_smp_kernel, lam_init=lam_init),
        grid=(DEC_BATCH, B_HEADS),
        in_specs=[
            pl.BlockSpec(memory_space=pltpu.SMEM),
            pl.BlockSpec((DEC_SEQ, LANES), lambda b, h: (r0 + b, qcol + h)),
            pl.BlockSpec((DEC_SEQ, LANES), lambda b, h: (r0 + b, kcol + h)),
            pl.BlockSpec((DEC_SEQ, LANES), lambda b, h: (r0 + b, vcol + h)),
            pl.BlockSpec((1, 1, 2, 1, PAST, B_DK), lambda b, h: (b, layer, 0, h, 0, 0)),
            pl.BlockSpec((1, 1, 1, PAST, B_DV), lambda b, h: (b, layer, h, 0, 0)),
            pl.BlockSpec((DEC_SEQ, LANES), lambda b, h: (0, 0)),
            pl.BlockSpec((DEC_SEQ, LANES), lambda b, h: (0, 0)),
            pl.BlockSpec((1, LANES), lambda b, h: (0, 0)),
            pl.BlockSpec((1, LANES), lambda b, h: (0, 0)),
            pl.BlockSpec((1, LANES), lambda b, h: (0, 0)),
        ],
        out_specs=pl.BlockSpec((DEC_SEQ, LANES), lambda b, h: (b, h)),
        out_shape=jax.ShapeDtypeStruct((T_SMP, B_WIDTH), F32),
        scratch_shapes=[pltpu.VMEM((DEC_SEQ, LANES), BF16), pltpu.VMEM((DEC_SEQ, LANES), BF16)],
        compiler_params=_cparams(("arbitrary", "arbitrary")),
        name="diff_attention_smp",
    )(lam, proj, proj, proj, cache_k, cache_v, cos, sin, gq2, gk2, gs)


def _rope_tables():
    n_rows = DEC_SEQ // GRID_W
    row = np.repeat(np.arange(n_rows), GRID_W).astype(np.float32)
    col = np.tile(np.arange(GRID_W), n_rows).astype(np.float32)
    half = B_DK // 2
    inv_freq = (ROPE_BASE ** (-jnp.arange(0, half, 2, dtype=F32) / half))
    row_ang = jnp.asarray(row)[:, None] * inv_freq
    col_ang = jnp.asarray(col)[:, None] * inv_freq
    ang = jnp.concatenate([row_ang, row_ang, col_ang, col_ang], axis=-1)
    ang = jnp.concatenate([ang, ang], axis=-1)
    sign = np.where((np.arange(LANES) % 32) < 16, -1.0, 1.0).astype(np.float32)
    return jnp.cos(ang), jnp.sin(ang) * sign


CM_ROWS = 512


def _gelu(x):
    return 0.5 * x * (1.0 + lax.erf(x * (2.0 ** -0.5)))


def _cmlp_kernel(u_ref, v_ref, g_ref, b_ref, ws_ref, bs_ref, o_ref):
    u = _gelu(u_ref[...])
    gv = _gelu(v_ref[...])
    mu = jnp.mean(gv, axis=-1, keepdims=True)
    dv = gv - mu
    var = jnp.mean(dv * dv, axis=-1, keepdims=True)
    vn = (dv * lax.rsqrt(var + EPS) * g_ref[...] + b_ref[...]).astype(BF16)
    lane = _lane((C_CHUNK, LANES))
    for c in range(CM_ROWS // C_CHUNK):
        rs = slice(c * C_CHUNK, (c + 1) * C_CHUNK)
        for p in range(2):
            cs = slice(p * LANES, (p + 1) * LANES)
            vp = vn[rs, cs]
            m0 = jnp.dot(ws_ref[2 * p].astype(BF16), vp, preferred_element_type=F32)
            m1 = jnp.dot(ws_ref[2 * p + 1].astype(BF16), vp, preferred_element_type=F32)
            mixed = jnp.where(lane < C_DG, m0, m1) + bs_ref[:, cs]
            o_ref[rs, cs] = u[rs, cs] * mixed


def _chunk_mlp(proj, ln_g, ln_b, w_s, bias_full):
    ucol = (5 * A_WIDTH + 3 * B_WIDTH) // C_WIDTH
    return pl.pallas_call(
        _cmlp_kernel,
        grid=(T // CM_ROWS,),
        in_specs=[
            pl.BlockSpec((CM_ROWS, C_WIDTH), lambda i: (i, ucol)),
            pl.BlockSpec((CM_ROWS, C_WIDTH), lambda i: (i, ucol + 1)),
            pl.BlockSpec((1, C_WIDTH), lambda i: (0, 0)),
            pl.BlockSpec((1, C_WIDTH), lambda i: (0, 0)),
            pl.BlockSpec((C_GROUPS, C_CHUNK, C_CHUNK), lambda i: (0, 0, 0)),
            pl.BlockSpec((C_CHUNK, C_WIDTH), lambda i: (0, 0)),
        ],
        out_specs=pl.BlockSpec((CM_ROWS, C_WIDTH), lambda i: (i, 0)),
        out_shape=jax.ShapeDtypeStruct((T, C_WIDTH), F32),
        compiler_params=_cparams(("arbitrary",)),
        name="chunk_mlp",
    )(proj, proj, ln_g.reshape(1, C_WIDTH), ln_b.reshape(1, C_WIDTH), w_s, bias_full)


def _postmix_kernel(of_ref, ob_ref, ag_ref, hg_ref, hsel_ref, b_ref, c_ref, w_ref, x_ref, gate1_ref, shift2_ref,
                    scale2_ref, g2_ref, wr_ref, br_ref, x1_ref, h2_ref, idx_ref, gw_ref):
    o = of_ref[...] + ob_ref[...]
    ms = _dot_sel(o * o, hsel_ref[...]) * (1.0 / A_DK)
    ag = ag_ref[...]
    a = o * lax.rsqrt(ms + EPS) * hg_ref[...] * (ag * jax.nn.sigmoid(ag))
    mixed = jnp.dot(a.astype(BF16), w_ref[0:A_WIDTH, :], preferred_element_type=F32)
    mixed = mixed + jnp.dot(b_ref[...].astype(BF16), w_ref[A_WIDTH:A_WIDTH + B_WIDTH, :], preferred_element_type=F32)
    mixed = mixed + jnp.dot(c_ref[...].astype(BF16), w_ref[A_WIDTH + B_WIDTH:, :], preferred_element_type=F32)
    x1 = x_ref[...] + gate1_ref[0] * mixed
    x1_ref[...] = x1
    y = x1 * lax.rsqrt(jnp.mean(x1 * x1, axis=-1, keepdims=True) + EPS) * g2_ref[...]
    h2 = y * (1.0 + scale2_ref[0]) + shift2_ref[0]
    h2_ref[...] = h2.astype(BF16)
    logits = jnp.dot(h2, wr_ref[...], precision=lax.Precision.HIGHEST, preferred_element_type=F32) + br_ref[...]
    lane = _lane(logits.shape)
    idx_out = jnp.zeros(logits.shape, jnp.int32)
    val_out = jnp.zeros(logits.shape, F32)
    top0 = None
    den = None
    for kk in range(TOP_K):
        mx = jnp.max(logits, axis=-1, keepdims=True)
        am = jnp.min(jnp.where(logits == mx, lane, LANES), axis=-1, keepdims=True)
        if kk == 0:
            top0 = mx
        e = jnp.exp(mx - top0)
        den = e if den is None else den + e
        idx_out = jnp.where(lane == kk, am, idx_out)
        val_out = jnp.where(lane == kk, e, val_out)
        logits = jnp.where(lane == am, -jnp.inf, logits)
    idx_ref[...] = idx_out
    gw_ref[...] = val_out / den


def _post_mix(o_f, o_b, proj, hg, hsel, b_out, c_out, w_out_bf, x, mod3, g2, wr_pad, br_pad):
    tile = lambda w: pl.BlockSpec((TM, w), lambda i: (i, 0))
    const = lambda shape: pl.BlockSpec(shape, lambda i: tuple(0 for _ in shape))
    modspec = lambda j: pl.BlockSpec((1, 1, D), lambda i: (_mod_row(i), 0, j))
    return pl.pallas_call(
        _postmix_kernel,
        grid=(N_TILES,),
        in_specs=[
            tile(A_WIDTH), tile(A_WIDTH),
            pl.BlockSpec((TM, A_WIDTH), lambda i: (i, 4)),
            const((1, A_WIDTH)), const((A_WIDTH, A_WIDTH)),
            tile(B_WIDTH), tile(C_WIDTH),
            const((D, D)),
            tile(D),
            modspec(2), modspec(3), modspec(4),
            const((1, D)), const((D, LANES)), const((1, LANES)),
        ],
        out_specs=[tile(D), tile(D), tile(LANES), tile(LANES)],
        out_shape=[
            jax.ShapeDtypeStruct((T, D), F32),
            jax.ShapeDtypeStruct((T, D), BF16),
            jax.ShapeDtypeStruct((T, LANES), jnp.int32),
            jax.ShapeDtypeStruct((T, LANES), F32),
        ],
        compiler_params=_cparams(("arbitrary",)),
        name="post_mix_router",
    )(o_f, o_b, proj, hg, hsel, b_out, c_out, w_out_bf, x, mod3, mod3, mod3, g2.reshape(1, D), wr_pad, br_pad)


def _moe_kernel(be_ref, nb_ref, x_ref, wgu_ref, bgu_ref, wdn_ref, bdn_ref, o_ref, wgu_bf, wdn_bf):
    i = pl.program_id(0)
    prev = be_ref[jnp.maximum(i - 1, 0)]
    new_expert = jnp.logical_or(i == 0, be_ref[i] != prev)

    @pl.when(new_expert)
    def _():
        wgu_bf[...] = wgu_ref[0, 0].astype(BF16)
        wdn_bf[...] = wdn_ref[0, 0].astype(BF16)

    @pl.when(i < nb_ref[0])
    def _():
        gu = jnp.dot(x_ref[...], wgu_bf[...], preferred_element_type=F32) + bgu_ref[0, 0]
        glu = jnp.minimum(gu[:, :D], SWIGLU_LIMIT)
        lin = jnp.clip(gu[:, D:], -SWIGLU_LIMIT, SWIGLU_LIMIT)
        act = glu * jax.nn.sigmoid(SWIGLU_ALPHA * glu) * (lin + 1.0)
        o_ref[...] = jnp.dot(act.astype(BF16), wdn_bf[...], preferred_element_type=F32) + bdn_ref[0, 0]

    @pl.when(i >= nb_ref[0])
    def _():
        o_ref[...] = jnp.zeros_like(o_ref)


def _moe_ffn(block_e, n_blocks, xs, w_gu, b_gu, w_dn, b_dn, layer):
    return pl.pallas_call(
        _moe_kernel,
        grid_spec=pltpu.PrefetchScalarGridSpec(
            num_scalar_prefetch=2,
            grid=(MOE_BLOCKS,),
            in_specs=[
                pl.BlockSpec((MOE_BM, D), lambda i, be, nb: (i, 0)),
                pl.BlockSpec((1, 1, D, 2 * D), lambda i, be, nb: (layer, be[i], 0, 0)),
                pl.BlockSpec((1, 1, 1, 2 * D), lambda i, be, nb: (layer, be[i], 0, 0)),
                pl.BlockSpec((1, 1, D, D), lambda i, be, nb: (layer, be[i], 0, 0)),
                pl.BlockSpec((1, 1, 1, D), lambda i, be, nb: (layer, be[i], 0, 0)),
            ],
            out_specs=pl.BlockSpec((MOE_BM, D), lambda i, be, nb: (i, 0)),
            scratch_shapes=[pltpu.VMEM((D, 2 * D), BF16), pltpu.VMEM((D, D), BF16)],
        ),
        out_shape=jax.ShapeDtypeStruct((MOE_ROWS, D), F32),
        compiler_params=_cparams(("arbitrary",)),
        name="moe_expert_ffn",
    )(block_e, n_blocks, xs, w_gu, b_gu.reshape(DEPTH, N_EXPERTS, 1, 2 * D), w_dn, b_dn.reshape(DEPTH, N_EXPERTS, 1, D))


def _combine_kernel(x1_ref, y_ref, gw_ref, gate2_ref, o_ref):
    gw = gw_ref[...]
    acc = None
    for kk in range(TOP_K):
        term = y_ref[kk] * gw[:, kk:kk + 1]
        acc = term if acc is None else acc + term
    o_ref[...] = x1_ref[...] + gate2_ref[0] * acc


def _combine(x1, yg, gw, mod3):
    return pl.pallas_call(
        _combine_kernel,
        grid=(N_TILES,),
        in_specs=[
            pl.BlockSpec((TM, D), lambda i: (i, 0)),
            pl.BlockSpec((TOP_K, TM, D), lambda i: (0, i, 0)),
            pl.BlockSpec((TM, LANES), lambda i: (i, 0)),
            pl.BlockSpec((1, 1, D), lambda i: (_mod_row(i), 0, 5)),
        ],
        out_specs=pl.BlockSpec((TM, D), lambda i: (i, 0)),
        out_shape=jax.ShapeDtypeStruct((T, D), F32),
        compiler_params=_cparams(("arbitrary",)),
        name="moe_combine",
    )(x1, yg, gw, mod3)


def _route(top_idx):
    flat_e = top_idx.reshape(-1)
    onehot = (flat_e[:, None] == jnp.arange(N_EXPERTS, dtype=jnp.int32)[None, :]).astype(jnp.int32)
    csum = jnp.cumsum(onehot, axis=0)
    counts = csum[-1]
    pos = jnp.take_along_axis(csum, flat_e[:, None], axis=1)[:, 0] - 1
    padded = (counts + MOE_BM - 1) // MOE_BM * MOE_BM
    pad_end = jnp.cumsum(padded)
    pad_start = pad_end - padded
    dest = pad_start[flat_e] + pos
    tok = jnp.arange(T * TOP_K, dtype=jnp.int32) // TOP_K
    row_tok = jnp.zeros((MOE_ROWS,), jnp.int32).at[dest].set(tok)
    n_blocks = (pad_end[-1] // MOE_BM).astype(jnp.int32)
    blk_start = jnp.arange(MOE_BLOCKS, dtype=jnp.int32) * MOE_BM
    blk_start = jnp.minimum(blk_start, (n_blocks - 1) * MOE_BM)
    block_e = jnp.sum((pad_end[None, :] <= blk_start[:, None]).astype(jnp.int32), axis=1)
    return dest.reshape(T, TOP_K), row_tok, block_e, n_blocks.reshape(1)


def kernel(x_prompt, x_sample, c, cache_diff_k, cache_diff_v, state_hgrn, c_ctx, norm_mix_g, norm_ffn_g, w_mod, b_mod, w_in, w_out, hgrn_lower_bounds, hgrn_norm_g, diff_q_norm_g, diff_k_norm_g, diff_lambda_q1, diff_lambda_k1, diff_lambda_q2, diff_lambda_k2, diff_subln_g, cmlp_ln_g, cmlp_ln_b, cmlp_w_s, cmlp_b_s, router_w, router_b, moe_w_gate_up, moe_b_gate_up, moe_w_down, moe_b_down):
    x = jnp.concatenate([x_prompt.reshape(T_CTX, D), x_sample.reshape(T_SMP, D)], axis=0)
    cvec = jnp.concatenate([c_ctx[None, :], c, jnp.zeros((MOD_ROWS - 1 - DEC_BATCH, D), F32)], axis=0)
    mod = _modulation(cvec, w_mod, b_mod)

    lvl_np, tri_np = _hgrn_tables()
    lvl = jnp.asarray(lvl_np)
    tri = jnp.asarray(tri_np, dtype=BF16)
    cos, sin = _rope_tables()
    hsel = jnp.asarray(np.kron(np.eye(A_HEADS), np.ones((A_DK, A_DK))), dtype=BF16)
    sm = jax.nn.softmax(hgrn_lower_bounds.astype(F32), axis=0)
    lb_all = jnp.cumsum(sm, axis=0) - sm[0]

    new_k, new_v, new_s = [], [], []
    for l in range(DEPTH):
        mod3 = mod[l].reshape(MOD_ROWS, 1, 6 * D)
        proj = _in_projection(x, norm_mix_g[l], mod3, w_in[l].astype(BF16))

        s0 = jnp.concatenate([jnp.zeros((BATCH, 2, A_HEADS, A_DK, A_DK), F32), state_hgrn[:, l]], axis=0)
        o_dir, fin_dir = [], []
        for d in range(2):
            o_d, fin_d = _hgrn_scan(proj, lb_all[l, d].reshape(1, A_WIDTH), _pack_state(s0[:, d]),
                                    lvl, tri, l, d == 1)
            o_dir.append(o_d)
            fin_dir.append(_unpack_state(fin_d[:BATCH]))
        new_s.append(jnp.stack(fin_dir, axis=1))

        lam_init = 0.8 - 0.6 * math.exp(-0.3 * l)
        lam = (jnp.exp(jnp.sum(diff_lambda_q1[l] * diff_lambda_k1[l]))
               - jnp.exp(jnp.sum(diff_lambda_q2[l] * diff_lambda_k2[l])) + lam_init).reshape(1, 1)
        gq2 = jnp.tile(diff_q_norm_g[l], 2).reshape(1, LANES)
        gk2 = jnp.tile(diff_k_norm_g[l], 2).reshape(1, LANES)
        gs = diff_subln_g[l].reshape(1, LANES)
        b_ctx, k_l, v_l = _attn_ctx(proj, lam, gq2, gk2, gs, lam_init)
        b_smp = _attn_smp(proj, lam, cache_diff_k, cache_diff_v, cos, sin, gq2, gk2, gs, l, lam_init)
        b_out = jnp.concatenate([b_ctx, b_smp], axis=0)
        new_k.append(k_l)
        new_v.append(v_l)

        bias_full = jnp.repeat(cmlp_b_s[l].T, C_DG, axis=1)
        c_out = _chunk_mlp(proj, cmlp_ln_g[l], cmlp_ln_b[l], cmlp_w_s[l], bias_full)

        hg = jnp.tile(hgrn_norm_g[l], A_HEADS).reshape(1, A_WIDTH)
        wr_pad = jnp.pad(router_w[l], ((0, 0), (0, LANES - N_EXPERTS)))
        br_pad = jnp.pad(router_b[l], (0, LANES - N_EXPERTS), constant_values=-jnp.inf).reshape(1, LANES)
        x1, h2, idx_pad, gw_pad = _post_mix(o_dir[0], o_dir[1], proj, hg, hsel, b_out, c_out,
                                            w_out[l].astype(BF16), x, mod3, norm_ffn_g[l], wr_pad, br_pad)

        dest, row_tok, block_e, n_blocks = _route(idx_pad[:, :TOP_K])
        xs = h2.at[row_tok].get(mode='promise_in_bounds')
        yb = _moe_ffn(block_e, n_blocks, xs, moe_w_gate_up, moe_b_gate_up, moe_w_down, moe_b_down, l)
        yg = yb.at[dest.T.reshape(-1)].get(mode='promise_in_bounds').reshape(TOP_K, T, D)
        x = _combine(x1, yg, gw_pad, mod3)

    y_prompt = x[:T_CTX].reshape(BATCH, SEQ, D)
    y_sample = x[T_CTX:].reshape(DEC_BATCH, DEC_SEQ, D)
    return (y_prompt, y_sample, jnp.stack(new_k, axis=1), jnp.stack(new_v, axis=1), jnp.stack(new_s, axis=1))
```

```python
import functools
import math

import numpy as np
import jax
import jax.numpy as jnp
from jax import lax
from jax.experimental import pallas as pl
from jax.experimental.pallas import tpu as pltpu
from jax.experimental.pallas import tpu_sc as plsc

F32 = jnp.float32
BF16 = jnp.bfloat16

D = 1024
DEPTH = 2
BATCH, SEQ = 16, 256
DEC_BATCH, DEC_SEQ = 8, 1024
PAST = 512
GRID_W = 64
A_HEADS, A_DK = 4, 64
A_WIDTH = 256
B_HEADS, B_DK, B_DV = 4, 64, 128
B_WIDTH = 512
C_GROUPS, C_CHUNK, C_WIDTH, C_DG = 4, 128, 256, 64
IN_WIDTH = 5 * A_WIDTH + 3 * B_WIDTH + 2 * C_WIDTH
N_EXPERTS, TOP_K = 32, 4
SWIGLU_LIMIT, SWIGLU_ALPHA = 7.0, 1.702
ROPE_BASE = 10000.0
EPS = 1e-6

T_CTX = BATCH * SEQ
T_SMP = DEC_BATCH * DEC_SEQ
T = T_CTX + T_SMP
N_SEQ = BATCH + DEC_BATCH
MOD_ROWS = 16

TM = 256
N_TILES = T // TM
CTX_TILES = T_CTX // TM
SMP_TILES_PER_SEQ = DEC_SEQ // TM
LANES = 128
MOE_BM = 256
MOE_ROWS = T * TOP_K + N_EXPERTS * MOE_BM
MOE_BLOCKS = MOE_ROWS // MOE_BM
ROW_TILES = D // LANES
SC_CORES, SC_SUBCORES = 2, 16
SC_WORKERS = SC_CORES * SC_SUBCORES
SC_WIN = 64
VMEM_LIMIT = 56 * 1024 * 1024


def _cparams(sem):
    return pltpu.CompilerParams(dimension_semantics=sem, vmem_limit_bytes=VMEM_LIMIT)


def _mod_row(i):
    return jnp.where(i < CTX_TILES, 0, 1 + (i - CTX_TILES) // SMP_TILES_PER_SEQ)


def _split3(x):
    hi = x.astype(BF16)
    r = x - hi.astype(F32)
    mid = r.astype(BF16)
    lo = (r - mid.astype(F32)).astype(BF16)
    return hi, mid, lo


def _sel_dot(sel, x):
    hi, mid, lo = _split3(x)
    acc = jnp.dot(sel, lo, preferred_element_type=F32)
    acc = acc + jnp.dot(sel, mid, preferred_element_type=F32)
    return acc + jnp.dot(sel, hi, preferred_element_type=F32)


def _dot_sel(x, sel):
    hi, mid, lo = _split3(x)
    acc = jnp.dot(lo, sel, preferred_element_type=F32)
    acc = acc + jnp.dot(mid, sel, preferred_element_type=F32)
    return acc + jnp.dot(hi, sel, preferred_element_type=F32)


def _dot_nt(a, b):
    return lax.dot_general(a, b, (((1,), (1,)), ((), ())), preferred_element_type=F32)


def _dot_tn(a, b):
    return lax.dot_general(a, b, (((0,), (0,)), ((), ())), preferred_element_type=F32)


def _lane(shape):
    return lax.broadcasted_iota(jnp.int32, shape, len(shape) - 1)


def _mod_kernel(c_ref, w_ref, b_ref, o_ref):
    c = c_ref[...]
    s = c * jax.nn.sigmoid(c)
    o_ref[0] = jnp.dot(s.astype(BF16), w_ref[0].astype(BF16), preferred_element_type=F32) + b_ref[0]


def _modulation(cvec, w_mod, b_mod):
    tn = 1536
    return pl.pallas_call(
        _mod_kernel,
        grid=(DEPTH, 6 * D // tn),
        in_specs=[
            pl.BlockSpec((MOD_ROWS, D), lambda l, j: (0, 0)),
            pl.BlockSpec((1, D, tn), lambda l, j: (l, 0, j)),
            pl.BlockSpec((1, 1, tn), lambda l, j: (l, 0, j)),
        ],
        out_specs=pl.BlockSpec((1, MOD_ROWS, tn), lambda l, j: (l, 0, j)),
        out_shape=jax.ShapeDtypeStruct((DEPTH, MOD_ROWS, 6 * D), F32),
        compiler_params=_cparams(("arbitrary", "arbitrary")),
        name="modulation",
    )(cvec, w_mod, b_mod.reshape(DEPTH, 1, 6 * D))


def _inproj_kernel(x_ref, g_ref, shift_ref, scale_ref, w_ref, o_ref):
    x = x_ref[...]
    y = x * lax.rsqrt(jnp.mean(x * x, axis=-1, keepdims=True) + EPS) * g_ref[...]
    h = y * (1.0 + scale_ref[0]) + shift_ref[0]
    o_ref[...] = jnp.dot(h.astype(BF16), w_ref[...], preferred_element_type=F32)


def _in_projection(x, g, mod3, w_in_bf):
    return pl.pallas_call(
        _inproj_kernel,
        grid=(N_TILES,),
        in_specs=[
            pl.BlockSpec((TM, D), lambda i: (i, 0)),
            pl.BlockSpec((1, D), lambda i: (0, 0)),
            pl.BlockSpec((1, 1, D), lambda i: (_mod_row(i), 0, 0)),
            pl.BlockSpec((1, 1, D), lambda i: (_mod_row(i), 0, 1)),
            pl.BlockSpec((D, IN_WIDTH), lambda i: (0, 0)),
        ],
        out_specs=pl.BlockSpec((TM, IN_WIDTH), lambda i: (i, 0)),
        out_shape=jax.ShapeDtypeStruct((T, IN_WIDTH), F32),
        compiler_params=_cparams(("arbitrary",)),
        name="in_projection",
    )(x, g.reshape(1, D), mod3, mod3, w_in_bf)


HG_C = TM
HG_LEVELS = tuple(2 ** j for j in range(1, int(math.log2(HG_C)) + 1))


def _hgrn_tables():
    t = np.arange(HG_C)[:, None]
    s = np.arange(HG_C)[None, :]
    x = t ^ s
    lvl = np.zeros((HG_C, HG_C), np.int32)
    nz = x > 0
    lvl[nz] = np.floor(np.log2(x[nz])).astype(np.int32) + 1
    fwd = np.where(t >= s, lvl, -1).astype(np.int32)
    bwd = np.where(t <= s, lvl, -1).astype(np.int32)
    tri_f = (t >= s).astype(np.float32)
    tri_b = (t <= s).astype(np.float32)
    return np.stack([fwd, bwd]), np.stack([tri_f, tri_b])


def _block_ref(cum, m, idx):
    c, l = cum.shape
    if m >= 16:
        c3 = cum.reshape(c // m, m, l)
        r = c3[:, idx:idx + 1, :]
        return jnp.broadcast_to(r, (c // m, m, l)).reshape(c, l)
    c3 = cum.reshape(c // 8, 8, l)
    sub = lax.broadcasted_iota(jnp.int32, c3.shape, 1)
    out = None
    for j in range(8 // m - 1, -1, -1):
        cand = jnp.broadcast_to(c3[:, j * m + idx:j * m + idx + 1, :], c3.shape)
        out = cand if out is None else jnp.where(sub < (j + 1) * m, cand, out)
    return out.reshape(c, l)


def _hgrn_kernel(q_ref, z_ref, v_ref, lb_ref, s0_ref, lvl_ref, tri_ref, o_ref, fin_ref, st_ref, *, layer, rev):
    g = pl.program_id(0)
    first = jnp.logical_or(g < CTX_TILES, (g - CTX_TILES) % SMP_TILES_PER_SEQ == 0)

    @pl.when(first)
    def _():
        st_ref[...] = s0_ref[0]

    qr = q_ref[...]
    q = qr * jax.nn.sigmoid(qr) * (A_DK ** -0.5)
    z = z_ref[...]
    if layer == 0:
        lf = jnp.minimum(z, 0.0) - jnp.log(1.0 + jnp.exp(-jnp.abs(z)))
        k = jax.nn.sigmoid(-z)
    else:
        lbd = lb_ref[...]
        lf = jnp.log(lbd + (1.0 - lbd) * jax.nn.sigmoid(z))
        k = (1.0 - lbd) * jax.nn.sigmoid(-z)
    v = v_ref[...]
    cum = _sel_dot(tri_ref[0], lf)
    lvl = lvl_ref[0]
    last_row = 0 if rev else HG_C - 1

    for p in range(2):
        sl = slice(p * LANES, (p + 1) * LANES)
        q_p, k_p, v_p, cum_p = q[:, sl], k[:, sl], v[:, sl], cum[:, sl]
        lane = _lane((HG_C, LANES))
        head_masks = (lane < A_DK, lane >= A_DK)
        v_bf = v_p.astype(BF16)
        k_bf = k_p.astype(BF16)
        scores = []
        for hm in head_masks:
            qm = jnp.where(hm, q_p, 0.0).astype(BF16)
            scores.append(jnp.where(lvl == 0, _dot_nt(qm, k_bf), 0.0))
        for li, m in enumerate(HG_LEVELS):
            ref = _block_ref(cum_p, m, m // 2 if rev else m // 2 - 1)
            qd = q_p * jnp.exp(jnp.minimum(cum_p - ref, 0.0))
            kd = (k_p * jnp.exp(jnp.minimum(ref - cum_p, 0.0))).astype(BF16)
            for hi, hm in enumerate(head_masks):
                qm = jnp.where(hm, qd, 0.0).astype(BF16)
                scores[hi] = jnp.where(lvl == li + 1, _dot_nt(qm, kd), scores[hi])
        st = st_ref[p]
        o_intra = jnp.where(head_masks[0],
                            jnp.dot(scores[0].astype(BF16), v_bf, preferred_element_type=F32),
                            jnp.dot(scores[1].astype(BF16), v_bf, preferred_element_type=F32))
        q0 = (q_p * jnp.exp(cum_p)).astype(BF16)
        o_ref[:, sl] = o_intra + _dot_nt(q0, st.astype(BF16))
        last = cum_p[last_row:last_row + 1, :]
        ks = (k_p * jnp.exp(last - cum_p)).astype(BF16)
        upd = _dot_tn(v_bf, ks)
        r = lax.broadcasted_iota(jnp.int32, (LANES, LANES), 0)
        cl = lax.broadcasted_iota(jnp.int32, (LANES, LANES), 1)
        same_head = (r < A_DK) == (cl < A_DK)
        st_new = st * jnp.exp(last) + jnp.where(same_head, upd, 0.0)
        st_ref[p] = st_new
        fin_ref[0, p] = st_new


def _hgrn_seq(g):
    return jnp.where(g < CTX_TILES, g, CTX_TILES + (g - CTX_TILES) // SMP_TILES_PER_SEQ)


def _hgrn_blk(g, rev):
    if not rev:
        return g
    j = g - CTX_TILES
    return jnp.where(g < CTX_TILES, g,
                     CTX_TILES + (j // SMP_TILES_PER_SEQ) * SMP_TILES_PER_SEQ
                     + (SMP_TILES_PER_SEQ - 1 - j % SMP_TILES_PER_SEQ))


def _hgrn_scan(proj, lb_dir, s0_dir, lvl, tri, layer, rev):
    d = 1 if rev else 0
    blk = functools.partial(_hgrn_blk, rev=rev)
    return pl.pallas_call(
        functools.partial(_hgrn_kernel, layer=layer, rev=rev),
        grid=(N_TILES,),
        in_specs=[
            pl.BlockSpec((HG_C, A_WIDTH), lambda g: (blk(g), 0)),
            pl.BlockSpec((HG_C, A_WIDTH), lambda g: (blk(g), 1 + d)),
            pl.BlockSpec((HG_C, A_WIDTH), lambda g: (blk(g), 3)),
            pl.BlockSpec((1, A_WIDTH), lambda g: (0, 0)),
            pl.BlockSpec((1, 2, LANES, LANES), lambda g: (_hgrn_seq(g), 0, 0, 0)),
            pl.BlockSpec((1, HG_C, HG_C), lambda g: (d, 0, 0)),
            pl.BlockSpec((1, HG_C, HG_C), lambda g: (d, 0, 0)),
        ],
        out_specs=[
            pl.BlockSpec((HG_C, A_WIDTH), lambda g: (blk(g), 0)),
            pl.BlockSpec((1, 2, LANES, LANES), lambda g: (_hgrn_seq(g), 0, 0, 0)),
        ],
        out_shape=[
            jax.ShapeDtypeStruct((T, A_WIDTH), F32),
            jax.ShapeDtypeStruct((N_SEQ, 2, LANES, LANES), F32),
        ],
        scratch_shapes=[pltpu.VMEM((2, LANES, LANES), F32)],
        compiler_params=_cparams(("arbitrary",)),
        name=f"hgrn_scan_{'bwd' if rev else 'fwd'}",
    )(proj, proj, proj, lb_dir, s0_dir, lvl, tri)


def _pack_state(s):
    n = s.shape[0]
    st = jnp.swapaxes(s, -1, -2).reshape(n, 2, 2, A_DK, A_DK)
    z = jnp.zeros_like(st[:, :, 0])
    top = jnp.concatenate([st[:, :, 0], z], axis=-1)
    bot = jnp.concatenate([z, st[:, :, 1]], axis=-1)
    return jnp.concatenate([top, bot], axis=-2)


def _unpack_state(sp):
    n = sp.shape[0]
    h0 = sp[:, :, :A_DK, :A_DK]
    h1 = sp[:, :, A_DK:, A_DK:]
    st = jnp.stack([h0, h1], axis=2).reshape(n, A_HEADS, A_DK, A_DK)
    return jnp.swapaxes(st, -1, -2)


def _half_rms(x, g):
    lane = _lane(x.shape)
    lo = lane < B_DK
    xx = x * x
    ms0 = jnp.sum(jnp.where(lo, xx, 0.0), axis=-1, keepdims=True) * (1.0 / B_DK)
    ms1 = jnp.sum(jnp.where(lo, 0.0, xx), axis=-1, keepdims=True) * (1.0 / B_DK)
    inv = jnp.where(lo, lax.rsqrt(ms0 + EPS), lax.rsqrt(ms1 + EPS))
    return x * inv * g


def _rope(x, cos, sin_signed):
    lane = _lane(x.shape)
    first = (lane % 32) < 16
    rot = jnp.where(first, pltpu.roll(x, LANES - 16, 1), pltpu.roll(x, 16, 1))
    return x * cos + rot * sin_signed


def _diff_softmax_pv(q_bf, keys_bf, vals_bf, lam):
    lane = _lane(q_bf.shape)
    zero = jnp.zeros_like(q_bf)
    acc = None
    parts = []
    for mp in range(2):
        qm = jnp.where((lane < B_DK) == (mp == 0), q_bf, zero)
        s = [_dot_nt(qm, kk) for kk in keys_bf]
        mx = functools.reduce(jnp.maximum, [jnp.max(si, axis=-1, keepdims=True) for si in s])
        e = [jnp.exp(si - mx) for si in s]
        den = functools.reduce(lambda a, b: a + b, [jnp.sum(ei, axis=-1, keepdims=True) for ei in e])
        parts.append((e, 1.0 / den))
    (e0, r0), (e1, r1) = parts
    r1 = r1 * lam
    for i in range(len(keys_bf)):
        a = (e0[i] * r0 - e1[i] * r1).astype(BF16)
        pv = jnp.dot(a, vals_bf[i], preferred_element_type=F32)
        acc = pv if acc is None else acc + pv
    return acc


def _subln(o, g, lam_init):
    return o * lax.rsqrt(jnp.mean(o * o, axis=-1, keepdims=True) + EPS) * g * (1.0 - lam_init)


def _attn_ctx_kernel(lam_ref, q_ref, k_ref, v_ref, gq_ref, gk_ref, gs_ref, o_ref, nk_ref, nv_ref, *, lam_init):
    lam = lam_ref[0, 0]
    qn = _half_rms(q_ref[...], gq_ref[...]) * (B_DK ** -0.5)
    kn = _half_rms(k_ref[...], gk_ref[...])
    v = v_ref[...]
    nk_ref[0, 0, 0] = kn[:, :B_DK]
    nk_ref[0, 1, 0] = kn[:, B_DK:]
    nv_ref[0, 0] = v
    o = _diff_softmax_pv(qn.astype(BF16), [kn.astype(BF16)], [v.astype(BF16)], lam)
    o_ref[...] = _subln(o, gs_ref[...], lam_init)


def _attn_ctx(proj, lam, gq2, gk2, gs, lam_init):
    qcol, kcol, vcol = 5 * A_WIDTH // LANES, 5 * A_WIDTH // LANES + 4, 5 * A_WIDTH // LANES + 8
    return pl.pallas_call(
        functools.partial(_attn_ctx_kernel, lam_init=lam_init),
        grid=(BATCH, B_HEADS),
        in_specs=[
            pl.BlockSpec(memory_space=pltpu.SMEM),
            pl.BlockSpec((SEQ, LANES), lambda b, h: (b, qcol + h)),
            pl.BlockSpec((SEQ, LANES), lambda b, h: (b, kcol + h)),
            pl.BlockSpec((SEQ, LANES), lambda b, h: (b, vcol + h)),
            pl.BlockSpec((1, LANES), lambda b, h: (0, 0)),
            pl.BlockSpec((1, LANES), lambda b, h: (0, 0)),
            pl.BlockSpec((1, LANES), lambda b, h: (0, 0)),
        ],
        out_specs=[
            pl.BlockSpec((SEQ, LANES), lambda b, h: (b, h)),
            pl.BlockSpec((1, 2, 1, SEQ, B_DK), lambda b, h: (b, 0, h, 0, 0)),
            pl.BlockSpec((1, 1, SEQ, B_DV), lambda b, h: (b, h, 0, 0)),
        ],
        out_shape=[
            jax.ShapeDtypeStruct((T_CTX, B_WIDTH), F32),
            jax.ShapeDtypeStruct((BATCH, 2, B_HEADS, SEQ, B_DK), F32),
            jax.ShapeDtypeStruct((BATCH, B_HEADS, SEQ, B_DV), F32),
        ],
        compiler_params=_cparams(("arbitrary", "arbitrary")),
        name="diff_attention_ctx",
    )(lam, proj, proj, proj, gq2, gk2, gs)


ATT_TQ = 256


def _attn_smp_kernel(lam_ref, q_ref, k_ref, v_ref, ck_ref, cv_ref, cos_ref, sin_ref, gq_ref, gk_ref, gs_ref,
                     o_ref, qs_ref, ks_ref, *, lam_init):
    lam = lam_ref[0, 0]
    cos = cos_ref[...]
    sin = sin_ref[...]
    qn = _rope(_half_rms(q_ref[...], gq_ref[...]), cos, sin) * (B_DK ** -0.5)
    qs_ref[...] = qn.astype(BF16)
    ks_ref[...] = _rope(_half_rms(k_ref[...], gk_ref[...]), cos, sin).astype(BF16)
    ck = jnp.concatenate([ck_ref[0, 0, 0, 0], ck_ref[0, 0, 1, 0]], axis=-1).astype(BF16)
    cv = cv_ref[0, 0, 0].astype(BF16)
    v_bf = v_ref[...].astype(BF16)
    k_bf = ks_ref[...]
    g = gs_ref[...]

    def body(i, carry):
        r0 = pl.multiple_of(i * ATT_TQ, ATT_TQ)
        q_bf = qs_ref[pl.ds(r0, ATT_TQ), :]
        o = _diff_softmax_pv(q_bf, [k_bf, ck], [v_bf, cv], lam)
        o_ref[pl.ds(r0, ATT_TQ), :] = _subln(o, g, lam_init)
        return carry

    lax.fori_loop(0, DEC_SEQ // ATT_TQ, body, 0)


def _attn_smp(proj, lam, cache_k, cache_v, cos, sin, gq2, gk2, gs, layer, lam_init):
    qcol, kcol, vcol = 5 * A_WIDTH // LANES, 5 * A_WIDTH // LANES + 4, 5 * A_WIDTH // LANES + 8
    r0 = T_CTX // DEC_SEQ
    return pl.pallas_call(
        functools.partial(_attn_smp_kernel, lam_init=lam_init),
        grid=(DEC_BATCH, B_HEADS),
        in_specs=[
            pl.BlockSpec(memory_space=pltpu.SMEM),
            pl.BlockSpec((DEC_SEQ, LANES), lambda b, h: (r0 + b, qcol + h)),
            pl.BlockSpec((DEC_SEQ, LANES), lambda b, h: (r0 + b, kcol + h)),
            pl.BlockSpec((DEC_SEQ, LANES), lambda b, h: (r0 + b, vcol + h)),
            pl.BlockSpec((1, 1, 2, 1, PAST, B_DK), lambda b, h: (b, layer, 0, h, 0, 0)),
            pl.BlockSpec((1, 1, 1, PAST, B_DV), lambda b, h: (b, layer, h, 0, 0)),
            pl.BlockSpec((DEC_SEQ, LANES), lambda b, h: (0, 0)),
            pl.BlockSpec((DEC_SEQ, LANES), lambda b, h: (0, 0)),
            pl.BlockSpec((1, LANES), lambda b, h: (0, 0)),
            pl.BlockSpec((1, LANES), lambda b, h: (0, 0)),
            pl.BlockSpec((1, LANES), lambda b, h: (0, 0)),
        ],
        out_specs=pl.BlockSpec((DEC_SEQ, LANES), lambda b, h: (b, h)),
        out_shape=jax.ShapeDtypeStruct((T_SMP, B_WIDTH), F32),
        scratch_shapes=[pltpu.VMEM((DEC_SEQ, LANES), BF16), pltpu.VMEM((DEC_SEQ, LANES), BF16)],
        compiler_params=_cparams(("arbitrary", "arbitrary")),
        name="diff_attention_smp",
    )(lam, proj, proj, proj, cache_k, cache_v, cos, sin, gq2, gk2, gs)


def _rope_tables():
    n_rows = DEC_SEQ // GRID_W
    row = np.repeat(np.arange(n_rows), GRID_W).astype(np.float32)
    col = np.tile(np.arange(GRID_W), n_rows).astype(np.float32)
    half = B_DK // 2
    inv_freq = (ROPE_BASE ** (-jnp.arange(0, half, 2, dtype=F32) / half))
    row_ang = jnp.asarray(row)[:, None] * inv_freq
    col_ang = jnp.asarray(col)[:, None] * inv_freq
    ang = jnp.concatenate([row_ang, row_ang, col_ang, col_ang], axis=-1)
    ang = jnp.concatenate([ang, ang], axis=-1)
    sign = np.where((np.arange(LANES) % 32) < 16, -1.0, 1.0).astype(np.float32)
    return jnp.cos(ang), jnp.sin(ang) * sign


CM_ROWS = 512


def _gelu(x):
    return 0.5 * x * (1.0 + lax.erf(x * (2.0 ** -0.5)))


def _cmlp_kernel(u_ref, v_ref, g_ref, b_ref, ws_ref, bs_ref, o_ref):
    u = _gelu(u_ref[...])
    gv = _gelu(v_ref[...])
    mu = jnp.mean(gv, axis=-1, keepdims=True)
    dv = gv - mu
    var = jnp.mean(dv * dv, axis=-1, keepdims=True)
    vn = (dv * lax.rsqrt(var + EPS) * g_ref[...] + b_ref[...]).astype(BF16)
    lane = _lane((C_CHUNK, LANES))
    for c in range(CM_ROWS // C_CHUNK):
        rs = slice(c * C_CHUNK, (c + 1) * C_CHUNK)
        for p in range(2):
            cs = slice(p * LANES, (p + 1) * LANES)
            vp = vn[rs, cs]
            m0 = jnp.dot(ws_ref[2 * p].astype(BF16), vp, preferred_element_type=F32)
            m1 = jnp.dot(ws_ref[2 * p + 1].astype(BF16), vp, preferred_element_type=F32)
            mixed = jnp.where(lane < C_DG, m0, m1) + bs_ref[:, cs]
            o_ref[rs, cs] = u[rs, cs] * mixed


def _chunk_mlp(proj, ln_g, ln_b, w_s, bias_full):
    ucol = (5 * A_WIDTH + 3 * B_WIDTH) // C_WIDTH
    return pl.pallas_call(
        _cmlp_kernel,
        grid=(T // CM_ROWS,),
        in_specs=[
            pl.BlockSpec((CM_ROWS, C_WIDTH), lambda i: (i, ucol)),
            pl.BlockSpec((CM_ROWS, C_WIDTH), lambda i: (i, ucol + 1)),
            pl.BlockSpec((1, C_WIDTH), lambda i: (0, 0)),
            pl.BlockSpec((1, C_WIDTH), lambda i: (0, 0)),
            pl.BlockSpec((C_GROUPS, C_CHUNK, C_CHUNK), lambda i: (0, 0, 0)),
            pl.BlockSpec((C_CHUNK, C_WIDTH), lambda i: (0, 0)),
        ],
        out_specs=pl.BlockSpec((CM_ROWS, C_WIDTH), lambda i: (i, 0)),
        out_shape=jax.ShapeDtypeStruct((T, C_WIDTH), F32),
        compiler_params=_cparams(("arbitrary",)),
        name="chunk_mlp",
    )(proj, proj, ln_g.reshape(1, C_WIDTH), ln_b.reshape(1, C_WIDTH), w_s, bias_full)


def _postmix_kernel(of_ref, ob_ref, ag_ref, hg_ref, hsel_ref, b_ref, c_ref, w_ref, x_ref, gate1_ref, shift2_ref,
                    scale2_ref, g2_ref, wr_ref, br_ref, x1_ref, h2_ref, idx_ref, gw_ref):
    o = of_ref[...] + ob_ref[...]
    ms = _dot_sel(o * o, hsel_ref[...]) * (1.0 / A_DK)
    ag = ag_ref[...]
    a = o * lax.rsqrt(ms + EPS) * hg_ref[...] * (ag * jax.nn.sigmoid(ag))
    mixed = jnp.dot(a.astype(BF16), w_ref[0:A_WIDTH, :], preferred_element_type=F32)
    mixed = mixed + jnp.dot(b_ref[...].astype(BF16), w_ref[A_WIDTH:A_WIDTH + B_WIDTH, :], preferred_element_type=F32)
    mixed = mixed + jnp.dot(c_ref[...].astype(BF16), w_ref[A_WIDTH + B_WIDTH:, :], preferred_element_type=F32)
    x1 = x_ref[...] + gate1_ref[0] * mixed
    x1_ref[...] = x1
    y = x1 * lax.rsqrt(jnp.mean(x1 * x1, axis=-1, keepdims=True) + EPS) * g2_ref[...]
    h2 = y * (1.0 + scale2_ref[0]) + shift2_ref[0]
    for j in range(ROW_TILES):
        h2_ref[:, j, :] = h2[:, j * LANES:(j + 1) * LANES]
    logits = jnp.dot(h2, wr_ref[...], precision=lax.Precision.HIGHEST, preferred_element_type=F32) + br_ref[...]
    lane = _lane(logits.shape)
    idx_out = jnp.zeros(logits.shape, jnp.int32)
    val_out = jnp.zeros(logits.shape, F32)
    top0 = None
    den = None
    for kk in range(TOP_K):
        mx = jnp.max(logits, axis=-1, keepdims=True)
        am = jnp.min(jnp.where(logits == mx, lane, LANES), axis=-1, keepdims=True)
        if kk == 0:
            top0 = mx
        e = jnp.exp(mx - top0)
        den = e if den is None else den + e
        idx_out = jnp.where(lane == kk, am, idx_out)
        val_out = jnp.where(lane == kk, e, val_out)
        logits = jnp.where(lane == am, -jnp.inf, logits)
    idx_ref[...] = idx_out
    gw_ref[...] = val_out / den


def _post_mix(o_f, o_b, proj, hg, hsel, b_out, c_out, w_out_bf, x, mod3, g2, wr_pad, br_pad):
    tile = lambda w: pl.BlockSpec((TM, w), lambda i: (i, 0))
    const = lambda shape: pl.BlockSpec(shape, lambda i: tuple(0 for _ in shape))
    modspec = lambda j: pl.BlockSpec((1, 1, D), lambda i: (_mod_row(i), 0, j))
    return pl.pallas_call(
        _postmix_kernel,
        grid=(N_TILES,),
        in_specs=[
            tile(A_WIDTH), tile(A_WIDTH),
            pl.BlockSpec((TM, A_WIDTH), lambda i: (i, 4)),
            const((1, A_WIDTH)), const((A_WIDTH, A_WIDTH)),
            tile(B_WIDTH), tile(C_WIDTH),
            const((D, D)),
            tile(D),
            modspec(2), modspec(3), modspec(4),
            const((1, D)), const((D, LANES)), const((1, LANES)),
        ],
        out_specs=[tile(D), pl.BlockSpec((TM, ROW_TILES, LANES), lambda i: (i, 0, 0)), tile(LANES), tile(LANES)],
        out_shape=[
            jax.ShapeDtypeStruct((T, D), F32),
            jax.ShapeDtypeStruct((T, ROW_TILES, LANES), F32),
            jax.ShapeDtypeStruct((T, LANES), jnp.int32),
            jax.ShapeDtypeStruct((T, LANES), F32),
        ],
        compiler_params=_cparams(("arbitrary",)),
        name="post_mix_router",
    )(o_f, o_b, proj, hg, hsel, b_out, c_out, w_out_bf, x, mod3, mod3, mod3, g2.reshape(1, D), wr_pad, br_pad)


def _moe_kernel(be_ref, bv_ref, x_ref, wgu_ref, bgu_ref, wdn_ref, bdn_ref, o_ref, wgu_bf, wdn_bf, xb_ref):
    i = pl.program_id(0)
    prev = be_ref[jnp.maximum(i - 1, 0)]
    new_expert = jnp.logical_or(i == 0, be_ref[i] != prev)
    n_valid = bv_ref[i]

    @pl.when(new_expert)
    def _():
        wgu_bf[...] = wgu_ref[0, 0].astype(BF16)
        wdn_bf[...] = wdn_ref[0, 0].astype(BF16)

    @pl.when(n_valid > 0)
    def _():
        live = lax.broadcasted_iota(jnp.int32, (MOE_BM, LANES), 0) < n_valid
        for j in range(ROW_TILES):
            xb_ref[:, j * LANES:(j + 1) * LANES] = jnp.where(live, x_ref[:, j, :], 0.0).astype(BF16)
        gu = jnp.dot(xb_ref[...], wgu_bf[...], preferred_element_type=F32) + bgu_ref[0, 0]
        glu = jnp.minimum(gu[:, :D], SWIGLU_LIMIT)
        lin = jnp.clip(gu[:, D:], -SWIGLU_LIMIT, SWIGLU_LIMIT)
        act = glu * jax.nn.sigmoid(SWIGLU_ALPHA * glu) * (lin + 1.0)
        y = jnp.dot(act.astype(BF16), wdn_bf[...], preferred_element_type=F32) + bdn_ref[0, 0]
        for j in range(ROW_TILES):
            o_ref[:, j, :] = y[:, j * LANES:(j + 1) * LANES]

    @pl.when(n_valid <= 0)
    def _():
        o_ref[...] = jnp.zeros_like(o_ref)


def _moe_ffn(block_e, block_valid, xs, w_gu, b_gu, w_dn, b_dn, layer):
    row_spec = pl.BlockSpec((MOE_BM, ROW_TILES, LANES), lambda i, be, bv: (i, 0, 0))
    return pl.pallas_call(
        _moe_kernel,
        grid_spec=pltpu.PrefetchScalarGridSpec(
            num_scalar_prefetch=2,
            grid=(MOE_BLOCKS,),
            in_specs=[
                row_spec,
                pl.BlockSpec((1, 1, D, 2 * D), lambda i, be, bv: (layer, be[i], 0, 0)),
                pl.BlockSpec((1, 1, 1, 2 * D), lambda i, be, bv: (layer, be[i], 0, 0)),
                pl.BlockSpec((1, 1, D, D), lambda i, be, bv: (layer, be[i], 0, 0)),
                pl.BlockSpec((1, 1, 1, D), lambda i, be, bv: (layer, be[i], 0, 0)),
            ],
            out_specs=row_spec,
            scratch_shapes=[pltpu.VMEM((D, 2 * D), BF16), pltpu.VMEM((D, D), BF16), pltpu.VMEM((MOE_BM, D), BF16)],
        ),
        out_shape=jax.ShapeDtypeStruct((MOE_ROWS, ROW_TILES, LANES), F32),
        compiler_params=_cparams(("arbitrary",)),
        name="moe_expert_ffn",
    )(block_e, block_valid, xs, w_gu, b_gu.reshape(DEPTH, N_EXPERTS, 1, 2 * D), w_dn,
      b_dn.reshape(DEPTH, N_EXPERTS, 1, D))


def _combine_kernel(x1_ref, y_ref, gw_ref, gate2_ref, o_ref):
    gw = gw_ref[...]
    for j in range(ROW_TILES):
        cs = slice(j * LANES, (j + 1) * LANES)
        acc = None
        for kk in range(TOP_K):
            term = y_ref[kk, :, j, :] * gw[:, kk:kk + 1]
            acc = term if acc is None else acc + term
        o_ref[:, cs] = x1_ref[:, cs] + gate2_ref[0, :, cs] * acc


def _combine(x1, yg, gw, mod3):
    return pl.pallas_call(
        _combine_kernel,
        grid=(N_TILES,),
        in_specs=[
            pl.BlockSpec((TM, D), lambda i: (i, 0)),
            pl.BlockSpec((TOP_K, TM, ROW_TILES, LANES), lambda i: (0, i, 0, 0)),
            pl.BlockSpec((TM, LANES), lambda i: (i, 0)),
            pl.BlockSpec((1, 1, D), lambda i: (_mod_row(i), 0, 5)),
        ],
        out_specs=pl.BlockSpec((TM, D), lambda i: (i, 0)),
        out_shape=jax.ShapeDtypeStruct((T, D), F32),
        compiler_params=_cparams(("arbitrary",)),
        name="moe_combine",
    )(x1, yg, gw, mod3)


def _sc_mesh():
    return plsc.VectorSubcoreMesh(core_axis_name="c", subcore_axis_name="s")


def _sc_worker():
    return lax.axis_index("s") * SC_CORES + lax.axis_index("c")


def _sc_dispatch(h2t, dest_km):
    per_w = T // SC_WORKERS

    @functools.partial(
        pl.kernel, mesh=_sc_mesh(),
        out_type=jax.ShapeDtypeStruct((MOE_ROWS, ROW_TILES, LANES), F32),
        scratch_types=[pltpu.VMEM((SC_WIN,), jnp.int32), pltpu.VMEM((SC_WIN, ROW_TILES, LANES), F32),
                       pltpu.SemaphoreType.DMA],
    )
    def run(h_hbm, d_hbm, o_hbm, idx_v, rows_v, sem):
        w0 = _sc_worker() * per_w

        @pl.loop(0, per_w // SC_WIN)
        def _(w):
            base = pl.multiple_of(w0 + w * SC_WIN, SC_WIN)
            pltpu.sync_copy(h_hbm.at[pl.ds(base, SC_WIN)], rows_v)
            for kk in range(TOP_K):
                pltpu.sync_copy(d_hbm.at[pl.ds(kk * T + base, SC_WIN)], idx_v)
                pltpu.async_copy(rows_v, o_hbm.at[idx_v], sem).wait()

    return run(h2t, dest_km)


def _sc_gather(yb, dest_km):
    n = TOP_K * T
    per_w = n // SC_WORKERS

    @functools.partial(
        pl.kernel, mesh=_sc_mesh(),
        out_type=jax.ShapeDtypeStruct((n, ROW_TILES, LANES), F32),
        scratch_types=[pltpu.VMEM((SC_WIN,), jnp.int32), pltpu.VMEM((SC_WIN, ROW_TILES, LANES), F32),
                       pltpu.SemaphoreType.DMA],
    )
    def run(y_hbm, d_hbm, o_hbm, idx_v, rows_v, sem):
        w0 = _sc_worker() * per_w

        @pl.loop(0, per_w // SC_WIN)
        def _(w):
            base = pl.multiple_of(w0 + w * SC_WIN, SC_WIN)
            pltpu.sync_copy(d_hbm.at[pl.ds(base, SC_WIN)], idx_v)
            pltpu.async_copy(y_hbm.at[idx_v], rows_v, sem).wait()
            pltpu.sync_copy(rows_v, o_hbm.at[pl.ds(base, SC_WIN)])

    return run(yb, dest_km)


def _route(top_idx):
    flat_e = top_idx.reshape(-1)
    onehot = (flat_e[:, None] == jnp.arange(N_EXPERTS, dtype=jnp.int32)[None, :]).astype(jnp.int32)
    csum = jnp.cumsum(onehot, axis=0)
    counts = csum[-1]
    pos = jnp.take_along_axis(csum, flat_e[:, None], axis=1)[:, 0] - 1
    padded = (counts + MOE_BM - 1) // MOE_BM * MOE_BM
    pad_end = jnp.cumsum(padded)
    pad_start = pad_end - padded
    dest = pad_start[flat_e] + pos
    n_blocks = pad_end[-1] // MOE_BM
    blk_start = jnp.arange(MOE_BLOCKS, dtype=jnp.int32) * MOE_BM
    blk_clamped = jnp.minimum(blk_start, (n_blocks - 1) * MOE_BM)
    block_e = jnp.sum((pad_end[None, :] <= blk_clamped[:, None]).astype(jnp.int32), axis=1)
    live_end = (pad_start + counts)[block_e]
    block_valid = jnp.where(blk_start < pad_end[-1], jnp.clip(live_end - blk_start, 0, MOE_BM), 0)
    dest_km = dest.reshape(T, TOP_K).T.reshape(-1)
    return dest_km, block_e, block_valid.astype(jnp.int32)


def kernel(x_prompt, x_sample, c, cache_diff_k, cache_diff_v, state_hgrn, c_ctx, norm_mix_g, norm_ffn_g, w_mod, b_mod, w_in, w_out, hgrn_lower_bounds, hgrn_norm_g, diff_q_norm_g, diff_k_norm_g, diff_lambda_q1, diff_lambda_k1, diff_lambda_q2, diff_lambda_k2, diff_subln_g, cmlp_ln_g, cmlp_ln_b, cmlp_w_s, cmlp_b_s, router_w, router_b, moe_w_gate_up, moe_b_gate_up, moe_w_down, moe_b_down):
    x = jnp.concatenate([x_prompt.reshape(T_CTX, D), x_sample.reshape(T_SMP, D)], axis=0)
    cvec = jnp.concatenate([c_ctx[None, :], c, jnp.zeros((MOD_ROWS - 1 - DEC_BATCH, D), F32)], axis=0)
    mod = _modulation(cvec, w_mod, b_mod)

    lvl_np, tri_np = _hgrn_tables()
    lvl = jnp.asarray(lvl_np)
    tri = jnp.asarray(tri_np, dtype=BF16)
    cos, sin = _rope_tables()
    hsel = jnp.asarray(np.kron(np.eye(A_HEADS), np.ones((A_DK, A_DK))), dtype=BF16)
    sm = jax.nn.softmax(hgrn_lower_bounds.astype(F32), axis=0)
    lb_all = jnp.cumsum(sm, axis=0) - sm[0]

    new_k, new_v, new_s = [], [], []
    for l in range(DEPTH):
        mod3 = mod[l].reshape(MOD_ROWS, 1, 6 * D)
        proj = _in_projection(x, norm_mix_g[l], mod3, w_in[l].astype(BF16))

        s0 = jnp.concatenate([jnp.zeros((BATCH, 2, A_HEADS, A_DK, A_DK), F32), state_hgrn[:, l]], axis=0)
        o_dir, fin_dir = [], []
        for d in range(2):
            o_d, fin_d = _hgrn_scan(proj, lb_all[l, d].reshape(1, A_WIDTH), _pack_state(s0[:, d]),
                                    lvl, tri, l, d == 1)
            o_dir.append(o_d)
            fin_dir.append(_unpack_state(fin_d[:BATCH]))
        new_s.append(jnp.stack(fin_dir, axis=1))

        lam_init = 0.8 - 0.6 * math.exp(-0.3 * l)
        lam = (jnp.exp(jnp.sum(diff_lambda_q1[l] * diff_lambda_k1[l]))
               - jnp.exp(jnp.sum(diff_lambda_q2[l] * diff_lambda_k2[l])) + lam_init).reshape(1, 1)
        gq2 = jnp.tile(diff_q_norm_g[l], 2).reshape(1, LANES)
        gk2 = jnp.tile(diff_k_norm_g[l], 2).reshape(1, LANES)
        gs = diff_subln_g[l].reshape(1, LANES)
        b_ctx, k_l, v_l = _attn_ctx(proj, lam, gq2, gk2, gs, lam_init)
        b_smp = _attn_smp(proj, lam, cache_diff_k, cache_diff_v, cos, sin, gq2, gk2, gs, l, lam_init)
        b_out = jnp.concatenate([b_ctx, b_smp], axis=0)
        new_k.append(k_l)
        new_v.append(v_l)

        bias_full = jnp.repeat(cmlp_b_s[l].T, C_DG, axis=1)
        c_out = _chunk_mlp(proj, cmlp_ln_g[l], cmlp_ln_b[l], cmlp_w_s[l], bias_full)

        hg = jnp.tile(hgrn_norm_g[l], A_HEADS).reshape(1, A_WIDTH)
        wr_pad = jnp.pad(router_w[l], ((0, 0), (0, LANES - N_EXPERTS)))
        br_pad = jnp.pad(router_b[l], (0, LANES - N_EXPERTS), constant_values=-jnp.inf).reshape(1, LANES)
        x1, h2, idx_pad, gw_pad = _post_mix(o_dir[0], o_dir[1], proj, hg, hsel, b_out, c_out,
                                            w_out[l].astype(BF16), x, mod3, norm_ffn_g[l], wr_pad, br_pad)

        dest_km, block_e, block_valid = _route(idx_pad[:, :TOP_K])
        xs = _sc_dispatch(h2, dest_km)
        yb = _moe_ffn(block_e, block_valid, xs, moe_w_gate_up, moe_b_gate_up, moe_w_down, moe_b_down, l)
        yg = _sc_gather(yb, dest_km).reshape(TOP_K, T, ROW_TILES, LANES)
        x = _combine(x1, yg, gw_pad, mod3)

    y_prompt = x[:T_CTX].reshape(BATCH, SEQ, D)
    y_sample = x[T_CTX:].reshape(DEC_BATCH, DEC_SEQ, D)
    return (y_prompt, y_sample, jnp.stack(new_k, axis=1), jnp.stack(new_v, axis=1), jnp.stack(new_s, axis=1))
```

```python
import functools
import math

import numpy as np
import jax
import jax.numpy as jnp
from jax import lax
from jax.experimental import pallas as pl
from jax.experimental.pallas import tpu as pltpu
from jax.experimental.pallas import tpu_sc as plsc

F32 = jnp.float32
BF16 = jnp.bfloat16

D = 1024
DEPTH = 2
BATCH, SEQ = 16, 256
DEC_BATCH, DEC_SEQ = 8, 1024
PAST = 512
GRID_W = 64
A_HEADS, A_DK = 4, 64
A_WIDTH = 256
B_HEADS, B_DK, B_DV = 4, 64, 128
B_WIDTH = 512
C_GROUPS, C_CHUNK, C_WIDTH, C_DG = 4, 128, 256, 64
IN_WIDTH = 5 * A_WIDTH + 3 * B_WIDTH + 2 * C_WIDTH
N_EXPERTS, TOP_K = 32, 4
SWIGLU_LIMIT, SWIGLU_ALPHA = 7.0, 1.702
ROPE_BASE = 10000.0
EPS = 1e-6

T_CTX = BATCH * SEQ
T_SMP = DEC_BATCH * DEC_SEQ
T = T_CTX + T_SMP
N_SEQ = BATCH + DEC_BATCH
MOD_ROWS = 16

TM = 256
N_TILES = T // TM
CTX_TILES = T_CTX // TM
SMP_TILES_PER_SEQ = DEC_SEQ // TM
LANES = 128
MOE_BM = 256
MOE_ROWS = T * TOP_K + N_EXPERTS * MOE_BM
MOE_BLOCKS = MOE_ROWS // MOE_BM
ROW_TILES = D // LANES
SC_CORES, SC_SUBCORES = 2, 16
SC_WORKERS = SC_CORES * SC_SUBCORES
SC_WIN = 64
VMEM_LIMIT = 56 * 1024 * 1024


def _cparams(sem):
    return pltpu.CompilerParams(dimension_semantics=sem, vmem_limit_bytes=VMEM_LIMIT)


def _mod_row(i):
    return jnp.where(i < CTX_TILES, 0, 1 + (i - CTX_TILES) // SMP_TILES_PER_SEQ)


def _split3(x):
    hi = x.astype(BF16)
    r = x - hi.astype(F32)
    mid = r.astype(BF16)
    lo = (r - mid.astype(F32)).astype(BF16)
    return hi, mid, lo


def _sel_dot(sel, x):
    hi, mid, lo = _split3(x)
    acc = jnp.dot(sel, lo, preferred_element_type=F32)
    acc = acc + jnp.dot(sel, mid, preferred_element_type=F32)
    return acc + jnp.dot(sel, hi, preferred_element_type=F32)


def _dot_sel(x, sel):
    hi, mid, lo = _split3(x)
    acc = jnp.dot(lo, sel, preferred_element_type=F32)
    acc = acc + jnp.dot(mid, sel, preferred_element_type=F32)
    return acc + jnp.dot(hi, sel, preferred_element_type=F32)


def _dot_nt(a, b):
    return lax.dot_general(a, b, (((1,), (1,)), ((), ())), preferred_element_type=F32)


def _dot_tn(a, b):
    return lax.dot_general(a, b, (((0,), (0,)), ((), ())), preferred_element_type=F32)


def _lane(shape):
    return lax.broadcasted_iota(jnp.int32, shape, len(shape) - 1)


def _mod_kernel(c_ref, w_ref, b_ref, o_ref):
    c = c_ref[...]
    s = c * jax.nn.sigmoid(c)
    o_ref[0] = jnp.dot(s.astype(BF16), w_ref[0].astype(BF16), preferred_element_type=F32) + b_ref[0]


def _modulation(cvec, w_mod, b_mod):
    tn = 1536
    return pl.pallas_call(
        _mod_kernel,
        grid=(DEPTH, 6 * D // tn),
        in_specs=[
            pl.BlockSpec((MOD_ROWS, D), lambda l, j: (0, 0)),
            pl.BlockSpec((1, D, tn), lambda l, j: (l, 0, j)),
            pl.BlockSpec((1, 1, tn), lambda l, j: (l, 0, j)),
        ],
        out_specs=pl.BlockSpec((1, MOD_ROWS, tn), lambda l, j: (l, 0, j)),
        out_shape=jax.ShapeDtypeStruct((DEPTH, MOD_ROWS, 6 * D), F32),
        compiler_params=_cparams(("arbitrary", "arbitrary")),
        name="modulation",
    )(cvec, w_mod, b_mod.reshape(DEPTH, 1, 6 * D))


def _inproj_kernel(x_ref, g_ref, shift_ref, scale_ref, w_ref, o_ref):
    x = x_ref[...]
    y = x * lax.rsqrt(jnp.mean(x * x, axis=-1, keepdims=True) + EPS) * g_ref[...]
    h = y * (1.0 + scale_ref[0]) + shift_ref[0]
    o_ref[...] = jnp.dot(h.astype(BF16), w_ref[...], preferred_element_type=F32)


def _in_projection(x, g, mod3, w_in_bf):
    return pl.pallas_call(
        _inproj_kernel,
        grid=(N_TILES,),
        in_specs=[
            pl.BlockSpec((TM, D), lambda i: (i, 0)),
            pl.BlockSpec((1, D), lambda i: (0, 0)),
            pl.BlockSpec((1, 1, D), lambda i: (_mod_row(i), 0, 0)),
            pl.BlockSpec((1, 1, D), lambda i: (_mod_row(i), 0, 1)),
            pl.BlockSpec((D, IN_WIDTH), lambda i: (0, 0)),
        ],
        out_specs=pl.BlockSpec((TM, IN_WIDTH), lambda i: (i, 0)),
        out_shape=jax.ShapeDtypeStruct((T, IN_WIDTH), F32),
        compiler_params=_cparams(("arbitrary",)),
        name="in_projection",
    )(x, g.reshape(1, D), mod3, mod3, w_in_bf)


HG_C = TM
HG_LEVELS = tuple(2 ** j for j in range(1, int(math.log2(HG_C)) + 1))


def _hgrn_tables():
    t = np.arange(HG_C)[:, None]
    s = np.arange(HG_C)[None, :]
    x = t ^ s
    lvl = np.zeros((HG_C, HG_C), np.int32)
    nz = x > 0
    lvl[nz] = np.floor(np.log2(x[nz])).astype(np.int32) + 1
    fwd = np.where(t >= s, lvl, -1).astype(np.int32)
    bwd = np.where(t <= s, lvl, -1).astype(np.int32)
    tri_f = (t >= s).astype(np.float32)
    tri_b = (t <= s).astype(np.float32)
    return np.stack([fwd, bwd]), np.stack([tri_f, tri_b])


def _block_ref(cum, m, idx):
    c, l = cum.shape
    if m >= 16:
        c3 = cum.reshape(c // m, m, l)
        r = c3[:, idx:idx + 1, :]
        return jnp.broadcast_to(r, (c // m, m, l)).reshape(c, l)
    c3 = cum.reshape(c // 8, 8, l)
    sub = lax.broadcasted_iota(jnp.int32, c3.shape, 1)
    out = None
    for j in range(8 // m - 1, -1, -1):
        cand = jnp.broadcast_to(c3[:, j * m + idx:j * m + idx + 1, :], c3.shape)
        out = cand if out is None else jnp.where(sub < (j + 1) * m, cand, out)
    return out.reshape(c, l)


def _hgrn_kernel(q_ref, z_ref, v_ref, lb_ref, s0_ref, lvl_ref, tri_ref, o_ref, fin_ref, st_ref, *, layer, rev):
    g = pl.program_id(0)
    first = jnp.logical_or(g < CTX_TILES, (g - CTX_TILES) % SMP_TILES_PER_SEQ == 0)

    @pl.when(first)
    def _():
        st_ref[...] = s0_ref[0]

    qr = q_ref[...]
    q = qr * jax.nn.sigmoid(qr) * (A_DK ** -0.5)
    z = z_ref[...]
    if layer == 0:
        lf = jnp.minimum(z, 0.0) - jnp.log(1.0 + jnp.exp(-jnp.abs(z)))
        k = jax.nn.sigmoid(-z)
    else:
        lbd = lb_ref[...]
        lf = jnp.log(lbd + (1.0 - lbd) * jax.nn.sigmoid(z))
        k = (1.0 - lbd) * jax.nn.sigmoid(-z)
    v = v_ref[...]
    cum = _sel_dot(tri_ref[0], lf)
    lvl = lvl_ref[0]
    last_row = 0 if rev else HG_C - 1

    for p in range(2):
        sl = slice(p * LANES, (p + 1) * LANES)
        q_p, k_p, v_p, cum_p = q[:, sl], k[:, sl], v[:, sl], cum[:, sl]
        lane = _lane((HG_C, LANES))
        head_masks = (lane < A_DK, lane >= A_DK)
        v_bf = v_p.astype(BF16)
        k_bf = k_p.astype(BF16)
        scores = []
        for hm in head_masks:
            qm = jnp.where(hm, q_p, 0.0).astype(BF16)
            scores.append(jnp.where(lvl == 0, _dot_nt(qm, k_bf), 0.0))
        for li, m in enumerate(HG_LEVELS):
            ref = _block_ref(cum_p, m, m // 2 if rev else m // 2 - 1)
            qd = q_p * jnp.exp(jnp.minimum(cum_p - ref, 0.0))
            kd = (k_p * jnp.exp(jnp.minimum(ref - cum_p, 0.0))).astype(BF16)
            for hi, hm in enumerate(head_masks):
                qm = jnp.where(hm, qd, 0.0).astype(BF16)
                scores[hi] = jnp.where(lvl == li + 1, _dot_nt(qm, kd), scores[hi])
        st = st_ref[p]
        o_intra = jnp.where(head_masks[0],
                            jnp.dot(scores[0].astype(BF16), v_bf, preferred_element_type=F32),
                            jnp.dot(scores[1].astype(BF16), v_bf, preferred_element_type=F32))
        q0 = (q_p * jnp.exp(cum_p)).astype(BF16)
        o_ref[:, sl] = o_intra + _dot_nt(q0, st.astype(BF16))
        last = cum_p[last_row:last_row + 1, :]
        ks = (k_p * jnp.exp(last - cum_p)).astype(BF16)
        upd = _dot_tn(v_bf, ks)
        r = lax.broadcasted_iota(jnp.int32, (LANES, LANES), 0)
        cl = lax.broadcasted_iota(jnp.int32, (LANES, LANES), 1)
        same_head = (r < A_DK) == (cl < A_DK)
        st_new = st * jnp.exp(last) + jnp.where(same_head, upd, 0.0)
        st_ref[p] = st_new
        fin_ref[0, p] = st_new


def _hgrn_seq(g):
    return jnp.where(g < CTX_TILES, g, CTX_TILES + (g - CTX_TILES) // SMP_TILES_PER_SEQ)


def _hgrn_blk(g, rev):
    if not rev:
        return g
    j = g - CTX_TILES
    return jnp.where(g < CTX_TILES, g,
                     CTX_TILES + (j // SMP_TILES_PER_SEQ) * SMP_TILES_PER_SEQ
                     + (SMP_TILES_PER_SEQ - 1 - j % SMP_TILES_PER_SEQ))


def _hgrn_scan(proj, lb_dir, s0_dir, lvl, tri, layer, rev):
    d = 1 if rev else 0
    blk = functools.partial(_hgrn_blk, rev=rev)
    return pl.pallas_call(
        functools.partial(_hgrn_kernel, layer=layer, rev=rev),
        grid=(N_TILES,),
        in_specs=[
            pl.BlockSpec((HG_C, A_WIDTH), lambda g: (blk(g), 0)),
            pl.BlockSpec((HG_C, A_WIDTH), lambda g: (blk(g), 1 + d)),
            pl.BlockSpec((HG_C, A_WIDTH), lambda g: (blk(g), 3)),
            pl.BlockSpec((1, A_WIDTH), lambda g: (0, 0)),
            pl.BlockSpec((1, 2, LANES, LANES), lambda g: (_hgrn_seq(g), 0, 0, 0)),
            pl.BlockSpec((1, HG_C, HG_C), lambda g: (d, 0, 0)),
            pl.BlockSpec((1, HG_C, HG_C), lambda g: (d, 0, 0)),
        ],
        out_specs=[
            pl.BlockSpec((HG_C, A_WIDTH), lambda g: (blk(g), 0)),
            pl.BlockSpec((1, 2, LANES, LANES), lambda g: (_hgrn_seq(g), 0, 0, 0)),
        ],
        out_shape=[
            jax.ShapeDtypeStruct((T, A_WIDTH), F32),
            jax.ShapeDtypeStruct((N_SEQ, 2, LANES, LANES), F32),
        ],
        scratch_shapes=[pltpu.VMEM((2, LANES, LANES), F32)],
        compiler_params=_cparams(("arbitrary",)),
        name=f"hgrn_scan_{'bwd' if rev else 'fwd'}",
    )(proj, proj, proj, lb_dir, s0_dir, lvl, tri)


def _pack_state(s):
    n = s.shape[0]
    st = jnp.swapaxes(s, -1, -2).reshape(n, 2, 2, A_DK, A_DK)
    z = jnp.zeros_like(st[:, :, 0])
    top = jnp.concatenate([st[:, :, 0], z], axis=-1)
    bot = jnp.concatenate([z, st[:, :, 1]], axis=-1)
    return jnp.concatenate([top, bot], axis=-2)


def _unpack_state(sp):
    n = sp.shape[0]
    h0 = sp[:, :, :A_DK, :A_DK]
    h1 = sp[:, :, A_DK:, A_DK:]
    st = jnp.stack([h0, h1], axis=2).reshape(n, A_HEADS, A_DK, A_DK)
    return jnp.swapaxes(st, -1, -2)


def _half_rms(x, g):
    lane = _lane(x.shape)
    lo = lane < B_DK
    xx = x * x
    ms0 = jnp.sum(jnp.where(lo, xx, 0.0), axis=-1, keepdims=True) * (1.0 / B_DK)
    ms1 = jnp.sum(jnp.where(lo, 0.0, xx), axis=-1, keepdims=True) * (1.0 / B_DK)
    inv = jnp.where(lo, lax.rsqrt(ms0 + EPS), lax.rsqrt(ms1 + EPS))
    return x * inv * g


def _rope(x, cos, sin_signed):
    lane = _lane(x.shape)
    first = (lane % 32) < 16
    rot = jnp.where(first, pltpu.roll(x, LANES - 16, 1), pltpu.roll(x, 16, 1))
    return x * cos + rot * sin_signed


def _diff_softmax_pv(q_bf, keys_bf, vals_bf, lam):
    lane = _lane(q_bf.shape)
    zero = jnp.zeros_like(q_bf)
    acc = None
    parts = []
    for mp in range(2):
        qm = jnp.where((lane < B_DK) == (mp == 0), q_bf, zero)
        s = [_dot_nt(qm, kk) for kk in keys_bf]
        mx = functools.reduce(jnp.maximum, [jnp.max(si, axis=-1, keepdims=True) for si in s])
        e = [jnp.exp(si - mx) for si in s]
        den = functools.reduce(lambda a, b: a + b, [jnp.sum(ei, axis=-1, keepdims=True) for ei in e])
        parts.append((e, 1.0 / den))
    (e0, r0), (e1, r1) = parts
    r1 = r1 * lam
    for i in range(len(keys_bf)):
        a = (e0[i] * r0 - e1[i] * r1).astype(BF16)
        pv = jnp.dot(a, vals_bf[i], preferred_element_type=F32)
        acc = pv if acc is None else acc + pv
    return acc


def _subln(o, g, lam_init):
    return o * lax.rsqrt(jnp.mean(o * o, axis=-1, keepdims=True) + EPS) * g * (1.0 - lam_init)


def _attn_ctx_kernel(lam_ref, q_ref, k_ref, v_ref, gq_ref, gk_ref, gs_ref, o_ref, nk_ref, nv_ref, *, lam_init):
    lam = lam_ref[0, 0]
    qn = _half_rms(q_ref[...], gq_ref[...]) * (B_DK ** -0.5)
    kn = _half_rms(k_ref[...], gk_ref[...])
    v = v_ref[...]
    nk_ref[0, 0, 0] = kn[:, :B_DK]
    nk_ref[0, 1, 0] = kn[:, B_DK:]
    nv_ref[0, 0] = v
    o = _diff_softmax_pv(qn.astype(BF16), [kn.astype(BF16)], [v.astype(BF16)], lam)
    o_ref[...] = _subln(o, gs_ref[...], lam_init)


def _attn_ctx(proj, lam, gq2, gk2, gs, lam_init):
    qcol, kcol, vcol = 5 * A_WIDTH // LANES, 5 * A_WIDTH // LANES + 4, 5 * A_WIDTH // LANES + 8
    return pl.pallas_call(
        functools.partial(_attn_ctx_kernel, lam_init=lam_init),
        grid=(BATCH, B_HEADS),
        in_specs=[
            pl.BlockSpec(memory_space=pltpu.SMEM),
            pl.BlockSpec((SEQ, LANES), lambda b, h: (b, qcol + h)),
            pl.BlockSpec((SEQ, LANES), lambda b, h: (b, kcol + h)),
            pl.BlockSpec((SEQ, LANES), lambda b, h: (b, vcol + h)),
            pl.BlockSpec((1, LANES), lambda b, h: (0, 0)),
            pl.BlockSpec((1, LANES), lambda b, h: (0, 0)),
            pl.BlockSpec((1, LANES), lambda b, h: (0, 0)),
        ],
        out_specs=[
            pl.BlockSpec((SEQ, LANES), lambda b, h: (b, h)),
            pl.BlockSpec((1, 2, 1, SEQ, B_DK), lambda b, h: (b, 0, h, 0, 0)),
            pl.BlockSpec((1, 1, SEQ, B_DV), lambda b, h: (b, h, 0, 0)),
        ],
        out_shape=[
            jax.ShapeDtypeStruct((T_CTX, B_WIDTH), F32),
            jax.ShapeDtypeStruct((BATCH, 2, B_HEADS, SEQ, B_DK), F32),
            jax.ShapeDtypeStruct((BATCH, B_HEADS, SEQ, B_DV), F32),
        ],
        compiler_params=_cparams(("arbitrary", "arbitrary")),
        name="diff_attention_ctx",
    )(lam, proj, proj, proj, gq2, gk2, gs)


ATT_TQ = 256


def _attn_smp_kernel(lam_ref, q_ref, k_ref, v_ref, ck_ref, cv_ref, cos_ref, sin_ref, gq_ref, gk_ref, gs_ref,
                     o_ref, qs_ref, ks_ref, *, lam_init):
    lam = lam_ref[0, 0]
    cos = cos_ref[...]
    sin = sin_ref[...]
    qn = _rope(_half_rms(q_ref[...], gq_ref[...]), cos, sin) * (B_DK ** -0.5)
    qs_ref[...] = qn.astype(BF16)
    ks_ref[...] = _rope(_half_rms(k_ref[...], gk_ref[...]), cos, sin).astype(BF16)
    ck = jnp.concatenate([ck_ref[0, 0, 0, 0], ck_ref[0, 0, 1, 0]], axis=-1).astype(BF16)
    cv = cv_ref[0, 0, 0].astype(BF16)
    v_bf = v_ref[...].astype(BF16)
    k_bf = ks_ref[...]
    g = gs_ref[...]

    def body(i, carry):
        r0 = pl.multiple_of(i * ATT_TQ, ATT_TQ)
        q_bf = qs_ref[pl.ds(r0, ATT_TQ), :]
        o = _diff_softmax_pv(q_bf, [k_bf, ck], [v_bf, cv], lam)
        o_ref[pl.ds(r0, ATT_TQ), :] = _subln(o, g, lam_init)
        return carry

    lax.fori_loop(0, DEC_SEQ // ATT_TQ, body, 0)


def _attn_smp(proj, lam, cache_k, cache_v, cos, sin, gq2, gk2, gs, layer, lam_init):
    qcol, kcol, vcol = 5 * A_WIDTH // LANES, 5 * A_WIDTH // LANES + 4, 5 * A_WIDTH // LANES + 8
    r0 = T_CTX // DEC_SEQ
    return pl.pallas_call(
        functools.partial(_attn_smp_kernel, lam_init=lam_init),
        grid=(DEC_BATCH, B_HEADS),
        in_specs=[
            pl.BlockSpec(memory_space=pltpu.SMEM),
            pl.BlockSpec((DEC_SEQ, LANES), lambda b, h: (r0 + b, qcol + h)),
            pl.BlockSpec((DEC_SEQ, LANES), lambda b, h: (r0 + b, kcol + h)),
            pl.BlockSpec((DEC_SEQ, LANES), lambda b, h: (r0 + b, vcol + h)),
            pl.BlockSpec((1, 1, 2, 1, PAST, B_DK), lambda b, h: (b, layer, 0, h, 0, 0)),
            pl.BlockSpec((1, 1, 1, PAST, B_DV), lambda b, h: (b, layer, h, 0, 0)),
            pl.BlockSpec((DEC_SEQ, LANES), lambda b, h: (0, 0)),
            pl.BlockSpec((DEC_SEQ, LANES), lambda b, h: (0, 0)),
            pl.BlockSpec((1, LANES), lambda b, h: (0, 0)),
            pl.BlockSpec((1, LANES), lambda b, h: (0, 0)),
            pl.BlockSpec((1, LANES), lambda b, h: (0, 0)),
        ],
        out_specs=pl.BlockSpec((DEC_SEQ, LANES), lambda b, h: (b, h)),
        out_shape=jax.ShapeDtypeStruct((T_SMP, B_WIDTH), F32),
        scratch_shapes=[pltpu.VMEM((DEC_SEQ, LANES), BF16), pltpu.VMEM((DEC_SEQ, LANES), BF16)],
        compiler_params=_cparams(("arbitrary", "arbitrary")),
        name="diff_attention_smp",
    )(lam, proj, proj, proj, cache_k, cache_v, cos, sin, gq2, gk2, gs)


def _rope_tables():
    n_rows = DEC_SEQ // GRID_W
    row = np.repeat(np.arange(n_rows), GRID_W).astype(np.float32)
    col = np.tile(np.arange(GRID_W), n_rows).astype(np.float32)
    half = B_DK // 2
    inv_freq = (ROPE_BASE ** (-jnp.arange(0, half, 2, dtype=F32) / half))
    row_ang = jnp.asarray(row)[:, None] * inv_freq
    col_ang = jnp.asarray(col)[:, None] * inv_freq
    ang = jnp.concatenate([row_ang, row_ang, col_ang, col_ang], axis=-1)
    ang = jnp.concatenate([ang, ang], axis=-1)
    sign = np.where((np.arange(LANES) % 32) < 16, -1.0, 1.0).astype(np.float32)
    return jnp.cos(ang), jnp.sin(ang) * sign


CM_ROWS = 512


def _gelu(x):
    return 0.5 * x * (1.0 + lax.erf(x * (2.0 ** -0.5)))


def _cmlp_kernel(u_ref, v_ref, g_ref, b_ref, ws_ref, bs_ref, o_ref):
    u = _gelu(u_ref[...])
    gv = _gelu(v_ref[...])
    mu = jnp.mean(gv, axis=-1, keepdims=True)
    dv = gv - mu
    var = jnp.mean(dv * dv, axis=-1, keepdims=True)
    vn = (dv * lax.rsqrt(var + EPS) * g_ref[...] + b_ref[...]).astype(BF16)
    lane = _lane((C_CHUNK, LANES))
    for c in range(CM_ROWS // C_CHUNK):
        rs = slice(c * C_CHUNK, (c + 1) * C_CHUNK)
        for p in range(2):
            cs = slice(p * LANES, (p + 1) * LANES)
            vp = vn[rs, cs]
            m0 = jnp.dot(ws_ref[2 * p].astype(BF16), vp, preferred_element_type=F32)
            m1 = jnp.dot(ws_ref[2 * p + 1].astype(BF16), vp, preferred_element_type=F32)
            mixed = jnp.where(lane < C_DG, m0, m1) + bs_ref[:, cs]
            o_ref[rs, cs] = u[rs, cs] * mixed


def _chunk_mlp(proj, ln_g, ln_b, w_s, bias_full):
    ucol = (5 * A_WIDTH + 3 * B_WIDTH) // C_WIDTH
    return pl.pallas_call(
        _cmlp_kernel,
        grid=(T // CM_ROWS,),
        in_specs=[
            pl.BlockSpec((CM_ROWS, C_WIDTH), lambda i: (i, ucol)),
            pl.BlockSpec((CM_ROWS, C_WIDTH), lambda i: (i, ucol + 1)),
            pl.BlockSpec((1, C_WIDTH), lambda i: (0, 0)),
            pl.BlockSpec((1, C_WIDTH), lambda i: (0, 0)),
            pl.BlockSpec((C_GROUPS, C_CHUNK, C_CHUNK), lambda i: (0, 0, 0)),
            pl.BlockSpec((C_CHUNK, C_WIDTH), lambda i: (0, 0)),
        ],
        out_specs=pl.BlockSpec((CM_ROWS, C_WIDTH), lambda i: (i, 0)),
        out_shape=jax.ShapeDtypeStruct((T, C_WIDTH), F32),
        compiler_params=_cparams(("arbitrary",)),
        name="chunk_mlp",
    )(proj, proj, ln_g.reshape(1, C_WIDTH), ln_b.reshape(1, C_WIDTH), w_s, bias_full)


def _postmix_kernel(of_ref, ob_ref, ag_ref, hg_ref, hsel_ref, b_ref, c_ref, w_ref, x_ref, gate1_ref, shift2_ref,
                    scale2_ref, g2_ref, wr_ref, br_ref, x1_ref, h2_ref, idx_ref, gw_ref):
    o = of_ref[...] + ob_ref[...]
    ms = _dot_sel(o * o, hsel_ref[...]) * (1.0 / A_DK)
    ag = ag_ref[...]
    a = o * lax.rsqrt(ms + EPS) * hg_ref[...] * (ag * jax.nn.sigmoid(ag))
    mixed = jnp.dot(a.astype(BF16), w_ref[0:A_WIDTH, :], preferred_element_type=F32)
    mixed = mixed + jnp.dot(b_ref[...].astype(BF16), w_ref[A_WIDTH:A_WIDTH + B_WIDTH, :], preferred_element_type=F32)
    mixed = mixed + jnp.dot(c_ref[...].astype(BF16), w_ref[A_WIDTH + B_WIDTH:, :], preferred_element_type=F32)
    x1 = x_ref[...] + gate1_ref[0] * mixed
    x1_ref[...] = x1
    y = x1 * lax.rsqrt(jnp.mean(x1 * x1, axis=-1, keepdims=True) + EPS) * g2_ref[...]
    h2 = y * (1.0 + scale2_ref[0]) + shift2_ref[0]
    h2_ref[...] = pltpu.einshape("t(jl)->tjl", h2, l=LANES)
    logits = jnp.dot(h2, wr_ref[...], precision=lax.Precision.HIGHEST, preferred_element_type=F32) + br_ref[...]
    lane = _lane(logits.shape)
    idx_out = jnp.zeros(logits.shape, jnp.int32)
    val_out = jnp.zeros(logits.shape, F32)
    top0 = None
    den = None
    for kk in range(TOP_K):
        mx = jnp.max(logits, axis=-1, keepdims=True)
        am = jnp.min(jnp.where(logits == mx, lane, LANES), axis=-1, keepdims=True)
        if kk == 0:
            top0 = mx
        e = jnp.exp(mx - top0)
        den = e if den is None else den + e
        idx_out = jnp.where(lane == kk, am, idx_out)
        val_out = jnp.where(lane == kk, e, val_out)
        logits = jnp.where(lane == am, -jnp.inf, logits)
    idx_ref[...] = idx_out
    gw_ref[...] = val_out / den


def _post_mix(o_f, o_b, proj, hg, hsel, b_out, c_out, w_out_bf, x, mod3, g2, wr_pad, br_pad):
    tile = lambda w: pl.BlockSpec((TM, w), lambda i: (i, 0))
    const = lambda shape: pl.BlockSpec(shape, lambda i: tuple(0 for _ in shape))
    modspec = lambda j: pl.BlockSpec((1, 1, D), lambda i: (_mod_row(i), 0, j))
    return pl.pallas_call(
        _postmix_kernel,
        grid=(N_TILES,),
        in_specs=[
            tile(A_WIDTH), tile(A_WIDTH),
            pl.BlockSpec((TM, A_WIDTH), lambda i: (i, 4)),
            const((1, A_WIDTH)), const((A_WIDTH, A_WIDTH)),
            tile(B_WIDTH), tile(C_WIDTH),
            const((D, D)),
            tile(D),
            modspec(2), modspec(3), modspec(4),
            const((1, D)), const((D, LANES)), const((1, LANES)),
        ],
        out_specs=[tile(D), pl.BlockSpec((TM, ROW_TILES, LANES), lambda i: (i, 0, 0)), tile(LANES), tile(LANES)],
        out_shape=[
            jax.ShapeDtypeStruct((T, D), F32),
            jax.ShapeDtypeStruct((T, ROW_TILES, LANES), F32),
            jax.ShapeDtypeStruct((T, LANES), jnp.int32),
            jax.ShapeDtypeStruct((T, LANES), F32),
        ],
        compiler_params=_cparams(("arbitrary",)),
        name="post_mix_router",
    )(o_f, o_b, proj, hg, hsel, b_out, c_out, w_out_bf, x, mod3, mod3, mod3, g2.reshape(1, D), wr_pad, br_pad)


def _moe_kernel(be_ref, bv_ref, x_ref, wgu_ref, bgu_ref, wdn_ref, bdn_ref, o_ref, wgu_bf, wdn_bf, xb_ref):
    i = pl.program_id(0)
    prev = be_ref[jnp.maximum(i - 1, 0)]
    new_expert = jnp.logical_or(i == 0, be_ref[i] != prev)
    n_valid = bv_ref[i]

    @pl.when(new_expert)
    def _():
        wgu_bf[...] = wgu_ref[0, 0].astype(BF16)
        wdn_bf[...] = wdn_ref[0, 0].astype(BF16)

    @pl.when(n_valid > 0)
    def _():
        live = lax.broadcasted_iota(jnp.int32, (MOE_BM, LANES), 0) < n_valid
        xt = pltpu.einshape("tjl->jtl", x_ref[...])
        for j in range(ROW_TILES):
            xb_ref[:, j * LANES:(j + 1) * LANES] = jnp.where(live, xt[j], 0.0).astype(BF16)
        gu = jnp.dot(xb_ref[...], wgu_bf[...], preferred_element_type=F32) + bgu_ref[0, 0]
        glu = jnp.minimum(gu[:, :D], SWIGLU_LIMIT)
        lin = jnp.clip(gu[:, D:], -SWIGLU_LIMIT, SWIGLU_LIMIT)
        act = glu * jax.nn.sigmoid(SWIGLU_ALPHA * glu) * (lin + 1.0)
        y = jnp.dot(act.astype(BF16), wdn_bf[...], preferred_element_type=F32) + bdn_ref[0, 0]
        o_ref[...] = pltpu.einshape("t(jl)->tjl", y, l=LANES)

    @pl.when(n_valid <= 0)
    def _():
        o_ref[...] = jnp.zeros_like(o_ref)


def _moe_ffn(block_e, block_valid, xs, w_gu, b_gu, w_dn, b_dn, layer):
    row_spec = pl.BlockSpec((MOE_BM, ROW_TILES, LANES), lambda i, be, bv: (i, 0, 0))
    return pl.pallas_call(
        _moe_kernel,
        grid_spec=pltpu.PrefetchScalarGridSpec(
            num_scalar_prefetch=2,
            grid=(MOE_BLOCKS,),
            in_specs=[
                row_spec,
                pl.BlockSpec((1, 1, D, 2 * D), lambda i, be, bv: (layer, be[i], 0, 0)),
                pl.BlockSpec((1, 1, 1, 2 * D), lambda i, be, bv: (layer, be[i], 0, 0)),
                pl.BlockSpec((1, 1, D, D), lambda i, be, bv: (layer, be[i], 0, 0)),
                pl.BlockSpec((1, 1, 1, D), lambda i, be, bv: (layer, be[i], 0, 0)),
            ],
            out_specs=row_spec,
            scratch_shapes=[pltpu.VMEM((D, 2 * D), BF16), pltpu.VMEM((D, D), BF16), pltpu.VMEM((MOE_BM, D), BF16)],
        ),
        out_shape=jax.ShapeDtypeStruct((MOE_ROWS, ROW_TILES, LANES), F32),
        compiler_params=_cparams(("arbitrary",)),
        name="moe_expert_ffn",
    )(block_e, block_valid, xs, w_gu, b_gu.reshape(DEPTH, N_EXPERTS, 1, 2 * D), w_dn,
      b_dn.reshape(DEPTH, N_EXPERTS, 1, D))


def _combine_kernel(x1_ref, y_ref, gw_ref, gate2_ref, o_ref):
    gw = gw_ref[...]
    ys = [pltpu.einshape("tjl->jtl", y_ref[kk]) for kk in range(TOP_K)]
    for j in range(ROW_TILES):
        cs = slice(j * LANES, (j + 1) * LANES)
        acc = None
        for kk in range(TOP_K):
            term = ys[kk][j] * gw[:, kk:kk + 1]
            acc = term if acc is None else acc + term
        o_ref[:, cs] = x1_ref[:, cs] + gate2_ref[0, :, cs] * acc


def _combine(x1, yg, gw, mod3):
    return pl.pallas_call(
        _combine_kernel,
        grid=(N_TILES,),
        in_specs=[
            pl.BlockSpec((TM, D), lambda i: (i, 0)),
            pl.BlockSpec((TOP_K, TM, ROW_TILES, LANES), lambda i: (0, i, 0, 0)),
            pl.BlockSpec((TM, LANES), lambda i: (i, 0)),
            pl.BlockSpec((1, 1, D), lambda i: (_mod_row(i), 0, 5)),
        ],
        out_specs=pl.BlockSpec((TM, D), lambda i: (i, 0)),
        out_shape=jax.ShapeDtypeStruct((T, D), F32),
        compiler_params=_cparams(("arbitrary",)),
        name="moe_combine",
    )(x1, yg, gw, mod3)


def _sc_mesh():
    return plsc.VectorSubcoreMesh(core_axis_name="c", subcore_axis_name="s")


def _sc_worker():
    return lax.axis_index("s") * SC_CORES + lax.axis_index("c")


def _sc_dispatch(h2t, dest_km):
    per_w = T // SC_WORKERS

    @functools.partial(
        pl.kernel, mesh=_sc_mesh(),
        out_type=jax.ShapeDtypeStruct((MOE_ROWS, ROW_TILES, LANES), F32),
        scratch_types=[pltpu.VMEM((SC_WIN,), jnp.int32), pltpu.VMEM((SC_WIN, ROW_TILES, LANES), F32),
                       pltpu.SemaphoreType.DMA],
    )
    def run(h_hbm, d_hbm, o_hbm, idx_v, rows_v, sem):
        w0 = _sc_worker() * per_w

        @pl.loop(0, per_w // SC_WIN)
        def _(w):
            base = pl.multiple_of(w0 + w * SC_WIN, SC_WIN)
            pltpu.sync_copy(h_hbm.at[pl.ds(base, SC_WIN)], rows_v)
            for kk in range(TOP_K):
                pltpu.sync_copy(d_hbm.at[pl.ds(kk * T + base, SC_WIN)], idx_v)
                pltpu.async_copy(rows_v, o_hbm.at[idx_v], sem).wait()

    return run(h2t, dest_km)


def _sc_gather(yb, dest_km):
    n = TOP_K * T
    per_w = n // SC_WORKERS

    @functools.partial(
        pl.kernel, mesh=_sc_mesh(),
        out_type=jax.ShapeDtypeStruct((n, ROW_TILES, LANES), F32),
        scratch_types=[pltpu.VMEM((SC_WIN,), jnp.int32), pltpu.VMEM((SC_WIN, ROW_TILES, LANES), F32),
                       pltpu.SemaphoreType.DMA],
    )
    def run(y_hbm, d_hbm, o_hbm, idx_v, rows_v, sem):
        w0 = _sc_worker() * per_w

        @pl.loop(0, per_w // SC_WIN)
        def _(w):
            base = pl.multiple_of(w0 + w * SC_WIN, SC_WIN)
            pltpu.sync_copy(d_hbm.at[pl.ds(base, SC_WIN)], idx_v)
            pltpu.async_copy(y_hbm.at[idx_v], rows_v, sem).wait()
            pltpu.sync_copy(rows_v, o_hbm.at[pl.ds(base, SC_WIN)])

    return run(yb, dest_km)


def _route(top_idx):
    flat_e = top_idx.reshape(-1)
    onehot = (flat_e[:, None] == jnp.arange(N_EXPERTS, dtype=jnp.int32)[None, :]).astype(jnp.int32)
    csum = jnp.cumsum(onehot, axis=0)
    counts = csum[-1]
    pos = jnp.take_along_axis(csum, flat_e[:, None], axis=1)[:, 0] - 1
    padded = (counts + MOE_BM - 1) // MOE_BM * MOE_BM
    pad_end = jnp.cumsum(padded)
    pad_start = pad_end - padded
    dest = pad_start[flat_e] + pos
    n_blocks = pad_end[-1] // MOE_BM
    blk_start = jnp.arange(MOE_BLOCKS, dtype=jnp.int32) * MOE_BM
    blk_clamped = jnp.minimum(blk_start, (n_blocks - 1) * MOE_BM)
    block_e = jnp.sum((pad_end[None, :] <= blk_clamped[:, None]).astype(jnp.int32), axis=1)
    live_end = (pad_start + counts)[block_e]
    block_valid = jnp.where(blk_start < pad_end[-1], jnp.clip(live_end - blk_start, 0, MOE_BM), 0)
    dest_km = dest.reshape(T, TOP_K).T.reshape(-1)
    return dest_km, block_e, block_valid.astype(jnp.int32)


def kernel(x_prompt, x_sample, c, cache_diff_k, cache_diff_v, state_hgrn, c_ctx, norm_mix_g, norm_ffn_g, w_mod, b_mod, w_in, w_out, hgrn_lower_bounds, hgrn_norm_g, diff_q_norm_g, diff_k_norm_g, diff_lambda_q1, diff_lambda_k1, diff_lambda_q2, diff_lambda_k2, diff_subln_g, cmlp_ln_g, cmlp_ln_b, cmlp_w_s, cmlp_b_s, router_w, router_b, moe_w_gate_up, moe_b_gate_up, moe_w_down, moe_b_down):
    x = jnp.concatenate([x_prompt.reshape(T_CTX, D), x_sample.reshape(T_SMP, D)], axis=0)
    cvec = jnp.concatenate([c_ctx[None, :], c, jnp.zeros((MOD_ROWS - 1 - DEC_BATCH, D), F32)], axis=0)
    mod = _modulation(cvec, w_mod, b_mod)

    lvl_np, tri_np = _hgrn_tables()
    lvl = jnp.asarray(lvl_np)
    tri = jnp.asarray(tri_np, dtype=BF16)
    cos, sin = _rope_tables()
    hsel = jnp.asarray(np.kron(np.eye(A_HEADS), np.ones((A_DK, A_DK))), dtype=BF16)
    sm = jax.nn.softmax(hgrn_lower_bounds.astype(F32), axis=0)
    lb_all = jnp.cumsum(sm, axis=0) - sm[0]

    new_k, new_v, new_s = [], [], []
    for l in range(DEPTH):
        mod3 = mod[l].reshape(MOD_ROWS, 1, 6 * D)
        proj = _in_projection(x, norm_mix_g[l], mod3, w_in[l].astype(BF16))

        s0 = jnp.concatenate([jnp.zeros((BATCH, 2, A_HEADS, A_DK, A_DK), F32), state_hgrn[:, l]], axis=0)
        o_dir, fin_dir = [], []
        for d in range(2):
            o_d, fin_d = _hgrn_scan(proj, lb_all[l, d].reshape(1, A_WIDTH), _pack_state(s0[:, d]),
                                    lvl, tri, l, d == 1)
            o_dir.append(o_d)
            fin_dir.append(_unpack_state(fin_d[:BATCH]))
        new_s.append(jnp.stack(fin_dir, axis=1))

        lam_init = 0.8 - 0.6 * math.exp(-0.3 * l)
        lam = (jnp.exp(jnp.sum(diff_lambda_q1[l] * diff_lambda_k1[l]))
               - jnp.exp(jnp.sum(diff_lambda_q2[l] * diff_lambda_k2[l])) + lam_init).reshape(1, 1)
        gq2 = jnp.tile(diff_q_norm_g[l], 2).reshape(1, LANES)
        gk2 = jnp.tile(diff_k_norm_g[l], 2).reshape(1, LANES)
        gs = diff_subln_g[l].reshape(1, LANES)
        b_ctx, k_l, v_l = _attn_ctx(proj, lam, gq2, gk2, gs, lam_init)
        b_smp = _attn_smp(proj, lam, cache_diff_k, cache_diff_v, cos, sin, gq2, gk2, gs, l, lam_init)
        b_out = jnp.concatenate([b_ctx, b_smp], axis=0)
        new_k.append(k_l)
        new_v.append(v_l)

        bias_full = jnp.repeat(cmlp_b_s[l].T, C_DG, axis=1)
        c_out = _chunk_mlp(proj, cmlp_ln_g[l], cmlp_ln_b[l], cmlp_w_s[l], bias_full)

        hg = jnp.tile(hgrn_norm_g[l], A_HEADS).reshape(1, A_WIDTH)
        wr_pad = jnp.pad(router_w[l], ((0, 0), (0, LANES - N_EXPERTS)))
        br_pad = jnp.pad(router_b[l], (0, LANES - N_EXPERTS), constant_values=-jnp.inf).reshape(1, LANES)
        x1, h2, idx_pad, gw_pad = _post_mix(o_dir[0], o_dir[1], proj, hg, hsel, b_out, c_out,
                                            w_out[l].astype(BF16), x, mod3, norm_ffn_g[l], wr_pad, br_pad)

        dest_km, block_e, block_valid = _route(idx_pad[:, :TOP_K])
        xs = _sc_dispatch(h2, dest_km)
        yb = _moe_ffn(block_e, block_valid, xs, moe_w_gate_up, moe_b_gate_up, moe_w_down, moe_b_down, l)
        yg = _sc_gather(yb, dest_km).reshape(TOP_K, T, ROW_TILES, LANES)
        x = _combine(x1, yg, gw_pad, mod3)

    y_prompt = x[:T_CTX].reshape(BATCH, SEQ, D)
    y_sample = x[T_CTX:].reshape(DEC_BATCH, DEC_SEQ, D)
    return (y_prompt, y_sample, jnp.stack(new_k, axis=1), jnp.stack(new_v, axis=1), jnp.stack(new_s, axis=1))
```

```python
import functools
import math

import numpy as np
import jax
import jax.numpy as jnp
from jax import lax
from jax.experimental import pallas as pl
from jax.experimental.pallas import tpu as pltpu
from jax.experimental.pallas import tpu_sc as plsc

F32 = jnp.float32
BF16 = jnp.bfloat16

D = 1024
DEPTH = 2
BATCH, SEQ = 16, 256
DEC_BATCH, DEC_SEQ = 8, 1024
PAST = 512
GRID_W = 64
A_HEADS, A_DK = 4, 64
A_WIDTH = 256
B_HEADS, B_DK, B_DV = 4, 64, 128
B_WIDTH = 512
C_GROUPS, C_CHUNK, C_WIDTH, C_DG = 4, 128, 256, 64
IN_WIDTH = 5 * A_WIDTH + 3 * B_WIDTH + 2 * C_WIDTH
N_EXPERTS, TOP_K = 32, 4
SWIGLU_LIMIT, SWIGLU_ALPHA = 7.0, 1.702
ROPE_BASE = 10000.0
EPS = 1e-6

T_CTX = BATCH * SEQ
T_SMP = DEC_BATCH * DEC_SEQ
T = T_CTX + T_SMP
N_SEQ = BATCH + DEC_BATCH
MOD_ROWS = 16

TM = 256
N_TILES = T // TM
CTX_TILES = T_CTX // TM
SMP_TILES_PER_SEQ = DEC_SEQ // TM
LANES = 128
MOE_BM = 256
MOE_ROWS = T * TOP_K + N_EXPERTS * MOE_BM
MOE_BLOCKS = MOE_ROWS // MOE_BM
ROW_TILES = D // LANES
SC_CORES, SC_SUBCORES = 2, 16
SC_WORKERS = SC_CORES * SC_SUBCORES
SC_WIN = 64
VMEM_LIMIT = 56 * 1024 * 1024


def _cparams(sem):
    return pltpu.CompilerParams(dimension_semantics=sem, vmem_limit_bytes=VMEM_LIMIT)


def _mod_row(i):
    return jnp.where(i < CTX_TILES, 0, 1 + (i - CTX_TILES) // SMP_TILES_PER_SEQ)


def _split3(x):
    hi = x.astype(BF16)
    r = x - hi.astype(F32)
    mid = r.astype(BF16)
    lo = (r - mid.astype(F32)).astype(BF16)
    return hi, mid, lo


def _sel_dot(sel, x):
    hi, mid, lo = _split3(x)
    acc = jnp.dot(sel, lo, preferred_element_type=F32)
    acc = acc + jnp.dot(sel, mid, preferred_element_type=F32)
    return acc + jnp.dot(sel, hi, preferred_element_type=F32)


def _dot_sel(x, sel):
    hi, mid, lo = _split3(x)
    acc = jnp.dot(lo, sel, preferred_element_type=F32)
    acc = acc + jnp.dot(mid, sel, preferred_element_type=F32)
    return acc + jnp.dot(hi, sel, preferred_element_type=F32)


def _dot_nt(a, b):
    return lax.dot_general(a, b, (((1,), (1,)), ((), ())), preferred_element_type=F32)


def _dot_tn(a, b):
    return lax.dot_general(a, b, (((0,), (0,)), ((), ())), preferred_element_type=F32)


def _lane(shape):
    return lax.broadcasted_iota(jnp.int32, shape, len(shape) - 1)


def _mod_kernel(c_ref, w_ref, b_ref, o_ref):
    c = c_ref[...]
    s = c * jax.nn.sigmoid(c)
    o_ref[0] = jnp.dot(s.astype(BF16), w_ref[0].astype(BF16), preferred_element_type=F32) + b_ref[0]


def _modulation(cvec, w_mod, b_mod):
    tn = 1536
    return pl.pallas_call(
        _mod_kernel,
        grid=(DEPTH, 6 * D // tn),
        in_specs=[
            pl.BlockSpec((MOD_ROWS, D), lambda l, j: (0, 0)),
            pl.BlockSpec((1, D, tn), lambda l, j: (l, 0, j)),
            pl.BlockSpec((1, 1, tn), lambda l, j: (l, 0, j)),
        ],
        out_specs=pl.BlockSpec((1, MOD_ROWS, tn), lambda l, j: (l, 0, j)),
        out_shape=jax.ShapeDtypeStruct((DEPTH, MOD_ROWS, 6 * D), F32),
        compiler_params=_cparams(("arbitrary", "arbitrary")),
        name="modulation",
    )(cvec, w_mod, b_mod.reshape(DEPTH, 1, 6 * D))


def _inproj_kernel(x_ref, g_ref, shift_ref, scale_ref, w_ref, o_ref):
    x = x_ref[...]
    y = x * lax.rsqrt(jnp.mean(x * x, axis=-1, keepdims=True) + EPS) * g_ref[...]
    h = y * (1.0 + scale_ref[0]) + shift_ref[0]
    o_ref[...] = jnp.dot(h.astype(BF16), w_ref[...], preferred_element_type=F32)


def _in_projection(x, g, mod3, w_in_bf):
    return pl.pallas_call(
        _inproj_kernel,
        grid=(N_TILES,),
        in_specs=[
            pl.BlockSpec((TM, D), lambda i: (i, 0)),
            pl.BlockSpec((1, D), lambda i: (0, 0)),
            pl.BlockSpec((1, 1, D), lambda i: (_mod_row(i), 0, 0)),
            pl.BlockSpec((1, 1, D), lambda i: (_mod_row(i), 0, 1)),
            pl.BlockSpec((D, IN_WIDTH), lambda i: (0, 0)),
        ],
        out_specs=pl.BlockSpec((TM, IN_WIDTH), lambda i: (i, 0)),
        out_shape=jax.ShapeDtypeStruct((T, IN_WIDTH), F32),
        compiler_params=_cparams(("arbitrary",)),
        name="in_projection",
    )(x, g.reshape(1, D), mod3, mod3, w_in_bf)


HG_C = TM
HG_LEVELS = tuple(2 ** j for j in range(1, int(math.log2(HG_C)) + 1))


def _hgrn_tables():
    t = np.arange(HG_C)[:, None]
    s = np.arange(HG_C)[None, :]
    x = t ^ s
    lvl = np.zeros((HG_C, HG_C), np.int32)
    nz = x > 0
    lvl[nz] = np.floor(np.log2(x[nz])).astype(np.int32) + 1
    fwd = np.where(t >= s, lvl, -1).astype(np.int32)
    bwd = np.where(t <= s, lvl, -1).astype(np.int32)
    tri_f = (t >= s).astype(np.float32)
    tri_b = (t <= s).astype(np.float32)
    return np.stack([fwd, bwd]), np.stack([tri_f, tri_b])


def _block_ref(cum, m, idx):
    c, l = cum.shape
    if m >= 16:
        c3 = cum.reshape(c // m, m, l)
        r = c3[:, idx:idx + 1, :]
        return jnp.broadcast_to(r, (c // m, m, l)).reshape(c, l)
    c3 = cum.reshape(c // 8, 8, l)
    sub = lax.broadcasted_iota(jnp.int32, c3.shape, 1)
    out = None
    for j in range(8 // m - 1, -1, -1):
        cand = jnp.broadcast_to(c3[:, j * m + idx:j * m + idx + 1, :], c3.shape)
        out = cand if out is None else jnp.where(sub < (j + 1) * m, cand, out)
    return out.reshape(c, l)


def _hgrn_kernel(q_ref, z_ref, v_ref, lb_ref, s0_ref, lvl_ref, tri_ref, o_ref, fin_ref, st_ref, *, layer, rev):
    g = pl.program_id(0)
    first = jnp.logical_or(g < CTX_TILES, (g - CTX_TILES) % SMP_TILES_PER_SEQ == 0)

    @pl.when(first)
    def _():
        st_ref[...] = s0_ref[0]

    qr = q_ref[...]
    q = qr * jax.nn.sigmoid(qr) * (A_DK ** -0.5)
    z = z_ref[...]
    if layer == 0:
        lf = jnp.minimum(z, 0.0) - jnp.log(1.0 + jnp.exp(-jnp.abs(z)))
        k = jax.nn.sigmoid(-z)
    else:
        lbd = lb_ref[...]
        lf = jnp.log(lbd + (1.0 - lbd) * jax.nn.sigmoid(z))
        k = (1.0 - lbd) * jax.nn.sigmoid(-z)
    v = v_ref[...]
    cum = _sel_dot(tri_ref[0], lf)
    lvl = lvl_ref[0]
    last_row = 0 if rev else HG_C - 1

    for p in range(2):
        sl = slice(p * LANES, (p + 1) * LANES)
        q_p, k_p, v_p, cum_p = q[:, sl], k[:, sl], v[:, sl], cum[:, sl]
        lane = _lane((HG_C, LANES))
        head_masks = (lane < A_DK, lane >= A_DK)
        v_bf = v_p.astype(BF16)
        k_bf = k_p.astype(BF16)
        scores = []
        for hm in head_masks:
            qm = jnp.where(hm, q_p, 0.0).astype(BF16)
            scores.append(jnp.where(lvl == 0, _dot_nt(qm, k_bf), 0.0))
        for li, m in enumerate(HG_LEVELS):
            ref = _block_ref(cum_p, m, m // 2 if rev else m // 2 - 1)
            qd = q_p * jnp.exp(jnp.minimum(cum_p - ref, 0.0))
            kd = (k_p * jnp.exp(jnp.minimum(ref - cum_p, 0.0))).astype(BF16)
            for hi, hm in enumerate(head_masks):
                qm = jnp.where(hm, qd, 0.0).astype(BF16)
                scores[hi] = jnp.where(lvl == li + 1, _dot_nt(qm, kd), scores[hi])
        st = st_ref[p]
        o_intra = jnp.where(head_masks[0],
                            jnp.dot(scores[0].astype(BF16), v_bf, preferred_element_type=F32),
                            jnp.dot(scores[1].astype(BF16), v_bf, preferred_element_type=F32))
        q0 = (q_p * jnp.exp(cum_p)).astype(BF16)
        o_ref[:, sl] = o_intra + _dot_nt(q0, st.astype(BF16))
        last = cum_p[last_row:last_row + 1, :]
        ks = (k_p * jnp.exp(last - cum_p)).astype(BF16)
        upd = _dot_tn(v_bf, ks)
        r = lax.broadcasted_iota(jnp.int32, (LANES, LANES), 0)
        cl = lax.broadcasted_iota(jnp.int32, (LANES, LANES), 1)
        same_head = (r < A_DK) == (cl < A_DK)
        st_new = st * jnp.exp(last) + jnp.where(same_head, upd, 0.0)
        st_ref[p] = st_new
        fin_ref[0, p] = st_new


def _hgrn_seq(g):
    return jnp.where(g < CTX_TILES, g, CTX_TILES + (g - CTX_TILES) // SMP_TILES_PER_SEQ)


def _hgrn_blk(g, rev):
    if not rev:
        return g
    j = g - CTX_TILES
    return jnp.where(g < CTX_TILES, g,
                     CTX_TILES + (j // SMP_TILES_PER_SEQ) * SMP_TILES_PER_SEQ
                     + (SMP_TILES_PER_SEQ - 1 - j % SMP_TILES_PER_SEQ))


def _hgrn_scan(proj, lb_dir, s0_dir, lvl, tri, layer, rev):
    d = 1 if rev else 0
    blk = functools.partial(_hgrn_blk, rev=rev)
    return pl.pallas_call(
        functools.partial(_hgrn_kernel, layer=layer, rev=rev),
        grid=(N_TILES,),
        in_specs=[
            pl.BlockSpec((HG_C, A_WIDTH), lambda g: (blk(g), 0)),
            pl.BlockSpec((HG_C, A_WIDTH), lambda g: (blk(g), 1 + d)),
            pl.BlockSpec((HG_C, A_WIDTH), lambda g: (blk(g), 3)),
            pl.BlockSpec((1, A_WIDTH), lambda g: (0, 0)),
            pl.BlockSpec((1, 2, LANES, LANES), lambda g: (_hgrn_seq(g), 0, 0, 0)),
            pl.BlockSpec((1, HG_C, HG_C), lambda g: (d, 0, 0)),
            pl.BlockSpec((1, HG_C, HG_C), lambda g: (d, 0, 0)),
        ],
        out_specs=[
            pl.BlockSpec((HG_C, A_WIDTH), lambda g: (blk(g), 0)),
            pl.BlockSpec((1, 2, LANES, LANES), lambda g: (_hgrn_seq(g), 0, 0, 0)),
        ],
        out_shape=[
            jax.ShapeDtypeStruct((T, A_WIDTH), F32),
            jax.ShapeDtypeStruct((N_SEQ, 2, LANES, LANES), F32),
        ],
        scratch_shapes=[pltpu.VMEM((2, LANES, LANES), F32)],
        compiler_params=_cparams(("arbitrary",)),
        name=f"hgrn_scan_{'bwd' if rev else 'fwd'}",
    )(proj, proj, proj, lb_dir, s0_dir, lvl, tri)


def _pack_state(s):
    n = s.shape[0]
    st = jnp.swapaxes(s, -1, -2).reshape(n, 2, 2, A_DK, A_DK)
    z = jnp.zeros_like(st[:, :, 0])
    top = jnp.concatenate([st[:, :, 0], z], axis=-1)
    bot = jnp.concatenate([z, st[:, :, 1]], axis=-1)
    return jnp.concatenate([top, bot], axis=-2)


def _unpack_state(sp):
    n = sp.shape[0]
    h0 = sp[:, :, :A_DK, :A_DK]
    h1 = sp[:, :, A_DK:, A_DK:]
    st = jnp.stack([h0, h1], axis=2).reshape(n, A_HEADS, A_DK, A_DK)
    return jnp.swapaxes(st, -1, -2)


def _half_rms(x, g):
    lane = _lane(x.shape)
    lo = lane < B_DK
    xx = x * x
    ms0 = jnp.sum(jnp.where(lo, xx, 0.0), axis=-1, keepdims=True) * (1.0 / B_DK)
    ms1 = jnp.sum(jnp.where(lo, 0.0, xx), axis=-1, keepdims=True) * (1.0 / B_DK)
    inv = jnp.where(lo, lax.rsqrt(ms0 + EPS), lax.rsqrt(ms1 + EPS))
    return x * inv * g


def _rope(x, cos, sin_signed):
    lane = _lane(x.shape)
    first = (lane % 32) < 16
    rot = jnp.where(first, pltpu.roll(x, LANES - 16, 1), pltpu.roll(x, 16, 1))
    return x * cos + rot * sin_signed


def _diff_softmax_pv(q_bf, keys_bf, vals_bf, lam):
    lane = _lane(q_bf.shape)
    zero = jnp.zeros_like(q_bf)
    acc = None
    parts = []
    for mp in range(2):
        qm = jnp.where((lane < B_DK) == (mp == 0), q_bf, zero)
        s = [_dot_nt(qm, kk) for kk in keys_bf]
        mx = functools.reduce(jnp.maximum, [jnp.max(si, axis=-1, keepdims=True) for si in s])
        e = [jnp.exp(si - mx) for si in s]
        den = functools.reduce(lambda a, b: a + b, [jnp.sum(ei, axis=-1, keepdims=True) for ei in e])
        parts.append((e, 1.0 / den))
    (e0, r0), (e1, r1) = parts
    r1 = r1 * lam
    for i in range(len(keys_bf)):
        a = (e0[i] * r0 - e1[i] * r1).astype(BF16)
        pv = jnp.dot(a, vals_bf[i], preferred_element_type=F32)
        acc = pv if acc is None else acc + pv
    return acc


def _subln(o, g, lam_init):
    return o * lax.rsqrt(jnp.mean(o * o, axis=-1, keepdims=True) + EPS) * g * (1.0 - lam_init)


def _attn_ctx_kernel(lam_ref, q_ref, k_ref, v_ref, gq_ref, gk_ref, gs_ref, o_ref, nk_ref, nv_ref, *, lam_init):
    lam = lam_ref[0, 0]
    qn = _half_rms(q_ref[...], gq_ref[...]) * (B_DK ** -0.5)
    kn = _half_rms(k_ref[...], gk_ref[...])
    v = v_ref[...]
    nk_ref[0, 0, 0] = kn[:, :B_DK]
    nk_ref[0, 1, 0] = kn[:, B_DK:]
    nv_ref[0, 0] = v
    o = _diff_softmax_pv(qn.astype(BF16), [kn.astype(BF16)], [v.astype(BF16)], lam)
    o_ref[...] = _subln(o, gs_ref[...], lam_init)


def _attn_ctx(proj, lam, gq2, gk2, gs, lam_init):
    qcol, kcol, vcol = 5 * A_WIDTH // LANES, 5 * A_WIDTH // LANES + 4, 5 * A_WIDTH // LANES + 8
    return pl.pallas_call(
        functools.partial(_attn_ctx_kernel, lam_init=lam_init),
        grid=(BATCH, B_HEADS),
        in_specs=[
            pl.BlockSpec(memory_space=pltpu.SMEM),
            pl.BlockSpec((SEQ, LANES), lambda b, h: (b, qcol + h)),
            pl.BlockSpec((SEQ, LANES), lambda b, h: (b, kcol + h)),
            pl.BlockSpec((SEQ, LANES), lambda b, h: (b, vcol + h)),
            pl.BlockSpec((1, LANES), lambda b, h: (0, 0)),
            pl.BlockSpec((1, LANES), lambda b, h: (0, 0)),
            pl.BlockSpec((1, LANES), lambda b, h: (0, 0)),
        ],
        out_specs=[
            pl.BlockSpec((SEQ, LANES), lambda b, h: (b, h)),
            pl.BlockSpec((1, 2, 1, SEQ, B_DK), lambda b, h: (b, 0, h, 0, 0)),
            pl.BlockSpec((1, 1, SEQ, B_DV), lambda b, h: (b, h, 0, 0)),
        ],
        out_shape=[
            jax.ShapeDtypeStruct((T_CTX, B_WIDTH), F32),
            jax.ShapeDtypeStruct((BATCH, 2, B_HEADS, SEQ, B_DK), F32),
            jax.ShapeDtypeStruct((BATCH, B_HEADS, SEQ, B_DV), F32),
        ],
        compiler_params=_cparams(("arbitrary", "arbitrary")),
        name="diff_attention_ctx",
    )(lam, proj, proj, proj, gq2, gk2, gs)


ATT_TQ = 256


def _attn_smp_kernel(lam_ref, q_ref, k_ref, v_ref, ck_ref, cv_ref, cos_ref, sin_ref, gq_ref, gk_ref, gs_ref,
                     o_ref, qs_ref, ks_ref, *, lam_init):
    lam = lam_ref[0, 0]
    cos = cos_ref[...]
    sin = sin_ref[...]
    qn = _rope(_half_rms(q_ref[...], gq_ref[...]), cos, sin) * (B_DK ** -0.5)
    qs_ref[...] = qn.astype(BF16)
    ks_ref[...] = _rope(_half_rms(k_ref[...], gk_ref[...]), cos, sin).astype(BF16)
    ck = jnp.concatenate([ck_ref[0, 0, 0, 0], ck_ref[0, 0, 1, 0]], axis=-1).astype(BF16)
    cv = cv_ref[0, 0, 0].astype(BF16)
    v_bf = v_ref[...].astype(BF16)
    k_bf = ks_ref[...]
    g = gs_ref[...]

    def body(i, carry):
        r0 = pl.multiple_of(i * ATT_TQ, ATT_TQ)
        q_bf = qs_ref[pl.ds(r0, ATT_TQ), :]
        o = _diff_softmax_pv(q_bf, [k_bf, ck], [v_bf, cv], lam)
        o_ref[pl.ds(r0, ATT_TQ), :] = _subln(o, g, lam_init)
        return carry

    lax.fori_loop(0, DEC_SEQ // ATT_TQ, body, 0)


def _attn_smp(proj, lam, cache_k, cache_v, cos, sin, gq2, gk2, gs, layer, lam_init):
    qcol, kcol, vcol = 5 * A_WIDTH // LANES, 5 * A_WIDTH // LANES + 4, 5 * A_WIDTH // LANES + 8
    r0 = T_CTX // DEC_SEQ
    return pl.pallas_call(
        functools.partial(_attn_smp_kernel, lam_init=lam_init),
        grid=(DEC_BATCH, B_HEADS),
        in_specs=[
            pl.BlockSpec(memory_space=pltpu.SMEM),
            pl.BlockSpec((DEC_SEQ, LANES), lambda b, h: (r0 + b, qcol + h)),
            pl.BlockSpec((DEC_SEQ, LANES), lambda b, h: (r0 + b, kcol + h)),
            pl.BlockSpec((DEC_SEQ, LANES), lambda b, h: (r0 + b, vcol + h)),
            pl.BlockSpec((1, 1, 2, 1, PAST, B_DK), lambda b, h: (b, layer, 0, h, 0, 0)),
            pl.BlockSpec((1, 1, 1, PAST, B_DV), lambda b, h: (b, layer, h, 0, 0)),
            pl.BlockSpec((DEC_SEQ, LANES), lambda b, h: (0, 0)),
            pl.BlockSpec((DEC_SEQ, LANES), lambda b, h: (0, 0)),
            pl.BlockSpec((1, LANES), lambda b, h: (0, 0)),
            pl.BlockSpec((1, LANES), lambda b, h: (0, 0)),
            pl.BlockSpec((1, LANES), lambda b, h: (0, 0)),
        ],
        out_specs=pl.BlockSpec((DEC_SEQ, LANES), lambda b, h: (b, h)),
        out_shape=jax.ShapeDtypeStruct((T_SMP, B_WIDTH), F32),
        scratch_shapes=[pltpu.VMEM((DEC_SEQ, LANES), BF16), pltpu.VMEM((DEC_SEQ, LANES), BF16)],
        compiler_params=_cparams(("arbitrary", "arbitrary")),
        name="diff_attention_smp",
    )(lam, proj, proj, proj, cache_k, cache_v, cos, sin, gq2, gk2, gs)


def _rope_tables():
    n_rows = DEC_SEQ // GRID_W
    row = np.repeat(np.arange(n_rows), GRID_W).astype(np.float32)
    col = np.tile(np.arange(GRID_W), n_rows).astype(np.float32)
    half = B_DK // 2
    inv_freq = (ROPE_BASE ** (-jnp.arange(0, half, 2, dtype=F32) / half))
    row_ang = jnp.asarray(row)[:, None] * inv_freq
    col_ang = jnp.asarray(col)[:, None] * inv_freq
    ang = jnp.concatenate([row_ang, row_ang, col_ang, col_ang], axis=-1)
    ang = jnp.concatenate([ang, ang], axis=-1)
    sign = np.where((np.arange(LANES) % 32) < 16, -1.0, 1.0).astype(np.float32)
    return jnp.cos(ang), jnp.sin(ang) * sign


CM_ROWS = 512


def _gelu(x):
    return 0.5 * x * (1.0 + lax.erf(x * (2.0 ** -0.5)))


def _cmlp_kernel(u_ref, v_ref, g_ref, b_ref, ws_ref, bs_ref, o_ref):
    u = _gelu(u_ref[...])
    gv = _gelu(v_ref[...])
    mu = jnp.mean(gv, axis=-1, keepdims=True)
    dv = gv - mu
    var = jnp.mean(dv * dv, axis=-1, keepdims=True)
    vn = (dv * lax.rsqrt(var + EPS) * g_ref[...] + b_ref[...]).astype(BF16)
    lane = _lane((C_CHUNK, LANES))
    for c in range(CM_ROWS // C_CHUNK):
        rs = slice(c * C_CHUNK, (c + 1) * C_CHUNK)
        for p in range(2):
            cs = slice(p * LANES, (p + 1) * LANES)
            vp = vn[rs, cs]
            m0 = jnp.dot(ws_ref[2 * p].astype(BF16), vp, preferred_element_type=F32)
            m1 = jnp.dot(ws_ref[2 * p + 1].astype(BF16), vp, preferred_element_type=F32)
            mixed = jnp.where(lane < C_DG, m0, m1) + bs_ref[:, cs]
            o_ref[rs, cs] = u[rs, cs] * mixed


def _chunk_mlp(proj, ln_g, ln_b, w_s, bias_full):
    ucol = (5 * A_WIDTH + 3 * B_WIDTH) // C_WIDTH
    return pl.pallas_call(
        _cmlp_kernel,
        grid=(T // CM_ROWS,),
        in_specs=[
            pl.BlockSpec((CM_ROWS, C_WIDTH), lambda i: (i, ucol)),
            pl.BlockSpec((CM_ROWS, C_WIDTH), lambda i: (i, ucol + 1)),
            pl.BlockSpec((1, C_WIDTH), lambda i: (0, 0)),
            pl.BlockSpec((1, C_WIDTH), lambda i: (0, 0)),
            pl.BlockSpec((C_GROUPS, C_CHUNK, C_CHUNK), lambda i: (0, 0, 0)),
            pl.BlockSpec((C_CHUNK, C_WIDTH), lambda i: (0, 0)),
        ],
        out_specs=pl.BlockSpec((CM_ROWS, C_WIDTH), lambda i: (i, 0)),
        out_shape=jax.ShapeDtypeStruct((T, C_WIDTH), F32),
        compiler_params=_cparams(("arbitrary",)),
        name="chunk_mlp",
    )(proj, proj, ln_g.reshape(1, C_WIDTH), ln_b.reshape(1, C_WIDTH), w_s, bias_full)


def _postmix_kernel(of_ref, ob_ref, ag_ref, hg_ref, hsel_ref, bc_ref, bs_ref, c_ref, w_ref, x_ref, gate1_ref,
                    shift2_ref, scale2_ref, g2_ref, wrh_ref, wrl_ref, br_ref, x1_ref, h2_ref, idx_ref, gw_ref):
    o = of_ref[...] + ob_ref[...]
    ms = _dot_sel(o * o, hsel_ref[...]) * (1.0 / A_DK)
    ag = ag_ref[...]
    a = o * lax.rsqrt(ms + EPS) * hg_ref[...] * (ag * jax.nn.sigmoid(ag))
    b = jnp.where(pl.program_id(0) < CTX_TILES, bc_ref[...], bs_ref[...])
    mixed = jnp.dot(a.astype(BF16), w_ref[0:A_WIDTH, :], preferred_element_type=F32)
    mixed = mixed + jnp.dot(b.astype(BF16), w_ref[A_WIDTH:A_WIDTH + B_WIDTH, :], preferred_element_type=F32)
    mixed = mixed + jnp.dot(c_ref[...].astype(BF16), w_ref[A_WIDTH + B_WIDTH:, :], preferred_element_type=F32)
    x1 = x_ref[...] + gate1_ref[0] * mixed
    x1_ref[...] = x1
    y = x1 * lax.rsqrt(jnp.mean(x1 * x1, axis=-1, keepdims=True) + EPS) * g2_ref[...]
    h2 = y * (1.0 + scale2_ref[0]) + shift2_ref[0]
    h2_ref[...] = pltpu.einshape("t(jl)->tjl", h2, l=LANES)
    hi = h2.astype(BF16)
    lo = (h2 - hi.astype(F32)).astype(BF16)
    lg = jnp.dot(lo, wrh_ref[...], preferred_element_type=F32)
    lg = lg + jnp.dot(hi, wrl_ref[...], preferred_element_type=F32)
    lg = lg + jnp.dot(hi, wrh_ref[...], preferred_element_type=F32) + br_ref[...]
    lt = lg.T[:N_EXPERTS]
    row = lax.broadcasted_iota(jnp.int32, lt.shape, 0)
    out_row = lax.broadcasted_iota(jnp.int32, (8, TM), 0)
    idx_out = jnp.zeros((8, TM), jnp.int32)
    val_out = jnp.zeros((8, TM), F32)
    top0 = None
    den = None
    for kk in range(TOP_K):
        mx = jnp.max(lt, axis=0, keepdims=True)
        am = jnp.min(jnp.where(lt == mx, row, N_EXPERTS), axis=0, keepdims=True)
        if kk == 0:
            top0 = mx
        e = jnp.exp(mx - top0)
        den = e if den is None else den + e
        idx_out = jnp.where(out_row == kk, am, idx_out)
        val_out = jnp.where(out_row == kk, e, val_out)
        lt = jnp.where(row == am, -jnp.inf, lt)
    idx_ref[...] = idx_out
    gw_ref[...] = val_out / den


def _post_mix(o_f, o_b, proj, hg, hsel, b_ctx, b_smp, c_out, w_out_bf, x, mod3, g2, wr_hi, wr_lo, br_pad):
    tile = lambda w: pl.BlockSpec((TM, w), lambda i: (i, 0))
    const = lambda shape: pl.BlockSpec(shape, lambda i: tuple(0 for _ in shape))
    modspec = lambda j: pl.BlockSpec((1, 1, D), lambda i: (_mod_row(i), 0, j))
    rowsT = pl.BlockSpec((8, TM), lambda i: (0, i))
    return pl.pallas_call(
        _postmix_kernel,
        grid=(N_TILES,),
        in_specs=[
            tile(A_WIDTH), tile(A_WIDTH),
            pl.BlockSpec((TM, A_WIDTH), lambda i: (i, 4)),
            const((1, A_WIDTH)), const((A_WIDTH, A_WIDTH)),
            pl.BlockSpec((TM, B_WIDTH), lambda i: (jnp.minimum(i, CTX_TILES - 1), 0)),
            pl.BlockSpec((TM, B_WIDTH), lambda i: (jnp.maximum(i - CTX_TILES, 0), 0)),
            tile(C_WIDTH),
            const((D, D)),
            tile(D),
            modspec(2), modspec(3), modspec(4),
            const((1, D)), const((D, LANES)), const((D, LANES)), const((1, LANES)),
        ],
        out_specs=[tile(D), pl.BlockSpec((TM, ROW_TILES, LANES), lambda i: (i, 0, 0)), rowsT, rowsT],
        out_shape=[
            jax.ShapeDtypeStruct((T, D), F32),
            jax.ShapeDtypeStruct((T, ROW_TILES, LANES), F32),
            jax.ShapeDtypeStruct((8, T), jnp.int32),
            jax.ShapeDtypeStruct((8, T), F32),
        ],
        compiler_params=_cparams(("arbitrary",)),
        name="post_mix_router",
    )(o_f, o_b, proj, hg, hsel, b_ctx, b_smp, c_out, w_out_bf, x, mod3, mod3, mod3, g2.reshape(1, D),
      wr_hi, wr_lo, br_pad)


def _route_kernel(idx_ref, dest_ref, meta_ref):
    erow = lax.broadcasted_iota(jnp.int32, (N_EXPERTS, TM), 0)
    s_i = lax.broadcasted_iota(jnp.int32, (TM, TM), 0)
    t_i = lax.broadcasted_iota(jnp.int32, (TM, TM), 1)
    earlier = (s_i < t_i).astype(BF16)
    out_row = lax.broadcasted_iota(jnp.int32, (8, TM), 0)

    def onehots(i):
        idx = idx_ref[:, pl.ds(pl.multiple_of(i * TM, TM), TM)]
        return [(erow == idx[kk:kk + 1, :]) for kk in range(TOP_K)]

    def count_tile(i, run):
        ohs = onehots(i)
        base = run
        pos = jnp.zeros((8, TM), F32)
        for kk in range(TOP_K):
            ohf = ohs[kk].astype(F32)
            before = jnp.dot(ohs[kk].astype(BF16), earlier, preferred_element_type=F32)
            p = jnp.sum(ohf * (base + before), axis=0, keepdims=True)
            pos = jnp.where(out_row == kk, p, pos)
            base = base + jnp.sum(ohf, axis=1, keepdims=True)
        dest_ref[:, pl.ds(pl.multiple_of(i * TM, TM), TM)] = pos.astype(jnp.int32)
        return base

    counts = lax.fori_loop(0, N_TILES, count_tile, jnp.zeros((N_EXPERTS, 1), F32)).astype(jnp.int32)
    bm_shift = MOE_BM.bit_length() - 1
    padded = lax.shift_left(lax.shift_right_logical(counts + (MOE_BM - 1), bm_shift), bm_shift)
    e_r = lax.broadcasted_iota(jnp.int32, (N_EXPERTS, N_EXPERTS), 0)
    e_c = lax.broadcasted_iota(jnp.int32, (N_EXPERTS, N_EXPERTS), 1)
    incl = (e_c <= e_r).astype(BF16)
    pad_end = _sel_dot(incl, jnp.broadcast_to(padded.astype(F32), (N_EXPERTS, LANES)))[:, :1]
    pad_start = pad_end - padded.astype(F32)

    def place_tile(i, carry):
        ohs = onehots(i)
        sl = pl.ds(pl.multiple_of(i * TM, TM), TM)
        off = jnp.zeros((8, TM), F32)
        for kk in range(TOP_K):
            o = jnp.sum(ohs[kk].astype(F32) * pad_start, axis=0, keepdims=True)
            off = jnp.where(out_row == kk, o, off)
        dest_ref[:, sl] = dest_ref[:, sl] + off.astype(jnp.int32)
        return carry

    lax.fori_loop(0, N_TILES, place_tile, 0)

    total = jnp.max(pad_end, axis=0, keepdims=True)
    blk0 = (lax.broadcasted_iota(jnp.int32, (1, TM), 1) * MOE_BM).astype(F32)
    blk_c = jnp.minimum(blk0, total - MOE_BM)
    block_e = jnp.sum((pad_end <= blk_c).astype(F32), axis=0, keepdims=True)
    live_end = pad_start + counts.astype(F32)
    sel = erow.astype(F32) == block_e
    live = jnp.sum(jnp.where(sel, live_end, 0.0), axis=0, keepdims=True)
    valid = jnp.where(blk0 < total, jnp.clip(live - blk0, 0.0, float(MOE_BM)), 0.0)
    meta = jnp.where(out_row == 0, block_e, jnp.where(out_row == 1, valid, 0.0))
    meta_ref[...] = meta.astype(jnp.int32)


def _route(idx_t):
    assert MOE_BLOCKS <= TM
    return pl.pallas_call(
        _route_kernel,
        out_shape=[jax.ShapeDtypeStruct((8, T), jnp.int32), jax.ShapeDtypeStruct((8, TM), jnp.int32)],
        compiler_params=pltpu.CompilerParams(vmem_limit_bytes=VMEM_LIMIT),
        name="moe_route",
    )(idx_t)


def _moe_kernel(be_ref, bv_ref, x_ref, wgu_ref, bgu_ref, wdn_ref, bdn_ref, o_ref, wgu_bf, wdn_bf, xb_ref):
    i = pl.program_id(0)
    prev = be_ref[jnp.maximum(i - 1, 0)]
    new_expert = jnp.logical_or(i == 0, be_ref[i] != prev)
    n_valid = bv_ref[i]

    @pl.when(new_expert)
    def _():
        wgu_bf[...] = wgu_ref[0, 0].astype(BF16)
        wdn_bf[...] = wdn_ref[0, 0].astype(BF16)

    @pl.when(n_valid > 0)
    def _():
        live = lax.broadcasted_iota(jnp.int32, (MOE_BM, LANES), 0) < n_valid
        xt = pltpu.einshape("tjl->jtl", x_ref[...])
        for j in range(ROW_TILES):
            xb_ref[:, j * LANES:(j + 1) * LANES] = jnp.where(live, xt[j], 0.0).astype(BF16)
        gu = jnp.dot(xb_ref[...], wgu_bf[...], preferred_element_type=F32) + bgu_ref[0, 0]
        glu = jnp.minimum(gu[:, :D], SWIGLU_LIMIT)
        lin = jnp.clip(gu[:, D:], -SWIGLU_LIMIT, SWIGLU_LIMIT)
        act = glu * jax.nn.sigmoid(SWIGLU_ALPHA * glu) * (lin + 1.0)
        y = jnp.dot(act.astype(BF16), wdn_bf[...], preferred_element_type=F32) + bdn_ref[0, 0]
        o_ref[...] = pltpu.einshape("t(jl)->tjl", y, l=LANES)

    @pl.when(n_valid <= 0)
    def _():
        o_ref[...] = jnp.zeros_like(o_ref)


def _moe_ffn(block_e, block_valid, xs, w_gu, b_gu, w_dn, b_dn, layer):
    row_spec = pl.BlockSpec((MOE_BM, ROW_TILES, LANES), lambda i, be, bv: (i, 0, 0))
    return pl.pallas_call(
        _moe_kernel,
        grid_spec=pltpu.PrefetchScalarGridSpec(
            num_scalar_prefetch=2,
            grid=(MOE_BLOCKS,),
            in_specs=[
                row_spec,
                pl.BlockSpec((1, 1, D, 2 * D), lambda i, be, bv: (layer, be[i], 0, 0)),
                pl.BlockSpec((1, 1, 1, 2 * D), lambda i, be, bv: (layer, be[i], 0, 0)),
                pl.BlockSpec((1, 1, D, D), lambda i, be, bv: (layer, be[i], 0, 0)),
                pl.BlockSpec((1, 1, 1, D), lambda i, be, bv: (layer, be[i], 0, 0)),
            ],
            out_specs=row_spec,
            scratch_shapes=[pltpu.VMEM((D, 2 * D), BF16), pltpu.VMEM((D, D), BF16), pltpu.VMEM((MOE_BM, D), BF16)],
        ),
        out_shape=jax.ShapeDtypeStruct((MOE_ROWS, ROW_TILES, LANES), F32),
        compiler_params=_cparams(("arbitrary",)),
        name="moe_expert_ffn",
    )(block_e, block_valid, xs, w_gu, b_gu.reshape(DEPTH, N_EXPERTS, 1, 2 * D), w_dn,
      b_dn.reshape(DEPTH, N_EXPERTS, 1, D))


def _combine_kernel(x1_ref, y_ref, gw_ref, gate2_ref, o_ref):
    gw = jnp.concatenate([gw_ref[...], jnp.zeros((LANES - 8, TM), F32)], axis=0).T
    ys = [pltpu.einshape("tjl->jtl", y_ref[kk]) for kk in range(TOP_K)]
    for j in range(ROW_TILES):
        cs = slice(j * LANES, (j + 1) * LANES)
        acc = None
        for kk in range(TOP_K):
            term = ys[kk][j] * gw[:, kk:kk + 1]
            acc = term if acc is None else acc + term
        o_ref[:, cs] = x1_ref[:, cs] + gate2_ref[0, :, cs] * acc


def _combine(x1, yg, gw, mod3):
    return pl.pallas_call(
        _combine_kernel,
        grid=(N_TILES,),
        in_specs=[
            pl.BlockSpec((TM, D), lambda i: (i, 0)),
            pl.BlockSpec((TOP_K, TM, ROW_TILES, LANES), lambda i: (0, i, 0, 0)),
            pl.BlockSpec((8, TM), lambda i: (0, i)),
            pl.BlockSpec((1, 1, D), lambda i: (_mod_row(i), 0, 5)),
        ],
        out_specs=pl.BlockSpec((TM, D), lambda i: (i, 0)),
        out_shape=jax.ShapeDtypeStruct((T, D), F32),
        compiler_params=_cparams(("arbitrary",)),
        name="moe_combine",
    )(x1, yg, gw, mod3)


def _sc_mesh():
    return plsc.VectorSubcoreMesh(core_axis_name="c", subcore_axis_name="s")


def _sc_worker():
    return lax.axis_index("s") * SC_CORES + lax.axis_index("c")


def _sc_dispatch(h2t, dest_km):
    per_w = T // SC_WORKERS

    @functools.partial(
        pl.kernel, mesh=_sc_mesh(),
        out_type=jax.ShapeDtypeStruct((MOE_ROWS, ROW_TILES, LANES), F32),
        scratch_types=[pltpu.VMEM((SC_WIN,), jnp.int32), pltpu.VMEM((SC_WIN, ROW_TILES, LANES), F32),
                       pltpu.SemaphoreType.DMA],
    )
    def run(h_hbm, d_hbm, o_hbm, idx_v, rows_v, sem):
        w0 = _sc_worker() * per_w

        @pl.loop(0, per_w // SC_WIN)
        def _(w):
            base = pl.multiple_of(w0 + w * SC_WIN, SC_WIN)
            pltpu.sync_copy(h_hbm.at[pl.ds(base, SC_WIN)], rows_v)
            for kk in range(TOP_K):
                pltpu.sync_copy(d_hbm.at[pl.ds(kk * T + base, SC_WIN)], idx_v)
                pltpu.async_copy(rows_v, o_hbm.at[idx_v], sem).wait()

    return run(h2t, dest_km)


def _sc_gather(yb, dest_km):
    n = TOP_K * T
    per_w = n // SC_WORKERS

    @functools.partial(
        pl.kernel, mesh=_sc_mesh(),
        out_type=jax.ShapeDtypeStruct((n, ROW_TILES, LANES), F32),
        scratch_types=[pltpu.VMEM((SC_WIN,), jnp.int32), pltpu.VMEM((SC_WIN, ROW_TILES, LANES), F32),
                       pltpu.SemaphoreType.DMA],
    )
    def run(y_hbm, d_hbm, o_hbm, idx_v, rows_v, sem):
        w0 = _sc_worker() * per_w

        @pl.loop(0, per_w // SC_WIN)
        def _(w):
            base = pl.multiple_of(w0 + w * SC_WIN, SC_WIN)
            pltpu.sync_copy(d_hbm.at[pl.ds(base, SC_WIN)], idx_v)
            pltpu.async_copy(y_hbm.at[idx_v], rows_v, sem).wait()
            pltpu.sync_copy(rows_v, o_hbm.at[pl.ds(base, SC_WIN)])

    return run(yb, dest_km)


def kernel(x_prompt, x_sample, c, cache_diff_k, cache_diff_v, state_hgrn, c_ctx, norm_mix_g, norm_ffn_g, w_mod, b_mod, w_in, w_out, hgrn_lower_bounds, hgrn_norm_g, diff_q_norm_g, diff_k_norm_g, diff_lambda_q1, diff_lambda_k1, diff_lambda_q2, diff_lambda_k2, diff_subln_g, cmlp_ln_g, cmlp_ln_b, cmlp_w_s, cmlp_b_s, router_w, router_b, moe_w_gate_up, moe_b_gate_up, moe_w_down, moe_b_down):
    x = jnp.concatenate([x_prompt.reshape(T_CTX, D), x_sample.reshape(T_SMP, D)], axis=0)
    cvec = jnp.concatenate([c_ctx[None, :], c, jnp.zeros((MOD_ROWS - 1 - DEC_BATCH, D), F32)], axis=0)
    mod = _modulation(cvec, w_mod, b_mod)

    lvl_np, tri_np = _hgrn_tables()
    lvl = jnp.asarray(lvl_np)
    tri = jnp.asarray(tri_np, dtype=BF16)
    cos, sin = _rope_tables()
    hsel = jnp.asarray(np.kron(np.eye(A_HEADS), np.ones((A_DK, A_DK))), dtype=BF16)
    sm = jax.nn.softmax(hgrn_lower_bounds.astype(F32), axis=0)
    lb_all = jnp.cumsum(sm, axis=0) - sm[0]

    new_k, new_v, new_s = [], [], []
    for l in range(DEPTH):
        mod3 = mod[l].reshape(MOD_ROWS, 1, 6 * D)
        proj = _in_projection(x, norm_mix_g[l], mod3, w_in[l].astype(BF16))

        s0 = jnp.concatenate([jnp.zeros((BATCH, 2, A_HEADS, A_DK, A_DK), F32), state_hgrn[:, l]], axis=0)
        o_dir, fin_dir = [], []
        for d in range(2):
            o_d, fin_d = _hgrn_scan(proj, lb_all[l, d].reshape(1, A_WIDTH), _pack_state(s0[:, d]),
                                    lvl, tri, l, d == 1)
            o_dir.append(o_d)
            fin_dir.append(_unpack_state(fin_d[:BATCH]))
        new_s.append(jnp.stack(fin_dir, axis=1))

        lam_init = 0.8 - 0.6 * math.exp(-0.3 * l)
        lam = (jnp.exp(jnp.sum(diff_lambda_q1[l] * diff_lambda_k1[l]))
               - jnp.exp(jnp.sum(diff_lambda_q2[l] * diff_lambda_k2[l])) + lam_init).reshape(1, 1)
        gq2 = jnp.tile(diff_q_norm_g[l], 2).reshape(1, LANES)
        gk2 = jnp.tile(diff_k_norm_g[l], 2).reshape(1, LANES)
        gs = diff_subln_g[l].reshape(1, LANES)
        b_ctx, k_l, v_l = _attn_ctx(proj, lam, gq2, gk2, gs, lam_init)
        b_smp = _attn_smp(proj, lam, cache_diff_k, cache_diff_v, cos, sin, gq2, gk2, gs, l, lam_init)
        new_k.append(k_l)
        new_v.append(v_l)

        bias_full = jnp.repeat(cmlp_b_s[l].T, C_DG, axis=1)
        c_out = _chunk_mlp(proj, cmlp_ln_g[l], cmlp_ln_b[l], cmlp_w_s[l], bias_full)

        hg = jnp.tile(hgrn_norm_g[l], A_HEADS).reshape(1, A_WIDTH)
        wr_pad = jnp.pad(router_w[l], ((0, 0), (0, LANES - N_EXPERTS)))
        wr_hi = wr_pad.astype(BF16)
        wr_lo = (wr_pad - wr_hi.astype(F32)).astype(BF16)
        br_pad = jnp.pad(router_b[l], (0, LANES - N_EXPERTS)).reshape(1, LANES)
        x1, h2, idx_t, gw_t = _post_mix(o_dir[0], o_dir[1], proj, hg, hsel, b_ctx, b_smp, c_out,
                                        w_out[l].astype(BF16), x, mod3, norm_ffn_g[l], wr_hi, wr_lo, br_pad)

        dest_t, meta = _route(idx_t)
        dest_km = dest_t[:TOP_K].reshape(-1)
        block_e, block_valid = meta[0, :MOE_BLOCKS], meta[1, :MOE_BLOCKS]
        xs = _sc_dispatch(h2, dest_km)
        yb = _moe_ffn(block_e, block_valid, xs, moe_w_gate_up, moe_b_gate_up, moe_w_down, moe_b_down, l)
        yg = _sc_gather(yb, dest_km).reshape(TOP_K, T, ROW_TILES, LANES)
        x = _combine(x1, yg, gw_t, mod3)

    y_prompt = x[:T_CTX].reshape(BATCH, SEQ, D)
    y_sample = x[T_CTX:].reshape(DEC_BATCH, DEC_SEQ, D)
    return (y_prompt, y_sample, jnp.stack(new_k, axis=1), jnp.stack(new_v, axis=1), jnp.stack(new_s, axis=1))
```

```python
import functools
import math

import numpy as np
import jax
import jax.numpy as jnp
from jax import lax
from jax.experimental import pallas as pl
from jax.experimental.pallas import tpu as pltpu
from jax.experimental.pallas import tpu_sc as plsc

F32 = jnp.float32
BF16 = jnp.bfloat16

D = 1024
DEPTH = 2
BATCH, SEQ = 16, 256
DEC_BATCH, DEC_SEQ = 8, 1024
PAST = 512
GRID_W = 64
A_HEADS, A_DK = 4, 64
A_WIDTH = 256
B_HEADS, B_DK, B_DV = 4, 64, 128
B_WIDTH = 512
C_GROUPS, C_CHUNK, C_WIDTH, C_DG = 4, 128, 256, 64
IN_WIDTH = 5 * A_WIDTH + 3 * B_WIDTH + 2 * C_WIDTH
N_EXPERTS, TOP_K = 32, 4
SWIGLU_LIMIT, SWIGLU_ALPHA = 7.0, 1.702
ROPE_BASE = 10000.0
EPS = 1e-6

T_CTX = BATCH * SEQ
T_SMP = DEC_BATCH * DEC_SEQ
T = T_CTX + T_SMP
N_SEQ = BATCH + DEC_BATCH
MOD_ROWS = 16

TM = 256
N_TILES = T // TM
CTX_TILES = T_CTX // TM
SMP_TILES_PER_SEQ = DEC_SEQ // TM
LANES = 128
MOE_BM = 256
MOE_ROWS = T * TOP_K + N_EXPERTS * MOE_BM
MOE_BLOCKS = MOE_ROWS // MOE_BM
ROW_TILES = D // LANES
SC_CORES, SC_SUBCORES = 2, 16
SC_WORKERS = SC_CORES * SC_SUBCORES
SC_WIN = 64
VMEM_LIMIT = 56 * 1024 * 1024


def _cparams(sem):
    return pltpu.CompilerParams(dimension_semantics=sem, vmem_limit_bytes=VMEM_LIMIT)


def _mod_row(i):
    return jnp.where(i < CTX_TILES, 0, 1 + (i - CTX_TILES) // SMP_TILES_PER_SEQ)


def _split3(x):
    hi = x.astype(BF16)
    r = x - hi.astype(F32)
    mid = r.astype(BF16)
    lo = (r - mid.astype(F32)).astype(BF16)
    return hi, mid, lo


def _sel_dot(sel, x):
    hi, mid, lo = _split3(x)
    acc = jnp.dot(sel, lo, preferred_element_type=F32)
    acc = acc + jnp.dot(sel, mid, preferred_element_type=F32)
    return acc + jnp.dot(sel, hi, preferred_element_type=F32)


def _dot_sel(x, sel):
    hi, mid, lo = _split3(x)
    acc = jnp.dot(lo, sel, preferred_element_type=F32)
    acc = acc + jnp.dot(mid, sel, preferred_element_type=F32)
    return acc + jnp.dot(hi, sel, preferred_element_type=F32)


def _dot_nt(a, b):
    return lax.dot_general(a, b, (((1,), (1,)), ((), ())), preferred_element_type=F32)


def _dot_tn(a, b):
    return lax.dot_general(a, b, (((0,), (0,)), ((), ())), preferred_element_type=F32)


def _lane(shape):
    return lax.broadcasted_iota(jnp.int32, shape, len(shape) - 1)


def _mod_kernel(c_ref, w_ref, b_ref, o_ref):
    c = c_ref[...]
    s = c * jax.nn.sigmoid(c)
    o_ref[0] = jnp.dot(s.astype(BF16), w_ref[0].astype(BF16), preferred_element_type=F32) + b_ref[0]


def _modulation(cvec, w_mod, b_mod):
    tn = 1536
    return pl.pallas_call(
        _mod_kernel,
        grid=(DEPTH, 6 * D // tn),
        in_specs=[
            pl.BlockSpec((MOD_ROWS, D), lambda l, j: (0, 0)),
            pl.BlockSpec((1, D, tn), lambda l, j: (l, 0, j)),
            pl.BlockSpec((1, 1, tn), lambda l, j: (l, 0, j)),
        ],
        out_specs=pl.BlockSpec((1, MOD_ROWS, tn), lambda l, j: (l, 0, j)),
        out_shape=jax.ShapeDtypeStruct((DEPTH, MOD_ROWS, 6 * D), F32),
        compiler_params=_cparams(("arbitrary", "arbitrary")),
        name="modulation",
    )(cvec, w_mod, b_mod.reshape(DEPTH, 1, 6 * D))


def _inproj_kernel(x_ref, g_ref, shift_ref, scale_ref, w_ref, o_ref):
    x = x_ref[...]
    y = x * lax.rsqrt(jnp.mean(x * x, axis=-1, keepdims=True) + EPS) * g_ref[...]
    h = y * (1.0 + scale_ref[0]) + shift_ref[0]
    o_ref[...] = jnp.dot(h.astype(BF16), w_ref[...], preferred_element_type=F32)


def _in_projection(x, g, mod3, w_in_bf):
    return pl.pallas_call(
        _inproj_kernel,
        grid=(N_TILES,),
        in_specs=[
            pl.BlockSpec((TM, D), lambda i: (i, 0)),
            pl.BlockSpec((1, D), lambda i: (0, 0)),
            pl.BlockSpec((1, 1, D), lambda i: (_mod_row(i), 0, 0)),
            pl.BlockSpec((1, 1, D), lambda i: (_mod_row(i), 0, 1)),
            pl.BlockSpec((D, IN_WIDTH), lambda i: (0, 0)),
        ],
        out_specs=pl.BlockSpec((TM, IN_WIDTH), lambda i: (i, 0)),
        out_shape=jax.ShapeDtypeStruct((T, IN_WIDTH), F32),
        compiler_params=_cparams(("arbitrary",)),
        name="in_projection",
    )(x, g.reshape(1, D), mod3, mod3, w_in_bf)


HG_C = TM
HG_LEVELS = tuple(2 ** j for j in range(1, int(math.log2(HG_C)) + 1))


def _hgrn_tables():
    t = np.arange(HG_C)[:, None]
    s = np.arange(HG_C)[None, :]
    x = t ^ s
    lvl = np.zeros((HG_C, HG_C), np.int32)
    nz = x > 0
    lvl[nz] = np.floor(np.log2(x[nz])).astype(np.int32) + 1
    fwd = np.where(t >= s, lvl, -1).astype(np.int32)
    bwd = np.where(t <= s, lvl, -1).astype(np.int32)
    tri_f = (t >= s).astype(np.float32)
    tri_b = (t <= s).astype(np.float32)
    return np.stack([fwd, bwd]), np.stack([tri_f, tri_b])


def _block_ref(cum, m, idx):
    c, l = cum.shape
    if m >= 16:
        c3 = cum.reshape(c // m, m, l)
        r = c3[:, idx:idx + 1, :]
        return jnp.broadcast_to(r, (c // m, m, l)).reshape(c, l)
    c3 = cum.reshape(c // 8, 8, l)
    sub = lax.broadcasted_iota(jnp.int32, c3.shape, 1)
    out = None
    for j in range(8 // m - 1, -1, -1):
        cand = jnp.broadcast_to(c3[:, j * m + idx:j * m + idx + 1, :], c3.shape)
        out = cand if out is None else jnp.where(sub < (j + 1) * m, cand, out)
    return out.reshape(c, l)


def _hgrn_kernel(q_ref, z_ref, v_ref, lb_ref, s0_ref, lvl_ref, tri_ref, o_ref, fin_ref, st_ref, *, layer, rev):
    g = pl.program_id(0)
    first = jnp.logical_or(g < CTX_TILES, (g - CTX_TILES) % SMP_TILES_PER_SEQ == 0)

    @pl.when(first)
    def _():
        st_ref[...] = s0_ref[0]

    qr = q_ref[...]
    q = qr * jax.nn.sigmoid(qr) * (A_DK ** -0.5)
    z = z_ref[...]
    if layer == 0:
        lf = jnp.minimum(z, 0.0) - jnp.log(1.0 + jnp.exp(-jnp.abs(z)))
        k = jax.nn.sigmoid(-z)
    else:
        lbd = lb_ref[...]
        lf = jnp.log(lbd + (1.0 - lbd) * jax.nn.sigmoid(z))
        k = (1.0 - lbd) * jax.nn.sigmoid(-z)
    v = v_ref[...]
    cum = _sel_dot(tri_ref[0], lf)
    lvl = lvl_ref[0]
    last_row = 0 if rev else HG_C - 1

    for p in range(2):
        sl = slice(p * LANES, (p + 1) * LANES)
        q_p, k_p, v_p, cum_p = q[:, sl], k[:, sl], v[:, sl], cum[:, sl]
        lane = _lane((HG_C, LANES))
        head_masks = (lane < A_DK, lane >= A_DK)
        v_bf = v_p.astype(BF16)
        k_bf = k_p.astype(BF16)
        scores = []
        for hm in head_masks:
            qm = jnp.where(hm, q_p, 0.0).astype(BF16)
            scores.append(jnp.where(lvl == 0, _dot_nt(qm, k_bf), 0.0))
        for li, m in enumerate(HG_LEVELS):
            ref = _block_ref(cum_p, m, m // 2 if rev else m // 2 - 1)
            qd = q_p * jnp.exp(jnp.minimum(cum_p - ref, 0.0))
            kd = (k_p * jnp.exp(jnp.minimum(ref - cum_p, 0.0))).astype(BF16)
            for hi, hm in enumerate(head_masks):
                qm = jnp.where(hm, qd, 0.0).astype(BF16)
                scores[hi] = jnp.where(lvl == li + 1, _dot_nt(qm, kd), scores[hi])
        st = st_ref[p]
        o_intra = jnp.where(head_masks[0],
                            jnp.dot(scores[0].astype(BF16), v_bf, preferred_element_type=F32),
                            jnp.dot(scores[1].astype(BF16), v_bf, preferred_element_type=F32))
        q0 = (q_p * jnp.exp(cum_p)).astype(BF16)
        o_ref[:, sl] = o_intra + _dot_nt(q0, st.astype(BF16))
        last = cum_p[last_row:last_row + 1, :]
        ks = (k_p * jnp.exp(last - cum_p)).astype(BF16)
        upd = _dot_tn(v_bf, ks)
        r = lax.broadcasted_iota(jnp.int32, (LANES, LANES), 0)
        cl = lax.broadcasted_iota(jnp.int32, (LANES, LANES), 1)
        same_head = (r < A_DK) == (cl < A_DK)
        st_new = st * jnp.exp(last) + jnp.where(same_head, upd, 0.0)
        st_ref[p] = st_new
        fin_ref[0, p] = st_new


def _hgrn_seq(g):
    return jnp.where(g < CTX_TILES, g, CTX_TILES + (g - CTX_TILES) // SMP_TILES_PER_SEQ)


def _hgrn_blk(g, rev):
    if not rev:
        return g
    j = g - CTX_TILES
    return jnp.where(g < CTX_TILES, g,
                     CTX_TILES + (j // SMP_TILES_PER_SEQ) * SMP_TILES_PER_SEQ
                     + (SMP_TILES_PER_SEQ - 1 - j % SMP_TILES_PER_SEQ))


def _hgrn_scan(proj, lb_dir, s0_dir, lvl, tri, layer, rev):
    d = 1 if rev else 0
    blk = functools.partial(_hgrn_blk, rev=rev)
    return pl.pallas_call(
        functools.partial(_hgrn_kernel, layer=layer, rev=rev),
        grid=(N_TILES,),
        in_specs=[
            pl.BlockSpec((HG_C, A_WIDTH), lambda g: (blk(g), 0)),
            pl.BlockSpec((HG_C, A_WIDTH), lambda g: (blk(g), 1 + d)),
            pl.BlockSpec((HG_C, A_WIDTH), lambda g: (blk(g), 3)),
            pl.BlockSpec((1, A_WIDTH), lambda g: (0, 0)),
            pl.BlockSpec((1, 2, LANES, LANES), lambda g: (_hgrn_seq(g), 0, 0, 0)),
            pl.BlockSpec((1, HG_C, HG_C), lambda g: (d, 0, 0)),
            pl.BlockSpec((1, HG_C, HG_C), lambda g: (d, 0, 0)),
        ],
        out_specs=[
            pl.BlockSpec((HG_C, A_WIDTH), lambda g: (blk(g), 0)),
            pl.BlockSpec((1, 2, LANES, LANES), lambda g: (_hgrn_seq(g), 0, 0, 0)),
        ],
        out_shape=[
            jax.ShapeDtypeStruct((T, A_WIDTH), F32),
            jax.ShapeDtypeStruct((N_SEQ, 2, LANES, LANES), F32),
        ],
        scratch_shapes=[pltpu.VMEM((2, LANES, LANES), F32)],
        compiler_params=_cparams(("arbitrary",)),
        name=f"hgrn_scan_{'bwd' if rev else 'fwd'}",
    )(proj, proj, proj, lb_dir, s0_dir, lvl, tri)


def _pack_state(s):
    n = s.shape[0]
    st = jnp.swapaxes(s, -1, -2).reshape(n, 2, 2, A_DK, A_DK)
    z = jnp.zeros_like(st[:, :, 0])
    top = jnp.concatenate([st[:, :, 0], z], axis=-1)
    bot = jnp.concatenate([z, st[:, :, 1]], axis=-1)
    return jnp.concatenate([top, bot], axis=-2)


def _unpack_state(sp):
    n = sp.shape[0]
    h0 = sp[:, :, :A_DK, :A_DK]
    h1 = sp[:, :, A_DK:, A_DK:]
    st = jnp.stack([h0, h1], axis=2).reshape(n, A_HEADS, A_DK, A_DK)
    return jnp.swapaxes(st, -1, -2)


def _half_rms(x, g):
    lane = _lane(x.shape)
    lo = lane < B_DK
    xx = x * x
    ms0 = jnp.sum(jnp.where(lo, xx, 0.0), axis=-1, keepdims=True) * (1.0 / B_DK)
    ms1 = jnp.sum(jnp.where(lo, 0.0, xx), axis=-1, keepdims=True) * (1.0 / B_DK)
    inv = jnp.where(lo, lax.rsqrt(ms0 + EPS), lax.rsqrt(ms1 + EPS))
    return x * inv * g


def _rope(x, cos, sin_signed):
    lane = _lane(x.shape)
    first = (lane % 32) < 16
    rot = jnp.where(first, pltpu.roll(x, LANES - 16, 1), pltpu.roll(x, 16, 1))
    return x * cos + rot * sin_signed


def _diff_softmax_pv(q_bf, keys_bf, vals_bf, lam):
    lane = _lane(q_bf.shape)
    zero = jnp.zeros_like(q_bf)
    acc = None
    parts = []
    for mp in range(2):
        qm = jnp.where((lane < B_DK) == (mp == 0), q_bf, zero)
        s = [_dot_nt(qm, kk) for kk in keys_bf]
        mx = functools.reduce(jnp.maximum, [jnp.max(si, axis=-1, keepdims=True) for si in s])
        e = [jnp.exp(si - mx) for si in s]
        den = functools.reduce(lambda a, b: a + b, [jnp.sum(ei, axis=-1, keepdims=True) for ei in e])
        parts.append((e, 1.0 / den))
    (e0, r0), (e1, r1) = parts
    r1 = r1 * lam
    for i in range(len(keys_bf)):
        a = (e0[i] * r0 - e1[i] * r1).astype(BF16)
        pv = jnp.dot(a, vals_bf[i], preferred_element_type=F32)
        acc = pv if acc is None else acc + pv
    return acc


def _subln(o, g, lam_init):
    return o * lax.rsqrt(jnp.mean(o * o, axis=-1, keepdims=True) + EPS) * g * (1.0 - lam_init)


def _attn_ctx_kernel(lam_ref, q_ref, k_ref, v_ref, gq_ref, gk_ref, gs_ref, o_ref, nk_ref, nv_ref, *, lam_init):
    lam = lam_ref[0, 0]
    qn = _half_rms(q_ref[...], gq_ref[...]) * (B_DK ** -0.5)
    kn = _half_rms(k_ref[...], gk_ref[...])
    v = v_ref[...]
    nk_ref[0, 0, 0] = kn[:, :B_DK]
    nk_ref[0, 1, 0] = kn[:, B_DK:]
    nv_ref[0, 0] = v
    o = _diff_softmax_pv(qn.astype(BF16), [kn.astype(BF16)], [v.astype(BF16)], lam)
    o_ref[...] = _subln(o, gs_ref[...], lam_init)


def _attn_ctx(proj, lam, gq2, gk2, gs, lam_init):
    qcol, kcol, vcol = 5 * A_WIDTH // LANES, 5 * A_WIDTH // LANES + 4, 5 * A_WIDTH // LANES + 8
    return pl.pallas_call(
        functools.partial(_attn_ctx_kernel, lam_init=lam_init),
        grid=(BATCH, B_HEADS),
        in_specs=[
            pl.BlockSpec(memory_space=pltpu.SMEM),
            pl.BlockSpec((SEQ, LANES), lambda b, h: (b, qcol + h)),
            pl.BlockSpec((SEQ, LANES), lambda b, h: (b, kcol + h)),
            pl.BlockSpec((SEQ, LANES), lambda b, h: (b, vcol + h)),
            pl.BlockSpec((1, LANES), lambda b, h: (0, 0)),
            pl.BlockSpec((1, LANES), lambda b, h: (0, 0)),
            pl.BlockSpec((1, LANES), lambda b, h: (0, 0)),
        ],
        out_specs=[
            pl.BlockSpec((SEQ, LANES), lambda b, h: (b, h)),
            pl.BlockSpec((1, 2, 1, SEQ, B_DK), lambda b, h: (b, 0, h, 0, 0)),
            pl.BlockSpec((1, 1, SEQ, B_DV), lambda b, h: (b, h, 0, 0)),
        ],
        out_shape=[
            jax.ShapeDtypeStruct((T_CTX, B_WIDTH), F32),
            jax.ShapeDtypeStruct((BATCH, 2, B_HEADS, SEQ, B_DK), F32),
            jax.ShapeDtypeStruct((BATCH, B_HEADS, SEQ, B_DV), F32),
        ],
        compiler_params=_cparams(("arbitrary", "arbitrary")),
        name="diff_attention_ctx",
    )(lam, proj, proj, proj, gq2, gk2, gs)


ATT_TQ = 256


def _attn_smp_kernel(lam_ref, q_ref, k_ref, v_ref, ck_ref, cv_ref, cos_ref, sin_ref, gq_ref, gk_ref, gs_ref,
                     o_ref, qs_ref, ks_ref, *, lam_init):
    lam = lam_ref[0, 0]
    cos = cos_ref[...]
    sin = sin_ref[...]
    qn = _rope(_half_rms(q_ref[...], gq_ref[...]), cos, sin) * (B_DK ** -0.5)
    qs_ref[...] = qn.astype(BF16)
    ks_ref[...] = _rope(_half_rms(k_ref[...], gk_ref[...]), cos, sin).astype(BF16)
    ck = jnp.concatenate([ck_ref[0, 0, 0, 0], ck_ref[0, 0, 1, 0]], axis=-1).astype(BF16)
    cv = cv_ref[0, 0, 0].astype(BF16)
    v_bf = v_ref[...].astype(BF16)
    k_bf = ks_ref[...]
    g = gs_ref[...]

    def body(i, carry):
        r0 = pl.multiple_of(i * ATT_TQ, ATT_TQ)
        q_bf = qs_ref[pl.ds(r0, ATT_TQ), :]
        o = _diff_softmax_pv(q_bf, [k_bf, ck], [v_bf, cv], lam)
        o_ref[pl.ds(r0, ATT_TQ), :] = _subln(o, g, lam_init)
        return carry

    lax.fori_loop(0, DEC_SEQ // ATT_TQ, body, 0)


def _attn_smp(proj, lam, cache_k, cache_v, cos, sin, gq2, gk2, gs, layer, lam_init):
    qcol, kcol, vcol = 5 * A_WIDTH // LANES, 5 * A_WIDTH // LANES + 4, 5 * A_WIDTH // LANES + 8
    r0 = T_CTX // DEC_SEQ
    return pl.pallas_call(
        functools.partial(_attn_smp_kernel, lam_init=lam_init),
        grid=(DEC_BATCH, B_HEADS),
        in_specs=[
            pl.BlockSpec(memory_space=pltpu.SMEM),
            pl.BlockSpec((DEC_SEQ, LANES), lambda b, h: (r0 + b, qcol + h)),
            pl.BlockSpec((DEC_SEQ, LANES), lambda b, h: (r0 + b, kcol + h)),
            pl.BlockSpec((DEC_SEQ, LANES), lambda b, h: (r0 + b, vcol + h)),
            pl.BlockSpec((1, 1, 2, 1, PAST, B_DK), lambda b, h: (b, layer, 0, h, 0, 0)),
            pl.BlockSpec((1, 1, 1, PAST, B_DV), lambda b, h: (b, layer, h, 0, 0)),
            pl.BlockSpec((DEC_SEQ, LANES), lambda b, h: (0, 0)),
            pl.BlockSpec((DEC_SEQ, LANES), lambda b, h: (0, 0)),
            pl.BlockSpec((1, LANES), lambda b, h: (0, 0)),
            pl.BlockSpec((1, LANES), lambda b, h: (0, 0)),
            pl.BlockSpec((1, LANES), lambda b, h: (0, 0)),
        ],
        out_specs=pl.BlockSpec((DEC_SEQ, LANES), lambda b, h: (b, h)),
        out_shape=jax.ShapeDtypeStruct((T_SMP, B_WIDTH), F32),
        scratch_shapes=[pltpu.VMEM((DEC_SEQ, LANES), BF16), pltpu.VMEM((DEC_SEQ, LANES), BF16)],
        compiler_params=_cparams(("arbitrary", "arbitrary")),
        name="diff_attention_smp",
    )(lam, proj, proj, proj, cache_k, cache_v, cos, sin, gq2, gk2, gs)


def _rope_tables():
    n_rows = DEC_SEQ // GRID_W
    row = np.repeat(np.arange(n_rows), GRID_W).astype(np.float32)
    col = np.tile(np.arange(GRID_W), n_rows).astype(np.float32)
    half = B_DK // 2
    inv_freq = (ROPE_BASE ** (-jnp.arange(0, half, 2, dtype=F32) / half))
    row_ang = jnp.asarray(row)[:, None] * inv_freq
    col_ang = jnp.asarray(col)[:, None] * inv_freq
    ang = jnp.concatenate([row_ang, row_ang, col_ang, col_ang], axis=-1)
    ang = jnp.concatenate([ang, ang], axis=-1)
    sign = np.where((np.arange(LANES) % 32) < 16, -1.0, 1.0).astype(np.float32)
    return jnp.cos(ang), jnp.sin(ang) * sign


CM_ROWS = 512


def _gelu(x):
    return 0.5 * x * (1.0 + lax.erf(x * (2.0 ** -0.5)))


def _cmlp_kernel(u_ref, v_ref, g_ref, b_ref, ws_ref, bs_ref, o_ref):
    u = _gelu(u_ref[...])
    gv = _gelu(v_ref[...])
    mu = jnp.mean(gv, axis=-1, keepdims=True)
    dv = gv - mu
    var = jnp.mean(dv * dv, axis=-1, keepdims=True)
    vn = (dv * lax.rsqrt(var + EPS) * g_ref[...] + b_ref[...]).astype(BF16)
    lane = _lane((C_CHUNK, LANES))
    for c in range(CM_ROWS // C_CHUNK):
        rs = slice(c * C_CHUNK, (c + 1) * C_CHUNK)
        for p in range(2):
            cs = slice(p * LANES, (p + 1) * LANES)
            vp = vn[rs, cs]
            m0 = jnp.dot(ws_ref[2 * p].astype(BF16), vp, preferred_element_type=F32)
            m1 = jnp.dot(ws_ref[2 * p + 1].astype(BF16), vp, preferred_element_type=F32)
            mixed = jnp.where(lane < C_DG, m0, m1) + bs_ref[:, cs]
            o_ref[rs, cs] = u[rs, cs] * mixed


def _chunk_mlp(proj, ln_g, ln_b, w_s, bias_full):
    ucol = (5 * A_WIDTH + 3 * B_WIDTH) // C_WIDTH
    return pl.pallas_call(
        _cmlp_kernel,
        grid=(T // CM_ROWS,),
        in_specs=[
            pl.BlockSpec((CM_ROWS, C_WIDTH), lambda i: (i, ucol)),
            pl.BlockSpec((CM_ROWS, C_WIDTH), lambda i: (i, ucol + 1)),
            pl.BlockSpec((1, C_WIDTH), lambda i: (0, 0)),
            pl.BlockSpec((1, C_WIDTH), lambda i: (0, 0)),
            pl.BlockSpec((C_GROUPS, C_CHUNK, C_CHUNK), lambda i: (0, 0, 0)),
            pl.BlockSpec((C_CHUNK, C_WIDTH), lambda i: (0, 0)),
        ],
        out_specs=pl.BlockSpec((CM_ROWS, C_WIDTH), lambda i: (i, 0)),
        out_shape=jax.ShapeDtypeStruct((T, C_WIDTH), F32),
        compiler_params=_cparams(("arbitrary",)),
        name="chunk_mlp",
    )(proj, proj, ln_g.reshape(1, C_WIDTH), ln_b.reshape(1, C_WIDTH), w_s, bias_full)


def _postmix_kernel(of_ref, ob_ref, ag_ref, hg_ref, hsel_ref, bc_ref, bs_ref, c_ref, w_ref, x_ref, gate1_ref,
                    shift2_ref, scale2_ref, g2_ref, wrh_ref, wrl_ref, br_ref, x1_ref, h2_ref, idx_ref, gw_ref):
    o = of_ref[...] + ob_ref[...]
    ms = _dot_sel(o * o, hsel_ref[...]) * (1.0 / A_DK)
    ag = ag_ref[...]
    a = o * lax.rsqrt(ms + EPS) * hg_ref[...] * (ag * jax.nn.sigmoid(ag))
    b = jnp.where(pl.program_id(0) < CTX_TILES, bc_ref[...], bs_ref[...])
    mixed = jnp.dot(a.astype(BF16), w_ref[0:A_WIDTH, :], preferred_element_type=F32)
    mixed = mixed + jnp.dot(b.astype(BF16), w_ref[A_WIDTH:A_WIDTH + B_WIDTH, :], preferred_element_type=F32)
    mixed = mixed + jnp.dot(c_ref[...].astype(BF16), w_ref[A_WIDTH + B_WIDTH:, :], preferred_element_type=F32)
    x1 = x_ref[...] + gate1_ref[0] * mixed
    x1_ref[...] = x1
    y = x1 * lax.rsqrt(jnp.mean(x1 * x1, axis=-1, keepdims=True) + EPS) * g2_ref[...]
    h2 = y * (1.0 + scale2_ref[0]) + shift2_ref[0]
    h2_ref[...] = pltpu.einshape("t(jl)->tjl", h2, l=LANES)
    hi = h2.astype(BF16)
    lo = (h2 - hi.astype(F32)).astype(BF16)
    lg = jnp.dot(lo, wrh_ref[...], preferred_element_type=F32)
    lg = lg + jnp.dot(hi, wrl_ref[...], preferred_element_type=F32)
    lg = lg + jnp.dot(hi, wrh_ref[...], preferred_element_type=F32) + br_ref[...]
    lt = lg.T[:N_EXPERTS]
    row = lax.broadcasted_iota(jnp.int32, lt.shape, 0)
    out_row = lax.broadcasted_iota(jnp.int32, (8, TM), 0)
    idx_out = jnp.zeros((8, TM), jnp.int32)
    val_out = jnp.zeros((8, TM), F32)
    top0 = None
    den = None
    for kk in range(TOP_K):
        mx = jnp.max(lt, axis=0, keepdims=True)
        am = jnp.min(jnp.where(lt == mx, row, N_EXPERTS), axis=0, keepdims=True)
        if kk == 0:
            top0 = mx
        e = jnp.exp(mx - top0)
        den = e if den is None else den + e
        idx_out = jnp.where(out_row == kk, am, idx_out)
        val_out = jnp.where(out_row == kk, e, val_out)
        lt = jnp.where(row == am, -jnp.inf, lt)
    idx_ref[...] = idx_out
    gw_ref[...] = val_out / den


def _post_mix(o_f, o_b, proj, hg, hsel, b_ctx, b_smp, c_out, w_out_bf, x, mod3, g2, wr_hi, wr_lo, br_pad):
    tile = lambda w: pl.BlockSpec((TM, w), lambda i: (i, 0))
    const = lambda shape: pl.BlockSpec(shape, lambda i: tuple(0 for _ in shape))
    modspec = lambda j: pl.BlockSpec((1, 1, D), lambda i: (_mod_row(i), 0, j))
    rowsT = pl.BlockSpec((8, TM), lambda i: (0, i))
    return pl.pallas_call(
        _postmix_kernel,
        grid=(N_TILES,),
        in_specs=[
            tile(A_WIDTH), tile(A_WIDTH),
            pl.BlockSpec((TM, A_WIDTH), lambda i: (i, 4)),
            const((1, A_WIDTH)), const((A_WIDTH, A_WIDTH)),
            pl.BlockSpec((TM, B_WIDTH), lambda i: (jnp.minimum(i, CTX_TILES - 1), 0)),
            pl.BlockSpec((TM, B_WIDTH), lambda i: (jnp.maximum(i - CTX_TILES, 0), 0)),
            tile(C_WIDTH),
            const((D, D)),
            tile(D),
            modspec(2), modspec(3), modspec(4),
            const((1, D)), const((D, LANES)), const((D, LANES)), const((1, LANES)),
        ],
        out_specs=[tile(D), pl.BlockSpec((TM, ROW_TILES, LANES), lambda i: (i, 0, 0)), rowsT, rowsT],
        out_shape=[
            jax.ShapeDtypeStruct((T, D), F32),
            jax.ShapeDtypeStruct((T, ROW_TILES, LANES), F32),
            jax.ShapeDtypeStruct((8, T), jnp.int32),
            jax.ShapeDtypeStruct((8, T), F32),
        ],
        compiler_params=_cparams(("arbitrary",)),
        name="post_mix_router",
    )(o_f, o_b, proj, hg, hsel, b_ctx, b_smp, c_out, w_out_bf, x, mod3, mod3, mod3, g2.reshape(1, D),
      wr_hi, wr_lo, br_pad)


def _route_kernel(idx_ref, dest_ref, meta_ref):
    erow = lax.broadcasted_iota(jnp.int32, (N_EXPERTS, TM), 0)
    s_i = lax.broadcasted_iota(jnp.int32, (TM, TM), 0)
    t_i = lax.broadcasted_iota(jnp.int32, (TM, TM), 1)
    earlier = (s_i < t_i).astype(BF16)
    out_row = lax.broadcasted_iota(jnp.int32, (8, TM), 0)

    def onehots(i):
        idx = idx_ref[:, pl.ds(pl.multiple_of(i * TM, TM), TM)]
        return [(erow == idx[kk:kk + 1, :]) for kk in range(TOP_K)]

    def count_tile(i, run):
        ohs = onehots(i)
        base = run
        pos = jnp.zeros((8, TM), F32)
        for kk in range(TOP_K):
            ohf = ohs[kk].astype(F32)
            before = jnp.dot(ohs[kk].astype(BF16), earlier, preferred_element_type=F32)
            p = jnp.sum(ohf * (base + before), axis=0, keepdims=True)
            pos = jnp.where(out_row == kk, p, pos)
            base = base + jnp.sum(ohf, axis=1, keepdims=True)
        dest_ref[:, pl.ds(pl.multiple_of(i * TM, TM), TM)] = pos.astype(jnp.int32)
        return base

    counts = lax.fori_loop(0, N_TILES, count_tile, jnp.zeros((N_EXPERTS, 1), F32)).astype(jnp.int32)
    bm_shift = MOE_BM.bit_length() - 1
    padded = lax.shift_left(lax.shift_right_logical(counts + (MOE_BM - 1), bm_shift), bm_shift)
    e_r = lax.broadcasted_iota(jnp.int32, (N_EXPERTS, N_EXPERTS), 0)
    e_c = lax.broadcasted_iota(jnp.int32, (N_EXPERTS, N_EXPERTS), 1)
    incl = (e_c <= e_r).astype(BF16)
    pad_end = _sel_dot(incl, jnp.broadcast_to(padded.astype(F32), (N_EXPERTS, LANES)))[:, :1]
    pad_start = pad_end - padded.astype(F32)

    def place_tile(i, carry):
        ohs = onehots(i)
        sl = pl.ds(pl.multiple_of(i * TM, TM), TM)
        off = jnp.zeros((8, TM), F32)
        for kk in range(TOP_K):
            o = jnp.sum(ohs[kk].astype(F32) * pad_start, axis=0, keepdims=True)
            off = jnp.where(out_row == kk, o, off)
        dest_ref[:, sl] = dest_ref[:, sl] + off.astype(jnp.int32)
        return carry

    lax.fori_loop(0, N_TILES, place_tile, 0)

    total = jnp.max(pad_end, axis=0, keepdims=True)
    lane_i = lax.broadcasted_iota(jnp.int32, (1, TM), 1)
    blk0 = (lane_i * MOE_BM).astype(F32)
    block_e = jnp.sum((pad_end <= blk0).astype(F32), axis=0, keepdims=True)
    live_end = pad_start + counts.astype(F32)
    sel = erow.astype(F32) == block_e
    live = jnp.sum(jnp.where(sel, live_end, 0.0), axis=0, keepdims=True)
    valid = jnp.where(blk0 < total, jnp.clip(live - blk0, 0.0, float(MOE_BM)), 0.0)
    own = erow == lane_i
    n_blk = jnp.sum(jnp.where(own, padded.astype(F32), 0.0), axis=0, keepdims=True) * (1.0 / MOE_BM)
    first_blk = jnp.sum(jnp.where(own, pad_start, 0.0), axis=0, keepdims=True) * (1.0 / MOE_BM)
    meta = jnp.where(out_row == 0, valid, 0.0)
    meta = jnp.where(out_row == 1, n_blk, meta)
    meta = jnp.where(out_row == 2, first_blk, meta)
    meta = jnp.where(out_row == 3, total * (1.0 / MOE_BM), meta)
    meta_ref[...] = meta.astype(jnp.int32)


def _route(idx_t):
    assert MOE_BLOCKS <= TM
    return pl.pallas_call(
        _route_kernel,
        out_shape=[jax.ShapeDtypeStruct((8, T), jnp.int32), jax.ShapeDtypeStruct((8, TM), jnp.int32)],
        compiler_params=pltpu.CompilerParams(vmem_limit_bytes=VMEM_LIMIT),
        name="moe_route",
    )(idx_t)


def _moe_kernel(bv_ref, nb_ref, g0_ref, tot_ref, x_hbm, wgu_ref, bgu_ref, wdn_ref, bdn_ref, y_hbm,
                wgu_bf, wdn_bf, xbuf, ybuf, xb_ref, xsem, ysem):
    e = pl.program_id(0)
    n_blk = nb_ref[e]
    first = g0_ref[e]
    total = tot_ref[0]

    def x_copy(g, slot):
        return pltpu.make_async_copy(x_hbm.at[pl.ds(g * MOE_BM, MOE_BM)], xbuf.at[slot], xsem.at[slot])

    def y_copy(g, slot):
        return pltpu.make_async_copy(ybuf.at[slot], y_hbm.at[pl.ds(g * MOE_BM, MOE_BM)], ysem.at[slot])

    @pl.when(jnp.logical_and(e == 0, total > 0))
    def _():
        x_copy(0, 0).start()

    @pl.when(n_blk > 0)
    def _():
        wgu_bf[...] = wgu_ref[0, 0].astype(BF16)
        wdn_bf[...] = wdn_ref[0, 0].astype(BF16)

    def block(j, carry):
        g = first + j
        slot = g % 2
        x_copy(g, slot).wait()

        @pl.when(g + 1 < total)
        def _():
            x_copy(g + 1, 1 - slot).start()

        @pl.when(g >= 2)
        def _():
            y_copy(g - 2, slot).wait()

        live = lax.broadcasted_iota(jnp.int32, (MOE_BM, LANES), 0) < bv_ref[g]
        xt = pltpu.einshape("tjl->jtl", xbuf[slot])
        for jj in range(ROW_TILES):
            xb_ref[:, jj * LANES:(jj + 1) * LANES] = jnp.where(live, xt[jj], 0.0).astype(BF16)
        gu = jnp.dot(xb_ref[...], wgu_bf[...], preferred_element_type=F32) + bgu_ref[0, 0]
        glu = jnp.minimum(gu[:, :D], SWIGLU_LIMIT)
        lin = jnp.clip(gu[:, D:], -SWIGLU_LIMIT, SWIGLU_LIMIT)
        act = glu * jax.nn.sigmoid(SWIGLU_ALPHA * glu) * (lin + 1.0)
        y = jnp.dot(act.astype(BF16), wdn_bf[...], preferred_element_type=F32) + bdn_ref[0, 0]
        ybuf[slot] = pltpu.einshape("t(jl)->tjl", y, l=LANES)
        y_copy(g, slot).start()
        return carry

    lax.fori_loop(0, n_blk, block, 0)

    @pl.when(e == N_EXPERTS - 1)
    def _():
        @pl.when(total >= 2)
        def _():
            y_copy(total - 2, total % 2).wait()

        @pl.when(total >= 1)
        def _():
            y_copy(total - 1, (total - 1) % 2).wait()

        ybuf[0] = jnp.zeros((MOE_BM, ROW_TILES, LANES), F32)

        def fill(g, carry):
            y_copy(g, 0).start()
            y_copy(g, 0).wait()
            return carry

        lax.fori_loop(total, MOE_BLOCKS, fill, 0)


def _moe_ffn(block_valid, n_blk, first_blk, total_blk, xs, w_gu, b_gu, w_dn, b_dn, layer):
    rows = (MOE_BM, ROW_TILES, LANES)
    return pl.pallas_call(
        _moe_kernel,
        grid_spec=pltpu.PrefetchScalarGridSpec(
            num_scalar_prefetch=4,
            grid=(N_EXPERTS,),
            in_specs=[
                pl.BlockSpec(memory_space=pl.ANY),
                pl.BlockSpec((1, 1, D, 2 * D), lambda e, *_: (layer, e, 0, 0)),
                pl.BlockSpec((1, 1, 1, 2 * D), lambda e, *_: (layer, e, 0, 0)),
                pl.BlockSpec((1, 1, D, D), lambda e, *_: (layer, e, 0, 0)),
                pl.BlockSpec((1, 1, 1, D), lambda e, *_: (layer, e, 0, 0)),
            ],
            out_specs=pl.BlockSpec(memory_space=pl.ANY),
            scratch_shapes=[
                pltpu.VMEM((D, 2 * D), BF16), pltpu.VMEM((D, D), BF16),
                pltpu.VMEM((2,) + rows, F32), pltpu.VMEM((2,) + rows, F32), pltpu.VMEM((MOE_BM, D), BF16),
                pltpu.SemaphoreType.DMA((2,)), pltpu.SemaphoreType.DMA((2,)),
            ],
        ),
        out_shape=jax.ShapeDtypeStruct((MOE_ROWS, ROW_TILES, LANES), F32),
        compiler_params=_cparams(("arbitrary",)),
        name="moe_expert_ffn",
    )(block_valid, n_blk, first_blk, total_blk, xs, w_gu, b_gu.reshape(DEPTH, N_EXPERTS, 1, 2 * D), w_dn,
      b_dn.reshape(DEPTH, N_EXPERTS, 1, D))


def _combine_kernel(x1_ref, y_ref, gw_ref, gate2_ref, o_ref):
    gw = jnp.concatenate([gw_ref[...], jnp.zeros((LANES - 8, TM), F32)], axis=0).T
    ys = [pltpu.einshape("tjl->jtl", y_ref[kk]) for kk in range(TOP_K)]
    for j in range(ROW_TILES):
        cs = slice(j * LANES, (j + 1) * LANES)
        acc = None
        for kk in range(TOP_K):
            term = ys[kk][j] * gw[:, kk:kk + 1]
            acc = term if acc is None else acc + term
        o_ref[:, cs] = x1_ref[:, cs] + gate2_ref[0, :, cs] * acc


def _combine(x1, yg, gw, mod3):
    return pl.pallas_call(
        _combine_kernel,
        grid=(N_TILES,),
        in_specs=[
            pl.BlockSpec((TM, D), lambda i: (i, 0)),
            pl.BlockSpec((TOP_K, TM, ROW_TILES, LANES), lambda i: (0, i, 0, 0)),
            pl.BlockSpec((8, TM), lambda i: (0, i)),
            pl.BlockSpec((1, 1, D), lambda i: (_mod_row(i), 0, 5)),
        ],
        out_specs=pl.BlockSpec((TM, D), lambda i: (i, 0)),
        out_shape=jax.ShapeDtypeStruct((T, D), F32),
        compiler_params=_cparams(("arbitrary",)),
        name="moe_combine",
    )(x1, yg, gw, mod3)


def _sc_mesh():
    return plsc.VectorSubcoreMesh(core_axis_name="c", subcore_axis_name="s")


def _sc_worker():
    return lax.axis_index("s") * SC_CORES + lax.axis_index("c")


def _sc_dispatch(h2t, dest_km):
    per_w = T // SC_WORKERS

    @functools.partial(
        pl.kernel, mesh=_sc_mesh(),
        out_type=jax.ShapeDtypeStruct((MOE_ROWS, ROW_TILES, LANES), F32),
        scratch_types=[pltpu.VMEM((SC_WIN,), jnp.int32), pltpu.VMEM((SC_WIN, ROW_TILES, LANES), F32),
                       pltpu.SemaphoreType.DMA],
    )
    def run(h_hbm, d_hbm, o_hbm, idx_v, rows_v, sem):
        w0 = _sc_worker() * per_w

        @pl.loop(0, per_w // SC_WIN)
        def _(w):
            base = pl.multiple_of(w0 + w * SC_WIN, SC_WIN)
            pltpu.sync_copy(h_hbm.at[pl.ds(base, SC_WIN)], rows_v)
            for kk in range(TOP_K):
                pltpu.sync_copy(d_hbm.at[pl.ds(kk * T + base, SC_WIN)], idx_v)
                pltpu.async_copy(rows_v, o_hbm.at[idx_v], sem).wait()

    return run(h2t, dest_km)


def _sc_gather(yb, dest_km):
    n = TOP_K * T
    per_w = n // SC_WORKERS

    @functools.partial(
        pl.kernel, mesh=_sc_mesh(),
        out_type=jax.ShapeDtypeStruct((n, ROW_TILES, LANES), F32),
        scratch_types=[pltpu.VMEM((SC_WIN,), jnp.int32), pltpu.VMEM((SC_WIN, ROW_TILES, LANES), F32),
                       pltpu.SemaphoreType.DMA],
    )
    def run(y_hbm, d_hbm, o_hbm, idx_v, rows_v, sem):
        w0 = _sc_worker() * per_w

        @pl.loop(0, per_w // SC_WIN)
        def _(w):
            base = pl.multiple_of(w0 + w * SC_WIN, SC_WIN)
            pltpu.sync_copy(d_hbm.at[pl.ds(base, SC_WIN)], idx_v)
            pltpu.async_copy(y_hbm.at[idx_v], rows_v, sem).wait()
            pltpu.sync_copy(rows_v, o_hbm.at[pl.ds(base, SC_WIN)])

    return run(yb, dest_km)


def kernel(x_prompt, x_sample, c, cache_diff_k, cache_diff_v, state_hgrn, c_ctx, norm_mix_g, norm_ffn_g, w_mod, b_mod, w_in, w_out, hgrn_lower_bounds, hgrn_norm_g, diff_q_norm_g, diff_k_norm_g, diff_lambda_q1, diff_lambda_k1, diff_lambda_q2, diff_lambda_k2, diff_subln_g, cmlp_ln_g, cmlp_ln_b, cmlp_w_s, cmlp_b_s, router_w, router_b, moe_w_gate_up, moe_b_gate_up, moe_w_down, moe_b_down):
    x = jnp.concatenate([x_prompt.reshape(T_CTX, D), x_sample.reshape(T_SMP, D)], axis=0)
    cvec = jnp.concatenate([c_ctx[None, :], c, jnp.zeros((MOD_ROWS - 1 - DEC_BATCH, D), F32)], axis=0)
    mod = _modulation(cvec, w_mod, b_mod)

    lvl_np, tri_np = _hgrn_tables()
    lvl = jnp.asarray(lvl_np)
    tri = jnp.asarray(tri_np, dtype=BF16)
    cos, sin = _rope_tables()
    hsel = jnp.asarray(np.kron(np.eye(A_HEADS), np.ones((A_DK, A_DK))), dtype=BF16)
    sm = jax.nn.softmax(hgrn_lower_bounds.astype(F32), axis=0)
    lb_all = jnp.cumsum(sm, axis=0) - sm[0]

    new_k, new_v, new_s = [], [], []
    for l in range(DEPTH):
        mod3 = mod[l].reshape(MOD_ROWS, 1, 6 * D)
        proj = _in_projection(x, norm_mix_g[l], mod3, w_in[l].astype(BF16))

        s0 = jnp.concatenate([jnp.zeros((BATCH, 2, A_HEADS, A_DK, A_DK), F32), state_hgrn[:, l]], axis=0)
        o_dir, fin_dir = [], []
        for d in range(2):
            o_d, fin_d = _hgrn_scan(proj, lb_all[l, d].reshape(1, A_WIDTH), _pack_state(s0[:, d]),
                                    lvl, tri, l, d == 1)
            o_dir.append(o_d)
            fin_dir.append(_unpack_state(fin_d[:BATCH]))
        new_s.append(jnp.stack(fin_dir, axis=1))

        lam_init = 0.8 - 0.6 * math.exp(-0.3 * l)
        lam = (jnp.exp(jnp.sum(diff_lambda_q1[l] * diff_lambda_k1[l]))
               - jnp.exp(jnp.sum(diff_lambda_q2[l] * diff_lambda_k2[l])) + lam_init).reshape(1, 1)
        gq2 = jnp.tile(diff_q_norm_g[l], 2).reshape(1, LANES)
        gk2 = jnp.tile(diff_k_norm_g[l], 2).reshape(1, LANES)
        gs = diff_subln_g[l].reshape(1, LANES)
        b_ctx, k_l, v_l = _attn_ctx(proj, lam, gq2, gk2, gs, lam_init)
        b_smp = _attn_smp(proj, lam, cache_diff_k, cache_diff_v, cos, sin, gq2, gk2, gs, l, lam_init)
        new_k.append(k_l)
        new_v.append(v_l)

        bias_full = jnp.repeat(cmlp_b_s[l].T, C_DG, axis=1)
        c_out = _chunk_mlp(proj, cmlp_ln_g[l], cmlp_ln_b[l], cmlp_w_s[l], bias_full)

        hg = jnp.tile(hgrn_norm_g[l], A_HEADS).reshape(1, A_WIDTH)
        wr_pad = jnp.pad(router_w[l], ((0, 0), (0, LANES - N_EXPERTS)))
        wr_hi = wr_pad.astype(BF16)
        wr_lo = (wr_pad - wr_hi.astype(F32)).astype(BF16)
        br_pad = jnp.pad(router_b[l], (0, LANES - N_EXPERTS)).reshape(1, LANES)
        x1, h2, idx_t, gw_t = _post_mix(o_dir[0], o_dir[1], proj, hg, hsel, b_ctx, b_smp, c_out,
                                        w_out[l].astype(BF16), x, mod3, norm_ffn_g[l], wr_hi, wr_lo, br_pad)

        dest_t, meta = _route(idx_t)
        dest_km = dest_t[:TOP_K].reshape(-1)
        xs = _sc_dispatch(h2, dest_km)
        yb = _moe_ffn(meta[0, :MOE_BLOCKS], meta[1, :N_EXPERTS], meta[2, :N_EXPERTS], meta[3, :1], xs,
                      moe_w_gate_up, moe_b_gate_up, moe_w_down, moe_b_down, l)
        yg = _sc_gather(yb, dest_km).reshape(TOP_K, T, ROW_TILES, LANES)
        x = _combine(x1, yg, gw_t, mod3)

    y_prompt = x[:T_CTX].reshape(BATCH, SEQ, D)
    y_sample = x[T_CTX:].reshape(DEC_BATCH, DEC_SEQ, D)
    return (y_prompt, y_sample, jnp.stack(new_k, axis=1), jnp.stack(new_v, axis=1), jnp.stack(new_s, axis=1))
```

```python
import functools
import math

import numpy as np
import jax
import jax.numpy as jnp
from jax import lax
from jax.experimental import pallas as pl
from jax.experimental.pallas import tpu as pltpu
from jax.experimental.pallas import tpu_sc as plsc

F32 = jnp.float32
BF16 = jnp.bfloat16

D = 1024
DEPTH = 2
BATCH, SEQ = 16, 256
DEC_BATCH, DEC_SEQ = 8, 1024
PAST = 512
GRID_W = 64
A_HEADS, A_DK = 4, 64
A_WIDTH = 256
B_HEADS, B_DK, B_DV = 4, 64, 128
B_WIDTH = 512
C_GROUPS, C_CHUNK, C_WIDTH, C_DG = 4, 128, 256, 64
IN_WIDTH = 5 * A_WIDTH + 3 * B_WIDTH + 2 * C_WIDTH
N_EXPERTS, TOP_K = 32, 4
SWIGLU_LIMIT, SWIGLU_ALPHA = 7.0, 1.702
ROPE_BASE = 10000.0
EPS = 1e-6

T_CTX = BATCH * SEQ
T_SMP = DEC_BATCH * DEC_SEQ
T = T_CTX + T_SMP
N_SEQ = BATCH + DEC_BATCH
MOD_ROWS = 16

TM = 256
N_TILES = T // TM
CTX_TILES = T_CTX // TM
SMP_TILES_PER_SEQ = DEC_SEQ // TM
LANES = 128
MOE_BM = 256
MOE_ROWS = T * TOP_K + N_EXPERTS * MOE_BM
MOE_BLOCKS = MOE_ROWS // MOE_BM
MOE_RING = 4
ROW_TILES = D // LANES
SC_CORES, SC_SUBCORES = 2, 16
SC_WORKERS = SC_CORES * SC_SUBCORES
SC_WIN = 64
VMEM_LIMIT = 56 * 1024 * 1024


def _cparams(sem):
    return pltpu.CompilerParams(dimension_semantics=sem, vmem_limit_bytes=VMEM_LIMIT)


def _mod_row(i):
    return jnp.where(i < CTX_TILES, 0, 1 + (i - CTX_TILES) // SMP_TILES_PER_SEQ)


def _split3(x):
    hi = x.astype(BF16)
    r = x - hi.astype(F32)
    mid = r.astype(BF16)
    lo = (r - mid.astype(F32)).astype(BF16)
    return hi, mid, lo


def _sel_dot(sel, x):
    hi, mid, lo = _split3(x)
    acc = jnp.dot(sel, lo, preferred_element_type=F32)
    acc = acc + jnp.dot(sel, mid, preferred_element_type=F32)
    return acc + jnp.dot(sel, hi, preferred_element_type=F32)


def _dot_sel(x, sel):
    hi, mid, lo = _split3(x)
    acc = jnp.dot(lo, sel, preferred_element_type=F32)
    acc = acc + jnp.dot(mid, sel, preferred_element_type=F32)
    return acc + jnp.dot(hi, sel, preferred_element_type=F32)


def _dot_nt(a, b):
    return lax.dot_general(a, b, (((1,), (1,)), ((), ())), preferred_element_type=F32)


def _dot_tn(a, b):
    return lax.dot_general(a, b, (((0,), (0,)), ((), ())), preferred_element_type=F32)


def _lane(shape):
    return lax.broadcasted_iota(jnp.int32, shape, len(shape) - 1)


def _mod_kernel(c_ref, w_ref, b_ref, o_ref):
    c = c_ref[...]
    s = c * jax.nn.sigmoid(c)
    o_ref[0] = jnp.dot(s.astype(BF16), w_ref[0].astype(BF16), preferred_element_type=F32) + b_ref[0]


def _modulation(cvec, w_mod, b_mod):
    tn = 1536
    return pl.pallas_call(
        _mod_kernel,
        grid=(DEPTH, 6 * D // tn),
        in_specs=[
            pl.BlockSpec((MOD_ROWS, D), lambda l, j: (0, 0)),
            pl.BlockSpec((1, D, tn), lambda l, j: (l, 0, j)),
            pl.BlockSpec((1, 1, tn), lambda l, j: (l, 0, j)),
        ],
        out_specs=pl.BlockSpec((1, MOD_ROWS, tn), lambda l, j: (l, 0, j)),
        out_shape=jax.ShapeDtypeStruct((DEPTH, MOD_ROWS, 6 * D), F32),
        compiler_params=_cparams(("arbitrary", "arbitrary")),
        name="modulation",
    )(cvec, w_mod, b_mod.reshape(DEPTH, 1, 6 * D))


def _stream_specs(x):
    if isinstance(x, tuple):
        return [pl.BlockSpec((TM, D), lambda i: (jnp.minimum(i, CTX_TILES - 1), 0)),
                pl.BlockSpec((TM, D), lambda i: (jnp.maximum(i - CTX_TILES, 0), 0))], list(x)
    return [pl.BlockSpec((TM, D), lambda i: (i, 0))], [x]


def _stream_tile(x_refs):
    if len(x_refs) == 1:
        return x_refs[0][...]
    return jnp.where(pl.program_id(0) < CTX_TILES, x_refs[0][...], x_refs[1][...])


def _inproj_kernel(*refs, n_x):
    g_ref, shift_ref, scale_ref, w_ref, o_ref = refs[n_x:]
    x = _stream_tile(refs[:n_x])
    y = x * lax.rsqrt(jnp.mean(x * x, axis=-1, keepdims=True) + EPS) * g_ref[...]
    h = y * (1.0 + scale_ref[0]) + shift_ref[0]
    o_ref[...] = jnp.dot(h.astype(BF16), w_ref[...], preferred_element_type=F32)


def _in_projection(x, g, mod3, w_in_bf):
    x_specs, x_args = _stream_specs(x)
    return pl.pallas_call(
        functools.partial(_inproj_kernel, n_x=len(x_args)),
        grid=(N_TILES,),
        in_specs=x_specs + [
            pl.BlockSpec((1, D), lambda i: (0, 0)),
            pl.BlockSpec((1, 1, D), lambda i: (_mod_row(i), 0, 0)),
            pl.BlockSpec((1, 1, D), lambda i: (_mod_row(i), 0, 1)),
            pl.BlockSpec((D, IN_WIDTH), lambda i: (0, 0)),
        ],
        out_specs=pl.BlockSpec((TM, IN_WIDTH), lambda i: (i, 0)),
        out_shape=jax.ShapeDtypeStruct((T, IN_WIDTH), F32),
        compiler_params=_cparams(("arbitrary",)),
        name="in_projection",
    )(*x_args, g.reshape(1, D), mod3, mod3, w_in_bf)


HG_C = TM
HG_LEVELS = tuple(2 ** j for j in range(1, int(math.log2(HG_C)) + 1))


def _hgrn_tables():
    t = np.arange(HG_C)[:, None]
    s = np.arange(HG_C)[None, :]
    x = t ^ s
    lvl = np.zeros((HG_C, HG_C), np.int32)
    nz = x > 0
    lvl[nz] = np.floor(np.log2(x[nz])).astype(np.int32) + 1
    fwd = np.where(t >= s, lvl, -1).astype(np.int32)
    bwd = np.where(t <= s, lvl, -1).astype(np.int32)
    tri_f = (t >= s).astype(np.float32)
    tri_b = (t <= s).astype(np.float32)
    return np.stack([fwd, bwd]), np.stack([tri_f, tri_b])


def _block_ref(cum, m, idx):
    c, l = cum.shape
    if m >= 16:
        c3 = cum.reshape(c // m, m, l)
        r = c3[:, idx:idx + 1, :]
        return jnp.broadcast_to(r, (c // m, m, l)).reshape(c, l)
    c3 = cum.reshape(c // 8, 8, l)
    sub = lax.broadcasted_iota(jnp.int32, c3.shape, 1)
    out = None
    for j in range(8 // m - 1, -1, -1):
        cand = jnp.broadcast_to(c3[:, j * m + idx:j * m + idx + 1, :], c3.shape)
        out = cand if out is None else jnp.where(sub < (j + 1) * m, cand, out)
    return out.reshape(c, l)


def _hgrn_kernel(q_ref, z_ref, v_ref, lb_ref, s0_ref, lvl_ref, tri_ref, o_ref, fin_ref, st_ref, *, layer, rev):
    g = pl.program_id(0)
    first = jnp.logical_or(g < CTX_TILES, (g - CTX_TILES) % SMP_TILES_PER_SEQ == 0)

    @pl.when(first)
    def _():
        st_ref[...] = s0_ref[0]

    qr = q_ref[...]
    q = qr * jax.nn.sigmoid(qr) * (A_DK ** -0.5)
    z = z_ref[...]
    if layer == 0:
        lf = jnp.minimum(z, 0.0) - jnp.log(1.0 + jnp.exp(-jnp.abs(z)))
        k = jax.nn.sigmoid(-z)
    else:
        lbd = lb_ref[...]
        lf = jnp.log(lbd + (1.0 - lbd) * jax.nn.sigmoid(z))
        k = (1.0 - lbd) * jax.nn.sigmoid(-z)
    v = v_ref[...]
    cum = _sel_dot(tri_ref[0], lf)
    lvl = lvl_ref[0]
    last_row = 0 if rev else HG_C - 1

    for p in range(2):
        sl = slice(p * LANES, (p + 1) * LANES)
        q_p, k_p, v_p, cum_p = q[:, sl], k[:, sl], v[:, sl], cum[:, sl]
        lane = _lane((HG_C, LANES))
        head_masks = (lane < A_DK, lane >= A_DK)
        v_bf = v_p.astype(BF16)
        k_bf = k_p.astype(BF16)
        scores = []
        for hm in head_masks:
            qm = jnp.where(hm, q_p, 0.0).astype(BF16)
            scores.append(jnp.where(lvl == 0, _dot_nt(qm, k_bf), 0.0))
        for li, m in enumerate(HG_LEVELS):
            ref = _block_ref(cum_p, m, m // 2 if rev else m // 2 - 1)
            qd = q_p * jnp.exp(jnp.minimum(cum_p - ref, 0.0))
            kd = (k_p * jnp.exp(jnp.minimum(ref - cum_p, 0.0))).astype(BF16)
            for hi, hm in enumerate(head_masks):
                qm = jnp.where(hm, qd, 0.0).astype(BF16)
                scores[hi] = jnp.where(lvl == li + 1, _dot_nt(qm, kd), scores[hi])
        st = st_ref[p]
        o_intra = jnp.where(head_masks[0],
                            jnp.dot(scores[0].astype(BF16), v_bf, preferred_element_type=F32),
                            jnp.dot(scores[1].astype(BF16), v_bf, preferred_element_type=F32))
        q0 = (q_p * jnp.exp(cum_p)).astype(BF16)
        o_ref[:, sl] = o_intra + _dot_nt(q0, st.astype(BF16))
        last = cum_p[last_row:last_row + 1, :]
        ks = (k_p * jnp.exp(last - cum_p)).astype(BF16)
        upd = _dot_tn(v_bf, ks)
        r = lax.broadcasted_iota(jnp.int32, (LANES, LANES), 0)
        cl = lax.broadcasted_iota(jnp.int32, (LANES, LANES), 1)
        same_head = (r < A_DK) == (cl < A_DK)
        st_new = st * jnp.exp(last) + jnp.where(same_head, upd, 0.0)
        st_ref[p] = st_new
        fin_ref[0, p] = st_new


def _hgrn_seq(g):
    return jnp.where(g < CTX_TILES, g, CTX_TILES + (g - CTX_TILES) // SMP_TILES_PER_SEQ)


def _hgrn_blk(g, rev):
    if not rev:
        return g
    j = g - CTX_TILES
    return jnp.where(g < CTX_TILES, g,
                     CTX_TILES + (j // SMP_TILES_PER_SEQ) * SMP_TILES_PER_SEQ
                     + (SMP_TILES_PER_SEQ - 1 - j % SMP_TILES_PER_SEQ))


def _hgrn_scan(proj, lb_dir, s0_dir, lvl, tri, layer, rev):
    d = 1 if rev else 0
    blk = functools.partial(_hgrn_blk, rev=rev)
    return pl.pallas_call(
        functools.partial(_hgrn_kernel, layer=layer, rev=rev),
        grid=(N_TILES,),
        in_specs=[
            pl.BlockSpec((HG_C, A_WIDTH), lambda g: (blk(g), 0)),
            pl.BlockSpec((HG_C, A_WIDTH), lambda g: (blk(g), 1 + d)),
            pl.BlockSpec((HG_C, A_WIDTH), lambda g: (blk(g), 3)),
            pl.BlockSpec((1, A_WIDTH), lambda g: (0, 0)),
            pl.BlockSpec((1, 2, LANES, LANES), lambda g: (_hgrn_seq(g), 0, 0, 0)),
            pl.BlockSpec((1, HG_C, HG_C), lambda g: (d, 0, 0)),
            pl.BlockSpec((1, HG_C, HG_C), lambda g: (d, 0, 0)),
        ],
        out_specs=[
            pl.BlockSpec((HG_C, A_WIDTH), lambda g: (blk(g), 0)),
            pl.BlockSpec((1, 2, LANES, LANES), lambda g: (_hgrn_seq(g), 0, 0, 0)),
        ],
        out_shape=[
            jax.ShapeDtypeStruct((T, A_WIDTH), F32),
            jax.ShapeDtypeStruct((N_SEQ, 2, LANES, LANES), F32),
        ],
        scratch_shapes=[pltpu.VMEM((2, LANES, LANES), F32)],
        compiler_params=_cparams(("arbitrary",)),
        name=f"hgrn_scan_{'bwd' if rev else 'fwd'}",
    )(proj, proj, proj, lb_dir, s0_dir, lvl, tri)


def _pack_state(s):
    n = s.shape[0]
    st = jnp.swapaxes(s, -1, -2).reshape(n, 2, 2, A_DK, A_DK)
    z = jnp.zeros_like(st[:, :, 0])
    top = jnp.concatenate([st[:, :, 0], z], axis=-1)
    bot = jnp.concatenate([z, st[:, :, 1]], axis=-1)
    return jnp.concatenate([top, bot], axis=-2)


def _unpack_state(sp):
    n = sp.shape[0]
    h0 = sp[:, :, :A_DK, :A_DK]
    h1 = sp[:, :, A_DK:, A_DK:]
    st = jnp.stack([h0, h1], axis=2).reshape(n, A_HEADS, A_DK, A_DK)
    return jnp.swapaxes(st, -1, -2)


def _half_rms(x, g):
    lane = _lane(x.shape)
    lo = lane < B_DK
    xx = x * x
    ms0 = jnp.sum(jnp.where(lo, xx, 0.0), axis=-1, keepdims=True) * (1.0 / B_DK)
    ms1 = jnp.sum(jnp.where(lo, 0.0, xx), axis=-1, keepdims=True) * (1.0 / B_DK)
    inv = jnp.where(lo, lax.rsqrt(ms0 + EPS), lax.rsqrt(ms1 + EPS))
    return x * inv * g


def _rope(x, cos, sin_signed):
    lane = _lane(x.shape)
    first = (lane % 32) < 16
    rot = jnp.where(first, pltpu.roll(x, LANES - 16, 1), pltpu.roll(x, 16, 1))
    return x * cos + rot * sin_signed


def _diff_softmax_pv(q_bf, keys_bf, vals_bf, lam):
    lane = _lane(q_bf.shape)
    zero = jnp.zeros_like(q_bf)
    acc = None
    parts = []
    for mp in range(2):
        qm = jnp.where((lane < B_DK) == (mp == 0), q_bf, zero)
        s = [_dot_nt(qm, kk) for kk in keys_bf]
        mx = functools.reduce(jnp.maximum, [jnp.max(si, axis=-1, keepdims=True) for si in s])
        e = [jnp.exp(si - mx) for si in s]
        den = functools.reduce(lambda a, b: a + b, [jnp.sum(ei, axis=-1, keepdims=True) for ei in e])
        parts.append((e, 1.0 / den))
    (e0, r0), (e1, r1) = parts
    r1 = r1 * lam
    for i in range(len(keys_bf)):
        a = (e0[i] * r0 - e1[i] * r1).astype(BF16)
        pv = jnp.dot(a, vals_bf[i], preferred_element_type=F32)
        acc = pv if acc is None else acc + pv
    return acc


def _subln(o, g, lam_init):
    return o * lax.rsqrt(jnp.mean(o * o, axis=-1, keepdims=True) + EPS) * g * (1.0 - lam_init)


def _attn_ctx_kernel(lam_ref, q_ref, k_ref, v_ref, gq_ref, gk_ref, gs_ref, *rest, lam_init, layer):
    if layer:
        pk_ref, pv_ref, o_ref, nk_ref, nv_ref = rest
        nk_ref[0, :layer] = pk_ref[0]
        nv_ref[0, :layer] = pv_ref[0]
    else:
        o_ref, nk_ref, nv_ref = rest
    lam = lam_ref[0, 0]
    qn = _half_rms(q_ref[...], gq_ref[...]) * (B_DK ** -0.5)
    kn = _half_rms(k_ref[...], gk_ref[...])
    v = v_ref[...]
    nk_ref[0, layer, 0, 0] = kn[:, :B_DK]
    nk_ref[0, layer, 1, 0] = kn[:, B_DK:]
    nv_ref[0, layer, 0] = v
    o = _diff_softmax_pv(qn.astype(BF16), [kn.astype(BF16)], [v.astype(BF16)], lam)
    o_ref[...] = _subln(o, gs_ref[...], lam_init)


def _attn_ctx(proj, lam, gq2, gk2, gs, lam_init, layer, prev_k, prev_v):
    qcol, kcol, vcol = 5 * A_WIDTH // LANES, 5 * A_WIDTH // LANES + 4, 5 * A_WIDTH // LANES + 8
    prev_specs, prev_args = [], []
    if layer:
        prev_specs = [pl.BlockSpec((1, layer, 2, 1, SEQ, B_DK), lambda b, h: (b, 0, 0, h, 0, 0)),
                      pl.BlockSpec((1, layer, 1, SEQ, B_DV), lambda b, h: (b, 0, h, 0, 0))]
        prev_args = [prev_k, prev_v]
    n_l = layer + 1
    return pl.pallas_call(
        functools.partial(_attn_ctx_kernel, lam_init=lam_init, layer=layer),
        grid=(BATCH, B_HEADS),
        in_specs=[
            pl.BlockSpec(memory_space=pltpu.SMEM),
            pl.BlockSpec((SEQ, LANES), lambda b, h: (b, qcol + h)),
            pl.BlockSpec((SEQ, LANES), lambda b, h: (b, kcol + h)),
            pl.BlockSpec((SEQ, LANES), lambda b, h: (b, vcol + h)),
            pl.BlockSpec((1, LANES), lambda b, h: (0, 0)),
            pl.BlockSpec((1, LANES), lambda b, h: (0, 0)),
            pl.BlockSpec((1, LANES), lambda b, h: (0, 0)),
        ] + prev_specs,
        out_specs=[
            pl.BlockSpec((SEQ, LANES), lambda b, h: (b, h)),
            pl.BlockSpec((1, n_l, 2, 1, SEQ, B_DK), lambda b, h: (b, 0, 0, h, 0, 0)),
            pl.BlockSpec((1, n_l, 1, SEQ, B_DV), lambda b, h: (b, 0, h, 0, 0)),
        ],
        out_shape=[
            jax.ShapeDtypeStruct((T_CTX, B_WIDTH), F32),
            jax.ShapeDtypeStruct((BATCH, n_l, 2, B_HEADS, SEQ, B_DK), F32),
            jax.ShapeDtypeStruct((BATCH, n_l, B_HEADS, SEQ, B_DV), F32),
        ],
        compiler_params=_cparams(("arbitrary", "arbitrary")),
        name="diff_attention_ctx",
    )(lam, proj, proj, proj, gq2, gk2, gs, *prev_args)


ATT_TQ = 256


def _attn_smp_kernel(lam_ref, q_ref, k_ref, v_ref, ck_ref, cv_ref, cos_ref, sin_ref, gq_ref, gk_ref, gs_ref,
                     o_ref, qs_ref, ks_ref, *, lam_init):
    lam = lam_ref[0, 0]
    cos = cos_ref[...]
    sin = sin_ref[...]
    qn = _rope(_half_rms(q_ref[...], gq_ref[...]), cos, sin) * (B_DK ** -0.5)
    qs_ref[...] = qn.astype(BF16)
    ks_ref[...] = _rope(_half_rms(k_ref[...], gk_ref[...]), cos, sin).astype(BF16)
    ck = jnp.concatenate([ck_ref[0, 0, 0, 0], ck_ref[0, 0, 1, 0]], axis=-1).astype(BF16)
    cv = cv_ref[0, 0, 0].astype(BF16)
    v_bf = v_ref[...].astype(BF16)
    k_bf = ks_ref[...]
    g = gs_ref[...]

    def body(i, carry):
        r0 = pl.multiple_of(i * ATT_TQ, ATT_TQ)
        q_bf = qs_ref[pl.ds(r0, ATT_TQ), :]
        o = _diff_softmax_pv(q_bf, [k_bf, ck], [v_bf, cv], lam)
        o_ref[pl.ds(r0, ATT_TQ), :] = _subln(o, g, lam_init)
        return carry

    lax.fori_loop(0, DEC_SEQ // ATT_TQ, body, 0)


def _attn_smp(proj, lam, cache_k, cache_v, cos, sin, gq2, gk2, gs, layer, lam_init):
    qcol, kcol, vcol = 5 * A_WIDTH // LANES, 5 * A_WIDTH // LANES + 4, 5 * A_WIDTH // LANES + 8
    r0 = T_CTX // DEC_SEQ
    return pl.pallas_call(
        functools.partial(_attn_smp_kernel, lam_init=lam_init),
        grid=(DEC_BATCH, B_HEADS),
        in_specs=[
            pl.BlockSpec(memory_space=pltpu.SMEM),
            pl.BlockSpec((DEC_SEQ, LANES), lambda b, h: (r0 + b, qcol + h)),
            pl.BlockSpec((DEC_SEQ, LANES), lambda b, h: (r0 + b, kcol + h)),
            pl.BlockSpec((DEC_SEQ, LANES), lambda b, h: (r0 + b, vcol + h)),
            pl.BlockSpec((1, 1, 2, 1, PAST, B_DK), lambda b, h: (b, layer, 0, h, 0, 0)),
            pl.BlockSpec((1, 1, 1, PAST, B_DV), lambda b, h: (b, layer, h, 0, 0)),
            pl.BlockSpec((DEC_SEQ, LANES), lambda b, h: (0, 0)),
            pl.BlockSpec((DEC_SEQ, LANES), lambda b, h: (0, 0)),
            pl.BlockSpec((1, LANES), lambda b, h: (0, 0)),
            pl.BlockSpec((1, LANES), lambda b, h: (0, 0)),
            pl.BlockSpec((1, LANES), lambda b, h: (0, 0)),
        ],
        out_specs=pl.BlockSpec((DEC_SEQ, LANES), lambda b, h: (b, h)),
        out_shape=jax.ShapeDtypeStruct((T_SMP, B_WIDTH), F32),
        scratch_shapes=[pltpu.VMEM((DEC_SEQ, LANES), BF16), pltpu.VMEM((DEC_SEQ, LANES), BF16)],
        compiler_params=_cparams(("arbitrary", "arbitrary")),
        name="diff_attention_smp",
    )(lam, proj, proj, proj, cache_k, cache_v, cos, sin, gq2, gk2, gs)


def _rope_tables():
    n_rows = DEC_SEQ // GRID_W
    row = np.repeat(np.arange(n_rows), GRID_W).astype(np.float32)
    col = np.tile(np.arange(GRID_W), n_rows).astype(np.float32)
    half = B_DK // 2
    inv_freq = (ROPE_BASE ** (-jnp.arange(0, half, 2, dtype=F32) / half))
    row_ang = jnp.asarray(row)[:, None] * inv_freq
    col_ang = jnp.asarray(col)[:, None] * inv_freq
    ang = jnp.concatenate([row_ang, row_ang, col_ang, col_ang], axis=-1)
    ang = jnp.concatenate([ang, ang], axis=-1)
    sign = np.where((np.arange(LANES) % 32) < 16, -1.0, 1.0).astype(np.float32)
    return jnp.cos(ang), jnp.sin(ang) * sign


CM_ROWS = 512


def _gelu(x):
    return 0.5 * x * (1.0 + lax.erf(x * (2.0 ** -0.5)))


def _cmlp_kernel(u_ref, v_ref, g_ref, b_ref, ws_ref, bs_ref, o_ref):
    u = _gelu(u_ref[...])
    gv = _gelu(v_ref[...])
    mu = jnp.mean(gv, axis=-1, keepdims=True)
    dv = gv - mu
    var = jnp.mean(dv * dv, axis=-1, keepdims=True)
    vn = (dv * lax.rsqrt(var + EPS) * g_ref[...] + b_ref[...]).astype(BF16)
    lane = _lane((C_CHUNK, LANES))
    for c in range(CM_ROWS // C_CHUNK):
        rs = slice(c * C_CHUNK, (c + 1) * C_CHUNK)
        for p in range(2):
            cs = slice(p * LANES, (p + 1) * LANES)
            vp = vn[rs, cs]
            m0 = jnp.dot(ws_ref[2 * p].astype(BF16), vp, preferred_element_type=F32)
            m1 = jnp.dot(ws_ref[2 * p + 1].astype(BF16), vp, preferred_element_type=F32)
            mixed = jnp.where(lane < C_DG, m0, m1) + bs_ref[:, cs]
            o_ref[rs, cs] = u[rs, cs] * mixed


def _chunk_mlp(proj, ln_g, ln_b, w_s, bias_full):
    ucol = (5 * A_WIDTH + 3 * B_WIDTH) // C_WIDTH
    return pl.pallas_call(
        _cmlp_kernel,
        grid=(T // CM_ROWS,),
        in_specs=[
            pl.BlockSpec((CM_ROWS, C_WIDTH), lambda i: (i, ucol)),
            pl.BlockSpec((CM_ROWS, C_WIDTH), lambda i: (i, ucol + 1)),
            pl.BlockSpec((1, C_WIDTH), lambda i: (0, 0)),
            pl.BlockSpec((1, C_WIDTH), lambda i: (0, 0)),
            pl.BlockSpec((C_GROUPS, C_CHUNK, C_CHUNK), lambda i: (0, 0, 0)),
            pl.BlockSpec((C_CHUNK, C_WIDTH), lambda i: (0, 0)),
        ],
        out_specs=pl.BlockSpec((CM_ROWS, C_WIDTH), lambda i: (i, 0)),
        out_shape=jax.ShapeDtypeStruct((T, C_WIDTH), F32),
        compiler_params=_cparams(("arbitrary",)),
        name="chunk_mlp",
    )(proj, proj, ln_g.reshape(1, C_WIDTH), ln_b.reshape(1, C_WIDTH), w_s, bias_full)


def _postmix_kernel(*refs, n_x):
    (of_ref, ob_ref, ag_ref, hg_ref, hsel_ref, bc_ref, bs_ref, c_ref, w_ref, gate1_ref, shift2_ref, scale2_ref,
     g2_ref, wrh_ref, wrl_ref, br_ref, x1_ref, h2_ref, idx_ref, gw_ref) = refs[n_x:]
    x = _stream_tile(refs[:n_x])
    o = of_ref[...] + ob_ref[...]
    ms = _dot_sel(o * o, hsel_ref[...]) * (1.0 / A_DK)
    ag = ag_ref[...]
    a = o * lax.rsqrt(ms + EPS) * hg_ref[...] * (ag * jax.nn.sigmoid(ag))
    b = jnp.where(pl.program_id(0) < CTX_TILES, bc_ref[...], bs_ref[...])
    mixed = jnp.dot(a.astype(BF16), w_ref[0:A_WIDTH, :], preferred_element_type=F32)
    mixed = mixed + jnp.dot(b.astype(BF16), w_ref[A_WIDTH:A_WIDTH + B_WIDTH, :], preferred_element_type=F32)
    mixed = mixed + jnp.dot(c_ref[...].astype(BF16), w_ref[A_WIDTH + B_WIDTH:, :], preferred_element_type=F32)
    x1 = x + gate1_ref[0] * mixed
    x1_ref[...] = x1
    y = x1 * lax.rsqrt(jnp.mean(x1 * x1, axis=-1, keepdims=True) + EPS) * g2_ref[...]
    h2 = y * (1.0 + scale2_ref[0]) + shift2_ref[0]
    h2_ref[...] = pltpu.einshape("t(jl)->tjl", h2, l=LANES)
    hi = h2.astype(BF16)
    lo = (h2 - hi.astype(F32)).astype(BF16)
    lg = jnp.dot(lo, wrh_ref[...], preferred_element_type=F32)
    lg = lg + jnp.dot(hi, wrl_ref[...], preferred_element_type=F32)
    lg = lg + jnp.dot(hi, wrh_ref[...], preferred_element_type=F32) + br_ref[...]
    lt = lg.T[:N_EXPERTS]
    row = lax.broadcasted_iota(jnp.int32, lt.shape, 0)
    out_row = lax.broadcasted_iota(jnp.int32, (8, TM), 0)
    idx_out = jnp.zeros((8, TM), jnp.int32)
    val_out = jnp.zeros((8, TM), F32)
    top0 = None
    den = None
    for kk in range(TOP_K):
        mx = jnp.max(lt, axis=0, keepdims=True)
        am = jnp.min(jnp.where(lt == mx, row, N_EXPERTS), axis=0, keepdims=True)
        if kk == 0:
            top0 = mx
        e = jnp.exp(mx - top0)
        den = e if den is None else den + e
        idx_out = jnp.where(out_row == kk, am, idx_out)
        val_out = jnp.where(out_row == kk, e, val_out)
        lt = jnp.where(row == am, -jnp.inf, lt)
    idx_ref[...] = idx_out
    gw_ref[...] = val_out / den


def _post_mix(o_f, o_b, proj, hg, hsel, b_ctx, b_smp, c_out, w_out_bf, x, mod3, g2, wr_hi, wr_lo, br_pad):
    tile = lambda w: pl.BlockSpec((TM, w), lambda i: (i, 0))
    const = lambda shape: pl.BlockSpec(shape, lambda i: tuple(0 for _ in shape))
    modspec = lambda j: pl.BlockSpec((1, 1, D), lambda i: (_mod_row(i), 0, j))
    rowsT = pl.BlockSpec((8, TM), lambda i: (0, i))
    x_specs, x_args = _stream_specs(x)
    return pl.pallas_call(
        functools.partial(_postmix_kernel, n_x=len(x_args)),
        grid=(N_TILES,),
        in_specs=x_specs + [
            tile(A_WIDTH), tile(A_WIDTH),
            pl.BlockSpec((TM, A_WIDTH), lambda i: (i, 4)),
            const((1, A_WIDTH)), const((A_WIDTH, A_WIDTH)),
            pl.BlockSpec((TM, B_WIDTH), lambda i: (jnp.minimum(i, CTX_TILES - 1), 0)),
            pl.BlockSpec((TM, B_WIDTH), lambda i: (jnp.maximum(i - CTX_TILES, 0), 0)),
            tile(C_WIDTH),
            const((D, D)),
            modspec(2), modspec(3), modspec(4),
            const((1, D)), const((D, LANES)), const((D, LANES)), const((1, LANES)),
        ],
        out_specs=[tile(D), pl.BlockSpec((TM, ROW_TILES, LANES), lambda i: (i, 0, 0)), rowsT, rowsT],
        out_shape=[
            jax.ShapeDtypeStruct((T, D), F32),
            jax.ShapeDtypeStruct((T, ROW_TILES, LANES), F32),
            jax.ShapeDtypeStruct((8, T), jnp.int32),
            jax.ShapeDtypeStruct((8, T), F32),
        ],
        compiler_params=_cparams(("arbitrary",)),
        name="post_mix_router",
    )(*x_args, o_f, o_b, proj, hg, hsel, b_ctx, b_smp, c_out, w_out_bf, mod3, mod3, mod3, g2.reshape(1, D),
      wr_hi, wr_lo, br_pad)


def _route_kernel(idx_ref, dest_ref, meta_ref):
    erow = lax.broadcasted_iota(jnp.int32, (N_EXPERTS, TM), 0)
    s_i = lax.broadcasted_iota(jnp.int32, (TM, TM), 0)
    t_i = lax.broadcasted_iota(jnp.int32, (TM, TM), 1)
    earlier = (s_i < t_i).astype(BF16)
    out_row = lax.broadcasted_iota(jnp.int32, (8, TM), 0)

    def onehots(i):
        idx = idx_ref[:, pl.ds(pl.multiple_of(i * TM, TM), TM)]
        return [(erow == idx[kk:kk + 1, :]) for kk in range(TOP_K)]

    def count_tile(i, run):
        ohs = onehots(i)
        base = run
        pos = jnp.zeros((8, TM), F32)
        for kk in range(TOP_K):
            ohf = ohs[kk].astype(F32)
            before = jnp.dot(ohs[kk].astype(BF16), earlier, preferred_element_type=F32)
            p = jnp.sum(ohf * (base + before), axis=0, keepdims=True)
            pos = jnp.where(out_row == kk, p, pos)
            base = base + jnp.sum(ohf, axis=1, keepdims=True)
        dest_ref[:, pl.ds(pl.multiple_of(i * TM, TM), TM)] = pos.astype(jnp.int32)
        return base

    counts = lax.fori_loop(0, N_TILES, count_tile, jnp.zeros((N_EXPERTS, 1), F32)).astype(jnp.int32)
    bm_shift = MOE_BM.bit_length() - 1
    padded = lax.shift_left(lax.shift_right_logical(counts + (MOE_BM - 1), bm_shift), bm_shift)
    e_r = lax.broadcasted_iota(jnp.int32, (N_EXPERTS, N_EXPERTS), 0)
    e_c = lax.broadcasted_iota(jnp.int32, (N_EXPERTS, N_EXPERTS), 1)
    incl = (e_c <= e_r).astype(BF16)
    pad_end = _sel_dot(incl, jnp.broadcast_to(padded.astype(F32), (N_EXPERTS, LANES)))[:, :1]
    pad_start = pad_end - padded.astype(F32)

    def place_tile(i, carry):
        ohs = onehots(i)
        sl = pl.ds(pl.multiple_of(i * TM, TM), TM)
        off = jnp.zeros((8, TM), F32)
        for kk in range(TOP_K):
            o = jnp.sum(ohs[kk].astype(F32) * pad_start, axis=0, keepdims=True)
            off = jnp.where(out_row == kk, o, off)
        dest_ref[:, sl] = dest_ref[:, sl] + off.astype(jnp.int32)
        return carry

    lax.fori_loop(0, N_TILES, place_tile, 0)

    total = jnp.max(pad_end, axis=0, keepdims=True)
    lane_i = lax.broadcasted_iota(jnp.int32, (1, TM), 1)
    blk0 = (lane_i * MOE_BM).astype(F32)
    block_e = jnp.sum((pad_end <= blk0).astype(F32), axis=0, keepdims=True)
    live_end = pad_start + counts.astype(F32)
    sel = erow.astype(F32) == block_e
    live = jnp.sum(jnp.where(sel, live_end, 0.0), axis=0, keepdims=True)
    valid = jnp.where(blk0 < total, jnp.clip(live - blk0, 0.0, float(MOE_BM)), 0.0)
    own = erow == lane_i
    n_blk = jnp.sum(jnp.where(own, padded.astype(F32), 0.0), axis=0, keepdims=True) * (1.0 / MOE_BM)
    first_blk = jnp.sum(jnp.where(own, pad_start, 0.0), axis=0, keepdims=True) * (1.0 / MOE_BM)
    meta = jnp.where(out_row == 0, valid, 0.0)
    meta = jnp.where(out_row == 1, n_blk, meta)
    meta = jnp.where(out_row == 2, first_blk, meta)
    meta = jnp.where(out_row == 3, total * (1.0 / MOE_BM), meta)
    meta_ref[...] = meta.astype(jnp.int32)


def _route(idx_t):
    assert MOE_BLOCKS <= TM
    return pl.pallas_call(
        _route_kernel,
        out_shape=[jax.ShapeDtypeStruct((8, T), jnp.int32), jax.ShapeDtypeStruct((8, TM), jnp.int32)],
        compiler_params=pltpu.CompilerParams(vmem_limit_bytes=VMEM_LIMIT),
        name="moe_route",
    )(idx_t)


def _moe_kernel(bv_ref, nb_ref, g0_ref, tot_ref, x_hbm, wgu_ref, bgu_ref, wdn_ref, bdn_ref, y_hbm,
                wgu_bf, wdn_bf, xbuf, ybuf, xb_ref, xsem, ysem):
    e = pl.program_id(0)
    n_blk = nb_ref[e]
    first = g0_ref[e]
    total = tot_ref[0]
    ahead = MOE_RING - 1

    def x_copy(g):
        slot = g % MOE_RING
        return pltpu.make_async_copy(x_hbm.at[pl.ds(g * MOE_BM, MOE_BM)], xbuf.at[slot], xsem.at[slot])

    def y_copy(g):
        slot = g % MOE_RING
        return pltpu.make_async_copy(ybuf.at[slot], y_hbm.at[pl.ds(g * MOE_BM, MOE_BM)], ysem.at[slot])

    @pl.when(e == 0)
    def _():
        for g in range(ahead):
            @pl.when(g < total)
            def _():
                x_copy(g).start()

    @pl.when(n_blk > 0)
    def _():
        wgu_bf[...] = wgu_ref[0, 0].astype(BF16)
        wdn_bf[...] = wdn_ref[0, 0].astype(BF16)

    def block(j, carry):
        g = first + j
        slot = g % MOE_RING
        x_copy(g).wait()

        @pl.when(g + ahead < total)
        def _():
            x_copy(g + ahead).start()

        @pl.when(g >= MOE_RING)
        def _():
            y_copy(g - MOE_RING).wait()

        live = lax.broadcasted_iota(jnp.int32, (MOE_BM, LANES), 0) < bv_ref[g]
        xt = pltpu.einshape("tjl->jtl", xbuf[slot])
        for jj in range(ROW_TILES):
            xb_ref[:, jj * LANES:(jj + 1) * LANES] = jnp.where(live, xt[jj], 0.0).astype(BF16)
        gu = jnp.dot(xb_ref[...], wgu_bf[...], preferred_element_type=F32) + bgu_ref[0, 0]
        glu = jnp.minimum(gu[:, :D], SWIGLU_LIMIT)
        lin = jnp.clip(gu[:, D:], -SWIGLU_LIMIT, SWIGLU_LIMIT)
        act = glu * jax.nn.sigmoid(SWIGLU_ALPHA * glu) * (lin + 1.0)
        y = jnp.dot(act.astype(BF16), wdn_bf[...], preferred_element_type=F32) + bdn_ref[0, 0]
        ybuf[slot] = pltpu.einshape("t(jl)->tjl", y, l=LANES)
        y_copy(g).start()
        return carry

    lax.fori_loop(0, n_blk, block, 0)

    @pl.when(e == N_EXPERTS - 1)
    def _():
        for back in range(MOE_RING, 0, -1):
            @pl.when(total >= back)
            def _():
                y_copy(total - back).wait()

        def fill(g, carry):
            ybuf[g % MOE_RING] = jnp.zeros((MOE_BM, ROW_TILES, LANES), F32)
            y_copy(g).start()
            y_copy(g).wait()
            return carry

        lax.fori_loop(total, MOE_BLOCKS, fill, 0)


def _moe_ffn(block_valid, n_blk, first_blk, total_blk, xs, w_gu, b_gu, w_dn, b_dn, layer):
    rows = (MOE_BM, ROW_TILES, LANES)
    return pl.pallas_call(
        _moe_kernel,
        grid_spec=pltpu.PrefetchScalarGridSpec(
            num_scalar_prefetch=4,
            grid=(N_EXPERTS,),
            in_specs=[
                pl.BlockSpec(memory_space=pl.ANY),
                pl.BlockSpec((1, 1, D, 2 * D), lambda e, *_: (layer, e, 0, 0)),
                pl.BlockSpec((1, 1, 1, 2 * D), lambda e, *_: (layer, e, 0, 0)),
                pl.BlockSpec((1, 1, D, D), lambda e, *_: (layer, e, 0, 0)),
                pl.BlockSpec((1, 1, 1, D), lambda e, *_: (layer, e, 0, 0)),
            ],
            out_specs=pl.BlockSpec(memory_space=pl.ANY),
            scratch_shapes=[
                pltpu.VMEM((D, 2 * D), BF16), pltpu.VMEM((D, D), BF16),
                pltpu.VMEM((MOE_RING,) + rows, F32), pltpu.VMEM((MOE_RING,) + rows, F32),
                pltpu.VMEM((MOE_BM, D), BF16),
                pltpu.SemaphoreType.DMA((MOE_RING,)), pltpu.SemaphoreType.DMA((MOE_RING,)),
            ],
        ),
        out_shape=jax.ShapeDtypeStruct((MOE_ROWS, ROW_TILES, LANES), F32),
        compiler_params=_cparams(("arbitrary",)),
        name="moe_expert_ffn",
    )(block_valid, n_blk, first_blk, total_blk, xs, w_gu, b_gu.reshape(DEPTH, N_EXPERTS, 1, 2 * D), w_dn,
      b_dn.reshape(DEPTH, N_EXPERTS, 1, D))


def _combine_kernel(x1_ref, y_ref, gw_ref, gate2_ref, o_ref):
    gw = jnp.concatenate([gw_ref[...], jnp.zeros((LANES - 8, TM), F32)], axis=0).T
    ys = [pltpu.einshape("tjl->jtl", y_ref[kk]) for kk in range(TOP_K)]
    for j in range(ROW_TILES):
        cs = slice(j * LANES, (j + 1) * LANES)
        acc = None
        for kk in range(TOP_K):
            term = ys[kk][j] * gw[:, kk:kk + 1]
            acc = term if acc is None else acc + term
        o_ref[:, cs] = x1_ref[:, cs] + gate2_ref[0, :, cs] * acc


def _combine(x1, yg, gw, mod3):
    return pl.pallas_call(
        _combine_kernel,
        grid=(N_TILES,),
        in_specs=[
            pl.BlockSpec((TM, D), lambda i: (i, 0)),
            pl.BlockSpec((TOP_K, TM, ROW_TILES, LANES), lambda i: (0, i, 0, 0)),
            pl.BlockSpec((8, TM), lambda i: (0, i)),
            pl.BlockSpec((1, 1, D), lambda i: (_mod_row(i), 0, 5)),
        ],
        out_specs=pl.BlockSpec((TM, D), lambda i: (i, 0)),
        out_shape=jax.ShapeDtypeStruct((T, D), F32),
        compiler_params=_cparams(("arbitrary",)),
        name="moe_combine",
    )(x1, yg, gw, mod3)


def _sc_mesh():
    return plsc.VectorSubcoreMesh(core_axis_name="c", subcore_axis_name="s")


def _sc_worker():
    return lax.axis_index("s") * SC_CORES + lax.axis_index("c")


def _sc_dispatch(h2t, dest_km):
    per_w = T // SC_WORKERS

    @functools.partial(
        pl.kernel, mesh=_sc_mesh(),
        out_type=jax.ShapeDtypeStruct((MOE_ROWS, ROW_TILES, LANES), F32),
        scratch_types=[pltpu.VMEM((SC_WIN,), jnp.int32), pltpu.VMEM((SC_WIN, ROW_TILES, LANES), F32),
                       pltpu.SemaphoreType.DMA],
    )
    def run(h_hbm, d_hbm, o_hbm, idx_v, rows_v, sem):
        w0 = _sc_worker() * per_w

        @pl.loop(0, per_w // SC_WIN)
        def _(w):
            base = pl.multiple_of(w0 + w * SC_WIN, SC_WIN)
            pltpu.sync_copy(h_hbm.at[pl.ds(base, SC_WIN)], rows_v)
            for kk in range(TOP_K):
                pltpu.sync_copy(d_hbm.at[pl.ds(kk * T + base, SC_WIN)], idx_v)
                pltpu.async_copy(rows_v, o_hbm.at[idx_v], sem).wait()

    return run(h2t, dest_km)


def _sc_gather(yb, dest_km):
    n = TOP_K * T
    per_w = n // SC_WORKERS

    @functools.partial(
        pl.kernel, mesh=_sc_mesh(),
        out_type=jax.ShapeDtypeStruct((n, ROW_TILES, LANES), F32),
        scratch_types=[pltpu.VMEM((SC_WIN,), jnp.int32), pltpu.VMEM((SC_WIN, ROW_TILES, LANES), F32),
                       pltpu.SemaphoreType.DMA],
    )
    def run(y_hbm, d_hbm, o_hbm, idx_v, rows_v, sem):
        w0 = _sc_worker() * per_w

        @pl.loop(0, per_w // SC_WIN)
        def _(w):
            base = pl.multiple_of(w0 + w * SC_WIN, SC_WIN)
            pltpu.sync_copy(d_hbm.at[pl.ds(base, SC_WIN)], idx_v)
            pltpu.async_copy(y_hbm.at[idx_v], rows_v, sem).wait()
            pltpu.sync_copy(rows_v, o_hbm.at[pl.ds(base, SC_WIN)])

    return run(yb, dest_km)


def kernel(x_prompt, x_sample, c, cache_diff_k, cache_diff_v, state_hgrn, c_ctx, norm_mix_g, norm_ffn_g, w_mod, b_mod, w_in, w_out, hgrn_lower_bounds, hgrn_norm_g, diff_q_norm_g, diff_k_norm_g, diff_lambda_q1, diff_lambda_k1, diff_lambda_q2, diff_lambda_k2, diff_subln_g, cmlp_ln_g, cmlp_ln_b, cmlp_w_s, cmlp_b_s, router_w, router_b, moe_w_gate_up, moe_b_gate_up, moe_w_down, moe_b_down):
    x = (x_prompt.reshape(T_CTX, D), x_sample.reshape(T_SMP, D))
    cvec = jnp.concatenate([c_ctx[None, :], c, jnp.zeros((MOD_ROWS - 1 - DEC_BATCH, D), F32)], axis=0)
    mod = _modulation(cvec, w_mod, b_mod)

    lvl_np, tri_np = _hgrn_tables()
    lvl = jnp.asarray(lvl_np)
    tri = jnp.asarray(tri_np, dtype=BF16)
    cos, sin = _rope_tables()
    hsel = jnp.asarray(np.kron(np.eye(A_HEADS), np.ones((A_DK, A_DK))), dtype=BF16)
    sm = jax.nn.softmax(hgrn_lower_bounds.astype(F32), axis=0)
    lb_all = jnp.cumsum(sm, axis=0) - sm[0]

    new_k, new_v, new_s = None, None, []
    for l in range(DEPTH):
        mod3 = mod[l].reshape(MOD_ROWS, 1, 6 * D)
        proj = _in_projection(x, norm_mix_g[l], mod3, w_in[l].astype(BF16))

        s0 = jnp.concatenate([jnp.zeros((BATCH, 2, A_HEADS, A_DK, A_DK), F32), state_hgrn[:, l]], axis=0)
        o_dir, fin_dir = [], []
        for d in range(2):
            o_d, fin_d = _hgrn_scan(proj, lb_all[l, d].reshape(1, A_WIDTH), _pack_state(s0[:, d]),
                                    lvl, tri, l, d == 1)
            o_dir.append(o_d)
            fin_dir.append(_unpack_state(fin_d[:BATCH]))
        new_s.append(jnp.stack(fin_dir, axis=1))

        lam_init = 0.8 - 0.6 * math.exp(-0.3 * l)
        lam = (jnp.exp(jnp.sum(diff_lambda_q1[l] * diff_lambda_k1[l]))
               - jnp.exp(jnp.sum(diff_lambda_q2[l] * diff_lambda_k2[l])) + lam_init).reshape(1, 1)
        gq2 = jnp.tile(diff_q_norm_g[l], 2).reshape(1, LANES)
        gk2 = jnp.tile(diff_k_norm_g[l], 2).reshape(1, LANES)
        gs = diff_subln_g[l].reshape(1, LANES)
        b_ctx, new_k, new_v = _attn_ctx(proj, lam, gq2, gk2, gs, lam_init, l, new_k, new_v)
        b_smp = _attn_smp(proj, lam, cache_diff_k, cache_diff_v, cos, sin, gq2, gk2, gs, l, lam_init)

        bias_full = jnp.repeat(cmlp_b_s[l].T, C_DG, axis=1)
        c_out = _chunk_mlp(proj, cmlp_ln_g[l], cmlp_ln_b[l], cmlp_w_s[l], bias_full)

        hg = jnp.tile(hgrn_norm_g[l], A_HEADS).reshape(1, A_WIDTH)
        wr_pad = jnp.pad(router_w[l], ((0, 0), (0, LANES - N_EXPERTS)))
        wr_hi = wr_pad.astype(BF16)
        wr_lo = (wr_pad - wr_hi.astype(F32)).astype(BF16)
        br_pad = jnp.pad(router_b[l], (0, LANES - N_EXPERTS)).reshape(1, LANES)
        x1, h2, idx_t, gw_t = _post_mix(o_dir[0], o_dir[1], proj, hg, hsel, b_ctx, b_smp, c_out,
                                        w_out[l].astype(BF16), x, mod3, norm_ffn_g[l], wr_hi, wr_lo, br_pad)

        dest_t, meta = _route(idx_t)
        dest_km = dest_t[:TOP_K].reshape(-1)
        xs = _sc_dispatch(h2, dest_km)
        yb = _moe_ffn(meta[0, :MOE_BLOCKS], meta[1, :N_EXPERTS], meta[2, :N_EXPERTS], meta[3, :1], xs,
                      moe_w_gate_up, moe_b_gate_up, moe_w_down, moe_b_down, l)
        yg = _sc_gather(yb, dest_km).reshape(TOP_K, T, ROW_TILES, LANES)
        x = _combine(x1, yg, gw_t, mod3)

    y_prompt = x[:T_CTX].reshape(BATCH, SEQ, D)
    y_sample = x[T_CTX:].reshape(DEC_BATCH, DEC_SEQ, D)
    return (y_prompt, y_sample, new_k, new_v, jnp.stack(new_s, axis=1))
```

```python
import functools
import math

import numpy as np
import jax
import jax.numpy as jnp
from jax import lax
from jax.experimental import pallas as pl
from jax.experimental.pallas import tpu as pltpu
from jax.experimental.pallas import tpu_sc as plsc

F32 = jnp.float32
BF16 = jnp.bfloat16

D = 1024
DEPTH = 2
BATCH, SEQ = 16, 256
DEC_BATCH, DEC_SEQ = 8, 1024
PAST = 512
GRID_W = 64
A_HEADS, A_DK = 4, 64
A_WIDTH = 256
B_HEADS, B_DK, B_DV = 4, 64, 128
B_WIDTH = 512
C_GROUPS, C_CHUNK, C_WIDTH, C_DG = 4, 128, 256, 64
IN_WIDTH = 5 * A_WIDTH + 3 * B_WIDTH + 2 * C_WIDTH
N_EXPERTS, TOP_K = 32, 4
SWIGLU_LIMIT, SWIGLU_ALPHA = 7.0, 1.702
ROPE_BASE = 10000.0
EPS = 1e-6

T_CTX = BATCH * SEQ
T_SMP = DEC_BATCH * DEC_SEQ
T = T_CTX + T_SMP
N_SEQ = BATCH + DEC_BATCH
MOD_ROWS = 16

TM = 256
N_TILES = T // TM
CTX_TILES = T_CTX // TM
SMP_TILES_PER_SEQ = DEC_SEQ // TM
LANES = 128
MOE_BM = 256
MOE_ROWS = T * TOP_K + N_EXPERTS * MOE_BM
MOE_BLOCKS = MOE_ROWS // MOE_BM
MOE_RING = 4
ROW_TILES = D // LANES
SC_CORES, SC_SUBCORES = 2, 16
SC_WORKERS = SC_CORES * SC_SUBCORES
SC_WIN = 64
VMEM_LIMIT = 56 * 1024 * 1024


def _cparams(sem):
    return pltpu.CompilerParams(dimension_semantics=sem, vmem_limit_bytes=VMEM_LIMIT)


def _mod_row(i):
    return jnp.where(i < CTX_TILES, 0, 1 + (i - CTX_TILES) // SMP_TILES_PER_SEQ)


def _split3(x):
    hi = x.astype(BF16)
    r = x - hi.astype(F32)
    mid = r.astype(BF16)
    lo = (r - mid.astype(F32)).astype(BF16)
    return hi, mid, lo


def _sel_dot(sel, x):
    hi, mid, lo = _split3(x)
    acc = jnp.dot(sel, lo, preferred_element_type=F32)
    acc = acc + jnp.dot(sel, mid, preferred_element_type=F32)
    return acc + jnp.dot(sel, hi, preferred_element_type=F32)


def _dot_sel(x, sel):
    hi, mid, lo = _split3(x)
    acc = jnp.dot(lo, sel, preferred_element_type=F32)
    acc = acc + jnp.dot(mid, sel, preferred_element_type=F32)
    return acc + jnp.dot(hi, sel, preferred_element_type=F32)


def _dot_nt(a, b):
    return lax.dot_general(a, b, (((1,), (1,)), ((), ())), preferred_element_type=F32)


def _dot_tn(a, b):
    return lax.dot_general(a, b, (((0,), (0,)), ((), ())), preferred_element_type=F32)


def _lane(shape):
    return lax.broadcasted_iota(jnp.int32, shape, len(shape) - 1)


def _mod_kernel(c_ref, w_ref, b_ref, o_ref):
    c = c_ref[...]
    s = c * jax.nn.sigmoid(c)
    o_ref[0] = jnp.dot(s.astype(BF16), w_ref[0].astype(BF16), preferred_element_type=F32) + b_ref[0]


def _modulation(cvec, w_mod, b_mod):
    tn = 1536
    return pl.pallas_call(
        _mod_kernel,
        grid=(DEPTH, 6 * D // tn),
        in_specs=[
            pl.BlockSpec((MOD_ROWS, D), lambda l, j: (0, 0)),
            pl.BlockSpec((1, D, tn), lambda l, j: (l, 0, j)),
            pl.BlockSpec((1, 1, tn), lambda l, j: (l, 0, j)),
        ],
        out_specs=pl.BlockSpec((1, MOD_ROWS, tn), lambda l, j: (l, 0, j)),
        out_shape=jax.ShapeDtypeStruct((DEPTH, MOD_ROWS, 6 * D), F32),
        compiler_params=_cparams(("arbitrary", "arbitrary")),
        name="modulation",
    )(cvec, w_mod, b_mod.reshape(DEPTH, 1, 6 * D))


def _stream_specs(x):
    if isinstance(x, tuple):
        return [pl.BlockSpec((TM, D), lambda i: (jnp.minimum(i, CTX_TILES - 1), 0)),
                pl.BlockSpec((TM, D), lambda i: (jnp.maximum(i - CTX_TILES, 0), 0))], list(x)
    return [pl.BlockSpec((TM, D), lambda i: (i, 0))], [x]


def _stream_tile(x_refs):
    if len(x_refs) == 1:
        return x_refs[0][...]
    return jnp.where(pl.program_id(0) < CTX_TILES, x_refs[0][...], x_refs[1][...])


def _inproj_kernel(*refs, n_x):
    g_ref, shift_ref, scale_ref, w_ref, o_ref = refs[n_x:]
    x = _stream_tile(refs[:n_x])
    y = x * lax.rsqrt(jnp.mean(x * x, axis=-1, keepdims=True) + EPS) * g_ref[...]
    h = y * (1.0 + scale_ref[0]) + shift_ref[0]
    o_ref[...] = jnp.dot(h.astype(BF16), w_ref[...], preferred_element_type=F32)


def _in_projection(x, g, mod3, w_in_bf):
    x_specs, x_args = _stream_specs(x)
    return pl.pallas_call(
        functools.partial(_inproj_kernel, n_x=len(x_args)),
        grid=(N_TILES,),
        in_specs=x_specs + [
            pl.BlockSpec((1, D), lambda i: (0, 0)),
            pl.BlockSpec((1, 1, D), lambda i: (_mod_row(i), 0, 0)),
            pl.BlockSpec((1, 1, D), lambda i: (_mod_row(i), 0, 1)),
            pl.BlockSpec((D, IN_WIDTH), lambda i: (0, 0)),
        ],
        out_specs=pl.BlockSpec((TM, IN_WIDTH), lambda i: (i, 0)),
        out_shape=jax.ShapeDtypeStruct((T, IN_WIDTH), F32),
        compiler_params=_cparams(("arbitrary",)),
        name="in_projection",
    )(*x_args, g.reshape(1, D), mod3, mod3, w_in_bf)


HG_C = 128
HG_LEVELS = tuple(2 ** j for j in range(1, int(math.log2(HG_C)) + 1))


def _hgrn_tables():
    t = np.arange(HG_C)[:, None]
    s = np.arange(HG_C)[None, :]
    x = t ^ s
    lvl = np.zeros((HG_C, HG_C), np.int32)
    nz = x > 0
    lvl[nz] = np.floor(np.log2(x[nz])).astype(np.int32) + 1
    fwd = np.where(t >= s, lvl, -1).astype(np.int32)
    bwd = np.where(t <= s, lvl, -1).astype(np.int32)
    tri_f = (t >= s).astype(np.float32)
    tri_b = (t <= s).astype(np.float32)
    return np.stack([fwd, bwd]), np.stack([tri_f, tri_b])


def _block_ref(cum, m, idx):
    c, l = cum.shape
    if m >= 16:
        c3 = cum.reshape(c // m, m, l)
        r = c3[:, idx:idx + 1, :]
        return jnp.broadcast_to(r, (c // m, m, l)).reshape(c, l)
    c3 = cum.reshape(c // 8, 8, l)
    sub = lax.broadcasted_iota(jnp.int32, c3.shape, 1)
    out = None
    for j in range(8 // m - 1, -1, -1):
        cand = jnp.broadcast_to(c3[:, j * m + idx:j * m + idx + 1, :], c3.shape)
        out = cand if out is None else jnp.where(sub < (j + 1) * m, cand, out)
    return out.reshape(c, l)


def _hgrn_kernel(q_ref, z_ref, v_ref, lb_ref, s0_ref, lvl_ref, tri_ref, o_ref, fin_ref, st_ref, *, layer, rev):
    g = pl.program_id(0)
    first = jnp.logical_or(g < CTX_TILES, (g - CTX_TILES) % SMP_TILES_PER_SEQ == 0)

    @pl.when(first)
    def _():
        st_ref[...] = s0_ref[0]

    qr = q_ref[...]
    q = qr * jax.nn.sigmoid(qr) * (A_DK ** -0.5)
    z = z_ref[...]
    if layer == 0:
        lf = jnp.minimum(z, 0.0) - jnp.log(1.0 + jnp.exp(-jnp.abs(z)))
        k = jax.nn.sigmoid(-z)
    else:
        lbd = lb_ref[...]
        lf = jnp.log(lbd + (1.0 - lbd) * jax.nn.sigmoid(z))
        k = (1.0 - lbd) * jax.nn.sigmoid(-z)
    v = v_ref[...]
    tri = tri_ref[0]
    lvl = lvl_ref[0]
    last_row = 0 if rev else HG_C - 1
    n_chunks = TM // HG_C
    order = range(n_chunks - 1, -1, -1) if rev else range(n_chunks)
    cums = [_sel_dot(tri, lf[c * HG_C:(c + 1) * HG_C]) for c in range(n_chunks)]
    lane = _lane((HG_C, LANES))
    head_masks = (lane < A_DK, lane >= A_DK)
    lane_row = _lane((1, LANES))
    head_keep = ((lane_row < A_DK).astype(BF16), (lane_row >= A_DK).astype(BF16))
    lvl2 = jnp.concatenate([lvl, lvl], axis=0)
    level_masks = [lvl2 == i for i in range(len(HG_LEVELS) + 1)]
    r = lax.broadcasted_iota(jnp.int32, (LANES, LANES), 0)
    cl = lax.broadcasted_iota(jnp.int32, (LANES, LANES), 1)
    same_head = (r < A_DK) == (cl < A_DK)

    def chunk(q_p, k_p, v_p, cum_p, st):
        v_bf = v_p.astype(BF16)
        k_bf = k_p.astype(BF16)
        q_bf = q_p.astype(BF16)

        def both_heads(x_bf):
            return jnp.concatenate([x_bf * head_keep[0], x_bf * head_keep[1]], axis=0)

        scores = jnp.where(level_masks[0], _dot_nt(both_heads(q_bf), k_bf), 0.0)
        for li, m in enumerate(HG_LEVELS):
            ref = _block_ref(cum_p, m, m // 2 if rev else m // 2 - 1)
            dec = jnp.exp(-jnp.abs(cum_p - ref))
            qd = (q_p * dec).astype(BF16)
            kd = (k_p * dec).astype(BF16)
            scores = jnp.where(level_masks[li + 1], _dot_nt(both_heads(qd), kd), scores)
        pv = jnp.dot(scores.astype(BF16), v_bf, preferred_element_type=F32)
        o_intra = jnp.where(head_masks[0], pv[:HG_C], pv[HG_C:])
        q0 = (q_p * jnp.exp(cum_p)).astype(BF16)
        out = o_intra + _dot_nt(q0, st.astype(BF16))
        last = cum_p[last_row:last_row + 1, :]
        ks = (k_p * jnp.exp(last - cum_p)).astype(BF16)
        upd = _dot_tn(v_bf, ks)
        return out, st * jnp.exp(last) + jnp.where(same_head, upd, 0.0)

    for p in range(2):
        sl = slice(p * LANES, (p + 1) * LANES)
        st = st_ref[p]
        for c in order:
            rows = slice(c * HG_C, (c + 1) * HG_C)
            o_ref[rows, sl], st = chunk(q[rows, sl], k[rows, sl], v[rows, sl], cums[c][:, sl], st)
        st_ref[p] = st
        fin_ref[0, p] = st


def _hgrn_seq(g):
    return jnp.where(g < CTX_TILES, g, CTX_TILES + (g - CTX_TILES) // SMP_TILES_PER_SEQ)


def _hgrn_blk(g, rev):
    if not rev:
        return g
    j = g - CTX_TILES
    return jnp.where(g < CTX_TILES, g,
                     CTX_TILES + (j // SMP_TILES_PER_SEQ) * SMP_TILES_PER_SEQ
                     + (SMP_TILES_PER_SEQ - 1 - j % SMP_TILES_PER_SEQ))


def _hgrn_scan(proj, lb_dir, s0_dir, lvl, tri, layer, rev):
    d = 1 if rev else 0
    blk = functools.partial(_hgrn_blk, rev=rev)
    return pl.pallas_call(
        functools.partial(_hgrn_kernel, layer=layer, rev=rev),
        grid=(N_TILES,),
        in_specs=[
            pl.BlockSpec((TM, A_WIDTH), lambda g: (blk(g), 0)),
            pl.BlockSpec((TM, A_WIDTH), lambda g: (blk(g), 1 + d)),
            pl.BlockSpec((TM, A_WIDTH), lambda g: (blk(g), 3)),
            pl.BlockSpec((1, A_WIDTH), lambda g: (0, 0)),
            pl.BlockSpec((1, 2, LANES, LANES), lambda g: (_hgrn_seq(g), 0, 0, 0)),
            pl.BlockSpec((1, HG_C, HG_C), lambda g: (d, 0, 0)),
            pl.BlockSpec((1, HG_C, HG_C), lambda g: (d, 0, 0)),
        ],
        out_specs=[
            pl.BlockSpec((TM, A_WIDTH), lambda g: (blk(g), 0)),
            pl.BlockSpec((1, 2, LANES, LANES), lambda g: (_hgrn_seq(g), 0, 0, 0)),
        ],
        out_shape=[
            jax.ShapeDtypeStruct((T, A_WIDTH), F32),
            jax.ShapeDtypeStruct((N_SEQ, 2, LANES, LANES), F32),
        ],
        scratch_shapes=[pltpu.VMEM((2, LANES, LANES), F32)],
        compiler_params=_cparams(("arbitrary",)),
        name=f"hgrn_scan_{'bwd' if rev else 'fwd'}",
    )(proj, proj, proj, lb_dir, s0_dir, lvl, tri)


def _pack_state(s):
    n = s.shape[0]
    st = jnp.swapaxes(s, -1, -2).reshape(n, 2, 2, A_DK, A_DK)
    z = jnp.zeros_like(st[:, :, 0])
    top = jnp.concatenate([st[:, :, 0], z], axis=-1)
    bot = jnp.concatenate([z, st[:, :, 1]], axis=-1)
    return jnp.concatenate([top, bot], axis=-2)


def _unpack_state(sp):
    n = sp.shape[0]
    h0 = sp[:, :, :A_DK, :A_DK]
    h1 = sp[:, :, A_DK:, A_DK:]
    st = jnp.stack([h0, h1], axis=2).reshape(n, A_HEADS, A_DK, A_DK)
    return jnp.swapaxes(st, -1, -2)


def _half_rms(x, g):
    lane = _lane(x.shape)
    lo = lane < B_DK
    xx = x * x
    ms0 = jnp.sum(jnp.where(lo, xx, 0.0), axis=-1, keepdims=True) * (1.0 / B_DK)
    ms1 = jnp.sum(jnp.where(lo, 0.0, xx), axis=-1, keepdims=True) * (1.0 / B_DK)
    inv = jnp.where(lo, lax.rsqrt(ms0 + EPS), lax.rsqrt(ms1 + EPS))
    return x * inv * g


def _rope(x, cos, sin_signed):
    lane = _lane(x.shape)
    first = (lane % 32) < 16
    rot = jnp.where(first, pltpu.roll(x, LANES - 16, 1), pltpu.roll(x, 16, 1))
    return x * cos + rot * sin_signed


def _diff_softmax_pv(q_bf, keys_bf, vals_bf, lam):
    lane = _lane(q_bf.shape)
    zero = jnp.zeros_like(q_bf)
    acc = None
    parts = []
    for mp in range(2):
        qm = jnp.where((lane < B_DK) == (mp == 0), q_bf, zero)
        s = [_dot_nt(qm, kk) for kk in keys_bf]
        mx = functools.reduce(jnp.maximum, [jnp.max(si, axis=-1, keepdims=True) for si in s])
        e = [jnp.exp(si - mx) for si in s]
        den = functools.reduce(lambda a, b: a + b, [jnp.sum(ei, axis=-1, keepdims=True) for ei in e])
        parts.append((e, 1.0 / den))
    (e0, r0), (e1, r1) = parts
    r1 = r1 * lam
    for i in range(len(keys_bf)):
        a = (e0[i] * r0 - e1[i] * r1).astype(BF16)
        pv = jnp.dot(a, vals_bf[i], preferred_element_type=F32)
        acc = pv if acc is None else acc + pv
    return acc


def _subln(o, g, lam_init):
    return o * lax.rsqrt(jnp.mean(o * o, axis=-1, keepdims=True) + EPS) * g * (1.0 - lam_init)


def _attn_ctx_kernel(lam_ref, q_ref, k_ref, v_ref, gq_ref, gk_ref, gs_ref, *rest, lam_init, layer):
    if layer:
        pk_ref, pv_ref, o_ref, nk_ref, nv_ref = rest
        nk_ref[0, :layer] = pk_ref[0]
        nv_ref[0, :layer] = pv_ref[0]
    else:
        o_ref, nk_ref, nv_ref = rest
    lam = lam_ref[0, 0]
    qn = _half_rms(q_ref[...], gq_ref[...]) * (B_DK ** -0.5)
    kn = _half_rms(k_ref[...], gk_ref[...])
    v = v_ref[...]
    nk_ref[0, layer, 0, 0] = kn[:, :B_DK]
    nk_ref[0, layer, 1, 0] = kn[:, B_DK:]
    nv_ref[0, layer, 0] = v
    o = _diff_softmax_pv(qn.astype(BF16), [kn.astype(BF16)], [v.astype(BF16)], lam)
    o_ref[...] = _subln(o, gs_ref[...], lam_init)


def _attn_ctx(proj, lam, gq2, gk2, gs, lam_init, layer, prev_k, prev_v):
    qcol, kcol, vcol = 5 * A_WIDTH // LANES, 5 * A_WIDTH // LANES + 4, 5 * A_WIDTH // LANES + 8
    prev_specs, prev_args = [], []
    if layer:
        prev_specs = [pl.BlockSpec((1, layer, 2, 1, SEQ, B_DK), lambda b, h: (b, 0, 0, h, 0, 0)),
                      pl.BlockSpec((1, layer, 1, SEQ, B_DV), lambda b, h: (b, 0, h, 0, 0))]
        prev_args = [prev_k, prev_v]
    n_l = layer + 1
    return pl.pallas_call(
        functools.partial(_attn_ctx_kernel, lam_init=lam_init, layer=layer),
        grid=(BATCH, B_HEADS),
        in_specs=[
            pl.BlockSpec(memory_space=pltpu.SMEM),
            pl.BlockSpec((SEQ, LANES), lambda b, h: (b, qcol + h)),
            pl.BlockSpec((SEQ, LANES), lambda b, h: (b, kcol + h)),
            pl.BlockSpec((SEQ, LANES), lambda b, h: (b, vcol + h)),
            pl.BlockSpec((1, LANES), lambda b, h: (0, 0)),
            pl.BlockSpec((1, LANES), lambda b, h: (0, 0)),
            pl.BlockSpec((1, LANES), lambda b, h: (0, 0)),
        ] + prev_specs,
        out_specs=[
            pl.BlockSpec((SEQ, LANES), lambda b, h: (b, h)),
            pl.BlockSpec((1, n_l, 2, 1, SEQ, B_DK), lambda b, h: (b, 0, 0, h, 0, 0)),
            pl.BlockSpec((1, n_l, 1, SEQ, B_DV), lambda b, h: (b, 0, h, 0, 0)),
        ],
        out_shape=[
            jax.ShapeDtypeStruct((T_CTX, B_WIDTH), F32),
            jax.ShapeDtypeStruct((BATCH, n_l, 2, B_HEADS, SEQ, B_DK), F32),
            jax.ShapeDtypeStruct((BATCH, n_l, B_HEADS, SEQ, B_DV), F32),
        ],
        compiler_params=_cparams(("arbitrary", "arbitrary")),
        name="diff_attention_ctx",
    )(lam, proj, proj, proj, gq2, gk2, gs, *prev_args)


ATT_TQ = 256


def _attn_smp_kernel(lam_ref, q_ref, k_ref, v_ref, ck_ref, cv_ref, cos_ref, sin_ref, gq_ref, gk_ref, gs_ref,
                     o_ref, qs_ref, ks_ref, *, lam_init):
    lam = lam_ref[0, 0]
    cos = cos_ref[...]
    sin = sin_ref[...]
    qn = _rope(_half_rms(q_ref[...], gq_ref[...]), cos, sin) * (B_DK ** -0.5)
    qs_ref[...] = qn.astype(BF16)
    ks_ref[...] = _rope(_half_rms(k_ref[...], gk_ref[...]), cos, sin).astype(BF16)
    ck = jnp.concatenate([ck_ref[0, 0, 0, 0], ck_ref[0, 0, 1, 0]], axis=-1).astype(BF16)
    cv = cv_ref[0, 0, 0].astype(BF16)
    v_bf = v_ref[...].astype(BF16)
    k_bf = ks_ref[...]
    g = gs_ref[...]

    def body(i, carry):
        r0 = pl.multiple_of(i * ATT_TQ, ATT_TQ)
        q_bf = qs_ref[pl.ds(r0, ATT_TQ), :]
        o = _diff_softmax_pv(q_bf, [k_bf, ck], [v_bf, cv], lam)
        o_ref[pl.ds(r0, ATT_TQ), :] = _subln(o, g, lam_init)
        return carry

    lax.fori_loop(0, DEC_SEQ // ATT_TQ, body, 0)


def _attn_smp(proj, lam, cache_k, cache_v, cos, sin, gq2, gk2, gs, layer, lam_init):
    qcol, kcol, vcol = 5 * A_WIDTH // LANES, 5 * A_WIDTH // LANES + 4, 5 * A_WIDTH // LANES + 8
    r0 = T_CTX // DEC_SEQ
    return pl.pallas_call(
        functools.partial(_attn_smp_kernel, lam_init=lam_init),
        grid=(DEC_BATCH, B_HEADS),
        in_specs=[
            pl.BlockSpec(memory_space=pltpu.SMEM),
            pl.BlockSpec((DEC_SEQ, LANES), lambda b, h: (r0 + b, qcol + h)),
            pl.BlockSpec((DEC_SEQ, LANES), lambda b, h: (r0 + b, kcol + h)),
            pl.BlockSpec((DEC_SEQ, LANES), lambda b, h: (r0 + b, vcol + h)),
            pl.BlockSpec((1, 1, 2, 1, PAST, B_DK), lambda b, h: (b, layer, 0, h, 0, 0)),
            pl.BlockSpec((1, 1, 1, PAST, B_DV), lambda b, h: (b, layer, h, 0, 0)),
            pl.BlockSpec((DEC_SEQ, LANES), lambda b, h: (0, 0)),
            pl.BlockSpec((DEC_SEQ, LANES), lambda b, h: (0, 0)),
            pl.BlockSpec((1, LANES), lambda b, h: (0, 0)),
            pl.BlockSpec((1, LANES), lambda b, h: (0, 0)),
            pl.BlockSpec((1, LANES), lambda b, h: (0, 0)),
        ],
        out_specs=pl.BlockSpec((DEC_SEQ, LANES), lambda b, h: (b, h)),
        out_shape=jax.ShapeDtypeStruct((T_SMP, B_WIDTH), F32),
        scratch_shapes=[pltpu.VMEM((DEC_SEQ, LANES), BF16), pltpu.VMEM((DEC_SEQ, LANES), BF16)],
        compiler_params=_cparams(("arbitrary", "arbitrary")),
        name="diff_attention_smp",
    )(lam, proj, proj, proj, cache_k, cache_v, cos, sin, gq2, gk2, gs)


def _rope_tables():
    n_rows = DEC_SEQ // GRID_W
    row = np.repeat(np.arange(n_rows), GRID_W).astype(np.float32)
    col = np.tile(np.arange(GRID_W), n_rows).astype(np.float32)
    half = B_DK // 2
    inv_freq = (ROPE_BASE ** (-jnp.arange(0, half, 2, dtype=F32) / half))
    row_ang = jnp.asarray(row)[:, None] * inv_freq
    col_ang = jnp.asarray(col)[:, None] * inv_freq
    ang = jnp.concatenate([row_ang, row_ang, col_ang, col_ang], axis=-1)
    ang = jnp.concatenate([ang, ang], axis=-1)
    sign = np.where((np.arange(LANES) % 32) < 16, -1.0, 1.0).astype(np.float32)
    return jnp.cos(ang), jnp.sin(ang) * sign


CM_ROWS = 512


def _gelu(x):
    return 0.5 * x * (1.0 + lax.erf(x * (2.0 ** -0.5)))


def _cmlp_kernel(u_ref, v_ref, g_ref, b_ref, ws_ref, bs_ref, o_ref):
    u = _gelu(u_ref[...])
    gv = _gelu(v_ref[...])
    mu = jnp.mean(gv, axis=-1, keepdims=True)
    dv = gv - mu
    var = jnp.mean(dv * dv, axis=-1, keepdims=True)
    vn = (dv * lax.rsqrt(var + EPS) * g_ref[...] + b_ref[...]).astype(BF16)
    lane = _lane((C_CHUNK, LANES))
    for c in range(CM_ROWS // C_CHUNK):
        rs = slice(c * C_CHUNK, (c + 1) * C_CHUNK)
        for p in range(2):
            cs = slice(p * LANES, (p + 1) * LANES)
            vp = vn[rs, cs]
            m0 = jnp.dot(ws_ref[2 * p].astype(BF16), vp, preferred_element_type=F32)
            m1 = jnp.dot(ws_ref[2 * p + 1].astype(BF16), vp, preferred_element_type=F32)
            mixed = jnp.where(lane < C_DG, m0, m1) + bs_ref[:, cs]
            o_ref[rs, cs] = u[rs, cs] * mixed


def _chunk_mlp(proj, ln_g, ln_b, w_s, bias_full):
    ucol = (5 * A_WIDTH + 3 * B_WIDTH) // C_WIDTH
    return pl.pallas_call(
        _cmlp_kernel,
        grid=(T // CM_ROWS,),
        in_specs=[
            pl.BlockSpec((CM_ROWS, C_WIDTH), lambda i: (i, ucol)),
            pl.BlockSpec((CM_ROWS, C_WIDTH), lambda i: (i, ucol + 1)),
            pl.BlockSpec((1, C_WIDTH), lambda i: (0, 0)),
            pl.BlockSpec((1, C_WIDTH), lambda i: (0, 0)),
            pl.BlockSpec((C_GROUPS, C_CHUNK, C_CHUNK), lambda i: (0, 0, 0)),
            pl.BlockSpec((C_CHUNK, C_WIDTH), lambda i: (0, 0)),
        ],
        out_specs=pl.BlockSpec((CM_ROWS, C_WIDTH), lambda i: (i, 0)),
        out_shape=jax.ShapeDtypeStruct((T, C_WIDTH), F32),
        compiler_params=_cparams(("arbitrary",)),
        name="chunk_mlp",
    )(proj, proj, ln_g.reshape(1, C_WIDTH), ln_b.reshape(1, C_WIDTH), w_s, bias_full)


def _postmix_kernel(*refs, n_x):
    (of_ref, ob_ref, ag_ref, hg_ref, hsel_ref, bc_ref, bs_ref, c_ref, w_ref, gate1_ref, shift2_ref, scale2_ref,
     g2_ref, wrh_ref, wrl_ref, br_ref, x1_ref, h2_ref, idx_ref, gw_ref) = refs[n_x:]
    x = _stream_tile(refs[:n_x])
    o = of_ref[...] + ob_ref[...]
    ms = _dot_sel(o * o, hsel_ref[...]) * (1.0 / A_DK)
    ag = ag_ref[...]
    a = o * lax.rsqrt(ms + EPS) * hg_ref[...] * (ag * jax.nn.sigmoid(ag))
    b = jnp.where(pl.program_id(0) < CTX_TILES, bc_ref[...], bs_ref[...])
    mixed = jnp.dot(a.astype(BF16), w_ref[0:A_WIDTH, :], preferred_element_type=F32)
    mixed = mixed + jnp.dot(b.astype(BF16), w_ref[A_WIDTH:A_WIDTH + B_WIDTH, :], preferred_element_type=F32)
    mixed = mixed + jnp.dot(c_ref[...].astype(BF16), w_ref[A_WIDTH + B_WIDTH:, :], preferred_element_type=F32)
    x1 = x + gate1_ref[0] * mixed
    x1_ref[...] = x1
    y = x1 * lax.rsqrt(jnp.mean(x1 * x1, axis=-1, keepdims=True) + EPS) * g2_ref[...]
    h2 = y * (1.0 + scale2_ref[0]) + shift2_ref[0]
    h2_ref[...] = pltpu.einshape("t(jl)->tjl", h2, l=LANES)
    hi = h2.astype(BF16)
    lo = (h2 - hi.astype(F32)).astype(BF16)
    lg = jnp.dot(lo, wrh_ref[...], preferred_element_type=F32)
    lg = lg + jnp.dot(hi, wrl_ref[...], preferred_element_type=F32)
    lg = lg + jnp.dot(hi, wrh_ref[...], preferred_element_type=F32) + br_ref[...]
    lt = lg.T[:N_EXPERTS]
    row = lax.broadcasted_iota(jnp.int32, lt.shape, 0)
    out_row = lax.broadcasted_iota(jnp.int32, (8, TM), 0)
    idx_out = jnp.zeros((8, TM), jnp.int32)
    val_out = jnp.zeros((8, TM), F32)
    top0 = None
    den = None
    for kk in range(TOP_K):
        mx = jnp.max(lt, axis=0, keepdims=True)
        am = jnp.min(jnp.where(lt == mx, row, N_EXPERTS), axis=0, keepdims=True)
        if kk == 0:
            top0 = mx
        e = jnp.exp(mx - top0)
        den = e if den is None else den + e
        idx_out = jnp.where(out_row == kk, am, idx_out)
        val_out = jnp.where(out_row == kk, e, val_out)
        lt = jnp.where(row == am, -jnp.inf, lt)
    idx_ref[...] = idx_out
    gw_ref[...] = val_out / den


def _post_mix(o_f, o_b, proj, hg, hsel, b_ctx, b_smp, c_out, w_out_bf, x, mod3, g2, wr_hi, wr_lo, br_pad):
    tile = lambda w: pl.BlockSpec((TM, w), lambda i: (i, 0))
    const = lambda shape: pl.BlockSpec(shape, lambda i: tuple(0 for _ in shape))
    modspec = lambda j: pl.BlockSpec((1, 1, D), lambda i: (_mod_row(i), 0, j))
    rowsT = pl.BlockSpec((8, TM), lambda i: (0, i))
    x_specs, x_args = _stream_specs(x)
    return pl.pallas_call(
        functools.partial(_postmix_kernel, n_x=len(x_args)),
        grid=(N_TILES,),
        in_specs=x_specs + [
            tile(A_WIDTH), tile(A_WIDTH),
            pl.BlockSpec((TM, A_WIDTH), lambda i: (i, 4)),
            const((1, A_WIDTH)), const((A_WIDTH, A_WIDTH)),
            pl.BlockSpec((TM, B_WIDTH), lambda i: (jnp.minimum(i, CTX_TILES - 1), 0)),
            pl.BlockSpec((TM, B_WIDTH), lambda i: (jnp.maximum(i - CTX_TILES, 0), 0)),
            tile(C_WIDTH),
            const((D, D)),
            modspec(2), modspec(3), modspec(4),
            const((1, D)), const((D, LANES)), const((D, LANES)), const((1, LANES)),
        ],
        out_specs=[tile(D), pl.BlockSpec((TM, ROW_TILES, LANES), lambda i: (i, 0, 0)), rowsT, rowsT],
        out_shape=[
            jax.ShapeDtypeStruct((T, D), F32),
            jax.ShapeDtypeStruct((T, ROW_TILES, LANES), F32),
            jax.ShapeDtypeStruct((8, T), jnp.int32),
            jax.ShapeDtypeStruct((8, T), F32),
        ],
        compiler_params=_cparams(("arbitrary",)),
        name="post_mix_router",
    )(*x_args, o_f, o_b, proj, hg, hsel, b_ctx, b_smp, c_out, w_out_bf, mod3, mod3, mod3, g2.reshape(1, D),
      wr_hi, wr_lo, br_pad)


def _route_kernel(idx_ref, dest_ref, meta_ref):
    erow = lax.broadcasted_iota(jnp.int32, (N_EXPERTS, TM), 0)
    s_i = lax.broadcasted_iota(jnp.int32, (TM, TM), 0)
    t_i = lax.broadcasted_iota(jnp.int32, (TM, TM), 1)
    earlier = (s_i < t_i).astype(BF16)
    out_row = lax.broadcasted_iota(jnp.int32, (8, TM), 0)

    def onehots(i):
        idx = idx_ref[:, pl.ds(pl.multiple_of(i * TM, TM), TM)]
        return [(erow == idx[kk:kk + 1, :]) for kk in range(TOP_K)]

    def count_tile(i, run):
        ohs = onehots(i)
        base = run
        pos = jnp.zeros((8, TM), F32)
        for kk in range(TOP_K):
            ohf = ohs[kk].astype(F32)
            before = jnp.dot(ohs[kk].astype(BF16), earlier, preferred_element_type=F32)
            p = jnp.sum(ohf * (base + before), axis=0, keepdims=True)
            pos = jnp.where(out_row == kk, p, pos)
            base = base + jnp.sum(ohf, axis=1, keepdims=True)
        dest_ref[:, pl.ds(pl.multiple_of(i * TM, TM), TM)] = pos.astype(jnp.int32)
        return base

    counts = lax.fori_loop(0, N_TILES, count_tile, jnp.zeros((N_EXPERTS, 1), F32)).astype(jnp.int32)
    bm_shift = MOE_BM.bit_length() - 1
    padded = lax.shift_left(lax.shift_right_logical(counts + (MOE_BM - 1), bm_shift), bm_shift)
    e_r = lax.broadcasted_iota(jnp.int32, (N_EXPERTS, N_EXPERTS), 0)
    e_c = lax.broadcasted_iota(jnp.int32, (N_EXPERTS, N_EXPERTS), 1)
    incl = (e_c <= e_r).astype(BF16)
    pad_end = _sel_dot(incl, jnp.broadcast_to(padded.astype(F32), (N_EXPERTS, LANES)))[:, :1]
    pad_start = pad_end - padded.astype(F32)

    def place_tile(i, carry):
        ohs = onehots(i)
        sl = pl.ds(pl.multiple_of(i * TM, TM), TM)
        off = jnp.zeros((8, TM), F32)
        for kk in range(TOP_K):
            o = jnp.sum(ohs[kk].astype(F32) * pad_start, axis=0, keepdims=True)
            off = jnp.where(out_row == kk, o, off)
        dest_ref[:, sl] = dest_ref[:, sl] + off.astype(jnp.int32)
        return carry

    lax.fori_loop(0, N_TILES, place_tile, 0)

    total = jnp.max(pad_end, axis=0, keepdims=True)
    lane_i = lax.broadcasted_iota(jnp.int32, (1, TM), 1)
    blk0 = (lane_i * MOE_BM).astype(F32)
    block_e = jnp.sum((pad_end <= blk0).astype(F32), axis=0, keepdims=True)
    live_end = pad_start + counts.astype(F32)
    sel = erow.astype(F32) == block_e
    live = jnp.sum(jnp.where(sel, live_end, 0.0), axis=0, keepdims=True)
    valid = jnp.where(blk0 < total, jnp.clip(live - blk0, 0.0, float(MOE_BM)), 0.0)
    own = erow == lane_i
    n_blk = jnp.sum(jnp.where(own, padded.astype(F32), 0.0), axis=0, keepdims=True) * (1.0 / MOE_BM)
    first_blk = jnp.sum(jnp.where(own, pad_start, 0.0), axis=0, keepdims=True) * (1.0 / MOE_BM)
    meta = jnp.where(out_row == 0, valid, 0.0)
    meta = jnp.where(out_row == 1, n_blk, meta)
    meta = jnp.where(out_row == 2, first_blk, meta)
    meta = jnp.where(out_row == 3, total * (1.0 / MOE_BM), meta)
    meta_ref[...] = meta.astype(jnp.int32)


def _route(idx_t):
    assert MOE_BLOCKS <= TM
    return pl.pallas_call(
        _route_kernel,
        out_shape=[jax.ShapeDtypeStruct((8, T), jnp.int32), jax.ShapeDtypeStruct((8, TM), jnp.int32)],
        compiler_params=pltpu.CompilerParams(vmem_limit_bytes=VMEM_LIMIT),
        name="moe_route",
    )(idx_t)


def _moe_kernel(bv_ref, nb_ref, g0_ref, tot_ref, x_hbm, wgu_ref, bgu_ref, wdn_ref, bdn_ref, y_hbm,
                wgu_bf, wdn_bf, xbuf, ybuf, xb_ref, xsem, ysem):
    e = pl.program_id(0)
    n_blk = nb_ref[e]
    first = g0_ref[e]
    total = tot_ref[0]
    ahead = MOE_RING - 1

    def x_copy(g):
        slot = g % MOE_RING
        return pltpu.make_async_copy(x_hbm.at[pl.ds(g * MOE_BM, MOE_BM)], xbuf.at[slot], xsem.at[slot])

    def y_copy(g):
        slot = g % MOE_RING
        return pltpu.make_async_copy(ybuf.at[slot], y_hbm.at[pl.ds(g * MOE_BM, MOE_BM)], ysem.at[slot])

    @pl.when(e == 0)
    def _():
        for g in range(ahead):
            @pl.when(g < total)
            def _():
                x_copy(g).start()

    @pl.when(n_blk > 0)
    def _():
        wgu_bf[...] = wgu_ref[0, 0].astype(BF16)
        wdn_bf[...] = wdn_ref[0, 0].astype(BF16)

    def block(j, carry):
        g = first + j
        slot = g % MOE_RING
        x_copy(g).wait()

        @pl.when(g + ahead < total)
        def _():
            x_copy(g + ahead).start()

        @pl.when(g >= MOE_RING)
        def _():
            y_copy(g - MOE_RING).wait()

        live = lax.broadcasted_iota(jnp.int32, (MOE_BM, LANES), 0) < bv_ref[g]
        xt = pltpu.einshape("tjl->jtl", xbuf[slot])
        for jj in range(ROW_TILES):
            xb_ref[:, jj * LANES:(jj + 1) * LANES] = jnp.where(live, xt[jj], 0.0).astype(BF16)
        gu = jnp.dot(xb_ref[...], wgu_bf[...], preferred_element_type=F32) + bgu_ref[0, 0]
        glu = jnp.minimum(gu[:, :D], SWIGLU_LIMIT)
        lin = jnp.clip(gu[:, D:], -SWIGLU_LIMIT, SWIGLU_LIMIT)
        act = glu * jax.nn.sigmoid(SWIGLU_ALPHA * glu) * (lin + 1.0)
        y = jnp.dot(act.astype(BF16), wdn_bf[...], preferred_element_type=F32) + bdn_ref[0, 0]
        ybuf[slot] = pltpu.einshape("t(jl)->tjl", y, l=LANES)
        y_copy(g).start()
        return carry

    lax.fori_loop(0, n_blk, block, 0)

    @pl.when(e == N_EXPERTS - 1)
    def _():
        for back in range(MOE_RING, 0, -1):
            @pl.when(total >= back)
            def _():
                y_copy(total - back).wait()

        def fill(g, carry):
            ybuf[g % MOE_RING] = jnp.zeros((MOE_BM, ROW_TILES, LANES), F32)
            y_copy(g).start()
            y_copy(g).wait()
            return carry

        lax.fori_loop(total, MOE_BLOCKS, fill, 0)


def _moe_ffn(block_valid, n_blk, first_blk, total_blk, xs, w_gu, b_gu, w_dn, b_dn, layer):
    rows = (MOE_BM, ROW_TILES, LANES)
    return pl.pallas_call(
        _moe_kernel,
        grid_spec=pltpu.PrefetchScalarGridSpec(
            num_scalar_prefetch=4,
            grid=(N_EXPERTS,),
            in_specs=[
                pl.BlockSpec(memory_space=pl.ANY),
                pl.BlockSpec((1, 1, D, 2 * D), lambda e, *_: (layer, e, 0, 0)),
                pl.BlockSpec((1, 1, 1, 2 * D), lambda e, *_: (layer, e, 0, 0)),
                pl.BlockSpec((1, 1, D, D), lambda e, *_: (layer, e, 0, 0)),
                pl.BlockSpec((1, 1, 1, D), lambda e, *_: (layer, e, 0, 0)),
            ],
            out_specs=pl.BlockSpec(memory_space=pl.ANY),
            scratch_shapes=[
                pltpu.VMEM((D, 2 * D), BF16), pltpu.VMEM((D, D), BF16),
                pltpu.VMEM((MOE_RING,) + rows, F32), pltpu.VMEM((MOE_RING,) + rows, F32),
                pltpu.VMEM((MOE_BM, D), BF16),
                pltpu.SemaphoreType.DMA((MOE_RING,)), pltpu.SemaphoreType.DMA((MOE_RING,)),
            ],
        ),
        out_shape=jax.ShapeDtypeStruct((MOE_ROWS, ROW_TILES, LANES), F32),
        compiler_params=_cparams(("arbitrary",)),
        name="moe_expert_ffn",
    )(block_valid, n_blk, first_blk, total_blk, xs, w_gu, b_gu.reshape(DEPTH, N_EXPERTS, 1, 2 * D), w_dn,
      b_dn.reshape(DEPTH, N_EXPERTS, 1, D))


def _combine_kernel(x1_ref, y_ref, gw_ref, gate2_ref, o_ref):
    gw = jnp.concatenate([gw_ref[...], jnp.zeros((LANES - 8, TM), F32)], axis=0).T
    ys = [pltpu.einshape("tjl->jtl", y_ref[kk]) for kk in range(TOP_K)]
    for j in range(ROW_TILES):
        cs = slice(j * LANES, (j + 1) * LANES)
        acc = None
        for kk in range(TOP_K):
            term = ys[kk][j] * gw[:, kk:kk + 1]
            acc = term if acc is None else acc + term
        o_ref[:, cs] = x1_ref[:, cs] + gate2_ref[0, :, cs] * acc


def _combine(x1, yg, gw, mod3):
    return pl.pallas_call(
        _combine_kernel,
        grid=(N_TILES,),
        in_specs=[
            pl.BlockSpec((TM, D), lambda i: (i, 0)),
            pl.BlockSpec((TOP_K, TM, ROW_TILES, LANES), lambda i: (0, i, 0, 0)),
            pl.BlockSpec((8, TM), lambda i: (0, i)),
            pl.BlockSpec((1, 1, D), lambda i: (_mod_row(i), 0, 5)),
        ],
        out_specs=pl.BlockSpec((TM, D), lambda i: (i, 0)),
        out_shape=jax.ShapeDtypeStruct((T, D), F32),
        compiler_params=_cparams(("arbitrary",)),
        name="moe_combine",
    )(x1, yg, gw, mod3)


def _sc_mesh():
    return plsc.VectorSubcoreMesh(core_axis_name="c", subcore_axis_name="s")


def _sc_worker():
    return lax.axis_index("s") * SC_CORES + lax.axis_index("c")


def _sc_dispatch(h2t, dest_km):
    per_w = T // SC_WORKERS

    @functools.partial(
        pl.kernel, mesh=_sc_mesh(),
        out_type=jax.ShapeDtypeStruct((MOE_ROWS, ROW_TILES, LANES), F32),
        scratch_types=[pltpu.VMEM((SC_WIN,), jnp.int32), pltpu.VMEM((SC_WIN, ROW_TILES, LANES), F32),
                       pltpu.SemaphoreType.DMA],
    )
    def run(h_hbm, d_hbm, o_hbm, idx_v, rows_v, sem):
        w0 = _sc_worker() * per_w

        @pl.loop(0, per_w // SC_WIN)
        def _(w):
            base = pl.multiple_of(w0 + w * SC_WIN, SC_WIN)
            pltpu.sync_copy(h_hbm.at[pl.ds(base, SC_WIN)], rows_v)
            for kk in range(TOP_K):
                pltpu.sync_copy(d_hbm.at[pl.ds(kk * T + base, SC_WIN)], idx_v)
                pltpu.async_copy(rows_v, o_hbm.at[idx_v], sem).wait()

    return run(h2t, dest_km)


def _sc_gather(yb, dest_km):
    n = TOP_K * T
    per_w = n // SC_WORKERS

    @functools.partial(
        pl.kernel, mesh=_sc_mesh(),
        out_type=jax.ShapeDtypeStruct((n, ROW_TILES, LANES), F32),
        scratch_types=[pltpu.VMEM((SC_WIN,), jnp.int32), pltpu.VMEM((SC_WIN, ROW_TILES, LANES), F32),
                       pltpu.SemaphoreType.DMA],
    )
    def run(y_hbm, d_hbm, o_hbm, idx_v, rows_v, sem):
        w0 = _sc_worker() * per_w

        @pl.loop(0, per_w // SC_WIN)
        def _(w):
            base = pl.multiple_of(w0 + w * SC_WIN, SC_WIN)
            pltpu.sync_copy(d_hbm.at[pl.ds(base, SC_WIN)], idx_v)
            pltpu.async_copy(y_hbm.at[idx_v], rows_v, sem).wait()
            pltpu.sync_copy(rows_v, o_hbm.at[pl.ds(base, SC_WIN)])

    return run(yb, dest_km)


def kernel(x_prompt, x_sample, c, cache_diff_k, cache_diff_v, state_hgrn, c_ctx, norm_mix_g, norm_ffn_g, w_mod, b_mod, w_in, w_out, hgrn_lower_bounds, hgrn_norm_g, diff_q_norm_g, diff_k_norm_g, diff_lambda_q1, diff_lambda_k1, diff_lambda_q2, diff_lambda_k2, diff_subln_g, cmlp_ln_g, cmlp_ln_b, cmlp_w_s, cmlp_b_s, router_w, router_b, moe_w_gate_up, moe_b_gate_up, moe_w_down, moe_b_down):
    x = (x_prompt.reshape(T_CTX, D), x_sample.reshape(T_SMP, D))
    cvec = jnp.concatenate([c_ctx[None, :], c, jnp.zeros((MOD_ROWS - 1 - DEC_BATCH, D), F32)], axis=0)
    mod = _modulation(cvec, w_mod, b_mod)

    lvl_np, tri_np = _hgrn_tables()
    lvl = jnp.asarray(lvl_np)
    tri = jnp.asarray(tri_np, dtype=BF16)
    cos, sin = _rope_tables()
    hsel = jnp.asarray(np.kron(np.eye(A_HEADS), np.ones((A_DK, A_DK))), dtype=BF16)
    sm = jax.nn.softmax(hgrn_lower_bounds.astype(F32), axis=0)
    lb_all = jnp.cumsum(sm, axis=0) - sm[0]

    new_k, new_v, new_s = None, None, []
    for l in range(DEPTH):
        mod3 = mod[l].reshape(MOD_ROWS, 1, 6 * D)
        proj = _in_projection(x, norm_mix_g[l], mod3, w_in[l].astype(BF16))

        s0 = jnp.concatenate([jnp.zeros((BATCH, 2, A_HEADS, A_DK, A_DK), F32), state_hgrn[:, l]], axis=0)
        o_dir, fin_dir = [], []
        for d in range(2):
            o_d, fin_d = _hgrn_scan(proj, lb_all[l, d].reshape(1, A_WIDTH), _pack_state(s0[:, d]),
                                    lvl, tri, l, d == 1)
            o_dir.append(o_d)
            fin_dir.append(_unpack_state(fin_d[:BATCH]))
        new_s.append(jnp.stack(fin_dir, axis=1))

        lam_init = 0.8 - 0.6 * math.exp(-0.3 * l)
        lam = (jnp.exp(jnp.sum(diff_lambda_q1[l] * diff_lambda_k1[l]))
               - jnp.exp(jnp.sum(diff_lambda_q2[l] * diff_lambda_k2[l])) + lam_init).reshape(1, 1)
        gq2 = jnp.tile(diff_q_norm_g[l], 2).reshape(1, LANES)
        gk2 = jnp.tile(diff_k_norm_g[l], 2).reshape(1, LANES)
        gs = diff_subln_g[l].reshape(1, LANES)
        b_ctx, new_k, new_v = _attn_ctx(proj, lam, gq2, gk2, gs, lam_init, l, new_k, new_v)
        b_smp = _attn_smp(proj, lam, cache_diff_k, cache_diff_v, cos, sin, gq2, gk2, gs, l, lam_init)

        bias_full = jnp.repeat(cmlp_b_s[l].T, C_DG, axis=1)
        c_out = _chunk_mlp(proj, cmlp_ln_g[l], cmlp_ln_b[l], cmlp_w_s[l], bias_full)

        hg = jnp.tile(hgrn_norm_g[l], A_HEADS).reshape(1, A_WIDTH)
        wr_pad = jnp.pad(router_w[l], ((0, 0), (0, LANES - N_EXPERTS)))
        wr_hi = wr_pad.astype(BF16)
        wr_lo = (wr_pad - wr_hi.astype(F32)).astype(BF16)
        br_pad = jnp.pad(router_b[l], (0, LANES - N_EXPERTS)).reshape(1, LANES)
        x1, h2, idx_t, gw_t = _post_mix(o_dir[0], o_dir[1], proj, hg, hsel, b_ctx, b_smp, c_out,
                                        w_out[l].astype(BF16), x, mod3, norm_ffn_g[l], wr_hi, wr_lo, br_pad)

        dest_t, meta = _route(idx_t)
        dest_km = dest_t[:TOP_K].reshape(-1)
        xs = _sc_dispatch(h2, dest_km)
        yb = _moe_ffn(meta[0, :MOE_BLOCKS], meta[1, :N_EXPERTS], meta[2, :N_EXPERTS], meta[3, :1], xs,
                      moe_w_gate_up, moe_b_gate_up, moe_w_down, moe_b_down, l)
        yg = _sc_gather(yb, dest_km).reshape(TOP_K, T, ROW_TILES, LANES)
        x = _combine(x1, yg, gw_t, mod3)

    y_prompt = x[:T_CTX].reshape(BATCH, SEQ, D)
    y_sample = x[T_CTX:].reshape(DEC_BATCH, DEC_SEQ, D)
    return (y_prompt, y_sample, new_k, new_v, jnp.stack(new_s, axis=1))
```

```python
import functools
import math

import numpy as np
import jax
import jax.numpy as jnp
from jax import lax
from jax.experimental import pallas as pl
from jax.experimental.pallas import tpu as pltpu
from jax.experimental.pallas import tpu_sc as plsc

F32 = jnp.float32
BF16 = jnp.bfloat16

D = 1024
DEPTH = 2
BATCH, SEQ = 16, 256
DEC_BATCH, DEC_SEQ = 8, 1024
PAST = 512
GRID_W = 64
A_HEADS, A_DK = 4, 64
A_WIDTH = 256
B_HEADS, B_DK, B_DV = 4, 64, 128
B_WIDTH = 512
C_GROUPS, C_CHUNK, C_WIDTH, C_DG = 4, 128, 256, 64
IN_WIDTH = 5 * A_WIDTH + 3 * B_WIDTH + 2 * C_WIDTH
N_EXPERTS, TOP_K = 32, 4
SWIGLU_LIMIT, SWIGLU_ALPHA = 7.0, 1.702
ROPE_BASE = 10000.0
EPS = 1e-6

T_CTX = BATCH * SEQ
T_SMP = DEC_BATCH * DEC_SEQ
T = T_CTX + T_SMP
N_SEQ = BATCH + DEC_BATCH
MOD_ROWS = 16

TM = 256
N_TILES = T // TM
CTX_TILES = T_CTX // TM
SMP_TILES_PER_SEQ = DEC_SEQ // TM
LANES = 128
MOE_BM = 256
MOE_ROWS = T * TOP_K + N_EXPERTS * MOE_BM
MOE_BLOCKS = MOE_ROWS // MOE_BM
MOE_RING = 4
ROW_TILES = D // LANES
SC_CORES, SC_SUBCORES = 2, 16
SC_WORKERS = SC_CORES * SC_SUBCORES
SC_WIN = 64
VMEM_LIMIT = 56 * 1024 * 1024


def _cparams(sem):
    return pltpu.CompilerParams(dimension_semantics=sem, vmem_limit_bytes=VMEM_LIMIT)


def _mod_row(i):
    return jnp.where(i < CTX_TILES, 0, 1 + (i - CTX_TILES) // SMP_TILES_PER_SEQ)


def _split3(x):
    hi = x.astype(BF16)
    r = x - hi.astype(F32)
    mid = r.astype(BF16)
    lo = (r - mid.astype(F32)).astype(BF16)
    return hi, mid, lo


def _sel_dot(sel, x):
    hi, mid, lo = _split3(x)
    acc = jnp.dot(sel, lo, preferred_element_type=F32)
    acc = acc + jnp.dot(sel, mid, preferred_element_type=F32)
    return acc + jnp.dot(sel, hi, preferred_element_type=F32)


def _dot_sel(x, sel):
    hi, mid, lo = _split3(x)
    acc = jnp.dot(lo, sel, preferred_element_type=F32)
    acc = acc + jnp.dot(mid, sel, preferred_element_type=F32)
    return acc + jnp.dot(hi, sel, preferred_element_type=F32)


def _dot_nt(a, b):
    return lax.dot_general(a, b, (((1,), (1,)), ((), ())), preferred_element_type=F32)


def _dot_tn(a, b):
    return lax.dot_general(a, b, (((0,), (0,)), ((), ())), preferred_element_type=F32)


def _lane(shape):
    return lax.broadcasted_iota(jnp.int32, shape, len(shape) - 1)


def _mod_kernel(c_ref, w_ref, b_ref, o_ref):
    c = c_ref[...]
    s = c * jax.nn.sigmoid(c)
    o_ref[0] = jnp.dot(s.astype(BF16), w_ref[0].astype(BF16), preferred_element_type=F32) + b_ref[0]


def _modulation(cvec, w_mod, b_mod):
    tn = 1536
    return pl.pallas_call(
        _mod_kernel,
        grid=(DEPTH, 6 * D // tn),
        in_specs=[
            pl.BlockSpec((MOD_ROWS, D), lambda l, j: (0, 0)),
            pl.BlockSpec((1, D, tn), lambda l, j: (l, 0, j)),
            pl.BlockSpec((1, 1, tn), lambda l, j: (l, 0, j)),
        ],
        out_specs=pl.BlockSpec((1, MOD_ROWS, tn), lambda l, j: (l, 0, j)),
        out_shape=jax.ShapeDtypeStruct((DEPTH, MOD_ROWS, 6 * D), F32),
        compiler_params=_cparams(("arbitrary", "arbitrary")),
        name="modulation",
    )(cvec, w_mod, b_mod.reshape(DEPTH, 1, 6 * D))


def _stream_specs(x):
    if isinstance(x, tuple):
        return [pl.BlockSpec((TM, D), lambda i: (jnp.minimum(i, CTX_TILES - 1), 0)),
                pl.BlockSpec((TM, D), lambda i: (jnp.maximum(i - CTX_TILES, 0), 0))], list(x)
    return [pl.BlockSpec((TM, D), lambda i: (i, 0))], [x]


def _stream_tile(x_refs):
    if len(x_refs) == 1:
        return x_refs[0][...]
    return jnp.where(pl.program_id(0) < CTX_TILES, x_refs[0][...], x_refs[1][...])


def _inproj_kernel(*refs, n_x):
    g_ref, shift_ref, scale_ref, w_ref, o_ref = refs[n_x:]
    x = _stream_tile(refs[:n_x])
    y = x * lax.rsqrt(jnp.mean(x * x, axis=-1, keepdims=True) + EPS) * g_ref[...]
    h = y * (1.0 + scale_ref[0]) + shift_ref[0]
    o_ref[...] = jnp.dot(h.astype(BF16), w_ref[...], preferred_element_type=F32)


def _in_projection(x, g, mod3, w_in_bf):
    x_specs, x_args = _stream_specs(x)
    return pl.pallas_call(
        functools.partial(_inproj_kernel, n_x=len(x_args)),
        grid=(N_TILES,),
        in_specs=x_specs + [
            pl.BlockSpec((1, D), lambda i: (0, 0)),
            pl.BlockSpec((1, 1, D), lambda i: (_mod_row(i), 0, 0)),
            pl.BlockSpec((1, 1, D), lambda i: (_mod_row(i), 0, 1)),
            pl.BlockSpec((D, IN_WIDTH), lambda i: (0, 0)),
        ],
        out_specs=pl.BlockSpec((TM, IN_WIDTH), lambda i: (i, 0)),
        out_shape=jax.ShapeDtypeStruct((T, IN_WIDTH), F32),
        compiler_params=_cparams(("arbitrary",)),
        name="in_projection",
    )(*x_args, g.reshape(1, D), mod3, mod3, w_in_bf)


HG_C = 128
HG_LEVELS = tuple(2 ** j for j in range(1, int(math.log2(HG_C)) + 1))


def _hgrn_tables():
    t = np.arange(HG_C)[:, None]
    s = np.arange(HG_C)[None, :]
    x = t ^ s
    lvl = np.zeros((HG_C, HG_C), np.int32)
    nz = x > 0
    lvl[nz] = np.floor(np.log2(x[nz])).astype(np.int32) + 1
    fwd = np.where(t >= s, lvl, -1).astype(np.int32)
    bwd = np.where(t <= s, lvl, -1).astype(np.int32)
    tri_f = (t >= s).astype(np.float32)
    tri_b = (t <= s).astype(np.float32)
    return np.stack([fwd, bwd]), np.stack([tri_f, tri_b])


def _block_ref(cum, m, idx):
    c, l = cum.shape
    if m >= 16:
        c3 = cum.reshape(c // m, m, l)
        r = c3[:, idx:idx + 1, :]
        return jnp.broadcast_to(r, (c // m, m, l)).reshape(c, l)
    c3 = cum.reshape(c // 8, 8, l)
    sub = lax.broadcasted_iota(jnp.int32, c3.shape, 1)
    out = None
    for j in range(8 // m - 1, -1, -1):
        cand = jnp.broadcast_to(c3[:, j * m + idx:j * m + idx + 1, :], c3.shape)
        out = cand if out is None else jnp.where(sub < (j + 1) * m, cand, out)
    return out.reshape(c, l)


def _hgrn_kernel(q_ref, z_ref, v_ref, lb_ref, s0_ref, lvl_ref, tri_ref, o_ref, fin_ref, st_ref, *, layer, rev):
    g = pl.program_id(0)
    first = jnp.logical_or(g < CTX_TILES, (g - CTX_TILES) % SMP_TILES_PER_SEQ == 0)

    @pl.when(first)
    def _():
        st_ref[...] = s0_ref[0]

    qr = q_ref[...]
    q = qr * jax.nn.sigmoid(qr) * (A_DK ** -0.5)
    z = z_ref[...]
    if layer == 0:
        lf = jnp.minimum(z, 0.0) - jnp.log(1.0 + jnp.exp(-jnp.abs(z)))
        k = jax.nn.sigmoid(-z)
    else:
        lbd = lb_ref[...]
        lf = jnp.log(lbd + (1.0 - lbd) * jax.nn.sigmoid(z))
        k = (1.0 - lbd) * jax.nn.sigmoid(-z)
    v = v_ref[...]
    tri = tri_ref[0]
    lvl = lvl_ref[0]
    last_row = 0 if rev else HG_C - 1
    n_chunks = TM // HG_C
    order = range(n_chunks - 1, -1, -1) if rev else range(n_chunks)
    cums = [_sel_dot(tri, lf[c * HG_C:(c + 1) * HG_C]) for c in range(n_chunks)]
    lane = _lane((HG_C, LANES))
    head_masks = (lane < A_DK, lane >= A_DK)
    lane_row = _lane((1, LANES))
    head_keep = ((lane_row < A_DK).astype(BF16), (lane_row >= A_DK).astype(BF16))
    lvl2 = jnp.concatenate([lvl, lvl], axis=0)
    level_masks = [lvl2 == i for i in range(len(HG_LEVELS) + 1)]
    r = lax.broadcasted_iota(jnp.int32, (LANES, LANES), 0)
    cl = lax.broadcasted_iota(jnp.int32, (LANES, LANES), 1)
    same_head = (r < A_DK) == (cl < A_DK)

    def chunk(q_p, k_p, v_p, cum_p, st):
        v_bf = v_p.astype(BF16)
        k_bf = k_p.astype(BF16)
        q_bf = q_p.astype(BF16)

        def both_heads(x_bf):
            return jnp.concatenate([x_bf * head_keep[0], x_bf * head_keep[1]], axis=0)

        scores = jnp.where(level_masks[0], _dot_nt(both_heads(q_bf), k_bf), 0.0)
        for li, m in enumerate(HG_LEVELS):
            ref = _block_ref(cum_p, m, m // 2 if rev else m // 2 - 1)
            dec = jnp.exp(-jnp.abs(cum_p - ref))
            qd = (q_p * dec).astype(BF16)
            kd = (k_p * dec).astype(BF16)
            scores = jnp.where(level_masks[li + 1], _dot_nt(both_heads(qd), kd), scores)
        pv = jnp.dot(scores.astype(BF16), v_bf, preferred_element_type=F32)
        o_intra = jnp.where(head_masks[0], pv[:HG_C], pv[HG_C:])
        q0 = (q_p * jnp.exp(cum_p)).astype(BF16)
        out = o_intra + _dot_nt(q0, st.astype(BF16))
        last = cum_p[last_row:last_row + 1, :]
        ks = (k_p * jnp.exp(last - cum_p)).astype(BF16)
        upd = _dot_tn(v_bf, ks)
        return out, st * jnp.exp(last) + jnp.where(same_head, upd, 0.0)

    for p in range(2):
        sl = slice(p * LANES, (p + 1) * LANES)
        st = st_ref[p]
        for c in order:
            rows = slice(c * HG_C, (c + 1) * HG_C)
            o_ref[rows, sl], st = chunk(q[rows, sl], k[rows, sl], v[rows, sl], cums[c][:, sl], st)
        st_ref[p] = st
        fin_ref[0, p] = st


def _hgrn_seq(g):
    return jnp.where(g < CTX_TILES, g, CTX_TILES + (g - CTX_TILES) // SMP_TILES_PER_SEQ)


def _hgrn_blk(g, rev):
    if not rev:
        return g
    j = g - CTX_TILES
    return jnp.where(g < CTX_TILES, g,
                     CTX_TILES + (j // SMP_TILES_PER_SEQ) * SMP_TILES_PER_SEQ
                     + (SMP_TILES_PER_SEQ - 1 - j % SMP_TILES_PER_SEQ))


def _hgrn_scan(proj, lb_dir, s0_dir, lvl, tri, layer, rev):
    d = 1 if rev else 0
    blk = functools.partial(_hgrn_blk, rev=rev)
    return pl.pallas_call(
        functools.partial(_hgrn_kernel, layer=layer, rev=rev),
        grid=(N_TILES,),
        in_specs=[
            pl.BlockSpec((TM, A_WIDTH), lambda g: (blk(g), 0)),
            pl.BlockSpec((TM, A_WIDTH), lambda g: (blk(g), 1 + d)),
            pl.BlockSpec((TM, A_WIDTH), lambda g: (blk(g), 3)),
            pl.BlockSpec((1, A_WIDTH), lambda g: (0, 0)),
            pl.BlockSpec((1, 2, LANES, LANES), lambda g: (_hgrn_seq(g), 0, 0, 0)),
            pl.BlockSpec((1, HG_C, HG_C), lambda g: (d, 0, 0)),
            pl.BlockSpec((1, HG_C, HG_C), lambda g: (d, 0, 0)),
        ],
        out_specs=[
            pl.BlockSpec((TM, A_WIDTH), lambda g: (blk(g), 0)),
            pl.BlockSpec((1, 2, LANES, LANES), lambda g: (_hgrn_seq(g), 0, 0, 0)),
        ],
        out_shape=[
            jax.ShapeDtypeStruct((T, A_WIDTH), F32),
            jax.ShapeDtypeStruct((N_SEQ, 2, LANES, LANES), F32),
        ],
        scratch_shapes=[pltpu.VMEM((2, LANES, LANES), F32)],
        compiler_params=_cparams(("arbitrary",)),
        name=f"hgrn_scan_{'bwd' if rev else 'fwd'}",
    )(proj, proj, proj, lb_dir, s0_dir, lvl, tri)


def _pack_state(s):
    n = s.shape[0]
    st = jnp.swapaxes(s, -1, -2).reshape(n, 2, 2, A_DK, A_DK)
    z = jnp.zeros_like(st[:, :, 0])
    top = jnp.concatenate([st[:, :, 0], z], axis=-1)
    bot = jnp.concatenate([z, st[:, :, 1]], axis=-1)
    return jnp.concatenate([top, bot], axis=-2)


def _unpack_state(sp):
    n = sp.shape[0]
    h0 = sp[:, :, :A_DK, :A_DK]
    h1 = sp[:, :, A_DK:, A_DK:]
    st = jnp.stack([h0, h1], axis=2).reshape(n, A_HEADS, A_DK, A_DK)
    return jnp.swapaxes(st, -1, -2)


def _half_rms(x, g):
    r = lax.broadcasted_iota(jnp.int32, (LANES, LANES), 0)
    c = lax.broadcasted_iota(jnp.int32, (LANES, LANES), 1)
    half_mean = jnp.where((r < B_DK) == (c < B_DK), 1.0 / B_DK, 0.0).astype(BF16)
    xx = x * x
    hi = xx.astype(BF16)
    lo = (xx - hi.astype(F32)).astype(BF16)
    ms = jnp.dot(lo, half_mean, preferred_element_type=F32) + jnp.dot(hi, half_mean, preferred_element_type=F32)
    return x * lax.rsqrt(ms + EPS) * g


def _rope(x, cos, sin_signed):
    lane = _lane(x.shape)
    first = (lane % 32) < 16
    rot = jnp.where(first, pltpu.roll(x, LANES - 16, 1), pltpu.roll(x, 16, 1))
    return x * cos + rot * sin_signed


def _with_ones(v_bf):
    return jnp.concatenate([v_bf, jnp.ones_like(v_bf)], axis=-1)


def _diff_softmax_pv(q_bf, keys_bf, vals_ext, lam):
    lane = _lane(q_bf.shape)
    zero = jnp.zeros_like(q_bf)
    outs = []
    for mp in range(2):
        qm = jnp.where((lane < B_DK) == (mp == 0), q_bf, zero)
        s = [_dot_nt(qm, kk) for kk in keys_bf]
        mx = functools.reduce(jnp.maximum, [jnp.max(si, axis=-1, keepdims=True) for si in s])
        acc = None
        for si, ve in zip(s, vals_ext):
            e = jnp.exp((si - mx).astype(BF16))
            pv = jnp.dot(e, ve, preferred_element_type=F32)
            acc = pv if acc is None else acc + pv
        outs.append(acc[:, :B_DV] / acc[:, B_DV:])
    return outs[0] - lam * outs[1]


def _subln(o, g, lam_init):
    return o * lax.rsqrt(jnp.mean(o * o, axis=-1, keepdims=True) + EPS) * g * (1.0 - lam_init)


def _attn_ctx_kernel(lam_ref, q_ref, k_ref, v_ref, gq_ref, gk_ref, gs_ref, *rest, lam_init, layer):
    if layer:
        pk_ref, pv_ref, o_ref, nk_ref, nv_ref = rest
        nk_ref[0, :layer] = pk_ref[0]
        nv_ref[0, :layer] = pv_ref[0]
    else:
        o_ref, nk_ref, nv_ref = rest
    lam = lam_ref[0, 0]
    qn = _half_rms(q_ref[...], gq_ref[...]) * (B_DK ** -0.5)
    kn = _half_rms(k_ref[...], gk_ref[...])
    v = v_ref[...]
    nk_ref[0, layer, 0, 0] = kn[:, :B_DK]
    nk_ref[0, layer, 1, 0] = kn[:, B_DK:]
    nv_ref[0, layer, 0] = v
    o = _diff_softmax_pv(qn.astype(BF16), [kn.astype(BF16)], [_with_ones(v.astype(BF16))], lam)
    o_ref[...] = _subln(o, gs_ref[...], lam_init)


def _attn_ctx(proj, lam, gq2, gk2, gs, lam_init, layer, prev_k, prev_v):
    qcol, kcol, vcol = 5 * A_WIDTH // LANES, 5 * A_WIDTH // LANES + 4, 5 * A_WIDTH // LANES + 8
    prev_specs, prev_args = [], []
    if layer:
        prev_specs = [pl.BlockSpec((1, layer, 2, 1, SEQ, B_DK), lambda b, h: (b, 0, 0, h, 0, 0)),
                      pl.BlockSpec((1, layer, 1, SEQ, B_DV), lambda b, h: (b, 0, h, 0, 0))]
        prev_args = [prev_k, prev_v]
    n_l = layer + 1
    return pl.pallas_call(
        functools.partial(_attn_ctx_kernel, lam_init=lam_init, layer=layer),
        grid=(BATCH, B_HEADS),
        in_specs=[
            pl.BlockSpec(memory_space=pltpu.SMEM),
            pl.BlockSpec((SEQ, LANES), lambda b, h: (b, qcol + h)),
            pl.BlockSpec((SEQ, LANES), lambda b, h: (b, kcol + h)),
            pl.BlockSpec((SEQ, LANES), lambda b, h: (b, vcol + h)),
            pl.BlockSpec((1, LANES), lambda b, h: (0, 0)),
            pl.BlockSpec((1, LANES), lambda b, h: (0, 0)),
            pl.BlockSpec((1, LANES), lambda b, h: (0, 0)),
        ] + prev_specs,
        out_specs=[
            pl.BlockSpec((SEQ, LANES), lambda b, h: (b, h)),
            pl.BlockSpec((1, n_l, 2, 1, SEQ, B_DK), lambda b, h: (b, 0, 0, h, 0, 0)),
            pl.BlockSpec((1, n_l, 1, SEQ, B_DV), lambda b, h: (b, 0, h, 0, 0)),
        ],
        out_shape=[
            jax.ShapeDtypeStruct((T_CTX, B_WIDTH), F32),
            jax.ShapeDtypeStruct((BATCH, n_l, 2, B_HEADS, SEQ, B_DK), F32),
            jax.ShapeDtypeStruct((BATCH, n_l, B_HEADS, SEQ, B_DV), F32),
        ],
        compiler_params=_cparams(("arbitrary", "arbitrary")),
        name="diff_attention_ctx",
    )(lam, proj, proj, proj, gq2, gk2, gs, *prev_args)


ATT_TQ = 256


def _attn_smp_kernel(lam_ref, q_ref, k_ref, v_ref, ck_ref, cv_ref, cos_ref, sin_ref, gq_ref, gk_ref, gs_ref,
                     o_ref, qs_ref, ks_ref, *, lam_init):
    lam = lam_ref[0, 0]
    cos = cos_ref[...]
    sin = sin_ref[...]
    qn = _rope(_half_rms(q_ref[...], gq_ref[...]), cos, sin) * (B_DK ** -0.5)
    qs_ref[...] = qn.astype(BF16)
    ks_ref[...] = _rope(_half_rms(k_ref[...], gk_ref[...]), cos, sin).astype(BF16)
    ck = jnp.concatenate([ck_ref[0, 0, 0, 0], ck_ref[0, 0, 1, 0]], axis=-1).astype(BF16)
    cv = _with_ones(cv_ref[0, 0, 0].astype(BF16))
    v_bf = _with_ones(v_ref[...].astype(BF16))
    k_bf = ks_ref[...]
    g = gs_ref[...]

    for i in range(DEC_SEQ // ATT_TQ):
        rows = slice(i * ATT_TQ, (i + 1) * ATT_TQ)
        o = _diff_softmax_pv(qs_ref[rows, :], [k_bf, ck], [v_bf, cv], lam)
        o_ref[rows, :] = _subln(o, g, lam_init)


def _attn_smp(proj, lam, cache_k, cache_v, cos, sin, gq2, gk2, gs, layer, lam_init):
    qcol, kcol, vcol = 5 * A_WIDTH // LANES, 5 * A_WIDTH // LANES + 4, 5 * A_WIDTH // LANES + 8
    r0 = T_CTX // DEC_SEQ
    return pl.pallas_call(
        functools.partial(_attn_smp_kernel, lam_init=lam_init),
        grid=(DEC_BATCH, B_HEADS),
        in_specs=[
            pl.BlockSpec(memory_space=pltpu.SMEM),
            pl.BlockSpec((DEC_SEQ, LANES), lambda b, h: (r0 + b, qcol + h)),
            pl.BlockSpec((DEC_SEQ, LANES), lambda b, h: (r0 + b, kcol + h)),
            pl.BlockSpec((DEC_SEQ, LANES), lambda b, h: (r0 + b, vcol + h)),
            pl.BlockSpec((1, 1, 2, 1, PAST, B_DK), lambda b, h: (b, layer, 0, h, 0, 0)),
            pl.BlockSpec((1, 1, 1, PAST, B_DV), lambda b, h: (b, layer, h, 0, 0)),
            pl.BlockSpec((DEC_SEQ, LANES), lambda b, h: (0, 0)),
            pl.BlockSpec((DEC_SEQ, LANES), lambda b, h: (0, 0)),
            pl.BlockSpec((1, LANES), lambda b, h: (0, 0)),
            pl.BlockSpec((1, LANES), lambda b, h: (0, 0)),
            pl.BlockSpec((1, LANES), lambda b, h: (0, 0)),
        ],
        out_specs=pl.BlockSpec((DEC_SEQ, LANES), lambda b, h: (b, h)),
        out_shape=jax.ShapeDtypeStruct((T_SMP, B_WIDTH), F32),
        scratch_shapes=[pltpu.VMEM((DEC_SEQ, LANES), BF16), pltpu.VMEM((DEC_SEQ, LANES), BF16)],
        compiler_params=_cparams(("arbitrary", "arbitrary")),
        name="diff_attention_smp",
    )(lam, proj, proj, proj, cache_k, cache_v, cos, sin, gq2, gk2, gs)


def _rope_tables():
    n_rows = DEC_SEQ // GRID_W
    row = np.repeat(np.arange(n_rows), GRID_W).astype(np.float32)
    col = np.tile(np.arange(GRID_W), n_rows).astype(np.float32)
    half = B_DK // 2
    inv_freq = (ROPE_BASE ** (-jnp.arange(0, half, 2, dtype=F32) / half))
    row_ang = jnp.asarray(row)[:, None] * inv_freq
    col_ang = jnp.asarray(col)[:, None] * inv_freq
    ang = jnp.concatenate([row_ang, row_ang, col_ang, col_ang], axis=-1)
    ang = jnp.concatenate([ang, ang], axis=-1)
    sign = np.where((np.arange(LANES) % 32) < 16, -1.0, 1.0).astype(np.float32)
    return jnp.cos(ang), jnp.sin(ang) * sign


CM_ROWS = 512


def _gelu(x):
    return 0.5 * x * (1.0 + lax.erf(x * (2.0 ** -0.5)))


def _cmlp_kernel(u_ref, v_ref, g_ref, b_ref, ws_ref, bs_ref, o_ref):
    u = _gelu(u_ref[...])
    gv = _gelu(v_ref[...])
    mu = jnp.mean(gv, axis=-1, keepdims=True)
    dv = gv - mu
    var = jnp.mean(dv * dv, axis=-1, keepdims=True)
    vn = (dv * lax.rsqrt(var + EPS) * g_ref[...] + b_ref[...]).astype(BF16)
    lane = _lane((C_CHUNK, LANES))
    for c in range(CM_ROWS // C_CHUNK):
        rs = slice(c * C_CHUNK, (c + 1) * C_CHUNK)
        for p in range(2):
            cs = slice(p * LANES, (p + 1) * LANES)
            vp = vn[rs, cs]
            m0 = jnp.dot(ws_ref[2 * p].astype(BF16), vp, preferred_element_type=F32)
            m1 = jnp.dot(ws_ref[2 * p + 1].astype(BF16), vp, preferred_element_type=F32)
            mixed = jnp.where(lane < C_DG, m0, m1) + bs_ref[:, cs]
            o_ref[rs, cs] = u[rs, cs] * mixed


def _chunk_mlp(proj, ln_g, ln_b, w_s, bias_full):
    ucol = (5 * A_WIDTH + 3 * B_WIDTH) // C_WIDTH
    return pl.pallas_call(
        _cmlp_kernel,
        grid=(T // CM_ROWS,),
        in_specs=[
            pl.BlockSpec((CM_ROWS, C_WIDTH), lambda i: (i, ucol)),
            pl.BlockSpec((CM_ROWS, C_WIDTH), lambda i: (i, ucol + 1)),
            pl.BlockSpec((1, C_WIDTH), lambda i: (0, 0)),
            pl.BlockSpec((1, C_WIDTH), lambda i: (0, 0)),
            pl.BlockSpec((C_GROUPS, C_CHUNK, C_CHUNK), lambda i: (0, 0, 0)),
            pl.BlockSpec((C_CHUNK, C_WIDTH), lambda i: (0, 0)),
        ],
        out_specs=pl.BlockSpec((CM_ROWS, C_WIDTH), lambda i: (i, 0)),
        out_shape=jax.ShapeDtypeStruct((T, C_WIDTH), F32),
        compiler_params=_cparams(("arbitrary",)),
        name="chunk_mlp",
    )(proj, proj, ln_g.reshape(1, C_WIDTH), ln_b.reshape(1, C_WIDTH), w_s, bias_full)


def _postmix_kernel(*refs, n_x):
    (of_ref, ob_ref, ag_ref, hg_ref, hsel_ref, bc_ref, bs_ref, c_ref, w_ref, gate1_ref, shift2_ref, scale2_ref,
     g2_ref, wrh_ref, wrl_ref, br_ref, x1_ref, h2_ref, idx_ref, gw_ref) = refs[n_x:]
    x = _stream_tile(refs[:n_x])
    o = of_ref[...] + ob_ref[...]
    ms = _dot_sel(o * o, hsel_ref[...]) * (1.0 / A_DK)
    ag = ag_ref[...]
    a = o * lax.rsqrt(ms + EPS) * hg_ref[...] * (ag * jax.nn.sigmoid(ag))
    b = jnp.where(pl.program_id(0) < CTX_TILES, bc_ref[...], bs_ref[...])
    mixed = jnp.dot(a.astype(BF16), w_ref[0:A_WIDTH, :], preferred_element_type=F32)
    mixed = mixed + jnp.dot(b.astype(BF16), w_ref[A_WIDTH:A_WIDTH + B_WIDTH, :], preferred_element_type=F32)
    mixed = mixed + jnp.dot(c_ref[...].astype(BF16), w_ref[A_WIDTH + B_WIDTH:, :], preferred_element_type=F32)
    x1 = x + gate1_ref[0] * mixed
    x1_ref[...] = x1
    y = x1 * lax.rsqrt(jnp.mean(x1 * x1, axis=-1, keepdims=True) + EPS) * g2_ref[...]
    h2 = y * (1.0 + scale2_ref[0]) + shift2_ref[0]
    h2_ref[...] = pltpu.einshape("t(jl)->tjl", h2, l=LANES)
    hi = h2.astype(BF16)
    lo = (h2 - hi.astype(F32)).astype(BF16)
    lg = jnp.dot(lo, wrh_ref[...], preferred_element_type=F32)
    lg = lg + jnp.dot(hi, wrl_ref[...], preferred_element_type=F32)
    lg = lg + jnp.dot(hi, wrh_ref[...], preferred_element_type=F32) + br_ref[...]
    lt = lg.T[:N_EXPERTS]
    row = lax.broadcasted_iota(jnp.int32, lt.shape, 0)
    out_row = lax.broadcasted_iota(jnp.int32, (8, TM), 0)
    idx_out = jnp.zeros((8, TM), jnp.int32)
    val_out = jnp.zeros((8, TM), F32)
    top0 = None
    den = None
    for kk in range(TOP_K):
        mx = jnp.max(lt, axis=0, keepdims=True)
        am = jnp.min(jnp.where(lt == mx, row, N_EXPERTS), axis=0, keepdims=True)
        if kk == 0:
            top0 = mx
        e = jnp.exp(mx - top0)
        den = e if den is None else den + e
        idx_out = jnp.where(out_row == kk, am, idx_out)
        val_out = jnp.where(out_row == kk, e, val_out)
        lt = jnp.where(row == am, -jnp.inf, lt)
    idx_ref[...] = idx_out
    gw_ref[...] = val_out / den


def _post_mix(o_f, o_b, proj, hg, hsel, b_ctx, b_smp, c_out, w_out_bf, x, mod3, g2, wr_hi, wr_lo, br_pad):
    tile = lambda w: pl.BlockSpec((TM, w), lambda i: (i, 0))
    const = lambda shape: pl.BlockSpec(shape, lambda i: tuple(0 for _ in shape))
    modspec = lambda j: pl.BlockSpec((1, 1, D), lambda i: (_mod_row(i), 0, j))
    rowsT = pl.BlockSpec((8, TM), lambda i: (0, i))
    x_specs, x_args = _stream_specs(x)
    return pl.pallas_call(
        functools.partial(_postmix_kernel, n_x=len(x_args)),
        grid=(N_TILES,),
        in_specs=x_specs + [
            tile(A_WIDTH), tile(A_WIDTH),
            pl.BlockSpec((TM, A_WIDTH), lambda i: (i, 4)),
            const((1, A_WIDTH)), const((A_WIDTH, A_WIDTH)),
            pl.BlockSpec((TM, B_WIDTH), lambda i: (jnp.minimum(i, CTX_TILES - 1), 0)),
            pl.BlockSpec((TM, B_WIDTH), lambda i: (jnp.maximum(i - CTX_TILES, 0), 0)),
            tile(C_WIDTH),
            const((D, D)),
            modspec(2), modspec(3), modspec(4),
            const((1, D)), const((D, LANES)), const((D, LANES)), const((1, LANES)),
        ],
        out_specs=[tile(D), pl.BlockSpec((TM, ROW_TILES, LANES), lambda i: (i, 0, 0)), rowsT, rowsT],
        out_shape=[
            jax.ShapeDtypeStruct((T, D), F32),
            jax.ShapeDtypeStruct((T, ROW_TILES, LANES), F32),
            jax.ShapeDtypeStruct((8, T), jnp.int32),
            jax.ShapeDtypeStruct((8, T), F32),
        ],
        compiler_params=_cparams(("arbitrary",)),
        name="post_mix_router",
    )(*x_args, o_f, o_b, proj, hg, hsel, b_ctx, b_smp, c_out, w_out_bf, mod3, mod3, mod3, g2.reshape(1, D),
      wr_hi, wr_lo, br_pad)


def _route_kernel(idx_ref, dest_ref, meta_ref):
    erow = lax.broadcasted_iota(jnp.int32, (N_EXPERTS, TM), 0)
    s_i = lax.broadcasted_iota(jnp.int32, (TM, TM), 0)
    t_i = lax.broadcasted_iota(jnp.int32, (TM, TM), 1)
    earlier = (s_i < t_i).astype(BF16)
    out_row = lax.broadcasted_iota(jnp.int32, (8, TM), 0)

    def onehots(i):
        idx = idx_ref[:, pl.ds(pl.multiple_of(i * TM, TM), TM)]
        return [(erow == idx[kk:kk + 1, :]) for kk in range(TOP_K)]

    def count_tile(i, run):
        ohs = onehots(i)
        base = run
        pos = jnp.zeros((8, TM), F32)
        for kk in range(TOP_K):
            ohf = ohs[kk].astype(F32)
            before = jnp.dot(ohs[kk].astype(BF16), earlier, preferred_element_type=F32)
            p = jnp.sum(ohf * (base + before), axis=0, keepdims=True)
            pos = jnp.where(out_row == kk, p, pos)
            base = base + jnp.sum(ohf, axis=1, keepdims=True)
        dest_ref[:, pl.ds(pl.multiple_of(i * TM, TM), TM)] = pos.astype(jnp.int32)
        return base

    counts = lax.fori_loop(0, N_TILES, count_tile, jnp.zeros((N_EXPERTS, 1), F32)).astype(jnp.int32)
    bm_shift = MOE_BM.bit_length() - 1
    padded = lax.shift_left(lax.shift_right_logical(counts + (MOE_BM - 1), bm_shift), bm_shift)
    e_r = lax.broadcasted_iota(jnp.int32, (N_EXPERTS, N_EXPERTS), 0)
    e_c = lax.broadcasted_iota(jnp.int32, (N_EXPERTS, N_EXPERTS), 1)
    incl = (e_c <= e_r).astype(BF16)
    pad_end = _sel_dot(incl, jnp.broadcast_to(padded.astype(F32), (N_EXPERTS, LANES)))[:, :1]
    pad_start = pad_end - padded.astype(F32)

    def place_tile(i, carry):
        ohs = onehots(i)
        sl = pl.ds(pl.multiple_of(i * TM, TM), TM)
        off = jnp.zeros((8, TM), F32)
        for kk in range(TOP_K):
            o = jnp.sum(ohs[kk].astype(F32) * pad_start, axis=0, keepdims=True)
            off = jnp.where(out_row == kk, o, off)
        dest_ref[:, sl] = dest_ref[:, sl] + off.astype(jnp.int32)
        return carry

    lax.fori_loop(0, N_TILES, place_tile, 0)

    total = jnp.max(pad_end, axis=0, keepdims=True)
    lane_i = lax.broadcasted_iota(jnp.int32, (1, TM), 1)
    blk0 = (lane_i * MOE_BM).astype(F32)
    block_e = jnp.sum((pad_end <= blk0).astype(F32), axis=0, keepdims=True)
    live_end = pad_start + counts.astype(F32)
    sel = erow.astype(F32) == block_e
    live = jnp.sum(jnp.where(sel, live_end, 0.0), axis=0, keepdims=True)
    valid = jnp.where(blk0 < total, jnp.clip(live - blk0, 0.0, float(MOE_BM)), 0.0)
    own = erow == lane_i
    n_blk = jnp.sum(jnp.where(own, padded.astype(F32), 0.0), axis=0, keepdims=True) * (1.0 / MOE_BM)
    first_blk = jnp.sum(jnp.where(own, pad_start, 0.0), axis=0, keepdims=True) * (1.0 / MOE_BM)
    meta = jnp.where(out_row == 0, valid, 0.0)
    meta = jnp.where(out_row == 1, n_blk, meta)
    meta = jnp.where(out_row == 2, first_blk, meta)
    meta = jnp.where(out_row == 3, total * (1.0 / MOE_BM), meta)
    meta_ref[...] = meta.astype(jnp.int32)


def _route(idx_t):
    assert MOE_BLOCKS <= TM
    return pl.pallas_call(
        _route_kernel,
        out_shape=[jax.ShapeDtypeStruct((8, T), jnp.int32), jax.ShapeDtypeStruct((8, TM), jnp.int32)],
        compiler_params=pltpu.CompilerParams(vmem_limit_bytes=VMEM_LIMIT),
        name="moe_route",
    )(idx_t)


def _moe_kernel(bv_ref, nb_ref, g0_ref, tot_ref, x_hbm, wgu_ref, bgu_ref, wdn_ref, bdn_ref, y_hbm,
                wgu_bf, wdn_bf, xbuf, ybuf, xb_ref, xsem, ysem):
    e = pl.program_id(0)
    n_blk = nb_ref[e]
    first = g0_ref[e]
    total = tot_ref[0]
    ahead = MOE_RING - 1

    def x_copy(g):
        slot = g % MOE_RING
        return pltpu.make_async_copy(x_hbm.at[pl.ds(g * MOE_BM, MOE_BM)], xbuf.at[slot], xsem.at[slot])

    def y_copy(g):
        slot = g % MOE_RING
        return pltpu.make_async_copy(ybuf.at[slot], y_hbm.at[pl.ds(g * MOE_BM, MOE_BM)], ysem.at[slot])

    @pl.when(e == 0)
    def _():
        for g in range(ahead):
            @pl.when(g < total)
            def _():
                x_copy(g).start()

    @pl.when(n_blk > 0)
    def _():
        wgu_bf[...] = wgu_ref[0, 0].astype(BF16)
        wdn_bf[...] = wdn_ref[0, 0].astype(BF16)

    def block(j, carry):
        g = first + j
        slot = g % MOE_RING
        x_copy(g).wait()

        @pl.when(g + ahead < total)
        def _():
            x_copy(g + ahead).start()

        @pl.when(g >= MOE_RING)
        def _():
            y_copy(g - MOE_RING).wait()

        live = lax.broadcasted_iota(jnp.int32, (MOE_BM, LANES), 0) < bv_ref[g]
        xt = pltpu.einshape("tjl->jtl", xbuf[slot])
        for jj in range(ROW_TILES):
            xb_ref[:, jj * LANES:(jj + 1) * LANES] = jnp.where(live, xt[jj], 0.0).astype(BF16)
        gu = jnp.dot(xb_ref[...], wgu_bf[...], preferred_element_type=F32) + bgu_ref[0, 0]
        glu = jnp.minimum(gu[:, :D], SWIGLU_LIMIT)
        lin = jnp.clip(gu[:, D:], -SWIGLU_LIMIT, SWIGLU_LIMIT)
        act = glu * jax.nn.sigmoid(SWIGLU_ALPHA * glu) * (lin + 1.0)
        y = jnp.dot(act.astype(BF16), wdn_bf[...], preferred_element_type=F32) + bdn_ref[0, 0]
        ybuf[slot] = pltpu.einshape("t(jl)->tjl", y, l=LANES)
        y_copy(g).start()
        return carry

    lax.fori_loop(0, n_blk, block, 0)

    @pl.when(e == N_EXPERTS - 1)
    def _():
        for back in range(MOE_RING, 0, -1):
            @pl.when(total >= back)
            def _():
                y_copy(total - back).wait()

        def fill(g, carry):
            ybuf[g % MOE_RING] = jnp.zeros((MOE_BM, ROW_TILES, LANES), F32)
            y_copy(g).start()
            y_copy(g).wait()
            return carry

        lax.fori_loop(total, MOE_BLOCKS, fill, 0)


def _moe_ffn(block_valid, n_blk, first_blk, total_blk, xs, w_gu, b_gu, w_dn, b_dn, layer):
    rows = (MOE_BM, ROW_TILES, LANES)
    return pl.pallas_call(
        _moe_kernel,
        grid_spec=pltpu.PrefetchScalarGridSpec(
            num_scalar_prefetch=4,
            grid=(N_EXPERTS,),
            in_specs=[
                pl.BlockSpec(memory_space=pl.ANY),
                pl.BlockSpec((1, 1, D, 2 * D), lambda e, *_: (layer, e, 0, 0)),
                pl.BlockSpec((1, 1, 1, 2 * D), lambda e, *_: (layer, e, 0, 0)),
                pl.BlockSpec((1, 1, D, D), lambda e, *_: (layer, e, 0, 0)),
                pl.BlockSpec((1, 1, 1, D), lambda e, *_: (layer, e, 0, 0)),
            ],
            out_specs=pl.BlockSpec(memory_space=pl.ANY),
            scratch_shapes=[
                pltpu.VMEM((D, 2 * D), BF16), pltpu.VMEM((D, D), BF16),
                pltpu.VMEM((MOE_RING,) + rows, F32), pltpu.VMEM((MOE_RING,) + rows, F32),
                pltpu.VMEM((MOE_BM, D), BF16),
                pltpu.SemaphoreType.DMA((MOE_RING,)), pltpu.SemaphoreType.DMA((MOE_RING,)),
            ],
        ),
        out_shape=jax.ShapeDtypeStruct((MOE_ROWS, ROW_TILES, LANES), F32),
        compiler_params=_cparams(("arbitrary",)),
        name="moe_expert_ffn",
    )(block_valid, n_blk, first_blk, total_blk, xs, w_gu, b_gu.reshape(DEPTH, N_EXPERTS, 1, 2 * D), w_dn,
      b_dn.reshape(DEPTH, N_EXPERTS, 1, D))


def _combine_kernel(x1_ref, y_ref, gw_ref, gate2_ref, o_ref):
    gw = jnp.concatenate([gw_ref[...], jnp.zeros((LANES - 8, TM), F32)], axis=0).T
    ys = [pltpu.einshape("tjl->jtl", y_ref[kk]) for kk in range(TOP_K)]
    for j in range(ROW_TILES):
        cs = slice(j * LANES, (j + 1) * LANES)
        acc = None
        for kk in range(TOP_K):
            term = ys[kk][j] * gw[:, kk:kk + 1]
            acc = term if acc is None else acc + term
        o_ref[:, cs] = x1_ref[:, cs] + gate2_ref[0, :, cs] * acc


def _combine(x1, yg, gw, mod3):
    return pl.pallas_call(
        _combine_kernel,
        grid=(N_TILES,),
        in_specs=[
            pl.BlockSpec((TM, D), lambda i: (i, 0)),
            pl.BlockSpec((TOP_K, TM, ROW_TILES, LANES), lambda i: (0, i, 0, 0)),
            pl.BlockSpec((8, TM), lambda i: (0, i)),
            pl.BlockSpec((1, 1, D), lambda i: (_mod_row(i), 0, 5)),
        ],
        out_specs=pl.BlockSpec((TM, D), lambda i: (i, 0)),
        out_shape=jax.ShapeDtypeStruct((T, D), F32),
        compiler_params=_cparams(("arbitrary",)),
        name="moe_combine",
    )(x1, yg, gw, mod3)


def _sc_mesh():
    return plsc.VectorSubcoreMesh(core_axis_name="c", subcore_axis_name="s")


def _sc_worker():
    return lax.axis_index("s") * SC_CORES + lax.axis_index("c")


def _sc_dispatch(h2t, dest_km):
    per_w = T // SC_WORKERS

    @functools.partial(
        pl.kernel, mesh=_sc_mesh(),
        out_type=jax.ShapeDtypeStruct((MOE_ROWS, ROW_TILES, LANES), F32),
        scratch_types=[pltpu.VMEM((SC_WIN,), jnp.int32), pltpu.VMEM((SC_WIN, ROW_TILES, LANES), F32),
                       pltpu.SemaphoreType.DMA],
    )
    def run(h_hbm, d_hbm, o_hbm, idx_v, rows_v, sem):
        w0 = _sc_worker() * per_w

        @pl.loop(0, per_w // SC_WIN)
        def _(w):
            base = pl.multiple_of(w0 + w * SC_WIN, SC_WIN)
            pltpu.sync_copy(h_hbm.at[pl.ds(base, SC_WIN)], rows_v)
            for kk in range(TOP_K):
                pltpu.sync_copy(d_hbm.at[pl.ds(kk * T + base, SC_WIN)], idx_v)
                pltpu.async_copy(rows_v, o_hbm.at[idx_v], sem).wait()

    return run(h2t, dest_km)


def _sc_gather(yb, dest_km):
    n = TOP_K * T
    per_w = n // SC_WORKERS

    @functools.partial(
        pl.kernel, mesh=_sc_mesh(),
        out_type=jax.ShapeDtypeStruct((n, ROW_TILES, LANES), F32),
        scratch_types=[pltpu.VMEM((SC_WIN,), jnp.int32), pltpu.VMEM((SC_WIN, ROW_TILES, LANES), F32),
                       pltpu.SemaphoreType.DMA],
    )
    def run(y_hbm, d_hbm, o_hbm, idx_v, rows_v, sem):
        w0 = _sc_worker() * per_w

        @pl.loop(0, per_w // SC_WIN)
        def _(w):
            base = pl.multiple_of(w0 + w * SC_WIN, SC_WIN)
            pltpu.sync_copy(d_hbm.at[pl.ds(base, SC_WIN)], idx_v)
            pltpu.async_copy(y_hbm.at[idx_v], rows_v, sem).wait()
            pltpu.sync_copy(rows_v, o_hbm.at[pl.ds(base, SC_WIN)])

    return run(yb, dest_km)


def kernel(x_prompt, x_sample, c, cache_diff_k, cache_diff_v, state_hgrn, c_ctx, norm_mix_g, norm_ffn_g, w_mod, b_mod, w_in, w_out, hgrn_lower_bounds, hgrn_norm_g, diff_q_norm_g, diff_k_norm_g, diff_lambda_q1, diff_lambda_k1, diff_lambda_q2, diff_lambda_k2, diff_subln_g, cmlp_ln_g, cmlp_ln_b, cmlp_w_s, cmlp_b_s, router_w, router_b, moe_w_gate_up, moe_b_gate_up, moe_w_down, moe_b_down):
    x = (x_prompt.reshape(T_CTX, D), x_sample.reshape(T_SMP, D))
    cvec = jnp.concatenate([c_ctx[None, :], c, jnp.zeros((MOD_ROWS - 1 - DEC_BATCH, D), F32)], axis=0)
    mod = _modulation(cvec, w_mod, b_mod)

    lvl_np, tri_np = _hgrn_tables()
    lvl = jnp.asarray(lvl_np)
    tri = jnp.asarray(tri_np, dtype=BF16)
    cos, sin = _rope_tables()
    hsel = jnp.asarray(np.kron(np.eye(A_HEADS), np.ones((A_DK, A_DK))), dtype=BF16)
    sm = jax.nn.softmax(hgrn_lower_bounds.astype(F32), axis=0)
    lb_all = jnp.cumsum(sm, axis=0) - sm[0]

    new_k, new_v, new_s = None, None, []
    for l in range(DEPTH):
        mod3 = mod[l].reshape(MOD_ROWS, 1, 6 * D)
        proj = _in_projection(x, norm_mix_g[l], mod3, w_in[l].astype(BF16))

        s0 = jnp.concatenate([jnp.zeros((BATCH, 2, A_HEADS, A_DK, A_DK), F32), state_hgrn[:, l]], axis=0)
        o_dir, fin_dir = [], []
        for d in range(2):
            o_d, fin_d = _hgrn_scan(proj, lb_all[l, d].reshape(1, A_WIDTH), _pack_state(s0[:, d]),
                                    lvl, tri, l, d == 1)
            o_dir.append(o_d)
            fin_dir.append(_unpack_state(fin_d[:BATCH]))
        new_s.append(jnp.stack(fin_dir, axis=1))

        lam_init = 0.8 - 0.6 * math.exp(-0.3 * l)
        lam = (jnp.exp(jnp.sum(diff_lambda_q1[l] * diff_lambda_k1[l]))
               - jnp.exp(jnp.sum(diff_lambda_q2[l] * diff_lambda_k2[l])) + lam_init).reshape(1, 1)
        gq2 = jnp.tile(diff_q_norm_g[l], 2).reshape(1, LANES)
        gk2 = jnp.tile(diff_k_norm_g[l], 2).reshape(1, LANES)
        gs = diff_subln_g[l].reshape(1, LANES)
        b_ctx, new_k, new_v = _attn_ctx(proj, lam, gq2, gk2, gs, lam_init, l, new_k, new_v)
        b_smp = _attn_smp(proj, lam, cache_diff_k, cache_diff_v, cos, sin, gq2, gk2, gs, l, lam_init)

        bias_full = jnp.repeat(cmlp_b_s[l].T, C_DG, axis=1)
        c_out = _chunk_mlp(proj, cmlp_ln_g[l], cmlp_ln_b[l], cmlp_w_s[l], bias_full)

        hg = jnp.tile(hgrn_norm_g[l], A_HEADS).reshape(1, A_WIDTH)
        wr_pad = jnp.pad(router_w[l], ((0, 0), (0, LANES - N_EXPERTS)))
        wr_hi = wr_pad.astype(BF16)
        wr_lo = (wr_pad - wr_hi.astype(F32)).astype(BF16)
        br_pad = jnp.pad(router_b[l], (0, LANES - N_EXPERTS)).reshape(1, LANES)
        x1, h2, idx_t, gw_t = _post_mix(o_dir[0], o_dir[1], proj, hg, hsel, b_ctx, b_smp, c_out,
                                        w_out[l].astype(BF16), x, mod3, norm_ffn_g[l], wr_hi, wr_lo, br_pad)

        dest_t, meta = _route(idx_t)
        dest_km = dest_t[:TOP_K].reshape(-1)
        xs = _sc_dispatch(h2, dest_km)
        yb = _moe_ffn(meta[0, :MOE_BLOCKS], meta[1, :N_EXPERTS], meta[2, :N_EXPERTS], meta[3, :1], xs,
                      moe_w_gate_up, moe_b_gate_up, moe_w_down, moe_b_down, l)
        yg = _sc_gather(yb, dest_km).reshape(TOP_K, T, ROW_TILES, LANES)
        x = _combine(x1, yg, gw_t, mod3)

    y_prompt = x[:T_CTX].reshape(BATCH, SEQ, D)
    y_sample = x[T_CTX:].reshape(DEC_BATCH, DEC_SEQ, D)
    return (y_prompt, y_sample, new_k, new_v, jnp.stack(new_s, axis=1))
```

```python
import functools
import math

import numpy as np
import jax
import jax.numpy as jnp
from jax import lax
from jax.experimental import pallas as pl
from jax.experimental.pallas import tpu as pltpu
from jax.experimental.pallas import tpu_sc as plsc

F32 = jnp.float32
BF16 = jnp.bfloat16

D = 1024
DEPTH = 2
BATCH, SEQ = 16, 256
DEC_BATCH, DEC_SEQ = 8, 1024
PAST = 512
GRID_W = 64
A_HEADS, A_DK = 4, 64
A_WIDTH = 256
B_HEADS, B_DK, B_DV = 4, 64, 128
B_WIDTH = 512
C_GROUPS, C_CHUNK, C_WIDTH, C_DG = 4, 128, 256, 64
IN_WIDTH = 5 * A_WIDTH + 3 * B_WIDTH + 2 * C_WIDTH
N_EXPERTS, TOP_K = 32, 4
SWIGLU_LIMIT, SWIGLU_ALPHA = 7.0, 1.702
ROPE_BASE = 10000.0
EPS = 1e-6

T_CTX = BATCH * SEQ
T_SMP = DEC_BATCH * DEC_SEQ
T = T_CTX + T_SMP
N_SEQ = BATCH + DEC_BATCH
MOD_ROWS = 16

TM = 256
N_TILES = T // TM
CTX_TILES = T_CTX // TM
SMP_TILES_PER_SEQ = DEC_SEQ // TM
LANES = 128
MOE_BM = 256
MOE_ROWS = T * TOP_K + N_EXPERTS * MOE_BM
MOE_BLOCKS = MOE_ROWS // MOE_BM
MOE_RING = 4
ROW_WORDS = D // 2
ROW_TILES = ROW_WORDS // LANES
ROW_DT = jnp.int32
SC_CORES, SC_SUBCORES = 2, 16
SC_WORKERS = SC_CORES * SC_SUBCORES
SC_WIN = 128
VMEM_LIMIT = 56 * 1024 * 1024


def _cparams(sem):
    return pltpu.CompilerParams(dimension_semantics=sem, vmem_limit_bytes=VMEM_LIMIT)


def _mod_row(i):
    return jnp.where(i < CTX_TILES, 0, 1 + (i - CTX_TILES) // SMP_TILES_PER_SEQ)


def _split3(x):
    hi = x.astype(BF16)
    r = x - hi.astype(F32)
    mid = r.astype(BF16)
    lo = (r - mid.astype(F32)).astype(BF16)
    return hi, mid, lo


def _sel_dot(sel, x):
    hi, mid, lo = _split3(x)
    acc = jnp.dot(sel, lo, preferred_element_type=F32)
    acc = acc + jnp.dot(sel, mid, preferred_element_type=F32)
    return acc + jnp.dot(sel, hi, preferred_element_type=F32)


def _dot_sel(x, sel):
    hi, mid, lo = _split3(x)
    acc = jnp.dot(lo, sel, preferred_element_type=F32)
    acc = acc + jnp.dot(mid, sel, preferred_element_type=F32)
    return acc + jnp.dot(hi, sel, preferred_element_type=F32)


def _dot_nt(a, b):
    return lax.dot_general(a, b, (((1,), (1,)), ((), ())), preferred_element_type=F32)


def _dot_tn(a, b):
    return lax.dot_general(a, b, (((0,), (0,)), ((), ())), preferred_element_type=F32)


def _lane(shape):
    return lax.broadcasted_iota(jnp.int32, shape, len(shape) - 1)


def _pack_rows(x):
    hi = lax.bitcast_convert_type(x[:, :ROW_WORDS].astype(BF16).astype(F32), jnp.int32)
    lo = lax.bitcast_convert_type(x[:, ROW_WORDS:].astype(BF16).astype(F32), jnp.int32)
    words = hi | lax.shift_right_logical(lo, 16)
    return pltpu.einshape("t(jl)->tjl", words, l=LANES)


def _unpack_rows(words3):
    wt = pltpu.einshape("tjl->jtl", words3)
    hi = [lax.bitcast_convert_type(wt[j] & jnp.int32(-65536), F32) for j in range(ROW_TILES)]
    lo = [lax.bitcast_convert_type(lax.shift_left(wt[j], 16), F32) for j in range(ROW_TILES)]
    return hi + lo


def _mod_kernel(c_ref, w_ref, b_ref, o_ref):
    c = c_ref[...]
    s = c * jax.nn.sigmoid(c)
    o_ref[0] = jnp.dot(s.astype(BF16), w_ref[0].astype(BF16), preferred_element_type=F32) + b_ref[0]


def _modulation(cvec, w_mod, b_mod):
    tn = 1536
    return pl.pallas_call(
        _mod_kernel,
        grid=(DEPTH, 6 * D // tn),
        in_specs=[
            pl.BlockSpec((MOD_ROWS, D), lambda l, j: (0, 0)),
            pl.BlockSpec((1, D, tn), lambda l, j: (l, 0, j)),
            pl.BlockSpec((1, 1, tn), lambda l, j: (l, 0, j)),
        ],
        out_specs=pl.BlockSpec((1, MOD_ROWS, tn), lambda l, j: (l, 0, j)),
        out_shape=jax.ShapeDtypeStruct((DEPTH, MOD_ROWS, 6 * D), F32),
        compiler_params=_cparams(("arbitrary", "arbitrary")),
        name="modulation",
    )(cvec, w_mod, b_mod.reshape(DEPTH, 1, 6 * D))


def _stream_specs(x):
    if isinstance(x, tuple):
        return [pl.BlockSpec((TM, D), lambda i: (jnp.minimum(i, CTX_TILES - 1), 0)),
                pl.BlockSpec((TM, D), lambda i: (jnp.maximum(i - CTX_TILES, 0), 0))], list(x)
    return [pl.BlockSpec((TM, D), lambda i: (i, 0))], [x]


def _stream_tile(x_refs):
    if len(x_refs) == 1:
        return x_refs[0][...]
    return jnp.where(pl.program_id(0) < CTX_TILES, x_refs[0][...], x_refs[1][...])


def _inproj_kernel(*refs, n_x):
    g_ref, shift_ref, scale_ref, w_ref, o_ref = refs[n_x:]
    x = _stream_tile(refs[:n_x])
    y = x * lax.rsqrt(jnp.mean(x * x, axis=-1, keepdims=True) + EPS) * g_ref[...]
    h = y * (1.0 + scale_ref[0]) + shift_ref[0]
    o_ref[...] = jnp.dot(h.astype(BF16), w_ref[...], preferred_element_type=F32)


def _in_projection(x, g, mod3, w_in_bf):
    x_specs, x_args = _stream_specs(x)
    return pl.pallas_call(
        functools.partial(_inproj_kernel, n_x=len(x_args)),
        grid=(N_TILES,),
        in_specs=x_specs + [
            pl.BlockSpec((1, D), lambda i: (0, 0)),
            pl.BlockSpec((1, 1, D), lambda i: (_mod_row(i), 0, 0)),
            pl.BlockSpec((1, 1, D), lambda i: (_mod_row(i), 0, 1)),
            pl.BlockSpec((D, IN_WIDTH), lambda i: (0, 0)),
        ],
        out_specs=pl.BlockSpec((TM, IN_WIDTH), lambda i: (i, 0)),
        out_shape=jax.ShapeDtypeStruct((T, IN_WIDTH), F32),
        compiler_params=_cparams(("arbitrary",)),
        name="in_projection",
    )(*x_args, g.reshape(1, D), mod3, mod3, w_in_bf)


HG_C = 128
HG_LEVELS = tuple(2 ** j for j in range(1, int(math.log2(HG_C)) + 1))


def _hgrn_tables():
    t = np.arange(HG_C)[:, None]
    s = np.arange(HG_C)[None, :]
    x = t ^ s
    lvl = np.zeros((HG_C, HG_C), np.int32)
    nz = x > 0
    lvl[nz] = np.floor(np.log2(x[nz])).astype(np.int32) + 1
    fwd = np.where(t >= s, lvl, -1).astype(np.int32)
    bwd = np.where(t <= s, lvl, -1).astype(np.int32)
    tri_f = (t >= s).astype(np.float32)
    tri_b = (t <= s).astype(np.float32)
    return np.stack([fwd, bwd]), np.stack([tri_f, tri_b])


def _block_ref(cum, m, idx):
    c, l = cum.shape
    if m >= 16:
        c3 = cum.reshape(c // m, m, l)
        r = c3[:, idx:idx + 1, :]
        return jnp.broadcast_to(r, (c // m, m, l)).reshape(c, l)
    c3 = cum.reshape(c // 8, 8, l)
    sub = lax.broadcasted_iota(jnp.int32, c3.shape, 1)
    out = None
    for j in range(8 // m - 1, -1, -1):
        cand = jnp.broadcast_to(c3[:, j * m + idx:j * m + idx + 1, :], c3.shape)
        out = cand if out is None else jnp.where(sub < (j + 1) * m, cand, out)
    return out.reshape(c, l)


def _hgrn_kernel(q_ref, z_ref, v_ref, lb_ref, s0_ref, lvl_ref, tri_ref, o_ref, fin_ref, st_ref, *, layer, rev):
    g = pl.program_id(0)
    first = jnp.logical_or(g < CTX_TILES, (g - CTX_TILES) % SMP_TILES_PER_SEQ == 0)

    @pl.when(first)
    def _():
        st_ref[...] = s0_ref[0]

    qr = q_ref[...]
    q = qr * jax.nn.sigmoid(qr) * (A_DK ** -0.5)
    z = z_ref[...]
    if layer == 0:
        lf = jnp.minimum(z, 0.0) - jnp.log(1.0 + jnp.exp(-jnp.abs(z)))
        k = jax.nn.sigmoid(-z)
    else:
        lbd = lb_ref[...]
        lf = jnp.log(lbd + (1.0 - lbd) * jax.nn.sigmoid(z))
        k = (1.0 - lbd) * jax.nn.sigmoid(-z)
    v = v_ref[...]
    tri = tri_ref[0]
    lvl = lvl_ref[0]
    last_row = 0 if rev else HG_C - 1
    n_chunks = TM // HG_C
    order = range(n_chunks - 1, -1, -1) if rev else range(n_chunks)
    cums = [_sel_dot(tri, lf[c * HG_C:(c + 1) * HG_C]) for c in range(n_chunks)]
    lane = _lane((HG_C, LANES))
    head_masks = (lane < A_DK, lane >= A_DK)
    lane_row = _lane((1, LANES))
    head_keep = ((lane_row < A_DK).astype(BF16), (lane_row >= A_DK).astype(BF16))
    lvl2 = jnp.concatenate([lvl, lvl], axis=0)
    level_masks = [lvl2 == i for i in range(len(HG_LEVELS) + 1)]
    r = lax.broadcasted_iota(jnp.int32, (LANES, LANES), 0)
    cl = lax.broadcasted_iota(jnp.int32, (LANES, LANES), 1)
    same_head = (r < A_DK) == (cl < A_DK)

    def chunk(q_p, k_p, v_p, cum_p, st):
        v_bf = v_p.astype(BF16)
        k_bf = k_p.astype(BF16)
        q_bf = q_p.astype(BF16)

        def both_heads(x_bf):
            return jnp.concatenate([x_bf * head_keep[0], x_bf * head_keep[1]], axis=0)

        scores = jnp.where(level_masks[0], _dot_nt(both_heads(q_bf), k_bf), 0.0)
        for li, m in enumerate(HG_LEVELS):
            ref = _block_ref(cum_p, m, m // 2 if rev else m // 2 - 1)
            dec = jnp.exp(-jnp.abs(cum_p - ref))
            qd = (q_p * dec).astype(BF16)
            kd = (k_p * dec).astype(BF16)
            scores = jnp.where(level_masks[li + 1], _dot_nt(both_heads(qd), kd), scores)
        pv = jnp.dot(scores.astype(BF16), v_bf, preferred_element_type=F32)
        o_intra = jnp.where(head_masks[0], pv[:HG_C], pv[HG_C:])
        q0 = (q_p * jnp.exp(cum_p)).astype(BF16)
        out = o_intra + _dot_nt(q0, st.astype(BF16))
        last = cum_p[last_row:last_row + 1, :]
        ks = (k_p * jnp.exp(last - cum_p)).astype(BF16)
        upd = _dot_tn(v_bf, ks)
        return out, st * jnp.exp(last) + jnp.where(same_head, upd, 0.0)

    for p in range(2):
        sl = slice(p * LANES, (p + 1) * LANES)
        st = st_ref[p]
        for c in order:
            rows = slice(c * HG_C, (c + 1) * HG_C)
            o_ref[rows, sl], st = chunk(q[rows, sl], k[rows, sl], v[rows, sl], cums[c][:, sl], st)
        st_ref[p] = st
        fin_ref[0, p] = st


def _hgrn_seq(g):
    return jnp.where(g < CTX_TILES, g, CTX_TILES + (g - CTX_TILES) // SMP_TILES_PER_SEQ)


def _hgrn_blk(g, rev):
    if not rev:
        return g
    j = g - CTX_TILES
    return jnp.where(g < CTX_TILES, g,
                     CTX_TILES + (j // SMP_TILES_PER_SEQ) * SMP_TILES_PER_SEQ
                     + (SMP_TILES_PER_SEQ - 1 - j % SMP_TILES_PER_SEQ))


def _hgrn_scan(proj, lb_dir, s0_dir, lvl, tri, layer, rev):
    d = 1 if rev else 0
    blk = functools.partial(_hgrn_blk, rev=rev)
    return pl.pallas_call(
        functools.partial(_hgrn_kernel, layer=layer, rev=rev),
        grid=(N_TILES,),
        in_specs=[
            pl.BlockSpec((TM, A_WIDTH), lambda g: (blk(g), 0)),
            pl.BlockSpec((TM, A_WIDTH), lambda g: (blk(g), 1 + d)),
            pl.BlockSpec((TM, A_WIDTH), lambda g: (blk(g), 3)),
            pl.BlockSpec((1, A_WIDTH), lambda g: (0, 0)),
            pl.BlockSpec((1, 2, LANES, LANES), lambda g: (_hgrn_seq(g), 0, 0, 0)),
            pl.BlockSpec((1, HG_C, HG_C), lambda g: (d, 0, 0)),
            pl.BlockSpec((1, HG_C, HG_C), lambda g: (d, 0, 0)),
        ],
        out_specs=[
            pl.BlockSpec((TM, A_WIDTH), lambda g: (blk(g), 0)),
            pl.BlockSpec((1, 2, LANES, LANES), lambda g: (_hgrn_seq(g), 0, 0, 0)),
        ],
        out_shape=[
            jax.ShapeDtypeStruct((T, A_WIDTH), F32),
            jax.ShapeDtypeStruct((N_SEQ, 2, LANES, LANES), F32),
        ],
        scratch_shapes=[pltpu.VMEM((2, LANES, LANES), F32)],
        compiler_params=_cparams(("arbitrary",)),
        name=f"hgrn_scan_{'bwd' if rev else 'fwd'}",
    )(proj, proj, proj, lb_dir, s0_dir, lvl, tri)


def _pack_state(s):
    n = s.shape[0]
    st = jnp.swapaxes(s, -1, -2).reshape(n, 2, 2, A_DK, A_DK)
    z = jnp.zeros_like(st[:, :, 0])
    top = jnp.concatenate([st[:, :, 0], z], axis=-1)
    bot = jnp.concatenate([z, st[:, :, 1]], axis=-1)
    return jnp.concatenate([top, bot], axis=-2)


def _unpack_state(sp):
    n = sp.shape[0]
    h0 = sp[:, :, :A_DK, :A_DK]
    h1 = sp[:, :, A_DK:, A_DK:]
    st = jnp.stack([h0, h1], axis=2).reshape(n, A_HEADS, A_DK, A_DK)
    return jnp.swapaxes(st, -1, -2)


def _half_rms(x, g):
    r = lax.broadcasted_iota(jnp.int32, (LANES, LANES), 0)
    c = lax.broadcasted_iota(jnp.int32, (LANES, LANES), 1)
    half_mean = jnp.where((r < B_DK) == (c < B_DK), 1.0 / B_DK, 0.0).astype(BF16)
    xx = x * x
    hi = xx.astype(BF16)
    lo = (xx - hi.astype(F32)).astype(BF16)
    ms = jnp.dot(lo, half_mean, preferred_element_type=F32) + jnp.dot(hi, half_mean, preferred_element_type=F32)
    return x * lax.rsqrt(ms + EPS) * g


def _rope(x, cos, sin_signed):
    lane = _lane(x.shape)
    first = (lane % 32) < 16
    rot = jnp.where(first, pltpu.roll(x, LANES - 16, 1), pltpu.roll(x, 16, 1))
    return x * cos + rot * sin_signed


def _with_ones(v_bf):
    return jnp.concatenate([v_bf, jnp.ones_like(v_bf)], axis=-1)


def _diff_softmax_pv(q_bf, keys_bf, vals_ext, lam):
    lane = _lane(q_bf.shape)
    zero = jnp.zeros_like(q_bf)
    outs = []
    for mp in range(2):
        qm = jnp.where((lane < B_DK) == (mp == 0), q_bf, zero)
        s = [_dot_nt(qm, kk) for kk in keys_bf]
        mx = functools.reduce(jnp.maximum, [jnp.max(si, axis=-1, keepdims=True) for si in s])
        acc = None
        for si, ve in zip(s, vals_ext):
            e = jnp.exp((si - mx).astype(BF16))
            pv = jnp.dot(e, ve, preferred_element_type=F32)
            acc = pv if acc is None else acc + pv
        outs.append(acc[:, :B_DV] / acc[:, B_DV:])
    return outs[0] - lam * outs[1]


def _subln(o, g, lam_init):
    return o * lax.rsqrt(jnp.mean(o * o, axis=-1, keepdims=True) + EPS) * g * (1.0 - lam_init)


def _attn_ctx_kernel(lam_ref, q_ref, k_ref, v_ref, gq_ref, gk_ref, gs_ref, *rest, lam_init, layer):
    if layer:
        pk_ref, pv_ref, o_ref, nk_ref, nv_ref = rest
        nk_ref[0, :layer] = pk_ref[0]
        nv_ref[0, :layer] = pv_ref[0]
    else:
        o_ref, nk_ref, nv_ref = rest
    lam = lam_ref[0, 0]
    qn = _half_rms(q_ref[...], gq_ref[...]) * (B_DK ** -0.5)
    kn = _half_rms(k_ref[...], gk_ref[...])
    v = v_ref[...]
    nk_ref[0, layer, 0, 0] = kn[:, :B_DK]
    nk_ref[0, layer, 1, 0] = kn[:, B_DK:]
    nv_ref[0, layer, 0] = v
    o = _diff_softmax_pv(qn.astype(BF16), [kn.astype(BF16)], [_with_ones(v.astype(BF16))], lam)
    o_ref[...] = _subln(o, gs_ref[...], lam_init)


def _attn_ctx(proj, lam, gq2, gk2, gs, lam_init, layer, prev_k, prev_v):
    qcol, kcol, vcol = 5 * A_WIDTH // LANES, 5 * A_WIDTH // LANES + 4, 5 * A_WIDTH // LANES + 8
    prev_specs, prev_args = [], []
    if layer:
        prev_specs = [pl.BlockSpec((1, layer, 2, 1, SEQ, B_DK), lambda b, h: (b, 0, 0, h, 0, 0)),
                      pl.BlockSpec((1, layer, 1, SEQ, B_DV), lambda b, h: (b, 0, h, 0, 0))]
        prev_args = [prev_k, prev_v]
    n_l = layer + 1
    return pl.pallas_call(
        functools.partial(_attn_ctx_kernel, lam_init=lam_init, layer=layer),
        grid=(BATCH, B_HEADS),
        in_specs=[
            pl.BlockSpec(memory_space=pltpu.SMEM),
            pl.BlockSpec((SEQ, LANES), lambda b, h: (b, qcol + h)),
            pl.BlockSpec((SEQ, LANES), lambda b, h: (b, kcol + h)),
            pl.BlockSpec((SEQ, LANES), lambda b, h: (b, vcol + h)),
            pl.BlockSpec((1, LANES), lambda b, h: (0, 0)),
            pl.BlockSpec((1, LANES), lambda b, h: (0, 0)),
            pl.BlockSpec((1, LANES), lambda b, h: (0, 0)),
        ] + prev_specs,
        out_specs=[
            pl.BlockSpec((SEQ, LANES), lambda b, h: (b, h)),
            pl.BlockSpec((1, n_l, 2, 1, SEQ, B_DK), lambda b, h: (b, 0, 0, h, 0, 0)),
            pl.BlockSpec((1, n_l, 1, SEQ, B_DV), lambda b, h: (b, 0, h, 0, 0)),
        ],
        out_shape=[
            jax.ShapeDtypeStruct((T_CTX, B_WIDTH), F32),
            jax.ShapeDtypeStruct((BATCH, n_l, 2, B_HEADS, SEQ, B_DK), F32),
            jax.ShapeDtypeStruct((BATCH, n_l, B_HEADS, SEQ, B_DV), F32),
        ],
        compiler_params=_cparams(("arbitrary", "arbitrary")),
        name="diff_attention_ctx",
    )(lam, proj, proj, proj, gq2, gk2, gs, *prev_args)


ATT_TQ = 256


def _attn_smp_kernel(lam_ref, q_ref, k_ref, v_ref, ck_ref, cv_ref, cos_ref, sin_ref, gq_ref, gk_ref, gs_ref,
                     o_ref, qs_ref, ks_ref, *, lam_init):
    lam = lam_ref[0, 0]
    cos = cos_ref[...]
    sin = sin_ref[...]
    qn = _rope(_half_rms(q_ref[...], gq_ref[...]), cos, sin) * (B_DK ** -0.5)
    qs_ref[...] = qn.astype(BF16)
    ks_ref[...] = _rope(_half_rms(k_ref[...], gk_ref[...]), cos, sin).astype(BF16)
    ck = jnp.concatenate([ck_ref[0, 0, 0, 0], ck_ref[0, 0, 1, 0]], axis=-1).astype(BF16)
    cv = _with_ones(cv_ref[0, 0, 0].astype(BF16))
    v_bf = _with_ones(v_ref[...].astype(BF16))
    k_bf = ks_ref[...]
    g = gs_ref[...]

    for i in range(DEC_SEQ // ATT_TQ):
        rows = slice(i * ATT_TQ, (i + 1) * ATT_TQ)
        o = _diff_softmax_pv(qs_ref[rows, :], [k_bf, ck], [v_bf, cv], lam)
        o_ref[rows, :] = _subln(o, g, lam_init)


def _attn_smp(proj, lam, cache_k, cache_v, cos, sin, gq2, gk2, gs, layer, lam_init):
    qcol, kcol, vcol = 5 * A_WIDTH // LANES, 5 * A_WIDTH // LANES + 4, 5 * A_WIDTH // LANES + 8
    r0 = T_CTX // DEC_SEQ
    return pl.pallas_call(
        functools.partial(_attn_smp_kernel, lam_init=lam_init),
        grid=(DEC_BATCH, B_HEADS),
        in_specs=[
            pl.BlockSpec(memory_space=pltpu.SMEM),
            pl.BlockSpec((DEC_SEQ, LANES), lambda b, h: (r0 + b, qcol + h)),
            pl.BlockSpec((DEC_SEQ, LANES), lambda b, h: (r0 + b, kcol + h)),
            pl.BlockSpec((DEC_SEQ, LANES), lambda b, h: (r0 + b, vcol + h)),
            pl.BlockSpec((1, 1, 2, 1, PAST, B_DK), lambda b, h: (b, layer, 0, h, 0, 0)),
            pl.BlockSpec((1, 1, 1, PAST, B_DV), lambda b, h: (b, layer, h, 0, 0)),
            pl.BlockSpec((DEC_SEQ, LANES), lambda b, h: (0, 0)),
            pl.BlockSpec((DEC_SEQ, LANES), lambda b, h: (0, 0)),
            pl.BlockSpec((1, LANES), lambda b, h: (0, 0)),
            pl.BlockSpec((1, LANES), lambda b, h: (0, 0)),
            pl.BlockSpec((1, LANES), lambda b, h: (0, 0)),
        ],
        out_specs=pl.BlockSpec((DEC_SEQ, LANES), lambda b, h: (b, h)),
        out_shape=jax.ShapeDtypeStruct((T_SMP, B_WIDTH), F32),
        scratch_shapes=[pltpu.VMEM((DEC_SEQ, LANES), BF16), pltpu.VMEM((DEC_SEQ, LANES), BF16)],
        compiler_params=_cparams(("arbitrary", "arbitrary")),
        name="diff_attention_smp",
    )(lam, proj, proj, proj, cache_k, cache_v, cos, sin, gq2, gk2, gs)


def _rope_tables():
    n_rows = DEC_SEQ // GRID_W
    row = np.repeat(np.arange(n_rows), GRID_W).astype(np.float32)
    col = np.tile(np.arange(GRID_W), n_rows).astype(np.float32)
    half = B_DK // 2
    inv_freq = (ROPE_BASE ** (-jnp.arange(0, half, 2, dtype=F32) / half))
    row_ang = jnp.asarray(row)[:, None] * inv_freq
    col_ang = jnp.asarray(col)[:, None] * inv_freq
    ang = jnp.concatenate([row_ang, row_ang, col_ang, col_ang], axis=-1)
    ang = jnp.concatenate([ang, ang], axis=-1)
    sign = np.where((np.arange(LANES) % 32) < 16, -1.0, 1.0).astype(np.float32)
    return jnp.cos(ang), jnp.sin(ang) * sign


CM_ROWS = 512


def _gelu(x):
    return 0.5 * x * (1.0 + lax.erf(x * (2.0 ** -0.5)))


def _cmlp_kernel(u_ref, v_ref, g_ref, b_ref, ws_ref, bs_ref, o_ref):
    u = _gelu(u_ref[...])
    gv = _gelu(v_ref[...])
    mu = jnp.mean(gv, axis=-1, keepdims=True)
    dv = gv - mu
    var = jnp.mean(dv * dv, axis=-1, keepdims=True)
    vn = (dv * lax.rsqrt(var + EPS) * g_ref[...] + b_ref[...]).astype(BF16)
    lane = _lane((C_CHUNK, LANES))
    for c in range(CM_ROWS // C_CHUNK):
        rs = slice(c * C_CHUNK, (c + 1) * C_CHUNK)
        for p in range(2):
            cs = slice(p * LANES, (p + 1) * LANES)
            vp = vn[rs, cs]
            m0 = jnp.dot(ws_ref[2 * p].astype(BF16), vp, preferred_element_type=F32)
            m1 = jnp.dot(ws_ref[2 * p + 1].astype(BF16), vp, preferred_element_type=F32)
            mixed = jnp.where(lane < C_DG, m0, m1) + bs_ref[:, cs]
            o_ref[rs, cs] = u[rs, cs] * mixed


def _chunk_mlp(proj, ln_g, ln_b, w_s, bias_full):
    ucol = (5 * A_WIDTH + 3 * B_WIDTH) // C_WIDTH
    return pl.pallas_call(
        _cmlp_kernel,
        grid=(T // CM_ROWS,),
        in_specs=[
            pl.BlockSpec((CM_ROWS, C_WIDTH), lambda i: (i, ucol)),
            pl.BlockSpec((CM_ROWS, C_WIDTH), lambda i: (i, ucol + 1)),
            pl.BlockSpec((1, C_WIDTH), lambda i: (0, 0)),
            pl.BlockSpec((1, C_WIDTH), lambda i: (0, 0)),
            pl.BlockSpec((C_GROUPS, C_CHUNK, C_CHUNK), lambda i: (0, 0, 0)),
            pl.BlockSpec((C_CHUNK, C_WIDTH), lambda i: (0, 0)),
        ],
        out_specs=pl.BlockSpec((CM_ROWS, C_WIDTH), lambda i: (i, 0)),
        out_shape=jax.ShapeDtypeStruct((T, C_WIDTH), F32),
        compiler_params=_cparams(("arbitrary",)),
        name="chunk_mlp",
    )(proj, proj, ln_g.reshape(1, C_WIDTH), ln_b.reshape(1, C_WIDTH), w_s, bias_full)


def _postmix_kernel(*refs, n_x):
    (of_ref, ob_ref, ag_ref, hg_ref, hsel_ref, bc_ref, bs_ref, c_ref, w_ref, gate1_ref, shift2_ref, scale2_ref,
     g2_ref, wrh_ref, wrl_ref, br_ref, x1_ref, h2_ref, idx_ref, gw_ref) = refs[n_x:]
    x = _stream_tile(refs[:n_x])
    o = of_ref[...] + ob_ref[...]
    ms = _dot_sel(o * o, hsel_ref[...]) * (1.0 / A_DK)
    ag = ag_ref[...]
    a = o * lax.rsqrt(ms + EPS) * hg_ref[...] * (ag * jax.nn.sigmoid(ag))
    b = jnp.where(pl.program_id(0) < CTX_TILES, bc_ref[...], bs_ref[...])
    mixed = jnp.dot(a.astype(BF16), w_ref[0:A_WIDTH, :], preferred_element_type=F32)
    mixed = mixed + jnp.dot(b.astype(BF16), w_ref[A_WIDTH:A_WIDTH + B_WIDTH, :], preferred_element_type=F32)
    mixed = mixed + jnp.dot(c_ref[...].astype(BF16), w_ref[A_WIDTH + B_WIDTH:, :], preferred_element_type=F32)
    x1 = x + gate1_ref[0] * mixed
    x1_ref[...] = x1
    y = x1 * lax.rsqrt(jnp.mean(x1 * x1, axis=-1, keepdims=True) + EPS) * g2_ref[...]
    h2 = y * (1.0 + scale2_ref[0]) + shift2_ref[0]
    h2_ref[...] = _pack_rows(h2)
    hi = h2.astype(BF16)
    lo = (h2 - hi.astype(F32)).astype(BF16)
    lg = jnp.dot(lo, wrh_ref[...], preferred_element_type=F32)
    lg = lg + jnp.dot(hi, wrl_ref[...], preferred_element_type=F32)
    lg = lg + jnp.dot(hi, wrh_ref[...], preferred_element_type=F32) + br_ref[...]
    lt = lg.T[:N_EXPERTS]
    row = lax.broadcasted_iota(jnp.int32, lt.shape, 0)
    out_row = lax.broadcasted_iota(jnp.int32, (8, TM), 0)
    idx_out = jnp.zeros((8, TM), jnp.int32)
    val_out = jnp.zeros((8, TM), F32)
    top0 = None
    den = None
    for kk in range(TOP_K):
        mx = jnp.max(lt, axis=0, keepdims=True)
        am = jnp.min(jnp.where(lt == mx, row, N_EXPERTS), axis=0, keepdims=True)
        if kk == 0:
            top0 = mx
        e = jnp.exp(mx - top0)
        den = e if den is None else den + e
        idx_out = jnp.where(out_row == kk, am, idx_out)
        val_out = jnp.where(out_row == kk, e, val_out)
        lt = jnp.where(row == am, -jnp.inf, lt)
    idx_ref[...] = idx_out
    gw_ref[...] = val_out / den


def _post_mix(o_f, o_b, proj, hg, hsel, b_ctx, b_smp, c_out, w_out_bf, x, mod3, g2, wr_hi, wr_lo, br_pad):
    tile = lambda w: pl.BlockSpec((TM, w), lambda i: (i, 0))
    const = lambda shape: pl.BlockSpec(shape, lambda i: tuple(0 for _ in shape))
    modspec = lambda j: pl.BlockSpec((1, 1, D), lambda i: (_mod_row(i), 0, j))
    rowsT = pl.BlockSpec((8, TM), lambda i: (0, i))
    x_specs, x_args = _stream_specs(x)
    return pl.pallas_call(
        functools.partial(_postmix_kernel, n_x=len(x_args)),
        grid=(N_TILES,),
        in_specs=x_specs + [
            tile(A_WIDTH), tile(A_WIDTH),
            pl.BlockSpec((TM, A_WIDTH), lambda i: (i, 4)),
            const((1, A_WIDTH)), const((A_WIDTH, A_WIDTH)),
            pl.BlockSpec((TM, B_WIDTH), lambda i: (jnp.minimum(i, CTX_TILES - 1), 0)),
            pl.BlockSpec((TM, B_WIDTH), lambda i: (jnp.maximum(i - CTX_TILES, 0), 0)),
            tile(C_WIDTH),
            const((D, D)),
            modspec(2), modspec(3), modspec(4),
            const((1, D)), const((D, LANES)), const((D, LANES)), const((1, LANES)),
        ],
        out_specs=[tile(D), pl.BlockSpec((TM, ROW_TILES, LANES), lambda i: (i, 0, 0)), rowsT, rowsT],
        out_shape=[
            jax.ShapeDtypeStruct((T, D), F32),
            jax.ShapeDtypeStruct((T, ROW_TILES, LANES), ROW_DT),
            jax.ShapeDtypeStruct((8, T), jnp.int32),
            jax.ShapeDtypeStruct((8, T), F32),
        ],
        compiler_params=_cparams(("arbitrary",)),
        name="post_mix_router",
    )(*x_args, o_f, o_b, proj, hg, hsel, b_ctx, b_smp, c_out, w_out_bf, mod3, mod3, mod3, g2.reshape(1, D),
      wr_hi, wr_lo, br_pad)


def _route_kernel(idx_ref, dest_ref, meta_ref):
    erow = lax.broadcasted_iota(jnp.int32, (N_EXPERTS, TM), 0)
    s_i = lax.broadcasted_iota(jnp.int32, (TM, TM), 0)
    t_i = lax.broadcasted_iota(jnp.int32, (TM, TM), 1)
    earlier = (s_i < t_i).astype(BF16)
    out_row = lax.broadcasted_iota(jnp.int32, (8, TM), 0)

    def onehots(i):
        idx = idx_ref[:, pl.ds(pl.multiple_of(i * TM, TM), TM)]
        return [(erow == idx[kk:kk + 1, :]) for kk in range(TOP_K)]

    def count_tile(i, run):
        ohs = onehots(i)
        base = run
        pos = jnp.zeros((8, TM), F32)
        for kk in range(TOP_K):
            ohf = ohs[kk].astype(F32)
            before = jnp.dot(ohs[kk].astype(BF16), earlier, preferred_element_type=F32)
            p = jnp.sum(ohf * (base + before), axis=0, keepdims=True)
            pos = jnp.where(out_row == kk, p, pos)
            base = base + jnp.sum(ohf, axis=1, keepdims=True)
        dest_ref[:, pl.ds(pl.multiple_of(i * TM, TM), TM)] = pos.astype(jnp.int32)
        return base

    counts = lax.fori_loop(0, N_TILES, count_tile, jnp.zeros((N_EXPERTS, 1), F32)).astype(jnp.int32)
    bm_shift = MOE_BM.bit_length() - 1
    padded = lax.shift_left(lax.shift_right_logical(counts + (MOE_BM - 1), bm_shift), bm_shift)
    e_r = lax.broadcasted_iota(jnp.int32, (N_EXPERTS, N_EXPERTS), 0)
    e_c = lax.broadcasted_iota(jnp.int32, (N_EXPERTS, N_EXPERTS), 1)
    incl = (e_c <= e_r).astype(BF16)
    pad_end = _sel_dot(incl, jnp.broadcast_to(padded.astype(F32), (N_EXPERTS, LANES)))[:, :1]
    pad_start = pad_end - padded.astype(F32)

    def place_tile(i, carry):
        ohs = onehots(i)
        sl = pl.ds(pl.multiple_of(i * TM, TM), TM)
        off = jnp.zeros((8, TM), F32)
        for kk in range(TOP_K):
            o = jnp.sum(ohs[kk].astype(F32) * pad_start, axis=0, keepdims=True)
            off = jnp.where(out_row == kk, o, off)
        dest_ref[:, sl] = dest_ref[:, sl] + off.astype(jnp.int32)
        return carry

    lax.fori_loop(0, N_TILES, place_tile, 0)

    total = jnp.max(pad_end, axis=0, keepdims=True)
    lane_i = lax.broadcasted_iota(jnp.int32, (1, TM), 1)
    blk0 = (lane_i * MOE_BM).astype(F32)
    block_e = jnp.sum((pad_end <= blk0).astype(F32), axis=0, keepdims=True)
    live_end = pad_start + counts.astype(F32)
    sel = erow.astype(F32) == block_e
    live = jnp.sum(jnp.where(sel, live_end, 0.0), axis=0, keepdims=True)
    valid = jnp.where(blk0 < total, jnp.clip(live - blk0, 0.0, float(MOE_BM)), 0.0)
    own = erow == lane_i
    n_blk = jnp.sum(jnp.where(own, padded.astype(F32), 0.0), axis=0, keepdims=True) * (1.0 / MOE_BM)
    first_blk = jnp.sum(jnp.where(own, pad_start, 0.0), axis=0, keepdims=True) * (1.0 / MOE_BM)
    meta = jnp.where(out_row == 0, valid, 0.0)
    meta = jnp.where(out_row == 1, n_blk, meta)
    meta = jnp.where(out_row == 2, first_blk, meta)
    meta = jnp.where(out_row == 3, total * (1.0 / MOE_BM), meta)
    meta_ref[...] = meta.astype(jnp.int32)


def _route(idx_t):
    assert MOE_BLOCKS <= TM
    return pl.pallas_call(
        _route_kernel,
        out_shape=[jax.ShapeDtypeStruct((8, T), jnp.int32), jax.ShapeDtypeStruct((8, TM), jnp.int32)],
        compiler_params=pltpu.CompilerParams(vmem_limit_bytes=VMEM_LIMIT),
        name="moe_route",
    )(idx_t)


def _moe_kernel(bv_ref, nb_ref, g0_ref, tot_ref, x_hbm, wgu_ref, bgu_ref, wdn_ref, bdn_ref, y_hbm,
                wgu_bf, wdn_bf, xbuf, ybuf, xb_ref, xsem, ysem):
    e = pl.program_id(0)
    n_blk = nb_ref[e]
    first = g0_ref[e]
    total = tot_ref[0]
    ahead = MOE_RING - 1

    def x_copy(g):
        slot = g % MOE_RING
        return pltpu.make_async_copy(x_hbm.at[pl.ds(g * MOE_BM, MOE_BM)], xbuf.at[slot], xsem.at[slot])

    def y_copy(g):
        slot = g % MOE_RING
        return pltpu.make_async_copy(ybuf.at[slot], y_hbm.at[pl.ds(g * MOE_BM, MOE_BM)], ysem.at[slot])

    @pl.when(e == 0)
    def _():
        for g in range(ahead):
            @pl.when(g < total)
            def _():
                x_copy(g).start()

    @pl.when(n_blk > 0)
    def _():
        wgu_bf[...] = wgu_ref[0, 0].astype(BF16)
        wdn_bf[...] = wdn_ref[0, 0].astype(BF16)

    def block(j, carry):
        g = first + j
        slot = g % MOE_RING
        x_copy(g).wait()

        @pl.when(g + ahead < total)
        def _():
            x_copy(g + ahead).start()

        @pl.when(g >= MOE_RING)
        def _():
            y_copy(g - MOE_RING).wait()

        live = lax.broadcasted_iota(jnp.int32, (MOE_BM, LANES), 0) < bv_ref[g]
        for c, chunk in enumerate(_unpack_rows(xbuf[slot])):
            xb_ref[:, c * LANES:(c + 1) * LANES] = jnp.where(live, chunk, 0.0).astype(BF16)
        gu = jnp.dot(xb_ref[...], wgu_bf[...], preferred_element_type=F32) + bgu_ref[0, 0]
        glu = jnp.minimum(gu[:, :D], SWIGLU_LIMIT)
        lin = jnp.clip(gu[:, D:], -SWIGLU_LIMIT, SWIGLU_LIMIT)
        act = glu * jax.nn.sigmoid(SWIGLU_ALPHA * glu) * (lin + 1.0)
        y = jnp.dot(act.astype(BF16), wdn_bf[...], preferred_element_type=F32) + bdn_ref[0, 0]
        ybuf[slot] = _pack_rows(y)
        y_copy(g).start()
        return carry

    lax.fori_loop(0, n_blk, block, 0)

    @pl.when(e == N_EXPERTS - 1)
    def _():
        for back in range(MOE_RING, 0, -1):
            @pl.when(total >= back)
            def _():
                y_copy(total - back).wait()

        def fill(g, carry):
            ybuf[g % MOE_RING] = jnp.zeros((MOE_BM, ROW_TILES, LANES), ROW_DT)
            y_copy(g).start()
            y_copy(g).wait()
            return carry

        lax.fori_loop(total, MOE_BLOCKS, fill, 0)


def _moe_ffn(block_valid, n_blk, first_blk, total_blk, xs, w_gu, b_gu, w_dn, b_dn, layer):
    rows = (MOE_BM, ROW_TILES, LANES)
    return pl.pallas_call(
        _moe_kernel,
        grid_spec=pltpu.PrefetchScalarGridSpec(
            num_scalar_prefetch=4,
            grid=(N_EXPERTS,),
            in_specs=[
                pl.BlockSpec(memory_space=pl.ANY),
                pl.BlockSpec((1, 1, D, 2 * D), lambda e, *_: (layer, e, 0, 0)),
                pl.BlockSpec((1, 1, 1, 2 * D), lambda e, *_: (layer, e, 0, 0)),
                pl.BlockSpec((1, 1, D, D), lambda e, *_: (layer, e, 0, 0)),
                pl.BlockSpec((1, 1, 1, D), lambda e, *_: (layer, e, 0, 0)),
            ],
            out_specs=pl.BlockSpec(memory_space=pl.ANY),
            scratch_shapes=[
                pltpu.VMEM((D, 2 * D), BF16), pltpu.VMEM((D, D), BF16),
                pltpu.VMEM((MOE_RING,) + rows, ROW_DT), pltpu.VMEM((MOE_RING,) + rows, ROW_DT),
                pltpu.VMEM((MOE_BM, D), BF16),
                pltpu.SemaphoreType.DMA((MOE_RING,)), pltpu.SemaphoreType.DMA((MOE_RING,)),
            ],
        ),
        out_shape=jax.ShapeDtypeStruct((MOE_ROWS, ROW_TILES, LANES), ROW_DT),
        compiler_params=_cparams(("arbitrary",)),
        name="moe_expert_ffn",
    )(block_valid, n_blk, first_blk, total_blk, xs, w_gu, b_gu.reshape(DEPTH, N_EXPERTS, 1, 2 * D), w_dn,
      b_dn.reshape(DEPTH, N_EXPERTS, 1, D))


def _combine_kernel(x1_ref, y_ref, gw_ref, gate2_ref, o_ref):
    gw = jnp.concatenate([gw_ref[...], jnp.zeros((LANES - 8, TM), F32)], axis=0).T
    ys = [_unpack_rows(y_ref[kk]) for kk in range(TOP_K)]
    for c in range(D // LANES):
        cs = slice(c * LANES, (c + 1) * LANES)
        acc = None
        for kk in range(TOP_K):
            term = ys[kk][c] * gw[:, kk:kk + 1]
            acc = term if acc is None else acc + term
        o_ref[:, cs] = x1_ref[:, cs] + gate2_ref[0, :, cs] * acc


def _combine(x1, yg, gw, mod3):
    return pl.pallas_call(
        _combine_kernel,
        grid=(N_TILES,),
        in_specs=[
            pl.BlockSpec((TM, D), lambda i: (i, 0)),
            pl.BlockSpec((TOP_K, TM, ROW_TILES, LANES), lambda i: (0, i, 0, 0)),
            pl.BlockSpec((8, TM), lambda i: (0, i)),
            pl.BlockSpec((1, 1, D), lambda i: (_mod_row(i), 0, 5)),
        ],
        out_specs=pl.BlockSpec((TM, D), lambda i: (i, 0)),
        out_shape=jax.ShapeDtypeStruct((T, D), F32),
        compiler_params=_cparams(("arbitrary",)),
        name="moe_combine",
    )(x1, yg, gw, mod3)


def _sc_mesh():
    return plsc.VectorSubcoreMesh(core_axis_name="c", subcore_axis_name="s")


def _sc_worker():
    return lax.axis_index("s") * SC_CORES + lax.axis_index("c")


def _sc_dispatch(h2t, dest_km):
    per_w = T // SC_WORKERS

    @functools.partial(
        pl.kernel, mesh=_sc_mesh(),
        out_type=jax.ShapeDtypeStruct((MOE_ROWS, ROW_TILES, LANES), ROW_DT),
        scratch_types=[pltpu.VMEM((SC_WIN,), jnp.int32), pltpu.VMEM((SC_WIN, ROW_TILES, LANES), ROW_DT),
                       pltpu.SemaphoreType.DMA],
    )
    def run(h_hbm, d_hbm, o_hbm, idx_v, rows_v, sem):
        w0 = _sc_worker() * per_w

        @pl.loop(0, per_w // SC_WIN)
        def _(w):
            base = pl.multiple_of(w0 + w * SC_WIN, SC_WIN)
            pltpu.sync_copy(h_hbm.at[pl.ds(base, SC_WIN)], rows_v)
            for kk in range(TOP_K):
                pltpu.sync_copy(d_hbm.at[pl.ds(kk * T + base, SC_WIN)], idx_v)
                pltpu.async_copy(rows_v, o_hbm.at[idx_v], sem).wait()

    return run(h2t, dest_km)


def _sc_gather(yb, dest_km):
    n = TOP_K * T
    per_w = n // SC_WORKERS

    @functools.partial(
        pl.kernel, mesh=_sc_mesh(),
        out_type=jax.ShapeDtypeStruct((n, ROW_TILES, LANES), ROW_DT),
        scratch_types=[pltpu.VMEM((SC_WIN,), jnp.int32), pltpu.VMEM((SC_WIN, ROW_TILES, LANES), ROW_DT),
                       pltpu.SemaphoreType.DMA],
    )
    def run(y_hbm, d_hbm, o_hbm, idx_v, rows_v, sem):
        w0 = _sc_worker() * per_w

        @pl.loop(0, per_w // SC_WIN)
        def _(w):
            base = pl.multiple_of(w0 + w * SC_WIN, SC_WIN)
            pltpu.sync_copy(d_hbm.at[pl.ds(base, SC_WIN)], idx_v)
            pltpu.async_copy(y_hbm.at[idx_v], rows_v, sem).wait()
            pltpu.sync_copy(rows_v, o_hbm.at[pl.ds(base, SC_WIN)])

    return run(yb, dest_km)


def kernel(x_prompt, x_sample, c, cache_diff_k, cache_diff_v, state_hgrn, c_ctx, norm_mix_g, norm_ffn_g, w_mod, b_mod, w_in, w_out, hgrn_lower_bounds, hgrn_norm_g, diff_q_norm_g, diff_k_norm_g, diff_lambda_q1, diff_lambda_k1, diff_lambda_q2, diff_lambda_k2, diff_subln_g, cmlp_ln_g, cmlp_ln_b, cmlp_w_s, cmlp_b_s, router_w, router_b, moe_w_gate_up, moe_b_gate_up, moe_w_down, moe_b_down):
    x = (x_prompt.reshape(T_CTX, D), x_sample.reshape(T_SMP, D))
    cvec = jnp.concatenate([c_ctx[None, :], c, jnp.zeros((MOD_ROWS - 1 - DEC_BATCH, D), F32)], axis=0)
    mod = _modulation(cvec, w_mod, b_mod)

    lvl_np, tri_np = _hgrn_tables()
    lvl = jnp.asarray(lvl_np)
    tri = jnp.asarray(tri_np, dtype=BF16)
    cos, sin = _rope_tables()
    hsel = jnp.asarray(np.kron(np.eye(A_HEADS), np.ones((A_DK, A_DK))), dtype=BF16)
    sm = jax.nn.softmax(hgrn_lower_bounds.astype(F32), axis=0)
    lb_all = jnp.cumsum(sm, axis=0) - sm[0]

    new_k, new_v, new_s = None, None, []
    for l in range(DEPTH):
        mod3 = mod[l].reshape(MOD_ROWS, 1, 6 * D)
        proj = _in_projection(x, norm_mix_g[l], mod3, w_in[l].astype(BF16))

        s0 = jnp.concatenate([jnp.zeros((BATCH, 2, A_HEADS, A_DK, A_DK), F32), state_hgrn[:, l]], axis=0)
        o_dir, fin_dir = [], []
        for d in range(2):
            o_d, fin_d = _hgrn_scan(proj, lb_all[l, d].reshape(1, A_WIDTH), _pack_state(s0[:, d]),
                                    lvl, tri, l, d == 1)
            o_dir.append(o_d)
            fin_dir.append(_unpack_state(fin_d[:BATCH]))
        new_s.append(jnp.stack(fin_dir, axis=1))

        lam_init = 0.8 - 0.6 * math.exp(-0.3 * l)
        lam = (jnp.exp(jnp.sum(diff_lambda_q1[l] * diff_lambda_k1[l]))
               - jnp.exp(jnp.sum(diff_lambda_q2[l] * diff_lambda_k2[l])) + lam_init).reshape(1, 1)
        gq2 = jnp.tile(diff_q_norm_g[l], 2).reshape(1, LANES)
        gk2 = jnp.tile(diff_k_norm_g[l], 2).reshape(1, LANES)
        gs = diff_subln_g[l].reshape(1, LANES)
        b_ctx, new_k, new_v = _attn_ctx(proj, lam, gq2, gk2, gs, lam_init, l, new_k, new_v)
        b_smp = _attn_smp(proj, lam, cache_diff_k, cache_diff_v, cos, sin, gq2, gk2, gs, l, lam_init)

        bias_full = jnp.repeat(cmlp_b_s[l].T, C_DG, axis=1)
        c_out = _chunk_mlp(proj, cmlp_ln_g[l], cmlp_ln_b[l], cmlp_w_s[l], bias_full)

        hg = jnp.tile(hgrn_norm_g[l], A_HEADS).reshape(1, A_WIDTH)
        wr_pad = jnp.pad(router_w[l], ((0, 0), (0, LANES - N_EXPERTS)))
        wr_hi = wr_pad.astype(BF16)
        wr_lo = (wr_pad - wr_hi.astype(F32)).astype(BF16)
        br_pad = jnp.pad(router_b[l], (0, LANES - N_EXPERTS)).reshape(1, LANES)
        x1, h2, idx_t, gw_t = _post_mix(o_dir[0], o_dir[1], proj, hg, hsel, b_ctx, b_smp, c_out,
                                        w_out[l].astype(BF16), x, mod3, norm_ffn_g[l], wr_hi, wr_lo, br_pad)

        dest_t, meta = _route(idx_t)
        dest_km = dest_t[:TOP_K].reshape(-1)
        xs = _sc_dispatch(h2, dest_km)
        yb = _moe_ffn(meta[0, :MOE_BLOCKS], meta[1, :N_EXPERTS], meta[2, :N_EXPERTS], meta[3, :1], xs,
                      moe_w_gate_up, moe_b_gate_up, moe_w_down, moe_b_down, l)
        yg = _sc_gather(yb, dest_km).reshape(TOP_K, T, ROW_TILES, LANES)
        x = _combine(x1, yg, gw_t, mod3)

    y_prompt = x[:T_CTX].reshape(BATCH, SEQ, D)
    y_sample = x[T_CTX:].reshape(DEC_BATCH, DEC_SEQ, D)
    return (y_prompt, y_sample, new_k, new_v, jnp.stack(new_s, axis=1))
```

```python
import functools
import math

import numpy as np
import jax
import jax.numpy as jnp
from jax import lax
from jax.experimental import pallas as pl
from jax.experimental.pallas import tpu as pltpu
from jax.experimental.pallas import tpu_sc as plsc

F32 = jnp.float32
BF16 = jnp.bfloat16

D = 1024
DEPTH = 2
BATCH, SEQ = 16, 256
DEC_BATCH, DEC_SEQ = 8, 1024
PAST = 512
GRID_W = 64
A_HEADS, A_DK = 4, 64
A_WIDTH = 256
B_HEADS, B_DK, B_DV = 4, 64, 128
B_WIDTH = 512
C_GROUPS, C_CHUNK, C_WIDTH, C_DG = 4, 128, 256, 64
IN_WIDTH = 5 * A_WIDTH + 3 * B_WIDTH + 2 * C_WIDTH
N_EXPERTS, TOP_K = 32, 4
SWIGLU_LIMIT, SWIGLU_ALPHA = 7.0, 1.702
ROPE_BASE = 10000.0
EPS = 1e-6

T_CTX = BATCH * SEQ
T_SMP = DEC_BATCH * DEC_SEQ
T = T_CTX + T_SMP
N_SEQ = BATCH + DEC_BATCH
MOD_ROWS = 16

TM = 256
N_TILES = T // TM
CTX_TILES = T_CTX // TM
SMP_TILES_PER_SEQ = DEC_SEQ // TM
LANES = 128
MOE_BM = 256
MOE_ROWS = T * TOP_K + N_EXPERTS * MOE_BM
MOE_BLOCKS = MOE_ROWS // MOE_BM
MOE_RING = 4
ROW_WORDS = D // 2
ROW_TILES = ROW_WORDS // LANES
ROW_DT = jnp.int32
SC_CORES, SC_SUBCORES = 2, 16
SC_WORKERS = SC_CORES * SC_SUBCORES
SC_WIN = 128
VMEM_LIMIT = 56 * 1024 * 1024


def _cparams(sem):
    return pltpu.CompilerParams(dimension_semantics=sem, vmem_limit_bytes=VMEM_LIMIT)


def _mod_row(i):
    return jnp.where(i < CTX_TILES, 0, 1 + (i - CTX_TILES) // SMP_TILES_PER_SEQ)


def _split3(x):
    hi = x.astype(BF16)
    r = x - hi.astype(F32)
    mid = r.astype(BF16)
    lo = (r - mid.astype(F32)).astype(BF16)
    return hi, mid, lo


def _sel_dot(sel, x):
    hi, mid, lo = _split3(x)
    acc = jnp.dot(sel, lo, preferred_element_type=F32)
    acc = acc + jnp.dot(sel, mid, preferred_element_type=F32)
    return acc + jnp.dot(sel, hi, preferred_element_type=F32)


def _dot_sel(x, sel):
    hi, mid, lo = _split3(x)
    acc = jnp.dot(lo, sel, preferred_element_type=F32)
    acc = acc + jnp.dot(mid, sel, preferred_element_type=F32)
    return acc + jnp.dot(hi, sel, preferred_element_type=F32)


def _dot_nt(a, b):
    return lax.dot_general(a, b, (((1,), (1,)), ((), ())), preferred_element_type=F32)


def _dot_tn(a, b):
    return lax.dot_general(a, b, (((0,), (0,)), ((), ())), preferred_element_type=F32)


def _lane(shape):
    return lax.broadcasted_iota(jnp.int32, shape, len(shape) - 1)


def _pack_rows(x):
    hi = lax.bitcast_convert_type(x[:, :ROW_WORDS].astype(BF16).astype(F32), jnp.int32)
    lo = lax.bitcast_convert_type(x[:, ROW_WORDS:].astype(BF16).astype(F32), jnp.int32)
    words = hi | lax.shift_right_logical(lo, 16)
    return pltpu.einshape("t(jl)->tjl", words, l=LANES)


def _unpack_rows(words3):
    wt = pltpu.einshape("tjl->jtl", words3)
    hi = [lax.bitcast_convert_type(wt[j] & jnp.int32(-65536), F32) for j in range(ROW_TILES)]
    lo = [lax.bitcast_convert_type(lax.shift_left(wt[j], 16), F32) for j in range(ROW_TILES)]
    return hi + lo


def _mod_kernel(c_ref, w_ref, b_ref, o_ref):
    c = c_ref[...]
    s = c * jax.nn.sigmoid(c)
    o_ref[0] = jnp.dot(s.astype(BF16), w_ref[0].astype(BF16), preferred_element_type=F32) + b_ref[0]


def _modulation(cvec, w_mod, b_mod):
    tn = 1536
    return pl.pallas_call(
        _mod_kernel,
        grid=(DEPTH, 6 * D // tn),
        in_specs=[
            pl.BlockSpec((MOD_ROWS, D), lambda l, j: (0, 0)),
            pl.BlockSpec((1, D, tn), lambda l, j: (l, 0, j)),
            pl.BlockSpec((1, 1, tn), lambda l, j: (l, 0, j)),
        ],
        out_specs=pl.BlockSpec((1, MOD_ROWS, tn), lambda l, j: (l, 0, j)),
        out_shape=jax.ShapeDtypeStruct((DEPTH, MOD_ROWS, 6 * D), F32),
        compiler_params=_cparams(("arbitrary", "arbitrary")),
        name="modulation",
    )(cvec, w_mod, b_mod.reshape(DEPTH, 1, 6 * D))


def _stream_specs(x):
    if isinstance(x, tuple):
        return [pl.BlockSpec((TM, D), lambda i: (jnp.minimum(i, CTX_TILES - 1), 0)),
                pl.BlockSpec((TM, D), lambda i: (jnp.maximum(i - CTX_TILES, 0), 0))], list(x)
    return [pl.BlockSpec((TM, D), lambda i: (i, 0))], [x]


def _stream_tile(x_refs):
    if len(x_refs) == 1:
        return x_refs[0][...]
    return jnp.where(pl.program_id(0) < CTX_TILES, x_refs[0][...], x_refs[1][...])


def _inproj_kernel(*refs, n_x):
    g_ref, shift_ref, scale_ref, w_ref, o_ref = refs[n_x:]
    x = _stream_tile(refs[:n_x])
    y = x * lax.rsqrt(jnp.mean(x * x, axis=-1, keepdims=True) + EPS) * g_ref[...]
    h = y * (1.0 + scale_ref[0]) + shift_ref[0]
    o_ref[...] = jnp.dot(h.astype(BF16), w_ref[...], preferred_element_type=F32)


def _in_projection(x, g, mod3, w_in_bf):
    x_specs, x_args = _stream_specs(x)
    return pl.pallas_call(
        functools.partial(_inproj_kernel, n_x=len(x_args)),
        grid=(N_TILES,),
        in_specs=x_specs + [
            pl.BlockSpec((1, D), lambda i: (0, 0)),
            pl.BlockSpec((1, 1, D), lambda i: (_mod_row(i), 0, 0)),
            pl.BlockSpec((1, 1, D), lambda i: (_mod_row(i), 0, 1)),
            pl.BlockSpec((D, IN_WIDTH), lambda i: (0, 0)),
        ],
        out_specs=pl.BlockSpec((TM, IN_WIDTH), lambda i: (i, 0)),
        out_shape=jax.ShapeDtypeStruct((T, IN_WIDTH), F32),
        compiler_params=_cparams(("arbitrary",)),
        name="in_projection",
    )(*x_args, g.reshape(1, D), mod3, mod3, w_in_bf)


HG_C = 128
HG_LEVELS = tuple(2 ** j for j in range(1, int(math.log2(HG_C)) + 1))


def _hgrn_tables():
    t = np.arange(HG_C)[:, None]
    s = np.arange(HG_C)[None, :]
    x = t ^ s
    lvl = np.zeros((HG_C, HG_C), np.int32)
    nz = x > 0
    lvl[nz] = np.floor(np.log2(x[nz])).astype(np.int32) + 1
    fwd = np.where(t >= s, lvl, -1).astype(np.int32)
    bwd = np.where(t <= s, lvl, -1).astype(np.int32)
    tri_f = (t >= s).astype(np.float32)
    tri_b = (t <= s).astype(np.float32)
    return np.stack([fwd, bwd]), np.stack([tri_f, tri_b])


def _block_ref(cum, m, idx):
    c, l = cum.shape
    if m >= 16:
        c3 = cum.reshape(c // m, m, l)
        r = c3[:, idx:idx + 1, :]
        return jnp.broadcast_to(r, (c // m, m, l)).reshape(c, l)
    c3 = cum.reshape(c // 8, 8, l)
    sub = lax.broadcasted_iota(jnp.int32, c3.shape, 1)
    out = None
    for j in range(8 // m - 1, -1, -1):
        cand = jnp.broadcast_to(c3[:, j * m + idx:j * m + idx + 1, :], c3.shape)
        out = cand if out is None else jnp.where(sub < (j + 1) * m, cand, out)
    return out.reshape(c, l)


def _hgrn_kernel(q_ref, z_ref, v_ref, lb_ref, s0_ref, lvl_ref, tri_ref, o_ref, fin_ref, st_ref, *, layer, rev):
    g = pl.program_id(0)
    first = jnp.logical_or(g < CTX_TILES, (g - CTX_TILES) % SMP_TILES_PER_SEQ == 0)

    @pl.when(first)
    def _():
        st_ref[...] = s0_ref[0]

    qr = q_ref[...]
    q = qr * jax.nn.sigmoid(qr) * (A_DK ** -0.5)
    z = z_ref[...]
    if layer == 0:
        lf = jnp.minimum(z, 0.0) - jnp.log(1.0 + jnp.exp(-jnp.abs(z)))
        k = jax.nn.sigmoid(-z)
    else:
        lbd = lb_ref[...]
        lf = jnp.log(lbd + (1.0 - lbd) * jax.nn.sigmoid(z))
        k = (1.0 - lbd) * jax.nn.sigmoid(-z)
    v = v_ref[...]
    tri = tri_ref[0]
    lvl = lvl_ref[0]
    last_row = 0 if rev else HG_C - 1
    n_chunks = TM // HG_C
    order = range(n_chunks - 1, -1, -1) if rev else range(n_chunks)
    cums = [_sel_dot(tri, lf[c * HG_C:(c + 1) * HG_C]) for c in range(n_chunks)]
    lane = _lane((HG_C, LANES))
    head_masks = (lane < A_DK, lane >= A_DK)
    lane_row = _lane((1, LANES))
    head_keep = ((lane_row < A_DK).astype(BF16), (lane_row >= A_DK).astype(BF16))
    lvl2 = jnp.concatenate([lvl, lvl], axis=0)
    level_masks = [lvl2 == i for i in range(len(HG_LEVELS) + 1)]
    r = lax.broadcasted_iota(jnp.int32, (LANES, LANES), 0)
    cl = lax.broadcasted_iota(jnp.int32, (LANES, LANES), 1)
    same_head = (r < A_DK) == (cl < A_DK)

    def chunk(q_p, k_p, v_p, cum_p, st):
        v_bf = v_p.astype(BF16)
        k_bf = k_p.astype(BF16)
        q_bf = q_p.astype(BF16)

        def both_heads(x_bf):
            return jnp.concatenate([x_bf * head_keep[0], x_bf * head_keep[1]], axis=0)

        scores = jnp.where(level_masks[0], _dot_nt(both_heads(q_bf), k_bf), 0.0)
        for li, m in enumerate(HG_LEVELS):
            ref = _block_ref(cum_p, m, m // 2 if rev else m // 2 - 1)
            dec = jnp.exp(-jnp.abs(cum_p - ref))
            qd = (q_p * dec).astype(BF16)
            kd = (k_p * dec).astype(BF16)
            scores = jnp.where(level_masks[li + 1], _dot_nt(both_heads(qd), kd), scores)
        pv = jnp.dot(scores.astype(BF16), v_bf, preferred_element_type=F32)
        o_intra = jnp.where(head_masks[0], pv[:HG_C], pv[HG_C:])
        q0 = (q_p * jnp.exp(cum_p)).astype(BF16)
        out = o_intra + _dot_nt(q0, st.astype(BF16))
        last = cum_p[last_row:last_row + 1, :]
        ks = (k_p * jnp.exp(last - cum_p)).astype(BF16)
        upd = _dot_tn(v_bf, ks)
        return out, st * jnp.exp(last) + jnp.where(same_head, upd, 0.0)

    for p in range(2):
        sl = slice(p * LANES, (p + 1) * LANES)
        st = st_ref[p]
        for c in order:
            rows = slice(c * HG_C, (c + 1) * HG_C)
            o_ref[rows, sl], st = chunk(q[rows, sl], k[rows, sl], v[rows, sl], cums[c][:, sl], st)
        st_ref[p] = st
        fin_ref[0, p] = st


def _hgrn_seq(g):
    return jnp.where(g < CTX_TILES, g, CTX_TILES + (g - CTX_TILES) // SMP_TILES_PER_SEQ)


def _hgrn_blk(g, rev):
    if not rev:
        return g
    j = g - CTX_TILES
    return jnp.where(g < CTX_TILES, g,
                     CTX_TILES + (j // SMP_TILES_PER_SEQ) * SMP_TILES_PER_SEQ
                     + (SMP_TILES_PER_SEQ - 1 - j % SMP_TILES_PER_SEQ))


def _hgrn_scan(proj, lb_dir, s0_dir, lvl, tri, layer, rev):
    d = 1 if rev else 0
    blk = functools.partial(_hgrn_blk, rev=rev)
    return pl.pallas_call(
        functools.partial(_hgrn_kernel, layer=layer, rev=rev),
        grid=(N_TILES,),
        in_specs=[
            pl.BlockSpec((TM, A_WIDTH), lambda g: (blk(g), 0)),
            pl.BlockSpec((TM, A_WIDTH), lambda g: (blk(g), 1 + d)),
            pl.BlockSpec((TM, A_WIDTH), lambda g: (blk(g), 3)),
            pl.BlockSpec((1, A_WIDTH), lambda g: (0, 0)),
            pl.BlockSpec((1, 2, LANES, LANES), lambda g: (_hgrn_seq(g), 0, 0, 0)),
            pl.BlockSpec((1, HG_C, HG_C), lambda g: (d, 0, 0)),
            pl.BlockSpec((1, HG_C, HG_C), lambda g: (d, 0, 0)),
        ],
        out_specs=[
            pl.BlockSpec((TM, A_WIDTH), lambda g: (blk(g), 0)),
            pl.BlockSpec((1, 2, LANES, LANES), lambda g: (_hgrn_seq(g), 0, 0, 0)),
        ],
        out_shape=[
            jax.ShapeDtypeStruct((T, A_WIDTH), F32),
            jax.ShapeDtypeStruct((N_SEQ, 2, LANES, LANES), F32),
        ],
        scratch_shapes=[pltpu.VMEM((2, LANES, LANES), F32)],
        compiler_params=_cparams(("arbitrary",)),
        name=f"hgrn_scan_{'bwd' if rev else 'fwd'}",
    )(proj, proj, proj, lb_dir, s0_dir, lvl, tri)


def _pack_state(s):
    n = s.shape[0]
    st = jnp.swapaxes(s, -1, -2).reshape(n, 2, 2, A_DK, A_DK)
    z = jnp.zeros_like(st[:, :, 0])
    top = jnp.concatenate([st[:, :, 0], z], axis=-1)
    bot = jnp.concatenate([z, st[:, :, 1]], axis=-1)
    return jnp.concatenate([top, bot], axis=-2)


def _unpack_state(sp):
    n = sp.shape[0]
    h0 = sp[:, :, :A_DK, :A_DK]
    h1 = sp[:, :, A_DK:, A_DK:]
    st = jnp.stack([h0, h1], axis=2).reshape(n, A_HEADS, A_DK, A_DK)
    return jnp.swapaxes(st, -1, -2)


def _half_rms(x, g):
    r = lax.broadcasted_iota(jnp.int32, (LANES, LANES), 0)
    c = lax.broadcasted_iota(jnp.int32, (LANES, LANES), 1)
    half_mean = jnp.where((r < B_DK) == (c < B_DK), 1.0 / B_DK, 0.0).astype(BF16)
    xx = x * x
    hi = xx.astype(BF16)
    lo = (xx - hi.astype(F32)).astype(BF16)
    ms = jnp.dot(lo, half_mean, preferred_element_type=F32) + jnp.dot(hi, half_mean, preferred_element_type=F32)
    return x * lax.rsqrt(ms + EPS) * g


def _rope(x, cos, sin_signed):
    lane = _lane(x.shape)
    first = (lane % 32) < 16
    rot = jnp.where(first, pltpu.roll(x, LANES - 16, 1), pltpu.roll(x, 16, 1))
    return x * cos + rot * sin_signed


def _with_ones(v_bf):
    return jnp.concatenate([v_bf, jnp.ones_like(v_bf)], axis=-1)


def _diff_softmax_pv(q_bf, keys_bf, vals_ext, lam):
    lane = _lane(q_bf.shape)
    zero = jnp.zeros_like(q_bf)
    outs = []
    for mp in range(2):
        qm = jnp.where((lane < B_DK) == (mp == 0), q_bf, zero)
        s = [_dot_nt(qm, kk) for kk in keys_bf]
        mx = functools.reduce(jnp.maximum, [jnp.max(si, axis=-1, keepdims=True) for si in s])
        acc = None
        for si, ve in zip(s, vals_ext):
            e = jnp.exp((si - mx).astype(BF16))
            pv = jnp.dot(e, ve, preferred_element_type=F32)
            acc = pv if acc is None else acc + pv
        outs.append(acc[:, :B_DV] / acc[:, B_DV:])
    return outs[0] - lam * outs[1]


def _subln(o, g, lam_init):
    return o * lax.rsqrt(jnp.mean(o * o, axis=-1, keepdims=True) + EPS) * g * (1.0 - lam_init)


def _attn_ctx_kernel(lam_ref, *refs, lam_init, layer):
    q_refs, k_refs, v_refs = refs[:B_HEADS], refs[B_HEADS:2 * B_HEADS], refs[2 * B_HEADS:3 * B_HEADS]
    gq_ref, gk_ref, gs_ref = refs[3 * B_HEADS:3 * B_HEADS + 3]
    rest = refs[3 * B_HEADS + 3:]
    if layer:
        pk_ref, pv_ref, o_ref, nk_ref, nv_ref = rest
        nk_ref[0, :layer] = pk_ref[0]
        nv_ref[0, :layer] = pv_ref[0]
    else:
        o_ref, nk_ref, nv_ref = rest
    lam = lam_ref[0, 0]
    for h in range(B_HEADS):
        qn = _half_rms(q_refs[h][...], gq_ref[...]) * (B_DK ** -0.5)
        kn = _half_rms(k_refs[h][...], gk_ref[...])
        v = v_refs[h][...]
        nk_ref[0, layer, 0, h] = kn[:, :B_DK]
        nk_ref[0, layer, 1, h] = kn[:, B_DK:]
        nv_ref[0, layer, h] = v
        o = _diff_softmax_pv(qn.astype(BF16), [kn.astype(BF16)], [_with_ones(v.astype(BF16))], lam)
        o_ref[:, h * LANES:(h + 1) * LANES] = _subln(o, gs_ref[...], lam_init)


def _attn_ctx(proj, lam, gq2, gk2, gs, lam_init, layer, prev_k, prev_v):
    qcol, kcol, vcol = 5 * A_WIDTH // LANES, 5 * A_WIDTH // LANES + 4, 5 * A_WIDTH // LANES + 8
    prev_specs, prev_args = [], []
    if layer:
        prev_specs = [pl.BlockSpec((1, layer, 2, B_HEADS, SEQ, B_DK), lambda b: (b, 0, 0, 0, 0, 0)),
                      pl.BlockSpec((1, layer, B_HEADS, SEQ, B_DV), lambda b: (b, 0, 0, 0, 0))]
        prev_args = [prev_k, prev_v]
    n_l = layer + 1
    head_specs = [pl.BlockSpec((SEQ, LANES), functools.partial(lambda b, col: (b, col), col=c0 + h))
                  for c0 in (qcol, kcol, vcol) for h in range(B_HEADS)]
    return pl.pallas_call(
        functools.partial(_attn_ctx_kernel, lam_init=lam_init, layer=layer),
        grid=(BATCH,),
        in_specs=[pl.BlockSpec(memory_space=pltpu.SMEM)] + head_specs + [
            pl.BlockSpec((1, LANES), lambda b: (0, 0)),
            pl.BlockSpec((1, LANES), lambda b: (0, 0)),
            pl.BlockSpec((1, LANES), lambda b: (0, 0)),
        ] + prev_specs,
        out_specs=[
            pl.BlockSpec((SEQ, B_WIDTH), lambda b: (b, 0)),
            pl.BlockSpec((1, n_l, 2, B_HEADS, SEQ, B_DK), lambda b: (b, 0, 0, 0, 0, 0)),
            pl.BlockSpec((1, n_l, B_HEADS, SEQ, B_DV), lambda b: (b, 0, 0, 0, 0)),
        ],
        out_shape=[
            jax.ShapeDtypeStruct((T_CTX, B_WIDTH), F32),
            jax.ShapeDtypeStruct((BATCH, n_l, 2, B_HEADS, SEQ, B_DK), F32),
            jax.ShapeDtypeStruct((BATCH, n_l, B_HEADS, SEQ, B_DV), F32),
        ],
        compiler_params=_cparams(("arbitrary",)),
        name="diff_attention_ctx",
    )(lam, *([proj] * (3 * B_HEADS)), gq2, gk2, gs, *prev_args)


ATT_TQ = 256


def _attn_smp_kernel(lam_ref, q_ref, k_ref, v_ref, ck_ref, cv_ref, cos_ref, sin_ref, gq_ref, gk_ref, gs_ref,
                     o_ref, qs_ref, ks_ref, *, lam_init):
    lam = lam_ref[0, 0]
    cos = cos_ref[...]
    sin = sin_ref[...]
    qn = _rope(_half_rms(q_ref[...], gq_ref[...]), cos, sin) * (B_DK ** -0.5)
    qs_ref[...] = qn.astype(BF16)
    ks_ref[...] = _rope(_half_rms(k_ref[...], gk_ref[...]), cos, sin).astype(BF16)
    ck = jnp.concatenate([ck_ref[0, 0, 0, 0], ck_ref[0, 0, 1, 0]], axis=-1).astype(BF16)
    cv = _with_ones(cv_ref[0, 0, 0].astype(BF16))
    v_bf = _with_ones(v_ref[...].astype(BF16))
    k_bf = ks_ref[...]
    g = gs_ref[...]

    for i in range(DEC_SEQ // ATT_TQ):
        rows = slice(i * ATT_TQ, (i + 1) * ATT_TQ)
        o = _diff_softmax_pv(qs_ref[rows, :], [k_bf, ck], [v_bf, cv], lam)
        o_ref[rows, :] = _subln(o, g, lam_init)


def _attn_smp(proj, lam, cache_k, cache_v, cos, sin, gq2, gk2, gs, layer, lam_init):
    qcol, kcol, vcol = 5 * A_WIDTH // LANES, 5 * A_WIDTH // LANES + 4, 5 * A_WIDTH // LANES + 8
    r0 = T_CTX // DEC_SEQ
    return pl.pallas_call(
        functools.partial(_attn_smp_kernel, lam_init=lam_init),
        grid=(DEC_BATCH, B_HEADS),
        in_specs=[
            pl.BlockSpec(memory_space=pltpu.SMEM),
            pl.BlockSpec((DEC_SEQ, LANES), lambda b, h: (r0 + b, qcol + h)),
            pl.BlockSpec((DEC_SEQ, LANES), lambda b, h: (r0 + b, kcol + h)),
            pl.BlockSpec((DEC_SEQ, LANES), lambda b, h: (r0 + b, vcol + h)),
            pl.BlockSpec((1, 1, 2, 1, PAST, B_DK), lambda b, h: (b, layer, 0, h, 0, 0)),
            pl.BlockSpec((1, 1, 1, PAST, B_DV), lambda b, h: (b, layer, h, 0, 0)),
            pl.BlockSpec((DEC_SEQ, LANES), lambda b, h: (0, 0)),
            pl.BlockSpec((DEC_SEQ, LANES), lambda b, h: (0, 0)),
            pl.BlockSpec((1, LANES), lambda b, h: (0, 0)),
            pl.BlockSpec((1, LANES), lambda b, h: (0, 0)),
            pl.BlockSpec((1, LANES), lambda b, h: (0, 0)),
        ],
        out_specs=pl.BlockSpec((DEC_SEQ, LANES), lambda b, h: (b, h)),
        out_shape=jax.ShapeDtypeStruct((T_SMP, B_WIDTH), F32),
        scratch_shapes=[pltpu.VMEM((DEC_SEQ, LANES), BF16), pltpu.VMEM((DEC_SEQ, LANES), BF16)],
        compiler_params=_cparams(("arbitrary", "arbitrary")),
        name="diff_attention_smp",
    )(lam, proj, proj, proj, cache_k, cache_v, cos, sin, gq2, gk2, gs)


def _rope_tables():
    n_rows = DEC_SEQ // GRID_W
    row = np.repeat(np.arange(n_rows), GRID_W).astype(np.float32)
    col = np.tile(np.arange(GRID_W), n_rows).astype(np.float32)
    half = B_DK // 2
    inv_freq = (ROPE_BASE ** (-jnp.arange(0, half, 2, dtype=F32) / half))
    row_ang = jnp.asarray(row)[:, None] * inv_freq
    col_ang = jnp.asarray(col)[:, None] * inv_freq
    ang = jnp.concatenate([row_ang, row_ang, col_ang, col_ang], axis=-1)
    ang = jnp.concatenate([ang, ang], axis=-1)
    sign = np.where((np.arange(LANES) % 32) < 16, -1.0, 1.0).astype(np.float32)
    return jnp.cos(ang), jnp.sin(ang) * sign


CM_ROWS = 512


def _gelu(x):
    return 0.5 * x * (1.0 + lax.erf(x * (2.0 ** -0.5)))


def _cmlp_kernel(u_ref, v_ref, g_ref, b_ref, ws_ref, bs_ref, o_ref):
    u = _gelu(u_ref[...])
    gv = _gelu(v_ref[...])
    mu = jnp.mean(gv, axis=-1, keepdims=True)
    dv = gv - mu
    var = jnp.mean(dv * dv, axis=-1, keepdims=True)
    vn = (dv * lax.rsqrt(var + EPS) * g_ref[...] + b_ref[...]).astype(BF16)
    lane = _lane((C_CHUNK, LANES))
    for c in range(CM_ROWS // C_CHUNK):
        rs = slice(c * C_CHUNK, (c + 1) * C_CHUNK)
        for p in range(2):
            cs = slice(p * LANES, (p + 1) * LANES)
            vp = vn[rs, cs]
            m0 = jnp.dot(ws_ref[2 * p].astype(BF16), vp, preferred_element_type=F32)
            m1 = jnp.dot(ws_ref[2 * p + 1].astype(BF16), vp, preferred_element_type=F32)
            mixed = jnp.where(lane < C_DG, m0, m1) + bs_ref[:, cs]
            o_ref[rs, cs] = u[rs, cs] * mixed


def _chunk_mlp(proj, ln_g, ln_b, w_s, bias_full):
    ucol = (5 * A_WIDTH + 3 * B_WIDTH) // C_WIDTH
    return pl.pallas_call(
        _cmlp_kernel,
        grid=(T // CM_ROWS,),
        in_specs=[
            pl.BlockSpec((CM_ROWS, C_WIDTH), lambda i: (i, ucol)),
            pl.BlockSpec((CM_ROWS, C_WIDTH), lambda i: (i, ucol + 1)),
            pl.BlockSpec((1, C_WIDTH), lambda i: (0, 0)),
            pl.BlockSpec((1, C_WIDTH), lambda i: (0, 0)),
            pl.BlockSpec((C_GROUPS, C_CHUNK, C_CHUNK), lambda i: (0, 0, 0)),
            pl.BlockSpec((C_CHUNK, C_WIDTH), lambda i: (0, 0)),
        ],
        out_specs=pl.BlockSpec((CM_ROWS, C_WIDTH), lambda i: (i, 0)),
        out_shape=jax.ShapeDtypeStruct((T, C_WIDTH), F32),
        compiler_params=_cparams(("arbitrary",)),
        name="chunk_mlp",
    )(proj, proj, ln_g.reshape(1, C_WIDTH), ln_b.reshape(1, C_WIDTH), w_s, bias_full)


PM_SUB = TM


def _postmix_kernel(*refs, n_x):
    (of_ref, ob_ref, ag_ref, hg_ref, hsel_ref, bc_ref, bs_ref, c_ref, w_ref, gate1_ref, shift2_ref, scale2_ref,
     g2_ref, wrh_ref, wrl_ref, br_ref, x1_ref, h2_ref, idx_ref, gw_ref) = refs[n_x:]
    is_ctx = pl.program_id(0) < CTX_TILES
    x_refs = refs[:n_x]
    for s in range(TM // PM_SUB):
        rows = slice(s * PM_SUB, (s + 1) * PM_SUB)
        x = x_refs[0][rows, :] if n_x == 1 else jnp.where(is_ctx, x_refs[0][rows, :], x_refs[1][rows, :])
        o = of_ref[rows, :] + ob_ref[rows, :]
        ms = _dot_sel(o * o, hsel_ref[...]) * (1.0 / A_DK)
        ag = ag_ref[rows, :]
        a = o * lax.rsqrt(ms + EPS) * hg_ref[...] * (ag * jax.nn.sigmoid(ag))
        b = jnp.where(is_ctx, bc_ref[rows, :], bs_ref[rows, :])
        mixed = jnp.dot(a.astype(BF16), w_ref[0:A_WIDTH, :], preferred_element_type=F32)
        mixed = mixed + jnp.dot(b.astype(BF16), w_ref[A_WIDTH:A_WIDTH + B_WIDTH, :], preferred_element_type=F32)
        mixed = mixed + jnp.dot(c_ref[rows, :].astype(BF16), w_ref[A_WIDTH + B_WIDTH:, :],
                                preferred_element_type=F32)
        x1 = x + gate1_ref[0] * mixed
        x1_ref[rows, :] = x1
        y = x1 * lax.rsqrt(jnp.mean(x1 * x1, axis=-1, keepdims=True) + EPS) * g2_ref[...]
        h2 = y * (1.0 + scale2_ref[0]) + shift2_ref[0]
        h2_ref[rows] = _pack_rows(h2)
        hi = h2.astype(BF16)
        lo = (h2 - hi.astype(F32)).astype(BF16)
        lg = jnp.dot(lo, wrh_ref[...], preferred_element_type=F32)
        lg = lg + jnp.dot(hi, wrl_ref[...], preferred_element_type=F32)
        lg = lg + jnp.dot(hi, wrh_ref[...], preferred_element_type=F32) + br_ref[...]
        lt = lg.T[:N_EXPERTS]
        row = lax.broadcasted_iota(jnp.int32, lt.shape, 0)
        out_row = lax.broadcasted_iota(jnp.int32, (8, PM_SUB), 0)
        idx_out = jnp.zeros((8, PM_SUB), jnp.int32)
        val_out = jnp.zeros((8, PM_SUB), F32)
        top0 = None
        den = None
        for kk in range(TOP_K):
            mx = jnp.max(lt, axis=0, keepdims=True)
            am = jnp.min(jnp.where(lt == mx, row, N_EXPERTS), axis=0, keepdims=True)
            if kk == 0:
                top0 = mx
            e = jnp.exp(mx - top0)
            den = e if den is None else den + e
            idx_out = jnp.where(out_row == kk, am, idx_out)
            val_out = jnp.where(out_row == kk, e, val_out)
            lt = jnp.where(row == am, -jnp.inf, lt)
        idx_ref[:, rows] = idx_out
        gw_ref[:, rows] = val_out / den


def _post_mix(o_f, o_b, proj, hg, hsel, b_ctx, b_smp, c_out, w_out_bf, x, mod3, g2, wr_hi, wr_lo, br_pad):
    tile = lambda w: pl.BlockSpec((TM, w), lambda i: (i, 0))
    const = lambda shape: pl.BlockSpec(shape, lambda i: tuple(0 for _ in shape))
    modspec = lambda j: pl.BlockSpec((1, 1, D), lambda i: (_mod_row(i), 0, j))
    rowsT = pl.BlockSpec((8, TM), lambda i: (0, i))
    x_specs, x_args = _stream_specs(x)
    return pl.pallas_call(
        functools.partial(_postmix_kernel, n_x=len(x_args)),
        grid=(N_TILES,),
        in_specs=x_specs + [
            tile(A_WIDTH), tile(A_WIDTH),
            pl.BlockSpec((TM, A_WIDTH), lambda i: (i, 4)),
            const((1, A_WIDTH)), const((A_WIDTH, A_WIDTH)),
            pl.BlockSpec((TM, B_WIDTH), lambda i: (jnp.minimum(i, CTX_TILES - 1), 0)),
            pl.BlockSpec((TM, B_WIDTH), lambda i: (jnp.maximum(i - CTX_TILES, 0), 0)),
            tile(C_WIDTH),
            const((D, D)),
            modspec(2), modspec(3), modspec(4),
            const((1, D)), const((D, LANES)), const((D, LANES)), const((1, LANES)),
        ],
        out_specs=[tile(D), pl.BlockSpec((TM, ROW_TILES, LANES), lambda i: (i, 0, 0)), rowsT, rowsT],
        out_shape=[
            jax.ShapeDtypeStruct((T, D), F32),
            jax.ShapeDtypeStruct((T, ROW_TILES, LANES), ROW_DT),
            jax.ShapeDtypeStruct((8, T), jnp.int32),
            jax.ShapeDtypeStruct((8, T), F32),
        ],
        compiler_params=_cparams(("arbitrary",)),
        name="post_mix_router",
    )(*x_args, o_f, o_b, proj, hg, hsel, b_ctx, b_smp, c_out, w_out_bf, mod3, mod3, mod3, g2.reshape(1, D),
      wr_hi, wr_lo, br_pad)


def _route_kernel(idx_ref, dest_ref, meta_ref):
    erow = lax.broadcasted_iota(jnp.int32, (N_EXPERTS, TM), 0)
    s_i = lax.broadcasted_iota(jnp.int32, (TM, TM), 0)
    t_i = lax.broadcasted_iota(jnp.int32, (TM, TM), 1)
    earlier = (s_i < t_i).astype(BF16)
    out_row = lax.broadcasted_iota(jnp.int32, (8, TM), 0)

    def onehots(i):
        idx = idx_ref[:, pl.ds(pl.multiple_of(i * TM, TM), TM)]
        return [(erow == idx[kk:kk + 1, :]) for kk in range(TOP_K)]

    def count_tile(i, run):
        ohs = onehots(i)
        base = run
        pos = jnp.zeros((8, TM), F32)
        for kk in range(TOP_K):
            ohf = ohs[kk].astype(F32)
            before = jnp.dot(ohs[kk].astype(BF16), earlier, preferred_element_type=F32)
            p = jnp.sum(ohf * (base + before), axis=0, keepdims=True)
            pos = jnp.where(out_row == kk, p, pos)
            base = base + jnp.sum(ohf, axis=1, keepdims=True)
        dest_ref[:, pl.ds(pl.multiple_of(i * TM, TM), TM)] = pos.astype(jnp.int32)
        return base

    counts = lax.fori_loop(0, N_TILES, count_tile, jnp.zeros((N_EXPERTS, 1), F32)).astype(jnp.int32)
    bm_shift = MOE_BM.bit_length() - 1
    padded = lax.shift_left(lax.shift_right_logical(counts + (MOE_BM - 1), bm_shift), bm_shift)
    e_r = lax.broadcasted_iota(jnp.int32, (N_EXPERTS, N_EXPERTS), 0)
    e_c = lax.broadcasted_iota(jnp.int32, (N_EXPERTS, N_EXPERTS), 1)
    incl = (e_c <= e_r).astype(BF16)
    pad_end = _sel_dot(incl, jnp.broadcast_to(padded.astype(F32), (N_EXPERTS, LANES)))[:, :1]
    pad_start = pad_end - padded.astype(F32)

    def place_tile(i, carry):
        ohs = onehots(i)
        sl = pl.ds(pl.multiple_of(i * TM, TM), TM)
        off = jnp.zeros((8, TM), F32)
        for kk in range(TOP_K):
            o = jnp.sum(ohs[kk].astype(F32) * pad_start, axis=0, keepdims=True)
            off = jnp.where(out_row == kk, o, off)
        dest_ref[:, sl] = dest_ref[:, sl] + off.astype(jnp.int32)
        return carry

    lax.fori_loop(0, N_TILES, place_tile, 0)

    total = jnp.max(pad_end, axis=0, keepdims=True)
    lane_i = lax.broadcasted_iota(jnp.int32, (1, TM), 1)
    blk0 = (lane_i * MOE_BM).astype(F32)
    block_e = jnp.sum((pad_end <= blk0).astype(F32), axis=0, keepdims=True)
    live_end = pad_start + counts.astype(F32)
    sel = erow.astype(F32) == block_e
    live = jnp.sum(jnp.where(sel, live_end, 0.0), axis=0, keepdims=True)
    valid = jnp.where(blk0 < total, jnp.clip(live - blk0, 0.0, float(MOE_BM)), 0.0)
    own = erow == lane_i
    n_blk = jnp.sum(jnp.where(own, padded.astype(F32), 0.0), axis=0, keepdims=True) * (1.0 / MOE_BM)
    first_blk = jnp.sum(jnp.where(own, pad_start, 0.0), axis=0, keepdims=True) * (1.0 / MOE_BM)
    meta = jnp.where(out_row == 0, valid, 0.0)
    meta = jnp.where(out_row == 1, n_blk, meta)
    meta = jnp.where(out_row == 2, first_blk, meta)
    meta = jnp.where(out_row == 3, total * (1.0 / MOE_BM), meta)
    meta_ref[...] = meta.astype(jnp.int32)


def _route(idx_t):
    assert MOE_BLOCKS <= TM
    return pl.pallas_call(
        _route_kernel,
        out_shape=[jax.ShapeDtypeStruct((8, T), jnp.int32), jax.ShapeDtypeStruct((8, TM), jnp.int32)],
        compiler_params=pltpu.CompilerParams(vmem_limit_bytes=VMEM_LIMIT),
        name="moe_route",
    )(idx_t)


def _moe_kernel(bv_ref, nb_ref, g0_ref, tot_ref, x_hbm, wgu_ref, bgu_ref, wdn_ref, bdn_ref, y_hbm,
                wgu_bf, wdn_bf, xbuf, ybuf, xb_ref, xsem, ysem):
    e = pl.program_id(0)
    n_blk = nb_ref[e]
    first = g0_ref[e]
    total = tot_ref[0]
    ahead = MOE_RING - 1

    def x_copy(g):
        slot = g % MOE_RING
        return pltpu.make_async_copy(x_hbm.at[pl.ds(g * MOE_BM, MOE_BM)], xbuf.at[slot], xsem.at[slot])

    def y_copy(g):
        slot = g % MOE_RING
        return pltpu.make_async_copy(ybuf.at[slot], y_hbm.at[pl.ds(g * MOE_BM, MOE_BM)], ysem.at[slot])

    @pl.when(e == 0)
    def _():
        for g in range(ahead):
            @pl.when(g < total)
            def _():
                x_copy(g).start()

    @pl.when(n_blk > 0)
    def _():
        wgu_bf[...] = wgu_ref[0, 0].astype(BF16)
        wdn_bf[...] = wdn_ref[0, 0].astype(BF16)

    def block(j, carry):
        g = first + j
        slot = g % MOE_RING
        x_copy(g).wait()

        @pl.when(g + ahead < total)
        def _():
            x_copy(g + ahead).start()

        @pl.when(g >= MOE_RING)
        def _():
            y_copy(g - MOE_RING).wait()

        live = lax.broadcasted_iota(jnp.int32, (MOE_BM, LANES), 0) < bv_ref[g]
        for c, chunk in enumerate(_unpack_rows(xbuf[slot])):
            xb_ref[:, c * LANES:(c + 1) * LANES] = jnp.where(live, chunk, 0.0).astype(BF16)
        gu = jnp.dot(xb_ref[...], wgu_bf[...], preferred_element_type=F32) + bgu_ref[0, 0]
        glu = jnp.minimum(gu[:, :D], SWIGLU_LIMIT)
        lin = jnp.clip(gu[:, D:], -SWIGLU_LIMIT, SWIGLU_LIMIT)
        act = glu * jax.nn.sigmoid(SWIGLU_ALPHA * glu) * (lin + 1.0)
        y = jnp.dot(act.astype(BF16), wdn_bf[...], preferred_element_type=F32) + bdn_ref[0, 0]
        ybuf[slot] = _pack_rows(y)
        y_copy(g).start()
        return carry

    lax.fori_loop(0, n_blk, block, 0)

    @pl.when(e == N_EXPERTS - 1)
    def _():
        for back in range(MOE_RING, 0, -1):
            @pl.when(total >= back)
            def _():
                y_copy(total - back).wait()

        def fill(g, carry):
            ybuf[g % MOE_RING] = jnp.zeros((MOE_BM, ROW_TILES, LANES), ROW_DT)
            y_copy(g).start()
            y_copy(g).wait()
            return carry

        lax.fori_loop(total, MOE_BLOCKS, fill, 0)


def _moe_ffn(block_valid, n_blk, first_blk, total_blk, xs, w_gu, b_gu, w_dn, b_dn, layer):
    rows = (MOE_BM, ROW_TILES, LANES)
    return pl.pallas_call(
        _moe_kernel,
        grid_spec=pltpu.PrefetchScalarGridSpec(
            num_scalar_prefetch=4,
            grid=(N_EXPERTS,),
            in_specs=[
                pl.BlockSpec(memory_space=pl.ANY),
                pl.BlockSpec((1, 1, D, 2 * D), lambda e, *_: (layer, e, 0, 0)),
                pl.BlockSpec((1, 1, 1, 2 * D), lambda e, *_: (layer, e, 0, 0)),
                pl.BlockSpec((1, 1, D, D), lambda e, *_: (layer, e, 0, 0)),
                pl.BlockSpec((1, 1, 1, D), lambda e, *_: (layer, e, 0, 0)),
            ],
            out_specs=pl.BlockSpec(memory_space=pl.ANY),
            scratch_shapes=[
                pltpu.VMEM((D, 2 * D), BF16), pltpu.VMEM((D, D), BF16),
                pltpu.VMEM((MOE_RING,) + rows, ROW_DT), pltpu.VMEM((MOE_RING,) + rows, ROW_DT),
                pltpu.VMEM((MOE_BM, D), BF16),
                pltpu.SemaphoreType.DMA((MOE_RING,)), pltpu.SemaphoreType.DMA((MOE_RING,)),
            ],
        ),
        out_shape=jax.ShapeDtypeStruct((MOE_ROWS, ROW_TILES, LANES), ROW_DT),
        compiler_params=_cparams(("arbitrary",)),
        name="moe_expert_ffn",
    )(block_valid, n_blk, first_blk, total_blk, xs, w_gu, b_gu.reshape(DEPTH, N_EXPERTS, 1, 2 * D), w_dn,
      b_dn.reshape(DEPTH, N_EXPERTS, 1, D))


def _combine_kernel(x1_ref, y_ref, gw_ref, gate2_ref, o_ref):
    gw = jnp.concatenate([gw_ref[...], jnp.zeros((LANES - 8, TM), F32)], axis=0).T
    ys = [_unpack_rows(y_ref[kk]) for kk in range(TOP_K)]
    for c in range(D // LANES):
        cs = slice(c * LANES, (c + 1) * LANES)
        acc = None
        for kk in range(TOP_K):
            term = ys[kk][c] * gw[:, kk:kk + 1]
            acc = term if acc is None else acc + term
        o_ref[:, cs] = x1_ref[:, cs] + gate2_ref[0, :, cs] * acc


def _combine(x1, yg, gw, mod3, tile0, n_tiles):
    return pl.pallas_call(
        _combine_kernel,
        grid=(n_tiles,),
        in_specs=[
            pl.BlockSpec((TM, D), lambda i: (tile0 + i, 0)),
            pl.BlockSpec((TOP_K, TM, ROW_TILES, LANES), lambda i: (0, tile0 + i, 0, 0)),
            pl.BlockSpec((8, TM), lambda i: (0, tile0 + i)),
            pl.BlockSpec((1, 1, D), lambda i: (_mod_row(tile0 + i), 0, 5)),
        ],
        out_specs=pl.BlockSpec((TM, D), lambda i: (i, 0)),
        out_shape=jax.ShapeDtypeStruct((n_tiles * TM, D), F32),
        compiler_params=_cparams(("arbitrary",)),
        name="moe_combine",
    )(x1, yg, gw, mod3)


def _sc_mesh():
    return plsc.VectorSubcoreMesh(core_axis_name="c", subcore_axis_name="s")


def _sc_worker():
    return lax.axis_index("s") * SC_CORES + lax.axis_index("c")


def _sc_dispatch(h2t, dest_km):
    per_w = T // SC_WORKERS

    @functools.partial(
        pl.kernel, mesh=_sc_mesh(),
        out_type=jax.ShapeDtypeStruct((MOE_ROWS, ROW_TILES, LANES), ROW_DT),
        scratch_types=[pltpu.VMEM((SC_WIN,), jnp.int32), pltpu.VMEM((SC_WIN, ROW_TILES, LANES), ROW_DT),
                       pltpu.SemaphoreType.DMA],
    )
    def run(h_hbm, d_hbm, o_hbm, idx_v, rows_v, sem):
        w0 = _sc_worker() * per_w

        @pl.loop(0, per_w // SC_WIN)
        def _(w):
            base = pl.multiple_of(w0 + w * SC_WIN, SC_WIN)
            pltpu.sync_copy(h_hbm.at[pl.ds(base, SC_WIN)], rows_v)
            for kk in range(TOP_K):
                pltpu.sync_copy(d_hbm.at[pl.ds(kk * T + base, SC_WIN)], idx_v)
                pltpu.async_copy(rows_v, o_hbm.at[idx_v], sem).wait()

    return run(h2t, dest_km)


def _sc_gather(yb, dest_km):
    n = TOP_K * T
    per_w = n // SC_WORKERS

    @functools.partial(
        pl.kernel, mesh=_sc_mesh(),
        out_type=jax.ShapeDtypeStruct((n, ROW_TILES, LANES), ROW_DT),
        scratch_types=[pltpu.VMEM((SC_WIN,), jnp.int32), pltpu.VMEM((SC_WIN, ROW_TILES, LANES), ROW_DT),
                       pltpu.SemaphoreType.DMA],
    )
    def run(y_hbm, d_hbm, o_hbm, idx_v, rows_v, sem):
        w0 = _sc_worker() * per_w

        @pl.loop(0, per_w // SC_WIN)
        def _(w):
            base = pl.multiple_of(w0 + w * SC_WIN, SC_WIN)
            pltpu.sync_copy(d_hbm.at[pl.ds(base, SC_WIN)], idx_v)
            pltpu.async_copy(y_hbm.at[idx_v], rows_v, sem).wait()
            pltpu.sync_copy(rows_v, o_hbm.at[pl.ds(base, SC_WIN)])

    return run(yb, dest_km)


def kernel(x_prompt, x_sample, c, cache_diff_k, cache_diff_v, state_hgrn, c_ctx, norm_mix_g, norm_ffn_g, w_mod, b_mod, w_in, w_out, hgrn_lower_bounds, hgrn_norm_g, diff_q_norm_g, diff_k_norm_g, diff_lambda_q1, diff_lambda_k1, diff_lambda_q2, diff_lambda_k2, diff_subln_g, cmlp_ln_g, cmlp_ln_b, cmlp_w_s, cmlp_b_s, router_w, router_b, moe_w_gate_up, moe_b_gate_up, moe_w_down, moe_b_down):
    x = (x_prompt.reshape(T_CTX, D), x_sample.reshape(T_SMP, D))
    cvec = jnp.concatenate([c_ctx[None, :], c, jnp.zeros((MOD_ROWS - 1 - DEC_BATCH, D), F32)], axis=0)
    mod = _modulation(cvec, w_mod, b_mod)

    lvl_np, tri_np = _hgrn_tables()
    lvl = jnp.asarray(lvl_np)
    tri = jnp.asarray(tri_np, dtype=BF16)
    cos, sin = _rope_tables()
    hsel = jnp.asarray(np.kron(np.eye(A_HEADS), np.ones((A_DK, A_DK))), dtype=BF16)
    sm = jax.nn.softmax(hgrn_lower_bounds.astype(F32), axis=0)
    lb_all = jnp.cumsum(sm, axis=0) - sm[0]

    new_k, new_v, new_s = None, None, []
    for l in range(DEPTH):
        mod3 = mod[l].reshape(MOD_ROWS, 1, 6 * D)
        proj = _in_projection(x, norm_mix_g[l], mod3, w_in[l].astype(BF16))

        s0 = jnp.concatenate([jnp.zeros((BATCH, 2, A_HEADS, A_DK, A_DK), F32), state_hgrn[:, l]], axis=0)
        o_dir, fin_dir = [], []
        for d in range(2):
            o_d, fin_d = _hgrn_scan(proj, lb_all[l, d].reshape(1, A_WIDTH), _pack_state(s0[:, d]),
                                    lvl, tri, l, d == 1)
            o_dir.append(o_d)
            fin_dir.append(_unpack_state(fin_d[:BATCH]))
        new_s.append(jnp.stack(fin_dir, axis=1))

        lam_init = 0.8 - 0.6 * math.exp(-0.3 * l)
        lam = (jnp.exp(jnp.sum(diff_lambda_q1[l] * diff_lambda_k1[l]))
               - jnp.exp(jnp.sum(diff_lambda_q2[l] * diff_lambda_k2[l])) + lam_init).reshape(1, 1)
        gq2 = jnp.tile(diff_q_norm_g[l], 2).reshape(1, LANES)
        gk2 = jnp.tile(diff_k_norm_g[l], 2).reshape(1, LANES)
        gs = diff_subln_g[l].reshape(1, LANES)
        b_ctx, new_k, new_v = _attn_ctx(proj, lam, gq2, gk2, gs, lam_init, l, new_k, new_v)
        b_smp = _attn_smp(proj, lam, cache_diff_k, cache_diff_v, cos, sin, gq2, gk2, gs, l, lam_init)

        bias_full = jnp.repeat(cmlp_b_s[l].T, C_DG, axis=1)
        c_out = _chunk_mlp(proj, cmlp_ln_g[l], cmlp_ln_b[l], cmlp_w_s[l], bias_full)

        hg = jnp.tile(hgrn_norm_g[l], A_HEADS).reshape(1, A_WIDTH)
        wr_pad = jnp.pad(router_w[l], ((0, 0), (0, LANES - N_EXPERTS)))
        wr_hi = wr_pad.astype(BF16)
        wr_lo = (wr_pad - wr_hi.astype(F32)).astype(BF16)
        br_pad = jnp.pad(router_b[l], (0, LANES - N_EXPERTS)).reshape(1, LANES)
        x1, h2, idx_t, gw_t = _post_mix(o_dir[0], o_dir[1], proj, hg, hsel, b_ctx, b_smp, c_out,
                                        w_out[l].astype(BF16), x, mod3, norm_ffn_g[l], wr_hi, wr_lo, br_pad)

        dest_t, meta = _route(idx_t)
        dest_km = dest_t[:TOP_K].reshape(-1)
        xs = _sc_dispatch(h2, dest_km)
        yb = _moe_ffn(meta[0, :MOE_BLOCKS], meta[1, :N_EXPERTS], meta[2, :N_EXPERTS], meta[3, :1], xs,
                      moe_w_gate_up, moe_b_gate_up, moe_w_down, moe_b_down, l)
        yg = _sc_gather(yb, dest_km).reshape(TOP_K, T, ROW_TILES, LANES)
        x = (_combine(x1, yg, gw_t, mod3, 0, CTX_TILES),
             _combine(x1, yg, gw_t, mod3, CTX_TILES, N_TILES - CTX_TILES))

    y_prompt = x[0].reshape(BATCH, SEQ, D)
    y_sample = x[1].reshape(DEC_BATCH, DEC_SEQ, D)
    return (y_prompt, y_sample, new_k, new_v, jnp.stack(new_s, axis=1))
```

```python
import functools
import math

import numpy as np
import jax
import jax.numpy as jnp
from jax import lax
from jax.experimental import pallas as pl
from jax.experimental.pallas import tpu as pltpu
from jax.experimental.pallas import tpu_sc as plsc

F32 = jnp.float32
BF16 = jnp.bfloat16

D = 1024
DEPTH = 2
BATCH, SEQ = 16, 256
DEC_BATCH, DEC_SEQ = 8, 1024
PAST = 512
GRID_W = 64
A_HEADS, A_DK = 4, 64
A_WIDTH = 256
B_HEADS, B_DK, B_DV = 4, 64, 128
B_WIDTH = 512
C_GROUPS, C_CHUNK, C_WIDTH, C_DG = 4, 128, 256, 64
IN_WIDTH = 5 * A_WIDTH + 3 * B_WIDTH + 2 * C_WIDTH
N_EXPERTS, TOP_K = 32, 4
SWIGLU_LIMIT, SWIGLU_ALPHA = 7.0, 1.702
ROPE_BASE = 10000.0
EPS = 1e-6

T_CTX = BATCH * SEQ
T_SMP = DEC_BATCH * DEC_SEQ
T = T_CTX + T_SMP
N_SEQ = BATCH + DEC_BATCH
MOD_ROWS = 16

TM = 256
N_TILES = T // TM
CTX_TILES = T_CTX // TM
SMP_TILES_PER_SEQ = DEC_SEQ // TM
LANES = 128
MOE_BM = 256
MOE_ROWS = T * TOP_K + N_EXPERTS * MOE_BM
MOE_BLOCKS = MOE_ROWS // MOE_BM
MOE_RING = 4
ROW_WORDS = D // 2
ROW_TILES = ROW_WORDS // LANES
ROW_DT = jnp.int32
SC_CORES, SC_SUBCORES = 2, 16
SC_WORKERS = SC_CORES * SC_SUBCORES
SC_WIN = 128
VMEM_LIMIT = 56 * 1024 * 1024


def _cparams(sem):
    return pltpu.CompilerParams(dimension_semantics=sem, vmem_limit_bytes=VMEM_LIMIT)


def _mod_row(i):
    return jnp.where(i < CTX_TILES, 0, 1 + (i - CTX_TILES) // SMP_TILES_PER_SEQ)


def _split3(x):
    hi = x.astype(BF16)
    r = x - hi.astype(F32)
    mid = r.astype(BF16)
    lo = (r - mid.astype(F32)).astype(BF16)
    return hi, mid, lo


def _sel_dot(sel, x):
    hi, mid, lo = _split3(x)
    acc = jnp.dot(sel, lo, preferred_element_type=F32)
    acc = acc + jnp.dot(sel, mid, preferred_element_type=F32)
    return acc + jnp.dot(sel, hi, preferred_element_type=F32)


def _dot_sel(x, sel):
    hi, mid, lo = _split3(x)
    acc = jnp.dot(lo, sel, preferred_element_type=F32)
    acc = acc + jnp.dot(mid, sel, preferred_element_type=F32)
    return acc + jnp.dot(hi, sel, preferred_element_type=F32)


def _dot_nt(a, b):
    return lax.dot_general(a, b, (((1,), (1,)), ((), ())), preferred_element_type=F32)


def _dot_tn(a, b):
    return lax.dot_general(a, b, (((0,), (0,)), ((), ())), preferred_element_type=F32)


def _lane(shape):
    return lax.broadcasted_iota(jnp.int32, shape, len(shape) - 1)


def _pack_rows(x):
    hi = lax.bitcast_convert_type(x[:, :ROW_WORDS].astype(BF16).astype(F32), jnp.int32)
    lo = lax.bitcast_convert_type(x[:, ROW_WORDS:].astype(BF16).astype(F32), jnp.int32)
    words = hi | lax.shift_right_logical(lo, 16)
    return pltpu.einshape("t(jl)->tjl", words, l=LANES)


def _unpack_rows(words3):
    wt = pltpu.einshape("tjl->jtl", words3)
    hi = [lax.bitcast_convert_type(wt[j] & jnp.int32(-65536), F32) for j in range(ROW_TILES)]
    lo = [lax.bitcast_convert_type(lax.shift_left(wt[j], 16), F32) for j in range(ROW_TILES)]
    return hi + lo


def _mod_kernel(c_ref, w_ref, b_ref, o_ref):
    c = c_ref[...]
    s = c * jax.nn.sigmoid(c)
    o_ref[0] = jnp.dot(s.astype(BF16), w_ref[0].astype(BF16), preferred_element_type=F32) + b_ref[0]


def _modulation(cvec, w_mod, b_mod):
    tn = 1536
    return pl.pallas_call(
        _mod_kernel,
        grid=(DEPTH, 6 * D // tn),
        in_specs=[
            pl.BlockSpec((MOD_ROWS, D), lambda l, j: (0, 0)),
            pl.BlockSpec((1, D, tn), lambda l, j: (l, 0, j)),
            pl.BlockSpec((1, 1, tn), lambda l, j: (l, 0, j)),
        ],
        out_specs=pl.BlockSpec((1, MOD_ROWS, tn), lambda l, j: (l, 0, j)),
        out_shape=jax.ShapeDtypeStruct((DEPTH, MOD_ROWS, 6 * D), F32),
        compiler_params=_cparams(("arbitrary", "arbitrary")),
        name="modulation",
    )(cvec, w_mod, b_mod.reshape(DEPTH, 1, 6 * D))


def _stream_specs(x):
    if isinstance(x, tuple):
        return [pl.BlockSpec((TM, D), lambda i: (jnp.minimum(i, CTX_TILES - 1), 0)),
                pl.BlockSpec((TM, D), lambda i: (jnp.maximum(i - CTX_TILES, 0), 0))], list(x)
    return [pl.BlockSpec((TM, D), lambda i: (i, 0))], [x]


def _stream_tile(x_refs):
    if len(x_refs) == 1:
        return x_refs[0][...]
    return jnp.where(pl.program_id(0) < CTX_TILES, x_refs[0][...], x_refs[1][...])


def _inproj_kernel(*refs, n_x):
    g_ref, shift_ref, scale_ref, w_ref, o_ref = refs[n_x:]
    x = _stream_tile(refs[:n_x])
    y = x * lax.rsqrt(jnp.mean(x * x, axis=-1, keepdims=True) + EPS) * g_ref[...]
    h = y * (1.0 + scale_ref[0]) + shift_ref[0]
    o_ref[...] = jnp.dot(h.astype(BF16), w_ref[...], preferred_element_type=F32)


def _in_projection(x, g, mod3, w_in_bf):
    x_specs, x_args = _stream_specs(x)
    return pl.pallas_call(
        functools.partial(_inproj_kernel, n_x=len(x_args)),
        grid=(N_TILES,),
        in_specs=x_specs + [
            pl.BlockSpec((1, D), lambda i: (0, 0)),
            pl.BlockSpec((1, 1, D), lambda i: (_mod_row(i), 0, 0)),
            pl.BlockSpec((1, 1, D), lambda i: (_mod_row(i), 0, 1)),
            pl.BlockSpec((D, IN_WIDTH), lambda i: (0, 0)),
        ],
        out_specs=pl.BlockSpec((TM, IN_WIDTH), lambda i: (i, 0)),
        out_shape=jax.ShapeDtypeStruct((T, IN_WIDTH), F32),
        compiler_params=_cparams(("arbitrary",)),
        name="in_projection",
    )(*x_args, g.reshape(1, D), mod3, mod3, w_in_bf)


HG_C = 128
HG_LEVELS = tuple(2 ** j for j in range(1, int(math.log2(HG_C)) + 1))


def _hgrn_tables():
    t = np.arange(HG_C)[:, None]
    s = np.arange(HG_C)[None, :]
    x = t ^ s
    lvl = np.zeros((HG_C, HG_C), np.int32)
    nz = x > 0
    lvl[nz] = np.floor(np.log2(x[nz])).astype(np.int32) + 1
    fwd = np.where(t >= s, lvl, -1).astype(np.int32)
    bwd = np.where(t <= s, lvl, -1).astype(np.int32)
    tri_f = (t >= s).astype(np.float32)
    tri_b = (t <= s).astype(np.float32)
    return np.stack([fwd, bwd]), np.stack([tri_f, tri_b])


def _block_ref(cum, m, idx):
    c, l = cum.shape
    if m >= 16:
        c3 = cum.reshape(c // m, m, l)
        r = c3[:, idx:idx + 1, :]
        return jnp.broadcast_to(r, (c // m, m, l)).reshape(c, l)
    c3 = cum.reshape(c // 8, 8, l)
    sub = lax.broadcasted_iota(jnp.int32, c3.shape, 1)
    out = None
    for j in range(8 // m - 1, -1, -1):
        cand = jnp.broadcast_to(c3[:, j * m + idx:j * m + idx + 1, :], c3.shape)
        out = cand if out is None else jnp.where(sub < (j + 1) * m, cand, out)
    return out.reshape(c, l)


def _hgrn_kernel(q_ref, z_ref, v_ref, lb_ref, s0_ref, lvl_ref, tri_ref, o_ref, fin_ref, st_ref, *, layer, rev):
    g = pl.program_id(0)
    first = jnp.logical_or(g < CTX_TILES, (g - CTX_TILES) % SMP_TILES_PER_SEQ == 0)

    @pl.when(first)
    def _():
        st_ref[...] = s0_ref[0]

    qr = q_ref[...]
    q = qr * jax.nn.sigmoid(qr) * (A_DK ** -0.5)
    z = z_ref[...]
    if layer == 0:
        lf = jnp.minimum(z, 0.0) - jnp.log(1.0 + jnp.exp(-jnp.abs(z)))
        k = jax.nn.sigmoid(-z)
    else:
        lbd = lb_ref[...]
        lf = jnp.log(lbd + (1.0 - lbd) * jax.nn.sigmoid(z))
        k = (1.0 - lbd) * jax.nn.sigmoid(-z)
    v = v_ref[...]
    tri = tri_ref[0]
    lvl = lvl_ref[0]
    last_row = 0 if rev else HG_C - 1
    n_chunks = TM // HG_C
    order = range(n_chunks - 1, -1, -1) if rev else range(n_chunks)
    lf2 = lf * math.log2(math.e)
    cums = [_sel_dot(tri, lf2[c * HG_C:(c + 1) * HG_C]) for c in range(n_chunks)]
    lane = _lane((HG_C, LANES))
    head_masks = (lane < A_DK, lane >= A_DK)
    lane_row = _lane((1, LANES))
    head_keep = ((lane_row < A_DK).astype(BF16), (lane_row >= A_DK).astype(BF16))
    lvl2 = jnp.concatenate([lvl, lvl], axis=0)
    level_masks = [lvl2 == i for i in range(len(HG_LEVELS) + 1)]
    r = lax.broadcasted_iota(jnp.int32, (LANES, LANES), 0)
    cl = lax.broadcasted_iota(jnp.int32, (LANES, LANES), 1)
    same_head = (r < A_DK) == (cl < A_DK)

    def chunk(q_p, k_p, v_p, cum_p, st):
        v_bf = v_p.astype(BF16)
        k_bf = k_p.astype(BF16)
        q_bf = q_p.astype(BF16)

        def both_heads(x_bf):
            return jnp.concatenate([x_bf * head_keep[0], x_bf * head_keep[1]], axis=0)

        scores = jnp.where(level_masks[0], _dot_nt(both_heads(q_bf), k_bf), 0.0)
        for li, m in enumerate(HG_LEVELS):
            ref = _block_ref(cum_p, m, m // 2 if rev else m // 2 - 1)
            dec = jnp.exp2(-jnp.abs(cum_p - ref))
            qd = (q_p * dec).astype(BF16)
            kd = (k_p * dec).astype(BF16)
            scores = jnp.where(level_masks[li + 1], _dot_nt(both_heads(qd), kd), scores)
        pv = jnp.dot(scores.astype(BF16), v_bf, preferred_element_type=F32)
        o_intra = jnp.where(head_masks[0], pv[:HG_C], pv[HG_C:])
        q0 = (q_p * jnp.exp2(cum_p)).astype(BF16)
        out = o_intra + _dot_nt(q0, st.astype(BF16))
        last = cum_p[last_row:last_row + 1, :]
        ks = (k_p * jnp.exp2(last - cum_p)).astype(BF16)
        upd = _dot_tn(v_bf, ks)
        return out, st * jnp.exp2(last) + jnp.where(same_head, upd, 0.0)

    for p in range(2):
        sl = slice(p * LANES, (p + 1) * LANES)
        st = st_ref[p]
        for c in order:
            rows = slice(c * HG_C, (c + 1) * HG_C)
            o_ref[rows, sl], st = chunk(q[rows, sl], k[rows, sl], v[rows, sl], cums[c][:, sl], st)
        st_ref[p] = st
        fin_ref[0, p] = st


def _hgrn_seq(g):
    return jnp.where(g < CTX_TILES, g, CTX_TILES + (g - CTX_TILES) // SMP_TILES_PER_SEQ)


def _hgrn_blk(g, rev):
    if not rev:
        return g
    j = g - CTX_TILES
    return jnp.where(g < CTX_TILES, g,
                     CTX_TILES + (j // SMP_TILES_PER_SEQ) * SMP_TILES_PER_SEQ
                     + (SMP_TILES_PER_SEQ - 1 - j % SMP_TILES_PER_SEQ))


def _hgrn_scan(proj, lb_dir, s0_dir, lvl, tri, layer, rev):
    d = 1 if rev else 0
    blk = functools.partial(_hgrn_blk, rev=rev)
    return pl.pallas_call(
        functools.partial(_hgrn_kernel, layer=layer, rev=rev),
        grid=(N_TILES,),
        in_specs=[
            pl.BlockSpec((TM, A_WIDTH), lambda g: (blk(g), 0)),
            pl.BlockSpec((TM, A_WIDTH), lambda g: (blk(g), 1 + d)),
            pl.BlockSpec((TM, A_WIDTH), lambda g: (blk(g), 3)),
            pl.BlockSpec((1, A_WIDTH), lambda g: (0, 0)),
            pl.BlockSpec((1, 2, LANES, LANES), lambda g: (_hgrn_seq(g), 0, 0, 0)),
            pl.BlockSpec((1, HG_C, HG_C), lambda g: (d, 0, 0)),
            pl.BlockSpec((1, HG_C, HG_C), lambda g: (d, 0, 0)),
        ],
        out_specs=[
            pl.BlockSpec((TM, A_WIDTH), lambda g: (blk(g), 0)),
            pl.BlockSpec((1, 2, LANES, LANES), lambda g: (_hgrn_seq(g), 0, 0, 0)),
        ],
        out_shape=[
            jax.ShapeDtypeStruct((T, A_WIDTH), F32),
            jax.ShapeDtypeStruct((N_SEQ, 2, LANES, LANES), F32),
        ],
        scratch_shapes=[pltpu.VMEM((2, LANES, LANES), F32)],
        compiler_params=_cparams(("arbitrary",)),
        name=f"hgrn_scan_{'bwd' if rev else 'fwd'}",
    )(proj, proj, proj, lb_dir, s0_dir, lvl, tri)


def _pack_state(s):
    n = s.shape[0]
    st = jnp.swapaxes(s, -1, -2).reshape(n, 2, 2, A_DK, A_DK)
    z = jnp.zeros_like(st[:, :, 0])
    top = jnp.concatenate([st[:, :, 0], z], axis=-1)
    bot = jnp.concatenate([z, st[:, :, 1]], axis=-1)
    return jnp.concatenate([top, bot], axis=-2)


def _unpack_state(sp):
    n = sp.shape[0]
    h0 = sp[:, :, :A_DK, :A_DK]
    h1 = sp[:, :, A_DK:, A_DK:]
    st = jnp.stack([h0, h1], axis=2).reshape(n, A_HEADS, A_DK, A_DK)
    return jnp.swapaxes(st, -1, -2)


def _half_rms(x, g):
    r = lax.broadcasted_iota(jnp.int32, (LANES, LANES), 0)
    c = lax.broadcasted_iota(jnp.int32, (LANES, LANES), 1)
    half_mean = jnp.where((r < B_DK) == (c < B_DK), 1.0 / B_DK, 0.0).astype(BF16)
    xx = x * x
    hi = xx.astype(BF16)
    lo = (xx - hi.astype(F32)).astype(BF16)
    ms = jnp.dot(lo, half_mean, preferred_element_type=F32) + jnp.dot(hi, half_mean, preferred_element_type=F32)
    return x * lax.rsqrt(ms + EPS) * g


def _rope(x, cos, sin_signed):
    lane = _lane(x.shape)
    first = (lane % 32) < 16
    rot = jnp.where(first, pltpu.roll(x, LANES - 16, 1), pltpu.roll(x, 16, 1))
    return x * cos + rot * sin_signed


def _with_ones(v_bf):
    return jnp.concatenate([v_bf, jnp.ones_like(v_bf)], axis=-1)


def _diff_softmax_pv(q_bf, keys_bf, vals_ext, lam):
    lane = _lane(q_bf.shape)
    zero = jnp.zeros_like(q_bf)
    outs = []
    for mp in range(2):
        qm = jnp.where((lane < B_DK) == (mp == 0), q_bf, zero)
        s = [_dot_nt(qm, kk) for kk in keys_bf]
        mx = functools.reduce(jnp.maximum, [jnp.max(si, axis=-1, keepdims=True) for si in s])
        acc = None
        for si, ve in zip(s, vals_ext):
            e = jnp.exp((si - mx).astype(BF16))
            pv = jnp.dot(e, ve, preferred_element_type=F32)
            acc = pv if acc is None else acc + pv
        outs.append(acc[:, :B_DV] / acc[:, B_DV:])
    return outs[0] - lam * outs[1]


def _subln(o, g, lam_init):
    return o * lax.rsqrt(jnp.mean(o * o, axis=-1, keepdims=True) + EPS) * g * (1.0 - lam_init)


def _attn_ctx_kernel(lam_ref, *refs, lam_init, layer):
    q_refs, k_refs, v_refs = refs[:B_HEADS], refs[B_HEADS:2 * B_HEADS], refs[2 * B_HEADS:3 * B_HEADS]
    gq_ref, gk_ref, gs_ref = refs[3 * B_HEADS:3 * B_HEADS + 3]
    rest = refs[3 * B_HEADS + 3:]
    if layer:
        pk_ref, pv_ref, o_ref, nk_ref, nv_ref = rest
        nk_ref[0, :layer] = pk_ref[0]
        nv_ref[0, :layer] = pv_ref[0]
    else:
        o_ref, nk_ref, nv_ref = rest
    lam = lam_ref[0, 0]
    for h in range(B_HEADS):
        qn = _half_rms(q_refs[h][...], gq_ref[...]) * (B_DK ** -0.5)
        kn = _half_rms(k_refs[h][...], gk_ref[...])
        v = v_refs[h][...]
        nk_ref[0, layer, 0, h] = kn[:, :B_DK]
        nk_ref[0, layer, 1, h] = kn[:, B_DK:]
        nv_ref[0, layer, h] = v
        o = _diff_softmax_pv(qn.astype(BF16), [kn.astype(BF16)], [_with_ones(v.astype(BF16))], lam)
        o_ref[:, h * LANES:(h + 1) * LANES] = _subln(o, gs_ref[...], lam_init)


def _attn_ctx(proj, lam, gq2, gk2, gs, lam_init, layer, prev_k, prev_v):
    qcol, kcol, vcol = 5 * A_WIDTH // LANES, 5 * A_WIDTH // LANES + 4, 5 * A_WIDTH // LANES + 8
    prev_specs, prev_args = [], []
    if layer:
        prev_specs = [pl.BlockSpec((1, layer, 2, B_HEADS, SEQ, B_DK), lambda b: (b, 0, 0, 0, 0, 0)),
                      pl.BlockSpec((1, layer, B_HEADS, SEQ, B_DV), lambda b: (b, 0, 0, 0, 0))]
        prev_args = [prev_k, prev_v]
    n_l = layer + 1
    head_specs = [pl.BlockSpec((SEQ, LANES), functools.partial(lambda b, col: (b, col), col=c0 + h))
                  for c0 in (qcol, kcol, vcol) for h in range(B_HEADS)]
    return pl.pallas_call(
        functools.partial(_attn_ctx_kernel, lam_init=lam_init, layer=layer),
        grid=(BATCH,),
        in_specs=[pl.BlockSpec(memory_space=pltpu.SMEM)] + head_specs + [
            pl.BlockSpec((1, LANES), lambda b: (0, 0)),
            pl.BlockSpec((1, LANES), lambda b: (0, 0)),
            pl.BlockSpec((1, LANES), lambda b: (0, 0)),
        ] + prev_specs,
        out_specs=[
            pl.BlockSpec((SEQ, B_WIDTH), lambda b: (b, 0)),
            pl.BlockSpec((1, n_l, 2, B_HEADS, SEQ, B_DK), lambda b: (b, 0, 0, 0, 0, 0)),
            pl.BlockSpec((1, n_l, B_HEADS, SEQ, B_DV), lambda b: (b, 0, 0, 0, 0)),
        ],
        out_shape=[
            jax.ShapeDtypeStruct((T_CTX, B_WIDTH), F32),
            jax.ShapeDtypeStruct((BATCH, n_l, 2, B_HEADS, SEQ, B_DK), F32),
            jax.ShapeDtypeStruct((BATCH, n_l, B_HEADS, SEQ, B_DV), F32),
        ],
        compiler_params=_cparams(("arbitrary",)),
        name="diff_attention_ctx",
    )(lam, *([proj] * (3 * B_HEADS)), gq2, gk2, gs, *prev_args)


ATT_TQ = 256


def _attn_smp_kernel(lam_ref, q_ref, k_ref, v_ref, ck_ref, cv_ref, cos_ref, sin_ref, gq_ref, gk_ref, gs_ref,
                     o_ref, qs_ref, ks_ref, *, lam_init):
    lam = lam_ref[0, 0]
    cos = cos_ref[...]
    sin = sin_ref[...]
    qn = _rope(_half_rms(q_ref[...], gq_ref[...]), cos, sin) * (B_DK ** -0.5)
    qs_ref[...] = qn.astype(BF16)
    ks_ref[...] = _rope(_half_rms(k_ref[...], gk_ref[...]), cos, sin).astype(BF16)
    ck = jnp.concatenate([ck_ref[0, 0, 0, 0], ck_ref[0, 0, 1, 0]], axis=-1).astype(BF16)
    cv = _with_ones(cv_ref[0, 0, 0].astype(BF16))
    v_bf = _with_ones(v_ref[...].astype(BF16))
    k_bf = ks_ref[...]
    g = gs_ref[...]

    for i in range(DEC_SEQ // ATT_TQ):
        rows = slice(i * ATT_TQ, (i + 1) * ATT_TQ)
        o = _diff_softmax_pv(qs_ref[rows, :], [k_bf, ck], [v_bf, cv], lam)
        o_ref[rows, :] = _subln(o, g, lam_init)


def _attn_smp(proj, lam, cache_k, cache_v, cos, sin, gq2, gk2, gs, layer, lam_init):
    qcol, kcol, vcol = 5 * A_WIDTH // LANES, 5 * A_WIDTH // LANES + 4, 5 * A_WIDTH // LANES + 8
    r0 = T_CTX // DEC_SEQ
    return pl.pallas_call(
        functools.partial(_attn_smp_kernel, lam_init=lam_init),
        grid=(DEC_BATCH, B_HEADS),
        in_specs=[
            pl.BlockSpec(memory_space=pltpu.SMEM),
            pl.BlockSpec((DEC_SEQ, LANES), lambda b, h: (r0 + b, qcol + h)),
            pl.BlockSpec((DEC_SEQ, LANES), lambda b, h: (r0 + b, kcol + h)),
            pl.BlockSpec((DEC_SEQ, LANES), lambda b, h: (r0 + b, vcol + h)),
            pl.BlockSpec((1, 1, 2, 1, PAST, B_DK), lambda b, h: (b, layer, 0, h, 0, 0)),
            pl.BlockSpec((1, 1, 1, PAST, B_DV), lambda b, h: (b, layer, h, 0, 0)),
            pl.BlockSpec((DEC_SEQ, LANES), lambda b, h: (0, 0)),
            pl.BlockSpec((DEC_SEQ, LANES), lambda b, h: (0, 0)),
            pl.BlockSpec((1, LANES), lambda b, h: (0, 0)),
            pl.BlockSpec((1, LANES), lambda b, h: (0, 0)),
            pl.BlockSpec((1, LANES), lambda b, h: (0, 0)),
        ],
        out_specs=pl.BlockSpec((DEC_SEQ, LANES), lambda b, h: (b, h)),
        out_shape=jax.ShapeDtypeStruct((T_SMP, B_WIDTH), F32),
        scratch_shapes=[pltpu.VMEM((DEC_SEQ, LANES), BF16), pltpu.VMEM((DEC_SEQ, LANES), BF16)],
        compiler_params=_cparams(("arbitrary", "arbitrary")),
        name="diff_attention_smp",
    )(lam, proj, proj, proj, cache_k, cache_v, cos, sin, gq2, gk2, gs)


def _rope_tables():
    n_rows = DEC_SEQ // GRID_W
    row = np.repeat(np.arange(n_rows), GRID_W).astype(np.float32)
    col = np.tile(np.arange(GRID_W), n_rows).astype(np.float32)
    half = B_DK // 2
    inv_freq = (ROPE_BASE ** (-jnp.arange(0, half, 2, dtype=F32) / half))
    row_ang = jnp.asarray(row)[:, None] * inv_freq
    col_ang = jnp.asarray(col)[:, None] * inv_freq
    ang = jnp.concatenate([row_ang, row_ang, col_ang, col_ang], axis=-1)
    ang = jnp.concatenate([ang, ang], axis=-1)
    sign = np.where((np.arange(LANES) % 32) < 16, -1.0, 1.0).astype(np.float32)
    return jnp.cos(ang), jnp.sin(ang) * sign


CM_ROWS = 1024


def _gelu(x):
    return 0.5 * x * (1.0 + lax.erf(x * (2.0 ** -0.5)))


def _cmlp_kernel(u_ref, v_ref, g_ref, b_ref, ws_ref, bs_ref, o_ref):
    u = _gelu(u_ref[...])
    gv = _gelu(v_ref[...])
    mu = jnp.mean(gv, axis=-1, keepdims=True)
    dv = gv - mu
    var = jnp.mean(dv * dv, axis=-1, keepdims=True)
    vn = (dv * lax.rsqrt(var + EPS) * g_ref[...] + b_ref[...]).astype(BF16)
    lane = _lane((C_CHUNK, LANES))
    for c in range(CM_ROWS // C_CHUNK):
        rs = slice(c * C_CHUNK, (c + 1) * C_CHUNK)
        for p in range(2):
            cs = slice(p * LANES, (p + 1) * LANES)
            vp = vn[rs, cs]
            m0 = jnp.dot(ws_ref[2 * p].astype(BF16), vp, preferred_element_type=F32)
            m1 = jnp.dot(ws_ref[2 * p + 1].astype(BF16), vp, preferred_element_type=F32)
            mixed = jnp.where(lane < C_DG, m0, m1) + bs_ref[:, cs]
            o_ref[rs, cs] = u[rs, cs] * mixed


def _chunk_mlp(proj, ln_g, ln_b, w_s, bias_full):
    ucol = (5 * A_WIDTH + 3 * B_WIDTH) // C_WIDTH
    return pl.pallas_call(
        _cmlp_kernel,
        grid=(T // CM_ROWS,),
        in_specs=[
            pl.BlockSpec((CM_ROWS, C_WIDTH), lambda i: (i, ucol)),
            pl.BlockSpec((CM_ROWS, C_WIDTH), lambda i: (i, ucol + 1)),
            pl.BlockSpec((1, C_WIDTH), lambda i: (0, 0)),
            pl.BlockSpec((1, C_WIDTH), lambda i: (0, 0)),
            pl.BlockSpec((C_GROUPS, C_CHUNK, C_CHUNK), lambda i: (0, 0, 0)),
            pl.BlockSpec((C_CHUNK, C_WIDTH), lambda i: (0, 0)),
        ],
        out_specs=pl.BlockSpec((CM_ROWS, C_WIDTH), lambda i: (i, 0)),
        out_shape=jax.ShapeDtypeStruct((T, C_WIDTH), F32),
        compiler_params=_cparams(("arbitrary",)),
        name="chunk_mlp",
    )(proj, proj, ln_g.reshape(1, C_WIDTH), ln_b.reshape(1, C_WIDTH), w_s, bias_full)


PM_SUB = TM


def _postmix_kernel(*refs, n_x):
    (of_ref, ob_ref, ag_ref, hg_ref, hsel_ref, bc_ref, bs_ref, c_ref, w_ref, gate1_ref, shift2_ref, scale2_ref,
     g2_ref, wrh_ref, wrl_ref, br_ref, x1_ref, h2_ref, idx_ref, gw_ref) = refs[n_x:]
    is_ctx = pl.program_id(0) < CTX_TILES
    x_refs = refs[:n_x]
    for s in range(TM // PM_SUB):
        rows = slice(s * PM_SUB, (s + 1) * PM_SUB)
        x = x_refs[0][rows, :] if n_x == 1 else jnp.where(is_ctx, x_refs[0][rows, :], x_refs[1][rows, :])
        o = of_ref[rows, :] + ob_ref[rows, :]
        ms = _dot_sel(o * o, hsel_ref[...]) * (1.0 / A_DK)
        ag = ag_ref[rows, :]
        a = o * lax.rsqrt(ms + EPS) * hg_ref[...] * (ag * jax.nn.sigmoid(ag))
        b = jnp.where(is_ctx, bc_ref[rows, :], bs_ref[rows, :])
        mixed = jnp.dot(a.astype(BF16), w_ref[0:A_WIDTH, :], preferred_element_type=F32)
        mixed = mixed + jnp.dot(b.astype(BF16), w_ref[A_WIDTH:A_WIDTH + B_WIDTH, :], preferred_element_type=F32)
        mixed = mixed + jnp.dot(c_ref[rows, :].astype(BF16), w_ref[A_WIDTH + B_WIDTH:, :],
                                preferred_element_type=F32)
        x1 = x + gate1_ref[0] * mixed
        x1_ref[rows, :] = x1
        y = x1 * lax.rsqrt(jnp.mean(x1 * x1, axis=-1, keepdims=True) + EPS) * g2_ref[...]
        h2 = y * (1.0 + scale2_ref[0]) + shift2_ref[0]
        h2_ref[rows] = _pack_rows(h2)
        hi = h2.astype(BF16)
        lo = (h2 - hi.astype(F32)).astype(BF16)
        lg = jnp.dot(lo, wrh_ref[...], preferred_element_type=F32)
        lg = lg + jnp.dot(hi, wrl_ref[...], preferred_element_type=F32)
        lg = lg + jnp.dot(hi, wrh_ref[...], preferred_element_type=F32) + br_ref[...]
        lt = lg.T[:N_EXPERTS]
        row = lax.broadcasted_iota(jnp.int32, lt.shape, 0)
        out_row = lax.broadcasted_iota(jnp.int32, (8, PM_SUB), 0)
        idx_out = jnp.zeros((8, PM_SUB), jnp.int32)
        val_out = jnp.zeros((8, PM_SUB), F32)
        top0 = None
        den = None
        for kk in range(TOP_K):
            mx = jnp.max(lt, axis=0, keepdims=True)
            am = jnp.min(jnp.where(lt == mx, row, N_EXPERTS), axis=0, keepdims=True)
            if kk == 0:
                top0 = mx
            e = jnp.exp(mx - top0)
            den = e if den is None else den + e
            idx_out = jnp.where(out_row == kk, am, idx_out)
            val_out = jnp.where(out_row == kk, e, val_out)
            lt = jnp.where(row == am, -jnp.inf, lt)
        idx_ref[:, rows] = idx_out
        gw_ref[:, rows] = val_out / den


def _post_mix(o_f, o_b, proj, hg, hsel, b_ctx, b_smp, c_out, w_out_bf, x, mod3, g2, wr_hi, wr_lo, br_pad):
    tile = lambda w: pl.BlockSpec((TM, w), lambda i: (i, 0))
    const = lambda shape: pl.BlockSpec(shape, lambda i: tuple(0 for _ in shape))
    modspec = lambda j: pl.BlockSpec((1, 1, D), lambda i: (_mod_row(i), 0, j))
    rowsT = pl.BlockSpec((8, TM), lambda i: (0, i))
    x_specs, x_args = _stream_specs(x)
    return pl.pallas_call(
        functools.partial(_postmix_kernel, n_x=len(x_args)),
        grid=(N_TILES,),
        in_specs=x_specs + [
            tile(A_WIDTH), tile(A_WIDTH),
            pl.BlockSpec((TM, A_WIDTH), lambda i: (i, 4)),
            const((1, A_WIDTH)), const((A_WIDTH, A_WIDTH)),
            pl.BlockSpec((TM, B_WIDTH), lambda i: (jnp.minimum(i, CTX_TILES - 1), 0)),
            pl.BlockSpec((TM, B_WIDTH), lambda i: (jnp.maximum(i - CTX_TILES, 0), 0)),
            tile(C_WIDTH),
            const((D, D)),
            modspec(2), modspec(3), modspec(4),
            const((1, D)), const((D, LANES)), const((D, LANES)), const((1, LANES)),
        ],
        out_specs=[tile(D), pl.BlockSpec((TM, ROW_TILES, LANES), lambda i: (i, 0, 0)), rowsT, rowsT],
        out_shape=[
            jax.ShapeDtypeStruct((T, D), F32),
            jax.ShapeDtypeStruct((T, ROW_TILES, LANES), ROW_DT),
            jax.ShapeDtypeStruct((8, T), jnp.int32),
            jax.ShapeDtypeStruct((8, T), F32),
        ],
        compiler_params=_cparams(("arbitrary",)),
        name="post_mix_router",
    )(*x_args, o_f, o_b, proj, hg, hsel, b_ctx, b_smp, c_out, w_out_bf, mod3, mod3, mod3, g2.reshape(1, D),
      wr_hi, wr_lo, br_pad)


def _route_kernel(idx_ref, dest_ref, meta_ref):
    erow = lax.broadcasted_iota(jnp.int32, (N_EXPERTS, TM), 0)
    s_i = lax.broadcasted_iota(jnp.int32, (TM, TM), 0)
    t_i = lax.broadcasted_iota(jnp.int32, (TM, TM), 1)
    earlier = (s_i < t_i).astype(BF16)
    out_row = lax.broadcasted_iota(jnp.int32, (8, TM), 0)

    def onehots(i):
        idx = idx_ref[:, pl.ds(pl.multiple_of(i * TM, TM), TM)]
        return [(erow == idx[kk:kk + 1, :]) for kk in range(TOP_K)]

    def count_tile(i, run):
        ohs = onehots(i)
        base = run
        pos = jnp.zeros((8, TM), F32)
        for kk in range(TOP_K):
            ohf = ohs[kk].astype(F32)
            before = jnp.dot(ohs[kk].astype(BF16), earlier, preferred_element_type=F32)
            p = jnp.sum(ohf * (base + before), axis=0, keepdims=True)
            pos = jnp.where(out_row == kk, p, pos)
            base = base + jnp.sum(ohf, axis=1, keepdims=True)
        dest_ref[:, pl.ds(pl.multiple_of(i * TM, TM), TM)] = pos.astype(jnp.int32)
        return base

    counts = lax.fori_loop(0, N_TILES, count_tile, jnp.zeros((N_EXPERTS, 1), F32)).astype(jnp.int32)
    bm_shift = MOE_BM.bit_length() - 1
    padded = lax.shift_left(lax.shift_right_logical(counts + (MOE_BM - 1), bm_shift), bm_shift)
    e_r = lax.broadcasted_iota(jnp.int32, (N_EXPERTS, N_EXPERTS), 0)
    e_c = lax.broadcasted_iota(jnp.int32, (N_EXPERTS, N_EXPERTS), 1)
    incl = (e_c <= e_r).astype(BF16)
    pad_end = _sel_dot(incl, jnp.broadcast_to(padded.astype(F32), (N_EXPERTS, LANES)))[:, :1]
    pad_start = pad_end - padded.astype(F32)

    def place_tile(i, carry):
        ohs = onehots(i)
        sl = pl.ds(pl.multiple_of(i * TM, TM), TM)
        off = jnp.zeros((8, TM), F32)
        for kk in range(TOP_K):
            o = jnp.sum(ohs[kk].astype(F32) * pad_start, axis=0, keepdims=True)
            off = jnp.where(out_row == kk, o, off)
        dest_ref[:, sl] = dest_ref[:, sl] + off.astype(jnp.int32)
        return carry

    lax.fori_loop(0, N_TILES, place_tile, 0)

    total = jnp.max(pad_end, axis=0, keepdims=True)
    lane_i = lax.broadcasted_iota(jnp.int32, (1, TM), 1)
    blk0 = (lane_i * MOE_BM).astype(F32)
    block_e = jnp.sum((pad_end <= blk0).astype(F32), axis=0, keepdims=True)
    live_end = pad_start + counts.astype(F32)
    sel = erow.astype(F32) == block_e
    live = jnp.sum(jnp.where(sel, live_end, 0.0), axis=0, keepdims=True)
    valid = jnp.where(blk0 < total, jnp.clip(live - blk0, 0.0, float(MOE_BM)), 0.0)
    own = erow == lane_i
    n_blk = jnp.sum(jnp.where(own, padded.astype(F32), 0.0), axis=0, keepdims=True) * (1.0 / MOE_BM)
    first_blk = jnp.sum(jnp.where(own, pad_start, 0.0), axis=0, keepdims=True) * (1.0 / MOE_BM)
    meta = jnp.where(out_row == 0, valid, 0.0)
    meta = jnp.where(out_row == 1, n_blk, meta)
    meta = jnp.where(out_row == 2, first_blk, meta)
    meta = jnp.where(out_row == 3, total * (1.0 / MOE_BM), meta)
    meta_ref[...] = meta.astype(jnp.int32)


def _route(idx_t):
    assert MOE_BLOCKS <= TM
    return pl.pallas_call(
        _route_kernel,
        out_shape=[jax.ShapeDtypeStruct((8, T), jnp.int32), jax.ShapeDtypeStruct((8, TM), jnp.int32)],
        compiler_params=pltpu.CompilerParams(vmem_limit_bytes=VMEM_LIMIT),
        name="moe_route",
    )(idx_t)


def _moe_kernel(bv_ref, nb_ref, g0_ref, tot_ref, x_hbm, wgu_ref, bgu_ref, wdn_ref, bdn_ref, y_hbm,
                wgu_bf, wdn_bf, xbuf, ybuf, xb_ref, xsem, ysem):
    e = pl.program_id(0)
    n_blk = nb_ref[e]
    first = g0_ref[e]
    total = tot_ref[0]
    ahead = MOE_RING - 1

    def x_copy(g):
        slot = g % MOE_RING
        return pltpu.make_async_copy(x_hbm.at[pl.ds(g * MOE_BM, MOE_BM)], xbuf.at[slot], xsem.at[slot])

    def y_copy(g):
        slot = g % MOE_RING
        return pltpu.make_async_copy(ybuf.at[slot], y_hbm.at[pl.ds(g * MOE_BM, MOE_BM)], ysem.at[slot])

    @pl.when(e == 0)
    def _():
        for g in range(ahead):
            @pl.when(g < total)
            def _():
                x_copy(g).start()

    @pl.when(n_blk > 0)
    def _():
        wgu_bf[...] = wgu_ref[0, 0].astype(BF16)
        wdn_bf[...] = wdn_ref[0, 0].astype(BF16)

    def block(j, carry):
        g = first + j
        slot = g % MOE_RING
        x_copy(g).wait()

        @pl.when(g + ahead < total)
        def _():
            x_copy(g + ahead).start()

        @pl.when(g >= MOE_RING)
        def _():
            y_copy(g - MOE_RING).wait()

        n_live = bv_ref[g]

        def ffn(n_rows):
            live = lax.broadcasted_iota(jnp.int32, (n_rows, LANES), 0) < n_live
            for c, chunk in enumerate(_unpack_rows(xbuf[slot, :n_rows])):
                xb_ref[:n_rows, c * LANES:(c + 1) * LANES] = jnp.where(live, chunk, 0.0).astype(BF16)
            gu = jnp.dot(xb_ref[:n_rows, :], wgu_bf[...], preferred_element_type=F32) + bgu_ref[0, 0]
            glu = jnp.minimum(gu[:, :D], SWIGLU_LIMIT)
            lin = jnp.clip(gu[:, D:], -SWIGLU_LIMIT, SWIGLU_LIMIT)
            act = glu * jax.nn.sigmoid(SWIGLU_ALPHA * glu) * (lin + 1.0)
            y = jnp.dot(act.astype(BF16), wdn_bf[...], preferred_element_type=F32) + bdn_ref[0, 0]
            ybuf[slot, :n_rows] = _pack_rows(y)

        @pl.when(n_live > MOE_BM // 2)
        def _():
            ffn(MOE_BM)

        @pl.when(n_live <= MOE_BM // 2)
        def _():
            ffn(MOE_BM // 2)
            ybuf[slot, MOE_BM // 2:] = jnp.zeros((MOE_BM // 2, ROW_TILES, LANES), ROW_DT)

        y_copy(g).start()
        return carry

    lax.fori_loop(0, n_blk, block, 0)

    @pl.when(e == N_EXPERTS - 1)
    def _():
        for back in range(MOE_RING, 0, -1):
            @pl.when(total >= back)
            def _():
                y_copy(total - back).wait()

        def fill(g, carry):
            ybuf[g % MOE_RING] = jnp.zeros((MOE_BM, ROW_TILES, LANES), ROW_DT)
            y_copy(g).start()
            y_copy(g).wait()
            return carry

        lax.fori_loop(total, MOE_BLOCKS, fill, 0)


def _moe_ffn(block_valid, n_blk, first_blk, total_blk, xs, w_gu, b_gu, w_dn, b_dn, layer):
    rows = (MOE_BM, ROW_TILES, LANES)
    return pl.pallas_call(
        _moe_kernel,
        grid_spec=pltpu.PrefetchScalarGridSpec(
            num_scalar_prefetch=4,
            grid=(N_EXPERTS,),
            in_specs=[
                pl.BlockSpec(memory_space=pl.ANY),
                pl.BlockSpec((1, 1, D, 2 * D), lambda e, *_: (layer, e, 0, 0)),
                pl.BlockSpec((1, 1, 1, 2 * D), lambda e, *_: (layer, e, 0, 0)),
                pl.BlockSpec((1, 1, D, D), lambda e, *_: (layer, e, 0, 0)),
                pl.BlockSpec((1, 1, 1, D), lambda e, *_: (layer, e, 0, 0)),
            ],
            out_specs=pl.BlockSpec(memory_space=pl.ANY),
            scratch_shapes=[
                pltpu.VMEM((D, 2 * D), BF16), pltpu.VMEM((D, D), BF16),
                pltpu.VMEM((MOE_RING,) + rows, ROW_DT), pltpu.VMEM((MOE_RING,) + rows, ROW_DT),
                pltpu.VMEM((MOE_BM, D), BF16),
                pltpu.SemaphoreType.DMA((MOE_RING,)), pltpu.SemaphoreType.DMA((MOE_RING,)),
            ],
        ),
        out_shape=jax.ShapeDtypeStruct((MOE_ROWS, ROW_TILES, LANES), ROW_DT),
        compiler_params=_cparams(("arbitrary",)),
        name="moe_expert_ffn",
    )(block_valid, n_blk, first_blk, total_blk, xs, w_gu, b_gu.reshape(DEPTH, N_EXPERTS, 1, 2 * D), w_dn,
      b_dn.reshape(DEPTH, N_EXPERTS, 1, D))


def _combine_kernel(x1_ref, y_ref, gw_ref, gate2_ref, o_ref):
    gw = jnp.concatenate([gw_ref[...], jnp.zeros((LANES - 8, TM), F32)], axis=0).T
    ys = [_unpack_rows(y_ref[kk]) for kk in range(TOP_K)]
    for c in range(D // LANES):
        cs = slice(c * LANES, (c + 1) * LANES)
        acc = None
        for kk in range(TOP_K):
            term = ys[kk][c] * gw[:, kk:kk + 1]
            acc = term if acc is None else acc + term
        o_ref[:, cs] = x1_ref[:, cs] + gate2_ref[0, :, cs] * acc


def _combine(x1, yg, gw, mod3, tile0, n_tiles):
    return pl.pallas_call(
        _combine_kernel,
        grid=(n_tiles,),
        in_specs=[
            pl.BlockSpec((TM, D), lambda i: (tile0 + i, 0)),
            pl.BlockSpec((TOP_K, TM, ROW_TILES, LANES), lambda i: (0, tile0 + i, 0, 0)),
            pl.BlockSpec((8, TM), lambda i: (0, tile0 + i)),
            pl.BlockSpec((1, 1, D), lambda i: (_mod_row(tile0 + i), 0, 5)),
        ],
        out_specs=pl.BlockSpec((TM, D), lambda i: (i, 0)),
        out_shape=jax.ShapeDtypeStruct((n_tiles * TM, D), F32),
        compiler_params=_cparams(("arbitrary",)),
        name="moe_combine",
    )(x1, yg, gw, mod3)


def _sc_mesh():
    return plsc.VectorSubcoreMesh(core_axis_name="c", subcore_axis_name="s")


def _sc_worker():
    return lax.axis_index("s") * SC_CORES + lax.axis_index("c")


def _sc_dispatch(h2t, dest_km):
    per_w = T // SC_WORKERS

    @functools.partial(
        pl.kernel, mesh=_sc_mesh(),
        out_type=jax.ShapeDtypeStruct((MOE_ROWS, ROW_TILES, LANES), ROW_DT),
        scratch_types=[pltpu.VMEM((SC_WIN,), jnp.int32), pltpu.VMEM((SC_WIN, ROW_TILES, LANES), ROW_DT),
                       pltpu.SemaphoreType.DMA],
    )
    def run(h_hbm, d_hbm, o_hbm, idx_v, rows_v, sem):
        w0 = _sc_worker() * per_w

        @pl.loop(0, per_w // SC_WIN)
        def _(w):
            base = pl.multiple_of(w0 + w * SC_WIN, SC_WIN)
            pltpu.sync_copy(h_hbm.at[pl.ds(base, SC_WIN)], rows_v)
            for kk in range(TOP_K):
                pltpu.sync_copy(d_hbm.at[pl.ds(kk * T + base, SC_WIN)], idx_v)
                pltpu.async_copy(rows_v, o_hbm.at[idx_v], sem).wait()

    return run(h2t, dest_km)


def _sc_gather(yb, dest_km):
    n = TOP_K * T
    per_w = n // SC_WORKERS

    @functools.partial(
        pl.kernel, mesh=_sc_mesh(),
        out_type=jax.ShapeDtypeStruct((n, ROW_TILES, LANES), ROW_DT),
        scratch_types=[pltpu.VMEM((SC_WIN,), jnp.int32), pltpu.VMEM((SC_WIN, ROW_TILES, LANES), ROW_DT),
                       pltpu.SemaphoreType.DMA],
    )
    def run(y_hbm, d_hbm, o_hbm, idx_v, rows_v, sem):
        w0 = _sc_worker() * per_w

        @pl.loop(0, per_w // SC_WIN)
        def _(w):
            base = pl.multiple_of(w0 + w * SC_WIN, SC_WIN)
            pltpu.sync_copy(d_hbm.at[pl.ds(base, SC_WIN)], idx_v)
            pltpu.async_copy(y_hbm.at[idx_v], rows_v, sem).wait()
            pltpu.sync_copy(rows_v, o_hbm.at[pl.ds(base, SC_WIN)])

    return run(yb, dest_km)


def kernel(x_prompt, x_sample, c, cache_diff_k, cache_diff_v, state_hgrn, c_ctx, norm_mix_g, norm_ffn_g, w_mod, b_mod, w_in, w_out, hgrn_lower_bounds, hgrn_norm_g, diff_q_norm_g, diff_k_norm_g, diff_lambda_q1, diff_lambda_k1, diff_lambda_q2, diff_lambda_k2, diff_subln_g, cmlp_ln_g, cmlp_ln_b, cmlp_w_s, cmlp_b_s, router_w, router_b, moe_w_gate_up, moe_b_gate_up, moe_w_down, moe_b_down):
    x = (x_prompt.reshape(T_CTX, D), x_sample.reshape(T_SMP, D))
    cvec = jnp.concatenate([c_ctx[None, :], c, jnp.zeros((MOD_ROWS - 1 - DEC_BATCH, D), F32)], axis=0)
    mod = _modulation(cvec, w_mod, b_mod)

    lvl_np, tri_np = _hgrn_tables()
    lvl = jnp.asarray(lvl_np)
    tri = jnp.asarray(tri_np, dtype=BF16)
    cos, sin = _rope_tables()
    hsel = jnp.asarray(np.kron(np.eye(A_HEADS), np.ones((A_DK, A_DK))), dtype=BF16)
    sm = jax.nn.softmax(hgrn_lower_bounds.astype(F32), axis=0)
    lb_all = jnp.cumsum(sm, axis=0) - sm[0]

    new_k, new_v, new_s = None, None, []
    for l in range(DEPTH):
        mod3 = mod[l].reshape(MOD_ROWS, 1, 6 * D)
        proj = _in_projection(x, norm_mix_g[l], mod3, w_in[l].astype(BF16))

        s0 = jnp.concatenate([jnp.zeros((BATCH, 2, A_HEADS, A_DK, A_DK), F32), state_hgrn[:, l]], axis=0)
        o_dir, fin_dir = [], []
        for d in range(2):
            o_d, fin_d = _hgrn_scan(proj, lb_all[l, d].reshape(1, A_WIDTH), _pack_state(s0[:, d]),
                                    lvl, tri, l, d == 1)
            o_dir.append(o_d)
            fin_dir.append(_unpack_state(fin_d[:BATCH]))
        new_s.append(jnp.stack(fin_dir, axis=1))

        lam_init = 0.8 - 0.6 * math.exp(-0.3 * l)
        lam = (jnp.exp(jnp.sum(diff_lambda_q1[l] * diff_lambda_k1[l]))
               - jnp.exp(jnp.sum(diff_lambda_q2[l] * diff_lambda_k2[l])) + lam_init).reshape(1, 1)
        gq2 = jnp.tile(diff_q_norm_g[l], 2).reshape(1, LANES)
        gk2 = jnp.tile(diff_k_norm_g[l], 2).reshape(1, LANES)
        gs = diff_subln_g[l].reshape(1, LANES)
        b_ctx, new_k, new_v = _attn_ctx(proj, lam, gq2, gk2, gs, lam_init, l, new_k, new_v)
        b_smp = _attn_smp(proj, lam, cache_diff_k, cache_diff_v, cos, sin, gq2, gk2, gs, l, lam_init)

        bias_full = jnp.repeat(cmlp_b_s[l].T, C_DG, axis=1)
        c_out = _chunk_mlp(proj, cmlp_ln_g[l], cmlp_ln_b[l], cmlp_w_s[l], bias_full)

        hg = jnp.tile(hgrn_norm_g[l], A_HEADS).reshape(1, A_WIDTH)
        wr_pad = jnp.pad(router_w[l], ((0, 0), (0, LANES - N_EXPERTS)))
        wr_hi = wr_pad.astype(BF16)
        wr_lo = (wr_pad - wr_hi.astype(F32)).astype(BF16)
        br_pad = jnp.pad(router_b[l], (0, LANES - N_EXPERTS)).reshape(1, LANES)
        x1, h2, idx_t, gw_t = _post_mix(o_dir[0], o_dir[1], proj, hg, hsel, b_ctx, b_smp, c_out,
                                        w_out[l].astype(BF16), x, mod3, norm_ffn_g[l], wr_hi, wr_lo, br_pad)

        dest_t, meta = _route(idx_t)
        dest_km = dest_t[:TOP_K].reshape(-1)
        xs = _sc_dispatch(h2, dest_km)
        yb = _moe_ffn(meta[0, :MOE_BLOCKS], meta[1, :N_EXPERTS], meta[2, :N_EXPERTS], meta[3, :1], xs,
                      moe_w_gate_up, moe_b_gate_up, moe_w_down, moe_b_down, l)
        yg = _sc_gather(yb, dest_km).reshape(TOP_K, T, ROW_TILES, LANES)
        x = (_combine(x1, yg, gw_t, mod3, 0, CTX_TILES),
             _combine(x1, yg, gw_t, mod3, CTX_TILES, N_TILES - CTX_TILES))

    y_prompt = x[0].reshape(BATCH, SEQ, D)
    y_sample = x[1].reshape(DEC_BATCH, DEC_SEQ, D)
    return (y_prompt, y_sample, new_k, new_v, jnp.stack(new_s, axis=1))
```

```python
import functools
import math

import numpy as np
import jax
import jax.numpy as jnp
from jax import lax
from jax.experimental import pallas as pl
from jax.experimental.pallas import tpu as pltpu
from jax.experimental.pallas import tpu_sc as plsc

F32 = jnp.float32
BF16 = jnp.bfloat16

D = 1024
DEPTH = 2
BATCH, SEQ = 16, 256
DEC_BATCH, DEC_SEQ = 8, 1024
PAST = 512
GRID_W = 64
A_HEADS, A_DK = 4, 64
A_WIDTH = 256
B_HEADS, B_DK, B_DV = 4, 64, 128
B_WIDTH = 512
C_GROUPS, C_CHUNK, C_WIDTH, C_DG = 4, 128, 256, 64
IN_WIDTH = 5 * A_WIDTH + 3 * B_WIDTH + 2 * C_WIDTH
N_EXPERTS, TOP_K = 32, 4
SWIGLU_LIMIT, SWIGLU_ALPHA = 7.0, 1.702
ROPE_BASE = 10000.0
EPS = 1e-6

T_CTX = BATCH * SEQ
T_SMP = DEC_BATCH * DEC_SEQ
T = T_CTX + T_SMP
N_SEQ = BATCH + DEC_BATCH
MOD_ROWS = 16

TM = 256
N_TILES = T // TM
CTX_TILES = T_CTX // TM
SMP_TILES_PER_SEQ = DEC_SEQ // TM
LANES = 128
MOE_BM = 256
MOE_ROWS = T * TOP_K + N_EXPERTS * MOE_BM
MOE_BLOCKS = MOE_ROWS // MOE_BM
MOE_RING = 4
ROW_WORDS = D // 2
ROW_TILES = ROW_WORDS // LANES
ROW_DT = jnp.int32
SC_CORES, SC_SUBCORES = 2, 16
SC_WORKERS = SC_CORES * SC_SUBCORES
SC_WIN = 128
VMEM_LIMIT = 56 * 1024 * 1024


def _cparams(sem):
    return pltpu.CompilerParams(dimension_semantics=sem, vmem_limit_bytes=VMEM_LIMIT)


def _mod_row(i):
    return jnp.where(i < CTX_TILES, 0, 1 + (i - CTX_TILES) // SMP_TILES_PER_SEQ)


def _split3(x):
    hi = x.astype(BF16)
    r = x - hi.astype(F32)
    mid = r.astype(BF16)
    lo = (r - mid.astype(F32)).astype(BF16)
    return hi, mid, lo


def _sel_dot(sel, x):
    hi, mid, lo = _split3(x)
    acc = jnp.dot(sel, lo, preferred_element_type=F32)
    acc = acc + jnp.dot(sel, mid, preferred_element_type=F32)
    return acc + jnp.dot(sel, hi, preferred_element_type=F32)


def _dot_sel(x, sel):
    hi, mid, lo = _split3(x)
    acc = jnp.dot(lo, sel, preferred_element_type=F32)
    acc = acc + jnp.dot(mid, sel, preferred_element_type=F32)
    return acc + jnp.dot(hi, sel, preferred_element_type=F32)


def _dot_nt(a, b):
    return lax.dot_general(a, b, (((1,), (1,)), ((), ())), preferred_element_type=F32)


def _dot_tn(a, b):
    return lax.dot_general(a, b, (((0,), (0,)), ((), ())), preferred_element_type=F32)


def _lane(shape):
    return lax.broadcasted_iota(jnp.int32, shape, len(shape) - 1)


def _pack_rows(x):
    hi = lax.bitcast_convert_type(x[:, :ROW_WORDS].astype(BF16).astype(F32), jnp.int32)
    lo = lax.bitcast_convert_type(x[:, ROW_WORDS:].astype(BF16).astype(F32), jnp.int32)
    words = hi | lax.shift_right_logical(lo, 16)
    return pltpu.einshape("t(jl)->tjl", words, l=LANES)


def _unpack_rows(words3):
    wt = pltpu.einshape("tjl->jtl", words3)
    hi = [lax.bitcast_convert_type(wt[j] & jnp.int32(-65536), F32) for j in range(ROW_TILES)]
    lo = [lax.bitcast_convert_type(lax.shift_left(wt[j], 16), F32) for j in range(ROW_TILES)]
    return hi + lo


def _mod_kernel(c_ref, w_ref, b_ref, o_ref):
    c = c_ref[...]
    s = c * jax.nn.sigmoid(c)
    o_ref[0] = jnp.dot(s.astype(BF16), w_ref[0].astype(BF16), preferred_element_type=F32) + b_ref[0]


def _modulation(cvec, w_mod, b_mod):
    tn = 1536
    return pl.pallas_call(
        _mod_kernel,
        grid=(DEPTH, 6 * D // tn),
        in_specs=[
            pl.BlockSpec((MOD_ROWS, D), lambda l, j: (0, 0)),
            pl.BlockSpec((1, D, tn), lambda l, j: (l, 0, j)),
            pl.BlockSpec((1, 1, tn), lambda l, j: (l, 0, j)),
        ],
        out_specs=pl.BlockSpec((1, MOD_ROWS, tn), lambda l, j: (l, 0, j)),
        out_shape=jax.ShapeDtypeStruct((DEPTH, MOD_ROWS, 6 * D), F32),
        compiler_params=_cparams(("arbitrary", "arbitrary")),
        name="modulation",
    )(cvec, w_mod, b_mod.reshape(DEPTH, 1, 6 * D))


def _stream_specs(x):
    if isinstance(x, tuple):
        return [pl.BlockSpec((TM, D), lambda i: (jnp.minimum(i, CTX_TILES - 1), 0)),
                pl.BlockSpec((TM, D), lambda i: (jnp.maximum(i - CTX_TILES, 0), 0))], list(x)
    return [pl.BlockSpec((TM, D), lambda i: (i, 0))], [x]


def _stream_tile(x_refs):
    if len(x_refs) == 1:
        return x_refs[0][...]
    return jnp.where(pl.program_id(0) < CTX_TILES, x_refs[0][...], x_refs[1][...])


def _inproj_kernel(*refs, n_x):
    g_ref, shift_ref, scale_ref, w_ref, o_ref = refs[n_x:]
    x = _stream_tile(refs[:n_x])
    y = x * lax.rsqrt(jnp.mean(x * x, axis=-1, keepdims=True) + EPS) * g_ref[...]
    h = y * (1.0 + scale_ref[0]) + shift_ref[0]
    o_ref[...] = jnp.dot(h.astype(BF16), w_ref[...], preferred_element_type=F32)


def _in_projection(x, g, mod3, w_in_bf):
    x_specs, x_args = _stream_specs(x)
    return pl.pallas_call(
        functools.partial(_inproj_kernel, n_x=len(x_args)),
        grid=(N_TILES,),
        in_specs=x_specs + [
            pl.BlockSpec((1, D), lambda i: (0, 0)),
            pl.BlockSpec((1, 1, D), lambda i: (_mod_row(i), 0, 0)),
            pl.BlockSpec((1, 1, D), lambda i: (_mod_row(i), 0, 1)),
            pl.BlockSpec((D, IN_WIDTH), lambda i: (0, 0)),
        ],
        out_specs=pl.BlockSpec((TM, IN_WIDTH), lambda i: (i, 0)),
        out_shape=jax.ShapeDtypeStruct((T, IN_WIDTH), F32),
        compiler_params=_cparams(("arbitrary",)),
        name="in_projection",
    )(*x_args, g.reshape(1, D), mod3, mod3, w_in_bf)


HG_C = 128
HG_LEVELS = tuple(2 ** j for j in range(1, int(math.log2(HG_C)) + 1))


def _hgrn_tables():
    t = np.arange(HG_C)[:, None]
    s = np.arange(HG_C)[None, :]
    x = t ^ s
    lvl = np.zeros((HG_C, HG_C), np.int32)
    nz = x > 0
    lvl[nz] = np.floor(np.log2(x[nz])).astype(np.int32) + 1
    fwd = np.where(t >= s, lvl, -1).astype(np.int32)
    bwd = np.where(t <= s, lvl, -1).astype(np.int32)
    tri_f = (t >= s).astype(np.float32)
    tri_b = (t <= s).astype(np.float32)
    return np.stack([fwd, bwd]), np.stack([tri_f, tri_b])


def _block_ref(cum, m, idx):
    c, l = cum.shape
    if m >= 16:
        c3 = cum.reshape(c // m, m, l)
        r = c3[:, idx:idx + 1, :]
        return jnp.broadcast_to(r, (c // m, m, l)).reshape(c, l)
    c3 = cum.reshape(c // 8, 8, l)
    sub = lax.broadcasted_iota(jnp.int32, c3.shape, 1)
    out = None
    for j in range(8 // m - 1, -1, -1):
        cand = jnp.broadcast_to(c3[:, j * m + idx:j * m + idx + 1, :], c3.shape)
        out = cand if out is None else jnp.where(sub < (j + 1) * m, cand, out)
    return out.reshape(c, l)


def _hgrn_kernel(q_ref, z_ref, v_ref, lb_ref, s0_ref, lvl_ref, tri_ref, o_ref, fin_ref, st_ref, *, layer, rev):
    g = pl.program_id(0)
    first = jnp.logical_or(g < CTX_TILES, (g - CTX_TILES) % SMP_TILES_PER_SEQ == 0)

    @pl.when(first)
    def _():
        st_ref[...] = s0_ref[0]

    qr = q_ref[...]
    q = qr * jax.nn.sigmoid(qr) * (A_DK ** -0.5)
    z = z_ref[...]
    if layer == 0:
        lf = jnp.minimum(z, 0.0) - jnp.log(1.0 + jnp.exp(-jnp.abs(z)))
        k = jax.nn.sigmoid(-z)
    else:
        lbd = lb_ref[...]
        lf = jnp.log(lbd + (1.0 - lbd) * jax.nn.sigmoid(z))
        k = (1.0 - lbd) * jax.nn.sigmoid(-z)
    v = v_ref[...]
    tri = tri_ref[0]
    lvl = lvl_ref[0]
    last_row = 0 if rev else HG_C - 1
    n_chunks = TM // HG_C
    order = range(n_chunks - 1, -1, -1) if rev else range(n_chunks)
    lf2 = lf * math.log2(math.e)
    cums = [_sel_dot(tri, lf2[c * HG_C:(c + 1) * HG_C]) for c in range(n_chunks)]
    lane = _lane((HG_C, LANES))
    head_masks = (lane < A_DK, lane >= A_DK)
    lane_row = _lane((1, LANES))
    head_keep = ((lane_row < A_DK).astype(BF16), (lane_row >= A_DK).astype(BF16))
    lvl2 = jnp.concatenate([lvl, lvl], axis=0)
    level_masks = [lvl2 == i for i in range(len(HG_LEVELS) + 1)]
    r = lax.broadcasted_iota(jnp.int32, (LANES, LANES), 0)
    cl = lax.broadcasted_iota(jnp.int32, (LANES, LANES), 1)
    same_head = (r < A_DK) == (cl < A_DK)

    def chunk(q_p, k_p, v_p, cum_p, st):
        v_bf = v_p.astype(BF16)
        k_bf = k_p.astype(BF16)
        q_bf = q_p.astype(BF16)

        def both_heads(x_bf):
            return jnp.concatenate([x_bf * head_keep[0], x_bf * head_keep[1]], axis=0)

        scores = jnp.where(level_masks[0], _dot_nt(both_heads(q_bf), k_bf), 0.0)
        for li, m in enumerate(HG_LEVELS):
            ref = _block_ref(cum_p, m, m // 2 if rev else m // 2 - 1)
            dec = jnp.exp2(-jnp.abs(cum_p - ref))
            qd = (q_p * dec).astype(BF16)
            kd = (k_p * dec).astype(BF16)
            scores = jnp.where(level_masks[li + 1], _dot_nt(both_heads(qd), kd), scores)
        pv = jnp.dot(scores.astype(BF16), v_bf, preferred_element_type=F32)
        o_intra = jnp.where(head_masks[0], pv[:HG_C], pv[HG_C:])
        q0 = (q_p * jnp.exp2(cum_p)).astype(BF16)
        out = o_intra + _dot_nt(q0, st.astype(BF16))
        last = cum_p[last_row:last_row + 1, :]
        ks = (k_p * jnp.exp2(last - cum_p)).astype(BF16)
        upd = _dot_tn(v_bf, ks)
        return out, st * jnp.exp2(last) + jnp.where(same_head, upd, 0.0)

    for p in range(2):
        sl = slice(p * LANES, (p + 1) * LANES)
        st = st_ref[p]
        for c in order:
            rows = slice(c * HG_C, (c + 1) * HG_C)
            o_ref[rows, sl], st = chunk(q[rows, sl], k[rows, sl], v[rows, sl], cums[c][:, sl], st)
        st_ref[p] = st
        fin_ref[0, p] = st


def _hgrn_seq(g):
    return jnp.where(g < CTX_TILES, g, CTX_TILES + (g - CTX_TILES) // SMP_TILES_PER_SEQ)


def _hgrn_blk(g, rev):
    if not rev:
        return g
    j = g - CTX_TILES
    return jnp.where(g < CTX_TILES, g,
                     CTX_TILES + (j // SMP_TILES_PER_SEQ) * SMP_TILES_PER_SEQ
                     + (SMP_TILES_PER_SEQ - 1 - j % SMP_TILES_PER_SEQ))


def _hgrn_scan(proj, lb_dir, s0_dir, lvl, tri, layer, rev):
    d = 1 if rev else 0
    blk = functools.partial(_hgrn_blk, rev=rev)
    return pl.pallas_call(
        functools.partial(_hgrn_kernel, layer=layer, rev=rev),
        grid=(N_TILES,),
        in_specs=[
            pl.BlockSpec((TM, A_WIDTH), lambda g: (blk(g), 0)),
            pl.BlockSpec((TM, A_WIDTH), lambda g: (blk(g), 1 + d)),
            pl.BlockSpec((TM, A_WIDTH), lambda g: (blk(g), 3)),
            pl.BlockSpec((1, A_WIDTH), lambda g: (0, 0)),
            pl.BlockSpec((1, 2, LANES, LANES), lambda g: (_hgrn_seq(g), 0, 0, 0)),
            pl.BlockSpec((1, HG_C, HG_C), lambda g: (d, 0, 0)),
            pl.BlockSpec((1, HG_C, HG_C), lambda g: (d, 0, 0)),
        ],
        out_specs=[
            pl.BlockSpec((TM, A_WIDTH), lambda g: (blk(g), 0)),
            pl.BlockSpec((1, 2, LANES, LANES), lambda g: (_hgrn_seq(g), 0, 0, 0)),
        ],
        out_shape=[
            jax.ShapeDtypeStruct((T, A_WIDTH), F32),
            jax.ShapeDtypeStruct((N_SEQ, 2, LANES, LANES), F32),
        ],
        scratch_shapes=[pltpu.VMEM((2, LANES, LANES), F32)],
        compiler_params=_cparams(("arbitrary",)),
        name=f"hgrn_scan_{'bwd' if rev else 'fwd'}",
    )(proj, proj, proj, lb_dir, s0_dir, lvl, tri)


def _pack_state(s):
    n = s.shape[0]
    st = jnp.swapaxes(s, -1, -2).reshape(n, 2, 2, A_DK, A_DK)
    z = jnp.zeros_like(st[:, :, 0])
    top = jnp.concatenate([st[:, :, 0], z], axis=-1)
    bot = jnp.concatenate([z, st[:, :, 1]], axis=-1)
    return jnp.concatenate([top, bot], axis=-2)


def _unpack_state(sp):
    n = sp.shape[0]
    h0 = sp[:, :, :A_DK, :A_DK]
    h1 = sp[:, :, A_DK:, A_DK:]
    st = jnp.stack([h0, h1], axis=2).reshape(n, A_HEADS, A_DK, A_DK)
    return jnp.swapaxes(st, -1, -2)


def _half_rms(x, g):
    r = lax.broadcasted_iota(jnp.int32, (LANES, LANES), 0)
    c = lax.broadcasted_iota(jnp.int32, (LANES, LANES), 1)
    half_mean = jnp.where((r < B_DK) == (c < B_DK), 1.0 / B_DK, 0.0).astype(BF16)
    xx = x * x
    hi = xx.astype(BF16)
    lo = (xx - hi.astype(F32)).astype(BF16)
    ms = jnp.dot(lo, half_mean, preferred_element_type=F32) + jnp.dot(hi, half_mean, preferred_element_type=F32)
    return x * lax.rsqrt(ms + EPS) * g


def _rope(x, cos, sin_signed):
    lane = _lane(x.shape)
    first = (lane % 32) < 16
    rot = jnp.where(first, pltpu.roll(x, LANES - 16, 1), pltpu.roll(x, 16, 1))
    return x * cos + rot * sin_signed


def _with_ones(v_bf):
    return jnp.concatenate([v_bf, jnp.ones_like(v_bf)], axis=-1)


def _diff_softmax_pv(q_bf, keys_bf, vals_ext, lam):
    lane = _lane(q_bf.shape)
    zero = jnp.zeros_like(q_bf)
    outs = []
    for mp in range(2):
        qm = jnp.where((lane < B_DK) == (mp == 0), q_bf, zero)
        s = [_dot_nt(qm, kk) for kk in keys_bf]
        mx = functools.reduce(jnp.maximum, [jnp.max(si, axis=-1, keepdims=True) for si in s])
        acc = None
        for si, ve in zip(s, vals_ext):
            e = jnp.exp((si - mx).astype(BF16))
            pv = jnp.dot(e, ve, preferred_element_type=F32)
            acc = pv if acc is None else acc + pv
        outs.append(acc[:, :B_DV] / acc[:, B_DV:])
    return outs[0] - lam * outs[1]


def _subln(o, g, lam_init):
    return o * lax.rsqrt(jnp.mean(o * o, axis=-1, keepdims=True) + EPS) * g * (1.0 - lam_init)


def _attn_ctx_kernel(lam_ref, *refs, lam_init, layer):
    q_refs, k_refs, v_refs = refs[:B_HEADS], refs[B_HEADS:2 * B_HEADS], refs[2 * B_HEADS:3 * B_HEADS]
    gq_ref, gk_ref, gs_ref = refs[3 * B_HEADS:3 * B_HEADS + 3]
    rest = refs[3 * B_HEADS + 3:]
    if layer:
        pk_ref, pv_ref, o_ref, nk_ref, nv_ref = rest
        nk_ref[0, :layer] = pk_ref[0]
        nv_ref[0, :layer] = pv_ref[0]
    else:
        o_ref, nk_ref, nv_ref = rest
    lam = lam_ref[0, 0]
    for h in range(B_HEADS):
        qn = _half_rms(q_refs[h][...], gq_ref[...]) * (B_DK ** -0.5)
        kn = _half_rms(k_refs[h][...], gk_ref[...])
        v = v_refs[h][...]
        nk_ref[0, layer, 0, h] = kn[:, :B_DK]
        nk_ref[0, layer, 1, h] = kn[:, B_DK:]
        nv_ref[0, layer, h] = v
        o = _diff_softmax_pv(qn.astype(BF16), [kn.astype(BF16)], [_with_ones(v.astype(BF16))], lam)
        o_ref[:, h * LANES:(h + 1) * LANES] = _subln(o, gs_ref[...], lam_init)


def _attn_ctx(proj, lam, gq2, gk2, gs, lam_init, layer, prev_k, prev_v):
    qcol, kcol, vcol = 5 * A_WIDTH // LANES, 5 * A_WIDTH // LANES + 4, 5 * A_WIDTH // LANES + 8
    prev_specs, prev_args = [], []
    if layer:
        prev_specs = [pl.BlockSpec((1, layer, 2, B_HEADS, SEQ, B_DK), lambda b: (b, 0, 0, 0, 0, 0)),
                      pl.BlockSpec((1, layer, B_HEADS, SEQ, B_DV), lambda b: (b, 0, 0, 0, 0))]
        prev_args = [prev_k, prev_v]
    n_l = layer + 1
    head_specs = [pl.BlockSpec((SEQ, LANES), functools.partial(lambda b, col: (b, col), col=c0 + h))
                  for c0 in (qcol, kcol, vcol) for h in range(B_HEADS)]
    return pl.pallas_call(
        functools.partial(_attn_ctx_kernel, lam_init=lam_init, layer=layer),
        grid=(BATCH,),
        in_specs=[pl.BlockSpec(memory_space=pltpu.SMEM)] + head_specs + [
            pl.BlockSpec((1, LANES), lambda b: (0, 0)),
            pl.BlockSpec((1, LANES), lambda b: (0, 0)),
            pl.BlockSpec((1, LANES), lambda b: (0, 0)),
        ] + prev_specs,
        out_specs=[
            pl.BlockSpec((SEQ, B_WIDTH), lambda b: (b, 0)),
            pl.BlockSpec((1, n_l, 2, B_HEADS, SEQ, B_DK), lambda b: (b, 0, 0, 0, 0, 0)),
            pl.BlockSpec((1, n_l, B_HEADS, SEQ, B_DV), lambda b: (b, 0, 0, 0, 0)),
        ],
        out_shape=[
            jax.ShapeDtypeStruct((T_CTX, B_WIDTH), F32),
            jax.ShapeDtypeStruct((BATCH, n_l, 2, B_HEADS, SEQ, B_DK), F32),
            jax.ShapeDtypeStruct((BATCH, n_l, B_HEADS, SEQ, B_DV), F32),
        ],
        compiler_params=_cparams(("arbitrary",)),
        name="diff_attention_ctx",
    )(lam, *([proj] * (3 * B_HEADS)), gq2, gk2, gs, *prev_args)


ATT_TQ = 256
ATT_HEADS = 4


def _attn_smp_kernel(lam_ref, *refs, lam_init):
    nh = ATT_HEADS
    q_refs, k_refs, v_refs = refs[:nh], refs[nh:2 * nh], refs[2 * nh:3 * nh]
    ck_refs, cv_refs = refs[3 * nh:4 * nh], refs[4 * nh:5 * nh]
    cos_ref, sin_ref, gq_ref, gk_ref, gs_ref, o_ref, qs_ref, ks_ref = refs[5 * nh:]
    lam = lam_ref[0, 0]
    cos = cos_ref[...]
    sin = sin_ref[...]
    g = gs_ref[...]
    prepared = []
    for h in range(nh):
        qn = _rope(_half_rms(q_refs[h][...], gq_ref[...]), cos, sin) * (B_DK ** -0.5)
        qs_ref[h] = qn.astype(BF16)
        ks_ref[h] = _rope(_half_rms(k_refs[h][...], gk_ref[...]), cos, sin).astype(BF16)
        ck = jnp.concatenate([ck_refs[h][0, 0, 0, 0], ck_refs[h][0, 0, 1, 0]], axis=-1).astype(BF16)
        cv = _with_ones(cv_refs[h][0, 0, 0].astype(BF16))
        v_bf = _with_ones(v_refs[h][...].astype(BF16))
        prepared.append((ck, cv, v_bf))
    for h in range(nh):
        ck, cv, v_bf = prepared[h]
        k_bf = ks_ref[h]
        for i in range(DEC_SEQ // ATT_TQ):
            rows = slice(i * ATT_TQ, (i + 1) * ATT_TQ)
            o = _diff_softmax_pv(qs_ref[h, rows, :], [k_bf, ck], [v_bf, cv], lam)
            o_ref[rows, h * LANES:(h + 1) * LANES] = _subln(o, g, lam_init)


def _attn_smp(proj, lam, cache_k, cache_v, cos, sin, gq2, gk2, gs, layer, lam_init):
    qcol, kcol, vcol = 5 * A_WIDTH // LANES, 5 * A_WIDTH // LANES + 4, 5 * A_WIDTH // LANES + 8
    r0 = T_CTX // DEC_SEQ
    nh = ATT_HEADS

    def per_head(shape, index):
        return [pl.BlockSpec(shape, functools.partial(index, dh=dh)) for dh in range(nh)]

    head_specs = (
        per_head((DEC_SEQ, LANES), lambda b, hp, dh: (r0 + b, qcol + hp * nh + dh))
        + per_head((DEC_SEQ, LANES), lambda b, hp, dh: (r0 + b, kcol + hp * nh + dh))
        + per_head((DEC_SEQ, LANES), lambda b, hp, dh: (r0 + b, vcol + hp * nh + dh))
        + per_head((1, 1, 2, 1, PAST, B_DK), lambda b, hp, dh: (b, layer, 0, hp * nh + dh, 0, 0))
        + per_head((1, 1, 1, PAST, B_DV), lambda b, hp, dh: (b, layer, hp * nh + dh, 0, 0)))
    return pl.pallas_call(
        functools.partial(_attn_smp_kernel, lam_init=lam_init),
        grid=(DEC_BATCH, B_HEADS // nh),
        in_specs=[pl.BlockSpec(memory_space=pltpu.SMEM)] + head_specs + [
            pl.BlockSpec((DEC_SEQ, LANES), lambda b, hp: (0, 0)),
            pl.BlockSpec((DEC_SEQ, LANES), lambda b, hp: (0, 0)),
            pl.BlockSpec((1, LANES), lambda b, hp: (0, 0)),
            pl.BlockSpec((1, LANES), lambda b, hp: (0, 0)),
            pl.BlockSpec((1, LANES), lambda b, hp: (0, 0)),
        ],
        out_specs=pl.BlockSpec((DEC_SEQ, nh * LANES), lambda b, hp: (b, hp)),
        out_shape=jax.ShapeDtypeStruct((T_SMP, B_WIDTH), F32),
        scratch_shapes=[pltpu.VMEM((nh, DEC_SEQ, LANES), BF16), pltpu.VMEM((nh, DEC_SEQ, LANES), BF16)],
        compiler_params=_cparams(("arbitrary", "arbitrary")),
        name="diff_attention_smp",
    )(lam, *([proj] * (3 * nh)), *([cache_k] * nh), *([cache_v] * nh), cos, sin, gq2, gk2, gs)


def _rope_tables():
    n_rows = DEC_SEQ // GRID_W
    row = np.repeat(np.arange(n_rows), GRID_W).astype(np.float32)
    col = np.tile(np.arange(GRID_W), n_rows).astype(np.float32)
    half = B_DK // 2
    inv_freq = (ROPE_BASE ** (-jnp.arange(0, half, 2, dtype=F32) / half))
    row_ang = jnp.asarray(row)[:, None] * inv_freq
    col_ang = jnp.asarray(col)[:, None] * inv_freq
    ang = jnp.concatenate([row_ang, row_ang, col_ang, col_ang], axis=-1)
    ang = jnp.concatenate([ang, ang], axis=-1)
    sign = np.where((np.arange(LANES) % 32) < 16, -1.0, 1.0).astype(np.float32)
    return jnp.cos(ang), jnp.sin(ang) * sign


CM_ROWS = 1024


def _gelu(x):
    return 0.5 * x * (1.0 + lax.erf(x * (2.0 ** -0.5)))


def _cmlp_kernel(u_ref, v_ref, g_ref, b_ref, ws_ref, bs_ref, o_ref):
    u = _gelu(u_ref[...])
    gv = _gelu(v_ref[...])
    mu = jnp.mean(gv, axis=-1, keepdims=True)
    dv = gv - mu
    var = jnp.mean(dv * dv, axis=-1, keepdims=True)
    vn = (dv * lax.rsqrt(var + EPS) * g_ref[...] + b_ref[...]).astype(BF16)
    lane = _lane((C_CHUNK, LANES))
    for c in range(CM_ROWS // C_CHUNK):
        rs = slice(c * C_CHUNK, (c + 1) * C_CHUNK)
        for p in range(2):
            cs = slice(p * LANES, (p + 1) * LANES)
            vp = vn[rs, cs]
            m0 = jnp.dot(ws_ref[2 * p].astype(BF16), vp, preferred_element_type=F32)
            m1 = jnp.dot(ws_ref[2 * p + 1].astype(BF16), vp, preferred_element_type=F32)
            mixed = jnp.where(lane < C_DG, m0, m1) + bs_ref[:, cs]
            o_ref[rs, cs] = u[rs, cs] * mixed


def _chunk_mlp(proj, ln_g, ln_b, w_s, bias_full):
    ucol = (5 * A_WIDTH + 3 * B_WIDTH) // C_WIDTH
    return pl.pallas_call(
        _cmlp_kernel,
        grid=(T // CM_ROWS,),
        in_specs=[
            pl.BlockSpec((CM_ROWS, C_WIDTH), lambda i: (i, ucol)),
            pl.BlockSpec((CM_ROWS, C_WIDTH), lambda i: (i, ucol + 1)),
            pl.BlockSpec((1, C_WIDTH), lambda i: (0, 0)),
            pl.BlockSpec((1, C_WIDTH), lambda i: (0, 0)),
            pl.BlockSpec((C_GROUPS, C_CHUNK, C_CHUNK), lambda i: (0, 0, 0)),
            pl.BlockSpec((C_CHUNK, C_WIDTH), lambda i: (0, 0)),
        ],
        out_specs=pl.BlockSpec((CM_ROWS, C_WIDTH), lambda i: (i, 0)),
        out_shape=jax.ShapeDtypeStruct((T, C_WIDTH), F32),
        compiler_params=_cparams(("arbitrary",)),
        name="chunk_mlp",
    )(proj, proj, ln_g.reshape(1, C_WIDTH), ln_b.reshape(1, C_WIDTH), w_s, bias_full)


PM_SUB = TM


def _postmix_kernel(*refs, n_x):
    (of_ref, ob_ref, ag_ref, hg_ref, hsel_ref, bc_ref, bs_ref, c_ref, w_ref, gate1_ref, shift2_ref, scale2_ref,
     g2_ref, wrh_ref, wrl_ref, br_ref, x1_ref, h2_ref, idx_ref, gw_ref) = refs[n_x:]
    is_ctx = pl.program_id(0) < CTX_TILES
    x_refs = refs[:n_x]
    for s in range(TM // PM_SUB):
        rows = slice(s * PM_SUB, (s + 1) * PM_SUB)
        x = x_refs[0][rows, :] if n_x == 1 else jnp.where(is_ctx, x_refs[0][rows, :], x_refs[1][rows, :])
        o = of_ref[rows, :] + ob_ref[rows, :]
        ms = _dot_sel(o * o, hsel_ref[...]) * (1.0 / A_DK)
        ag = ag_ref[rows, :]
        a = o * lax.rsqrt(ms + EPS) * hg_ref[...] * (ag * jax.nn.sigmoid(ag))
        b = jnp.where(is_ctx, bc_ref[rows, :], bs_ref[rows, :])
        mixed = jnp.dot(a.astype(BF16), w_ref[0:A_WIDTH, :], preferred_element_type=F32)
        mixed = mixed + jnp.dot(b.astype(BF16), w_ref[A_WIDTH:A_WIDTH + B_WIDTH, :], preferred_element_type=F32)
        mixed = mixed + jnp.dot(c_ref[rows, :].astype(BF16), w_ref[A_WIDTH + B_WIDTH:, :],
                                preferred_element_type=F32)
        x1 = x + gate1_ref[0] * mixed
        x1_ref[rows, :] = x1
        y = x1 * lax.rsqrt(jnp.mean(x1 * x1, axis=-1, keepdims=True) + EPS) * g2_ref[...]
        h2 = y * (1.0 + scale2_ref[0]) + shift2_ref[0]
        h2_ref[rows] = _pack_rows(h2)
        hi = h2.astype(BF16)
        lo = (h2 - hi.astype(F32)).astype(BF16)
        lg = jnp.dot(lo, wrh_ref[...], preferred_element_type=F32)
        lg = lg + jnp.dot(hi, wrl_ref[...], preferred_element_type=F32)
        lg = lg + jnp.dot(hi, wrh_ref[...], preferred_element_type=F32) + br_ref[...]
        lt = lg.T[:N_EXPERTS]
        row = lax.broadcasted_iota(jnp.int32, lt.shape, 0)
        out_row = lax.broadcasted_iota(jnp.int32, (8, PM_SUB), 0)
        idx_out = jnp.zeros((8, PM_SUB), jnp.int32)
        val_out = jnp.zeros((8, PM_SUB), F32)
        top0 = None
        den = None
        for kk in range(TOP_K):
            mx = jnp.max(lt, axis=0, keepdims=True)
            am = jnp.min(jnp.where(lt == mx, row, N_EXPERTS), axis=0, keepdims=True)
            if kk == 0:
                top0 = mx
            e = jnp.exp(mx - top0)
            den = e if den is None else den + e
            idx_out = jnp.where(out_row == kk, am, idx_out)
            val_out = jnp.where(out_row == kk, e, val_out)
            lt = jnp.where(row == am, -jnp.inf, lt)
        idx_ref[:, rows] = idx_out
        gw_ref[:, rows] = val_out / den


def _post_mix(o_f, o_b, proj, hg, hsel, b_ctx, b_smp, c_out, w_out_bf, x, mod3, g2, wr_hi, wr_lo, br_pad):
    tile = lambda w: pl.BlockSpec((TM, w), lambda i: (i, 0))
    const = lambda shape: pl.BlockSpec(shape, lambda i: tuple(0 for _ in shape))
    modspec = lambda j: pl.BlockSpec((1, 1, D), lambda i: (_mod_row(i), 0, j))
    rowsT = pl.BlockSpec((8, TM), lambda i: (0, i))
    x_specs, x_args = _stream_specs(x)
    return pl.pallas_call(
        functools.partial(_postmix_kernel, n_x=len(x_args)),
        grid=(N_TILES,),
        in_specs=x_specs + [
            tile(A_WIDTH), tile(A_WIDTH),
            pl.BlockSpec((TM, A_WIDTH), lambda i: (i, 4)),
            const((1, A_WIDTH)), const((A_WIDTH, A_WIDTH)),
            pl.BlockSpec((TM, B_WIDTH), lambda i: (jnp.minimum(i, CTX_TILES - 1), 0)),
            pl.BlockSpec((TM, B_WIDTH), lambda i: (jnp.maximum(i - CTX_TILES, 0), 0)),
            tile(C_WIDTH),
            const((D, D)),
            modspec(2), modspec(3), modspec(4),
            const((1, D)), const((D, LANES)), const((D, LANES)), const((1, LANES)),
        ],
        out_specs=[tile(D), pl.BlockSpec((TM, ROW_TILES, LANES), lambda i: (i, 0, 0)), rowsT, rowsT],
        out_shape=[
            jax.ShapeDtypeStruct((T, D), F32),
            jax.ShapeDtypeStruct((T, ROW_TILES, LANES), ROW_DT),
            jax.ShapeDtypeStruct((8, T), jnp.int32),
            jax.ShapeDtypeStruct((8, T), F32),
        ],
        compiler_params=_cparams(("arbitrary",)),
        name="post_mix_router",
    )(*x_args, o_f, o_b, proj, hg, hsel, b_ctx, b_smp, c_out, w_out_bf, mod3, mod3, mod3, g2.reshape(1, D),
      wr_hi, wr_lo, br_pad)


def _route_kernel(idx_ref, dest_ref, meta_ref):
    erow = lax.broadcasted_iota(jnp.int32, (N_EXPERTS, TM), 0)
    s_i = lax.broadcasted_iota(jnp.int32, (TM, TM), 0)
    t_i = lax.broadcasted_iota(jnp.int32, (TM, TM), 1)
    earlier = (s_i < t_i).astype(BF16)
    out_row = lax.broadcasted_iota(jnp.int32, (8, TM), 0)

    def onehots(i):
        idx = idx_ref[:, pl.ds(pl.multiple_of(i * TM, TM), TM)]
        return [(erow == idx[kk:kk + 1, :]) for kk in range(TOP_K)]

    def count_tile(i, run):
        ohs = onehots(i)
        base = run
        pos = jnp.zeros((8, TM), F32)
        for kk in range(TOP_K):
            ohf = ohs[kk].astype(F32)
            before = jnp.dot(ohs[kk].astype(BF16), earlier, preferred_element_type=F32)
            p = jnp.sum(ohf * (base + before), axis=0, keepdims=True)
            pos = jnp.where(out_row == kk, p, pos)
            base = base + jnp.sum(ohf, axis=1, keepdims=True)
        dest_ref[:, pl.ds(pl.multiple_of(i * TM, TM), TM)] = pos.astype(jnp.int32)
        return base

    counts = lax.fori_loop(0, N_TILES, count_tile, jnp.zeros((N_EXPERTS, 1), F32)).astype(jnp.int32)
    bm_shift = MOE_BM.bit_length() - 1
    padded = lax.shift_left(lax.shift_right_logical(counts + (MOE_BM - 1), bm_shift), bm_shift)
    e_r = lax.broadcasted_iota(jnp.int32, (N_EXPERTS, N_EXPERTS), 0)
    e_c = lax.broadcasted_iota(jnp.int32, (N_EXPERTS, N_EXPERTS), 1)
    incl = (e_c <= e_r).astype(BF16)
    pad_end = _sel_dot(incl, jnp.broadcast_to(padded.astype(F32), (N_EXPERTS, LANES)))[:, :1]
    pad_start = pad_end - padded.astype(F32)

    def place_tile(i, carry):
        ohs = onehots(i)
        sl = pl.ds(pl.multiple_of(i * TM, TM), TM)
        off = jnp.zeros((8, TM), F32)
        for kk in range(TOP_K):
            o = jnp.sum(ohs[kk].astype(F32) * pad_start, axis=0, keepdims=True)
            off = jnp.where(out_row == kk, o, off)
        dest_ref[:, sl] = dest_ref[:, sl] + off.astype(jnp.int32)
        return carry

    lax.fori_loop(0, N_TILES, place_tile, 0)

    total = jnp.max(pad_end, axis=0, keepdims=True)
    lane_i = lax.broadcasted_iota(jnp.int32, (1, TM), 1)
    blk0 = (lane_i * MOE_BM).astype(F32)
    block_e = jnp.sum((pad_end <= blk0).astype(F32), axis=0, keepdims=True)
    live_end = pad_start + counts.astype(F32)
    sel = erow.astype(F32) == block_e
    live = jnp.sum(jnp.where(sel, live_end, 0.0), axis=0, keepdims=True)
    valid = jnp.where(blk0 < total, jnp.clip(live - blk0, 0.0, float(MOE_BM)), 0.0)
    own = erow == lane_i
    n_blk = jnp.sum(jnp.where(own, padded.astype(F32), 0.0), axis=0, keepdims=True) * (1.0 / MOE_BM)
    first_blk = jnp.sum(jnp.where(own, pad_start, 0.0), axis=0, keepdims=True) * (1.0 / MOE_BM)
    meta = jnp.where(out_row == 0, valid, 0.0)
    meta = jnp.where(out_row == 1, n_blk, meta)
    meta = jnp.where(out_row == 2, first_blk, meta)
    meta = jnp.where(out_row == 3, total * (1.0 / MOE_BM), meta)
    meta_ref[...] = meta.astype(jnp.int32)


def _route(idx_t):
    assert MOE_BLOCKS <= TM
    return pl.pallas_call(
        _route_kernel,
        out_shape=[jax.ShapeDtypeStruct((8, T), jnp.int32), jax.ShapeDtypeStruct((8, TM), jnp.int32)],
        compiler_params=pltpu.CompilerParams(vmem_limit_bytes=VMEM_LIMIT),
        name="moe_route",
    )(idx_t)


def _moe_kernel(bv_ref, nb_ref, g0_ref, tot_ref, x_hbm, wgu_ref, bgu_ref, wdn_ref, bdn_ref, y_hbm,
                wgu_bf, wdn_bf, xbuf, ybuf, xb_ref, xsem, ysem):
    e = pl.program_id(0)
    n_blk = nb_ref[e]
    first = g0_ref[e]
    total = tot_ref[0]
    ahead = MOE_RING - 1

    def x_copy(g):
        slot = g % MOE_RING
        return pltpu.make_async_copy(x_hbm.at[pl.ds(g * MOE_BM, MOE_BM)], xbuf.at[slot], xsem.at[slot])

    def y_copy(g):
        slot = g % MOE_RING
        return pltpu.make_async_copy(ybuf.at[slot], y_hbm.at[pl.ds(g * MOE_BM, MOE_BM)], ysem.at[slot])

    @pl.when(e == 0)
    def _():
        for g in range(ahead):
            @pl.when(g < total)
            def _():
                x_copy(g).start()

    @pl.when(n_blk > 0)
    def _():
        wgu_bf[...] = wgu_ref[0, 0].astype(BF16)
        wdn_bf[...] = wdn_ref[0, 0].astype(BF16)

    def block(j, carry):
        g = first + j
        slot = g % MOE_RING
        x_copy(g).wait()

        @pl.when(g + ahead < total)
        def _():
            x_copy(g + ahead).start()

        @pl.when(g >= MOE_RING)
        def _():
            y_copy(g - MOE_RING).wait()

        n_live = bv_ref[g]

        def ffn(n_rows):
            live = lax.broadcasted_iota(jnp.int32, (n_rows, LANES), 0) < n_live
            for c, chunk in enumerate(_unpack_rows(xbuf[slot, :n_rows])):
                xb_ref[:n_rows, c * LANES:(c + 1) * LANES] = jnp.where(live, chunk, 0.0).astype(BF16)
            gu = jnp.dot(xb_ref[:n_rows, :], wgu_bf[...], preferred_element_type=F32) + bgu_ref[0, 0]
            glu = jnp.minimum(gu[:, :D], SWIGLU_LIMIT)
            lin = jnp.clip(gu[:, D:], -SWIGLU_LIMIT, SWIGLU_LIMIT)
            act = glu * jax.nn.sigmoid(SWIGLU_ALPHA * glu) * (lin + 1.0)
            y = jnp.dot(act.astype(BF16), wdn_bf[...], preferred_element_type=F32) + bdn_ref[0, 0]
            ybuf[slot, :n_rows] = _pack_rows(y)

        @pl.when(n_live > MOE_BM // 2)
        def _():
            ffn(MOE_BM)

        @pl.when(n_live <= MOE_BM // 2)
        def _():
            ffn(MOE_BM // 2)
            ybuf[slot, MOE_BM // 2:] = jnp.zeros((MOE_BM // 2, ROW_TILES, LANES), ROW_DT)

        y_copy(g).start()
        return carry

    lax.fori_loop(0, n_blk, block, 0)

    @pl.when(e == N_EXPERTS - 1)
    def _():
        for back in range(MOE_RING, 0, -1):
            @pl.when(total >= back)
            def _():
                y_copy(total - back).wait()

        def fill(g, carry):
            ybuf[g % MOE_RING] = jnp.zeros((MOE_BM, ROW_TILES, LANES), ROW_DT)
            y_copy(g).start()
            y_copy(g).wait()
            return carry

        lax.fori_loop(total, MOE_BLOCKS, fill, 0)


def _moe_ffn(block_valid, n_blk, first_blk, total_blk, xs, w_gu, b_gu, w_dn, b_dn, layer):
    rows = (MOE_BM, ROW_TILES, LANES)
    return pl.pallas_call(
        _moe_kernel,
        grid_spec=pltpu.PrefetchScalarGridSpec(
            num_scalar_prefetch=4,
            grid=(N_EXPERTS,),
            in_specs=[
                pl.BlockSpec(memory_space=pl.ANY),
                pl.BlockSpec((1, 1, D, 2 * D), lambda e, *_: (layer, e, 0, 0)),
                pl.BlockSpec((1, 1, 1, 2 * D), lambda e, *_: (layer, e, 0, 0)),
                pl.BlockSpec((1, 1, D, D), lambda e, *_: (layer, e, 0, 0)),
                pl.BlockSpec((1, 1, 1, D), lambda e, *_: (layer, e, 0, 0)),
            ],
            out_specs=pl.BlockSpec(memory_space=pl.ANY),
            scratch_shapes=[
                pltpu.VMEM((D, 2 * D), BF16), pltpu.VMEM((D, D), BF16),
                pltpu.VMEM((MOE_RING,) + rows, ROW_DT), pltpu.VMEM((MOE_RING,) + rows, ROW_DT),
                pltpu.VMEM((MOE_BM, D), BF16),
                pltpu.SemaphoreType.DMA((MOE_RING,)), pltpu.SemaphoreType.DMA((MOE_RING,)),
            ],
        ),
        out_shape=jax.ShapeDtypeStruct((MOE_ROWS, ROW_TILES, LANES), ROW_DT),
        compiler_params=_cparams(("arbitrary",)),
        name="moe_expert_ffn",
    )(block_valid, n_blk, first_blk, total_blk, xs, w_gu, b_gu.reshape(DEPTH, N_EXPERTS, 1, 2 * D), w_dn,
      b_dn.reshape(DEPTH, N_EXPERTS, 1, D))


def _combine_kernel(x1_ref, y_ref, gw_ref, gate2_ref, o_ref):
    gw = jnp.concatenate([gw_ref[...], jnp.zeros((LANES - 8, TM), F32)], axis=0).T
    ys = [_unpack_rows(y_ref[kk]) for kk in range(TOP_K)]
    for c in range(D // LANES):
        cs = slice(c * LANES, (c + 1) * LANES)
        acc = None
        for kk in range(TOP_K):
            term = ys[kk][c] * gw[:, kk:kk + 1]
            acc = term if acc is None else acc + term
        o_ref[:, cs] = x1_ref[:, cs] + gate2_ref[0, :, cs] * acc


def _combine(x1, yg, gw, mod3, tile0, n_tiles):
    return pl.pallas_call(
        _combine_kernel,
        grid=(n_tiles,),
        in_specs=[
            pl.BlockSpec((TM, D), lambda i: (tile0 + i, 0)),
            pl.BlockSpec((TOP_K, TM, ROW_TILES, LANES), lambda i: (0, tile0 + i, 0, 0)),
            pl.BlockSpec((8, TM), lambda i: (0, tile0 + i)),
            pl.BlockSpec((1, 1, D), lambda i: (_mod_row(tile0 + i), 0, 5)),
        ],
        out_specs=pl.BlockSpec((TM, D), lambda i: (i, 0)),
        out_shape=jax.ShapeDtypeStruct((n_tiles * TM, D), F32),
        compiler_params=_cparams(("arbitrary",)),
        name="moe_combine",
    )(x1, yg, gw, mod3)


def _sc_mesh():
    return plsc.VectorSubcoreMesh(core_axis_name="c", subcore_axis_name="s")


def _sc_worker():
    return lax.axis_index("s") * SC_CORES + lax.axis_index("c")


def _sc_dispatch(h2t, dest_km):
    per_w = T // SC_WORKERS

    @functools.partial(
        pl.kernel, mesh=_sc_mesh(),
        out_type=jax.ShapeDtypeStruct((MOE_ROWS, ROW_TILES, LANES), ROW_DT),
        scratch_types=[pltpu.VMEM((SC_WIN,), jnp.int32), pltpu.VMEM((SC_WIN, ROW_TILES, LANES), ROW_DT),
                       pltpu.SemaphoreType.DMA],
    )
    def run(h_hbm, d_hbm, o_hbm, idx_v, rows_v, sem):
        w0 = _sc_worker() * per_w

        @pl.loop(0, per_w // SC_WIN)
        def _(w):
            base = pl.multiple_of(w0 + w * SC_WIN, SC_WIN)
            pltpu.sync_copy(h_hbm.at[pl.ds(base, SC_WIN)], rows_v)
            for kk in range(TOP_K):
                pltpu.sync_copy(d_hbm.at[pl.ds(kk * T + base, SC_WIN)], idx_v)
                pltpu.async_copy(rows_v, o_hbm.at[idx_v], sem).wait()

    return run(h2t, dest_km)


def _sc_gather(yb, dest_km):
    n = TOP_K * T
    per_w = n // SC_WORKERS

    @functools.partial(
        pl.kernel, mesh=_sc_mesh(),
        out_type=jax.ShapeDtypeStruct((n, ROW_TILES, LANES), ROW_DT),
        scratch_types=[pltpu.VMEM((SC_WIN,), jnp.int32), pltpu.VMEM((SC_WIN, ROW_TILES, LANES), ROW_DT),
                       pltpu.SemaphoreType.DMA],
    )
    def run(y_hbm, d_hbm, o_hbm, idx_v, rows_v, sem):
        w0 = _sc_worker() * per_w

        @pl.loop(0, per_w // SC_WIN)
        def _(w):
            base = pl.multiple_of(w0 + w * SC_WIN, SC_WIN)
            pltpu.sync_copy(d_hbm.at[pl.ds(base, SC_WIN)], idx_v)
            pltpu.async_copy(y_hbm.at[idx_v], rows_v, sem).wait()
            pltpu.sync_copy(rows_v, o_hbm.at[pl.ds(base, SC_WIN)])

    return run(yb, dest_km)


def kernel(x_prompt, x_sample, c, cache_diff_k, cache_diff_v, state_hgrn, c_ctx, norm_mix_g, norm_ffn_g, w_mod, b_mod, w_in, w_out, hgrn_lower_bounds, hgrn_norm_g, diff_q_norm_g, diff_k_norm_g, diff_lambda_q1, diff_lambda_k1, diff_lambda_q2, diff_lambda_k2, diff_subln_g, cmlp_ln_g, cmlp_ln_b, cmlp_w_s, cmlp_b_s, router_w, router_b, moe_w_gate_up, moe_b_gate_up, moe_w_down, moe_b_down):
    x = (x_prompt.reshape(T_CTX, D), x_sample.reshape(T_SMP, D))
    cvec = jnp.concatenate([c_ctx[None, :], c, jnp.zeros((MOD_ROWS - 1 - DEC_BATCH, D), F32)], axis=0)
    mod = _modulation(cvec, w_mod, b_mod)

    lvl_np, tri_np = _hgrn_tables()
    lvl = jnp.asarray(lvl_np)
    tri = jnp.asarray(tri_np, dtype=BF16)
    cos, sin = _rope_tables()
    hsel = jnp.asarray(np.kron(np.eye(A_HEADS), np.ones((A_DK, A_DK))), dtype=BF16)
    sm = jax.nn.softmax(hgrn_lower_bounds.astype(F32), axis=0)
    lb_all = jnp.cumsum(sm, axis=0) - sm[0]

    new_k, new_v, new_s = None, None, []
    for l in range(DEPTH):
        mod3 = mod[l].reshape(MOD_ROWS, 1, 6 * D)
        proj = _in_projection(x, norm_mix_g[l], mod3, w_in[l].astype(BF16))

        s0 = jnp.concatenate([jnp.zeros((BATCH, 2, A_HEADS, A_DK, A_DK), F32), state_hgrn[:, l]], axis=0)
        o_dir, fin_dir = [], []
        for d in range(2):
            o_d, fin_d = _hgrn_scan(proj, lb_all[l, d].reshape(1, A_WIDTH), _pack_state(s0[:, d]),
                                    lvl, tri, l, d == 1)
            o_dir.append(o_d)
            fin_dir.append(_unpack_state(fin_d[:BATCH]))
        new_s.append(jnp.stack(fin_dir, axis=1))

        lam_init = 0.8 - 0.6 * math.exp(-0.3 * l)
        lam = (jnp.exp(jnp.sum(diff_lambda_q1[l] * diff_lambda_k1[l]))
               - jnp.exp(jnp.sum(diff_lambda_q2[l] * diff_lambda_k2[l])) + lam_init).reshape(1, 1)
        gq2 = jnp.tile(diff_q_norm_g[l], 2).reshape(1, LANES)
        gk2 = jnp.tile(diff_k_norm_g[l], 2).reshape(1, LANES)
        gs = diff_subln_g[l].reshape(1, LANES)
        b_ctx, new_k, new_v = _attn_ctx(proj, lam, gq2, gk2, gs, lam_init, l, new_k, new_v)
        b_smp = _attn_smp(proj, lam, cache_diff_k, cache_diff_v, cos, sin, gq2, gk2, gs, l, lam_init)

        bias_full = jnp.repeat(cmlp_b_s[l].T, C_DG, axis=1)
        c_out = _chunk_mlp(proj, cmlp_ln_g[l], cmlp_ln_b[l], cmlp_w_s[l], bias_full)

        hg = jnp.tile(hgrn_norm_g[l], A_HEADS).reshape(1, A_WIDTH)
        wr_pad = jnp.pad(router_w[l], ((0, 0), (0, LANES - N_EXPERTS)))
        wr_hi = wr_pad.astype(BF16)
        wr_lo = (wr_pad - wr_hi.astype(F32)).astype(BF16)
        br_pad = jnp.pad(router_b[l], (0, LANES - N_EXPERTS)).reshape(1, LANES)
        x1, h2, idx_t, gw_t = _post_mix(o_dir[0], o_dir[1], proj, hg, hsel, b_ctx, b_smp, c_out,
                                        w_out[l].astype(BF16), x, mod3, norm_ffn_g[l], wr_hi, wr_lo, br_pad)

        dest_t, meta = _route(idx_t)
        dest_km = dest_t[:TOP_K].reshape(-1)
        xs = _sc_dispatch(h2, dest_km)
        yb = _moe_ffn(meta[0, :MOE_BLOCKS], meta[1, :N_EXPERTS], meta[2, :N_EXPERTS], meta[3, :1], xs,
                      moe_w_gate_up, moe_b_gate_up, moe_w_down, moe_b_down, l)
        yg = _sc_gather(yb, dest_km).reshape(TOP_K, T, ROW_TILES, LANES)
        x = (_combine(x1, yg, gw_t, mod3, 0, CTX_TILES),
             _combine(x1, yg, gw_t, mod3, CTX_TILES, N_TILES - CTX_TILES))

    y_prompt = x[0].reshape(BATCH, SEQ, D)
    y_sample = x[1].reshape(DEC_BATCH, DEC_SEQ, D)
    return (y_prompt, y_sample, new_k, new_v, jnp.stack(new_s, axis=1))
```

```python
import functools
import math

import numpy as np
import jax
import jax.numpy as jnp
from jax import lax
from jax.experimental import pallas as pl
from jax.experimental.pallas import tpu as pltpu
from jax.experimental.pallas import tpu_sc as plsc

F32 = jnp.float32
BF16 = jnp.bfloat16

D = 1024
DEPTH = 2
BATCH, SEQ = 16, 256
DEC_BATCH, DEC_SEQ = 8, 1024
PAST = 512
GRID_W = 64
A_HEADS, A_DK = 4, 64
A_WIDTH = 256
B_HEADS, B_DK, B_DV = 4, 64, 128
B_WIDTH = 512
C_GROUPS, C_CHUNK, C_WIDTH, C_DG = 4, 128, 256, 64
IN_WIDTH = 5 * A_WIDTH + 3 * B_WIDTH + 2 * C_WIDTH
N_EXPERTS, TOP_K = 32, 4
SWIGLU_LIMIT, SWIGLU_ALPHA = 7.0, 1.702
ROPE_BASE = 10000.0
EPS = 1e-6

T_CTX = BATCH * SEQ
T_SMP = DEC_BATCH * DEC_SEQ
T = T_CTX + T_SMP
N_SEQ = BATCH + DEC_BATCH
MOD_ROWS = 16

TM = 256
N_TILES = T // TM
CTX_TILES = T_CTX // TM
SMP_TILES_PER_SEQ = DEC_SEQ // TM
LANES = 128
MOE_BM = 256
MOE_ROWS = T * TOP_K + N_EXPERTS * MOE_BM
MOE_BLOCKS = MOE_ROWS // MOE_BM
MOE_RING = 4
ROW_WORDS = D // 2
ROW_TILES = ROW_WORDS // LANES
ROW_DT = jnp.int32
SC_CORES, SC_SUBCORES = 2, 16
SC_WORKERS = SC_CORES * SC_SUBCORES
SC_WIN = 128
VMEM_LIMIT = 56 * 1024 * 1024


def _cparams(sem):
    return pltpu.CompilerParams(dimension_semantics=sem, vmem_limit_bytes=VMEM_LIMIT)


def _mod_row(i, rows=TM):
    ctx_tiles = T_CTX // rows
    return jnp.where(i < ctx_tiles, 0, 1 + (i - ctx_tiles) // (DEC_SEQ // rows))


def _split3(x):
    hi = x.astype(BF16)
    r = x - hi.astype(F32)
    mid = r.astype(BF16)
    lo = (r - mid.astype(F32)).astype(BF16)
    return hi, mid, lo


def _sel_dot(sel, x):
    hi, mid, lo = _split3(x)
    acc = jnp.dot(sel, lo, preferred_element_type=F32)
    acc = acc + jnp.dot(sel, mid, preferred_element_type=F32)
    return acc + jnp.dot(sel, hi, preferred_element_type=F32)


def _dot_sel(x, sel):
    hi, mid, lo = _split3(x)
    acc = jnp.dot(lo, sel, preferred_element_type=F32)
    acc = acc + jnp.dot(mid, sel, preferred_element_type=F32)
    return acc + jnp.dot(hi, sel, preferred_element_type=F32)


def _dot_nt(a, b):
    return lax.dot_general(a, b, (((1,), (1,)), ((), ())), preferred_element_type=F32)


def _dot_tn(a, b):
    return lax.dot_general(a, b, (((0,), (0,)), ((), ())), preferred_element_type=F32)


def _lane(shape):
    return lax.broadcasted_iota(jnp.int32, shape, len(shape) - 1)


def _pack_rows(x):
    hi = lax.bitcast_convert_type(x[:, :ROW_WORDS].astype(BF16).astype(F32), jnp.int32)
    lo = lax.bitcast_convert_type(x[:, ROW_WORDS:].astype(BF16).astype(F32), jnp.int32)
    words = hi | lax.shift_right_logical(lo, 16)
    return pltpu.einshape("t(jl)->tjl", words, l=LANES)


def _unpack_rows(words3):
    wt = pltpu.einshape("tjl->jtl", words3)
    hi = [lax.bitcast_convert_type(wt[j] & jnp.int32(-65536), F32) for j in range(ROW_TILES)]
    lo = [lax.bitcast_convert_type(lax.shift_left(wt[j], 16), F32) for j in range(ROW_TILES)]
    return hi + lo


def _mod_kernel(c_ref, w_ref, b_ref, o_ref):
    c = c_ref[...]
    s = c * jax.nn.sigmoid(c)
    o_ref[0] = jnp.dot(s.astype(BF16), w_ref[0].astype(BF16), preferred_element_type=F32) + b_ref[0]


def _modulation(cvec, w_mod, b_mod):
    tn = 1536
    return pl.pallas_call(
        _mod_kernel,
        grid=(DEPTH, 6 * D // tn),
        in_specs=[
            pl.BlockSpec((MOD_ROWS, D), lambda l, j: (0, 0)),
            pl.BlockSpec((1, D, tn), lambda l, j: (l, 0, j)),
            pl.BlockSpec((1, 1, tn), lambda l, j: (l, 0, j)),
        ],
        out_specs=pl.BlockSpec((1, MOD_ROWS, tn), lambda l, j: (l, 0, j)),
        out_shape=jax.ShapeDtypeStruct((DEPTH, MOD_ROWS, 6 * D), F32),
        compiler_params=_cparams(("arbitrary", "arbitrary")),
        name="modulation",
    )(cvec, w_mod, b_mod.reshape(DEPTH, 1, 6 * D))


def _stream_specs(x, rows=TM):
    ctx_tiles = T_CTX // rows
    if isinstance(x, tuple):
        return [pl.BlockSpec((rows, D), lambda i: (jnp.minimum(i, ctx_tiles - 1), 0)),
                pl.BlockSpec((rows, D), lambda i: (jnp.maximum(i - ctx_tiles, 0), 0))], list(x)
    return [pl.BlockSpec((rows, D), lambda i: (i, 0))], [x]


IN_ROWS = 512


def _inproj_kernel(*refs, n_x):
    g_ref, shift_ref, scale_ref, w_ref, o_ref = refs[n_x:]
    is_ctx = pl.program_id(0) < T_CTX // IN_ROWS
    for s in range(IN_ROWS // TM):
        rows = slice(s * TM, (s + 1) * TM)
        x = refs[0][rows, :] if n_x == 1 else jnp.where(is_ctx, refs[0][rows, :], refs[1][rows, :])
        y = x * lax.rsqrt(jnp.mean(x * x, axis=-1, keepdims=True) + EPS) * g_ref[...]
        h = y * (1.0 + scale_ref[0]) + shift_ref[0]
        o_ref[rows, :] = jnp.dot(h.astype(BF16), w_ref[...], preferred_element_type=F32)


def _in_projection(x, g, mod3, w_in_bf):
    x_specs, x_args = _stream_specs(x, IN_ROWS)
    mod_row = functools.partial(_mod_row, rows=IN_ROWS)
    return pl.pallas_call(
        functools.partial(_inproj_kernel, n_x=len(x_args)),
        grid=(T // IN_ROWS,),
        in_specs=x_specs + [
            pl.BlockSpec((1, D), lambda i: (0, 0)),
            pl.BlockSpec((1, 1, D), lambda i: (mod_row(i), 0, 0)),
            pl.BlockSpec((1, 1, D), lambda i: (mod_row(i), 0, 1)),
            pl.BlockSpec((D, IN_WIDTH), lambda i: (0, 0)),
        ],
        out_specs=pl.BlockSpec((IN_ROWS, IN_WIDTH), lambda i: (i, 0)),
        out_shape=jax.ShapeDtypeStruct((T, IN_WIDTH), F32),
        compiler_params=_cparams(("arbitrary",)),
        name="in_projection",
    )(*x_args, g.reshape(1, D), mod3, mod3, w_in_bf)


HG_C = 128
HG_LEVELS = tuple(2 ** j for j in range(1, int(math.log2(HG_C)) + 1))


def _hgrn_tables():
    t = np.arange(HG_C)[:, None]
    s = np.arange(HG_C)[None, :]
    x = t ^ s
    lvl = np.zeros((HG_C, HG_C), np.int32)
    nz = x > 0
    lvl[nz] = np.floor(np.log2(x[nz])).astype(np.int32) + 1
    fwd = np.where(t >= s, lvl, -1).astype(np.int32)
    bwd = np.where(t <= s, lvl, -1).astype(np.int32)
    tri_f = (t >= s).astype(np.float32)
    tri_b = (t <= s).astype(np.float32)
    return np.stack([fwd, bwd]), np.stack([tri_f, tri_b])


def _block_ref(cum, m, idx):
    c, l = cum.shape
    if m >= 16:
        c3 = cum.reshape(c // m, m, l)
        r = c3[:, idx:idx + 1, :]
        return jnp.broadcast_to(r, (c // m, m, l)).reshape(c, l)
    c3 = cum.reshape(c // 8, 8, l)
    sub = lax.broadcasted_iota(jnp.int32, c3.shape, 1)
    out = None
    for j in range(8 // m - 1, -1, -1):
        cand = jnp.broadcast_to(c3[:, j * m + idx:j * m + idx + 1, :], c3.shape)
        out = cand if out is None else jnp.where(sub < (j + 1) * m, cand, out)
    return out.reshape(c, l)


def _hgrn_kernel(q_ref, z_ref, v_ref, lb_ref, s0_ref, lvl_ref, tri_ref, o_ref, fin_ref, st_ref, *, layer, rev):
    g = pl.program_id(0)
    first = jnp.logical_or(g < CTX_TILES, (g - CTX_TILES) % SMP_TILES_PER_SEQ == 0)

    @pl.when(first)
    def _():
        st_ref[...] = s0_ref[0]

    qr = q_ref[...]
    q = qr * jax.nn.sigmoid(qr) * (A_DK ** -0.5)
    z = z_ref[...]
    if layer == 0:
        lf = jnp.minimum(z, 0.0) - jnp.log(1.0 + jnp.exp(-jnp.abs(z)))
        k = jax.nn.sigmoid(-z)
    else:
        lbd = lb_ref[...]
        lf = jnp.log(lbd + (1.0 - lbd) * jax.nn.sigmoid(z))
        k = (1.0 - lbd) * jax.nn.sigmoid(-z)
    v = v_ref[...]
    tri = tri_ref[0]
    lvl = lvl_ref[0]
    last_row = 0 if rev else HG_C - 1
    n_chunks = TM // HG_C
    order = range(n_chunks - 1, -1, -1) if rev else range(n_chunks)
    lf2 = lf * math.log2(math.e)
    cums = [_sel_dot(tri, lf2[c * HG_C:(c + 1) * HG_C]) for c in range(n_chunks)]
    lane = _lane((HG_C, LANES))
    head_masks = (lane < A_DK, lane >= A_DK)
    lane_row = _lane((1, LANES))
    head_keep = ((lane_row < A_DK).astype(BF16), (lane_row >= A_DK).astype(BF16))
    lvl2 = jnp.concatenate([lvl, lvl], axis=0)
    level_masks = [lvl2 == i for i in range(len(HG_LEVELS) + 1)]
    r = lax.broadcasted_iota(jnp.int32, (LANES, LANES), 0)
    cl = lax.broadcasted_iota(jnp.int32, (LANES, LANES), 1)
    same_head = (r < A_DK) == (cl < A_DK)

    def chunk(q_p, k_p, v_p, cum_p, st):
        v_bf = v_p.astype(BF16)
        k_bf = k_p.astype(BF16)
        q_bf = q_p.astype(BF16)

        def both_heads(x_bf):
            return jnp.concatenate([x_bf * head_keep[0], x_bf * head_keep[1]], axis=0)

        scores = jnp.where(level_masks[0], _dot_nt(both_heads(q_bf), k_bf), 0.0)
        for li, m in enumerate(HG_LEVELS):
            ref = _block_ref(cum_p, m, m // 2 if rev else m // 2 - 1)
            dec = jnp.exp2(-jnp.abs(cum_p - ref))
            qd = (q_p * dec).astype(BF16)
            kd = (k_p * dec).astype(BF16)
            scores = jnp.where(level_masks[li + 1], _dot_nt(both_heads(qd), kd), scores)
        pv = jnp.dot(scores.astype(BF16), v_bf, preferred_element_type=F32)
        o_intra = jnp.where(head_masks[0], pv[:HG_C], pv[HG_C:])
        q0 = (q_p * jnp.exp2(cum_p)).astype(BF16)
        out = o_intra + _dot_nt(q0, st.astype(BF16))
        last = cum_p[last_row:last_row + 1, :]
        ks = (k_p * jnp.exp2(last - cum_p)).astype(BF16)
        upd = _dot_tn(v_bf, ks)
        return out, st * jnp.exp2(last) + jnp.where(same_head, upd, 0.0)

    for p in range(2):
        sl = slice(p * LANES, (p + 1) * LANES)
        st = st_ref[p]
        for c in order:
            rows = slice(c * HG_C, (c + 1) * HG_C)
            o_ref[rows, sl], st = chunk(q[rows, sl], k[rows, sl], v[rows, sl], cums[c][:, sl], st)
        st_ref[p] = st
        fin_ref[0, p] = st


def _hgrn_seq(g):
    return jnp.where(g < CTX_TILES, g, CTX_TILES + (g - CTX_TILES) // SMP_TILES_PER_SEQ)


def _hgrn_blk(g, rev):
    if not rev:
        return g
    j = g - CTX_TILES
    return jnp.where(g < CTX_TILES, g,
                     CTX_TILES + (j // SMP_TILES_PER_SEQ) * SMP_TILES_PER_SEQ
                     + (SMP_TILES_PER_SEQ - 1 - j % SMP_TILES_PER_SEQ))


def _hgrn_scan(proj, lb_dir, s0_dir, lvl, tri, layer, rev):
    d = 1 if rev else 0
    blk = functools.partial(_hgrn_blk, rev=rev)
    return pl.pallas_call(
        functools.partial(_hgrn_kernel, layer=layer, rev=rev),
        grid=(N_TILES,),
        in_specs=[
            pl.BlockSpec((TM, A_WIDTH), lambda g: (blk(g), 0)),
            pl.BlockSpec((TM, A_WIDTH), lambda g: (blk(g), 1 + d)),
            pl.BlockSpec((TM, A_WIDTH), lambda g: (blk(g), 3)),
            pl.BlockSpec((1, A_WIDTH), lambda g: (0, 0)),
            pl.BlockSpec((1, 2, LANES, LANES), lambda g: (_hgrn_seq(g), 0, 0, 0)),
            pl.BlockSpec((1, HG_C, HG_C), lambda g: (d, 0, 0)),
            pl.BlockSpec((1, HG_C, HG_C), lambda g: (d, 0, 0)),
        ],
        out_specs=[
            pl.BlockSpec((TM, A_WIDTH), lambda g: (blk(g), 0)),
            pl.BlockSpec((1, 2, LANES, LANES), lambda g: (_hgrn_seq(g), 0, 0, 0)),
        ],
        out_shape=[
            jax.ShapeDtypeStruct((T, A_WIDTH), F32),
            jax.ShapeDtypeStruct((N_SEQ, 2, LANES, LANES), F32),
        ],
        scratch_shapes=[pltpu.VMEM((2, LANES, LANES), F32)],
        compiler_params=_cparams(("arbitrary",)),
        name=f"hgrn_scan_{'bwd' if rev else 'fwd'}",
    )(proj, proj, proj, lb_dir, s0_dir, lvl, tri)


def _pack_state(s):
    n = s.shape[0]
    st = jnp.swapaxes(s, -1, -2).reshape(n, 2, 2, A_DK, A_DK)
    z = jnp.zeros_like(st[:, :, 0])
    top = jnp.concatenate([st[:, :, 0], z], axis=-1)
    bot = jnp.concatenate([z, st[:, :, 1]], axis=-1)
    return jnp.concatenate([top, bot], axis=-2)


def _unpack_state(sp):
    n = sp.shape[0]
    h0 = sp[:, :, :A_DK, :A_DK]
    h1 = sp[:, :, A_DK:, A_DK:]
    st = jnp.stack([h0, h1], axis=2).reshape(n, A_HEADS, A_DK, A_DK)
    return jnp.swapaxes(st, -1, -2)


def _half_rms(x, g):
    r = lax.broadcasted_iota(jnp.int32, (LANES, LANES), 0)
    c = lax.broadcasted_iota(jnp.int32, (LANES, LANES), 1)
    half_mean = jnp.where((r < B_DK) == (c < B_DK), 1.0 / B_DK, 0.0).astype(BF16)
    xx = x * x
    hi = xx.astype(BF16)
    lo = (xx - hi.astype(F32)).astype(BF16)
    ms = jnp.dot(lo, half_mean, preferred_element_type=F32) + jnp.dot(hi, half_mean, preferred_element_type=F32)
    return x * lax.rsqrt(ms + EPS) * g


def _rope(x, cos, sin_signed):
    lane = _lane(x.shape)
    first = (lane % 32) < 16
    rot = jnp.where(first, pltpu.roll(x, LANES - 16, 1), pltpu.roll(x, 16, 1))
    return x * cos + rot * sin_signed


def _with_ones(v_bf):
    return jnp.concatenate([v_bf, jnp.ones_like(v_bf)], axis=-1)


def _diff_softmax_pv(q_bf, keys_bf, vals_ext, lam):
    lane = _lane(q_bf.shape)
    zero = jnp.zeros_like(q_bf)
    outs = []
    for mp in range(2):
        qm = jnp.where((lane < B_DK) == (mp == 0), q_bf, zero)
        s = [_dot_nt(qm, kk) for kk in keys_bf]
        mx = functools.reduce(jnp.maximum, [jnp.max(si, axis=-1, keepdims=True) for si in s])
        acc = None
        for si, ve in zip(s, vals_ext):
            e = jnp.exp((si - mx).astype(BF16))
            pv = jnp.dot(e, ve, preferred_element_type=F32)
            acc = pv if acc is None else acc + pv
        outs.append(acc[:, :B_DV] / acc[:, B_DV:])
    return outs[0] - lam * outs[1]


def _subln(o, g, lam_init):
    return o * lax.rsqrt(jnp.mean(o * o, axis=-1, keepdims=True) + EPS) * g * (1.0 - lam_init)


def _attn_ctx_kernel(lam_ref, *refs, lam_init, layer):
    q_refs, k_refs, v_refs = refs[:B_HEADS], refs[B_HEADS:2 * B_HEADS], refs[2 * B_HEADS:3 * B_HEADS]
    gq_ref, gk_ref, gs_ref = refs[3 * B_HEADS:3 * B_HEADS + 3]
    rest = refs[3 * B_HEADS + 3:]
    if layer:
        pk_ref, pv_ref, o_ref, nk_ref, nv_ref = rest
        nk_ref[0, :layer] = pk_ref[0]
        nv_ref[0, :layer] = pv_ref[0]
    else:
        o_ref, nk_ref, nv_ref = rest
    lam = lam_ref[0, 0]
    for h in range(B_HEADS):
        qn = _half_rms(q_refs[h][...], gq_ref[...]) * (B_DK ** -0.5)
        kn = _half_rms(k_refs[h][...], gk_ref[...])
        v = v_refs[h][...]
        nk_ref[0, layer, 0, h] = kn[:, :B_DK]
        nk_ref[0, layer, 1, h] = kn[:, B_DK:]
        nv_ref[0, layer, h] = v
        o = _diff_softmax_pv(qn.astype(BF16), [kn.astype(BF16)], [_with_ones(v.astype(BF16))], lam)
        o_ref[:, h * LANES:(h + 1) * LANES] = _subln(o, gs_ref[...], lam_init)


def _attn_ctx(proj, lam, gq2, gk2, gs, lam_init, layer, prev_k, prev_v):
    qcol, kcol, vcol = 5 * A_WIDTH // LANES, 5 * A_WIDTH // LANES + 4, 5 * A_WIDTH // LANES + 8
    prev_specs, prev_args = [], []
    if layer:
        prev_specs = [pl.BlockSpec((1, layer, 2, B_HEADS, SEQ, B_DK), lambda b: (b, 0, 0, 0, 0, 0)),
                      pl.BlockSpec((1, layer, B_HEADS, SEQ, B_DV), lambda b: (b, 0, 0, 0, 0))]
        prev_args = [prev_k, prev_v]
    n_l = layer + 1
    head_specs = [pl.BlockSpec((SEQ, LANES), functools.partial(lambda b, col: (b, col), col=c0 + h))
                  for c0 in (qcol, kcol, vcol) for h in range(B_HEADS)]
    return pl.pallas_call(
        functools.partial(_attn_ctx_kernel, lam_init=lam_init, layer=layer),
        grid=(BATCH,),
        in_specs=[pl.BlockSpec(memory_space=pltpu.SMEM)] + head_specs + [
            pl.BlockSpec((1, LANES), lambda b: (0, 0)),
            pl.BlockSpec((1, LANES), lambda b: (0, 0)),
            pl.BlockSpec((1, LANES), lambda b: (0, 0)),
        ] + prev_specs,
        out_specs=[
            pl.BlockSpec((SEQ, B_WIDTH), lambda b: (b, 0)),
            pl.BlockSpec((1, n_l, 2, B_HEADS, SEQ, B_DK), lambda b: (b, 0, 0, 0, 0, 0)),
            pl.BlockSpec((1, n_l, B_HEADS, SEQ, B_DV), lambda b: (b, 0, 0, 0, 0)),
        ],
        out_shape=[
            jax.ShapeDtypeStruct((T_CTX, B_WIDTH), F32),
            jax.ShapeDtypeStruct((BATCH, n_l, 2, B_HEADS, SEQ, B_DK), F32),
            jax.ShapeDtypeStruct((BATCH, n_l, B_HEADS, SEQ, B_DV), F32),
        ],
        compiler_params=_cparams(("arbitrary",)),
        name="diff_attention_ctx",
    )(lam, *([proj] * (3 * B_HEADS)), gq2, gk2, gs, *prev_args)


ATT_TQ = 256
ATT_HEADS = 4


def _attn_smp_kernel(lam_ref, *refs, lam_init):
    nh = ATT_HEADS
    q_refs, k_refs, v_refs = refs[:nh], refs[nh:2 * nh], refs[2 * nh:3 * nh]
    ck_refs, cv_refs = refs[3 * nh:4 * nh], refs[4 * nh:5 * nh]
    cos_ref, sin_ref, gq_ref, gk_ref, gs_ref, o_ref, qs_ref, ks_ref = refs[5 * nh:]
    lam = lam_ref[0, 0]
    cos = cos_ref[...]
    sin = sin_ref[...]
    g = gs_ref[...]
    prepared = []
    for h in range(nh):
        qn = _rope(_half_rms(q_refs[h][...], gq_ref[...]), cos, sin) * (B_DK ** -0.5)
        qs_ref[h] = qn.astype(BF16)
        ks_ref[h] = _rope(_half_rms(k_refs[h][...], gk_ref[...]), cos, sin).astype(BF16)
        ck = jnp.concatenate([ck_refs[h][0, 0, 0, 0], ck_refs[h][0, 0, 1, 0]], axis=-1).astype(BF16)
        cv = _with_ones(cv_refs[h][0, 0, 0].astype(BF16))
        v_bf = _with_ones(v_refs[h][...].astype(BF16))
        prepared.append((ck, cv, v_bf))
    for h in range(nh):
        ck, cv, v_bf = prepared[h]
        k_bf = ks_ref[h]
        for i in range(DEC_SEQ // ATT_TQ):
            rows = slice(i * ATT_TQ, (i + 1) * ATT_TQ)
            o = _diff_softmax_pv(qs_ref[h, rows, :], [k_bf, ck], [v_bf, cv], lam)
            o_ref[rows, h * LANES:(h + 1) * LANES] = _subln(o, g, lam_init)


def _attn_smp(proj, lam, cache_k, cache_v, cos, sin, gq2, gk2, gs, layer, lam_init):
    qcol, kcol, vcol = 5 * A_WIDTH // LANES, 5 * A_WIDTH // LANES + 4, 5 * A_WIDTH // LANES + 8
    r0 = T_CTX // DEC_SEQ
    nh = ATT_HEADS

    def per_head(shape, index):
        return [pl.BlockSpec(shape, functools.partial(index, dh=dh)) for dh in range(nh)]

    head_specs = (
        per_head((DEC_SEQ, LANES), lambda b, hp, dh: (r0 + b, qcol + hp * nh + dh))
        + per_head((DEC_SEQ, LANES), lambda b, hp, dh: (r0 + b, kcol + hp * nh + dh))
        + per_head((DEC_SEQ, LANES), lambda b, hp, dh: (r0 + b, vcol + hp * nh + dh))
        + per_head((1, 1, 2, 1, PAST, B_DK), lambda b, hp, dh: (b, layer, 0, hp * nh + dh, 0, 0))
        + per_head((1, 1, 1, PAST, B_DV), lambda b, hp, dh: (b, layer, hp * nh + dh, 0, 0)))
    return pl.pallas_call(
        functools.partial(_attn_smp_kernel, lam_init=lam_init),
        grid=(DEC_BATCH, B_HEADS // nh),
        in_specs=[pl.BlockSpec(memory_space=pltpu.SMEM)] + head_specs + [
            pl.BlockSpec((DEC_SEQ, LANES), lambda b, hp: (0, 0)),
            pl.BlockSpec((DEC_SEQ, LANES), lambda b, hp: (0, 0)),
            pl.BlockSpec((1, LANES), lambda b, hp: (0, 0)),
            pl.BlockSpec((1, LANES), lambda b, hp: (0, 0)),
            pl.BlockSpec((1, LANES), lambda b, hp: (0, 0)),
        ],
        out_specs=pl.BlockSpec((DEC_SEQ, nh * LANES), lambda b, hp: (b, hp)),
        out_shape=jax.ShapeDtypeStruct((T_SMP, B_WIDTH), F32),
        scratch_shapes=[pltpu.VMEM((nh, DEC_SEQ, LANES), BF16), pltpu.VMEM((nh, DEC_SEQ, LANES), BF16)],
        compiler_params=_cparams(("arbitrary", "arbitrary")),
        name="diff_attention_smp",
    )(lam, *([proj] * (3 * nh)), *([cache_k] * nh), *([cache_v] * nh), cos, sin, gq2, gk2, gs)


def _rope_tables():
    n_rows = DEC_SEQ // GRID_W
    row = np.repeat(np.arange(n_rows), GRID_W).astype(np.float32)
    col = np.tile(np.arange(GRID_W), n_rows).astype(np.float32)
    half = B_DK // 2
    inv_freq = (ROPE_BASE ** (-jnp.arange(0, half, 2, dtype=F32) / half))
    row_ang = jnp.asarray(row)[:, None] * inv_freq
    col_ang = jnp.asarray(col)[:, None] * inv_freq
    ang = jnp.concatenate([row_ang, row_ang, col_ang, col_ang], axis=-1)
    ang = jnp.concatenate([ang, ang], axis=-1)
    sign = np.where((np.arange(LANES) % 32) < 16, -1.0, 1.0).astype(np.float32)
    return jnp.cos(ang), jnp.sin(ang) * sign


CM_ROWS = 1024


def _gelu(x):
    return 0.5 * x * (1.0 + lax.erf(x * (2.0 ** -0.5)))


def _cmlp_kernel(u_ref, v_ref, g_ref, b_ref, ws_ref, bs_ref, o_ref):
    u = _gelu(u_ref[...])
    gv = _gelu(v_ref[...])
    mu = jnp.mean(gv, axis=-1, keepdims=True)
    dv = gv - mu
    var = jnp.mean(dv * dv, axis=-1, keepdims=True)
    vn = (dv * lax.rsqrt(var + EPS) * g_ref[...] + b_ref[...]).astype(BF16)
    lane = _lane((C_CHUNK, LANES))
    for c in range(CM_ROWS // C_CHUNK):
        rs = slice(c * C_CHUNK, (c + 1) * C_CHUNK)
        for p in range(2):
            cs = slice(p * LANES, (p + 1) * LANES)
            vp = vn[rs, cs]
            m0 = jnp.dot(ws_ref[2 * p].astype(BF16), vp, preferred_element_type=F32)
            m1 = jnp.dot(ws_ref[2 * p + 1].astype(BF16), vp, preferred_element_type=F32)
            mixed = jnp.where(lane < C_DG, m0, m1) + bs_ref[:, cs]
            o_ref[rs, cs] = u[rs, cs] * mixed


def _chunk_mlp(proj, ln_g, ln_b, w_s, bias_full):
    ucol = (5 * A_WIDTH + 3 * B_WIDTH) // C_WIDTH
    return pl.pallas_call(
        _cmlp_kernel,
        grid=(T // CM_ROWS,),
        in_specs=[
            pl.BlockSpec((CM_ROWS, C_WIDTH), lambda i: (i, ucol)),
            pl.BlockSpec((CM_ROWS, C_WIDTH), lambda i: (i, ucol + 1)),
            pl.BlockSpec((1, C_WIDTH), lambda i: (0, 0)),
            pl.BlockSpec((1, C_WIDTH), lambda i: (0, 0)),
            pl.BlockSpec((C_GROUPS, C_CHUNK, C_CHUNK), lambda i: (0, 0, 0)),
            pl.BlockSpec((C_CHUNK, C_WIDTH), lambda i: (0, 0)),
        ],
        out_specs=pl.BlockSpec((CM_ROWS, C_WIDTH), lambda i: (i, 0)),
        out_shape=jax.ShapeDtypeStruct((T, C_WIDTH), F32),
        compiler_params=_cparams(("arbitrary",)),
        name="chunk_mlp",
    )(proj, proj, ln_g.reshape(1, C_WIDTH), ln_b.reshape(1, C_WIDTH), w_s, bias_full)


PM_ROWS = 512
PM_SUB = TM


def _postmix_kernel(*refs, n_x):
    (of_ref, ob_ref, ag_ref, hg_ref, hsel_ref, bc_ref, bs_ref, c_ref, w_ref, gate1_ref, shift2_ref, scale2_ref,
     g2_ref, wrh_ref, wrl_ref, br_ref, x1_ref, h2_ref, idx_ref, gw_ref) = refs[n_x:]
    is_ctx = pl.program_id(0) < T_CTX // PM_ROWS
    x_refs = refs[:n_x]
    for s in range(PM_ROWS // PM_SUB):
        rows = slice(s * PM_SUB, (s + 1) * PM_SUB)
        x = x_refs[0][rows, :] if n_x == 1 else jnp.where(is_ctx, x_refs[0][rows, :], x_refs[1][rows, :])
        o = of_ref[rows, :] + ob_ref[rows, :]
        ms = _dot_sel(o * o, hsel_ref[...]) * (1.0 / A_DK)
        ag = ag_ref[rows, :]
        a = o * lax.rsqrt(ms + EPS) * hg_ref[...] * (ag * jax.nn.sigmoid(ag))
        b = jnp.where(is_ctx, bc_ref[rows, :], bs_ref[rows, :])
        mixed = jnp.dot(a.astype(BF16), w_ref[0:A_WIDTH, :], preferred_element_type=F32)
        mixed = mixed + jnp.dot(b.astype(BF16), w_ref[A_WIDTH:A_WIDTH + B_WIDTH, :], preferred_element_type=F32)
        mixed = mixed + jnp.dot(c_ref[rows, :].astype(BF16), w_ref[A_WIDTH + B_WIDTH:, :],
                                preferred_element_type=F32)
        x1 = x + gate1_ref[0] * mixed
        x1_ref[rows, :] = x1
        y = x1 * lax.rsqrt(jnp.mean(x1 * x1, axis=-1, keepdims=True) + EPS) * g2_ref[...]
        h2 = y * (1.0 + scale2_ref[0]) + shift2_ref[0]
        h2_ref[rows] = _pack_rows(h2)
        hi = h2.astype(BF16)
        lo = (h2 - hi.astype(F32)).astype(BF16)
        lg = jnp.dot(lo, wrh_ref[...], preferred_element_type=F32)
        lg = lg + jnp.dot(hi, wrl_ref[...], preferred_element_type=F32)
        lg = lg + jnp.dot(hi, wrh_ref[...], preferred_element_type=F32) + br_ref[...]
        lt = lg.T[:N_EXPERTS]
        row = lax.broadcasted_iota(jnp.int32, lt.shape, 0)
        out_row = lax.broadcasted_iota(jnp.int32, (8, PM_SUB), 0)
        idx_out = jnp.zeros((8, PM_SUB), jnp.int32)
        val_out = jnp.zeros((8, PM_SUB), F32)
        top0 = None
        den = None
        for kk in range(TOP_K):
            mx = jnp.max(lt, axis=0, keepdims=True)
            am = jnp.min(jnp.where(lt == mx, row, N_EXPERTS), axis=0, keepdims=True)
            if kk == 0:
                top0 = mx
            e = jnp.exp(mx - top0)
            den = e if den is None else den + e
            idx_out = jnp.where(out_row == kk, am, idx_out)
            val_out = jnp.where(out_row == kk, e, val_out)
            lt = jnp.where(row == am, -jnp.inf, lt)
        idx_ref[:, rows] = idx_out
        gw_ref[:, rows] = val_out / den


def _post_mix(o_f, o_b, proj, hg, hsel, b_ctx, b_smp, c_out, w_out_bf, x, mod3, g2, wr_hi, wr_lo, br_pad):
    tile = lambda w: pl.BlockSpec((PM_ROWS, w), lambda i: (i, 0))
    const = lambda shape: pl.BlockSpec(shape, lambda i: tuple(0 for _ in shape))
    modspec = lambda j: pl.BlockSpec((1, 1, D), lambda i: (_mod_row(i, PM_ROWS), 0, j))
    rowsT = pl.BlockSpec((8, PM_ROWS), lambda i: (0, i))
    x_specs, x_args = _stream_specs(x, PM_ROWS)
    ctx_tiles = T_CTX // PM_ROWS
    return pl.pallas_call(
        functools.partial(_postmix_kernel, n_x=len(x_args)),
        grid=(T // PM_ROWS,),
        in_specs=x_specs + [
            tile(A_WIDTH), tile(A_WIDTH),
            pl.BlockSpec((PM_ROWS, A_WIDTH), lambda i: (i, 4)),
            const((1, A_WIDTH)), const((A_WIDTH, A_WIDTH)),
            pl.BlockSpec((PM_ROWS, B_WIDTH), lambda i: (jnp.minimum(i, ctx_tiles - 1), 0)),
            pl.BlockSpec((PM_ROWS, B_WIDTH), lambda i: (jnp.maximum(i - ctx_tiles, 0), 0)),
            tile(C_WIDTH),
            const((D, D)),
            modspec(2), modspec(3), modspec(4),
            const((1, D)), const((D, LANES)), const((D, LANES)), const((1, LANES)),
        ],
        out_specs=[tile(D), pl.BlockSpec((PM_ROWS, ROW_TILES, LANES), lambda i: (i, 0, 0)), rowsT, rowsT],
        out_shape=[
            jax.ShapeDtypeStruct((T, D), F32),
            jax.ShapeDtypeStruct((T, ROW_TILES, LANES), ROW_DT),
            jax.ShapeDtypeStruct((8, T), jnp.int32),
            jax.ShapeDtypeStruct((8, T), F32),
        ],
        compiler_params=_cparams(("arbitrary",)),
        name="post_mix_router",
    )(*x_args, o_f, o_b, proj, hg, hsel, b_ctx, b_smp, c_out, w_out_bf, mod3, mod3, mod3, g2.reshape(1, D),
      wr_hi, wr_lo, br_pad)


def _route_kernel(idx_ref, dest_ref, meta_ref):
    erow = lax.broadcasted_iota(jnp.int32, (N_EXPERTS, TM), 0)
    s_i = lax.broadcasted_iota(jnp.int32, (TM, TM), 0)
    t_i = lax.broadcasted_iota(jnp.int32, (TM, TM), 1)
    earlier = (s_i < t_i).astype(BF16)
    out_row = lax.broadcasted_iota(jnp.int32, (8, TM), 0)

    def onehots(i):
        idx = idx_ref[:, pl.ds(pl.multiple_of(i * TM, TM), TM)]
        return [(erow == idx[kk:kk + 1, :]) for kk in range(TOP_K)]

    def count_tile(i, run):
        ohs = onehots(i)
        base = run
        pos = jnp.zeros((8, TM), F32)
        for kk in range(TOP_K):
            ohf = ohs[kk].astype(F32)
            before = jnp.dot(ohs[kk].astype(BF16), earlier, preferred_element_type=F32)
            p = jnp.sum(ohf * (base + before), axis=0, keepdims=True)
            pos = jnp.where(out_row == kk, p, pos)
            base = base + jnp.sum(ohf, axis=1, keepdims=True)
        dest_ref[:, pl.ds(pl.multiple_of(i * TM, TM), TM)] = pos.astype(jnp.int32)
        return base

    counts = lax.fori_loop(0, N_TILES, count_tile, jnp.zeros((N_EXPERTS, 1), F32)).astype(jnp.int32)
    bm_shift = MOE_BM.bit_length() - 1
    padded = lax.shift_left(lax.shift_right_logical(counts + (MOE_BM - 1), bm_shift), bm_shift)
    e_r = lax.broadcasted_iota(jnp.int32, (N_EXPERTS, N_EXPERTS), 0)
    e_c = lax.broadcasted_iota(jnp.int32, (N_EXPERTS, N_EXPERTS), 1)
    incl = (e_c <= e_r).astype(BF16)
    pad_end = _sel_dot(incl, jnp.broadcast_to(padded.astype(F32), (N_EXPERTS, LANES)))[:, :1]
    pad_start = pad_end - padded.astype(F32)

    def place_tile(i, carry):
        ohs = onehots(i)
        sl = pl.ds(pl.multiple_of(i * TM, TM), TM)
        off = jnp.zeros((8, TM), F32)
        for kk in range(TOP_K):
            o = jnp.sum(ohs[kk].astype(F32) * pad_start, axis=0, keepdims=True)
            off = jnp.where(out_row == kk, o, off)
        dest_ref[:, sl] = dest_ref[:, sl] + off.astype(jnp.int32)
        return carry

    lax.fori_loop(0, N_TILES, place_tile, 0)

    total = jnp.max(pad_end, axis=0, keepdims=True)
    lane_i = lax.broadcasted_iota(jnp.int32, (1, TM), 1)
    blk0 = (lane_i * MOE_BM).astype(F32)
    block_e = jnp.sum((pad_end <= blk0).astype(F32), axis=0, keepdims=True)
    live_end = pad_start + counts.astype(F32)
    sel = erow.astype(F32) == block_e
    live = jnp.sum(jnp.where(sel, live_end, 0.0), axis=0, keepdims=True)
    valid = jnp.where(blk0 < total, jnp.clip(live - blk0, 0.0, float(MOE_BM)), 0.0)
    own = erow == lane_i
    n_blk = jnp.sum(jnp.where(own, padded.astype(F32), 0.0), axis=0, keepdims=True) * (1.0 / MOE_BM)
    first_blk = jnp.sum(jnp.where(own, pad_start, 0.0), axis=0, keepdims=True) * (1.0 / MOE_BM)
    meta = jnp.where(out_row == 0, valid, 0.0)
    meta = jnp.where(out_row == 1, n_blk, meta)
    meta = jnp.where(out_row == 2, first_blk, meta)
    meta = jnp.where(out_row == 3, total * (1.0 / MOE_BM), meta)
    meta_ref[...] = meta.astype(jnp.int32)


def _route(idx_t):
    assert MOE_BLOCKS <= TM
    return pl.pallas_call(
        _route_kernel,
        out_shape=[jax.ShapeDtypeStruct((8, T), jnp.int32), jax.ShapeDtypeStruct((8, TM), jnp.int32)],
        compiler_params=pltpu.CompilerParams(vmem_limit_bytes=VMEM_LIMIT),
        name="moe_route",
    )(idx_t)


def _moe_kernel(bv_ref, nb_ref, g0_ref, tot_ref, x_hbm, wgu_ref, bgu_ref, wdn_ref, bdn_ref, y_hbm,
                wgu_bf, wdn_bf, xbuf, ybuf, xb_ref, xsem, ysem):
    e = pl.program_id(0)
    n_blk = nb_ref[e]
    first = g0_ref[e]
    total = tot_ref[0]
    ahead = MOE_RING - 1

    def x_copy(g):
        slot = g % MOE_RING
        return pltpu.make_async_copy(x_hbm.at[pl.ds(g * MOE_BM, MOE_BM)], xbuf.at[slot], xsem.at[slot])

    def y_copy(g):
        slot = g % MOE_RING
        return pltpu.make_async_copy(ybuf.at[slot], y_hbm.at[pl.ds(g * MOE_BM, MOE_BM)], ysem.at[slot])

    @pl.when(e == 0)
    def _():
        for g in range(ahead):
            @pl.when(g < total)
            def _():
                x_copy(g).start()

    @pl.when(n_blk > 0)
    def _():
        wgu_bf[...] = wgu_ref[0, 0].astype(BF16)
        wdn_bf[...] = wdn_ref[0, 0].astype(BF16)

    def block(j, carry):
        g = first + j
        slot = g % MOE_RING
        x_copy(g).wait()

        @pl.when(g + ahead < total)
        def _():
            x_copy(g + ahead).start()

        @pl.when(g >= MOE_RING)
        def _():
            y_copy(g - MOE_RING).wait()

        n_live = bv_ref[g]

        def ffn(n_rows):
            live = lax.broadcasted_iota(jnp.int32, (n_rows, LANES), 0) < n_live
            for c, chunk in enumerate(_unpack_rows(xbuf[slot, :n_rows])):
                xb_ref[:n_rows, c * LANES:(c + 1) * LANES] = jnp.where(live, chunk, 0.0).astype(BF16)
            gu = jnp.dot(xb_ref[:n_rows, :], wgu_bf[...], preferred_element_type=F32) + bgu_ref[0, 0]
            glu = jnp.minimum(gu[:, :D], SWIGLU_LIMIT)
            lin = jnp.clip(gu[:, D:], -SWIGLU_LIMIT, SWIGLU_LIMIT)
            act = glu * jax.nn.sigmoid(SWIGLU_ALPHA * glu) * (lin + 1.0)
            y = jnp.dot(act.astype(BF16), wdn_bf[...], preferred_element_type=F32) + bdn_ref[0, 0]
            ybuf[slot, :n_rows] = _pack_rows(y)

        @pl.when(n_live > MOE_BM // 2)
        def _():
            ffn(MOE_BM)

        @pl.when(n_live <= MOE_BM // 2)
        def _():
            ffn(MOE_BM // 2)
            ybuf[slot, MOE_BM // 2:] = jnp.zeros((MOE_BM // 2, ROW_TILES, LANES), ROW_DT)

        y_copy(g).start()
        return carry

    lax.fori_loop(0, n_blk, block, 0)

    @pl.when(e == N_EXPERTS - 1)
    def _():
        for back in range(MOE_RING, 0, -1):
            @pl.when(total >= back)
            def _():
                y_copy(total - back).wait()

        def fill(g, carry):
            ybuf[g % MOE_RING] = jnp.zeros((MOE_BM, ROW_TILES, LANES), ROW_DT)
            y_copy(g).start()
            y_copy(g).wait()
            return carry

        lax.fori_loop(total, MOE_BLOCKS, fill, 0)


def _moe_ffn(block_valid, n_blk, first_blk, total_blk, xs, w_gu, b_gu, w_dn, b_dn, layer):
    rows = (MOE_BM, ROW_TILES, LANES)
    return pl.pallas_call(
        _moe_kernel,
        grid_spec=pltpu.PrefetchScalarGridSpec(
            num_scalar_prefetch=4,
            grid=(N_EXPERTS,),
            in_specs=[
                pl.BlockSpec(memory_space=pl.ANY),
                pl.BlockSpec((1, 1, D, 2 * D), lambda e, *_: (layer, e, 0, 0)),
                pl.BlockSpec((1, 1, 1, 2 * D), lambda e, *_: (layer, e, 0, 0)),
                pl.BlockSpec((1, 1, D, D), lambda e, *_: (layer, e, 0, 0)),
                pl.BlockSpec((1, 1, 1, D), lambda e, *_: (layer, e, 0, 0)),
            ],
            out_specs=pl.BlockSpec(memory_space=pl.ANY),
            scratch_shapes=[
                pltpu.VMEM((D, 2 * D), BF16), pltpu.VMEM((D, D), BF16),
                pltpu.VMEM((MOE_RING,) + rows, ROW_DT), pltpu.VMEM((MOE_RING,) + rows, ROW_DT),
                pltpu.VMEM((MOE_BM, D), BF16),
                pltpu.SemaphoreType.DMA((MOE_RING,)), pltpu.SemaphoreType.DMA((MOE_RING,)),
            ],
        ),
        out_shape=jax.ShapeDtypeStruct((MOE_ROWS, ROW_TILES, LANES), ROW_DT),
        compiler_params=_cparams(("arbitrary",)),
        name="moe_expert_ffn",
    )(block_valid, n_blk, first_blk, total_blk, xs, w_gu, b_gu.reshape(DEPTH, N_EXPERTS, 1, 2 * D), w_dn,
      b_dn.reshape(DEPTH, N_EXPERTS, 1, D))


def _combine_kernel(x1_ref, y_ref, gw_ref, gate2_ref, o_ref):
    gw = jnp.concatenate([gw_ref[...], jnp.zeros((LANES - 8, TM), F32)], axis=0).T
    ys = [_unpack_rows(y_ref[kk]) for kk in range(TOP_K)]
    for c in range(D // LANES):
        cs = slice(c * LANES, (c + 1) * LANES)
        acc = None
        for kk in range(TOP_K):
            term = ys[kk][c] * gw[:, kk:kk + 1]
            acc = term if acc is None else acc + term
        o_ref[:, cs] = x1_ref[:, cs] + gate2_ref[0, :, cs] * acc


def _combine(x1, yg, gw, mod3, tile0, n_tiles):
    return pl.pallas_call(
        _combine_kernel,
        grid=(n_tiles,),
        in_specs=[
            pl.BlockSpec((TM, D), lambda i: (tile0 + i, 0)),
            pl.BlockSpec((TOP_K, TM, ROW_TILES, LANES), lambda i: (0, tile0 + i, 0, 0)),
            pl.BlockSpec((8, TM), lambda i: (0, tile0 + i)),
            pl.BlockSpec((1, 1, D), lambda i: (_mod_row(tile0 + i), 0, 5)),
        ],
        out_specs=pl.BlockSpec((TM, D), lambda i: (i, 0)),
        out_shape=jax.ShapeDtypeStruct((n_tiles * TM, D), F32),
        compiler_params=_cparams(("arbitrary",)),
        name="moe_combine",
    )(x1, yg, gw, mod3)


def _sc_mesh():
    return plsc.VectorSubcoreMesh(core_axis_name="c", subcore_axis_name="s")


def _sc_worker():
    return lax.axis_index("s") * SC_CORES + lax.axis_index("c")


def _sc_dispatch(h2t, dest_km):
    per_w = T // SC_WORKERS

    @functools.partial(
        pl.kernel, mesh=_sc_mesh(),
        out_type=jax.ShapeDtypeStruct((MOE_ROWS, ROW_TILES, LANES), ROW_DT),
        scratch_types=[pltpu.VMEM((SC_WIN,), jnp.int32), pltpu.VMEM((SC_WIN, ROW_TILES, LANES), ROW_DT),
                       pltpu.SemaphoreType.DMA],
    )
    def run(h_hbm, d_hbm, o_hbm, idx_v, rows_v, sem):
        w0 = _sc_worker() * per_w

        @pl.loop(0, per_w // SC_WIN)
        def _(w):
            base = pl.multiple_of(w0 + w * SC_WIN, SC_WIN)
            pltpu.sync_copy(h_hbm.at[pl.ds(base, SC_WIN)], rows_v)
            for kk in range(TOP_K):
                pltpu.sync_copy(d_hbm.at[pl.ds(kk * T + base, SC_WIN)], idx_v)
                pltpu.async_copy(rows_v, o_hbm.at[idx_v], sem).wait()

    return run(h2t, dest_km)


def _sc_gather(yb, dest_km):
    n = TOP_K * T
    per_w = n // SC_WORKERS

    @functools.partial(
        pl.kernel, mesh=_sc_mesh(),
        out_type=jax.ShapeDtypeStruct((n, ROW_TILES, LANES), ROW_DT),
        scratch_types=[pltpu.VMEM((SC_WIN,), jnp.int32), pltpu.VMEM((SC_WIN, ROW_TILES, LANES), ROW_DT),
                       pltpu.SemaphoreType.DMA],
    )
    def run(y_hbm, d_hbm, o_hbm, idx_v, rows_v, sem):
        w0 = _sc_worker() * per_w

        @pl.loop(0, per_w // SC_WIN)
        def _(w):
            base = pl.multiple_of(w0 + w * SC_WIN, SC_WIN)
            pltpu.sync_copy(d_hbm.at[pl.ds(base, SC_WIN)], idx_v)
            pltpu.async_copy(y_hbm.at[idx_v], rows_v, sem).wait()
            pltpu.sync_copy(rows_v, o_hbm.at[pl.ds(base, SC_WIN)])

    return run(yb, dest_km)


def kernel(x_prompt, x_sample, c, cache_diff_k, cache_diff_v, state_hgrn, c_ctx, norm_mix_g, norm_ffn_g, w_mod, b_mod, w_in, w_out, hgrn_lower_bounds, hgrn_norm_g, diff_q_norm_g, diff_k_norm_g, diff_lambda_q1, diff_lambda_k1, diff_lambda_q2, diff_lambda_k2, diff_subln_g, cmlp_ln_g, cmlp_ln_b, cmlp_w_s, cmlp_b_s, router_w, router_b, moe_w_gate_up, moe_b_gate_up, moe_w_down, moe_b_down):
    x = (x_prompt.reshape(T_CTX, D), x_sample.reshape(T_SMP, D))
    cvec = jnp.concatenate([c_ctx[None, :], c, jnp.zeros((MOD_ROWS - 1 - DEC_BATCH, D), F32)], axis=0)
    mod = _modulation(cvec, w_mod, b_mod)

    lvl_np, tri_np = _hgrn_tables()
    lvl = jnp.asarray(lvl_np)
    tri = jnp.asarray(tri_np, dtype=BF16)
    cos, sin = _rope_tables()
    hsel = jnp.asarray(np.kron(np.eye(A_HEADS), np.ones((A_DK, A_DK))), dtype=BF16)
    sm = jax.nn.softmax(hgrn_lower_bounds.astype(F32), axis=0)
    lb_all = jnp.cumsum(sm, axis=0) - sm[0]

    new_k, new_v, new_s = None, None, []
    for l in range(DEPTH):
        mod3 = mod[l].reshape(MOD_ROWS, 1, 6 * D)
        proj = _in_projection(x, norm_mix_g[l], mod3, w_in[l].astype(BF16))

        s0 = jnp.concatenate([jnp.zeros((BATCH, 2, A_HEADS, A_DK, A_DK), F32), state_hgrn[:, l]], axis=0)
        o_dir, fin_dir = [], []
        for d in range(2):
            o_d, fin_d = _hgrn_scan(proj, lb_all[l, d].reshape(1, A_WIDTH), _pack_state(s0[:, d]),
                                    lvl, tri, l, d == 1)
            o_dir.append(o_d)
            fin_dir.append(_unpack_state(fin_d[:BATCH]))
        new_s.append(jnp.stack(fin_dir, axis=1))

        lam_init = 0.8 - 0.6 * math.exp(-0.3 * l)
        lam = (jnp.exp(jnp.sum(diff_lambda_q1[l] * diff_lambda_k1[l]))
               - jnp.exp(jnp.sum(diff_lambda_q2[l] * diff_lambda_k2[l])) + lam_init).reshape(1, 1)
        gq2 = jnp.tile(diff_q_norm_g[l], 2).reshape(1, LANES)
        gk2 = jnp.tile(diff_k_norm_g[l], 2).reshape(1, LANES)
        gs = diff_subln_g[l].reshape(1, LANES)
        b_ctx, new_k, new_v = _attn_ctx(proj, lam, gq2, gk2, gs, lam_init, l, new_k, new_v)
        b_smp = _attn_smp(proj, lam, cache_diff_k, cache_diff_v, cos, sin, gq2, gk2, gs, l, lam_init)

        bias_full = jnp.repeat(cmlp_b_s[l].T, C_DG, axis=1)
        c_out = _chunk_mlp(proj, cmlp_ln_g[l], cmlp_ln_b[l], cmlp_w_s[l], bias_full)

        hg = jnp.tile(hgrn_norm_g[l], A_HEADS).reshape(1, A_WIDTH)
        wr_pad = jnp.pad(router_w[l], ((0, 0), (0, LANES - N_EXPERTS)))
        wr_hi = wr_pad.astype(BF16)
        wr_lo = (wr_pad - wr_hi.astype(F32)).astype(BF16)
        br_pad = jnp.pad(router_b[l], (0, LANES - N_EXPERTS)).reshape(1, LANES)
        x1, h2, idx_t, gw_t = _post_mix(o_dir[0], o_dir[1], proj, hg, hsel, b_ctx, b_smp, c_out,
                                        w_out[l].astype(BF16), x, mod3, norm_ffn_g[l], wr_hi, wr_lo, br_pad)

        dest_t, meta = _route(idx_t)
        dest_km = dest_t[:TOP_K].reshape(-1)
        xs = _sc_dispatch(h2, dest_km)
        yb = _moe_ffn(meta[0, :MOE_BLOCKS], meta[1, :N_EXPERTS], meta[2, :N_EXPERTS], meta[3, :1], xs,
                      moe_w_gate_up, moe_b_gate_up, moe_w_down, moe_b_down, l)
        yg = _sc_gather(yb, dest_km).reshape(TOP_K, T, ROW_TILES, LANES)
        x = (_combine(x1, yg, gw_t, mod3, 0, CTX_TILES),
             _combine(x1, yg, gw_t, mod3, CTX_TILES, N_TILES - CTX_TILES))

    y_prompt = x[0].reshape(BATCH, SEQ, D)
    y_sample = x[1].reshape(DEC_BATCH, DEC_SEQ, D)
    return (y_prompt, y_sample, new_k, new_v, jnp.stack(new_s, axis=1))
```

```python
import functools
import math

import numpy as np
import jax
import jax.numpy as jnp
from jax import lax
from jax.experimental import pallas as pl
from jax.experimental.pallas import tpu as pltpu
from jax.experimental.pallas import tpu_sc as plsc

F32 = jnp.float32
BF16 = jnp.bfloat16

D = 1024
DEPTH = 2
BATCH, SEQ = 16, 256
DEC_BATCH, DEC_SEQ = 8, 1024
PAST = 512
GRID_W = 64
A_HEADS, A_DK = 4, 64
A_WIDTH = 256
B_HEADS, B_DK, B_DV = 4, 64, 128
B_WIDTH = 512
C_GROUPS, C_CHUNK, C_WIDTH, C_DG = 4, 128, 256, 64
IN_WIDTH = 5 * A_WIDTH + 3 * B_WIDTH + 2 * C_WIDTH
N_EXPERTS, TOP_K = 32, 4
SWIGLU_LIMIT, SWIGLU_ALPHA = 7.0, 1.702
ROPE_BASE = 10000.0
EPS = 1e-6

T_CTX = BATCH * SEQ
T_SMP = DEC_BATCH * DEC_SEQ
T = T_CTX + T_SMP
N_SEQ = BATCH + DEC_BATCH
MOD_ROWS = 16

TM = 256
N_TILES = T // TM
CTX_TILES = T_CTX // TM
SMP_TILES_PER_SEQ = DEC_SEQ // TM
LANES = 128
MOE_BM = 256
MOE_ROWS = T * TOP_K + N_EXPERTS * MOE_BM
MOE_BLOCKS = MOE_ROWS // MOE_BM
MOE_RING = 4
ROW_WORDS = D // 2
ROW_TILES = ROW_WORDS // LANES
ROW_DT = jnp.int32
SC_CORES, SC_SUBCORES = 2, 16
SC_WORKERS = SC_CORES * SC_SUBCORES
SC_WIN = 128
VMEM_LIMIT = 56 * 1024 * 1024


def _cparams(sem):
    return pltpu.CompilerParams(dimension_semantics=sem, vmem_limit_bytes=VMEM_LIMIT)


def _mod_row(i, rows=TM):
    ctx_tiles = T_CTX // rows
    return jnp.where(i < ctx_tiles, 0, 1 + (i - ctx_tiles) // (DEC_SEQ // rows))


def _split3(x):
    hi = x.astype(BF16)
    r = x - hi.astype(F32)
    mid = r.astype(BF16)
    lo = (r - mid.astype(F32)).astype(BF16)
    return hi, mid, lo


def _sel_dot(sel, x):
    hi, mid, lo = _split3(x)
    acc = jnp.dot(sel, lo, preferred_element_type=F32)
    acc = acc + jnp.dot(sel, mid, preferred_element_type=F32)
    return acc + jnp.dot(sel, hi, preferred_element_type=F32)


def _dot_sel(x, sel):
    hi, mid, lo = _split3(x)
    acc = jnp.dot(lo, sel, preferred_element_type=F32)
    acc = acc + jnp.dot(mid, sel, preferred_element_type=F32)
    return acc + jnp.dot(hi, sel, preferred_element_type=F32)


def _dot_nt(a, b):
    return lax.dot_general(a, b, (((1,), (1,)), ((), ())), preferred_element_type=F32)


def _dot_tn(a, b):
    return lax.dot_general(a, b, (((0,), (0,)), ((), ())), preferred_element_type=F32)


def _lane(shape):
    return lax.broadcasted_iota(jnp.int32, shape, len(shape) - 1)


def _pack_rows(x):
    hi = lax.bitcast_convert_type(x[:, :ROW_WORDS].astype(BF16).astype(F32), jnp.int32)
    lo = lax.bitcast_convert_type(x[:, ROW_WORDS:].astype(BF16).astype(F32), jnp.int32)
    words = hi | lax.shift_right_logical(lo, 16)
    return pltpu.einshape("t(jl)->tjl", words, l=LANES)


def _unpack_rows(words3):
    wt = pltpu.einshape("tjl->jtl", words3)
    hi = [lax.bitcast_convert_type(wt[j] & jnp.int32(-65536), F32) for j in range(ROW_TILES)]
    lo = [lax.bitcast_convert_type(lax.shift_left(wt[j], 16), F32) for j in range(ROW_TILES)]
    return hi + lo


def _mod_kernel(c_ref, w_ref, b_ref, o_ref):
    c = c_ref[...]
    s = c * jax.nn.sigmoid(c)
    o_ref[0] = jnp.dot(s.astype(BF16), w_ref[0].astype(BF16), preferred_element_type=F32) + b_ref[0]


def _modulation(cvec, w_mod, b_mod):
    tn = 1536
    return pl.pallas_call(
        _mod_kernel,
        grid=(DEPTH, 6 * D // tn),
        in_specs=[
            pl.BlockSpec((MOD_ROWS, D), lambda l, j: (0, 0)),
            pl.BlockSpec((1, D, tn), lambda l, j: (l, 0, j)),
            pl.BlockSpec((1, 1, tn), lambda l, j: (l, 0, j)),
        ],
        out_specs=pl.BlockSpec((1, MOD_ROWS, tn), lambda l, j: (l, 0, j)),
        out_shape=jax.ShapeDtypeStruct((DEPTH, MOD_ROWS, 6 * D), F32),
        compiler_params=_cparams(("arbitrary", "arbitrary")),
        name="modulation",
    )(cvec, w_mod, b_mod.reshape(DEPTH, 1, 6 * D))


def _stream_specs(x, rows=TM):
    ctx_tiles = T_CTX // rows
    if isinstance(x, tuple):
        return [pl.BlockSpec((rows, D), lambda i: (jnp.minimum(i, ctx_tiles - 1), 0)),
                pl.BlockSpec((rows, D), lambda i: (jnp.maximum(i - ctx_tiles, 0), 0))], list(x)
    return [pl.BlockSpec((rows, D), lambda i: (i, 0))], [x]


IN_ROWS = 512


def _inproj_kernel(*refs, n_x):
    g_ref, shift_ref, scale_ref, w_ref, o_ref = refs[n_x:]
    is_ctx = pl.program_id(0) < T_CTX // IN_ROWS
    for s in range(IN_ROWS // TM):
        rows = slice(s * TM, (s + 1) * TM)
        x = refs[0][rows, :] if n_x == 1 else jnp.where(is_ctx, refs[0][rows, :], refs[1][rows, :])
        y = x * lax.rsqrt(jnp.mean(x * x, axis=-1, keepdims=True) + EPS) * g_ref[...]
        h = y * (1.0 + scale_ref[0]) + shift_ref[0]
        o_ref[rows, :] = jnp.dot(h.astype(BF16), w_ref[...], preferred_element_type=F32)


def _in_projection(x, g, mod3, w_in_bf):
    x_specs, x_args = _stream_specs(x, IN_ROWS)
    mod_row = functools.partial(_mod_row, rows=IN_ROWS)
    return pl.pallas_call(
        functools.partial(_inproj_kernel, n_x=len(x_args)),
        grid=(T // IN_ROWS,),
        in_specs=x_specs + [
            pl.BlockSpec((1, D), lambda i: (0, 0)),
            pl.BlockSpec((1, 1, D), lambda i: (mod_row(i), 0, 0)),
            pl.BlockSpec((1, 1, D), lambda i: (mod_row(i), 0, 1)),
            pl.BlockSpec((D, IN_WIDTH), lambda i: (0, 0)),
        ],
        out_specs=pl.BlockSpec((IN_ROWS, IN_WIDTH), lambda i: (i, 0)),
        out_shape=jax.ShapeDtypeStruct((T, IN_WIDTH), F32),
        compiler_params=_cparams(("arbitrary",)),
        name="in_projection",
    )(*x_args, g.reshape(1, D), mod3, mod3, w_in_bf)


HG_C = 128
HG_LEVELS = tuple(2 ** j for j in range(1, int(math.log2(HG_C)) + 1))


def _hgrn_tables():
    t = np.arange(HG_C)[:, None]
    s = np.arange(HG_C)[None, :]
    x = t ^ s
    lvl = np.zeros((HG_C, HG_C), np.int32)
    nz = x > 0
    lvl[nz] = np.floor(np.log2(x[nz])).astype(np.int32) + 1
    fwd = np.where(t >= s, lvl, -1).astype(np.int32)
    bwd = np.where(t <= s, lvl, -1).astype(np.int32)
    tri_f = (t >= s).astype(np.float32)
    tri_b = (t <= s).astype(np.float32)
    return np.stack([fwd, bwd]), np.stack([tri_f, tri_b])


def _block_ref(cum, m, idx):
    c, l = cum.shape
    if m >= 16:
        c3 = cum.reshape(c // m, m, l)
        r = c3[:, idx:idx + 1, :]
        return jnp.broadcast_to(r, (c // m, m, l)).reshape(c, l)
    c3 = cum.reshape(c // 8, 8, l)
    sub = lax.broadcasted_iota(jnp.int32, c3.shape, 1)
    out = None
    for j in range(8 // m - 1, -1, -1):
        cand = jnp.broadcast_to(c3[:, j * m + idx:j * m + idx + 1, :], c3.shape)
        out = cand if out is None else jnp.where(sub < (j + 1) * m, cand, out)
    return out.reshape(c, l)


def _hgrn_kernel(*refs, layer):
    n_in = 7
    fwd_in, bwd_in = refs[:n_in], refs[n_in:2 * n_in]
    o_f, fin_f, o_b, fin_b, st_f, st_b = refs[2 * n_in:]
    _hgrn_direction(*fwd_in, o_f, fin_f, st_f, layer=layer, rev=False)
    _hgrn_direction(*bwd_in, o_b, fin_b, st_b, layer=layer, rev=True)


def _hgrn_direction(q_ref, z_ref, v_ref, lb_ref, s0_ref, lvl_ref, tri_ref, o_ref, fin_ref, st_ref, *, layer, rev):
    g = pl.program_id(0)
    first = jnp.logical_or(g < CTX_TILES, (g - CTX_TILES) % SMP_TILES_PER_SEQ == 0)

    @pl.when(first)
    def _():
        st_ref[...] = s0_ref[0]

    qr = q_ref[...]
    q = qr * jax.nn.sigmoid(qr) * (A_DK ** -0.5)
    z = z_ref[...]
    if layer == 0:
        lf = jnp.minimum(z, 0.0) - jnp.log(1.0 + jnp.exp(-jnp.abs(z)))
        k = jax.nn.sigmoid(-z)
    else:
        lbd = lb_ref[...]
        lf = jnp.log(lbd + (1.0 - lbd) * jax.nn.sigmoid(z))
        k = (1.0 - lbd) * jax.nn.sigmoid(-z)
    v = v_ref[...]
    tri = tri_ref[0]
    lvl = lvl_ref[0]
    last_row = 0 if rev else HG_C - 1
    n_chunks = TM // HG_C
    order = range(n_chunks - 1, -1, -1) if rev else range(n_chunks)
    lf2 = lf * math.log2(math.e)
    cums = [_sel_dot(tri, lf2[c * HG_C:(c + 1) * HG_C]) for c in range(n_chunks)]
    lane = _lane((HG_C, LANES))
    head_masks = (lane < A_DK, lane >= A_DK)
    lane_row = _lane((1, LANES))
    head_keep = ((lane_row < A_DK).astype(BF16), (lane_row >= A_DK).astype(BF16))
    lvl2 = jnp.concatenate([lvl, lvl], axis=0)
    level_masks = [lvl2 == i for i in range(len(HG_LEVELS) + 1)]
    r = lax.broadcasted_iota(jnp.int32, (LANES, LANES), 0)
    cl = lax.broadcasted_iota(jnp.int32, (LANES, LANES), 1)
    same_head = (r < A_DK) == (cl < A_DK)

    def chunk(q_p, k_p, v_p, cum_p, st):
        v_bf = v_p.astype(BF16)
        k_bf = k_p.astype(BF16)
        q_bf = q_p.astype(BF16)

        def both_heads(x_bf):
            return jnp.concatenate([x_bf * head_keep[0], x_bf * head_keep[1]], axis=0)

        scores = jnp.where(level_masks[0], _dot_nt(both_heads(q_bf), k_bf), 0.0)
        for li, m in enumerate(HG_LEVELS):
            ref = _block_ref(cum_p, m, m // 2 if rev else m // 2 - 1)
            dec = jnp.exp2(-jnp.abs(cum_p - ref))
            qd = (q_p * dec).astype(BF16)
            kd = (k_p * dec).astype(BF16)
            scores = jnp.where(level_masks[li + 1], _dot_nt(both_heads(qd), kd), scores)
        pv = jnp.dot(scores.astype(BF16), v_bf, preferred_element_type=F32)
        o_intra = jnp.where(head_masks[0], pv[:HG_C], pv[HG_C:])
        q0 = (q_p * jnp.exp2(cum_p)).astype(BF16)
        out = o_intra + _dot_nt(q0, st.astype(BF16))
        last = cum_p[last_row:last_row + 1, :]
        ks = (k_p * jnp.exp2(last - cum_p)).astype(BF16)
        upd = _dot_tn(v_bf, ks)
        return out, st * jnp.exp2(last) + jnp.where(same_head, upd, 0.0)

    for p in range(2):
        sl = slice(p * LANES, (p + 1) * LANES)
        st = st_ref[p]
        for c in order:
            rows = slice(c * HG_C, (c + 1) * HG_C)
            o_ref[rows, sl], st = chunk(q[rows, sl], k[rows, sl], v[rows, sl], cums[c][:, sl], st)
        st_ref[p] = st
        fin_ref[0, p] = st


def _hgrn_seq(g):
    return jnp.where(g < CTX_TILES, g, CTX_TILES + (g - CTX_TILES) // SMP_TILES_PER_SEQ)


def _hgrn_blk(g, rev):
    if not rev:
        return g
    j = g - CTX_TILES
    return jnp.where(g < CTX_TILES, g,
                     CTX_TILES + (j // SMP_TILES_PER_SEQ) * SMP_TILES_PER_SEQ
                     + (SMP_TILES_PER_SEQ - 1 - j % SMP_TILES_PER_SEQ))


def _hgrn_scan(proj, lb_dirs, s0_dirs, lvl, tri, layer):
    def dir_in_specs(d):
        blk = functools.partial(_hgrn_blk, rev=d == 1)
        return [
            pl.BlockSpec((TM, A_WIDTH), lambda g: (blk(g), 0)),
            pl.BlockSpec((TM, A_WIDTH), lambda g: (blk(g), 1 + d)),
            pl.BlockSpec((TM, A_WIDTH), lambda g: (blk(g), 3)),
            pl.BlockSpec((1, A_WIDTH), lambda g: (0, 0)),
            pl.BlockSpec((1, 2, LANES, LANES), lambda g: (_hgrn_seq(g), 0, 0, 0)),
            pl.BlockSpec((1, HG_C, HG_C), lambda g: (d, 0, 0)),
            pl.BlockSpec((1, HG_C, HG_C), lambda g: (d, 0, 0)),
        ]

    def dir_out_specs(d):
        blk = functools.partial(_hgrn_blk, rev=d == 1)
        return [pl.BlockSpec((TM, A_WIDTH), lambda g: (blk(g), 0)),
                pl.BlockSpec((1, 2, LANES, LANES), lambda g: (_hgrn_seq(g), 0, 0, 0))]

    dir_out_shape = [jax.ShapeDtypeStruct((T, A_WIDTH), F32), jax.ShapeDtypeStruct((N_SEQ, 2, LANES, LANES), F32)]
    dir_args = lambda d: (proj, proj, proj, lb_dirs[d], s0_dirs[d], lvl, tri)
    return pl.pallas_call(
        functools.partial(_hgrn_kernel, layer=layer),
        grid=(N_TILES,),
        in_specs=dir_in_specs(0) + dir_in_specs(1),
        out_specs=dir_out_specs(0) + dir_out_specs(1),
        out_shape=dir_out_shape + dir_out_shape,
        scratch_shapes=[pltpu.VMEM((2, LANES, LANES), F32), pltpu.VMEM((2, LANES, LANES), F32)],
        compiler_params=_cparams(("arbitrary",)),
        name="hgrn_scan",
    )(*dir_args(0), *dir_args(1))


def _pack_state(s):
    n = s.shape[0]
    st = jnp.swapaxes(s, -1, -2).reshape(n, 2, 2, A_DK, A_DK)
    z = jnp.zeros_like(st[:, :, 0])
    top = jnp.concatenate([st[:, :, 0], z], axis=-1)
    bot = jnp.concatenate([z, st[:, :, 1]], axis=-1)
    return jnp.concatenate([top, bot], axis=-2)


def _unpack_state(sp):
    n = sp.shape[0]
    h0 = sp[:, :, :A_DK, :A_DK]
    h1 = sp[:, :, A_DK:, A_DK:]
    st = jnp.stack([h0, h1], axis=2).reshape(n, A_HEADS, A_DK, A_DK)
    return jnp.swapaxes(st, -1, -2)


def _half_rms(x, g):
    r = lax.broadcasted_iota(jnp.int32, (LANES, LANES), 0)
    c = lax.broadcasted_iota(jnp.int32, (LANES, LANES), 1)
    half_mean = jnp.where((r < B_DK) == (c < B_DK), 1.0 / B_DK, 0.0).astype(BF16)
    xx = x * x
    hi = xx.astype(BF16)
    lo = (xx - hi.astype(F32)).astype(BF16)
    ms = jnp.dot(lo, half_mean, preferred_element_type=F32) + jnp.dot(hi, half_mean, preferred_element_type=F32)
    return x * lax.rsqrt(ms + EPS) * g


def _rope(x, cos, sin_signed):
    lane = _lane(x.shape)
    first = (lane % 32) < 16
    rot = jnp.where(first, pltpu.roll(x, LANES - 16, 1), pltpu.roll(x, 16, 1))
    return x * cos + rot * sin_signed


def _with_ones(v_bf):
    return jnp.concatenate([v_bf, jnp.ones_like(v_bf)], axis=-1)


def _diff_softmax_pv(q_bf, keys_bf, vals_ext, lam):
    lane = _lane(q_bf.shape)
    zero = jnp.zeros_like(q_bf)
    outs = []
    for mp in range(2):
        qm = jnp.where((lane < B_DK) == (mp == 0), q_bf, zero)
        s = [_dot_nt(qm, kk) for kk in keys_bf]
        mx = functools.reduce(jnp.maximum, [jnp.max(si, axis=-1, keepdims=True) for si in s])
        acc = None
        for si, ve in zip(s, vals_ext):
            e = jnp.exp((si - mx).astype(BF16))
            pv = jnp.dot(e, ve, preferred_element_type=F32)
            acc = pv if acc is None else acc + pv
        outs.append(acc[:, :B_DV] / acc[:, B_DV:])
    return outs[0] - lam * outs[1]


def _subln(o, g, lam_init):
    return o * lax.rsqrt(jnp.mean(o * o, axis=-1, keepdims=True) + EPS) * g * (1.0 - lam_init)


def _attn_ctx_kernel(lam_ref, *refs, lam_init, layer):
    q_refs, k_refs, v_refs = refs[:B_HEADS], refs[B_HEADS:2 * B_HEADS], refs[2 * B_HEADS:3 * B_HEADS]
    gq_ref, gk_ref, gs_ref = refs[3 * B_HEADS:3 * B_HEADS + 3]
    rest = refs[3 * B_HEADS + 3:]
    if layer:
        pk_ref, pv_ref, o_ref, nk_ref, nv_ref = rest
        nk_ref[0, :layer] = pk_ref[0]
        nv_ref[0, :layer] = pv_ref[0]
    else:
        o_ref, nk_ref, nv_ref = rest
    lam = lam_ref[0, 0]
    for h in range(B_HEADS):
        qn = _half_rms(q_refs[h][...], gq_ref[...]) * (B_DK ** -0.5)
        kn = _half_rms(k_refs[h][...], gk_ref[...])
        v = v_refs[h][...]
        nk_ref[0, layer, 0, h] = kn[:, :B_DK]
        nk_ref[0, layer, 1, h] = kn[:, B_DK:]
        nv_ref[0, layer, h] = v
        o = _diff_softmax_pv(qn.astype(BF16), [kn.astype(BF16)], [_with_ones(v.astype(BF16))], lam)
        o_ref[:, h * LANES:(h + 1) * LANES] = _subln(o, gs_ref[...], lam_init)


def _attn_ctx(proj, lam, gq2, gk2, gs, lam_init, layer, prev_k, prev_v):
    qcol, kcol, vcol = 5 * A_WIDTH // LANES, 5 * A_WIDTH // LANES + 4, 5 * A_WIDTH // LANES + 8
    prev_specs, prev_args = [], []
    if layer:
        prev_specs = [pl.BlockSpec((1, layer, 2, B_HEADS, SEQ, B_DK), lambda b: (b, 0, 0, 0, 0, 0)),
                      pl.BlockSpec((1, layer, B_HEADS, SEQ, B_DV), lambda b: (b, 0, 0, 0, 0))]
        prev_args = [prev_k, prev_v]
    n_l = layer + 1
    head_specs = [pl.BlockSpec((SEQ, LANES), functools.partial(lambda b, col: (b, col), col=c0 + h))
                  for c0 in (qcol, kcol, vcol) for h in range(B_HEADS)]
    return pl.pallas_call(
        functools.partial(_attn_ctx_kernel, lam_init=lam_init, layer=layer),
        grid=(BATCH,),
        in_specs=[pl.BlockSpec(memory_space=pltpu.SMEM)] + head_specs + [
            pl.BlockSpec((1, LANES), lambda b: (0, 0)),
            pl.BlockSpec((1, LANES), lambda b: (0, 0)),
            pl.BlockSpec((1, LANES), lambda b: (0, 0)),
        ] + prev_specs,
        out_specs=[
            pl.BlockSpec((SEQ, B_WIDTH), lambda b: (b, 0)),
            pl.BlockSpec((1, n_l, 2, B_HEADS, SEQ, B_DK), lambda b: (b, 0, 0, 0, 0, 0)),
            pl.BlockSpec((1, n_l, B_HEADS, SEQ, B_DV), lambda b: (b, 0, 0, 0, 0)),
        ],
        out_shape=[
            jax.ShapeDtypeStruct((T_CTX, B_WIDTH), F32),
            jax.ShapeDtypeStruct((BATCH, n_l, 2, B_HEADS, SEQ, B_DK), F32),
            jax.ShapeDtypeStruct((BATCH, n_l, B_HEADS, SEQ, B_DV), F32),
        ],
        compiler_params=_cparams(("arbitrary",)),
        name="diff_attention_ctx",
    )(lam, *([proj] * (3 * B_HEADS)), gq2, gk2, gs, *prev_args)


ATT_TQ = 256
ATT_HEADS = 4


def _attn_smp_kernel(lam_ref, *refs, lam_init):
    nh = ATT_HEADS
    q_refs, k_refs, v_refs = refs[:nh], refs[nh:2 * nh], refs[2 * nh:3 * nh]
    ck_refs, cv_refs = refs[3 * nh:4 * nh], refs[4 * nh:5 * nh]
    cos_ref, sin_ref, gq_ref, gk_ref, gs_ref, o_ref, qs_ref, ks_ref = refs[5 * nh:]
    lam = lam_ref[0, 0]
    cos = cos_ref[...]
    sin = sin_ref[...]
    g = gs_ref[...]
    prepared = []
    for h in range(nh):
        qn = _rope(_half_rms(q_refs[h][...], gq_ref[...]), cos, sin) * (B_DK ** -0.5)
        qs_ref[h] = qn.astype(BF16)
        ks_ref[h] = _rope(_half_rms(k_refs[h][...], gk_ref[...]), cos, sin).astype(BF16)
        ck = jnp.concatenate([ck_refs[h][0, 0, 0, 0], ck_refs[h][0, 0, 1, 0]], axis=-1).astype(BF16)
        cv = _with_ones(cv_refs[h][0, 0, 0].astype(BF16))
        v_bf = _with_ones(v_refs[h][...].astype(BF16))
        prepared.append((ck, cv, v_bf))
    for h in range(nh):
        ck, cv, v_bf = prepared[h]
        k_bf = ks_ref[h]
        for i in range(DEC_SEQ // ATT_TQ):
            rows = slice(i * ATT_TQ, (i + 1) * ATT_TQ)
            o = _diff_softmax_pv(qs_ref[h, rows, :], [k_bf, ck], [v_bf, cv], lam)
            o_ref[rows, h * LANES:(h + 1) * LANES] = _subln(o, g, lam_init)


def _attn_smp(proj, lam, cache_k, cache_v, cos, sin, gq2, gk2, gs, layer, lam_init):
    qcol, kcol, vcol = 5 * A_WIDTH // LANES, 5 * A_WIDTH // LANES + 4, 5 * A_WIDTH // LANES + 8
    r0 = T_CTX // DEC_SEQ
    nh = ATT_HEADS

    def per_head(shape, index):
        return [pl.BlockSpec(shape, functools.partial(index, dh=dh)) for dh in range(nh)]

    head_specs = (
        per_head((DEC_SEQ, LANES), lambda b, hp, dh: (r0 + b, qcol + hp * nh + dh))
        + per_head((DEC_SEQ, LANES), lambda b, hp, dh: (r0 + b, kcol + hp * nh + dh))
        + per_head((DEC_SEQ, LANES), lambda b, hp, dh: (r0 + b, vcol + hp * nh + dh))
        + per_head((1, 1, 2, 1, PAST, B_DK), lambda b, hp, dh: (b, layer, 0, hp * nh + dh, 0, 0))
        + per_head((1, 1, 1, PAST, B_DV), lambda b, hp, dh: (b, layer, hp * nh + dh, 0, 0)))
    return pl.pallas_call(
        functools.partial(_attn_smp_kernel, lam_init=lam_init),
        grid=(DEC_BATCH, B_HEADS // nh),
        in_specs=[pl.BlockSpec(memory_space=pltpu.SMEM)] + head_specs + [
            pl.BlockSpec((DEC_SEQ, LANES), lambda b, hp: (0, 0)),
            pl.BlockSpec((DEC_SEQ, LANES), lambda b, hp: (0, 0)),
            pl.BlockSpec((1, LANES), lambda b, hp: (0, 0)),
            pl.BlockSpec((1, LANES), lambda b, hp: (0, 0)),
            pl.BlockSpec((1, LANES), lambda b, hp: (0, 0)),
        ],
        out_specs=pl.BlockSpec((DEC_SEQ, nh * LANES), lambda b, hp: (b, hp)),
        out_shape=jax.ShapeDtypeStruct((T_SMP, B_WIDTH), F32),
        scratch_shapes=[pltpu.VMEM((nh, DEC_SEQ, LANES), BF16), pltpu.VMEM((nh, DEC_SEQ, LANES), BF16)],
        compiler_params=_cparams(("arbitrary", "arbitrary")),
        name="diff_attention_smp",
    )(lam, *([proj] * (3 * nh)), *([cache_k] * nh), *([cache_v] * nh), cos, sin, gq2, gk2, gs)


def _rope_tables():
    n_rows = DEC_SEQ // GRID_W
    row = np.repeat(np.arange(n_rows), GRID_W).astype(np.float32)
    col = np.tile(np.arange(GRID_W), n_rows).astype(np.float32)
    half = B_DK // 2
    inv_freq = (ROPE_BASE ** (-jnp.arange(0, half, 2, dtype=F32) / half))
    row_ang = jnp.asarray(row)[:, None] * inv_freq
    col_ang = jnp.asarray(col)[:, None] * inv_freq
    ang = jnp.concatenate([row_ang, row_ang, col_ang, col_ang], axis=-1)
    ang = jnp.concatenate([ang, ang], axis=-1)
    sign = np.where((np.arange(LANES) % 32) < 16, -1.0, 1.0).astype(np.float32)
    return jnp.cos(ang), jnp.sin(ang) * sign


CM_ROWS = 1024


def _gelu(x):
    return 0.5 * x * (1.0 + lax.erf(x * (2.0 ** -0.5)))


def _cmlp_kernel(u_ref, v_ref, g_ref, b_ref, ws_ref, bs_ref, o_ref):
    u = _gelu(u_ref[...])
    gv = _gelu(v_ref[...])
    mu = jnp.mean(gv, axis=-1, keepdims=True)
    dv = gv - mu
    var = jnp.mean(dv * dv, axis=-1, keepdims=True)
    vn = (dv * lax.rsqrt(var + EPS) * g_ref[...] + b_ref[...]).astype(BF16)
    lane = _lane((C_CHUNK, LANES))
    for c in range(CM_ROWS // C_CHUNK):
        rs = slice(c * C_CHUNK, (c + 1) * C_CHUNK)
        for p in range(2):
            cs = slice(p * LANES, (p + 1) * LANES)
            vp = vn[rs, cs]
            m0 = jnp.dot(ws_ref[2 * p].astype(BF16), vp, preferred_element_type=F32)
            m1 = jnp.dot(ws_ref[2 * p + 1].astype(BF16), vp, preferred_element_type=F32)
            mixed = jnp.where(lane < C_DG, m0, m1) + bs_ref[:, cs]
            o_ref[rs, cs] = u[rs, cs] * mixed


def _chunk_mlp(proj, ln_g, ln_b, w_s, bias_full):
    ucol = (5 * A_WIDTH + 3 * B_WIDTH) // C_WIDTH
    return pl.pallas_call(
        _cmlp_kernel,
        grid=(T // CM_ROWS,),
        in_specs=[
            pl.BlockSpec((CM_ROWS, C_WIDTH), lambda i: (i, ucol)),
            pl.BlockSpec((CM_ROWS, C_WIDTH), lambda i: (i, ucol + 1)),
            pl.BlockSpec((1, C_WIDTH), lambda i: (0, 0)),
            pl.BlockSpec((1, C_WIDTH), lambda i: (0, 0)),
            pl.BlockSpec((C_GROUPS, C_CHUNK, C_CHUNK), lambda i: (0, 0, 0)),
            pl.BlockSpec((C_CHUNK, C_WIDTH), lambda i: (0, 0)),
        ],
        out_specs=pl.BlockSpec((CM_ROWS, C_WIDTH), lambda i: (i, 0)),
        out_shape=jax.ShapeDtypeStruct((T, C_WIDTH), F32),
        compiler_params=_cparams(("arbitrary",)),
        name="chunk_mlp",
    )(proj, proj, ln_g.reshape(1, C_WIDTH), ln_b.reshape(1, C_WIDTH), w_s, bias_full)


PM_ROWS = 512
PM_SUB = TM


def _postmix_kernel(*refs, n_x):
    (of_ref, ob_ref, ag_ref, hg_ref, hsel_ref, bc_ref, bs_ref, c_ref, w_ref, gate1_ref, shift2_ref, scale2_ref,
     g2_ref, wrh_ref, wrl_ref, br_ref, x1_ref, h2_ref, idx_ref, gw_ref) = refs[n_x:]
    is_ctx = pl.program_id(0) < T_CTX // PM_ROWS
    x_refs = refs[:n_x]
    for s in range(PM_ROWS // PM_SUB):
        rows = slice(s * PM_SUB, (s + 1) * PM_SUB)
        x = x_refs[0][rows, :] if n_x == 1 else jnp.where(is_ctx, x_refs[0][rows, :], x_refs[1][rows, :])
        o = of_ref[rows, :] + ob_ref[rows, :]
        ms = _dot_sel(o * o, hsel_ref[...]) * (1.0 / A_DK)
        ag = ag_ref[rows, :]
        a = o * lax.rsqrt(ms + EPS) * hg_ref[...] * (ag * jax.nn.sigmoid(ag))
        b = jnp.where(is_ctx, bc_ref[rows, :], bs_ref[rows, :])
        mixed = jnp.dot(a.astype(BF16), w_ref[0:A_WIDTH, :], preferred_element_type=F32)
        mixed = mixed + jnp.dot(b.astype(BF16), w_ref[A_WIDTH:A_WIDTH + B_WIDTH, :], preferred_element_type=F32)
        mixed = mixed + jnp.dot(c_ref[rows, :].astype(BF16), w_ref[A_WIDTH + B_WIDTH:, :],
                                preferred_element_type=F32)
        x1 = x + gate1_ref[0] * mixed
        x1_ref[rows, :] = x1
        y = x1 * lax.rsqrt(jnp.mean(x1 * x1, axis=-1, keepdims=True) + EPS) * g2_ref[...]
        h2 = y * (1.0 + scale2_ref[0]) + shift2_ref[0]
        h2_ref[rows] = _pack_rows(h2)
        hi = h2.astype(BF16)
        lo = (h2 - hi.astype(F32)).astype(BF16)
        lg = jnp.dot(lo, wrh_ref[...], preferred_element_type=F32)
        lg = lg + jnp.dot(hi, wrl_ref[...], preferred_element_type=F32)
        lg = lg + jnp.dot(hi, wrh_ref[...], preferred_element_type=F32) + br_ref[...]
        lt = lg.T[:N_EXPERTS]
        row = lax.broadcasted_iota(jnp.int32, lt.shape, 0)
        out_row = lax.broadcasted_iota(jnp.int32, (8, PM_SUB), 0)
        idx_out = jnp.zeros((8, PM_SUB), jnp.int32)
        val_out = jnp.zeros((8, PM_SUB), F32)
        top0 = None
        den = None
        for kk in range(TOP_K):
            mx = jnp.max(lt, axis=0, keepdims=True)
            am = jnp.min(jnp.where(lt == mx, row, N_EXPERTS), axis=0, keepdims=True)
            if kk == 0:
                top0 = mx
            e = jnp.exp(mx - top0)
            den = e if den is None else den + e
            idx_out = jnp.where(out_row == kk, am, idx_out)
            val_out = jnp.where(out_row == kk, e, val_out)
            lt = jnp.where(row == am, -jnp.inf, lt)
        idx_ref[:, rows] = idx_out
        gw_ref[:, rows] = val_out / den


def _post_mix(o_f, o_b, proj, hg, hsel, b_ctx, b_smp, c_out, w_out_bf, x, mod3, g2, wr_hi, wr_lo, br_pad):
    tile = lambda w: pl.BlockSpec((PM_ROWS, w), lambda i: (i, 0))
    const = lambda shape: pl.BlockSpec(shape, lambda i: tuple(0 for _ in shape))
    modspec = lambda j: pl.BlockSpec((1, 1, D), lambda i: (_mod_row(i, PM_ROWS), 0, j))
    rowsT = pl.BlockSpec((8, PM_ROWS), lambda i: (0, i))
    x_specs, x_args = _stream_specs(x, PM_ROWS)
    ctx_tiles = T_CTX // PM_ROWS
    return pl.pallas_call(
        functools.partial(_postmix_kernel, n_x=len(x_args)),
        grid=(T // PM_ROWS,),
        in_specs=x_specs + [
            tile(A_WIDTH), tile(A_WIDTH),
            pl.BlockSpec((PM_ROWS, A_WIDTH), lambda i: (i, 4)),
            const((1, A_WIDTH)), const((A_WIDTH, A_WIDTH)),
            pl.BlockSpec((PM_ROWS, B_WIDTH), lambda i: (jnp.minimum(i, ctx_tiles - 1), 0)),
            pl.BlockSpec((PM_ROWS, B_WIDTH), lambda i: (jnp.maximum(i - ctx_tiles, 0), 0)),
            tile(C_WIDTH),
            const((D, D)),
            modspec(2), modspec(3), modspec(4),
            const((1, D)), const((D, LANES)), const((D, LANES)), const((1, LANES)),
        ],
        out_specs=[tile(D), pl.BlockSpec((PM_ROWS, ROW_TILES, LANES), lambda i: (i, 0, 0)), rowsT, rowsT],
        out_shape=[
            jax.ShapeDtypeStruct((T, D), F32),
            jax.ShapeDtypeStruct((T, ROW_TILES, LANES), ROW_DT),
            jax.ShapeDtypeStruct((8, T), jnp.int32),
            jax.ShapeDtypeStruct((8, T), F32),
        ],
        compiler_params=_cparams(("arbitrary",)),
        name="post_mix_router",
    )(*x_args, o_f, o_b, proj, hg, hsel, b_ctx, b_smp, c_out, w_out_bf, mod3, mod3, mod3, g2.reshape(1, D),
      wr_hi, wr_lo, br_pad)


def _route_kernel(idx_ref, dest_ref, meta_ref):
    erow = lax.broadcasted_iota(jnp.int32, (N_EXPERTS, TM), 0)
    s_i = lax.broadcasted_iota(jnp.int32, (TM, TM), 0)
    t_i = lax.broadcasted_iota(jnp.int32, (TM, TM), 1)
    earlier = (s_i < t_i).astype(BF16)
    out_row = lax.broadcasted_iota(jnp.int32, (8, TM), 0)

    def onehots(i):
        idx = idx_ref[:, pl.ds(pl.multiple_of(i * TM, TM), TM)]
        return [(erow == idx[kk:kk + 1, :]) for kk in range(TOP_K)]

    def count_tile(i, run):
        ohs = onehots(i)
        base = run
        pos = jnp.zeros((8, TM), F32)
        for kk in range(TOP_K):
            ohf = ohs[kk].astype(F32)
            before = jnp.dot(ohs[kk].astype(BF16), earlier, preferred_element_type=F32)
            p = jnp.sum(ohf * (base + before), axis=0, keepdims=True)
            pos = jnp.where(out_row == kk, p, pos)
            base = base + jnp.sum(ohf, axis=1, keepdims=True)
        dest_ref[:, pl.ds(pl.multiple_of(i * TM, TM), TM)] = pos.astype(jnp.int32)
        return base

    counts = lax.fori_loop(0, N_TILES, count_tile, jnp.zeros((N_EXPERTS, 1), F32)).astype(jnp.int32)
    bm_shift = MOE_BM.bit_length() - 1
    padded = lax.shift_left(lax.shift_right_logical(counts + (MOE_BM - 1), bm_shift), bm_shift)
    e_r = lax.broadcasted_iota(jnp.int32, (N_EXPERTS, N_EXPERTS), 0)
    e_c = lax.broadcasted_iota(jnp.int32, (N_EXPERTS, N_EXPERTS), 1)
    incl = (e_c <= e_r).astype(BF16)
    pad_end = _sel_dot(incl, jnp.broadcast_to(padded.astype(F32), (N_EXPERTS, LANES)))[:, :1]
    pad_start = pad_end - padded.astype(F32)

    def place_tile(i, carry):
        ohs = onehots(i)
        sl = pl.ds(pl.multiple_of(i * TM, TM), TM)
        off = jnp.zeros((8, TM), F32)
        for kk in range(TOP_K):
            o = jnp.sum(ohs[kk].astype(F32) * pad_start, axis=0, keepdims=True)
            off = jnp.where(out_row == kk, o, off)
        dest_ref[:, sl] = dest_ref[:, sl] + off.astype(jnp.int32)
        return carry

    lax.fori_loop(0, N_TILES, place_tile, 0)

    total = jnp.max(pad_end, axis=0, keepdims=True)
    lane_i = lax.broadcasted_iota(jnp.int32, (1, TM), 1)
    blk0 = (lane_i * MOE_BM).astype(F32)
    block_e = jnp.sum((pad_end <= blk0).astype(F32), axis=0, keepdims=True)
    live_end = pad_start + counts.astype(F32)
    sel = erow.astype(F32) == block_e
    live = jnp.sum(jnp.where(sel, live_end, 0.0), axis=0, keepdims=True)
    valid = jnp.where(blk0 < total, jnp.clip(live - blk0, 0.0, float(MOE_BM)), 0.0)
    own = erow == lane_i
    n_blk = jnp.sum(jnp.where(own, padded.astype(F32), 0.0), axis=0, keepdims=True) * (1.0 / MOE_BM)
    first_blk = jnp.sum(jnp.where(own, pad_start, 0.0), axis=0, keepdims=True) * (1.0 / MOE_BM)
    meta = jnp.where(out_row == 0, valid, 0.0)
    meta = jnp.where(out_row == 1, n_blk, meta)
    meta = jnp.where(out_row == 2, first_blk, meta)
    meta = jnp.where(out_row == 3, total * (1.0 / MOE_BM), meta)
    meta_ref[...] = meta.astype(jnp.int32)


def _route(idx_t):
    assert MOE_BLOCKS <= TM
    return pl.pallas_call(
        _route_kernel,
        out_shape=[jax.ShapeDtypeStruct((8, T), jnp.int32), jax.ShapeDtypeStruct((8, TM), jnp.int32)],
        compiler_params=pltpu.CompilerParams(vmem_limit_bytes=VMEM_LIMIT),
        name="moe_route",
    )(idx_t)


def _moe_kernel(bv_ref, nb_ref, g0_ref, tot_ref, x_hbm, wgu_ref, bgu_ref, wdn_ref, bdn_ref, y_hbm,
                wgu_bf, wdn_bf, xbuf, ybuf, xb_ref, xsem, ysem):
    e = pl.program_id(0)
    n_blk = nb_ref[e]
    first = g0_ref[e]
    total = tot_ref[0]
    ahead = MOE_RING - 1

    def x_copy(g):
        slot = g % MOE_RING
        return pltpu.make_async_copy(x_hbm.at[pl.ds(g * MOE_BM, MOE_BM)], xbuf.at[slot], xsem.at[slot])

    def y_copy(g):
        slot = g % MOE_RING
        return pltpu.make_async_copy(ybuf.at[slot], y_hbm.at[pl.ds(g * MOE_BM, MOE_BM)], ysem.at[slot])

    @pl.when(e == 0)
    def _():
        for g in range(ahead):
            @pl.when(g < total)
            def _():
                x_copy(g).start()

    @pl.when(n_blk > 0)
    def _():
        wgu_bf[...] = wgu_ref[0, 0].astype(BF16)
        wdn_bf[...] = wdn_ref[0, 0].astype(BF16)

    def block(j, carry):
        g = first + j
        slot = g % MOE_RING
        x_copy(g).wait()

        @pl.when(g + ahead < total)
        def _():
            x_copy(g + ahead).start()

        @pl.when(g >= MOE_RING)
        def _():
            y_copy(g - MOE_RING).wait()

        n_live = bv_ref[g]

        def ffn(n_rows):
            live = lax.broadcasted_iota(jnp.int32, (n_rows, LANES), 0) < n_live
            for c, chunk in enumerate(_unpack_rows(xbuf[slot, :n_rows])):
                xb_ref[:n_rows, c * LANES:(c + 1) * LANES] = jnp.where(live, chunk, 0.0).astype(BF16)
            gu = jnp.dot(xb_ref[:n_rows, :], wgu_bf[...], preferred_element_type=F32) + bgu_ref[0, 0]
            glu = jnp.minimum(gu[:, :D], SWIGLU_LIMIT)
            lin = jnp.clip(gu[:, D:], -SWIGLU_LIMIT, SWIGLU_LIMIT)
            act = glu * jax.nn.sigmoid(SWIGLU_ALPHA * glu) * (lin + 1.0)
            y = jnp.dot(act.astype(BF16), wdn_bf[...], preferred_element_type=F32) + bdn_ref[0, 0]
            ybuf[slot, :n_rows] = _pack_rows(y)

        @pl.when(n_live > MOE_BM // 2)
        def _():
            ffn(MOE_BM)

        @pl.when(n_live <= MOE_BM // 2)
        def _():
            ffn(MOE_BM // 2)
            ybuf[slot, MOE_BM // 2:] = jnp.zeros((MOE_BM // 2, ROW_TILES, LANES), ROW_DT)

        y_copy(g).start()
        return carry

    lax.fori_loop(0, n_blk, block, 0)

    @pl.when(e == N_EXPERTS - 1)
    def _():
        for back in range(MOE_RING, 0, -1):
            @pl.when(total >= back)
            def _():
                y_copy(total - back).wait()

        def fill(g, carry):
            ybuf[g % MOE_RING] = jnp.zeros((MOE_BM, ROW_TILES, LANES), ROW_DT)
            y_copy(g).start()
            y_copy(g).wait()
            return carry

        lax.fori_loop(total, MOE_BLOCKS, fill, 0)


def _moe_ffn(block_valid, n_blk, first_blk, total_blk, xs, w_gu, b_gu, w_dn, b_dn, layer):
    rows = (MOE_BM, ROW_TILES, LANES)
    return pl.pallas_call(
        _moe_kernel,
        grid_spec=pltpu.PrefetchScalarGridSpec(
            num_scalar_prefetch=4,
            grid=(N_EXPERTS,),
            in_specs=[
                pl.BlockSpec(memory_space=pl.ANY),
                pl.BlockSpec((1, 1, D, 2 * D), lambda e, *_: (layer, e, 0, 0)),
                pl.BlockSpec((1, 1, 1, 2 * D), lambda e, *_: (layer, e, 0, 0)),
                pl.BlockSpec((1, 1, D, D), lambda e, *_: (layer, e, 0, 0)),
                pl.BlockSpec((1, 1, 1, D), lambda e, *_: (layer, e, 0, 0)),
            ],
            out_specs=pl.BlockSpec(memory_space=pl.ANY),
            scratch_shapes=[
                pltpu.VMEM((D, 2 * D), BF16), pltpu.VMEM((D, D), BF16),
                pltpu.VMEM((MOE_RING,) + rows, ROW_DT), pltpu.VMEM((MOE_RING,) + rows, ROW_DT),
                pltpu.VMEM((MOE_BM, D), BF16),
                pltpu.SemaphoreType.DMA((MOE_RING,)), pltpu.SemaphoreType.DMA((MOE_RING,)),
            ],
        ),
        out_shape=jax.ShapeDtypeStruct((MOE_ROWS, ROW_TILES, LANES), ROW_DT),
        compiler_params=_cparams(("arbitrary",)),
        name="moe_expert_ffn",
    )(block_valid, n_blk, first_blk, total_blk, xs, w_gu, b_gu.reshape(DEPTH, N_EXPERTS, 1, 2 * D), w_dn,
      b_dn.reshape(DEPTH, N_EXPERTS, 1, D))


def _combine_kernel(x1_ref, y_ref, gw_ref, gate2_ref, o_ref):
    gw = jnp.concatenate([gw_ref[...], jnp.zeros((LANES - 8, TM), F32)], axis=0).T
    ys = [_unpack_rows(y_ref[kk]) for kk in range(TOP_K)]
    for c in range(D // LANES):
        cs = slice(c * LANES, (c + 1) * LANES)
        acc = None
        for kk in range(TOP_K):
            term = ys[kk][c] * gw[:, kk:kk + 1]
            acc = term if acc is None else acc + term
        o_ref[:, cs] = x1_ref[:, cs] + gate2_ref[0, :, cs] * acc


def _combine(x1, yg, gw, mod3, tile0, n_tiles):
    return pl.pallas_call(
        _combine_kernel,
        grid=(n_tiles,),
        in_specs=[
            pl.BlockSpec((TM, D), lambda i: (tile0 + i, 0)),
            pl.BlockSpec((TOP_K, TM, ROW_TILES, LANES), lambda i: (0, i, 0, 0)),
            pl.BlockSpec((8, TM), lambda i: (0, tile0 + i)),
            pl.BlockSpec((1, 1, D), lambda i: (_mod_row(tile0 + i), 0, 5)),
        ],
        out_specs=pl.BlockSpec((TM, D), lambda i: (i, 0)),
        out_shape=jax.ShapeDtypeStruct((n_tiles * TM, D), F32),
        compiler_params=_cparams(("arbitrary",)),
        name="moe_combine",
    )(x1, yg, gw, mod3)


def _sc_mesh():
    return plsc.VectorSubcoreMesh(core_axis_name="c", subcore_axis_name="s")


def _sc_worker():
    return lax.axis_index("s") * SC_CORES + lax.axis_index("c")


def _sc_dispatch(h2t, dest_km):
    per_w = T // SC_WORKERS

    @functools.partial(
        pl.kernel, mesh=_sc_mesh(),
        out_type=jax.ShapeDtypeStruct((MOE_ROWS, ROW_TILES, LANES), ROW_DT),
        scratch_types=[pltpu.VMEM((SC_WIN,), jnp.int32), pltpu.VMEM((SC_WIN, ROW_TILES, LANES), ROW_DT),
                       pltpu.SemaphoreType.DMA],
    )
    def run(h_hbm, d_hbm, o_hbm, idx_v, rows_v, sem):
        w0 = _sc_worker() * per_w

        @pl.loop(0, per_w // SC_WIN)
        def _(w):
            base = pl.multiple_of(w0 + w * SC_WIN, SC_WIN)
            pltpu.sync_copy(h_hbm.at[pl.ds(base, SC_WIN)], rows_v)
            for kk in range(TOP_K):
                pltpu.sync_copy(d_hbm.at[pl.ds(kk * T + base, SC_WIN)], idx_v)
                pltpu.async_copy(rows_v, o_hbm.at[idx_v], sem).wait()

    return run(h2t, dest_km)


def _sc_gather(yb, dest):
    n = dest.shape[0]
    per_w = n // SC_WORKERS

    @functools.partial(
        pl.kernel, mesh=_sc_mesh(),
        out_type=jax.ShapeDtypeStruct((n, ROW_TILES, LANES), ROW_DT),
        scratch_types=[pltpu.VMEM((SC_WIN,), jnp.int32), pltpu.VMEM((SC_WIN, ROW_TILES, LANES), ROW_DT),
                       pltpu.SemaphoreType.DMA],
    )
    def run(y_hbm, d_hbm, o_hbm, idx_v, rows_v, sem):
        w0 = _sc_worker() * per_w

        @pl.loop(0, per_w // SC_WIN)
        def _(w):
            base = pl.multiple_of(w0 + w * SC_WIN, SC_WIN)
            pltpu.sync_copy(d_hbm.at[pl.ds(base, SC_WIN)], idx_v)
            pltpu.async_copy(y_hbm.at[idx_v], rows_v, sem).wait()
            pltpu.sync_copy(rows_v, o_hbm.at[pl.ds(base, SC_WIN)])

    return run(yb, dest)


def kernel(x_prompt, x_sample, c, cache_diff_k, cache_diff_v, state_hgrn, c_ctx, norm_mix_g, norm_ffn_g, w_mod, b_mod, w_in, w_out, hgrn_lower_bounds, hgrn_norm_g, diff_q_norm_g, diff_k_norm_g, diff_lambda_q1, diff_lambda_k1, diff_lambda_q2, diff_lambda_k2, diff_subln_g, cmlp_ln_g, cmlp_ln_b, cmlp_w_s, cmlp_b_s, router_w, router_b, moe_w_gate_up, moe_b_gate_up, moe_w_down, moe_b_down):
    x = (x_prompt.reshape(T_CTX, D), x_sample.reshape(T_SMP, D))
    cvec = jnp.concatenate([c_ctx[None, :], c, jnp.zeros((MOD_ROWS - 1 - DEC_BATCH, D), F32)], axis=0)
    mod = _modulation(cvec, w_mod, b_mod)

    lvl_np, tri_np = _hgrn_tables()
    lvl = jnp.asarray(lvl_np)
    tri = jnp.asarray(tri_np, dtype=BF16)
    cos, sin = _rope_tables()
    hsel = jnp.asarray(np.kron(np.eye(A_HEADS), np.ones((A_DK, A_DK))), dtype=BF16)
    sm = jax.nn.softmax(hgrn_lower_bounds.astype(F32), axis=0)
    lb_all = jnp.cumsum(sm, axis=0) - sm[0]

    new_k, new_v, new_s = None, None, []
    for l in range(DEPTH):
        mod3 = mod[l].reshape(MOD_ROWS, 1, 6 * D)
        proj = _in_projection(x, norm_mix_g[l], mod3, w_in[l].astype(BF16))

        s0 = jnp.concatenate([jnp.zeros((BATCH, 2, A_HEADS, A_DK, A_DK), F32), state_hgrn[:, l]], axis=0)
        o_f, fin_f, o_b, fin_b = _hgrn_scan(proj, [lb_all[l, d].reshape(1, A_WIDTH) for d in range(2)],
                                            [_pack_state(s0[:, d]) for d in range(2)], lvl, tri, l)
        o_dir = [o_f, o_b]
        new_s.append(jnp.stack([_unpack_state(fin_f[:BATCH]), _unpack_state(fin_b[:BATCH])], axis=1))

        lam_init = 0.8 - 0.6 * math.exp(-0.3 * l)
        lam = (jnp.exp(jnp.sum(diff_lambda_q1[l] * diff_lambda_k1[l]))
               - jnp.exp(jnp.sum(diff_lambda_q2[l] * diff_lambda_k2[l])) + lam_init).reshape(1, 1)
        gq2 = jnp.tile(diff_q_norm_g[l], 2).reshape(1, LANES)
        gk2 = jnp.tile(diff_k_norm_g[l], 2).reshape(1, LANES)
        gs = diff_subln_g[l].reshape(1, LANES)
        b_ctx, new_k, new_v = _attn_ctx(proj, lam, gq2, gk2, gs, lam_init, l, new_k, new_v)
        b_smp = _attn_smp(proj, lam, cache_diff_k, cache_diff_v, cos, sin, gq2, gk2, gs, l, lam_init)

        bias_full = jnp.repeat(cmlp_b_s[l].T, C_DG, axis=1)
        c_out = _chunk_mlp(proj, cmlp_ln_g[l], cmlp_ln_b[l], cmlp_w_s[l], bias_full)

        hg = jnp.tile(hgrn_norm_g[l], A_HEADS).reshape(1, A_WIDTH)
        wr_pad = jnp.pad(router_w[l], ((0, 0), (0, LANES - N_EXPERTS)))
        wr_hi = wr_pad.astype(BF16)
        wr_lo = (wr_pad - wr_hi.astype(F32)).astype(BF16)
        br_pad = jnp.pad(router_b[l], (0, LANES - N_EXPERTS)).reshape(1, LANES)
        x1, h2, idx_t, gw_t = _post_mix(o_dir[0], o_dir[1], proj, hg, hsel, b_ctx, b_smp, c_out,
                                        w_out[l].astype(BF16), x, mod3, norm_ffn_g[l], wr_hi, wr_lo, br_pad)

        dest_t, meta = _route(idx_t)
        dest_km = dest_t[:TOP_K].reshape(-1)
        xs = _sc_dispatch(h2, dest_km)
        yb = _moe_ffn(meta[0, :MOE_BLOCKS], meta[1, :N_EXPERTS], meta[2, :N_EXPERTS], meta[3, :1], xs,
                      moe_w_gate_up, moe_b_gate_up, moe_w_down, moe_b_down, l)
        yg_ctx = _sc_gather(yb, dest_t[:TOP_K, :T_CTX].reshape(-1)).reshape(TOP_K, T_CTX, ROW_TILES, LANES)
        yg_smp = _sc_gather(yb, dest_t[:TOP_K, T_CTX:].reshape(-1)).reshape(TOP_K, T_SMP, ROW_TILES, LANES)
        x = (_combine(x1, yg_ctx, gw_t, mod3, 0, CTX_TILES),
             _combine(x1, yg_smp, gw_t, mod3, CTX_TILES, N_TILES - CTX_TILES))

    y_prompt = x[0].reshape(BATCH, SEQ, D)
    y_sample = x[1].reshape(DEC_BATCH, DEC_SEQ, D)
    return (y_prompt, y_sample, new_k, new_v, jnp.stack(new_s, axis=1))
```

```python
import functools
import math

import numpy as np
import jax
import jax.numpy as jnp
from jax import lax
from jax.experimental import pallas as pl
from jax.experimental.pallas import tpu as pltpu
from jax.experimental.pallas import tpu_sc as plsc

F32 = jnp.float32
BF16 = jnp.bfloat16

D = 1024
DEPTH = 2
BATCH, SEQ = 16, 256
DEC_BATCH, DEC_SEQ = 8, 1024
PAST = 512
GRID_W = 64
A_HEADS, A_DK = 4, 64
A_WIDTH = 256
B_HEADS, B_DK, B_DV = 4, 64, 128
B_WIDTH = 512
C_GROUPS, C_CHUNK, C_WIDTH, C_DG = 4, 128, 256, 64
IN_WIDTH = 5 * A_WIDTH + 3 * B_WIDTH + 2 * C_WIDTH
N_EXPERTS, TOP_K = 32, 4
SWIGLU_LIMIT, SWIGLU_ALPHA = 7.0, 1.702
ROPE_BASE = 10000.0
EPS = 1e-6

T_CTX = BATCH * SEQ
T_SMP = DEC_BATCH * DEC_SEQ
T = T_CTX + T_SMP
N_SEQ = BATCH + DEC_BATCH
MOD_ROWS = 16

TM = 256
N_TILES = T // TM
CTX_TILES = T_CTX // TM
SMP_TILES_PER_SEQ = DEC_SEQ // TM
LANES = 128
MOE_BM = 256
MOE_ROWS = T * TOP_K + N_EXPERTS * MOE_BM
MOE_BLOCKS = MOE_ROWS // MOE_BM
MOE_RING = 4
ROW_WORDS = D // 2
ROW_TILES = ROW_WORDS // LANES
ROW_DT = jnp.int32
SC_CORES, SC_SUBCORES = 2, 16
SC_WORKERS = SC_CORES * SC_SUBCORES
SC_WIN = 128
VMEM_LIMIT = 56 * 1024 * 1024


def _cparams(sem):
    return pltpu.CompilerParams(dimension_semantics=sem, vmem_limit_bytes=VMEM_LIMIT)


def _mod_row(i, rows=TM):
    ctx_tiles = T_CTX // rows
    return jnp.where(i < ctx_tiles, 0, 1 + (i - ctx_tiles) // (DEC_SEQ // rows))


def _split3(x):
    hi = x.astype(BF16)
    r = x - hi.astype(F32)
    mid = r.astype(BF16)
    lo = (r - mid.astype(F32)).astype(BF16)
    return hi, mid, lo


def _sel_dot(sel, x):
    hi, mid, lo = _split3(x)
    acc = jnp.dot(sel, lo, preferred_element_type=F32)
    acc = acc + jnp.dot(sel, mid, preferred_element_type=F32)
    return acc + jnp.dot(sel, hi, preferred_element_type=F32)


def _dot_sel(x, sel):
    hi, mid, lo = _split3(x)
    acc = jnp.dot(lo, sel, preferred_element_type=F32)
    acc = acc + jnp.dot(mid, sel, preferred_element_type=F32)
    return acc + jnp.dot(hi, sel, preferred_element_type=F32)


def _dot_nt(a, b):
    return lax.dot_general(a, b, (((1,), (1,)), ((), ())), preferred_element_type=F32)


def _dot_tn(a, b):
    return lax.dot_general(a, b, (((0,), (0,)), ((), ())), preferred_element_type=F32)


def _lane(shape):
    return lax.broadcasted_iota(jnp.int32, shape, len(shape) - 1)


def _pack_rows(x):
    hi = lax.bitcast_convert_type(x[:, :ROW_WORDS].astype(BF16).astype(F32), jnp.int32)
    lo = lax.bitcast_convert_type(x[:, ROW_WORDS:].astype(BF16).astype(F32), jnp.int32)
    words = hi | lax.shift_right_logical(lo, 16)
    return pltpu.einshape("t(jl)->tjl", words, l=LANES)


def _unpack_rows(words3):
    wt = pltpu.einshape("tjl->jtl", words3)
    hi = [lax.bitcast_convert_type(wt[j] & jnp.int32(-65536), F32) for j in range(ROW_TILES)]
    lo = [lax.bitcast_convert_type(lax.shift_left(wt[j], 16), F32) for j in range(ROW_TILES)]
    return hi + lo


def _mod_kernel(c_ref, w_ref, b_ref, o_ref):
    c = c_ref[...]
    s = c * jax.nn.sigmoid(c)
    o_ref[0] = jnp.dot(s.astype(BF16), w_ref[0].astype(BF16), preferred_element_type=F32) + b_ref[0]


def _modulation(cvec, w_mod, b_mod):
    tn = 1536
    return pl.pallas_call(
        _mod_kernel,
        grid=(DEPTH, 6 * D // tn),
        in_specs=[
            pl.BlockSpec((MOD_ROWS, D), lambda l, j: (0, 0)),
            pl.BlockSpec((1, D, tn), lambda l, j: (l, 0, j)),
            pl.BlockSpec((1, 1, tn), lambda l, j: (l, 0, j)),
        ],
        out_specs=pl.BlockSpec((1, MOD_ROWS, tn), lambda l, j: (l, 0, j)),
        out_shape=jax.ShapeDtypeStruct((DEPTH, MOD_ROWS, 6 * D), F32),
        compiler_params=_cparams(("arbitrary", "arbitrary")),
        name="modulation",
    )(cvec, w_mod, b_mod.reshape(DEPTH, 1, 6 * D))


def _stream_specs(x, rows=TM):
    ctx_tiles = T_CTX // rows
    if isinstance(x, tuple):
        return [pl.BlockSpec((rows, D), lambda i: (jnp.minimum(i, ctx_tiles - 1), 0)),
                pl.BlockSpec((rows, D), lambda i: (jnp.maximum(i - ctx_tiles, 0), 0))], list(x)
    return [pl.BlockSpec((rows, D), lambda i: (i, 0))], [x]


IN_ROWS = 512


def _inproj_kernel(*refs, n_x):
    g_ref, shift_ref, scale_ref, w_ref, o_ref = refs[n_x:]
    is_ctx = pl.program_id(0) < T_CTX // IN_ROWS
    for s in range(IN_ROWS // TM):
        rows = slice(s * TM, (s + 1) * TM)
        x = refs[0][rows, :] if n_x == 1 else jnp.where(is_ctx, refs[0][rows, :], refs[1][rows, :])
        y = x * lax.rsqrt(jnp.mean(x * x, axis=-1, keepdims=True) + EPS) * g_ref[...]
        h = y * (1.0 + scale_ref[0]) + shift_ref[0]
        o_ref[rows, :] = jnp.dot(h.astype(BF16), w_ref[...], preferred_element_type=F32)


def _in_projection(x, g, mod3, w_in_bf):
    x_specs, x_args = _stream_specs(x, IN_ROWS)
    mod_row = functools.partial(_mod_row, rows=IN_ROWS)
    return pl.pallas_call(
        functools.partial(_inproj_kernel, n_x=len(x_args)),
        grid=(T // IN_ROWS,),
        in_specs=x_specs + [
            pl.BlockSpec((1, D), lambda i: (0, 0)),
            pl.BlockSpec((1, 1, D), lambda i: (mod_row(i), 0, 0)),
            pl.BlockSpec((1, 1, D), lambda i: (mod_row(i), 0, 1)),
            pl.BlockSpec((D, IN_WIDTH), lambda i: (0, 0)),
        ],
        out_specs=pl.BlockSpec((IN_ROWS, IN_WIDTH), lambda i: (i, 0)),
        out_shape=jax.ShapeDtypeStruct((T, IN_WIDTH), F32),
        compiler_params=_cparams(("arbitrary",)),
        name="in_projection",
    )(*x_args, g.reshape(1, D), mod3, mod3, w_in_bf)


HG_C = 128
HG_LEVELS = tuple(2 ** j for j in range(1, int(math.log2(HG_C)) + 1))


def _hgrn_tables():
    t = np.arange(HG_C)[:, None]
    s = np.arange(HG_C)[None, :]
    x = t ^ s
    lvl = np.zeros((HG_C, HG_C), np.int32)
    nz = x > 0
    lvl[nz] = np.floor(np.log2(x[nz])).astype(np.int32) + 1
    fwd = np.where(t >= s, lvl, -1).astype(np.int32)
    bwd = np.where(t <= s, lvl, -1).astype(np.int32)
    tri_f = (t >= s).astype(np.float32)
    tri_b = (t <= s).astype(np.float32)
    return np.stack([fwd, bwd]), np.stack([tri_f, tri_b])


def _block_ref(cum, m, idx):
    c, l = cum.shape
    if m >= 16:
        c3 = cum.reshape(c // m, m, l)
        r = c3[:, idx:idx + 1, :]
        return jnp.broadcast_to(r, (c // m, m, l)).reshape(c, l)
    c3 = cum.reshape(c // 8, 8, l)
    sub = lax.broadcasted_iota(jnp.int32, c3.shape, 1)
    out = None
    for j in range(8 // m - 1, -1, -1):
        cand = jnp.broadcast_to(c3[:, j * m + idx:j * m + idx + 1, :], c3.shape)
        out = cand if out is None else jnp.where(sub < (j + 1) * m, cand, out)
    return out.reshape(c, l)


def _hgrn_kernel(*refs, layer):
    n_in = 7
    fwd_in, bwd_in = refs[:n_in], refs[n_in:2 * n_in]
    o_f, fin_f, o_b, fin_b, st_f, st_b = refs[2 * n_in:]
    runs = [_hgrn_direction(*fwd_in, o_f, fin_f, st_f, layer=layer, rev=False),
            _hgrn_direction(*bwd_in, o_b, fin_b, st_b, layer=layer, rev=True)]
    for steps in zip(*runs):
        for step in steps:
            step()


def _hgrn_direction(q_ref, z_ref, v_ref, lb_ref, s0_ref, lvl_ref, tri_ref, o_ref, fin_ref, st_ref, *, layer, rev):
    g = pl.program_id(0)
    first = jnp.logical_or(g < CTX_TILES, (g - CTX_TILES) % SMP_TILES_PER_SEQ == 0)

    @pl.when(first)
    def _():
        st_ref[...] = s0_ref[0]

    qr = q_ref[...]
    q = qr * jax.nn.sigmoid(qr) * (A_DK ** -0.5)
    z = z_ref[...]
    if layer == 0:
        lf = jnp.minimum(z, 0.0) - jnp.log(1.0 + jnp.exp(-jnp.abs(z)))
        k = jax.nn.sigmoid(-z)
    else:
        lbd = lb_ref[...]
        lf = jnp.log(lbd + (1.0 - lbd) * jax.nn.sigmoid(z))
        k = (1.0 - lbd) * jax.nn.sigmoid(-z)
    v = v_ref[...]
    tri = tri_ref[0]
    lvl = lvl_ref[0]
    last_row = 0 if rev else HG_C - 1
    n_chunks = TM // HG_C
    order = range(n_chunks - 1, -1, -1) if rev else range(n_chunks)
    lf2 = lf * math.log2(math.e)
    cums = [_sel_dot(tri, lf2[c * HG_C:(c + 1) * HG_C]) for c in range(n_chunks)]
    lane = _lane((HG_C, LANES))
    head_masks = (lane < A_DK, lane >= A_DK)
    lane_row = _lane((1, LANES))
    head_keep = ((lane_row < A_DK).astype(BF16), (lane_row >= A_DK).astype(BF16))
    lvl2 = jnp.concatenate([lvl, lvl], axis=0)
    level_masks = [lvl2 == i for i in range(len(HG_LEVELS) + 1)]
    r = lax.broadcasted_iota(jnp.int32, (LANES, LANES), 0)
    cl = lax.broadcasted_iota(jnp.int32, (LANES, LANES), 1)
    same_head = (r < A_DK) == (cl < A_DK)

    def chunk(q_p, k_p, v_p, cum_p, st):
        v_bf = v_p.astype(BF16)
        k_bf = k_p.astype(BF16)
        q_bf = q_p.astype(BF16)

        def both_heads(x_bf):
            return jnp.concatenate([x_bf * head_keep[0], x_bf * head_keep[1]], axis=0)

        scores = jnp.where(level_masks[0], _dot_nt(both_heads(q_bf), k_bf), 0.0)
        for li, m in enumerate(HG_LEVELS):
            ref = _block_ref(cum_p, m, m // 2 if rev else m // 2 - 1)
            dec = jnp.exp2(-jnp.abs(cum_p - ref))
            qd = (q_p * dec).astype(BF16)
            kd = (k_p * dec).astype(BF16)
            scores = jnp.where(level_masks[li + 1], _dot_nt(both_heads(qd), kd), scores)
        pv = jnp.dot(scores.astype(BF16), v_bf, preferred_element_type=F32)
        o_intra = jnp.where(head_masks[0], pv[:HG_C], pv[HG_C:])
        q0 = (q_p * jnp.exp2(cum_p)).astype(BF16)
        out = o_intra + _dot_nt(q0, st.astype(BF16))
        last = cum_p[last_row:last_row + 1, :]
        ks = (k_p * jnp.exp2(last - cum_p)).astype(BF16)
        upd = _dot_tn(v_bf, ks)
        return out, st * jnp.exp2(last) + jnp.where(same_head, upd, 0.0)

    states = [st_ref[p] for p in range(2)]
    steps = []
    for p in range(2):
        sl = slice(p * LANES, (p + 1) * LANES)
        for n, c in enumerate(order):
            rows = slice(c * HG_C, (c + 1) * HG_C)

            def step(p=p, sl=sl, rows=rows, c=c, final=n == n_chunks - 1):
                o_ref[rows, sl], states[p] = chunk(q[rows, sl], k[rows, sl], v[rows, sl], cums[c][:, sl], states[p])
                if final:
                    st_ref[p] = states[p]
                    fin_ref[0, p] = states[p]

            steps.append(step)
    return steps


def _hgrn_seq(g):
    return jnp.where(g < CTX_TILES, g, CTX_TILES + (g - CTX_TILES) // SMP_TILES_PER_SEQ)


def _hgrn_blk(g, rev):
    if not rev:
        return g
    j = g - CTX_TILES
    return jnp.where(g < CTX_TILES, g,
                     CTX_TILES + (j // SMP_TILES_PER_SEQ) * SMP_TILES_PER_SEQ
                     + (SMP_TILES_PER_SEQ - 1 - j % SMP_TILES_PER_SEQ))


def _hgrn_scan(proj, lb_dirs, s0_dirs, lvl, tri, layer):
    def dir_in_specs(d):
        blk = functools.partial(_hgrn_blk, rev=d == 1)
        return [
            pl.BlockSpec((TM, A_WIDTH), lambda g: (blk(g), 0)),
            pl.BlockSpec((TM, A_WIDTH), lambda g: (blk(g), 1 + d)),
            pl.BlockSpec((TM, A_WIDTH), lambda g: (blk(g), 3)),
            pl.BlockSpec((1, A_WIDTH), lambda g: (0, 0)),
            pl.BlockSpec((1, 2, LANES, LANES), lambda g: (_hgrn_seq(g), 0, 0, 0)),
            pl.BlockSpec((1, HG_C, HG_C), lambda g: (d, 0, 0)),
            pl.BlockSpec((1, HG_C, HG_C), lambda g: (d, 0, 0)),
        ]

    def dir_out_specs(d):
        blk = functools.partial(_hgrn_blk, rev=d == 1)
        return [pl.BlockSpec((TM, A_WIDTH), lambda g: (blk(g), 0)),
                pl.BlockSpec((1, 2, LANES, LANES), lambda g: (_hgrn_seq(g), 0, 0, 0))]

    dir_out_shape = [jax.ShapeDtypeStruct((T, A_WIDTH), F32), jax.ShapeDtypeStruct((N_SEQ, 2, LANES, LANES), F32)]
    dir_args = lambda d: (proj, proj, proj, lb_dirs[d], s0_dirs[d], lvl, tri)
    return pl.pallas_call(
        functools.partial(_hgrn_kernel, layer=layer),
        grid=(N_TILES,),
        in_specs=dir_in_specs(0) + dir_in_specs(1),
        out_specs=dir_out_specs(0) + dir_out_specs(1),
        out_shape=dir_out_shape + dir_out_shape,
        scratch_shapes=[pltpu.VMEM((2, LANES, LANES), F32), pltpu.VMEM((2, LANES, LANES), F32)],
        compiler_params=_cparams(("arbitrary",)),
        name="hgrn_scan",
    )(*dir_args(0), *dir_args(1))


def _pack_state(s):
    n = s.shape[0]
    st = jnp.swapaxes(s, -1, -2).reshape(n, 2, 2, A_DK, A_DK)
    z = jnp.zeros_like(st[:, :, 0])
    top = jnp.concatenate([st[:, :, 0], z], axis=-1)
    bot = jnp.concatenate([z, st[:, :, 1]], axis=-1)
    return jnp.concatenate([top, bot], axis=-2)


def _unpack_state(sp):
    n = sp.shape[0]
    h0 = sp[:, :, :A_DK, :A_DK]
    h1 = sp[:, :, A_DK:, A_DK:]
    st = jnp.stack([h0, h1], axis=2).reshape(n, A_HEADS, A_DK, A_DK)
    return jnp.swapaxes(st, -1, -2)


def _half_rms(x, g):
    r = lax.broadcasted_iota(jnp.int32, (LANES, LANES), 0)
    c = lax.broadcasted_iota(jnp.int32, (LANES, LANES), 1)
    half_mean = jnp.where((r < B_DK) == (c < B_DK), 1.0 / B_DK, 0.0).astype(BF16)
    xx = x * x
    hi = xx.astype(BF16)
    lo = (xx - hi.astype(F32)).astype(BF16)
    ms = jnp.dot(lo, half_mean, preferred_element_type=F32) + jnp.dot(hi, half_mean, preferred_element_type=F32)
    return x * lax.rsqrt(ms + EPS) * g


def _rope(x, cos, sin_signed):
    lane = _lane(x.shape)
    first = (lane % 32) < 16
    rot = jnp.where(first, pltpu.roll(x, LANES - 16, 1), pltpu.roll(x, 16, 1))
    return x * cos + rot * sin_signed


def _with_ones(v_bf):
    return jnp.concatenate([v_bf, jnp.ones_like(v_bf)], axis=-1)


def _diff_softmax_pv(q_bf, keys_bf, vals_ext, lam):
    lane = _lane(q_bf.shape)
    zero = jnp.zeros_like(q_bf)
    outs = []
    for mp in range(2):
        qm = jnp.where((lane < B_DK) == (mp == 0), q_bf, zero)
        s = [_dot_nt(qm, kk) for kk in keys_bf]
        mx = functools.reduce(jnp.maximum, [jnp.max(si, axis=-1, keepdims=True) for si in s])
        acc = None
        for si, ve in zip(s, vals_ext):
            e = jnp.exp((si - mx).astype(BF16))
            pv = jnp.dot(e, ve, preferred_element_type=F32)
            acc = pv if acc is None else acc + pv
        outs.append(acc[:, :B_DV] / acc[:, B_DV:])
    return outs[0] - lam * outs[1]


def _subln(o, g, lam_init):
    return o * lax.rsqrt(jnp.mean(o * o, axis=-1, keepdims=True) + EPS) * g * (1.0 - lam_init)


def _attn_ctx_kernel(lam_ref, *refs, lam_init, layer):
    q_refs, k_refs, v_refs = refs[:B_HEADS], refs[B_HEADS:2 * B_HEADS], refs[2 * B_HEADS:3 * B_HEADS]
    gq_ref, gk_ref, gs_ref = refs[3 * B_HEADS:3 * B_HEADS + 3]
    rest = refs[3 * B_HEADS + 3:]
    if layer:
        pk_ref, pv_ref, o_ref, nk_ref, nv_ref = rest
        nk_ref[0, :layer] = pk_ref[0]
        nv_ref[0, :layer] = pv_ref[0]
    else:
        o_ref, nk_ref, nv_ref = rest
    lam = lam_ref[0, 0]
    for h in range(B_HEADS):
        qn = _half_rms(q_refs[h][...], gq_ref[...]) * (B_DK ** -0.5)
        kn = _half_rms(k_refs[h][...], gk_ref[...])
        v = v_refs[h][...]
        nk_ref[0, layer, 0, h] = kn[:, :B_DK]
        nk_ref[0, layer, 1, h] = kn[:, B_DK:]
        nv_ref[0, layer, h] = v
        o = _diff_softmax_pv(qn.astype(BF16), [kn.astype(BF16)], [_with_ones(v.astype(BF16))], lam)
        o_ref[:, h * LANES:(h + 1) * LANES] = _subln(o, gs_ref[...], lam_init)


def _attn_ctx(proj, lam, gq2, gk2, gs, lam_init, layer, prev_k, prev_v):
    qcol, kcol, vcol = 5 * A_WIDTH // LANES, 5 * A_WIDTH // LANES + 4, 5 * A_WIDTH // LANES + 8
    prev_specs, prev_args = [], []
    if layer:
        prev_specs = [pl.BlockSpec((1, layer, 2, B_HEADS, SEQ, B_DK), lambda b: (b, 0, 0, 0, 0, 0)),
                      pl.BlockSpec((1, layer, B_HEADS, SEQ, B_DV), lambda b: (b, 0, 0, 0, 0))]
        prev_args = [prev_k, prev_v]
    n_l = layer + 1
    head_specs = [pl.BlockSpec((SEQ, LANES), functools.partial(lambda b, col: (b, col), col=c0 + h))
                  for c0 in (qcol, kcol, vcol) for h in range(B_HEADS)]
    return pl.pallas_call(
        functools.partial(_attn_ctx_kernel, lam_init=lam_init, layer=layer),
        grid=(BATCH,),
        in_specs=[pl.BlockSpec(memory_space=pltpu.SMEM)] + head_specs + [
            pl.BlockSpec((1, LANES), lambda b: (0, 0)),
            pl.BlockSpec((1, LANES), lambda b: (0, 0)),
            pl.BlockSpec((1, LANES), lambda b: (0, 0)),
        ] + prev_specs,
        out_specs=[
            pl.BlockSpec((SEQ, B_WIDTH), lambda b: (b, 0)),
            pl.BlockSpec((1, n_l, 2, B_HEADS, SEQ, B_DK), lambda b: (b, 0, 0, 0, 0, 0)),
            pl.BlockSpec((1, n_l, B_HEADS, SEQ, B_DV), lambda b: (b, 0, 0, 0, 0)),
        ],
        out_shape=[
            jax.ShapeDtypeStruct((T_CTX, B_WIDTH), F32),
            jax.ShapeDtypeStruct((BATCH, n_l, 2, B_HEADS, SEQ, B_DK), F32),
            jax.ShapeDtypeStruct((BATCH, n_l, B_HEADS, SEQ, B_DV), F32),
        ],
        compiler_params=_cparams(("arbitrary",)),
        name="diff_attention_ctx",
    )(lam, *([proj] * (3 * B_HEADS)), gq2, gk2, gs, *prev_args)


ATT_TQ = 256
ATT_HEADS = 4


def _attn_smp_kernel(lam_ref, *refs, lam_init):
    nh = ATT_HEADS
    q_refs, k_refs, v_refs = refs[:nh], refs[nh:2 * nh], refs[2 * nh:3 * nh]
    ck_refs, cv_refs = refs[3 * nh:4 * nh], refs[4 * nh:5 * nh]
    cos_ref, sin_ref, gq_ref, gk_ref, gs_ref, o_ref, qs_ref, ks_ref = refs[5 * nh:]
    lam = lam_ref[0, 0]
    cos = cos_ref[...]
    sin = sin_ref[...]
    g = gs_ref[...]
    prepared = []
    for h in range(nh):
        qn = _rope(_half_rms(q_refs[h][...], gq_ref[...]), cos, sin) * (B_DK ** -0.5)
        qs_ref[h] = qn.astype(BF16)
        ks_ref[h] = _rope(_half_rms(k_refs[h][...], gk_ref[...]), cos, sin).astype(BF16)
        ck = jnp.concatenate([ck_refs[h][0, 0, 0, 0], ck_refs[h][0, 0, 1, 0]], axis=-1).astype(BF16)
        cv = _with_ones(cv_refs[h][0, 0, 0].astype(BF16))
        v_bf = _with_ones(v_refs[h][...].astype(BF16))
        prepared.append((ck, cv, v_bf))
    for h in range(nh):
        ck, cv, v_bf = prepared[h]
        k_bf = ks_ref[h]
        for i in range(DEC_SEQ // ATT_TQ):
            rows = slice(i * ATT_TQ, (i + 1) * ATT_TQ)
            o = _diff_softmax_pv(qs_ref[h, rows, :], [k_bf, ck], [v_bf, cv], lam)
            o_ref[rows, h * LANES:(h + 1) * LANES] = _subln(o, g, lam_init)


def _attn_smp(proj, lam, cache_k, cache_v, cos, sin, gq2, gk2, gs, layer, lam_init):
    qcol, kcol, vcol = 5 * A_WIDTH // LANES, 5 * A_WIDTH // LANES + 4, 5 * A_WIDTH // LANES + 8
    r0 = T_CTX // DEC_SEQ
    nh = ATT_HEADS

    def per_head(shape, index):
        return [pl.BlockSpec(shape, functools.partial(index, dh=dh)) for dh in range(nh)]

    head_specs = (
        per_head((DEC_SEQ, LANES), lambda b, hp, dh: (r0 + b, qcol + hp * nh + dh))
        + per_head((DEC_SEQ, LANES), lambda b, hp, dh: (r0 + b, kcol + hp * nh + dh))
        + per_head((DEC_SEQ, LANES), lambda b, hp, dh: (r0 + b, vcol + hp * nh + dh))
        + per_head((1, 1, 2, 1, PAST, B_DK), lambda b, hp, dh: (b, layer, 0, hp * nh + dh, 0, 0))
        + per_head((1, 1, 1, PAST, B_DV), lambda b, hp, dh: (b, layer, hp * nh + dh, 0, 0)))
    return pl.pallas_call(
        functools.partial(_attn_smp_kernel, lam_init=lam_init),
        grid=(DEC_BATCH, B_HEADS // nh),
        in_specs=[pl.BlockSpec(memory_space=pltpu.SMEM)] + head_specs + [
            pl.BlockSpec((DEC_SEQ, LANES), lambda b, hp: (0, 0)),
            pl.BlockSpec((DEC_SEQ, LANES), lambda b, hp: (0, 0)),
            pl.BlockSpec((1, LANES), lambda b, hp: (0, 0)),
            pl.BlockSpec((1, LANES), lambda b, hp: (0, 0)),
            pl.BlockSpec((1, LANES), lambda b, hp: (0, 0)),
        ],
        out_specs=pl.BlockSpec((DEC_SEQ, nh * LANES), lambda b, hp: (b, hp)),
        out_shape=jax.ShapeDtypeStruct((T_SMP, B_WIDTH), F32),
        scratch_shapes=[pltpu.VMEM((nh, DEC_SEQ, LANES), BF16), pltpu.VMEM((nh, DEC_SEQ, LANES), BF16)],
        compiler_params=_cparams(("arbitrary", "arbitrary")),
        name="diff_attention_smp",
    )(lam, *([proj] * (3 * nh)), *([cache_k] * nh), *([cache_v] * nh), cos, sin, gq2, gk2, gs)


def _rope_tables():
    n_rows = DEC_SEQ // GRID_W
    row = np.repeat(np.arange(n_rows), GRID_W).astype(np.float32)
    col = np.tile(np.arange(GRID_W), n_rows).astype(np.float32)
    half = B_DK // 2
    inv_freq = (ROPE_BASE ** (-jnp.arange(0, half, 2, dtype=F32) / half))
    row_ang = jnp.asarray(row)[:, None] * inv_freq
    col_ang = jnp.asarray(col)[:, None] * inv_freq
    ang = jnp.concatenate([row_ang, row_ang, col_ang, col_ang], axis=-1)
    ang = jnp.concatenate([ang, ang], axis=-1)
    sign = np.where((np.arange(LANES) % 32) < 16, -1.0, 1.0).astype(np.float32)
    return jnp.cos(ang), jnp.sin(ang) * sign


CM_ROWS = 1024


def _gelu(x):
    return 0.5 * x * (1.0 + lax.erf(x * (2.0 ** -0.5)))


def _cmlp_kernel(u_ref, v_ref, g_ref, b_ref, ws_ref, bs_ref, o_ref):
    u = _gelu(u_ref[...])
    gv = _gelu(v_ref[...])
    mu = jnp.mean(gv, axis=-1, keepdims=True)
    dv = gv - mu
    var = jnp.mean(dv * dv, axis=-1, keepdims=True)
    vn = (dv * lax.rsqrt(var + EPS) * g_ref[...] + b_ref[...]).astype(BF16)
    lane = _lane((C_CHUNK, LANES))
    for c in range(CM_ROWS // C_CHUNK):
        rs = slice(c * C_CHUNK, (c + 1) * C_CHUNK)
        for p in range(2):
            cs = slice(p * LANES, (p + 1) * LANES)
            vp = vn[rs, cs]
            m0 = jnp.dot(ws_ref[2 * p].astype(BF16), vp, preferred_element_type=F32)
            m1 = jnp.dot(ws_ref[2 * p + 1].astype(BF16), vp, preferred_element_type=F32)
            mixed = jnp.where(lane < C_DG, m0, m1) + bs_ref[:, cs]
            o_ref[rs, cs] = u[rs, cs] * mixed


def _chunk_mlp(proj, ln_g, ln_b, w_s, bias_full):
    ucol = (5 * A_WIDTH + 3 * B_WIDTH) // C_WIDTH
    return pl.pallas_call(
        _cmlp_kernel,
        grid=(T // CM_ROWS,),
        in_specs=[
            pl.BlockSpec((CM_ROWS, C_WIDTH), lambda i: (i, ucol)),
            pl.BlockSpec((CM_ROWS, C_WIDTH), lambda i: (i, ucol + 1)),
            pl.BlockSpec((1, C_WIDTH), lambda i: (0, 0)),
            pl.BlockSpec((1, C_WIDTH), lambda i: (0, 0)),
            pl.BlockSpec((C_GROUPS, C_CHUNK, C_CHUNK), lambda i: (0, 0, 0)),
            pl.BlockSpec((C_CHUNK, C_WIDTH), lambda i: (0, 0)),
        ],
        out_specs=pl.BlockSpec((CM_ROWS, C_WIDTH), lambda i: (i, 0)),
        out_shape=jax.ShapeDtypeStruct((T, C_WIDTH), F32),
        compiler_params=_cparams(("arbitrary",)),
        name="chunk_mlp",
    )(proj, proj, ln_g.reshape(1, C_WIDTH), ln_b.reshape(1, C_WIDTH), w_s, bias_full)


PM_ROWS = 512
PM_SUB = TM


def _postmix_kernel(*refs, n_x):
    (of_ref, ob_ref, ag_ref, hg_ref, hsel_ref, bc_ref, bs_ref, c_ref, w_ref, gate1_ref, shift2_ref, scale2_ref,
     g2_ref, wrh_ref, wrl_ref, br_ref, x1_ref, h2_ref, idx_ref, gw_ref) = refs[n_x:]
    is_ctx = pl.program_id(0) < T_CTX // PM_ROWS
    x_refs = refs[:n_x]
    for s in range(PM_ROWS // PM_SUB):
        rows = slice(s * PM_SUB, (s + 1) * PM_SUB)
        x = x_refs[0][rows, :] if n_x == 1 else jnp.where(is_ctx, x_refs[0][rows, :], x_refs[1][rows, :])
        o = of_ref[rows, :] + ob_ref[rows, :]
        ms = _dot_sel(o * o, hsel_ref[...]) * (1.0 / A_DK)
        ag = ag_ref[rows, :]
        a = o * lax.rsqrt(ms + EPS) * hg_ref[...] * (ag * jax.nn.sigmoid(ag))
        b = jnp.where(is_ctx, bc_ref[rows, :], bs_ref[rows, :])
        mixed = jnp.dot(a.astype(BF16), w_ref[0:A_WIDTH, :], preferred_element_type=F32)
        mixed = mixed + jnp.dot(b.astype(BF16), w_ref[A_WIDTH:A_WIDTH + B_WIDTH, :], preferred_element_type=F32)
        mixed = mixed + jnp.dot(c_ref[rows, :].astype(BF16), w_ref[A_WIDTH + B_WIDTH:, :],
                                preferred_element_type=F32)
        x1 = x + gate1_ref[0] * mixed
        x1_ref[rows, :] = x1
        y = x1 * lax.rsqrt(jnp.mean(x1 * x1, axis=-1, keepdims=True) + EPS) * g2_ref[...]
        h2 = y * (1.0 + scale2_ref[0]) + shift2_ref[0]
        h2_ref[rows] = _pack_rows(h2)
        hi = h2.astype(BF16)
        lo = (h2 - hi.astype(F32)).astype(BF16)
        lg = jnp.dot(lo, wrh_ref[...], preferred_element_type=F32)
        lg = lg + jnp.dot(hi, wrl_ref[...], preferred_element_type=F32)
        lg = lg + jnp.dot(hi, wrh_ref[...], preferred_element_type=F32) + br_ref[...]
        lt = lg.T[:N_EXPERTS]
        row = lax.broadcasted_iota(jnp.int32, lt.shape, 0)
        out_row = lax.broadcasted_iota(jnp.int32, (8, PM_SUB), 0)
        idx_out = jnp.zeros((8, PM_SUB), jnp.int32)
        val_out = jnp.zeros((8, PM_SUB), F32)
        top0 = None
        den = None
        for kk in range(TOP_K):
            mx = jnp.max(lt, axis=0, keepdims=True)
            am = jnp.min(jnp.where(lt == mx, row, N_EXPERTS), axis=0, keepdims=True)
            if kk == 0:
                top0 = mx
            e = jnp.exp(mx - top0)
            den = e if den is None else den + e
            idx_out = jnp.where(out_row == kk, am, idx_out)
            val_out = jnp.where(out_row == kk, e, val_out)
            lt = jnp.where(row == am, -jnp.inf, lt)
        idx_ref[:, rows] = idx_out
        gw_ref[:, rows] = val_out / den


def _post_mix(o_f, o_b, proj, hg, hsel, b_ctx, b_smp, c_out, w_out_bf, x, mod3, g2, wr_hi, wr_lo, br_pad):
    tile = lambda w: pl.BlockSpec((PM_ROWS, w), lambda i: (i, 0))
    const = lambda shape: pl.BlockSpec(shape, lambda i: tuple(0 for _ in shape))
    modspec = lambda j: pl.BlockSpec((1, 1, D), lambda i: (_mod_row(i, PM_ROWS), 0, j))
    rowsT = pl.BlockSpec((8, PM_ROWS), lambda i: (0, i))
    x_specs, x_args = _stream_specs(x, PM_ROWS)
    ctx_tiles = T_CTX // PM_ROWS
    return pl.pallas_call(
        functools.partial(_postmix_kernel, n_x=len(x_args)),
        grid=(T // PM_ROWS,),
        in_specs=x_specs + [
            tile(A_WIDTH), tile(A_WIDTH),
            pl.BlockSpec((PM_ROWS, A_WIDTH), lambda i: (i, 4)),
            const((1, A_WIDTH)), const((A_WIDTH, A_WIDTH)),
            pl.BlockSpec((PM_ROWS, B_WIDTH), lambda i: (jnp.minimum(i, ctx_tiles - 1), 0)),
            pl.BlockSpec((PM_ROWS, B_WIDTH), lambda i: (jnp.maximum(i - ctx_tiles, 0), 0)),
            tile(C_WIDTH),
            const((D, D)),
            modspec(2), modspec(3), modspec(4),
            const((1, D)), const((D, LANES)), const((D, LANES)), const((1, LANES)),
        ],
        out_specs=[tile(D), pl.BlockSpec((PM_ROWS, ROW_TILES, LANES), lambda i: (i, 0, 0)), rowsT, rowsT],
        out_shape=[
            jax.ShapeDtypeStruct((T, D), F32),
            jax.ShapeDtypeStruct((T, ROW_TILES, LANES), ROW_DT),
            jax.ShapeDtypeStruct((8, T), jnp.int32),
            jax.ShapeDtypeStruct((8, T), F32),
        ],
        compiler_params=_cparams(("arbitrary",)),
        name="post_mix_router",
    )(*x_args, o_f, o_b, proj, hg, hsel, b_ctx, b_smp, c_out, w_out_bf, mod3, mod3, mod3, g2.reshape(1, D),
      wr_hi, wr_lo, br_pad)


def _route_kernel(idx_ref, dest_ref, meta_ref):
    erow = lax.broadcasted_iota(jnp.int32, (N_EXPERTS, TM), 0)
    s_i = lax.broadcasted_iota(jnp.int32, (TM, TM), 0)
    t_i = lax.broadcasted_iota(jnp.int32, (TM, TM), 1)
    earlier = (s_i < t_i).astype(BF16)
    out_row = lax.broadcasted_iota(jnp.int32, (8, TM), 0)

    def onehots(i):
        idx = idx_ref[:, pl.ds(pl.multiple_of(i * TM, TM), TM)]
        return [(erow == idx[kk:kk + 1, :]) for kk in range(TOP_K)]

    def count_tile(i, run):
        ohs = onehots(i)
        base = run
        pos = jnp.zeros((8, TM), F32)
        for kk in range(TOP_K):
            ohf = ohs[kk].astype(F32)
            before = jnp.dot(ohs[kk].astype(BF16), earlier, preferred_element_type=F32)
            p = jnp.sum(ohf * (base + before), axis=0, keepdims=True)
            pos = jnp.where(out_row == kk, p, pos)
            base = base + jnp.sum(ohf, axis=1, keepdims=True)
        dest_ref[:, pl.ds(pl.multiple_of(i * TM, TM), TM)] = pos.astype(jnp.int32)
        return base

    counts = lax.fori_loop(0, N_TILES, count_tile, jnp.zeros((N_EXPERTS, 1), F32)).astype(jnp.int32)
    bm_shift = MOE_BM.bit_length() - 1
    padded = lax.shift_left(lax.shift_right_logical(counts + (MOE_BM - 1), bm_shift), bm_shift)
    e_r = lax.broadcasted_iota(jnp.int32, (N_EXPERTS, N_EXPERTS), 0)
    e_c = lax.broadcasted_iota(jnp.int32, (N_EXPERTS, N_EXPERTS), 1)
    incl = (e_c <= e_r).astype(BF16)
    pad_end = _sel_dot(incl, jnp.broadcast_to(padded.astype(F32), (N_EXPERTS, LANES)))[:, :1]
    pad_start = pad_end - padded.astype(F32)

    def place_tile(i, carry):
        ohs = onehots(i)
        sl = pl.ds(pl.multiple_of(i * TM, TM), TM)
        off = jnp.zeros((8, TM), F32)
        for kk in range(TOP_K):
            o = jnp.sum(ohs[kk].astype(F32) * pad_start, axis=0, keepdims=True)
            off = jnp.where(out_row == kk, o, off)
        dest_ref[:, sl] = dest_ref[:, sl] + off.astype(jnp.int32)
        return carry

    lax.fori_loop(0, N_TILES, place_tile, 0)

    total = jnp.max(pad_end, axis=0, keepdims=True)
    lane_i = lax.broadcasted_iota(jnp.int32, (1, TM), 1)
    blk0 = (lane_i * MOE_BM).astype(F32)
    block_e = jnp.sum((pad_end <= blk0).astype(F32), axis=0, keepdims=True)
    live_end = pad_start + counts.astype(F32)
    sel = erow.astype(F32) == block_e
    live = jnp.sum(jnp.where(sel, live_end, 0.0), axis=0, keepdims=True)
    valid = jnp.where(blk0 < total, jnp.clip(live - blk0, 0.0, float(MOE_BM)), 0.0)
    own = erow == lane_i
    n_blk = jnp.sum(jnp.where(own, padded.astype(F32), 0.0), axis=0, keepdims=True) * (1.0 / MOE_BM)
    first_blk = jnp.sum(jnp.where(own, pad_start, 0.0), axis=0, keepdims=True) * (1.0 / MOE_BM)
    meta = jnp.where(out_row == 0, valid, 0.0)
    meta = jnp.where(out_row == 1, n_blk, meta)
    meta = jnp.where(out_row == 2, first_blk, meta)
    meta = jnp.where(out_row == 3, total * (1.0 / MOE_BM), meta)
    meta_ref[...] = meta.astype(jnp.int32)


def _route(idx_t):
    assert MOE_BLOCKS <= TM
    return pl.pallas_call(
        _route_kernel,
        out_shape=[jax.ShapeDtypeStruct((8, T), jnp.int32), jax.ShapeDtypeStruct((8, TM), jnp.int32)],
        compiler_params=pltpu.CompilerParams(vmem_limit_bytes=VMEM_LIMIT),
        name="moe_route",
    )(idx_t)


def _moe_kernel(bv_ref, nb_ref, g0_ref, tot_ref, x_hbm, wgu_ref, bgu_ref, wdn_ref, bdn_ref, y_hbm,
                wgu_bf, wdn_bf, xbuf, ybuf, xb_ref, xsem, ysem):
    e = pl.program_id(0)
    n_blk = nb_ref[e]
    first = g0_ref[e]
    total = tot_ref[0]
    ahead = MOE_RING - 1

    def x_copy(g):
        slot = g % MOE_RING
        return pltpu.make_async_copy(x_hbm.at[pl.ds(g * MOE_BM, MOE_BM)], xbuf.at[slot], xsem.at[slot])

    def y_copy(g):
        slot = g % MOE_RING
        return pltpu.make_async_copy(ybuf.at[slot], y_hbm.at[pl.ds(g * MOE_BM, MOE_BM)], ysem.at[slot])

    @pl.when(e == 0)
    def _():
        for g in range(ahead):
            @pl.when(g < total)
            def _():
                x_copy(g).start()

    @pl.when(n_blk > 0)
    def _():
        wgu_bf[...] = wgu_ref[0, 0].astype(BF16)
        wdn_bf[...] = wdn_ref[0, 0].astype(BF16)

    def block(j, carry):
        g = first + j
        slot = g % MOE_RING
        x_copy(g).wait()

        @pl.when(g + ahead < total)
        def _():
            x_copy(g + ahead).start()

        @pl.when(g >= MOE_RING)
        def _():
            y_copy(g - MOE_RING).wait()

        n_live = bv_ref[g]

        def ffn(n_rows):
            live = lax.broadcasted_iota(jnp.int32, (n_rows, LANES), 0) < n_live
            for c, chunk in enumerate(_unpack_rows(xbuf[slot, :n_rows])):
                xb_ref[:n_rows, c * LANES:(c + 1) * LANES] = jnp.where(live, chunk, 0.0).astype(BF16)
            gu = jnp.dot(xb_ref[:n_rows, :], wgu_bf[...], preferred_element_type=F32) + bgu_ref[0, 0]
            glu = jnp.minimum(gu[:, :D], SWIGLU_LIMIT)
            lin = jnp.clip(gu[:, D:], -SWIGLU_LIMIT, SWIGLU_LIMIT)
            act = glu * jax.nn.sigmoid(SWIGLU_ALPHA * glu) * (lin + 1.0)
            y = jnp.dot(act.astype(BF16), wdn_bf[...], preferred_element_type=F32) + bdn_ref[0, 0]
            ybuf[slot, :n_rows] = _pack_rows(y)

        @pl.when(n_live > MOE_BM // 2)
        def _():
            ffn(MOE_BM)

        @pl.when(n_live <= MOE_BM // 2)
        def _():
            ffn(MOE_BM // 2)
            ybuf[slot, MOE_BM // 2:] = jnp.zeros((MOE_BM // 2, ROW_TILES, LANES), ROW_DT)

        y_copy(g).start()
        return carry

    lax.fori_loop(0, n_blk, block, 0)

    @pl.when(e == N_EXPERTS - 1)
    def _():
        for back in range(MOE_RING, 0, -1):
            @pl.when(total >= back)
            def _():
                y_copy(total - back).wait()

        def fill(g, carry):
            ybuf[g % MOE_RING] = jnp.zeros((MOE_BM, ROW_TILES, LANES), ROW_DT)
            y_copy(g).start()
            y_copy(g).wait()
            return carry

        lax.fori_loop(total, MOE_BLOCKS, fill, 0)


def _moe_ffn(block_valid, n_blk, first_blk, total_blk, xs, w_gu, b_gu, w_dn, b_dn, layer):
    rows = (MOE_BM, ROW_TILES, LANES)
    return pl.pallas_call(
        _moe_kernel,
        grid_spec=pltpu.PrefetchScalarGridSpec(
            num_scalar_prefetch=4,
            grid=(N_EXPERTS,),
            in_specs=[
                pl.BlockSpec(memory_space=pl.ANY),
                pl.BlockSpec((1, 1, D, 2 * D), lambda e, *_: (layer, e, 0, 0)),
                pl.BlockSpec((1, 1, 1, 2 * D), lambda e, *_: (layer, e, 0, 0)),
                pl.BlockSpec((1, 1, D, D), lambda e, *_: (layer, e, 0, 0)),
                pl.BlockSpec((1, 1, 1, D), lambda e, *_: (layer, e, 0, 0)),
            ],
            out_specs=pl.BlockSpec(memory_space=pl.ANY),
            scratch_shapes=[
                pltpu.VMEM((D, 2 * D), BF16), pltpu.VMEM((D, D), BF16),
                pltpu.VMEM((MOE_RING,) + rows, ROW_DT), pltpu.VMEM((MOE_RING,) + rows, ROW_DT),
                pltpu.VMEM((MOE_BM, D), BF16),
                pltpu.SemaphoreType.DMA((MOE_RING,)), pltpu.SemaphoreType.DMA((MOE_RING,)),
            ],
        ),
        out_shape=jax.ShapeDtypeStruct((MOE_ROWS, ROW_TILES, LANES), ROW_DT),
        compiler_params=_cparams(("arbitrary",)),
        name="moe_expert_ffn",
    )(block_valid, n_blk, first_blk, total_blk, xs, w_gu, b_gu.reshape(DEPTH, N_EXPERTS, 1, 2 * D), w_dn,
      b_dn.reshape(DEPTH, N_EXPERTS, 1, D))


def _combine_kernel(x1_ref, y_ref, gw_ref, gate2_ref, o_ref):
    gw = jnp.concatenate([gw_ref[...], jnp.zeros((LANES - 8, TM), F32)], axis=0).T
    ys = [_unpack_rows(y_ref[kk]) for kk in range(TOP_K)]
    for c in range(D // LANES):
        cs = slice(c * LANES, (c + 1) * LANES)
        acc = None
        for kk in range(TOP_K):
            term = ys[kk][c] * gw[:, kk:kk + 1]
            acc = term if acc is None else acc + term
        o_ref[:, cs] = x1_ref[:, cs] + gate2_ref[0, :, cs] * acc


def _combine(x1, yg, gw, mod3, tile0, n_tiles):
    return pl.pallas_call(
        _combine_kernel,
        grid=(n_tiles,),
        in_specs=[
            pl.BlockSpec((TM, D), lambda i: (tile0 + i, 0)),
            pl.BlockSpec((TOP_K, TM, ROW_TILES, LANES), lambda i: (0, i, 0, 0)),
            pl.BlockSpec((8, TM), lambda i: (0, tile0 + i)),
            pl.BlockSpec((1, 1, D), lambda i: (_mod_row(tile0 + i), 0, 5)),
        ],
        out_specs=pl.BlockSpec((TM, D), lambda i: (i, 0)),
        out_shape=jax.ShapeDtypeStruct((n_tiles * TM, D), F32),
        compiler_params=_cparams(("arbitrary",)),
        name="moe_combine",
    )(x1, yg, gw, mod3)


def _sc_mesh():
    return plsc.VectorSubcoreMesh(core_axis_name="c", subcore_axis_name="s")


def _sc_worker():
    return lax.axis_index("s") * SC_CORES + lax.axis_index("c")


def _sc_dispatch(h2t, dest_km):
    per_w = T // SC_WORKERS

    @functools.partial(
        pl.kernel, mesh=_sc_mesh(),
        out_type=jax.ShapeDtypeStruct((MOE_ROWS, ROW_TILES, LANES), ROW_DT),
        scratch_types=[pltpu.VMEM((SC_WIN,), jnp.int32), pltpu.VMEM((SC_WIN, ROW_TILES, LANES), ROW_DT),
                       pltpu.SemaphoreType.DMA],
    )
    def run(h_hbm, d_hbm, o_hbm, idx_v, rows_v, sem):
        w0 = _sc_worker() * per_w

        @pl.loop(0, per_w // SC_WIN)
        def _(w):
            base = pl.multiple_of(w0 + w * SC_WIN, SC_WIN)
            pltpu.sync_copy(h_hbm.at[pl.ds(base, SC_WIN)], rows_v)
            for kk in range(TOP_K):
                pltpu.sync_copy(d_hbm.at[pl.ds(kk * T + base, SC_WIN)], idx_v)
                pltpu.async_copy(rows_v, o_hbm.at[idx_v], sem).wait()

    return run(h2t, dest_km)


def _sc_gather(yb, dest):
    n = dest.shape[0]
    per_w = n // SC_WORKERS

    @functools.partial(
        pl.kernel, mesh=_sc_mesh(),
        out_type=jax.ShapeDtypeStruct((n, ROW_TILES, LANES), ROW_DT),
        scratch_types=[pltpu.VMEM((SC_WIN,), jnp.int32), pltpu.VMEM((SC_WIN, ROW_TILES, LANES), ROW_DT),
                       pltpu.SemaphoreType.DMA],
    )
    def run(y_hbm, d_hbm, o_hbm, idx_v, rows_v, sem):
        w0 = _sc_worker() * per_w

        @pl.loop(0, per_w // SC_WIN)
        def _(w):
            base = pl.multiple_of(w0 + w * SC_WIN, SC_WIN)
            pltpu.sync_copy(d_hbm.at[pl.ds(base, SC_WIN)], idx_v)
            pltpu.async_copy(y_hbm.at[idx_v], rows_v, sem).wait()
            pltpu.sync_copy(rows_v, o_hbm.at[pl.ds(base, SC_WIN)])

    return run(yb, dest)


def kernel(x_prompt, x_sample, c, cache_diff_k, cache_diff_v, state_hgrn, c_ctx, norm_mix_g, norm_ffn_g, w_mod, b_mod, w_in, w_out, hgrn_lower_bounds, hgrn_norm_g, diff_q_norm_g, diff_k_norm_g, diff_lambda_q1, diff_lambda_k1, diff_lambda_q2, diff_lambda_k2, diff_subln_g, cmlp_ln_g, cmlp_ln_b, cmlp_w_s, cmlp_b_s, router_w, router_b, moe_w_gate_up, moe_b_gate_up, moe_w_down, moe_b_down):
    x = (x_prompt.reshape(T_CTX, D), x_sample.reshape(T_SMP, D))
    cvec = jnp.concatenate([c_ctx[None, :], c, jnp.zeros((MOD_ROWS - 1 - DEC_BATCH, D), F32)], axis=0)
    mod = _modulation(cvec, w_mod, b_mod)

    lvl_np, tri_np = _hgrn_tables()
    lvl = jnp.asarray(lvl_np)
    tri = jnp.asarray(tri_np, dtype=BF16)
    cos, sin = _rope_tables()
    hsel = jnp.asarray(np.kron(np.eye(A_HEADS), np.ones((A_DK, A_DK))), dtype=BF16)
    sm = jax.nn.softmax(hgrn_lower_bounds.astype(F32), axis=0)
    lb_all = jnp.cumsum(sm, axis=0) - sm[0]

    new_k, new_v, new_s = None, None, []
    for l in range(DEPTH):
        mod3 = mod[l].reshape(MOD_ROWS, 1, 6 * D)
        proj = _in_projection(x, norm_mix_g[l], mod3, w_in[l].astype(BF16))

        s0 = jnp.concatenate([jnp.zeros((BATCH, 2, A_HEADS, A_DK, A_DK), F32), state_hgrn[:, l]], axis=0)
        o_f, fin_f, o_b, fin_b = _hgrn_scan(proj, [lb_all[l, d].reshape(1, A_WIDTH) for d in range(2)],
                                            [_pack_state(s0[:, d]) for d in range(2)], lvl, tri, l)
        o_dir = [o_f, o_b]
        new_s.append(jnp.stack([_unpack_state(fin_f[:BATCH]), _unpack_state(fin_b[:BATCH])], axis=1))

        lam_init = 0.8 - 0.6 * math.exp(-0.3 * l)
        lam = (jnp.exp(jnp.sum(diff_lambda_q1[l] * diff_lambda_k1[l]))
               - jnp.exp(jnp.sum(diff_lambda_q2[l] * diff_lambda_k2[l])) + lam_init).reshape(1, 1)
        gq2 = jnp.tile(diff_q_norm_g[l], 2).reshape(1, LANES)
        gk2 = jnp.tile(diff_k_norm_g[l], 2).reshape(1, LANES)
        gs = diff_subln_g[l].reshape(1, LANES)
        b_ctx, new_k, new_v = _attn_ctx(proj, lam, gq2, gk2, gs, lam_init, l, new_k, new_v)
        b_smp = _attn_smp(proj, lam, cache_diff_k, cache_diff_v, cos, sin, gq2, gk2, gs, l, lam_init)

        bias_full = jnp.repeat(cmlp_b_s[l].T, C_DG, axis=1)
        c_out = _chunk_mlp(proj, cmlp_ln_g[l], cmlp_ln_b[l], cmlp_w_s[l], bias_full)

        hg = jnp.tile(hgrn_norm_g[l], A_HEADS).reshape(1, A_WIDTH)
        wr_pad = jnp.pad(router_w[l], ((0, 0), (0, LANES - N_EXPERTS)))
        wr_hi = wr_pad.astype(BF16)
        wr_lo = (wr_pad - wr_hi.astype(F32)).astype(BF16)
        br_pad = jnp.pad(router_b[l], (0, LANES - N_EXPERTS)).reshape(1, LANES)
        x1, h2, idx_t, gw_t = _post_mix(o_dir[0], o_dir[1], proj, hg, hsel, b_ctx, b_smp, c_out,
                                        w_out[l].astype(BF16), x, mod3, norm_ffn_g[l], wr_hi, wr_lo, br_pad)

        dest_t, meta = _route(idx_t)
        dest_km = dest_t[:TOP_K].reshape(-1)
        xs = _sc_dispatch(h2, dest_km)
        yb = _moe_ffn(meta[0, :MOE_BLOCKS], meta[1, :N_EXPERTS], meta[2, :N_EXPERTS], meta[3, :1], xs,
                      moe_w_gate_up, moe_b_gate_up, moe_w_down, moe_b_down, l)
        yg_ctx = _sc_gather(yb, dest_t[:TOP_K, :T_CTX].reshape(-1)).reshape(TOP_K, T_CTX, ROW_TILES, LANES)
        yg_smp = _sc_gather(yb, dest_t[:TOP_K, T_CTX:].reshape(-1)).reshape(TOP_K, T_SMP, ROW_TILES, LANES)
        x = (_combine(x1, yg_ctx, gw_t, mod3, 0, CTX_TILES),
             _combine(x1, yg_smp, gw_t, mod3, CTX_TILES, N_TILES - CTX_TILES))

    y_prompt = x[0].reshape(BATCH, SEQ, D)
    y_sample = x[1].reshape(DEC_BATCH, DEC_SEQ, D)
    return (y_prompt, y_sample, new_k, new_v, jnp.stack(new_s, axis=1))
```

```python
import functools
import math

import numpy as np
import jax
import jax.numpy as jnp
from jax import lax
from jax.experimental import pallas as pl
from jax.experimental.pallas import tpu as pltpu
from jax.experimental.pallas import tpu_sc as plsc

F32 = jnp.float32
BF16 = jnp.bfloat16

D = 1024
DEPTH = 2
BATCH, SEQ = 16, 256
DEC_BATCH, DEC_SEQ = 8, 1024
PAST = 512
GRID_W = 64
A_HEADS, A_DK = 4, 64
A_WIDTH = 256
B_HEADS, B_DK, B_DV = 4, 64, 128
B_WIDTH = 512
C_GROUPS, C_CHUNK, C_WIDTH, C_DG = 4, 128, 256, 64
IN_WIDTH = 5 * A_WIDTH + 3 * B_WIDTH + 2 * C_WIDTH
N_EXPERTS, TOP_K = 32, 4
SWIGLU_LIMIT, SWIGLU_ALPHA = 7.0, 1.702
ROPE_BASE = 10000.0
EPS = 1e-6

T_CTX = BATCH * SEQ
T_SMP = DEC_BATCH * DEC_SEQ
T = T_CTX + T_SMP
N_SEQ = BATCH + DEC_BATCH
MOD_ROWS = 16

TM = 256
N_TILES = T // TM
CTX_TILES = T_CTX // TM
SMP_TILES_PER_SEQ = DEC_SEQ // TM
LANES = 128
MOE_BM = 256
MOE_ROWS = T * TOP_K + N_EXPERTS * MOE_BM
MOE_BLOCKS = MOE_ROWS // MOE_BM
MOE_RING = 4
ROW_WORDS = D // 2
ROW_TILES = ROW_WORDS // LANES
ROW_DT = jnp.int32
SC_CORES, SC_SUBCORES = 2, 16
SC_WORKERS = SC_CORES * SC_SUBCORES
SC_WIN = 128
VMEM_LIMIT = 56 * 1024 * 1024


def _cparams(sem):
    return pltpu.CompilerParams(dimension_semantics=sem, vmem_limit_bytes=VMEM_LIMIT)


def _mod_row(i, rows=TM):
    ctx_tiles = T_CTX // rows
    return jnp.where(i < ctx_tiles, 0, 1 + (i - ctx_tiles) // (DEC_SEQ // rows))


def _split3(x):
    hi = x.astype(BF16)
    r = x - hi.astype(F32)
    mid = r.astype(BF16)
    lo = (r - mid.astype(F32)).astype(BF16)
    return hi, mid, lo


def _sel_dot(sel, x):
    hi, mid, lo = _split3(x)
    acc = jnp.dot(sel, lo, preferred_element_type=F32)
    acc = acc + jnp.dot(sel, mid, preferred_element_type=F32)
    return acc + jnp.dot(sel, hi, preferred_element_type=F32)


def _dot_sel(x, sel):
    hi, mid, lo = _split3(x)
    acc = jnp.dot(lo, sel, preferred_element_type=F32)
    acc = acc + jnp.dot(mid, sel, preferred_element_type=F32)
    return acc + jnp.dot(hi, sel, preferred_element_type=F32)


def _dot_nt(a, b):
    return lax.dot_general(a, b, (((1,), (1,)), ((), ())), preferred_element_type=F32)


def _dot_tn(a, b):
    return lax.dot_general(a, b, (((0,), (0,)), ((), ())), preferred_element_type=F32)


def _lane(shape):
    return lax.broadcasted_iota(jnp.int32, shape, len(shape) - 1)


def _pack_rows(x):
    hi = lax.bitcast_convert_type(x[:, :ROW_WORDS].astype(BF16).astype(F32), jnp.int32)
    lo = lax.bitcast_convert_type(x[:, ROW_WORDS:].astype(BF16).astype(F32), jnp.int32)
    words = hi | lax.shift_right_logical(lo, 16)
    return pltpu.einshape("t(jl)->tjl", words, l=LANES)


def _unpack_rows(words3):
    wt = pltpu.einshape("tjl->jtl", words3)
    hi = [lax.bitcast_convert_type(wt[j] & jnp.int32(-65536), F32) for j in range(ROW_TILES)]
    lo = [lax.bitcast_convert_type(lax.shift_left(wt[j], 16), F32) for j in range(ROW_TILES)]
    return hi + lo


def _mod_kernel(c_ref, w_ref, b_ref, o_ref):
    c = c_ref[...]
    s = c * jax.nn.sigmoid(c)
    o_ref[0] = jnp.dot(s.astype(BF16), w_ref[0].astype(BF16), preferred_element_type=F32) + b_ref[0]


def _modulation(cvec, w_mod, b_mod):
    tn = 1536
    return pl.pallas_call(
        _mod_kernel,
        grid=(DEPTH, 6 * D // tn),
        in_specs=[
            pl.BlockSpec((MOD_ROWS, D), lambda l, j: (0, 0)),
            pl.BlockSpec((1, D, tn), lambda l, j: (l, 0, j)),
            pl.BlockSpec((1, 1, tn), lambda l, j: (l, 0, j)),
        ],
        out_specs=pl.BlockSpec((1, MOD_ROWS, tn), lambda l, j: (l, 0, j)),
        out_shape=jax.ShapeDtypeStruct((DEPTH, MOD_ROWS, 6 * D), F32),
        compiler_params=_cparams(("arbitrary", "arbitrary")),
        name="modulation",
    )(cvec, w_mod, b_mod.reshape(DEPTH, 1, 6 * D))


def _stream_specs(x, rows=TM):
    ctx_tiles = T_CTX // rows
    if isinstance(x, tuple):
        return [pl.BlockSpec((rows, D), lambda i: (jnp.minimum(i, ctx_tiles - 1), 0)),
                pl.BlockSpec((rows, D), lambda i: (jnp.maximum(i - ctx_tiles, 0), 0))], list(x)
    return [pl.BlockSpec((rows, D), lambda i: (i, 0))], [x]


IN_ROWS = 512


def _inproj_kernel(*refs, n_x):
    g_ref, shift_ref, scale_ref, w_ref, o_ref = refs[n_x:]
    is_ctx = pl.program_id(0) < T_CTX // IN_ROWS
    for s in range(IN_ROWS // TM):
        rows = slice(s * TM, (s + 1) * TM)
        x = refs[0][rows, :] if n_x == 1 else jnp.where(is_ctx, refs[0][rows, :], refs[1][rows, :])
        y = x * lax.rsqrt(jnp.mean(x * x, axis=-1, keepdims=True) + EPS) * g_ref[...]
        h = y * (1.0 + scale_ref[0]) + shift_ref[0]
        o_ref[rows, :] = jnp.dot(h.astype(BF16), w_ref[...], preferred_element_type=F32)


def _in_projection(x, g, mod3, w_in_bf):
    x_specs, x_args = _stream_specs(x, IN_ROWS)
    mod_row = functools.partial(_mod_row, rows=IN_ROWS)
    return pl.pallas_call(
        functools.partial(_inproj_kernel, n_x=len(x_args)),
        grid=(T // IN_ROWS,),
        in_specs=x_specs + [
            pl.BlockSpec((1, D), lambda i: (0, 0)),
            pl.BlockSpec((1, 1, D), lambda i: (mod_row(i), 0, 0)),
            pl.BlockSpec((1, 1, D), lambda i: (mod_row(i), 0, 1)),
            pl.BlockSpec((D, IN_WIDTH), lambda i: (0, 0)),
        ],
        out_specs=pl.BlockSpec((IN_ROWS, IN_WIDTH), lambda i: (i, 0)),
        out_shape=jax.ShapeDtypeStruct((T, IN_WIDTH), F32),
        compiler_params=_cparams(("arbitrary",)),
        name="in_projection",
    )(*x_args, g.reshape(1, D), mod3, mod3, w_in_bf)


HG_C = 128
HG_LEVELS = tuple(2 ** j for j in range(1, int(math.log2(HG_C)) + 1))


def _hgrn_tables():
    t = np.arange(HG_C)[:, None]
    s = np.arange(HG_C)[None, :]
    x = t ^ s
    lvl = np.zeros((HG_C, HG_C), np.int32)
    nz = x > 0
    lvl[nz] = np.floor(np.log2(x[nz])).astype(np.int32) + 1
    fwd = np.where(t >= s, lvl, -1).astype(np.int32)
    bwd = np.where(t <= s, lvl, -1).astype(np.int32)
    tri_f = (t >= s).astype(np.float32)
    tri_b = (t <= s).astype(np.float32)
    return np.stack([fwd, bwd]), np.stack([tri_f, tri_b])


def _block_ref(cum, m, idx):
    c, l = cum.shape
    if m >= 16:
        c3 = cum.reshape(c // m, m, l)
        r = c3[:, idx:idx + 1, :]
        return jnp.broadcast_to(r, (c // m, m, l)).reshape(c, l)
    c3 = cum.reshape(c // 8, 8, l)
    sub = lax.broadcasted_iota(jnp.int32, c3.shape, 1)
    out = None
    for j in range(8 // m - 1, -1, -1):
        cand = jnp.broadcast_to(c3[:, j * m + idx:j * m + idx + 1, :], c3.shape)
        out = cand if out is None else jnp.where(sub < (j + 1) * m, cand, out)
    return out.reshape(c, l)


def _hgrn_kernel(*refs, layer):
    n_in = 7
    fwd_in, bwd_in = refs[:n_in], refs[n_in:2 * n_in]
    o_f, fin_f, o_b, fin_b, st_f, st_b = refs[2 * n_in:]
    runs = [_hgrn_direction(*fwd_in, o_f, fin_f, st_f, layer=layer, rev=False),
            _hgrn_direction(*bwd_in, o_b, fin_b, st_b, layer=layer, rev=True)]
    for steps in zip(*runs):
        for step in steps:
            step()


def _hgrn_direction(q_ref, z_ref, v_ref, lb_ref, s0_ref, lvl_ref, tri_ref, o_ref, fin_ref, st_ref, *, layer, rev):
    g = pl.program_id(0)
    first = jnp.logical_or(g < CTX_TILES, (g - CTX_TILES) % SMP_TILES_PER_SEQ == 0)

    @pl.when(first)
    def _():
        st_ref[...] = s0_ref[0]

    qr = q_ref[...]
    q = qr * jax.nn.sigmoid(qr) * (A_DK ** -0.5)
    z = z_ref[...]
    if layer == 0:
        lf = jnp.minimum(z, 0.0) - jnp.log(1.0 + jnp.exp(-jnp.abs(z)))
        k = jax.nn.sigmoid(-z)
    else:
        lbd = lb_ref[...]
        lf = jnp.log(lbd + (1.0 - lbd) * jax.nn.sigmoid(z))
        k = (1.0 - lbd) * jax.nn.sigmoid(-z)
    v = v_ref[...]
    tri = tri_ref[0]
    lvl = lvl_ref[0]
    last_row = 0 if rev else HG_C - 1
    n_chunks = TM // HG_C
    order = range(n_chunks - 1, -1, -1) if rev else range(n_chunks)
    lf2 = lf * math.log2(math.e)
    cums = [_sel_dot(tri, lf2[c * HG_C:(c + 1) * HG_C]) for c in range(n_chunks)]
    lane = _lane((HG_C, LANES))
    head_masks = (lane < A_DK, lane >= A_DK)
    lane_row = _lane((1, LANES))
    head_keep = ((lane_row < A_DK).astype(BF16), (lane_row >= A_DK).astype(BF16))
    lvl2 = jnp.concatenate([lvl, lvl], axis=0)
    level_masks = [lvl2 == i for i in range(len(HG_LEVELS) + 1)]
    r = lax.broadcasted_iota(jnp.int32, (LANES, LANES), 0)
    cl = lax.broadcasted_iota(jnp.int32, (LANES, LANES), 1)
    same_head = (r < A_DK) == (cl < A_DK)

    def chunk(q_p, k_p, v_p, cum_p, st):
        v_bf = v_p.astype(BF16)
        k_bf = k_p.astype(BF16)
        q_bf = q_p.astype(BF16)

        def both_heads(x_bf):
            return jnp.concatenate([x_bf * head_keep[0], x_bf * head_keep[1]], axis=0)

        scores = jnp.where(level_masks[0], _dot_nt(both_heads(q_bf), k_bf), 0.0)
        for li, m in enumerate(HG_LEVELS):
            ref = _block_ref(cum_p, m, m // 2 if rev else m // 2 - 1)
            dec = jnp.exp2(-jnp.abs(cum_p - ref))
            qd = (q_p * dec).astype(BF16)
            kd = (k_p * dec).astype(BF16)
            scores = jnp.where(level_masks[li + 1], _dot_nt(both_heads(qd), kd), scores)
        pv = jnp.dot(scores.astype(BF16), v_bf, preferred_element_type=F32)
        o_intra = jnp.where(head_masks[0], pv[:HG_C], pv[HG_C:])
        q0 = (q_p * jnp.exp2(cum_p)).astype(BF16)
        out = o_intra + _dot_nt(q0, st.astype(BF16))
        last = cum_p[last_row:last_row + 1, :]
        ks = (k_p * jnp.exp2(last - cum_p)).astype(BF16)
        upd = _dot_tn(v_bf, ks)
        return out, st * jnp.exp2(last) + jnp.where(same_head, upd, 0.0)

    states = [st_ref[p] for p in range(2)]
    steps = []
    for p in range(2):
        sl = slice(p * LANES, (p + 1) * LANES)
        for n, c in enumerate(order):
            rows = slice(c * HG_C, (c + 1) * HG_C)

            def step(p=p, sl=sl, rows=rows, c=c, final=n == n_chunks - 1):
                o_ref[rows, sl], states[p] = chunk(q[rows, sl], k[rows, sl], v[rows, sl], cums[c][:, sl], states[p])
                if final:
                    st_ref[p] = states[p]
                    fin_ref[0, p] = states[p]

            steps.append(step)
    return steps


def _hgrn_seq(g):
    return jnp.where(g < CTX_TILES, g, CTX_TILES + (g - CTX_TILES) // SMP_TILES_PER_SEQ)


def _hgrn_blk(g, rev):
    if not rev:
        return g
    j = g - CTX_TILES
    return jnp.where(g < CTX_TILES, g,
                     CTX_TILES + (j // SMP_TILES_PER_SEQ) * SMP_TILES_PER_SEQ
                     + (SMP_TILES_PER_SEQ - 1 - j % SMP_TILES_PER_SEQ))


def _hgrn_scan(proj, lb_dirs, s0_dirs, lvl, tri, layer):
    def dir_in_specs(d):
        blk = functools.partial(_hgrn_blk, rev=d == 1)
        return [
            pl.BlockSpec((TM, A_WIDTH), lambda g: (blk(g), 0)),
            pl.BlockSpec((TM, A_WIDTH), lambda g: (blk(g), 1 + d)),
            pl.BlockSpec((TM, A_WIDTH), lambda g: (blk(g), 3)),
            pl.BlockSpec((1, A_WIDTH), lambda g: (0, 0)),
            pl.BlockSpec((1, 2, LANES, LANES), lambda g: (_hgrn_seq(g), 0, 0, 0)),
            pl.BlockSpec((1, HG_C, HG_C), lambda g: (d, 0, 0)),
            pl.BlockSpec((1, HG_C, HG_C), lambda g: (d, 0, 0)),
        ]

    def dir_out_specs(d):
        blk = functools.partial(_hgrn_blk, rev=d == 1)
        return [pl.BlockSpec((TM, A_WIDTH), lambda g: (blk(g), 0)),
                pl.BlockSpec((1, 2, LANES, LANES), lambda g: (_hgrn_seq(g), 0, 0, 0))]

    dir_out_shape = [jax.ShapeDtypeStruct((T, A_WIDTH), F32), jax.ShapeDtypeStruct((N_SEQ, 2, LANES, LANES), F32)]
    dir_args = lambda d: (proj, proj, proj, lb_dirs[d], s0_dirs[d], lvl, tri)
    return pl.pallas_call(
        functools.partial(_hgrn_kernel, layer=layer),
        grid=(N_TILES,),
        in_specs=dir_in_specs(0) + dir_in_specs(1),
        out_specs=dir_out_specs(0) + dir_out_specs(1),
        out_shape=dir_out_shape + dir_out_shape,
        scratch_shapes=[pltpu.VMEM((2, LANES, LANES), F32), pltpu.VMEM((2, LANES, LANES), F32)],
        compiler_params=_cparams(("arbitrary",)),
        name="hgrn_scan",
    )(*dir_args(0), *dir_args(1))


def _pack_state(s):
    n = s.shape[0]
    st = jnp.swapaxes(s, -1, -2).reshape(n, 2, 2, A_DK, A_DK)
    z = jnp.zeros_like(st[:, :, 0])
    top = jnp.concatenate([st[:, :, 0], z], axis=-1)
    bot = jnp.concatenate([z, st[:, :, 1]], axis=-1)
    return jnp.concatenate([top, bot], axis=-2)


def _unpack_state(sp):
    n = sp.shape[0]
    h0 = sp[:, :, :A_DK, :A_DK]
    h1 = sp[:, :, A_DK:, A_DK:]
    st = jnp.stack([h0, h1], axis=2).reshape(n, A_HEADS, A_DK, A_DK)
    return jnp.swapaxes(st, -1, -2)


def _half_rms(x, g):
    r = lax.broadcasted_iota(jnp.int32, (LANES, LANES), 0)
    c = lax.broadcasted_iota(jnp.int32, (LANES, LANES), 1)
    half_mean = jnp.where((r < B_DK) == (c < B_DK), 1.0 / B_DK, 0.0).astype(BF16)
    xx = x * x
    hi = xx.astype(BF16)
    lo = (xx - hi.astype(F32)).astype(BF16)
    ms = jnp.dot(lo, half_mean, preferred_element_type=F32) + jnp.dot(hi, half_mean, preferred_element_type=F32)
    return x * lax.rsqrt(ms + EPS) * g


def _rope(x, cos, sin_signed):
    lane = _lane(x.shape)
    first = (lane % 32) < 16
    rot = jnp.where(first, pltpu.roll(x, LANES - 16, 1), pltpu.roll(x, 16, 1))
    return x * cos + rot * sin_signed


def _with_ones(v_bf):
    return jnp.concatenate([v_bf, jnp.ones_like(v_bf)], axis=-1)


def _diff_softmax_pv(q_bf, keys_bf, vals_ext, lam):
    lane = _lane(q_bf.shape)
    zero = jnp.zeros_like(q_bf)
    outs = []
    for mp in range(2):
        qm = jnp.where((lane < B_DK) == (mp == 0), q_bf, zero)
        s = [_dot_nt(qm, kk) for kk in keys_bf]
        mx = functools.reduce(jnp.maximum, [jnp.max(si, axis=-1, keepdims=True) for si in s])
        acc = None
        for si, ve in zip(s, vals_ext):
            e = jnp.exp((si - mx).astype(BF16))
            pv = jnp.dot(e, ve, preferred_element_type=F32)
            acc = pv if acc is None else acc + pv
        outs.append(acc[:, :B_DV] / acc[:, B_DV:])
    return outs[0] - lam * outs[1]


def _subln(o, g, lam_init):
    return o * lax.rsqrt(jnp.mean(o * o, axis=-1, keepdims=True) + EPS) * g * (1.0 - lam_init)


def _attn_ctx_kernel(lam_ref, *refs, lam_init, layer):
    q_refs, k_refs, v_refs = refs[:B_HEADS], refs[B_HEADS:2 * B_HEADS], refs[2 * B_HEADS:3 * B_HEADS]
    gq_ref, gk_ref, gs_ref = refs[3 * B_HEADS:3 * B_HEADS + 3]
    rest = refs[3 * B_HEADS + 3:]
    if layer:
        pk_ref, pv_ref, o_ref, nk_ref, nv_ref = rest
        nk_ref[0, :layer] = pk_ref[0]
        nv_ref[0, :layer] = pv_ref[0]
    else:
        o_ref, nk_ref, nv_ref = rest
    lam = lam_ref[0, 0]
    for h in range(B_HEADS):
        qn = _half_rms(q_refs[h][...], gq_ref[...]) * (B_DK ** -0.5)
        kn = _half_rms(k_refs[h][...], gk_ref[...])
        v = v_refs[h][...]
        nk_ref[0, layer, 0, h] = kn[:, :B_DK]
        nk_ref[0, layer, 1, h] = kn[:, B_DK:]
        nv_ref[0, layer, h] = v
        o = _diff_softmax_pv(qn.astype(BF16), [kn.astype(BF16)], [_with_ones(v.astype(BF16))], lam)
        o_ref[:, h * LANES:(h + 1) * LANES] = _subln(o, gs_ref[...], lam_init)


def _attn_ctx(proj, lam, gq2, gk2, gs, lam_init, layer, prev_k, prev_v):
    qcol, kcol, vcol = 5 * A_WIDTH // LANES, 5 * A_WIDTH // LANES + 4, 5 * A_WIDTH // LANES + 8
    prev_specs, prev_args = [], []
    if layer:
        prev_specs = [pl.BlockSpec((1, layer, 2, B_HEADS, SEQ, B_DK), lambda b: (b, 0, 0, 0, 0, 0)),
                      pl.BlockSpec((1, layer, B_HEADS, SEQ, B_DV), lambda b: (b, 0, 0, 0, 0))]
        prev_args = [prev_k, prev_v]
    n_l = layer + 1
    head_specs = [pl.BlockSpec((SEQ, LANES), functools.partial(lambda b, col: (b, col), col=c0 + h))
                  for c0 in (qcol, kcol, vcol) for h in range(B_HEADS)]
    return pl.pallas_call(
        functools.partial(_attn_ctx_kernel, lam_init=lam_init, layer=layer),
        grid=(BATCH,),
        in_specs=[pl.BlockSpec(memory_space=pltpu.SMEM)] + head_specs + [
            pl.BlockSpec((1, LANES), lambda b: (0, 0)),
            pl.BlockSpec((1, LANES), lambda b: (0, 0)),
            pl.BlockSpec((1, LANES), lambda b: (0, 0)),
        ] + prev_specs,
        out_specs=[
            pl.BlockSpec((SEQ, B_WIDTH), lambda b: (b, 0)),
            pl.BlockSpec((1, n_l, 2, B_HEADS, SEQ, B_DK), lambda b: (b, 0, 0, 0, 0, 0)),
            pl.BlockSpec((1, n_l, B_HEADS, SEQ, B_DV), lambda b: (b, 0, 0, 0, 0)),
        ],
        out_shape=[
            jax.ShapeDtypeStruct((T_CTX, B_WIDTH), F32),
            jax.ShapeDtypeStruct((BATCH, n_l, 2, B_HEADS, SEQ, B_DK), F32),
            jax.ShapeDtypeStruct((BATCH, n_l, B_HEADS, SEQ, B_DV), F32),
        ],
        compiler_params=_cparams(("arbitrary",)),
        name="diff_attention_ctx",
    )(lam, *([proj] * (3 * B_HEADS)), gq2, gk2, gs, *prev_args)


ATT_TQ = 256
ATT_HEADS = 4


def _attn_smp_kernel(lam_ref, *refs, lam_init):
    nh = ATT_HEADS
    q_refs, k_refs, v_refs = refs[:nh], refs[nh:2 * nh], refs[2 * nh:3 * nh]
    ck_refs, cv_refs = refs[3 * nh:4 * nh], refs[4 * nh:5 * nh]
    cos_ref, sin_ref, gq_ref, gk_ref, gs_ref, o_ref, qs_ref, ks_ref = refs[5 * nh:]
    lam = lam_ref[0, 0]
    cos = cos_ref[...]
    sin = sin_ref[...]
    g = gs_ref[...]
    prepared = []
    for h in range(nh):
        qn = _rope(_half_rms(q_refs[h][...], gq_ref[...]), cos, sin) * (B_DK ** -0.5)
        qs_ref[h] = qn.astype(BF16)
        ks_ref[h] = _rope(_half_rms(k_refs[h][...], gk_ref[...]), cos, sin).astype(BF16)
        ck = jnp.concatenate([ck_refs[h][0, 0, 0, 0], ck_refs[h][0, 0, 1, 0]], axis=-1).astype(BF16)
        cv = _with_ones(cv_refs[h][0, 0, 0].astype(BF16))
        v_bf = _with_ones(v_refs[h][...].astype(BF16))
        prepared.append((ck, cv, v_bf))
    for h in range(nh):
        ck, cv, v_bf = prepared[h]
        k_bf = ks_ref[h]
        for i in range(DEC_SEQ // ATT_TQ):
            rows = slice(i * ATT_TQ, (i + 1) * ATT_TQ)
            o = _diff_softmax_pv(qs_ref[h, rows, :], [k_bf, ck], [v_bf, cv], lam)
            o_ref[rows, h * LANES:(h + 1) * LANES] = _subln(o, g, lam_init)


def _attn_smp(proj, lam, cache_k, cache_v, cos, sin, gq2, gk2, gs, layer, lam_init):
    qcol, kcol, vcol = 5 * A_WIDTH // LANES, 5 * A_WIDTH // LANES + 4, 5 * A_WIDTH // LANES + 8
    r0 = T_CTX // DEC_SEQ
    nh = ATT_HEADS

    def per_head(shape, index):
        return [pl.BlockSpec(shape, functools.partial(index, dh=dh)) for dh in range(nh)]

    head_specs = (
        per_head((DEC_SEQ, LANES), lambda b, hp, dh: (r0 + b, qcol + hp * nh + dh))
        + per_head((DEC_SEQ, LANES), lambda b, hp, dh: (r0 + b, kcol + hp * nh + dh))
        + per_head((DEC_SEQ, LANES), lambda b, hp, dh: (r0 + b, vcol + hp * nh + dh))
        + per_head((1, 1, 2, 1, PAST, B_DK), lambda b, hp, dh: (b, layer, 0, hp * nh + dh, 0, 0))
        + per_head((1, 1, 1, PAST, B_DV), lambda b, hp, dh: (b, layer, hp * nh + dh, 0, 0)))
    return pl.pallas_call(
        functools.partial(_attn_smp_kernel, lam_init=lam_init),
        grid=(DEC_BATCH, B_HEADS // nh),
        in_specs=[pl.BlockSpec(memory_space=pltpu.SMEM)] + head_specs + [
            pl.BlockSpec((DEC_SEQ, LANES), lambda b, hp: (0, 0)),
            pl.BlockSpec((DEC_SEQ, LANES), lambda b, hp: (0, 0)),
            pl.BlockSpec((1, LANES), lambda b, hp: (0, 0)),
            pl.BlockSpec((1, LANES), lambda b, hp: (0, 0)),
            pl.BlockSpec((1, LANES), lambda b, hp: (0, 0)),
        ],
        out_specs=pl.BlockSpec((DEC_SEQ, nh * LANES), lambda b, hp: (b, hp)),
        out_shape=jax.ShapeDtypeStruct((T_SMP, B_WIDTH), F32),
        scratch_shapes=[pltpu.VMEM((nh, DEC_SEQ, LANES), BF16), pltpu.VMEM((nh, DEC_SEQ, LANES), BF16)],
        compiler_params=_cparams(("arbitrary", "arbitrary")),
        name="diff_attention_smp",
    )(lam, *([proj] * (3 * nh)), *([cache_k] * nh), *([cache_v] * nh), cos, sin, gq2, gk2, gs)


def _rope_tables():
    n_rows = DEC_SEQ // GRID_W
    row = np.repeat(np.arange(n_rows), GRID_W).astype(np.float32)
    col = np.tile(np.arange(GRID_W), n_rows).astype(np.float32)
    half = B_DK // 2
    inv_freq = (ROPE_BASE ** (-jnp.arange(0, half, 2, dtype=F32) / half))
    row_ang = jnp.asarray(row)[:, None] * inv_freq
    col_ang = jnp.asarray(col)[:, None] * inv_freq
    ang = jnp.concatenate([row_ang, row_ang, col_ang, col_ang], axis=-1)
    ang = jnp.concatenate([ang, ang], axis=-1)
    sign = np.where((np.arange(LANES) % 32) < 16, -1.0, 1.0).astype(np.float32)
    return jnp.cos(ang), jnp.sin(ang) * sign


CM_ROWS = 1024


def _gelu(x):
    return 0.5 * x * (1.0 + lax.erf(x * (2.0 ** -0.5)))


def _cmlp_kernel(u_ref, v_ref, g_ref, b_ref, ws_ref, bs_ref, o_ref):
    u = _gelu(u_ref[...])
    gv = _gelu(v_ref[...])
    mu = jnp.mean(gv, axis=-1, keepdims=True)
    dv = gv - mu
    var = jnp.mean(dv * dv, axis=-1, keepdims=True)
    vn = (dv * lax.rsqrt(var + EPS) * g_ref[...] + b_ref[...]).astype(BF16)
    lane = _lane((C_CHUNK, LANES))
    for c in range(CM_ROWS // C_CHUNK):
        rs = slice(c * C_CHUNK, (c + 1) * C_CHUNK)
        for p in range(2):
            cs = slice(p * LANES, (p + 1) * LANES)
            vp = vn[rs, cs]
            m0 = jnp.dot(ws_ref[2 * p].astype(BF16), vp, preferred_element_type=F32)
            m1 = jnp.dot(ws_ref[2 * p + 1].astype(BF16), vp, preferred_element_type=F32)
            mixed = jnp.where(lane < C_DG, m0, m1) + bs_ref[:, cs]
            o_ref[rs, cs] = u[rs, cs] * mixed


def _chunk_mlp(proj, ln_g, ln_b, w_s, bias_full):
    ucol = (5 * A_WIDTH + 3 * B_WIDTH) // C_WIDTH
    return pl.pallas_call(
        _cmlp_kernel,
        grid=(T // CM_ROWS,),
        in_specs=[
            pl.BlockSpec((CM_ROWS, C_WIDTH), lambda i: (i, ucol)),
            pl.BlockSpec((CM_ROWS, C_WIDTH), lambda i: (i, ucol + 1)),
            pl.BlockSpec((1, C_WIDTH), lambda i: (0, 0)),
            pl.BlockSpec((1, C_WIDTH), lambda i: (0, 0)),
            pl.BlockSpec((C_GROUPS, C_CHUNK, C_CHUNK), lambda i: (0, 0, 0)),
            pl.BlockSpec((C_CHUNK, C_WIDTH), lambda i: (0, 0)),
        ],
        out_specs=pl.BlockSpec((CM_ROWS, C_WIDTH), lambda i: (i, 0)),
        out_shape=jax.ShapeDtypeStruct((T, C_WIDTH), F32),
        compiler_params=_cparams(("arbitrary",)),
        name="chunk_mlp",
    )(proj, proj, ln_g.reshape(1, C_WIDTH), ln_b.reshape(1, C_WIDTH), w_s, bias_full)


PM_ROWS = 512
PM_SUB = TM


def _postmix_kernel(*refs, n_x):
    (of_ref, ob_ref, ag_ref, hg_ref, hsel_ref, bc_ref, bs_ref, c_ref, w_ref, gate1_ref, shift2_ref, scale2_ref,
     g2_ref, wrh_ref, wrl_ref, br_ref, x1_ref, h2_ref, idx_ref, gw_ref) = refs[n_x:]
    is_ctx = pl.program_id(0) < T_CTX // PM_ROWS
    x_refs = refs[:n_x]
    for s in range(PM_ROWS // PM_SUB):
        rows = slice(s * PM_SUB, (s + 1) * PM_SUB)
        x = x_refs[0][rows, :] if n_x == 1 else jnp.where(is_ctx, x_refs[0][rows, :], x_refs[1][rows, :])
        o = of_ref[rows, :] + ob_ref[rows, :]
        ms = _dot_sel(o * o, hsel_ref[...]) * (1.0 / A_DK)
        ag = ag_ref[rows, :]
        a = o * lax.rsqrt(ms + EPS) * hg_ref[...] * (ag * jax.nn.sigmoid(ag))
        b = jnp.where(is_ctx, bc_ref[rows, :], bs_ref[rows, :])
        mixed = jnp.dot(a.astype(BF16), w_ref[0:A_WIDTH, :], preferred_element_type=F32)
        mixed = mixed + jnp.dot(b.astype(BF16), w_ref[A_WIDTH:A_WIDTH + B_WIDTH, :], preferred_element_type=F32)
        mixed = mixed + jnp.dot(c_ref[rows, :].astype(BF16), w_ref[A_WIDTH + B_WIDTH:, :],
                                preferred_element_type=F32)
        x1 = x + gate1_ref[0] * mixed
        x1_ref[rows, :] = x1
        y = x1 * lax.rsqrt(jnp.mean(x1 * x1, axis=-1, keepdims=True) + EPS) * g2_ref[...]
        h2 = y * (1.0 + scale2_ref[0]) + shift2_ref[0]
        h2_ref[rows] = _pack_rows(h2)
        hi = h2.astype(BF16)
        lo = (h2 - hi.astype(F32)).astype(BF16)
        lg = jnp.dot(lo, wrh_ref[...], preferred_element_type=F32)
        lg = lg + jnp.dot(hi, wrl_ref[...], preferred_element_type=F32)
        lg = lg + jnp.dot(hi, wrh_ref[...], preferred_element_type=F32) + br_ref[...]
        lt = lg.T[:N_EXPERTS]
        row = lax.broadcasted_iota(jnp.int32, lt.shape, 0)
        out_row = lax.broadcasted_iota(jnp.int32, (8, PM_SUB), 0)
        idx_out = jnp.zeros((8, PM_SUB), jnp.int32)
        val_out = jnp.zeros((8, PM_SUB), F32)
        top0 = None
        den = None
        for kk in range(TOP_K):
            mx = jnp.max(lt, axis=0, keepdims=True)
            am = jnp.min(jnp.where(lt == mx, row, N_EXPERTS), axis=0, keepdims=True)
            if kk == 0:
                top0 = mx
            e = jnp.exp(mx - top0)
            den = e if den is None else den + e
            idx_out = jnp.where(out_row == kk, am, idx_out)
            val_out = jnp.where(out_row == kk, e, val_out)
            lt = jnp.where(row == am, -jnp.inf, lt)
        idx_ref[:, rows] = idx_out
        gw_ref[:, rows] = val_out / den


def _post_mix(o_f, o_b, proj, hg, hsel, b_ctx, b_smp, c_out, w_out_bf, x, mod3, g2, wr_hi, wr_lo, br_pad):
    tile = lambda w: pl.BlockSpec((PM_ROWS, w), lambda i: (i, 0))
    const = lambda shape: pl.BlockSpec(shape, lambda i: tuple(0 for _ in shape))
    modspec = lambda j: pl.BlockSpec((1, 1, D), lambda i: (_mod_row(i, PM_ROWS), 0, j))
    rowsT = pl.BlockSpec((8, PM_ROWS), lambda i: (0, i))
    x_specs, x_args = _stream_specs(x, PM_ROWS)
    ctx_tiles = T_CTX // PM_ROWS
    return pl.pallas_call(
        functools.partial(_postmix_kernel, n_x=len(x_args)),
        grid=(T // PM_ROWS,),
        in_specs=x_specs + [
            tile(A_WIDTH), tile(A_WIDTH),
            pl.BlockSpec((PM_ROWS, A_WIDTH), lambda i: (i, 4)),
            const((1, A_WIDTH)), const((A_WIDTH, A_WIDTH)),
            pl.BlockSpec((PM_ROWS, B_WIDTH), lambda i: (jnp.minimum(i, ctx_tiles - 1), 0)),
            pl.BlockSpec((PM_ROWS, B_WIDTH), lambda i: (jnp.maximum(i - ctx_tiles, 0), 0)),
            tile(C_WIDTH),
            const((D, D)),
            modspec(2), modspec(3), modspec(4),
            const((1, D)), const((D, LANES)), const((D, LANES)), const((1, LANES)),
        ],
        out_specs=[tile(D), pl.BlockSpec((PM_ROWS, ROW_TILES, LANES), lambda i: (i, 0, 0)), rowsT, rowsT],
        out_shape=[
            jax.ShapeDtypeStruct((T, D), F32),
            jax.ShapeDtypeStruct((T, ROW_TILES, LANES), ROW_DT),
            jax.ShapeDtypeStruct((8, T), jnp.int32),
            jax.ShapeDtypeStruct((8, T), F32),
        ],
        compiler_params=_cparams(("arbitrary",)),
        name="post_mix_router",
    )(*x_args, o_f, o_b, proj, hg, hsel, b_ctx, b_smp, c_out, w_out_bf, mod3, mod3, mod3, g2.reshape(1, D),
      wr_hi, wr_lo, br_pad)


def _route_kernel(idx_ref, dest_ref, meta_ref):
    erow = lax.broadcasted_iota(jnp.int32, (N_EXPERTS, TM), 0)
    s_i = lax.broadcasted_iota(jnp.int32, (TM, TM), 0)
    t_i = lax.broadcasted_iota(jnp.int32, (TM, TM), 1)
    earlier = (s_i < t_i).astype(BF16)
    out_row = lax.broadcasted_iota(jnp.int32, (8, TM), 0)

    def onehots(i):
        idx = idx_ref[:, pl.ds(pl.multiple_of(i * TM, TM), TM)]
        return [(erow == idx[kk:kk + 1, :]) for kk in range(TOP_K)]

    def count_tile(i, run):
        ohs = onehots(i)
        base = run
        pos = jnp.zeros((8, TM), F32)
        for kk in range(TOP_K):
            ohf = ohs[kk].astype(F32)
            before = jnp.dot(ohs[kk].astype(BF16), earlier, preferred_element_type=F32)
            p = jnp.sum(ohf * (base + before), axis=0, keepdims=True)
            pos = jnp.where(out_row == kk, p, pos)
            base = base + jnp.sum(ohf, axis=1, keepdims=True)
        dest_ref[:, pl.ds(pl.multiple_of(i * TM, TM), TM)] = pos.astype(jnp.int32)
        return base

    counts = lax.fori_loop(0, N_TILES, count_tile, jnp.zeros((N_EXPERTS, 1), F32)).astype(jnp.int32)
    bm_shift = MOE_BM.bit_length() - 1
    padded = lax.shift_left(lax.shift_right_logical(counts + (MOE_BM - 1), bm_shift), bm_shift)
    e_r = lax.broadcasted_iota(jnp.int32, (N_EXPERTS, N_EXPERTS), 0)
    e_c = lax.broadcasted_iota(jnp.int32, (N_EXPERTS, N_EXPERTS), 1)
    incl = (e_c <= e_r).astype(BF16)
    pad_end = _sel_dot(incl, jnp.broadcast_to(padded.astype(F32), (N_EXPERTS, LANES)))[:, :1]
    pad_start = pad_end - padded.astype(F32)

    def place_tile(i, carry):
        ohs = onehots(i)
        sl = pl.ds(pl.multiple_of(i * TM, TM), TM)
        off = jnp.zeros((8, TM), F32)
        for kk in range(TOP_K):
            o = jnp.sum(ohs[kk].astype(F32) * pad_start, axis=0, keepdims=True)
            off = jnp.where(out_row == kk, o, off)
        dest_ref[:, sl] = dest_ref[:, sl] + off.astype(jnp.int32)
        return carry

    lax.fori_loop(0, N_TILES, place_tile, 0)

    total = jnp.max(pad_end, axis=0, keepdims=True)
    lane_i = lax.broadcasted_iota(jnp.int32, (1, TM), 1)
    blk0 = (lane_i * MOE_BM).astype(F32)
    block_e = jnp.sum((pad_end <= blk0).astype(F32), axis=0, keepdims=True)
    live_end = pad_start + counts.astype(F32)
    sel = erow.astype(F32) == block_e
    live = jnp.sum(jnp.where(sel, live_end, 0.0), axis=0, keepdims=True)
    valid = jnp.where(blk0 < total, jnp.clip(live - blk0, 0.0, float(MOE_BM)), 0.0)
    own = erow == lane_i
    n_blk = jnp.sum(jnp.where(own, padded.astype(F32), 0.0), axis=0, keepdims=True) * (1.0 / MOE_BM)
    first_blk = jnp.sum(jnp.where(own, pad_start, 0.0), axis=0, keepdims=True) * (1.0 / MOE_BM)
    meta = jnp.where(out_row == 0, valid, 0.0)
    meta = jnp.where(out_row == 1, n_blk, meta)
    meta = jnp.where(out_row == 2, first_blk, meta)
    meta = jnp.where(out_row == 3, total * (1.0 / MOE_BM), meta)
    meta_ref[...] = meta.astype(jnp.int32)


def _route(idx_t):
    assert MOE_BLOCKS <= TM
    return pl.pallas_call(
        _route_kernel,
        out_shape=[jax.ShapeDtypeStruct((8, T), jnp.int32), jax.ShapeDtypeStruct((8, TM), jnp.int32)],
        compiler_params=pltpu.CompilerParams(vmem_limit_bytes=VMEM_LIMIT),
        name="moe_route",
    )(idx_t)


def _moe_kernel(bv_ref, nb_ref, g0_ref, tot_ref, x_hbm, wgu_ref, bgu_ref, wdn_ref, bdn_ref, y_hbm,
                wgu_bf, wdn_bf, xbuf, ybuf, xb_ref, xsem, ysem):
    e = pl.program_id(0)
    n_blk = nb_ref[e]
    first = g0_ref[e]
    total = tot_ref[0]
    ahead = MOE_RING - 1

    def x_copy(g):
        slot = g % MOE_RING
        return pltpu.make_async_copy(x_hbm.at[pl.ds(g * MOE_BM, MOE_BM)], xbuf.at[slot], xsem.at[slot])

    def y_copy(g):
        slot = g % MOE_RING
        return pltpu.make_async_copy(ybuf.at[slot], y_hbm.at[pl.ds(g * MOE_BM, MOE_BM)], ysem.at[slot])

    @pl.when(e == 0)
    def _():
        for g in range(ahead):
            @pl.when(g < total)
            def _():
                x_copy(g).start()

    @pl.when(n_blk > 0)
    def _():
        wgu_bf[...] = wgu_ref[0, 0].astype(BF16)
        wdn_bf[...] = wdn_ref[0, 0].astype(BF16)

    def block(j, carry):
        g = first + j
        slot = g % MOE_RING
        x_copy(g).wait()

        @pl.when(g + ahead < total)
        def _():
            x_copy(g + ahead).start()

        @pl.when(g >= MOE_RING)
        def _():
            y_copy(g - MOE_RING).wait()

        n_live = bv_ref[g]

        def ffn(n_rows):
            live = lax.broadcasted_iota(jnp.int32, (n_rows, LANES), 0) < n_live
            for c, chunk in enumerate(_unpack_rows(xbuf[slot, :n_rows])):
                xb_ref[:n_rows, c * LANES:(c + 1) * LANES] = jnp.where(live, chunk, 0.0).astype(BF16)
            gu = jnp.dot(xb_ref[:n_rows, :], wgu_bf[...], preferred_element_type=F32) + bgu_ref[0, 0]
            glu = jnp.minimum(gu[:, :D], SWIGLU_LIMIT)
            lin = jnp.clip(gu[:, D:], -SWIGLU_LIMIT, SWIGLU_LIMIT)
            act = glu * jax.nn.sigmoid(SWIGLU_ALPHA * glu) * (lin + 1.0)
            y = jnp.dot(act.astype(BF16), wdn_bf[...], preferred_element_type=F32) + bdn_ref[0, 0]
            ybuf[slot, :n_rows] = _pack_rows(y)

        quarter = MOE_BM // 4
        for n_q in range(1, 5):
            height = n_q * quarter

            above = n_live > height - quarter if n_q > 1 else True
            below = n_live <= height if n_q < 4 else True

            @pl.when(jnp.logical_and(above, below))
            def _(height=height):
                ffn(height)
                if height < MOE_BM:
                    ybuf[slot, height:] = jnp.zeros((MOE_BM - height, ROW_TILES, LANES), ROW_DT)

        y_copy(g).start()
        return carry

    lax.fori_loop(0, n_blk, block, 0)

    @pl.when(e == N_EXPERTS - 1)
    def _():
        for back in range(MOE_RING, 0, -1):
            @pl.when(total >= back)
            def _():
                y_copy(total - back).wait()

        def fill(g, carry):
            ybuf[g % MOE_RING] = jnp.zeros((MOE_BM, ROW_TILES, LANES), ROW_DT)
            y_copy(g).start()
            y_copy(g).wait()
            return carry

        lax.fori_loop(total, MOE_BLOCKS, fill, 0)


def _moe_ffn(block_valid, n_blk, first_blk, total_blk, xs, w_gu, b_gu, w_dn, b_dn, layer):
    rows = (MOE_BM, ROW_TILES, LANES)
    return pl.pallas_call(
        _moe_kernel,
        grid_spec=pltpu.PrefetchScalarGridSpec(
            num_scalar_prefetch=4,
            grid=(N_EXPERTS,),
            in_specs=[
                pl.BlockSpec(memory_space=pl.ANY),
                pl.BlockSpec((1, 1, D, 2 * D), lambda e, *_: (layer, e, 0, 0)),
                pl.BlockSpec((1, 1, 1, 2 * D), lambda e, *_: (layer, e, 0, 0)),
                pl.BlockSpec((1, 1, D, D), lambda e, *_: (layer, e, 0, 0)),
                pl.BlockSpec((1, 1, 1, D), lambda e, *_: (layer, e, 0, 0)),
            ],
            out_specs=pl.BlockSpec(memory_space=pl.ANY),
            scratch_shapes=[
                pltpu.VMEM((D, 2 * D), BF16), pltpu.VMEM((D, D), BF16),
                pltpu.VMEM((MOE_RING,) + rows, ROW_DT), pltpu.VMEM((MOE_RING,) + rows, ROW_DT),
                pltpu.VMEM((MOE_BM, D), BF16),
                pltpu.SemaphoreType.DMA((MOE_RING,)), pltpu.SemaphoreType.DMA((MOE_RING,)),
            ],
        ),
        out_shape=jax.ShapeDtypeStruct((MOE_ROWS, ROW_TILES, LANES), ROW_DT),
        compiler_params=_cparams(("arbitrary",)),
        name="moe_expert_ffn",
    )(block_valid, n_blk, first_blk, total_blk, xs, w_gu, b_gu.reshape(DEPTH, N_EXPERTS, 1, 2 * D), w_dn,
      b_dn.reshape(DEPTH, N_EXPERTS, 1, D))


def _combine_kernel(x1_ref, y_ref, gw_ref, gate2_ref, o_ref):
    gw = jnp.concatenate([gw_ref[...], jnp.zeros((LANES - 8, TM), F32)], axis=0).T
    ys = [_unpack_rows(y_ref[kk]) for kk in range(TOP_K)]
    for c in range(D // LANES):
        cs = slice(c * LANES, (c + 1) * LANES)
        acc = None
        for kk in range(TOP_K):
            term = ys[kk][c] * gw[:, kk:kk + 1]
            acc = term if acc is None else acc + term
        o_ref[:, cs] = x1_ref[:, cs] + gate2_ref[0, :, cs] * acc


def _combine(x1, yg, gw, mod3, tile0, n_tiles):
    return pl.pallas_call(
        _combine_kernel,
        grid=(n_tiles,),
        in_specs=[
            pl.BlockSpec((TM, D), lambda i: (tile0 + i, 0)),
            pl.BlockSpec((TOP_K, TM, ROW_TILES, LANES), lambda i: (0, i, 0, 0)),
            pl.BlockSpec((8, TM), lambda i: (0, tile0 + i)),
            pl.BlockSpec((1, 1, D), lambda i: (_mod_row(tile0 + i), 0, 5)),
        ],
        out_specs=pl.BlockSpec((TM, D), lambda i: (i, 0)),
        out_shape=jax.ShapeDtypeStruct((n_tiles * TM, D), F32),
        compiler_params=_cparams(("arbitrary",)),
        name="moe_combine",
    )(x1, yg, gw, mod3)


def _sc_mesh():
    return plsc.VectorSubcoreMesh(core_axis_name="c", subcore_axis_name="s")


def _sc_worker():
    return lax.axis_index("s") * SC_CORES + lax.axis_index("c")


def _sc_dispatch(h2t, dest_km):
    per_w = T // SC_WORKERS

    @functools.partial(
        pl.kernel, mesh=_sc_mesh(),
        out_type=jax.ShapeDtypeStruct((MOE_ROWS, ROW_TILES, LANES), ROW_DT),
        scratch_types=[pltpu.VMEM((SC_WIN,), jnp.int32), pltpu.VMEM((SC_WIN, ROW_TILES, LANES), ROW_DT),
                       pltpu.SemaphoreType.DMA],
    )
    def run(h_hbm, d_hbm, o_hbm, idx_v, rows_v, sem):
        w0 = _sc_worker() * per_w

        @pl.loop(0, per_w // SC_WIN)
        def _(w):
            base = pl.multiple_of(w0 + w * SC_WIN, SC_WIN)
            pltpu.sync_copy(h_hbm.at[pl.ds(base, SC_WIN)], rows_v)
            for kk in range(TOP_K):
                pltpu.sync_copy(d_hbm.at[pl.ds(kk * T + base, SC_WIN)], idx_v)
                pltpu.async_copy(rows_v, o_hbm.at[idx_v], sem).wait()

    return run(h2t, dest_km)


def _sc_gather(yb, dest):
    n = dest.shape[0]
    per_w = n // SC_WORKERS

    @functools.partial(
        pl.kernel, mesh=_sc_mesh(),
        out_type=jax.ShapeDtypeStruct((n, ROW_TILES, LANES), ROW_DT),
        scratch_types=[pltpu.VMEM((SC_WIN,), jnp.int32), pltpu.VMEM((SC_WIN, ROW_TILES, LANES), ROW_DT),
                       pltpu.SemaphoreType.DMA],
    )
    def run(y_hbm, d_hbm, o_hbm, idx_v, rows_v, sem):
        w0 = _sc_worker() * per_w

        @pl.loop(0, per_w // SC_WIN)
        def _(w):
            base = pl.multiple_of(w0 + w * SC_WIN, SC_WIN)
            pltpu.sync_copy(d_hbm.at[pl.ds(base, SC_WIN)], idx_v)
            pltpu.async_copy(y_hbm.at[idx_v], rows_v, sem).wait()
            pltpu.sync_copy(rows_v, o_hbm.at[pl.ds(base, SC_WIN)])

    return run(yb, dest)


def kernel(x_prompt, x_sample, c, cache_diff_k, cache_diff_v, state_hgrn, c_ctx, norm_mix_g, norm_ffn_g, w_mod, b_mod, w_in, w_out, hgrn_lower_bounds, hgrn_norm_g, diff_q_norm_g, diff_k_norm_g, diff_lambda_q1, diff_lambda_k1, diff_lambda_q2, diff_lambda_k2, diff_subln_g, cmlp_ln_g, cmlp_ln_b, cmlp_w_s, cmlp_b_s, router_w, router_b, moe_w_gate_up, moe_b_gate_up, moe_w_down, moe_b_down):
    x = (x_prompt.reshape(T_CTX, D), x_sample.reshape(T_SMP, D))
    cvec = jnp.concatenate([c_ctx[None, :], c, jnp.zeros((MOD_ROWS - 1 - DEC_BATCH, D), F32)], axis=0)
    mod = _modulation(cvec, w_mod, b_mod)

    lvl_np, tri_np = _hgrn_tables()
    lvl = jnp.asarray(lvl_np)
    tri = jnp.asarray(tri_np, dtype=BF16)
    cos, sin = _rope_tables()
    hsel = jnp.asarray(np.kron(np.eye(A_HEADS), np.ones((A_DK, A_DK))), dtype=BF16)
    sm = jax.nn.softmax(hgrn_lower_bounds.astype(F32), axis=0)
    lb_all = jnp.cumsum(sm, axis=0) - sm[0]

    new_k, new_v, new_s = None, None, []
    for l in range(DEPTH):
        mod3 = mod[l].reshape(MOD_ROWS, 1, 6 * D)
        proj = _in_projection(x, norm_mix_g[l], mod3, w_in[l].astype(BF16))

        s0 = jnp.concatenate([jnp.zeros((BATCH, 2, A_HEADS, A_DK, A_DK), F32), state_hgrn[:, l]], axis=0)
        o_f, fin_f, o_b, fin_b = _hgrn_scan(proj, [lb_all[l, d].reshape(1, A_WIDTH) for d in range(2)],
                                            [_pack_state(s0[:, d]) for d in range(2)], lvl, tri, l)
        o_dir = [o_f, o_b]
        new_s.append(jnp.stack([_unpack_state(fin_f[:BATCH]), _unpack_state(fin_b[:BATCH])], axis=1))

        lam_init = 0.8 - 0.6 * math.exp(-0.3 * l)
        lam = (jnp.exp(jnp.sum(diff_lambda_q1[l] * diff_lambda_k1[l]))
               - jnp.exp(jnp.sum(diff_lambda_q2[l] * diff_lambda_k2[l])) + lam_init).reshape(1, 1)
        gq2 = jnp.tile(diff_q_norm_g[l], 2).reshape(1, LANES)
        gk2 = jnp.tile(diff_k_norm_g[l], 2).reshape(1, LANES)
        gs = diff_subln_g[l].reshape(1, LANES)
        b_ctx, new_k, new_v = _attn_ctx(proj, lam, gq2, gk2, gs, lam_init, l, new_k, new_v)
        b_smp = _attn_smp(proj, lam, cache_diff_k, cache_diff_v, cos, sin, gq2, gk2, gs, l, lam_init)

        bias_full = jnp.repeat(cmlp_b_s[l].T, C_DG, axis=1)
        c_out = _chunk_mlp(proj, cmlp_ln_g[l], cmlp_ln_b[l], cmlp_w_s[l], bias_full)

        hg = jnp.tile(hgrn_norm_g[l], A_HEADS).reshape(1, A_WIDTH)
        wr_pad = jnp.pad(router_w[l], ((0, 0), (0, LANES - N_EXPERTS)))
        wr_hi = wr_pad.astype(BF16)
        wr_lo = (wr_pad - wr_hi.astype(F32)).astype(BF16)
        br_pad = jnp.pad(router_b[l], (0, LANES - N_EXPERTS)).reshape(1, LANES)
        x1, h2, idx_t, gw_t = _post_mix(o_dir[0], o_dir[1], proj, hg, hsel, b_ctx, b_smp, c_out,
                                        w_out[l].astype(BF16), x, mod3, norm_ffn_g[l], wr_hi, wr_lo, br_pad)

        dest_t, meta = _route(idx_t)
        dest_km = dest_t[:TOP_K].reshape(-1)
        xs = _sc_dispatch(h2, dest_km)
        yb = _moe_ffn(meta[0, :MOE_BLOCKS], meta[1, :N_EXPERTS], meta[2, :N_EXPERTS], meta[3, :1], xs,
                      moe_w_gate_up, moe_b_gate_up, moe_w_down, moe_b_down, l)
        yg_ctx = _sc_gather(yb, dest_t[:TOP_K, :T_CTX].reshape(-1)).reshape(TOP_K, T_CTX, ROW_TILES, LANES)
        yg_smp = _sc_gather(yb, dest_t[:TOP_K, T_CTX:].reshape(-1)).reshape(TOP_K, T_SMP, ROW_TILES, LANES)
        x = (_combine(x1, yg_ctx, gw_t, mod3, 0, CTX_TILES),
             _combine(x1, yg_smp, gw_t, mod3, CTX_TILES, N_TILES - CTX_TILES))

    y_prompt = x[0].reshape(BATCH, SEQ, D)
    y_sample = x[1].reshape(DEC_BATCH, DEC_SEQ, D)
    return (y_prompt, y_sample, new_k, new_v, jnp.stack(new_s, axis=1))
```

```python
import functools
import math

import numpy as np
import jax
import jax.numpy as jnp
from jax import lax
from jax.experimental import pallas as pl
from jax.experimental.pallas import tpu as pltpu
from jax.experimental.pallas import tpu_sc as plsc

F32 = jnp.float32
BF16 = jnp.bfloat16

D = 1024
DEPTH = 2
BATCH, SEQ = 16, 256
DEC_BATCH, DEC_SEQ = 8, 1024
PAST = 512
GRID_W = 64
A_HEADS, A_DK = 4, 64
A_WIDTH = 256
B_HEADS, B_DK, B_DV = 4, 64, 128
B_WIDTH = 512
C_GROUPS, C_CHUNK, C_WIDTH, C_DG = 4, 128, 256, 64
IN_WIDTH = 5 * A_WIDTH + 3 * B_WIDTH + 2 * C_WIDTH
N_EXPERTS, TOP_K = 32, 4
SWIGLU_LIMIT, SWIGLU_ALPHA = 7.0, 1.702
ROPE_BASE = 10000.0
EPS = 1e-6

T_CTX = BATCH * SEQ
T_SMP = DEC_BATCH * DEC_SEQ
T = T_CTX + T_SMP
N_SEQ = BATCH + DEC_BATCH
MOD_ROWS = 16

TM = 256
N_TILES = T // TM
CTX_TILES = T_CTX // TM
SMP_TILES_PER_SEQ = DEC_SEQ // TM
LANES = 128
MOE_BM = 256
MOE_ROWS = T * TOP_K + N_EXPERTS * MOE_BM
MOE_BLOCKS = MOE_ROWS // MOE_BM
MOE_RING = 4
ROW_WORDS = D // 2
ROW_TILES = ROW_WORDS // LANES
ROW_DT = jnp.int32
SC_CORES, SC_SUBCORES = 2, 16
SC_WORKERS = SC_CORES * SC_SUBCORES
SC_WIN = 128
VMEM_LIMIT = 56 * 1024 * 1024


def _cparams(sem):
    return pltpu.CompilerParams(dimension_semantics=sem, vmem_limit_bytes=VMEM_LIMIT)


def _mod_row(i, rows=TM):
    ctx_tiles = T_CTX // rows
    return jnp.where(i < ctx_tiles, 0, 1 + (i - ctx_tiles) // (DEC_SEQ // rows))


def _split3(x):
    hi = x.astype(BF16)
    r = x - hi.astype(F32)
    mid = r.astype(BF16)
    lo = (r - mid.astype(F32)).astype(BF16)
    return hi, mid, lo


def _sel_dot(sel, x):
    hi, mid, lo = _split3(x)
    acc = jnp.dot(sel, lo, preferred_element_type=F32)
    acc = acc + jnp.dot(sel, mid, preferred_element_type=F32)
    return acc + jnp.dot(sel, hi, preferred_element_type=F32)


def _dot_sel(x, sel):
    hi, mid, lo = _split3(x)
    acc = jnp.dot(lo, sel, preferred_element_type=F32)
    acc = acc + jnp.dot(mid, sel, preferred_element_type=F32)
    return acc + jnp.dot(hi, sel, preferred_element_type=F32)


def _dot_nt(a, b):
    return lax.dot_general(a, b, (((1,), (1,)), ((), ())), preferred_element_type=F32)


def _dot_tn(a, b):
    return lax.dot_general(a, b, (((0,), (0,)), ((), ())), preferred_element_type=F32)


def _lane(shape):
    return lax.broadcasted_iota(jnp.int32, shape, len(shape) - 1)


def _pack_rows(x):
    hi = lax.bitcast_convert_type(x[:, :ROW_WORDS].astype(BF16).astype(F32), jnp.int32)
    lo = lax.bitcast_convert_type(x[:, ROW_WORDS:].astype(BF16).astype(F32), jnp.int32)
    words = hi | lax.shift_right_logical(lo, 16)
    return pltpu.einshape("t(jl)->tjl", words, l=LANES)


def _unpack_rows(words3):
    wt = pltpu.einshape("tjl->jtl", words3)
    hi = [lax.bitcast_convert_type(wt[j] & jnp.int32(-65536), F32) for j in range(ROW_TILES)]
    lo = [lax.bitcast_convert_type(lax.shift_left(wt[j], 16), F32) for j in range(ROW_TILES)]
    return hi + lo


def _mod_kernel(c_ref, w_ref, b_ref, o_ref):
    c = c_ref[...]
    s = c * jax.nn.sigmoid(c)
    o_ref[0] = jnp.dot(s.astype(BF16), w_ref[0].astype(BF16), preferred_element_type=F32) + b_ref[0]


def _modulation(cvec, w_mod, b_mod):
    tn = 1536
    return pl.pallas_call(
        _mod_kernel,
        grid=(DEPTH, 6 * D // tn),
        in_specs=[
            pl.BlockSpec((MOD_ROWS, D), lambda l, j: (0, 0)),
            pl.BlockSpec((1, D, tn), lambda l, j: (l, 0, j)),
            pl.BlockSpec((1, 1, tn), lambda l, j: (l, 0, j)),
        ],
        out_specs=pl.BlockSpec((1, MOD_ROWS, tn), lambda l, j: (l, 0, j)),
        out_shape=jax.ShapeDtypeStruct((DEPTH, MOD_ROWS, 6 * D), F32),
        compiler_params=_cparams(("arbitrary", "arbitrary")),
        name="modulation",
    )(cvec, w_mod, b_mod.reshape(DEPTH, 1, 6 * D))


def _stream_specs(x, rows=TM):
    ctx_tiles = T_CTX // rows
    if isinstance(x, tuple):
        return [pl.BlockSpec((rows, D), lambda i: (jnp.minimum(i, ctx_tiles - 1), 0)),
                pl.BlockSpec((rows, D), lambda i: (jnp.maximum(i - ctx_tiles, 0), 0))], list(x)
    return [pl.BlockSpec((rows, D), lambda i: (i, 0))], [x]


IN_ROWS = 512


def _inproj_kernel(*refs, n_x):
    g_ref, shift_ref, scale_ref, w_ref, o_ref = refs[n_x:]
    is_ctx = pl.program_id(0) < T_CTX // IN_ROWS
    for s in range(IN_ROWS // TM):
        rows = slice(s * TM, (s + 1) * TM)
        x = refs[0][rows, :] if n_x == 1 else jnp.where(is_ctx, refs[0][rows, :], refs[1][rows, :])
        y = x * lax.rsqrt(jnp.mean(x * x, axis=-1, keepdims=True) + EPS) * g_ref[...]
        h = y * (1.0 + scale_ref[0]) + shift_ref[0]
        o_ref[rows, :] = jnp.dot(h.astype(BF16), w_ref[...], preferred_element_type=F32)


def _in_projection(x, g, mod3, w_in_bf):
    x_specs, x_args = _stream_specs(x, IN_ROWS)
    mod_row = functools.partial(_mod_row, rows=IN_ROWS)
    return pl.pallas_call(
        functools.partial(_inproj_kernel, n_x=len(x_args)),
        grid=(T // IN_ROWS,),
        in_specs=x_specs + [
            pl.BlockSpec((1, D), lambda i: (0, 0)),
            pl.BlockSpec((1, 1, D), lambda i: (mod_row(i), 0, 0)),
            pl.BlockSpec((1, 1, D), lambda i: (mod_row(i), 0, 1)),
            pl.BlockSpec((D, IN_WIDTH), lambda i: (0, 0)),
        ],
        out_specs=pl.BlockSpec((IN_ROWS, IN_WIDTH), lambda i: (i, 0)),
        out_shape=jax.ShapeDtypeStruct((T, IN_WIDTH), F32),
        compiler_params=_cparams(("arbitrary",)),
        name="in_projection",
    )(*x_args, g.reshape(1, D), mod3, mod3, w_in_bf)


HG_C = 128
HG_LEVELS = tuple(2 ** j for j in range(1, int(math.log2(HG_C)) + 1))


def _hgrn_tables():
    t = np.arange(HG_C)[:, None]
    s = np.arange(HG_C)[None, :]
    x = t ^ s
    lvl = np.zeros((HG_C, HG_C), np.int32)
    nz = x > 0
    lvl[nz] = np.floor(np.log2(x[nz])).astype(np.int32) + 1
    fwd = np.where(t >= s, lvl, -1).astype(np.int32)
    bwd = np.where(t <= s, lvl, -1).astype(np.int32)
    tri_f = (t >= s).astype(np.float32)
    tri_b = (t <= s).astype(np.float32)
    return np.stack([fwd, bwd]), np.stack([tri_f, tri_b])


def _block_ref(cum, m, idx):
    c, l = cum.shape
    if m >= 16:
        c3 = cum.reshape(c // m, m, l)
        r = c3[:, idx:idx + 1, :]
        return jnp.broadcast_to(r, (c // m, m, l)).reshape(c, l)
    c3 = cum.reshape(c // 8, 8, l)
    sub = lax.broadcasted_iota(jnp.int32, c3.shape, 1)
    out = None
    for j in range(8 // m - 1, -1, -1):
        cand = jnp.broadcast_to(c3[:, j * m + idx:j * m + idx + 1, :], c3.shape)
        out = cand if out is None else jnp.where(sub < (j + 1) * m, cand, out)
    return out.reshape(c, l)


def _hgrn_kernel(*refs, layer):
    n_in = 7
    fwd_in, bwd_in = refs[:n_in], refs[n_in:2 * n_in]
    o_f, fin_f, o_b, fin_b, st_f, st_b = refs[2 * n_in:]
    runs = [_hgrn_direction(*fwd_in, o_f, fin_f, st_f, layer=layer, rev=False),
            _hgrn_direction(*bwd_in, o_b, fin_b, st_b, layer=layer, rev=True)]
    for steps in zip(*runs):
        for step in steps:
            step()


def _hgrn_direction(q_ref, z_ref, v_ref, lb_ref, s0_ref, lvl_ref, tri_ref, o_ref, fin_ref, st_ref, *, layer, rev):
    g = pl.program_id(0)
    first = jnp.logical_or(g < CTX_TILES, (g - CTX_TILES) % SMP_TILES_PER_SEQ == 0)

    @pl.when(first)
    def _():
        st_ref[...] = s0_ref[0]

    qr = q_ref[...]
    q = qr * jax.nn.sigmoid(qr) * (A_DK ** -0.5)
    z = z_ref[...]
    if layer == 0:
        lf = jnp.minimum(z, 0.0) - jnp.log(1.0 + jnp.exp(-jnp.abs(z)))
        k = jax.nn.sigmoid(-z)
    else:
        lbd = lb_ref[...]
        lf = jnp.log(lbd + (1.0 - lbd) * jax.nn.sigmoid(z))
        k = (1.0 - lbd) * jax.nn.sigmoid(-z)
    v = v_ref[...]
    tri = tri_ref[0]
    lvl = lvl_ref[0]
    last_row = 0 if rev else HG_C - 1
    n_chunks = TM // HG_C
    order = range(n_chunks - 1, -1, -1) if rev else range(n_chunks)
    lf2 = lf * math.log2(math.e)
    cums = [_sel_dot(tri, lf2[c * HG_C:(c + 1) * HG_C]) for c in range(n_chunks)]
    lane = _lane((HG_C, LANES))
    head_masks = (lane < A_DK, lane >= A_DK)
    lane_row = _lane((1, LANES))
    head_keep = ((lane_row < A_DK).astype(BF16), (lane_row >= A_DK).astype(BF16))
    lvl2 = jnp.concatenate([lvl, lvl], axis=0)
    level_masks = [lvl2 == i for i in range(len(HG_LEVELS) + 1)]
    r = lax.broadcasted_iota(jnp.int32, (LANES, LANES), 0)
    cl = lax.broadcasted_iota(jnp.int32, (LANES, LANES), 1)
    same_head = (r < A_DK) == (cl < A_DK)

    def chunk(q_p, k_p, v_p, cum_p, st):
        v_bf = v_p.astype(BF16)
        k_bf = k_p.astype(BF16)
        q_bf = q_p.astype(BF16)

        def both_heads(x_bf):
            return jnp.concatenate([x_bf * head_keep[0], x_bf * head_keep[1]], axis=0)

        scores = jnp.where(level_masks[0], _dot_nt(both_heads(q_bf), k_bf), 0.0)
        for li, m in enumerate(HG_LEVELS):
            ref = _block_ref(cum_p, m, m // 2 if rev else m // 2 - 1)
            dec = jnp.exp2(-jnp.abs(cum_p - ref))
            qd = (q_p * dec).astype(BF16)
            kd = (k_p * dec).astype(BF16)
            scores = jnp.where(level_masks[li + 1], _dot_nt(both_heads(qd), kd), scores)
        pv = jnp.dot(scores.astype(BF16), v_bf, preferred_element_type=F32)
        o_intra = jnp.where(head_masks[0], pv[:HG_C], pv[HG_C:])
        q0 = (q_p * jnp.exp2(cum_p)).astype(BF16)
        out = o_intra + _dot_nt(q0, st.astype(BF16))
        last = cum_p[last_row:last_row + 1, :]
        ks = (k_p * jnp.exp2(last - cum_p)).astype(BF16)
        upd = _dot_tn(v_bf, ks)
        return out, st * jnp.exp2(last) + jnp.where(same_head, upd, 0.0)

    states = [st_ref[p] for p in range(2)]
    steps = []
    for p in range(2):
        sl = slice(p * LANES, (p + 1) * LANES)
        for n, c in enumerate(order):
            rows = slice(c * HG_C, (c + 1) * HG_C)

            def step(p=p, sl=sl, rows=rows, c=c, final=n == n_chunks - 1):
                o_ref[rows, sl], states[p] = chunk(q[rows, sl], k[rows, sl], v[rows, sl], cums[c][:, sl], states[p])
                if final:
                    st_ref[p] = states[p]
                    fin_ref[0, p] = states[p]

            steps.append(step)
    return steps


def _hgrn_seq(g):
    return jnp.where(g < CTX_TILES, g, CTX_TILES + (g - CTX_TILES) // SMP_TILES_PER_SEQ)


def _hgrn_blk(g, rev):
    if not rev:
        return g
    j = g - CTX_TILES
    return jnp.where(g < CTX_TILES, g,
                     CTX_TILES + (j // SMP_TILES_PER_SEQ) * SMP_TILES_PER_SEQ
                     + (SMP_TILES_PER_SEQ - 1 - j % SMP_TILES_PER_SEQ))


def _hgrn_scan(proj, lb_dirs, s0_dirs, lvl, tri, layer):
    def dir_in_specs(d):
        blk = functools.partial(_hgrn_blk, rev=d == 1)
        return [
            pl.BlockSpec((TM, A_WIDTH), lambda g: (blk(g), 0)),
            pl.BlockSpec((TM, A_WIDTH), lambda g: (blk(g), 1 + d)),
            pl.BlockSpec((TM, A_WIDTH), lambda g: (blk(g), 3)),
            pl.BlockSpec((1, A_WIDTH), lambda g: (0, 0)),
            pl.BlockSpec((1, 2, LANES, LANES), lambda g: (_hgrn_seq(g), 0, 0, 0)),
            pl.BlockSpec((1, HG_C, HG_C), lambda g: (d, 0, 0)),
            pl.BlockSpec((1, HG_C, HG_C), lambda g: (d, 0, 0)),
        ]

    def dir_out_specs(d):
        blk = functools.partial(_hgrn_blk, rev=d == 1)
        return [pl.BlockSpec((TM, A_WIDTH), lambda g: (blk(g), 0)),
                pl.BlockSpec((1, 2, LANES, LANES), lambda g: (_hgrn_seq(g), 0, 0, 0))]

    dir_out_shape = [jax.ShapeDtypeStruct((T, A_WIDTH), F32), jax.ShapeDtypeStruct((N_SEQ, 2, LANES, LANES), F32)]
    dir_args = lambda d: (proj, proj, proj, lb_dirs[d], s0_dirs[d], lvl, tri)
    return pl.pallas_call(
        functools.partial(_hgrn_kernel, layer=layer),
        grid=(N_TILES,),
        in_specs=dir_in_specs(0) + dir_in_specs(1),
        out_specs=dir_out_specs(0) + dir_out_specs(1),
        out_shape=dir_out_shape + dir_out_shape,
        scratch_shapes=[pltpu.VMEM((2, LANES, LANES), F32), pltpu.VMEM((2, LANES, LANES), F32)],
        compiler_params=_cparams(("arbitrary",)),
        name="hgrn_scan",
    )(*dir_args(0), *dir_args(1))


def _pack_state(s):
    n = s.shape[0]
    st = jnp.swapaxes(s, -1, -2).reshape(n, 2, 2, A_DK, A_DK)
    z = jnp.zeros_like(st[:, :, 0])
    top = jnp.concatenate([st[:, :, 0], z], axis=-1)
    bot = jnp.concatenate([z, st[:, :, 1]], axis=-1)
    return jnp.concatenate([top, bot], axis=-2)


def _unpack_state(sp):
    n = sp.shape[0]
    h0 = sp[:, :, :A_DK, :A_DK]
    h1 = sp[:, :, A_DK:, A_DK:]
    st = jnp.stack([h0, h1], axis=2).reshape(n, A_HEADS, A_DK, A_DK)
    return jnp.swapaxes(st, -1, -2)


def _half_rms(x, g):
    r = lax.broadcasted_iota(jnp.int32, (LANES, LANES), 0)
    c = lax.broadcasted_iota(jnp.int32, (LANES, LANES), 1)
    half_mean = jnp.where((r < B_DK) == (c < B_DK), 1.0 / B_DK, 0.0).astype(BF16)
    xx = x * x
    hi = xx.astype(BF16)
    lo = (xx - hi.astype(F32)).astype(BF16)
    ms = jnp.dot(lo, half_mean, preferred_element_type=F32) + jnp.dot(hi, half_mean, preferred_element_type=F32)
    return x * lax.rsqrt(ms + EPS) * g


def _rope(x, cos, sin_signed):
    lane = _lane(x.shape)
    first = (lane % 32) < 16
    rot = jnp.where(first, pltpu.roll(x, LANES - 16, 1), pltpu.roll(x, 16, 1))
    return x * cos + rot * sin_signed


def _with_ones(v_bf):
    return jnp.concatenate([v_bf, jnp.ones_like(v_bf)], axis=-1)


def _diff_softmax_pv(q_bf, keys_bf, vals_ext, lam):
    lane = _lane(q_bf.shape)
    zero = jnp.zeros_like(q_bf)
    outs = []
    for mp in range(2):
        qm = jnp.where((lane < B_DK) == (mp == 0), q_bf, zero)
        s = [_dot_nt(qm, kk) for kk in keys_bf]
        mx = functools.reduce(jnp.maximum, [jnp.max(si, axis=-1, keepdims=True) for si in s])
        acc = None
        for si, ve in zip(s, vals_ext):
            e = jnp.exp((si - mx).astype(BF16))
            pv = jnp.dot(e, ve, preferred_element_type=F32)
            acc = pv if acc is None else acc + pv
        outs.append(acc[:, :B_DV] / acc[:, B_DV:])
    return outs[0] - lam * outs[1]


def _subln(o, g, lam_init):
    return o * lax.rsqrt(jnp.mean(o * o, axis=-1, keepdims=True) + EPS) * g * (1.0 - lam_init)


def _attn_ctx_kernel(lam_ref, *refs, lam_init, layer):
    q_refs, k_refs, v_refs = refs[:B_HEADS], refs[B_HEADS:2 * B_HEADS], refs[2 * B_HEADS:3 * B_HEADS]
    gq_ref, gk_ref, gs_ref = refs[3 * B_HEADS:3 * B_HEADS + 3]
    rest = refs[3 * B_HEADS + 3:]
    if layer:
        pk_ref, pv_ref, o_ref, nk_ref, nv_ref = rest
        nk_ref[0, :layer] = pk_ref[0]
        nv_ref[0, :layer] = pv_ref[0]
    else:
        o_ref, nk_ref, nv_ref = rest
    lam = lam_ref[0, 0]
    for h in range(B_HEADS):
        qn = _half_rms(q_refs[h][...], gq_ref[...]) * (B_DK ** -0.5)
        kn = _half_rms(k_refs[h][...], gk_ref[...])
        v = v_refs[h][...]
        nk_ref[0, layer, 0, h] = kn[:, :B_DK]
        nk_ref[0, layer, 1, h] = kn[:, B_DK:]
        nv_ref[0, layer, h] = v
        o = _diff_softmax_pv(qn.astype(BF16), [kn.astype(BF16)], [_with_ones(v.astype(BF16))], lam)
        o_ref[:, h * LANES:(h + 1) * LANES] = _subln(o, gs_ref[...], lam_init)


def _attn_ctx(proj, lam, gq2, gk2, gs, lam_init, layer, prev_k, prev_v):
    qcol, kcol, vcol = 5 * A_WIDTH // LANES, 5 * A_WIDTH // LANES + 4, 5 * A_WIDTH // LANES + 8
    prev_specs, prev_args = [], []
    if layer:
        prev_specs = [pl.BlockSpec((1, layer, 2, B_HEADS, SEQ, B_DK), lambda b: (b, 0, 0, 0, 0, 0)),
                      pl.BlockSpec((1, layer, B_HEADS, SEQ, B_DV), lambda b: (b, 0, 0, 0, 0))]
        prev_args = [prev_k, prev_v]
    n_l = layer + 1
    head_specs = [pl.BlockSpec((SEQ, LANES), functools.partial(lambda b, col: (b, col), col=c0 + h))
                  for c0 in (qcol, kcol, vcol) for h in range(B_HEADS)]
    return pl.pallas_call(
        functools.partial(_attn_ctx_kernel, lam_init=lam_init, layer=layer),
        grid=(BATCH,),
        in_specs=[pl.BlockSpec(memory_space=pltpu.SMEM)] + head_specs + [
            pl.BlockSpec((1, LANES), lambda b: (0, 0)),
            pl.BlockSpec((1, LANES), lambda b: (0, 0)),
            pl.BlockSpec((1, LANES), lambda b: (0, 0)),
        ] + prev_specs,
        out_specs=[
            pl.BlockSpec((SEQ, B_WIDTH), lambda b: (b, 0)),
            pl.BlockSpec((1, n_l, 2, B_HEADS, SEQ, B_DK), lambda b: (b, 0, 0, 0, 0, 0)),
            pl.BlockSpec((1, n_l, B_HEADS, SEQ, B_DV), lambda b: (b, 0, 0, 0, 0)),
        ],
        out_shape=[
            jax.ShapeDtypeStruct((T_CTX, B_WIDTH), F32),
            jax.ShapeDtypeStruct((BATCH, n_l, 2, B_HEADS, SEQ, B_DK), F32),
            jax.ShapeDtypeStruct((BATCH, n_l, B_HEADS, SEQ, B_DV), F32),
        ],
        compiler_params=_cparams(("arbitrary",)),
        name="diff_attention_ctx",
    )(lam, *([proj] * (3 * B_HEADS)), gq2, gk2, gs, *prev_args)


ATT_TQ = 256
ATT_HEADS = 4


def _attn_smp_kernel(lam_ref, *refs, lam_init):
    nh = ATT_HEADS
    q_refs, k_refs, v_refs = refs[:nh], refs[nh:2 * nh], refs[2 * nh:3 * nh]
    ck_refs, cv_refs = refs[3 * nh:4 * nh], refs[4 * nh:5 * nh]
    cos_ref, sin_ref, gq_ref, gk_ref, gs_ref, o_ref, qs_ref, ks_ref = refs[5 * nh:]
    lam = lam_ref[0, 0]
    cos = cos_ref[...]
    sin = sin_ref[...]
    g = gs_ref[...]
    prepared = []
    for h in range(nh):
        qn = _rope(_half_rms(q_refs[h][...], gq_ref[...]), cos, sin) * (B_DK ** -0.5)
        qs_ref[h] = qn.astype(BF16)
        ks_ref[h] = _rope(_half_rms(k_refs[h][...], gk_ref[...]), cos, sin).astype(BF16)
        ck = jnp.concatenate([ck_refs[h][0, 0, 0, 0], ck_refs[h][0, 0, 1, 0]], axis=-1).astype(BF16)
        cv = _with_ones(cv_refs[h][0, 0, 0].astype(BF16))
        v_bf = _with_ones(v_refs[h][...].astype(BF16))
        prepared.append((ck, cv, v_bf))
    for h in range(nh):
        ck, cv, v_bf = prepared[h]
        k_bf = ks_ref[h]
        for i in range(DEC_SEQ // ATT_TQ):
            rows = slice(i * ATT_TQ, (i + 1) * ATT_TQ)
            o = _diff_softmax_pv(qs_ref[h, rows, :], [k_bf, ck], [v_bf, cv], lam)
            o_ref[rows, h * LANES:(h + 1) * LANES] = _subln(o, g, lam_init)


def _attn_smp(proj, lam, cache_k, cache_v, cos, sin, gq2, gk2, gs, layer, lam_init):
    qcol, kcol, vcol = 5 * A_WIDTH // LANES, 5 * A_WIDTH // LANES + 4, 5 * A_WIDTH // LANES + 8
    r0 = T_CTX // DEC_SEQ
    nh = ATT_HEADS

    def per_head(shape, index):
        return [pl.BlockSpec(shape, functools.partial(index, dh=dh)) for dh in range(nh)]

    head_specs = (
        per_head((DEC_SEQ, LANES), lambda b, hp, dh: (r0 + b, qcol + hp * nh + dh))
        + per_head((DEC_SEQ, LANES), lambda b, hp, dh: (r0 + b, kcol + hp * nh + dh))
        + per_head((DEC_SEQ, LANES), lambda b, hp, dh: (r0 + b, vcol + hp * nh + dh))
        + per_head((1, 1, 2, 1, PAST, B_DK), lambda b, hp, dh: (b, layer, 0, hp * nh + dh, 0, 0))
        + per_head((1, 1, 1, PAST, B_DV), lambda b, hp, dh: (b, layer, hp * nh + dh, 0, 0)))
    return pl.pallas_call(
        functools.partial(_attn_smp_kernel, lam_init=lam_init),
        grid=(DEC_BATCH, B_HEADS // nh),
        in_specs=[pl.BlockSpec(memory_space=pltpu.SMEM)] + head_specs + [
            pl.BlockSpec((DEC_SEQ, LANES), lambda b, hp: (0, 0)),
            pl.BlockSpec((DEC_SEQ, LANES), lambda b, hp: (0, 0)),
            pl.BlockSpec((1, LANES), lambda b, hp: (0, 0)),
            pl.BlockSpec((1, LANES), lambda b, hp: (0, 0)),
            pl.BlockSpec((1, LANES), lambda b, hp: (0, 0)),
        ],
        out_specs=pl.BlockSpec((DEC_SEQ, nh * LANES), lambda b, hp: (b, hp)),
        out_shape=jax.ShapeDtypeStruct((T_SMP, B_WIDTH), F32),
        scratch_shapes=[pltpu.VMEM((nh, DEC_SEQ, LANES), BF16), pltpu.VMEM((nh, DEC_SEQ, LANES), BF16)],
        compiler_params=_cparams(("arbitrary", "arbitrary")),
        name="diff_attention_smp",
    )(lam, *([proj] * (3 * nh)), *([cache_k] * nh), *([cache_v] * nh), cos, sin, gq2, gk2, gs)


def _rope_tables():
    n_rows = DEC_SEQ // GRID_W
    row = np.repeat(np.arange(n_rows), GRID_W).astype(np.float32)
    col = np.tile(np.arange(GRID_W), n_rows).astype(np.float32)
    half = B_DK // 2
    inv_freq = (ROPE_BASE ** (-jnp.arange(0, half, 2, dtype=F32) / half))
    row_ang = jnp.asarray(row)[:, None] * inv_freq
    col_ang = jnp.asarray(col)[:, None] * inv_freq
    ang = jnp.concatenate([row_ang, row_ang, col_ang, col_ang], axis=-1)
    ang = jnp.concatenate([ang, ang], axis=-1)
    sign = np.where((np.arange(LANES) % 32) < 16, -1.0, 1.0).astype(np.float32)
    return jnp.cos(ang), jnp.sin(ang) * sign


CM_ROWS = 1024


def _gelu(x):
    return 0.5 * x * (1.0 + lax.erf(x * (2.0 ** -0.5)))


def _cmlp_kernel(u_ref, v_ref, g_ref, b_ref, ws_ref, bs_ref, o_ref):
    u = _gelu(u_ref[...])
    gv = _gelu(v_ref[...])
    mu = jnp.mean(gv, axis=-1, keepdims=True)
    dv = gv - mu
    var = jnp.mean(dv * dv, axis=-1, keepdims=True)
    vn = (dv * lax.rsqrt(var + EPS) * g_ref[...] + b_ref[...]).astype(BF16)
    lane = _lane((C_CHUNK, LANES))
    for c in range(CM_ROWS // C_CHUNK):
        rs = slice(c * C_CHUNK, (c + 1) * C_CHUNK)
        for p in range(2):
            cs = slice(p * LANES, (p + 1) * LANES)
            vp = vn[rs, cs]
            m0 = jnp.dot(ws_ref[2 * p].astype(BF16), vp, preferred_element_type=F32)
            m1 = jnp.dot(ws_ref[2 * p + 1].astype(BF16), vp, preferred_element_type=F32)
            mixed = jnp.where(lane < C_DG, m0, m1) + bs_ref[:, cs]
            o_ref[rs, cs] = u[rs, cs] * mixed


def _chunk_mlp(proj, ln_g, ln_b, w_s, bias_full):
    ucol = (5 * A_WIDTH + 3 * B_WIDTH) // C_WIDTH
    return pl.pallas_call(
        _cmlp_kernel,
        grid=(T // CM_ROWS,),
        in_specs=[
            pl.BlockSpec((CM_ROWS, C_WIDTH), lambda i: (i, ucol)),
            pl.BlockSpec((CM_ROWS, C_WIDTH), lambda i: (i, ucol + 1)),
            pl.BlockSpec((1, C_WIDTH), lambda i: (0, 0)),
            pl.BlockSpec((1, C_WIDTH), lambda i: (0, 0)),
            pl.BlockSpec((C_GROUPS, C_CHUNK, C_CHUNK), lambda i: (0, 0, 0)),
            pl.BlockSpec((C_CHUNK, C_WIDTH), lambda i: (0, 0)),
        ],
        out_specs=pl.BlockSpec((CM_ROWS, C_WIDTH), lambda i: (i, 0)),
        out_shape=jax.ShapeDtypeStruct((T, C_WIDTH), F32),
        compiler_params=_cparams(("arbitrary",)),
        name="chunk_mlp",
    )(proj, proj, ln_g.reshape(1, C_WIDTH), ln_b.reshape(1, C_WIDTH), w_s, bias_full)


PM_ROWS = 512
PM_SUB = TM


def _postmix_kernel(*refs, n_x):
    (of_ref, ob_ref, ag_ref, hg_ref, hsel_ref, bc_ref, bs_ref, c_ref, w_ref, gate1_ref, shift2_ref, scale2_ref,
     g2_ref, wrh_ref, wrl_ref, br_ref, x1_ref, h2_ref, idx_ref, gw_ref) = refs[n_x:]
    is_ctx = pl.program_id(0) < T_CTX // PM_ROWS
    x_refs = refs[:n_x]
    for s in range(PM_ROWS // PM_SUB):
        rows = slice(s * PM_SUB, (s + 1) * PM_SUB)
        x = x_refs[0][rows, :] if n_x == 1 else jnp.where(is_ctx, x_refs[0][rows, :], x_refs[1][rows, :])
        o = of_ref[rows, :] + ob_ref[rows, :]
        ms = _dot_sel(o * o, hsel_ref[...]) * (1.0 / A_DK)
        ag = ag_ref[rows, :]
        a = o * lax.rsqrt(ms + EPS) * hg_ref[...] * (ag * jax.nn.sigmoid(ag))
        b = jnp.where(is_ctx, bc_ref[rows, :], bs_ref[rows, :])
        mixed = jnp.dot(a.astype(BF16), w_ref[0:A_WIDTH, :], preferred_element_type=F32)
        mixed = mixed + jnp.dot(b.astype(BF16), w_ref[A_WIDTH:A_WIDTH + B_WIDTH, :], preferred_element_type=F32)
        mixed = mixed + jnp.dot(c_ref[rows, :].astype(BF16), w_ref[A_WIDTH + B_WIDTH:, :],
                                preferred_element_type=F32)
        x1 = x + gate1_ref[0] * mixed
        x1_ref[rows, :] = x1
        y = x1 * lax.rsqrt(jnp.mean(x1 * x1, axis=-1, keepdims=True) + EPS) * g2_ref[...]
        h2 = y * (1.0 + scale2_ref[0]) + shift2_ref[0]
        h2_ref[rows] = _pack_rows(h2)
        hi = h2.astype(BF16)
        lo = (h2 - hi.astype(F32)).astype(BF16)
        lg = jnp.dot(lo, wrh_ref[...], preferred_element_type=F32)
        lg = lg + jnp.dot(hi, wrl_ref[...], preferred_element_type=F32)
        lg = lg + jnp.dot(hi, wrh_ref[...], preferred_element_type=F32) + br_ref[...]
        lt = lg.T[:N_EXPERTS]
        row = lax.broadcasted_iota(jnp.int32, lt.shape, 0)
        out_row = lax.broadcasted_iota(jnp.int32, (8, PM_SUB), 0)
        idx_out = jnp.zeros((8, PM_SUB), jnp.int32)
        val_out = jnp.zeros((8, PM_SUB), F32)
        top0 = None
        den = None
        for kk in range(TOP_K):
            mx = jnp.max(lt, axis=0, keepdims=True)
            am = jnp.min(jnp.where(lt == mx, row, N_EXPERTS), axis=0, keepdims=True)
            if kk == 0:
                top0 = mx
            e = jnp.exp(mx - top0)
            den = e if den is None else den + e
            idx_out = jnp.where(out_row == kk, am, idx_out)
            val_out = jnp.where(out_row == kk, e, val_out)
            lt = jnp.where(row == am, -jnp.inf, lt)
        idx_ref[:, rows] = idx_out
        gw_ref[:, rows] = val_out / den


def _post_mix(o_f, o_b, proj, hg, hsel, b_ctx, b_smp, c_out, w_out_bf, x, mod3, g2, wr_hi, wr_lo, br_pad):
    tile = lambda w: pl.BlockSpec((PM_ROWS, w), lambda i: (i, 0))
    const = lambda shape: pl.BlockSpec(shape, lambda i: tuple(0 for _ in shape))
    modspec = lambda j: pl.BlockSpec((1, 1, D), lambda i: (_mod_row(i, PM_ROWS), 0, j))
    rowsT = pl.BlockSpec((8, PM_ROWS), lambda i: (0, i))
    x_specs, x_args = _stream_specs(x, PM_ROWS)
    ctx_tiles = T_CTX // PM_ROWS
    return pl.pallas_call(
        functools.partial(_postmix_kernel, n_x=len(x_args)),
        grid=(T // PM_ROWS,),
        in_specs=x_specs + [
            tile(A_WIDTH), tile(A_WIDTH),
            pl.BlockSpec((PM_ROWS, A_WIDTH), lambda i: (i, 4)),
            const((1, A_WIDTH)), const((A_WIDTH, A_WIDTH)),
            pl.BlockSpec((PM_ROWS, B_WIDTH), lambda i: (jnp.minimum(i, ctx_tiles - 1), 0)),
            pl.BlockSpec((PM_ROWS, B_WIDTH), lambda i: (jnp.maximum(i - ctx_tiles, 0), 0)),
            tile(C_WIDTH),
            const((D, D)),
            modspec(2), modspec(3), modspec(4),
            const((1, D)), const((D, LANES)), const((D, LANES)), const((1, LANES)),
        ],
        out_specs=[tile(D), pl.BlockSpec((PM_ROWS, ROW_TILES, LANES), lambda i: (i, 0, 0)), rowsT, rowsT],
        out_shape=[
            jax.ShapeDtypeStruct((T, D), F32),
            jax.ShapeDtypeStruct((T, ROW_TILES, LANES), ROW_DT),
            jax.ShapeDtypeStruct((8, T), jnp.int32),
            jax.ShapeDtypeStruct((8, T), F32),
        ],
        compiler_params=_cparams(("arbitrary",)),
        name="post_mix_router",
    )(*x_args, o_f, o_b, proj, hg, hsel, b_ctx, b_smp, c_out, w_out_bf, mod3, mod3, mod3, g2.reshape(1, D),
      wr_hi, wr_lo, br_pad)


def _route_kernel(idx_ref, dest_ref, meta_ref):
    erow = lax.broadcasted_iota(jnp.int32, (N_EXPERTS, TM), 0)
    s_i = lax.broadcasted_iota(jnp.int32, (TM, TM), 0)
    t_i = lax.broadcasted_iota(jnp.int32, (TM, TM), 1)
    earlier = (s_i < t_i).astype(BF16)
    out_row = lax.broadcasted_iota(jnp.int32, (8, TM), 0)

    def onehots(i):
        idx = idx_ref[:, pl.ds(pl.multiple_of(i * TM, TM), TM)]
        return [(erow == idx[kk:kk + 1, :]) for kk in range(TOP_K)]

    def count_tile(i, run):
        ohs = onehots(i)
        base = run
        pos = jnp.zeros((8, TM), F32)
        for kk in range(TOP_K):
            ohf = ohs[kk].astype(F32)
            before = jnp.dot(ohs[kk].astype(BF16), earlier, preferred_element_type=F32)
            p = jnp.sum(ohf * (base + before), axis=0, keepdims=True)
            pos = jnp.where(out_row == kk, p, pos)
            base = base + jnp.sum(ohf, axis=1, keepdims=True)
        dest_ref[:, pl.ds(pl.multiple_of(i * TM, TM), TM)] = pos.astype(jnp.int32)
        return base

    counts = lax.fori_loop(0, N_TILES, count_tile, jnp.zeros((N_EXPERTS, 1), F32)).astype(jnp.int32)
    bm_shift = MOE_BM.bit_length() - 1
    padded = lax.shift_left(lax.shift_right_logical(counts + (MOE_BM - 1), bm_shift), bm_shift)
    e_r = lax.broadcasted_iota(jnp.int32, (N_EXPERTS, N_EXPERTS), 0)
    e_c = lax.broadcasted_iota(jnp.int32, (N_EXPERTS, N_EXPERTS), 1)
    incl = (e_c <= e_r).astype(BF16)
    pad_end = _sel_dot(incl, jnp.broadcast_to(padded.astype(F32), (N_EXPERTS, LANES)))[:, :1]
    pad_start = pad_end - padded.astype(F32)

    def place_tile(i, carry):
        ohs = onehots(i)
        sl = pl.ds(pl.multiple_of(i * TM, TM), TM)
        off = jnp.zeros((8, TM), F32)
        for kk in range(TOP_K):
            o = jnp.sum(ohs[kk].astype(F32) * pad_start, axis=0, keepdims=True)
            off = jnp.where(out_row == kk, o, off)
        dest_ref[:, sl] = dest_ref[:, sl] + off.astype(jnp.int32)
        return carry

    lax.fori_loop(0, N_TILES, place_tile, 0)

    total = jnp.max(pad_end, axis=0, keepdims=True)
    lane_i = lax.broadcasted_iota(jnp.int32, (1, TM), 1)
    blk0 = (lane_i * MOE_BM).astype(F32)
    block_e = jnp.sum((pad_end <= blk0).astype(F32), axis=0, keepdims=True)
    live_end = pad_start + counts.astype(F32)
    sel = erow.astype(F32) == block_e
    live = jnp.sum(jnp.where(sel, live_end, 0.0), axis=0, keepdims=True)
    valid = jnp.where(blk0 < total, jnp.clip(live - blk0, 0.0, float(MOE_BM)), 0.0)
    own = erow == lane_i
    n_blk = jnp.sum(jnp.where(own, padded.astype(F32), 0.0), axis=0, keepdims=True) * (1.0 / MOE_BM)
    first_blk = jnp.sum(jnp.where(own, pad_start, 0.0), axis=0, keepdims=True) * (1.0 / MOE_BM)
    meta = jnp.where(out_row == 0, valid, 0.0)
    meta = jnp.where(out_row == 1, n_blk, meta)
    meta = jnp.where(out_row == 2, first_blk, meta)
    meta = jnp.where(out_row == 3, total * (1.0 / MOE_BM), meta)
    meta_ref[...] = meta.astype(jnp.int32)


def _route(idx_t):
    assert MOE_BLOCKS <= TM
    return pl.pallas_call(
        _route_kernel,
        out_shape=[jax.ShapeDtypeStruct((8, T), jnp.int32), jax.ShapeDtypeStruct((8, TM), jnp.int32)],
        compiler_params=pltpu.CompilerParams(vmem_limit_bytes=VMEM_LIMIT),
        name="moe_route",
    )(idx_t)


def _moe_kernel(bv_ref, nb_ref, g0_ref, tot_ref, x_hbm, wgu_ref, bgu_ref, wdn_ref, bdn_ref, y_hbm,
                wgu_bf, wdn_bf, xbuf, ybuf, xb_ref, xsem, ysem):
    e = pl.program_id(0)
    n_blk = nb_ref[e]
    first = g0_ref[e]
    total = tot_ref[0]
    ahead = MOE_RING - 1

    def x_copy(g):
        slot = g % MOE_RING
        return pltpu.make_async_copy(x_hbm.at[pl.ds(g * MOE_BM, MOE_BM)], xbuf.at[slot], xsem.at[slot])

    def y_copy(g):
        slot = g % MOE_RING
        return pltpu.make_async_copy(ybuf.at[slot], y_hbm.at[pl.ds(g * MOE_BM, MOE_BM)], ysem.at[slot])

    @pl.when(e == 0)
    def _():
        for g in range(ahead):
            @pl.when(g < total)
            def _():
                x_copy(g).start()

    @pl.when(n_blk > 0)
    def _():
        wgu_bf[...] = wgu_ref[0, 0].astype(BF16)
        wdn_bf[...] = wdn_ref[0, 0].astype(BF16)

    def block(j, carry):
        g = first + j
        slot = g % MOE_RING
        x_copy(g).wait()

        @pl.when(g + ahead < total)
        def _():
            x_copy(g + ahead).start()

        @pl.when(g >= MOE_RING)
        def _():
            y_copy(g - MOE_RING).wait()

        n_live = bv_ref[g]

        def ffn(n_rows):
            live = lax.broadcasted_iota(jnp.int32, (n_rows, LANES), 0) < n_live
            for c, chunk in enumerate(_unpack_rows(xbuf[slot, :n_rows])):
                xb_ref[:n_rows, c * LANES:(c + 1) * LANES] = jnp.where(live, chunk, 0.0).astype(BF16)
            gu = jnp.dot(xb_ref[:n_rows, :], wgu_bf[...], preferred_element_type=F32) + bgu_ref[0, 0]
            glu = jnp.minimum(gu[:, :D], SWIGLU_LIMIT)
            lin = jnp.clip(gu[:, D:], -SWIGLU_LIMIT, SWIGLU_LIMIT)
            act = glu * jax.nn.sigmoid(SWIGLU_ALPHA * glu) * (lin + 1.0)
            y = jnp.dot(act.astype(BF16), wdn_bf[...], preferred_element_type=F32) + bdn_ref[0, 0]
            ybuf[slot, :n_rows] = _pack_rows(y)

        quarter = MOE_BM // 4
        for n_q in range(1, 5):
            height = n_q * quarter

            above = n_live > height - quarter if n_q > 1 else True
            below = n_live <= height if n_q < 4 else True

            @pl.when(jnp.logical_and(above, below))
            def _(height=height):
                ffn(height)
                if height < MOE_BM:
                    ybuf[slot, height:] = jnp.zeros((MOE_BM - height, ROW_TILES, LANES), ROW_DT)

        y_copy(g).start()
        return carry

    lax.fori_loop(0, n_blk, block, 0)

    @pl.when(e == N_EXPERTS - 1)
    def _():
        for back in range(MOE_RING, 0, -1):
            @pl.when(total >= back)
            def _():
                y_copy(total - back).wait()

        def fill(g, carry):
            ybuf[g % MOE_RING] = jnp.zeros((MOE_BM, ROW_TILES, LANES), ROW_DT)
            y_copy(g).start()
            y_copy(g).wait()
            return carry

        lax.fori_loop(total, MOE_BLOCKS, fill, 0)


def _moe_ffn(block_valid, n_blk, first_blk, total_blk, xs, w_gu, b_gu, w_dn, b_dn, layer):
    rows = (MOE_BM, ROW_TILES, LANES)
    return pl.pallas_call(
        _moe_kernel,
        grid_spec=pltpu.PrefetchScalarGridSpec(
            num_scalar_prefetch=4,
            grid=(N_EXPERTS,),
            in_specs=[
                pl.BlockSpec(memory_space=pl.ANY),
                pl.BlockSpec((1, 1, D, 2 * D), lambda e, *_: (layer, e, 0, 0)),
                pl.BlockSpec((1, 1, 1, 2 * D), lambda e, *_: (layer, e, 0, 0)),
                pl.BlockSpec((1, 1, D, D), lambda e, *_: (layer, e, 0, 0)),
                pl.BlockSpec((1, 1, 1, D), lambda e, *_: (layer, e, 0, 0)),
            ],
            out_specs=pl.BlockSpec(memory_space=pl.ANY),
            scratch_shapes=[
                pltpu.VMEM((D, 2 * D), BF16), pltpu.VMEM((D, D), BF16),
                pltpu.VMEM((MOE_RING,) + rows, ROW_DT), pltpu.VMEM((MOE_RING,) + rows, ROW_DT),
                pltpu.VMEM((MOE_BM, D), BF16),
                pltpu.SemaphoreType.DMA((MOE_RING,)), pltpu.SemaphoreType.DMA((MOE_RING,)),
            ],
        ),
        out_shape=jax.ShapeDtypeStruct((MOE_ROWS, ROW_TILES, LANES), ROW_DT),
        compiler_params=_cparams(("arbitrary",)),
        name="moe_expert_ffn",
    )(block_valid, n_blk, first_blk, total_blk, xs, w_gu, b_gu.reshape(DEPTH, N_EXPERTS, 1, 2 * D), w_dn,
      b_dn.reshape(DEPTH, N_EXPERTS, 1, D))


def _combine_kernel(x1_ref, y_ref, gw_ref, gate2_ref, o_ref):
    gw = jnp.concatenate([gw_ref[...], jnp.zeros((LANES - 8, TM), F32)], axis=0).T
    ys = [_unpack_rows(y_ref[kk]) for kk in range(TOP_K)]
    for c in range(D // LANES):
        cs = slice(c * LANES, (c + 1) * LANES)
        acc = None
        for kk in range(TOP_K):
            term = ys[kk][c] * gw[:, kk:kk + 1]
            acc = term if acc is None else acc + term
        o_ref[:, cs] = x1_ref[:, cs] + gate2_ref[0, :, cs] * acc


def _combine(x1, yg, gw, mod3, tile0, n_tiles):
    return pl.pallas_call(
        _combine_kernel,
        grid=(n_tiles,),
        in_specs=[
            pl.BlockSpec((TM, D), lambda i: (tile0 + i, 0)),
            pl.BlockSpec((TOP_K, TM, ROW_TILES, LANES), lambda i: (0, i, 0, 0)),
            pl.BlockSpec((8, TM), lambda i: (0, tile0 + i)),
            pl.BlockSpec((1, 1, D), lambda i: (_mod_row(tile0 + i), 0, 5)),
        ],
        out_specs=pl.BlockSpec((TM, D), lambda i: (i, 0)),
        out_shape=jax.ShapeDtypeStruct((n_tiles * TM, D), F32),
        compiler_params=_cparams(("arbitrary",)),
        name="moe_combine",
    )(x1, yg, gw, mod3)


def _sc_mesh():
    return plsc.VectorSubcoreMesh(core_axis_name="c", subcore_axis_name="s")


def _sc_worker():
    return lax.axis_index("s") * SC_CORES + lax.axis_index("c")


def _sc_dispatch(h2t, dest_km):
    per_w = T // SC_WORKERS

    @functools.partial(
        pl.kernel, mesh=_sc_mesh(),
        out_type=jax.ShapeDtypeStruct((MOE_ROWS, ROW_TILES, LANES), ROW_DT),
        scratch_types=[pltpu.VMEM((SC_WIN,), jnp.int32) for _ in range(TOP_K)]
        + [pltpu.VMEM((SC_WIN, ROW_TILES, LANES), ROW_DT), pltpu.SemaphoreType.DMA],
    )
    def run(h_hbm, d_hbm, o_hbm, *scratch):
        idx_vs, rows_v, sem = scratch[:TOP_K], scratch[TOP_K], scratch[TOP_K + 1]
        w0 = _sc_worker() * per_w

        @pl.loop(0, per_w // SC_WIN)
        def _(w):
            base = pl.multiple_of(w0 + w * SC_WIN, SC_WIN)
            pltpu.sync_copy(h_hbm.at[pl.ds(base, SC_WIN)], rows_v)
            for kk in range(TOP_K):
                pltpu.sync_copy(d_hbm.at[pl.ds(kk * T + base, SC_WIN)], idx_vs[kk])
            copies = [pltpu.async_copy(rows_v, o_hbm.at[idx_vs[kk]], sem) for kk in range(TOP_K)]
            for cp in copies:
                cp.wait()

    return run(h2t, dest_km)


def _sc_gather(yb, dest):
    n = dest.shape[0]
    per_w = n // SC_WORKERS

    @functools.partial(
        pl.kernel, mesh=_sc_mesh(),
        out_type=jax.ShapeDtypeStruct((n, ROW_TILES, LANES), ROW_DT),
        scratch_types=[pltpu.VMEM((SC_WIN,), jnp.int32), pltpu.VMEM((SC_WIN, ROW_TILES, LANES), ROW_DT),
                       pltpu.SemaphoreType.DMA],
    )
    def run(y_hbm, d_hbm, o_hbm, idx_v, rows_v, sem):
        w0 = _sc_worker() * per_w

        @pl.loop(0, per_w // SC_WIN)
        def _(w):
            base = pl.multiple_of(w0 + w * SC_WIN, SC_WIN)
            pltpu.sync_copy(d_hbm.at[pl.ds(base, SC_WIN)], idx_v)
            pltpu.async_copy(y_hbm.at[idx_v], rows_v, sem).wait()
            pltpu.sync_copy(rows_v, o_hbm.at[pl.ds(base, SC_WIN)])

    return run(yb, dest)


def kernel(x_prompt, x_sample, c, cache_diff_k, cache_diff_v, state_hgrn, c_ctx, norm_mix_g, norm_ffn_g, w_mod, b_mod, w_in, w_out, hgrn_lower_bounds, hgrn_norm_g, diff_q_norm_g, diff_k_norm_g, diff_lambda_q1, diff_lambda_k1, diff_lambda_q2, diff_lambda_k2, diff_subln_g, cmlp_ln_g, cmlp_ln_b, cmlp_w_s, cmlp_b_s, router_w, router_b, moe_w_gate_up, moe_b_gate_up, moe_w_down, moe_b_down):
    x = (x_prompt.reshape(T_CTX, D), x_sample.reshape(T_SMP, D))
    cvec = jnp.concatenate([c_ctx[None, :], c, jnp.zeros((MOD_ROWS - 1 - DEC_BATCH, D), F32)], axis=0)
    mod = _modulation(cvec, w_mod, b_mod)

    lvl_np, tri_np = _hgrn_tables()
    lvl = jnp.asarray(lvl_np)
    tri = jnp.asarray(tri_np, dtype=BF16)
    cos, sin = _rope_tables()
    hsel = jnp.asarray(np.kron(np.eye(A_HEADS), np.ones((A_DK, A_DK))), dtype=BF16)
    sm = jax.nn.softmax(hgrn_lower_bounds.astype(F32), axis=0)
    lb_all = jnp.cumsum(sm, axis=0) - sm[0]

    new_k, new_v, new_s = None, None, []
    for l in range(DEPTH):
        mod3 = mod[l].reshape(MOD_ROWS, 1, 6 * D)
        proj = _in_projection(x, norm_mix_g[l], mod3, w_in[l].astype(BF16))

        s0 = jnp.concatenate([jnp.zeros((BATCH, 2, A_HEADS, A_DK, A_DK), F32), state_hgrn[:, l]], axis=0)
        o_f, fin_f, o_b, fin_b = _hgrn_scan(proj, [lb_all[l, d].reshape(1, A_WIDTH) for d in range(2)],
                                            [_pack_state(s0[:, d]) for d in range(2)], lvl, tri, l)
        o_dir = [o_f, o_b]
        new_s.append(jnp.stack([_unpack_state(fin_f[:BATCH]), _unpack_state(fin_b[:BATCH])], axis=1))

        lam_init = 0.8 - 0.6 * math.exp(-0.3 * l)
        lam = (jnp.exp(jnp.sum(diff_lambda_q1[l] * diff_lambda_k1[l]))
               - jnp.exp(jnp.sum(diff_lambda_q2[l] * diff_lambda_k2[l])) + lam_init).reshape(1, 1)
        gq2 = jnp.tile(diff_q_norm_g[l], 2).reshape(1, LANES)
        gk2 = jnp.tile(diff_k_norm_g[l], 2).reshape(1, LANES)
        gs = diff_subln_g[l].reshape(1, LANES)
        b_ctx, new_k, new_v = _attn_ctx(proj, lam, gq2, gk2, gs, lam_init, l, new_k, new_v)
        b_smp = _attn_smp(proj, lam, cache_diff_k, cache_diff_v, cos, sin, gq2, gk2, gs, l, lam_init)

        bias_full = jnp.repeat(cmlp_b_s[l].T, C_DG, axis=1)
        c_out = _chunk_mlp(proj, cmlp_ln_g[l], cmlp_ln_b[l], cmlp_w_s[l], bias_full)

        hg = jnp.tile(hgrn_norm_g[l], A_HEADS).reshape(1, A_WIDTH)
        wr_pad = jnp.pad(router_w[l], ((0, 0), (0, LANES - N_EXPERTS)))
        wr_hi = wr_pad.astype(BF16)
        wr_lo = (wr_pad - wr_hi.astype(F32)).astype(BF16)
        br_pad = jnp.pad(router_b[l], (0, LANES - N_EXPERTS)).reshape(1, LANES)
        x1, h2, idx_t, gw_t = _post_mix(o_dir[0], o_dir[1], proj, hg, hsel, b_ctx, b_smp, c_out,
                                        w_out[l].astype(BF16), x, mod3, norm_ffn_g[l], wr_hi, wr_lo, br_pad)

        dest_t, meta = _route(idx_t)
        dest_km = dest_t[:TOP_K].reshape(-1)
        xs = _sc_dispatch(h2, dest_km)
        yb = _moe_ffn(meta[0, :MOE_BLOCKS], meta[1, :N_EXPERTS], meta[2, :N_EXPERTS], meta[3, :1], xs,
                      moe_w_gate_up, moe_b_gate_up, moe_w_down, moe_b_down, l)
        yg_ctx = _sc_gather(yb, dest_t[:TOP_K, :T_CTX].reshape(-1)).reshape(TOP_K, T_CTX, ROW_TILES, LANES)
        yg_smp = _sc_gather(yb, dest_t[:TOP_K, T_CTX:].reshape(-1)).reshape(TOP_K, T_SMP, ROW_TILES, LANES)
        x = (_combine(x1, yg_ctx, gw_t, mod3, 0, CTX_TILES),
             _combine(x1, yg_smp, gw_t, mod3, CTX_TILES, N_TILES - CTX_TILES))

    y_prompt = x[0].reshape(BATCH, SEQ, D)
    y_sample = x[1].reshape(DEC_BATCH, DEC_SEQ, D)
    return (y_prompt, y_sample, new_k, new_v, jnp.stack(new_s, axis=1))
```
